```python
import math
import jax, jax.numpy as jnp
from jax import lax
import numpy as np

D_MODEL = 1024
BATCH = 8
SEQ = 4096
DEPTH = 4

SSD_D_INNER = 1024
SSD_HEAD_DIM = 64
SSD_HEADS = SSD_D_INNER // SSD_HEAD_DIM
SSD_GROUPS = 2
SSD_STATE = 128
SSD_CONV = 4
SSD_CHUNK = 128
SSD_CONV_DIM = SSD_D_INNER + 2 * SSD_GROUPS * SSD_STATE

ATTN_HEAD_DIM = 128
ATTN_HEADS_PER_GROUP = 4
DILATED_GROUPS = ((128, 1), (512, 4), (2048, 16))
ATTN_HEADS = ATTN_HEADS_PER_GROUP * len(DILATED_GROUPS)
ATTN_WIDTH = ATTN_HEADS * ATTN_HEAD_DIM
ATTN_OUT_WIDTH = ATTN_HEADS_PER_GROUP * ATTN_HEAD_DIM

POOL_WINDOWS = (2, 4, 8, 16)
POOL_WIDTH = 1024
POOL_GROUP_WIDTH = POOL_WIDTH // len(POOL_WINDOWS)

N_BRANCHES = 3
D_FF = 4 * D_MODEL
EPS = 1e-6
IN_SIZES = (SSD_D_INNER, SSD_CONV_DIM, SSD_HEADS, ATTN_WIDTH, ATTN_WIDTH, ATTN_WIDTH, POOL_WIDTH, N_BRANCHES * D_MODEL)
IN_WIDTH = SSD_D_INNER + SSD_CONV_DIM + SSD_HEADS + 3 * ATTN_WIDTH + POOL_WIDTH + N_BRANCHES * D_MODEL

kernel_name = "hybrid_ssd_dilattn_pool_gated"


def _rms(x, w):
    xf = x.astype(jnp.float32)
    y = xf * lax.rsqrt(jnp.mean(xf * xf, axis=-1, keepdims=True) + EPS)
    return (y * w.astype(jnp.float32)).astype(x.dtype)


def _alibi_slopes(n):
    def pow2(k):
        start = 2.0 ** (-8.0 / k)
        return [start ** (i + 1) for i in range(k)]
    if math.log2(n).is_integer():
        s = pow2(n)
    else:
        c = 2 ** math.floor(math.log2(n))
        s = pow2(c) + pow2(2 * c)[0::2][: n - c]
    return np.sort(np.asarray(s, np.float32))[::-1].copy()


def _causal_dwconv(x, w, bias):
    ch = x.shape[-1]
    y = lax.conv_general_dilated(x, w[:, None, :].astype(x.dtype), window_strides=(1,),
                                 padding=[(SSD_CONV - 1, 0)],
                                 dimension_numbers=('NWC', 'WIO', 'NWC'),
                                 feature_group_count=ch)
    return y + bias


def _ssd(xs, dt, A, B, C, D):
    b, S, H, P = xs.shape
    G, N, L = SSD_GROUPS, SSD_STATE, SSD_CHUNK
    J = H // G
    nc = S // L
    xc = xs.reshape(b, nc, L, G, J, P)
    X = xc * dt.reshape(b, nc, L, G, J)[..., None]
    a_cum = jnp.cumsum((dt * A).reshape(b, nc, L, G, J), axis=2)
    Bc = B.reshape(b, nc, L, G, N)
    Cc = C.reshape(b, nc, L, G, N)
    seg = a_cum[:, :, :, None] - a_cum[:, :, None, :]
    causal = jnp.tril(jnp.ones((L, L), bool))[:, :, None, None]
    decay = jnp.exp(jnp.where(causal, seg, -jnp.inf))
    cb = jnp.einsum('bclgn,bcsgn->bclsg', Cc, Bc)
    y_diag = jnp.einsum('bclsgj,bcsgjp->bclgjp', cb[..., None] * decay, X)
    Xd = X * jnp.exp(a_cum[:, :, -1:] - a_cum)[..., None]
    states = jnp.einsum('bclgn,bclgjp->bcgjpn', Bc, Xd)
    chunk_decay = jnp.exp(a_cum[:, :, -1])

    def step(h, inp):
        st, dec = inp
        return dec[..., None, None] * h + st, h

    h0 = jnp.zeros((b, G, J, P, N), X.dtype)
    _, prev = lax.scan(step, h0, (jnp.moveaxis(states, 1, 0), jnp.moveaxis(chunk_decay, 1, 0)))
    prev = jnp.moveaxis(prev, 0, 1)
    y_off = jnp.einsum('bclgn,bcgjpn->bclgjp', Cc, prev) * jnp.exp(a_cum)[..., None]
    y = y_diag + y_off + D.reshape(G, J)[:, :, None] * xc
    return y.reshape(b, S, H * P)


def _dilated_window_attn(q, k, v, steps, dilation, slopes):
    b, S, h, e = q.shape
    n = S // dilation
    nb = -(-n // steps)
    n_pad = nb * steps

    def to_sub(t):
        t = t.reshape(b, n, dilation, h, e).transpose(0, 3, 2, 1, 4)
        return jnp.pad(t, ((0, 0), (0, 0), (0, 0), (0, n_pad - n), (0, 0)))

    qs, ks, vs = to_sub(q), to_sub(k), to_sub(v)
    qb = qs.reshape(b, h, dilation, nb, steps, e)

    def band(t):
        tp = jnp.pad(t, ((0, 0), (0, 0), (0, 0), (steps, 0), (0, 0)))
        prev = tp[:, :, :, :n_pad].reshape(b, h, dilation, nb, steps, e)
        cur = t.reshape(b, h, dilation, nb, steps, e)
        return jnp.concatenate([prev, cur], axis=4)

    kb, vb = band(ks), band(vs)
    s = jnp.einsum('bhdnqe,bhdnke->bhdnqk', qb, kb) * (e ** -0.5)
    qi = jnp.arange(steps)[:, None]
    kj = jnp.arange(2 * steps)[None, :]
    rel = qi + steps - kj
    blk = jnp.arange(nb)[:, None, None]
    valid = (rel >= 0) & (rel <= steps) & (blk * steps + kj - steps >= 0)
    alibi = -slopes[:, None, None] * (rel * dilation).astype(jnp.float32)
    s = jnp.where(valid, s + alibi[:, None, None], -jnp.inf)
    lse = jax.nn.logsumexp(s, axis=-1)
    p = jnp.exp(s - lse[..., None])
    o = jnp.einsum('bhdnqk,bhdnke->bhdnqe', p, vb)
    o = o.reshape(b, h, dilation, n_pad, e)[:, :, :, :n].transpose(0, 3, 2, 1, 4).reshape(b, S, h, e)
    lse = lse.reshape(b, h, dilation, n_pad)[:, :, :, :n].transpose(0, 3, 2, 1).reshape(b, S, h)
    return o, lse


def _multi_scale_pool(u, w_mix):
    b, S, _ = u.shape
    ug = u.reshape(b, S, len(POOL_WINDOWS), POOL_GROUP_WIDTH)
    cs = jnp.cumsum(ug, axis=1)
    t = jnp.arange(S)
    outs = []
    for gi, w in enumerate(POOL_WINDOWS):
        csg = cs[:, :, gi]
        shifted = jnp.pad(csg, ((0, 0), (w, 0), (0, 0)))[:, :S]
        count = jnp.minimum(t + 1, w).astype(jnp.float32)[None, :, None]
        outs.append((csg - shifted) / count - ug[:, :, gi])
    pooled = jnp.stack(outs, axis=2)
    return jnp.einsum('bsgi,gio->bsgo', pooled, w_mix).reshape(b, S, POOL_WIDTH)


def _hybrid_mixer(h, w_in, conv_w, conv_b, dt_bias, a_log, d_skip, ssd_norm_w, w_ssd_out,
                  q_norm_w, k_norm_w, w_attn_out, w_pool_mix, pool_scale, w_pool_out, w_out, slopes):
    f32 = jnp.float32
    dtype = h.dtype
    b, S, _ = h.shape
    splits = np.cumsum(IN_SIZES)[:-1].tolist()
    z, xbc, dt_raw, q, k, v, u, gates = jnp.split(h @ w_in, splits, axis=-1)

    xbc = jax.nn.silu(_causal_dwconv(xbc, conv_w, conv_b)).astype(f32)
    xs, Bm, Cm = jnp.split(xbc, [SSD_D_INNER, SSD_D_INNER + SSD_GROUPS * SSD_STATE], axis=-1)
    dt = jax.nn.softplus(dt_raw.astype(f32) + dt_bias.astype(f32))
    A = -jnp.exp(a_log.astype(f32))
    y = _ssd(xs.reshape(b, S, SSD_HEADS, SSD_HEAD_DIM), dt, A,
             Bm.reshape(b, S, SSD_GROUPS, SSD_STATE), Cm.reshape(b, S, SSD_GROUPS, SSD_STATE),
             d_skip.astype(f32))
    y = (y * jax.nn.silu(z.astype(f32))).reshape(b, S, SSD_GROUPS, SSD_D_INNER // SSD_GROUPS)
    y = (y * lax.rsqrt(jnp.mean(y * y, axis=-1, keepdims=True) + EPS)).reshape(b, S, SSD_D_INNER)
    y_ssd = (y * ssd_norm_w.astype(f32)).astype(dtype) @ w_ssd_out

    qh = _rms(q.reshape(b, S, ATTN_HEADS, ATTN_HEAD_DIM).astype(f32), q_norm_w)
    kh = _rms(k.reshape(b, S, ATTN_HEADS, ATTN_HEAD_DIM).astype(f32), k_norm_w)
    vh = v.reshape(b, S, ATTN_HEADS, ATTN_HEAD_DIM).astype(f32)
    outs, lses = [], []
    for gi, (win, dil) in enumerate(DILATED_GROUPS):
        sl = slice(gi * ATTN_HEADS_PER_GROUP, (gi + 1) * ATTN_HEADS_PER_GROUP)
        o, lse = _dilated_window_attn(qh[:, :, sl], kh[:, :, sl], vh[:, :, sl], win // dil, dil, slopes[gi])
        outs.append(o)
        lses.append(lse)
    wts = jax.nn.softmax(jnp.stack(lses, axis=0), axis=0)
    o = jnp.sum(wts[..., None] * jnp.stack(outs, axis=0), axis=0).reshape(b, S, ATTN_OUT_WIDTH)
    y_attn = o.astype(dtype) @ w_attn_out

    y_pool = (_multi_scale_pool(u.astype(f32), w_pool_mix) * pool_scale).astype(dtype) @ w_pool_out

    g_ssd, g_attn, g_pool = jnp.split(jax.nn.sigmoid(gates.astype(f32)).astype(dtype), N_BRANCHES, axis=-1)
    return (g_ssd * y_ssd + g_attn * y_attn + g_pool * y_pool) @ w_out


def _fwd_setup_inputs(seed: int = 0) -> dict:
    key = jax.random.key(seed)
    ks = jax.random.split(key, 24)
    n = jax.random.normal
    L, D = DEPTH, D_MODEL
    u01 = jax.random.uniform(ks[9], (L, SSD_HEADS))
    dt0 = jnp.exp(u01 * (math.log(0.1) - math.log(0.001)) + math.log(0.001))
    return {
        "x": n(ks[0], (BATCH, SEQ, D), jnp.float32),
        "c": n(ks[1], (BATCH, D), jnp.float32),
        "w_ada": n(ks[2], (L, D, 6 * D)) * D ** -0.5,
        "b_ada": n(ks[3], (L, 6 * D)) * 0.01,
        "norm1_w": 1.0 + 0.05 * n(ks[4], (L, D)),
        "norm2_w": 1.0 + 0.05 * n(ks[5], (L, D)),
        "w_in": n(ks[6], (L, D, IN_WIDTH)) * D ** -0.5,
        "conv_w": n(ks[7], (L, SSD_CONV, SSD_CONV_DIM)) * SSD_CONV ** -0.5,
        "conv_b": n(ks[8], (L, SSD_CONV_DIM)) * 0.01,
        "dt_bias": dt0 + jnp.log(-jnp.expm1(-dt0)),
        "a_log": jnp.log(jax.random.uniform(ks[10], (L, SSD_HEADS), minval=1.0, maxval=16.0)),
        "d_skip": 1.0 + 0.05 * n(ks[11], (L, SSD_HEADS)),
        "ssd_norm_w": 1.0 + 0.05 * n(ks[12], (L, SSD_D_INNER)),
        "w_ssd_out": n(ks[13], (L, SSD_D_INNER, D)) * SSD_D_INNER ** -0.5,
        "q_norm_w": 1.0 + 0.05 * n(ks[14], (L, ATTN_HEAD_DIM)),
        "k_norm_w": 1.0 + 0.05 * n(ks[15], (L, ATTN_HEAD_DIM)),
        "w_attn_out": n(ks[16], (L, ATTN_OUT_WIDTH, D)) * ATTN_OUT_WIDTH ** -0.5,
        "w_pool_mix": n(ks[17], (L, len(POOL_WINDOWS), POOL_GROUP_WIDTH, POOL_GROUP_WIDTH)) * POOL_GROUP_WIDTH ** -0.5,
        "pool_scale": 1.0 + 0.05 * n(ks[18], (L, POOL_WIDTH)),
        "w_pool_out": n(ks[19], (L, POOL_WIDTH, D)) * POOL_WIDTH ** -0.5,
        "w_out": n(ks[20], (L, D, D)) * D ** -0.5,
        "w_ff1": n(ks[21], (L, D, D_FF)) * D ** -0.5,
        "w_ff2": n(ks[22], (L, D_FF, D)) * D_FF ** -0.5,
    }


def _fwd_reference(x, c, w_ada, b_ada, norm1_w, norm2_w, w_in, conv_w, conv_b, dt_bias, a_log, d_skip,
              ssd_norm_w, w_ssd_out, q_norm_w, k_norm_w, w_attn_out, w_pool_mix, pool_scale,
              w_pool_out, w_out, w_ff1, w_ff2):
    cond = jax.nn.silu(c)
    slopes = jnp.asarray(_alibi_slopes(ATTN_HEADS)).reshape(len(DILATED_GROUPS), ATTN_HEADS_PER_GROUP)
    for l in range(DEPTH):
        mod = (cond @ w_ada[l] + b_ada[l])[:, None, :]
        sh1, sc1, g1, sh2, sc2, g2 = jnp.split(mod, 6, axis=-1)
        h = _rms(x, norm1_w[l]) * (1 + sc1) + sh1
        x = x + g1 * _hybrid_mixer(h, w_in[l], conv_w[l], conv_b[l], dt_bias[l], a_log[l], d_skip[l],
                                   ssd_norm_w[l], w_ssd_out[l], q_norm_w[l], k_norm_w[l], w_attn_out[l],
                                   w_pool_mix[l], pool_scale[l], w_pool_out[l], w_out[l], slopes)
        h = _rms(x, norm2_w[l]) * (1 + sc2) + sh2
        x = x + g2 * (jnp.square(jax.nn.relu(h @ w_ff1[l])) @ w_ff2[l])
    return x


import jax as _jax
import jax.numpy as _jnp

TWIN_FORMAT = 'train_step'
FWD_PARAMS = ['x', 'c', 'w_ada', 'b_ada', 'norm1_w', 'norm2_w', 'w_in', 'conv_w', 'conv_b', 'dt_bias', 'a_log', 'd_skip', 'ssd_norm_w', 'w_ssd_out', 'q_norm_w', 'k_norm_w', 'w_attn_out', 'w_pool_mix', 'pool_scale', 'w_pool_out', 'w_out', 'w_ff1', 'w_ff2']
TWIN_WEIGHTS = ['w_ada', 'b_ada', 'norm1_w', 'norm2_w', 'w_in', 'conv_w', 'conv_b', 'dt_bias', 'a_log', 'd_skip', 'ssd_norm_w', 'w_ssd_out', 'q_norm_w', 'k_norm_w', 'w_attn_out', 'w_pool_mix', 'pool_scale', 'w_pool_out', 'w_out', 'w_ff1', 'w_ff2']
TWIN_DIFF_INPUT = 'x'
TWIN_INPUTS = ['x', 'c', 'w_ada', 'b_ada', 'norm1_w', 'norm2_w', 'w_in', 'conv_w', 'conv_b', 'dt_bias', 'a_log', 'd_skip', 'ssd_norm_w', 'w_ssd_out', 'q_norm_w', 'k_norm_w', 'w_attn_out', 'w_pool_mix', 'pool_scale', 'w_pool_out', 'w_out', 'w_ff1', 'w_ff2', 'loss_target', 'm_w_ada', 'm_b_ada', 'm_norm1_w', 'm_norm2_w', 'm_w_in', 'm_conv_w', 'm_conv_b', 'm_dt_bias', 'm_a_log', 'm_d_skip', 'm_ssd_norm_w', 'm_w_ssd_out', 'm_q_norm_w', 'm_k_norm_w', 'm_w_attn_out', 'm_w_pool_mix', 'm_pool_scale', 'm_w_pool_out', 'm_w_out', 'm_w_ff1', 'm_w_ff2', 'v_w_ada', 'v_b_ada', 'v_norm1_w', 'v_norm2_w', 'v_w_in', 'v_conv_w', 'v_conv_b', 'v_dt_bias', 'v_a_log', 'v_d_skip', 'v_ssd_norm_w', 'v_w_ssd_out', 'v_q_norm_w', 'v_k_norm_w', 'v_w_attn_out', 'v_w_pool_mix', 'v_pool_scale', 'v_w_pool_out', 'v_w_out', 'v_w_ff1', 'v_w_ff2']
TWIN_OUTPUTS = ['loss', 'grad_x', 'grad_w_ada', 'grad_b_ada', 'grad_norm1_w', 'grad_norm2_w', 'grad_w_in', 'grad_conv_w', 'grad_conv_b', 'grad_dt_bias', 'grad_a_log', 'grad_d_skip', 'grad_ssd_norm_w', 'grad_w_ssd_out', 'grad_q_norm_w', 'grad_k_norm_w', 'grad_w_attn_out', 'grad_w_pool_mix', 'grad_pool_scale', 'grad_w_pool_out', 'grad_w_out', 'grad_w_ff1', 'grad_w_ff2', 'delta_w_ada', 'delta_b_ada', 'delta_norm1_w', 'delta_norm2_w', 'delta_w_in', 'delta_conv_w', 'delta_conv_b', 'delta_dt_bias', 'delta_a_log', 'delta_d_skip', 'delta_ssd_norm_w', 'delta_w_ssd_out', 'delta_q_norm_w', 'delta_k_norm_w', 'delta_w_attn_out', 'delta_w_pool_mix', 'delta_pool_scale', 'delta_w_pool_out', 'delta_w_out', 'delta_w_ff1', 'delta_w_ff2', 'new_m_w_ada', 'new_m_b_ada', 'new_m_norm1_w', 'new_m_norm2_w', 'new_m_w_in', 'new_m_conv_w', 'new_m_conv_b', 'new_m_dt_bias', 'new_m_a_log', 'new_m_d_skip', 'new_m_ssd_norm_w', 'new_m_w_ssd_out', 'new_m_q_norm_w', 'new_m_k_norm_w', 'new_m_w_attn_out', 'new_m_w_pool_mix', 'new_m_pool_scale', 'new_m_w_pool_out', 'new_m_w_out', 'new_m_w_ff1', 'new_m_w_ff2', 'new_v_w_ada', 'new_v_b_ada', 'new_v_norm1_w', 'new_v_norm2_w', 'new_v_w_in', 'new_v_conv_w', 'new_v_conv_b', 'new_v_dt_bias', 'new_v_a_log', 'new_v_d_skip', 'new_v_ssd_norm_w', 'new_v_w_ssd_out', 'new_v_q_norm_w', 'new_v_k_norm_w', 'new_v_w_attn_out', 'new_v_w_pool_mix', 'new_v_pool_scale', 'new_v_w_pool_out', 'new_v_w_out', 'new_v_w_ff1', 'new_v_w_ff2']
TWIN_LEAF_KINDS = {'loss': 'loss', 'grad_x': 'grad_x', 'grad_w_ada': 'grad_w', 'grad_b_ada': 'grad_w', 'grad_norm1_w': 'grad_w', 'grad_norm2_w': 'grad_w', 'grad_w_in': 'grad_w', 'grad_conv_w': 'grad_w', 'grad_conv_b': 'grad_w', 'grad_dt_bias': 'grad_w', 'grad_a_log': 'grad_w', 'grad_d_skip': 'grad_w', 'grad_ssd_norm_w': 'grad_w', 'grad_w_ssd_out': 'grad_w', 'grad_q_norm_w': 'grad_w', 'grad_k_norm_w': 'grad_w', 'grad_w_attn_out': 'grad_w', 'grad_w_pool_mix': 'grad_w', 'grad_pool_scale': 'grad_w', 'grad_w_pool_out': 'grad_w', 'grad_w_out': 'grad_w', 'grad_w_ff1': 'grad_w', 'grad_w_ff2': 'grad_w', 'delta_w_ada': 'delta_w', 'delta_b_ada': 'delta_w', 'delta_norm1_w': 'delta_w', 'delta_norm2_w': 'delta_w', 'delta_w_in': 'delta_w', 'delta_conv_w': 'delta_w', 'delta_conv_b': 'delta_w', 'delta_dt_bias': 'delta_w', 'delta_a_log': 'delta_w', 'delta_d_skip': 'delta_w', 'delta_ssd_norm_w': 'delta_w', 'delta_w_ssd_out': 'delta_w', 'delta_q_norm_w': 'delta_w', 'delta_k_norm_w': 'delta_w', 'delta_w_attn_out': 'delta_w', 'delta_w_pool_mix': 'delta_w', 'delta_pool_scale': 'delta_w', 'delta_w_pool_out': 'delta_w', 'delta_w_out': 'delta_w', 'delta_w_ff1': 'delta_w', 'delta_w_ff2': 'delta_w', 'new_m_w_ada': 'new_m', 'new_m_b_ada': 'new_m', 'new_m_norm1_w': 'new_m', 'new_m_norm2_w': 'new_m', 'new_m_w_in': 'new_m', 'new_m_conv_w': 'new_m', 'new_m_conv_b': 'new_m', 'new_m_dt_bias': 'new_m', 'new_m_a_log': 'new_m', 'new_m_d_skip': 'new_m', 'new_m_ssd_norm_w': 'new_m', 'new_m_w_ssd_out': 'new_m', 'new_m_q_norm_w': 'new_m', 'new_m_k_norm_w': 'new_m', 'new_m_w_attn_out': 'new_m', 'new_m_w_pool_mix': 'new_m', 'new_m_pool_scale': 'new_m', 'new_m_w_pool_out': 'new_m', 'new_m_w_out': 'new_m', 'new_m_w_ff1': 'new_m', 'new_m_w_ff2': 'new_m', 'new_v_w_ada': 'new_v', 'new_v_b_ada': 'new_v', 'new_v_norm1_w': 'new_v', 'new_v_norm2_w': 'new_v', 'new_v_w_in': 'new_v', 'new_v_conv_w': 'new_v', 'new_v_conv_b': 'new_v', 'new_v_dt_bias': 'new_v', 'new_v_a_log': 'new_v', 'new_v_d_skip': 'new_v', 'new_v_ssd_norm_w': 'new_v', 'new_v_w_ssd_out': 'new_v', 'new_v_q_norm_w': 'new_v', 'new_v_k_norm_w': 'new_v', 'new_v_w_attn_out': 'new_v', 'new_v_w_pool_mix': 'new_v', 'new_v_pool_scale': 'new_v', 'new_v_w_pool_out': 'new_v', 'new_v_w_out': 'new_v', 'new_v_w_ff1': 'new_v', 'new_v_w_ff2': 'new_v'}


def _forward(args):
    return _fwd_reference(*[args[k] for k in FWD_PARAMS])


def _output_shape():
    def fwd():
        inp = _fwd_setup_inputs(0)
        return _fwd_reference(*[inp[k] for k in FWD_PARAMS])
    out = _jax.eval_shape(fwd)
    return out.shape, out.dtype

N_MICROBATCH = 1
ADAM_LR = 0.001
ADAM_B1 = 0.9
ADAM_B2 = 0.999
ADAM_EPS = 1e-08
ADAM_WD = 0.01
ADAM_STEP = 10
PER_EXAMPLE_BATCH_AXIS = {'x': 0, 'c': 0, 'loss_target': 0}
SHARED_INPUTS = []
_WEIGHT_DTYPES = {'w_ada': _jnp.float32, 'b_ada': _jnp.float32, 'norm1_w': _jnp.float32, 'norm2_w': _jnp.float32, 'w_in': _jnp.float32, 'conv_w': _jnp.float32, 'conv_b': _jnp.float32, 'dt_bias': _jnp.float32, 'a_log': _jnp.float32, 'd_skip': _jnp.float32, 'ssd_norm_w': _jnp.float32, 'w_ssd_out': _jnp.float32, 'q_norm_w': _jnp.float32, 'k_norm_w': _jnp.float32, 'w_attn_out': _jnp.float32, 'w_pool_mix': _jnp.float32, 'pool_scale': _jnp.float32, 'w_pool_out': _jnp.float32, 'w_out': _jnp.float32, 'w_ff1': _jnp.float32, 'w_ff2': _jnp.float32}
MOMENT_SCALE = {'w_ada': 1.946449e+01, 'b_ada': 4.331422e+01, 'norm1_w': 6.917886e+00, 'norm2_w': 9.919659e+01, 'w_in': 2.607050e+00, 'conv_w': 4.211280e+00, 'conv_b': 5.017272e+00, 'dt_bias': 3.609879e+00, 'a_log': 1.991892e+01, 'd_skip': 1.036131e+01, 'ssd_norm_w': 1.062174e+01, 'w_ssd_out': 5.852020e+00, 'q_norm_w': 3.960487e+00, 'k_norm_w': 3.957547e+00, 'w_attn_out': 5.161759e+00, 'w_pool_mix': 6.936027e-01, 'pool_scale': 3.220255e+00, 'w_pool_out': 6.899365e-01, 'w_out': 7.796025e+00, 'w_ff1': 1.142832e+01, 'w_ff2': 2.609520e+01}


def _to_microbatches(a, axis):
    t = _jnp.moveaxis(a, axis, 0)
    t = t.reshape((N_MICROBATCH, t.shape[0] // N_MICROBATCH) + t.shape[1:])
    return _jnp.moveaxis(t, 1, axis + 1)


def setup_inputs(seed: int = 0) -> dict:
    inp = _fwd_setup_inputs(seed)
    key = _jax.random.fold_in(_jax.random.key(seed), 7919)
    shape, _ = _output_shape()
    out = dict(inp)
    out["loss_target"] = _jax.random.normal(_jax.random.fold_in(key, 0), shape, _jnp.float32)
    for i, name in enumerate(TWIN_WEIGHTS):
        w = inp[name].astype(_jnp.float32)
        if MOMENT_SCALE is None:
            s = _jnp.sqrt(_jnp.mean(_jnp.square(w)) + 1e-30)
        else:
            s = MOMENT_SCALE[name]
        km, kv = _jax.random.split(_jax.random.fold_in(key, i + 1))
        out[name] = w
        out["m_" + name] = s * _jax.random.normal(km, w.shape, _jnp.float32)
        out["v_" + name] = (s * s) * _jax.random.uniform(kv, w.shape, _jnp.float32, 0.5, 1.5)
    if N_MICROBATCH > 1:
        for name, axis in PER_EXAMPLE_BATCH_AXIS.items():
            out[name] = _to_microbatches(out[name], axis)
    return {'x': out['x'], 'c': out['c'], 'w_ada': out['w_ada'], 'b_ada': out['b_ada'], 'norm1_w': out['norm1_w'], 'norm2_w': out['norm2_w'], 'w_in': out['w_in'], 'conv_w': out['conv_w'], 'conv_b': out['conv_b'], 'dt_bias': out['dt_bias'], 'a_log': out['a_log'], 'd_skip': out['d_skip'], 'ssd_norm_w': out['ssd_norm_w'], 'w_ssd_out': out['w_ssd_out'], 'q_norm_w': out['q_norm_w'], 'k_norm_w': out['k_norm_w'], 'w_attn_out': out['w_attn_out'], 'w_pool_mix': out['w_pool_mix'], 'pool_scale': out['pool_scale'], 'w_pool_out': out['w_pool_out'], 'w_out': out['w_out'], 'w_ff1': out['w_ff1'], 'w_ff2': out['w_ff2'], 'loss_target': out['loss_target'], 'm_w_ada': out['m_w_ada'], 'm_b_ada': out['m_b_ada'], 'm_norm1_w': out['m_norm1_w'], 'm_norm2_w': out['m_norm2_w'], 'm_w_in': out['m_w_in'], 'm_conv_w': out['m_conv_w'], 'm_conv_b': out['m_conv_b'], 'm_dt_bias': out['m_dt_bias'], 'm_a_log': out['m_a_log'], 'm_d_skip': out['m_d_skip'], 'm_ssd_norm_w': out['m_ssd_norm_w'], 'm_w_ssd_out': out['m_w_ssd_out'], 'm_q_norm_w': out['m_q_norm_w'], 'm_k_norm_w': out['m_k_norm_w'], 'm_w_attn_out': out['m_w_attn_out'], 'm_w_pool_mix': out['m_w_pool_mix'], 'm_pool_scale': out['m_pool_scale'], 'm_w_pool_out': out['m_w_pool_out'], 'm_w_out': out['m_w_out'], 'm_w_ff1': out['m_w_ff1'], 'm_w_ff2': out['m_w_ff2'], 'v_w_ada': out['v_w_ada'], 'v_b_ada': out['v_b_ada'], 'v_norm1_w': out['v_norm1_w'], 'v_norm2_w': out['v_norm2_w'], 'v_w_in': out['v_w_in'], 'v_conv_w': out['v_conv_w'], 'v_conv_b': out['v_conv_b'], 'v_dt_bias': out['v_dt_bias'], 'v_a_log': out['v_a_log'], 'v_d_skip': out['v_d_skip'], 'v_ssd_norm_w': out['v_ssd_norm_w'], 'v_w_ssd_out': out['v_w_ssd_out'], 'v_q_norm_w': out['v_q_norm_w'], 'v_k_norm_w': out['v_k_norm_w'], 'v_w_attn_out': out['v_w_attn_out'], 'v_w_pool_mix': out['v_w_pool_mix'], 'v_pool_scale': out['v_pool_scale'], 'v_w_pool_out': out['v_w_pool_out'], 'v_w_out': out['v_w_out'], 'v_w_ff1': out['v_w_ff1'], 'v_w_ff2': out['v_w_ff2']}


def _loss(weights, diff, rest, loss_target):
    with _jax.named_scope("forward"):
        args = {**rest, TWIN_DIFF_INPUT: diff, **{k: w.astype(_WEIGHT_DTYPES[k]) for k, w in weights.items()}}
        y = _forward(args)
    with _jax.named_scope("loss_head"):
        err = _jnp.square(y.astype(_jnp.float32) - loss_target)
        return 0.5 * _jnp.sum(_jnp.mean(err, axis=-1)) if err.ndim else 0.5 * err


def _adamw(w, g, m, v):
    m = ADAM_B1 * m + (1.0 - ADAM_B1) * g
    v = ADAM_B2 * v + (1.0 - ADAM_B2) * _jnp.square(g)
    m_hat = m / (1.0 - ADAM_B1 ** ADAM_STEP)
    v_hat = v / (1.0 - ADAM_B2 ** ADAM_STEP)
    delta = -ADAM_LR * (m_hat / (_jnp.sqrt(v_hat) + ADAM_EPS) + ADAM_WD * w)
    return delta, m, v


def reference(x, c, w_ada, b_ada, norm1_w, norm2_w, w_in, conv_w, conv_b, dt_bias, a_log, d_skip, ssd_norm_w, w_ssd_out, q_norm_w, k_norm_w, w_attn_out, w_pool_mix, pool_scale, w_pool_out, w_out, w_ff1, w_ff2, loss_target, m_w_ada, m_b_ada, m_norm1_w, m_norm2_w, m_w_in, m_conv_w, m_conv_b, m_dt_bias, m_a_log, m_d_skip, m_ssd_norm_w, m_w_ssd_out, m_q_norm_w, m_k_norm_w, m_w_attn_out, m_w_pool_mix, m_pool_scale, m_w_pool_out, m_w_out, m_w_ff1, m_w_ff2, v_w_ada, v_b_ada, v_norm1_w, v_norm2_w, v_w_in, v_conv_w, v_conv_b, v_dt_bias, v_a_log, v_d_skip, v_ssd_norm_w, v_w_ssd_out, v_q_norm_w, v_k_norm_w, v_w_attn_out, v_w_pool_mix, v_pool_scale, v_w_pool_out, v_w_out, v_w_ff1, v_w_ff2):
    given = dict(x=x, c=c, w_ada=w_ada, b_ada=b_ada, norm1_w=norm1_w, norm2_w=norm2_w, w_in=w_in, conv_w=conv_w, conv_b=conv_b, dt_bias=dt_bias, a_log=a_log, d_skip=d_skip, ssd_norm_w=ssd_norm_w, w_ssd_out=w_ssd_out, q_norm_w=q_norm_w, k_norm_w=k_norm_w, w_attn_out=w_attn_out, w_pool_mix=w_pool_mix, pool_scale=pool_scale, w_pool_out=w_pool_out, w_out=w_out, w_ff1=w_ff1, w_ff2=w_ff2, loss_target=loss_target, m_w_ada=m_w_ada, m_b_ada=m_b_ada, m_norm1_w=m_norm1_w, m_norm2_w=m_norm2_w, m_w_in=m_w_in, m_conv_w=m_conv_w, m_conv_b=m_conv_b, m_dt_bias=m_dt_bias, m_a_log=m_a_log, m_d_skip=m_d_skip, m_ssd_norm_w=m_ssd_norm_w, m_w_ssd_out=m_w_ssd_out, m_q_norm_w=m_q_norm_w, m_k_norm_w=m_k_norm_w, m_w_attn_out=m_w_attn_out, m_w_pool_mix=m_w_pool_mix, m_pool_scale=m_pool_scale, m_w_pool_out=m_w_pool_out, m_w_out=m_w_out, m_w_ff1=m_w_ff1, m_w_ff2=m_w_ff2, v_w_ada=v_w_ada, v_b_ada=v_b_ada, v_norm1_w=v_norm1_w, v_norm2_w=v_norm2_w, v_w_in=v_w_in, v_conv_w=v_conv_w, v_conv_b=v_conv_b, v_dt_bias=v_dt_bias, v_a_log=v_a_log, v_d_skip=v_d_skip, v_ssd_norm_w=v_ssd_norm_w, v_w_ssd_out=v_w_ssd_out, v_q_norm_w=v_q_norm_w, v_k_norm_w=v_k_norm_w, v_w_attn_out=v_w_attn_out, v_w_pool_mix=v_w_pool_mix, v_pool_scale=v_pool_scale, v_w_pool_out=v_w_pool_out, v_w_out=v_w_out, v_w_ff1=v_w_ff1, v_w_ff2=v_w_ff2)
    weights = {n: given[n] for n in TWIN_WEIGHTS}
    shared = {n: given[n] for n in SHARED_INPUTS}
    per_example = {n: given[n] for n in ['x', 'c']}
    grad_fn = _jax.value_and_grad(_loss, argnums=(0, 1))

    def one_microbatch(ex, loss_target):
        ex = dict(ex)
        diff = ex.pop(TWIN_DIFF_INPUT)
        return grad_fn(weights, diff, {**shared, **ex}, loss_target)

    if N_MICROBATCH == 1:
        loss, (grad_w, grad_x) = one_microbatch(per_example, given["loss_target"])
    else:
        def body(carry, xs):
            loss_sum, grad_sum = carry
            l_k, (gw_k, gx_k) = one_microbatch(xs[0], xs[1])
            with _jax.named_scope("update"):
                return (loss_sum + l_k, _jax.tree.map(_jnp.add, grad_sum, gw_k)), gx_k

        init = (_jnp.zeros((), _jnp.float32), _jax.tree.map(_jnp.zeros_like, weights))
        (loss, grad_w), grad_x = _jax.lax.scan(body, init, (per_example, given["loss_target"]))
    with _jax.named_scope("update"):
        delta_w, new_m, new_v = {}, {}, {}
        for n in TWIN_WEIGHTS:
            delta_w[n], new_m[n], new_v[n] = _adamw(weights[n], grad_w[n], given["m_" + n], given["v_" + n])
    return (loss, grad_x, *[grad_w[n] for n in TWIN_WEIGHTS], *[delta_w[n] for n in TWIN_WEIGHTS],
            *[new_m[n] for n in TWIN_WEIGHTS], *[new_v[n] for n in TWIN_WEIGHTS])
```

```python
import functools
import math

import numpy as np
import jax
import jax.numpy as jnp
from jax import lax
from jax.experimental import pallas as pl
from jax.experimental.pallas import tpu as pltpu

F32, BF16 = jnp.float32, jnp.bfloat16
MESH = pl.DeviceIdType.MESH

D_MODEL = 1024
DEPTH = 4
N_CHIPS = 4
N_DEV = 8
SSD_HEADS = 16
SSD_HEAD_DIM = 64
SSD_STATE = 128
SSD_CHUNK = 128
SSD_CONV = 4
CONV_DIM = 1536
ATTN_HEAD_DIM = 128
ATTN_GROUP_W = 512
DILATIONS = (1, 4, 16)
ATTN_STEPS = 128
POOL_WINDOWS = (2, 4, 8, 16)
POOL_GW = 256
D_FF = 4096
EPS = 1e-6
IN_SIZES = (1024, 1536, 16, 1536, 1536, 1536, 1024, 3072)
IN_WIDTH = sum(IN_SIZES)
P_XBC, P_Q, P_K, P_V, P_GATES, P_Z, P_U, P_DT = 0, 1536, 3072, 4608, 6144, 9216, 10240, 11264
P_WIDTH = 12288
LANES = 128
NEG = -1e30
VMEM_LIMIT = 56 * 1024 * 1024

ADAM_LR, ADAM_B1, ADAM_B2, ADAM_EPS, ADAM_WD, ADAM_STEP = 0.001, 0.9, 0.999, 1e-08, 0.01, 10


def _alibi_slopes(n):
    def pow2(k):
        start = 2.0 ** (-8.0 / k)
        return [start ** (i + 1) for i in range(k)]
    if math.log2(n).is_integer():
        s = pow2(n)
    else:
        c = 2 ** math.floor(math.log2(n))
        s = pow2(c) + pow2(2 * c)[0::2][: n - c]
    return np.sort(np.asarray(s, np.float32))[::-1].copy()


SLOPES = _alibi_slopes(12).reshape(3, 4)


def _cp(*sem):
    return pltpu.CompilerParams(dimension_semantics=sem, vmem_limit_bytes=VMEM_LIMIT)


_DIMS = {"nn": (((1,), (0,)), ((), ())), "nt": (((1,), (1,)), ((), ())), "tn": (((0,), (0,)), ((), ()))}


def _dot(a, b, mode):
    return lax.dot_general(a.astype(BF16), b.astype(BF16), _DIMS[mode], preferred_element_type=F32)


@functools.partial(jax.custom_vjp, nondiff_argnums=(2,))
def _bdot(a, b, mode):
    return _dot(a, b, mode)


def _bdot_fwd(a, b, mode):
    return _dot(a, b, mode), (a, b)


def _bdot_bwd(mode, res, ct):
    a, b = res
    if mode == "nn":
        return _dot(ct, b, "nt"), _dot(a, ct, "tn")
    if mode == "nt":
        return _dot(ct, b, "nn"), _dot(ct, a, "tn")
    return _dot(b, ct, "nt"), _dot(a, ct, "nn")


_bdot.defvjp(_bdot_fwd, _bdot_bwd)


def _hdot(a, b):
    return jnp.dot(a, b, precision=lax.Precision.HIGHEST, preferred_element_type=F32)


def _tri(n, lower):
    r = lax.broadcasted_iota(jnp.int32, (n, n), 0)
    c = lax.broadcasted_iota(jnp.int32, (n, n), 1)
    return (r >= c if lower else r <= c).astype(F32)


@jax.custom_vjp
def _csum(a):
    return _hdot(_tri(a.shape[0], True), a)


def _csum_fwd(a):
    return _csum(a), None


def _csum_bwd(_, ct):
    return (_hdot(_tri(ct.shape[0], False), ct),)


_csum.defvjp(_csum_fwd, _csum_bwd)


def _softplus(x):
    return jnp.maximum(x, 0.0) + jnp.log(1.0 + jnp.exp(-jnp.abs(x)))


def _sigmoid(x):
    return 1.0 / (1.0 + jnp.exp(-x))


def _silu(x):
    return x * _sigmoid(x)


def _tile(n, cap):
    t = min(n, cap)
    while n % t:
        t //= 2
    return t


def _matmul(name, a, b, mode, out_dtype=F32, precise=False):
    if mode == "nn":
        (m, k), n = a.shape, b.shape[1]
    elif mode == "nt":
        (m, k), n = a.shape, b.shape[0]
    else:
        (k, m), n = a.shape, b.shape[1]
    tm, tn, tk = _tile(m, 1024), _tile(n, 1024), _tile(k, 512)
    a_spec = pl.BlockSpec((tk, tm), lambda i, j, l: (l, i)) if mode == "tn" else pl.BlockSpec((tm, tk), lambda i, j, l: (i, l))
    b_spec = pl.BlockSpec((tn, tk), lambda i, j, l: (j, l)) if mode == "nt" else pl.BlockSpec((tk, tn), lambda i, j, l: (l, j))
    nk = k // tk

    def body(a_ref, b_ref, o_ref, acc_ref):
        l = pl.program_id(2)
        if precise:
            part = lax.dot_general(a_ref[...], b_ref[...], _DIMS[mode], precision=lax.Precision.HIGHEST,
                                   preferred_element_type=F32)
        else:
            part = _dot(a_ref[...], b_ref[...], mode)

        @pl.when(l == 0)
        def _():
            acc_ref[...] = part

        @pl.when(l > 0)
        def _():
            acc_ref[...] += part

        @pl.when(l == nk - 1)
        def _():
            o_ref[...] = acc_ref[...].astype(o_ref.dtype)

    return pl.pallas_call(
        body, grid=(m // tm, n // tn, nk), in_specs=[a_spec, b_spec],
        out_specs=pl.BlockSpec((tm, tn), lambda i, j, l: (i, j)),
        out_shape=jax.ShapeDtypeStruct((m, n), out_dtype),
        scratch_shapes=[pltpu.VMEM((tm, tn), F32)],
        compiler_params=_cp("parallel", "parallel", "arbitrary"), name=name)(a, b)


def _group_matmul(name, a, w, mode, out_dtype=F32):
    s = a.shape[0]
    tb = 512
    gw = POOL_GW
    if mode == "tn":
        def body(a_ref, b_ref, o_ref):
            part = _dot(a_ref[...], b_ref[...], "tn")

            @pl.when(pl.program_id(1) == 0)
            def _():
                o_ref[0] = part

            @pl.when(pl.program_id(1) > 0)
            def _():
                o_ref[0] += part

        return pl.pallas_call(
            body, grid=(4, s // tb),
            in_specs=[pl.BlockSpec((tb, gw), lambda g, i: (i, g)), pl.BlockSpec((tb, gw), lambda g, i: (i, g))],
            out_specs=pl.BlockSpec((1, gw, gw), lambda g, i: (g, 0, 0)),
            out_shape=jax.ShapeDtypeStruct((4, gw, gw), F32),
            compiler_params=_cp("parallel", "arbitrary"), name=name)(a, w)

    def body(a_ref, w_ref, o_ref):
        o_ref[...] = _dot(a_ref[...], w_ref[0], mode).astype(o_ref.dtype)

    return pl.pallas_call(
        body, grid=(s // tb, 4),
        in_specs=[pl.BlockSpec((tb, gw), lambda i, g: (i, g)), pl.BlockSpec((1, gw, gw), lambda i, g: (g, 0, 0))],
        out_specs=pl.BlockSpec((tb, gw), lambda i, g: (i, g)),
        out_shape=jax.ShapeDtypeStruct((s, 4 * gw), out_dtype),
        compiler_params=_cp("parallel", "parallel"), name=name)(a, w)


def _rspec(tb, width, cb):
    return pl.BlockSpec((tb, width), lambda i: (i, cb))


def _pspec(shape):
    return pl.BlockSpec(shape, lambda i: (0, 0))


def _rowwise_fwd(name, f, rows, pars, outs, tb=256):
    s = rows[0][0].shape[0]
    nin = len(rows) + len(pars)

    def body(*refs):
        res = f(*[r[...].astype(F32) for r in refs[:nin]])
        for o, v in zip(refs[nin:], res):
            o[...] = v.astype(o.dtype)

    return pl.pallas_call(
        body, grid=(s // tb,),
        in_specs=[_rspec(tb, w, cb) for _, w, cb in rows] + [_pspec(p.shape) for p in pars],
        out_specs=[_rspec(tb, w, 0) for w, _ in outs],
        out_shape=[jax.ShapeDtypeStruct((s, w), dt) for w, dt in outs],
        compiler_params=_cp("parallel"), name=name)(*[r[0] for r in rows], *pars)


def _rowwise_bwd(name, f, rows, pars, cts, need, add=None, tb=256):
    s = rows[0][0].shape[0]
    nr, npar, nc = len(rows), len(pars), len(cts)
    nin = nr + npar + nc + (1 if add is not None else 0)

    def body(*refs):
        ins = [r[...].astype(F32) for r in refs[:nr + npar]]
        _, vjp = jax.vjp(f, *ins)
        g = vjp(tuple(c[...].astype(F32) for c in refs[nr + npar:nr + npar + nc]))
        outs = refs[nin:]
        k = 0
        for j in range(nr):
            if need[j]:
                v = g[j]
                if add is not None and add[0] == j:
                    v = v + refs[nin - 1][...]
                outs[k][...] = v.astype(outs[k].dtype)
                k += 1
        first = pl.program_id(0) == 0
        for j in range(npar):
            o, v = outs[k + j], g[nr + j]

            @pl.when(first)
            def _(o=o, v=v):
                o[...] = v

            @pl.when(jnp.logical_not(first))
            def _(o=o, v=v):
                o[...] += v

    in_specs = ([_rspec(tb, w, cb) for _, w, cb in rows] + [_pspec(p.shape) for p in pars]
                + [_rspec(tb, w, cb) for _, w, cb in cts])
    args = [r[0] for r in rows] + list(pars) + [c[0] for c in cts]
    if add is not None:
        in_specs.append(_rspec(tb, rows[add[0]][1], 0))
        args.append(add[1])
    gr = [(w, F32) for (_, w, _), nd in zip(rows, need) if nd]
    return pl.pallas_call(
        body, grid=(s // tb,), in_specs=in_specs,
        out_specs=[_rspec(tb, w, 0) for w, _ in gr] + [_pspec(p.shape) for p in pars],
        out_shape=[jax.ShapeDtypeStruct((s, w), dt) for w, dt in gr] + [jax.ShapeDtypeStruct(p.shape, F32) for p in pars],
        compiler_params=_cp("arbitrary"), name=name)(*args)


def _f_norm(x, nw, sc, sh):
    r = lax.rsqrt(jnp.mean(x * x, axis=-1, keepdims=True) + EPS)
    return ((x * r * nw) * (1.0 + sc) + sh,)


def _f_ssdgate(y, z, w):
    y2 = y * _silu(z)
    low = lax.broadcasted_iota(jnp.int32, y2.shape, 1) < 512
    sq = y2 * y2
    m0 = jnp.sum(jnp.where(low, sq, 0.0), axis=-1, keepdims=True) / 512.0
    m1 = jnp.sum(jnp.where(low, 0.0, sq), axis=-1, keepdims=True) / 512.0
    r = jnp.where(low, lax.rsqrt(m0 + EPS), lax.rsqrt(m1 + EPS))
    return (y2 * r * w,)


def _head_rms(t, w):
    outs = []
    for h in range(t.shape[1] // ATTN_HEAD_DIM):
        th = t[:, h * ATTN_HEAD_DIM:(h + 1) * ATTN_HEAD_DIM]
        outs.append(th * lax.rsqrt(jnp.mean(th * th, axis=-1, keepdims=True) + EPS) * w)
    return jnp.concatenate(outs, axis=1)


def _f_qknorm(q, k, v, qw, kw):
    return _head_rms(q, qw), _head_rms(k, kw), v


def _f_combine(o1, o2, o3, l1, l2, l3):
    m = lax.stop_gradient(jnp.maximum(jnp.maximum(l1, l2), l3))
    e1, e2, e3 = jnp.exp(l1 - m), jnp.exp(l2 - m), jnp.exp(l3 - m)
    return ((e1 * o1 + e2 * o2 + e3 * o3) / (e1 + e2 + e3),)


def _f_poolscale(pm, ps):
    return (pm * ps,)


def _f_merge(gates, ys, ya, yp):
    g = _sigmoid(gates)
    return (g[:, 0:1024] * ys + g[:, 1024:2048] * ya + g[:, 2048:3072] * yp,)


def _f_resid(x, o, g):
    return (x + g * o,)


def _f_relu2(a):
    return (jnp.square(jnp.maximum(a, 0.0)),)


def _loss_and_grad(y, tgt, tb=512):
    s, d = y.shape

    def body(y_ref, t_ref, dy_ref, l_ref):
        e = y_ref[...] - t_ref[...]
        dy_ref[...] = e * (1.0 / d)
        part = jnp.zeros((1, LANES), F32) + jnp.sum(e * e) * (0.5 / d)

        @pl.when(pl.program_id(0) == 0)
        def _():
            l_ref[...] = part

        @pl.when(pl.program_id(0) > 0)
        def _():
            l_ref[...] += part

    return pl.pallas_call(
        body, grid=(s // tb,), in_specs=[_rspec(tb, d, 0), _rspec(tb, d, 0)],
        out_specs=[_rspec(tb, d, 0), _pspec((1, LANES))],
        out_shape=[jax.ShapeDtypeStruct((s, d), F32), jax.ShapeDtypeStruct((1, LANES), F32)],
        compiler_params=_cp("arbitrary"), name="loss")(y, tgt)


def _shift_down(x, j):
    rows = lax.broadcasted_iota(jnp.int32, x.shape, 0)
    return jnp.where(rows < j, 0.0, pltpu.roll(x, j, 0))


def _shift_up(x, j):
    s = x.shape[0]
    rows = lax.broadcasted_iota(jnp.int32, x.shape, 0)
    return jnp.where(rows >= s - j, 0.0, pltpu.roll(x, s - j, 0))


CONV_CB = 256


def _conv_pre(x, w_ref, b_ref):
    acc = b_ref[...] + w_ref[SSD_CONV - 1:SSD_CONV, :] * x
    for j in range(1, SSD_CONV):
        acc = acc + w_ref[SSD_CONV - 1 - j:SSD_CONV - j, :] * _shift_down(x, j)
    return acc


def _conv_fwd(proj, cw, cb):
    s = proj.shape[0]

    def body(x_ref, w_ref, b_ref, o_ref):
        o_ref[...] = _silu(_conv_pre(x_ref[...], w_ref, b_ref))

    return pl.pallas_call(
        body, grid=(CONV_DIM // CONV_CB,),
        in_specs=[pl.BlockSpec((s, CONV_CB), lambda i: (0, P_XBC // CONV_CB + i)),
                  pl.BlockSpec((SSD_CONV, CONV_CB), lambda i: (0, i)), pl.BlockSpec((1, CONV_CB), lambda i: (0, i))],
        out_specs=pl.BlockSpec((s, CONV_CB), lambda i: (0, i)),
        out_shape=jax.ShapeDtypeStruct((s, CONV_DIM), F32), compiler_params=_cp("parallel"), name="conv_fwd")(proj, cw, cb)


def _conv_bwd(proj, cw, cb, dout):
    s = proj.shape[0]

    def body(x_ref, w_ref, b_ref, d_ref, dx_ref, dw_ref, db_ref):
        x = x_ref[...]
        a = _conv_pre(x, w_ref, b_ref)
        sg = _sigmoid(a)
        da = d_ref[...] * (sg + a * sg * (1.0 - sg))
        db_ref[...] = jnp.sum(da, axis=0, keepdims=True)
        dx = w_ref[SSD_CONV - 1:SSD_CONV, :] * da
        dw_ref[SSD_CONV - 1:SSD_CONV, :] = jnp.sum(da * x, axis=0, keepdims=True)
        for j in range(1, SSD_CONV):
            dx = dx + w_ref[SSD_CONV - 1 - j:SSD_CONV - j, :] * _shift_up(da, j)
            dw_ref[SSD_CONV - 1 - j:SSD_CONV - j, :] = jnp.sum(da * _shift_down(x, j), axis=0, keepdims=True)
        dx_ref[...] = dx

    return pl.pallas_call(
        body, grid=(CONV_DIM // CONV_CB,),
        in_specs=[pl.BlockSpec((s, CONV_CB), lambda i: (0, P_XBC // CONV_CB + i)),
                  pl.BlockSpec((SSD_CONV, CONV_CB), lambda i: (0, i)), pl.BlockSpec((1, CONV_CB), lambda i: (0, i)),
                  pl.BlockSpec((s, CONV_CB), lambda i: (0, i))],
        out_specs=[pl.BlockSpec((s, CONV_CB), lambda i: (0, i)), pl.BlockSpec((SSD_CONV, CONV_CB), lambda i: (0, i)),
                   pl.BlockSpec((1, CONV_CB), lambda i: (0, i))],
        out_shape=[jax.ShapeDtypeStruct((s, CONV_DIM), F32), jax.ShapeDtypeStruct((SSD_CONV, CONV_DIM), F32),
                   jax.ShapeDtypeStruct((1, CONV_DIM), F32)],
        compiler_params=_cp("parallel"), name="conv_bwd")(proj, cw, cb, dout)


def _pool_window_sum(x, g, shift):
    s2 = x + shift(x, 1)
    s4 = s2 + shift(s2, 2)
    s8 = s4 + shift(s4, 4)
    s16 = s8 + shift(s8, 8)
    return jnp.where(g == 0, s2, jnp.where(g == 1, s4, jnp.where(g == 2, s8, s16)))


def _pool_count(shape, g):
    rows = lax.broadcasted_iota(jnp.int32, shape, 0)
    return jnp.minimum(rows + 1, jnp.left_shift(2, g)).astype(F32)


def _pool_fwd(proj):
    s = proj.shape[0]

    def body(u_ref, o_ref):
        g = pl.program_id(0)
        u = u_ref[...]
        o_ref[...] = (_pool_window_sum(u, g, _shift_down) / _pool_count(u.shape, g) - u).astype(o_ref.dtype)

    return pl.pallas_call(
        body, grid=(4,), in_specs=[pl.BlockSpec((s, POOL_GW), lambda g: (0, P_U // POOL_GW + g))],
        out_specs=pl.BlockSpec((s, POOL_GW), lambda g: (0, g)),
        out_shape=jax.ShapeDtypeStruct((s, 4 * POOL_GW), BF16), compiler_params=_cp("parallel"), name="pool_fwd")(proj)


def _pool_bwd(dp):
    s = dp.shape[0]

    def body(d_ref, o_ref):
        g = pl.program_id(0)
        d = d_ref[...]
        o_ref[...] = _pool_window_sum(d / _pool_count(d.shape, g), g, _shift_up) - d

    return pl.pallas_call(
        body, grid=(4,), in_specs=[pl.BlockSpec((s, POOL_GW), lambda g: (0, g))],
        out_specs=pl.BlockSpec((s, POOL_GW), lambda g: (0, g)),
        out_shape=jax.ShapeDtypeStruct((s, 4 * POOL_GW), F32), compiler_params=_cp("parallel"), name="pool_bwd")(dp)


N_PAIRS = SSD_HEADS // 2
STATE_ROWS = N_PAIRS * SSD_STATE


def _ssd_chunk(xbc, dtr, hprev, dtb, alog, dsk):
    L = xbc.shape[0]
    xs, bm, cm = xbc[:, 0:1024], xbc[:, 1024:1280], xbc[:, 1280:1536]
    dt = _softplus(dtr + dtb)
    a = dt * (-jnp.exp(alog))
    acum = _csum(a)
    alast = jnp.sum(a, axis=0, keepdims=True)
    xdt = xs * dt
    xdecay = xdt * jnp.exp(alast - acum)
    eacum = jnp.exp(acum)
    elast = jnp.exp(alast)
    cb = [_bdot(cm[:, g * 128:(g + 1) * 128], bm[:, g * 128:(g + 1) * 128], "nt") for g in range(2)]
    rows = lax.broadcasted_iota(jnp.int32, (L, L), 0)
    cols = lax.broadcasted_iota(jnp.int32, (L, L), 1)
    causal = rows >= cols
    lane = lax.broadcasted_iota(jnp.int32, (L, LANES), 1)
    sub = lax.broadcasted_iota(jnp.int32, (LANES, L), 0)
    ys, hs = [], []
    for p in range(N_PAIRS):
        g = p // (N_PAIRS // 2)
        sl = slice(p * LANES, (p + 1) * LANES)
        ac = acum[:, sl]
        act = ac.T
        xp = xdt[:, sl]
        hp = hprev[p * SSD_STATE:(p + 1) * SSD_STATE, :]
        y = _bdot(cm[:, g * 128:(g + 1) * 128], hp, "nn") * eacum[:, sl] + dsk[:, sl] * xs[:, sl]
        for half in range(2):
            l0 = half * SSD_HEAD_DIM
            col = jnp.sum(jnp.where(lane == l0, ac, 0.0), axis=1, keepdims=True)
            row = jnp.sum(jnp.where(sub == l0, act, 0.0), axis=0, keepdims=True)
            decay = jnp.exp(jnp.where(causal, col - row, NEG))
            xh = jnp.where((lane >= l0) & (lane < l0 + SSD_HEAD_DIM), xp, 0.0)
            y = y + _bdot(cb[g] * decay, xh, "nn")
        ys.append(y)
        hs.append(elast[:, sl] * hp + _bdot(bm[:, g * 128:(g + 1) * 128], xdecay[:, sl], "tn"))
    return tuple(ys), tuple(hs)


def _ssd_fwd(xbc, proj, dtb, alog, dsk):
    s = xbc.shape[0]
    nc = s // SSD_CHUNK

    def body(x_ref, dt_ref, b_ref, a_ref, d_ref, y_ref, hist_ref, h_ref):
        @pl.when(pl.program_id(0) == 0)
        def _():
            h_ref[...] = jnp.zeros_like(h_ref)

        hprev = h_ref[...]
        hist_ref[...] = hprev
        ys, hs = _ssd_chunk(x_ref[...], dt_ref[...], hprev, b_ref[...], a_ref[...], d_ref[...])
        for p in range(N_PAIRS):
            y_ref[:, p * LANES:(p + 1) * LANES] = ys[p]
            h_ref[p * SSD_STATE:(p + 1) * SSD_STATE, :] = hs[p]

    return pl.pallas_call(
        body, grid=(nc,),
        in_specs=[pl.BlockSpec((SSD_CHUNK, CONV_DIM), lambda i: (i, 0)),
                  pl.BlockSpec((SSD_CHUNK, 1024), lambda i: (i, P_DT // 1024)),
                  _pspec((1, 1024)), _pspec((1, 1024)), _pspec((1, 1024))],
        out_specs=[pl.BlockSpec((SSD_CHUNK, 1024), lambda i: (i, 0)), pl.BlockSpec((STATE_ROWS, LANES), lambda i: (i, 0))],
        out_shape=[jax.ShapeDtypeStruct((s, 1024), F32), jax.ShapeDtypeStruct((nc * STATE_ROWS, LANES), F32)],
        scratch_shapes=[pltpu.VMEM((STATE_ROWS, LANES), F32)],
        compiler_params=_cp("arbitrary"), name="ssd_fwd")(xbc, proj, dtb, alog, dsk)


def _ssd_bwd(xbc, proj, hist, dtb, alog, dsk, dy):
    s = xbc.shape[0]
    nc = s // SSD_CHUNK

    def body(x_ref, dt_ref, hist_ref, b_ref, a_ref, d_ref, dy_ref, dx_ref, ddt_ref, db_ref, da_ref, dd_ref, dh_ref):
        first = pl.program_id(0) == 0

        @pl.when(first)
        def _():
            dh_ref[...] = jnp.zeros_like(dh_ref)

        _, vjp = jax.vjp(_ssd_chunk, x_ref[...], dt_ref[...], hist_ref[...], b_ref[...], a_ref[...], d_ref[...])
        dys = tuple(dy_ref[:, p * LANES:(p + 1) * LANES] for p in range(N_PAIRS))
        dhs = tuple(dh_ref[p * SSD_STATE:(p + 1) * SSD_STATE, :] for p in range(N_PAIRS))
        dx, ddt, dhp, db, da, dd = vjp((dys, dhs))
        dx_ref[...] = dx
        ddt_ref[...] = ddt
        dh_ref[...] = dhp
        for o, v in ((db_ref, db), (da_ref, da), (dd_ref, dd)):
            @pl.when(first)
            def _(o=o, v=v):
                o[...] = v

            @pl.when(jnp.logical_not(first))
            def _(o=o, v=v):
                o[...] += v

    rev = lambda i: (nc - 1 - i, 0)
    return pl.pallas_call(
        body, grid=(nc,),
        in_specs=[pl.BlockSpec((SSD_CHUNK, CONV_DIM), rev),
                  pl.BlockSpec((SSD_CHUNK, 1024), lambda i: (nc - 1 - i, P_DT // 1024)),
                  pl.BlockSpec((STATE_ROWS, LANES), rev),
                  _pspec((1, 1024)), _pspec((1, 1024)), _pspec((1, 1024)),
                  pl.BlockSpec((SSD_CHUNK, 1024), rev)],
        out_specs=[pl.BlockSpec((SSD_CHUNK, CONV_DIM), rev), pl.BlockSpec((SSD_CHUNK, 1024), rev),
                   _pspec((1, 1024)), _pspec((1, 1024)), _pspec((1, 1024))],
        out_shape=[jax.ShapeDtypeStruct((s, CONV_DIM), F32), jax.ShapeDtypeStruct((s, 1024), F32)]
        + [jax.ShapeDtypeStruct((1, 1024), F32)] * 3,
        scratch_shapes=[pltpu.VMEM((STATE_ROWS, LANES), F32)],
        compiler_params=_cp("arbitrary"), name="ssd_bwd")(xbc, proj, hist, dtb, alog, dsk, dy)


def _attn_block(q, kp, kc, vp, vc, has_prev, gi):
    d = DILATIONS[gi]
    scale = ATTN_HEAD_DIM ** -0.5
    n = ATTN_STEPS
    qi = lax.broadcasted_iota(jnp.int32, (n, n), 0)
    kj = lax.broadcasted_iota(jnp.int32, (n, n), 1)
    valid_p = (kj >= qi) & has_prev
    valid_c = kj <= qi
    rel_p = (qi + n - kj).astype(F32)
    rel_c = (qi - kj).astype(F32)
    outs, lses = [], []
    for h in range(4):
        sl = slice(h * ATTN_HEAD_DIM, (h + 1) * ATTN_HEAD_DIM)
        slope = float(SLOPES[gi, h]) * d
        sp = jnp.where(valid_p, _bdot(q[:, sl], kp[:, sl], "nt") * scale - slope * rel_p, NEG)
        sc = jnp.where(valid_c, _bdot(q[:, sl], kc[:, sl], "nt") * scale - slope * rel_c, NEG)
        m = lax.stop_gradient(jnp.maximum(jnp.max(sp, axis=1, keepdims=True), jnp.max(sc, axis=1, keepdims=True)))
        pp, pc = jnp.exp(sp - m), jnp.exp(sc - m)
        den = jnp.sum(pp, axis=1, keepdims=True) + jnp.sum(pc, axis=1, keepdims=True)
        outs.append((_bdot(pp, vp[:, sl], "nn") + _bdot(pc, vc[:, sl], "nn")) / den)
        lses.append(jnp.broadcast_to(m + jnp.log(den), (n, ATTN_HEAD_DIM)))
    return tuple(outs), tuple(lses)


def _dilate(t, d):
    return t if d == 1 else t.reshape(t.shape[0] // d, d * t.shape[1])


def _undilate(t, d):
    return t if d == 1 else t.reshape(t.shape[0] * d, t.shape[1] // d)


def _attn_fwd(qn, kn, vv, gi):
    d = DILATIONS[gi]
    s = qn.shape[0]
    nb = s // d // ATTN_STEPS
    W = ATTN_GROUP_W
    q_d, k_d, v_d = _dilate(qn, d), _dilate(kn, d), _dilate(vv, d)

    def body(q_ref, kp_ref, kc_ref, vp_ref, vc_ref, o_ref, l_ref):
        outs, lses = _attn_block(q_ref[...], kp_ref[...], kc_ref[...], vp_ref[...], vc_ref[...], pl.program_id(1) > 0, gi)
        for h in range(4):
            o_ref[:, h * ATTN_HEAD_DIM:(h + 1) * ATTN_HEAD_DIM] = outs[h]
            l_ref[:, h * ATTN_HEAD_DIM:(h + 1) * ATTN_HEAD_DIM] = lses[h]

    cur = pl.BlockSpec((ATTN_STEPS, W), lambda r, b: (b, r * 3 + gi))
    prev = pl.BlockSpec((ATTN_STEPS, W), lambda r, b: (jnp.maximum(b - 1, 0), r * 3 + gi))
    out = pl.BlockSpec((ATTN_STEPS, W), lambda r, b: (b, r))
    o, l = pl.pallas_call(
        body, grid=(d, nb), in_specs=[cur, prev, cur, prev, cur], out_specs=[out, out],
        out_shape=[jax.ShapeDtypeStruct((s // d, d * W), F32)] * 2,
        compiler_params=_cp("parallel", "parallel"), name=f"attn_fwd_g{gi}")(q_d, k_d, k_d, v_d, v_d)
    return _undilate(o, d), _undilate(l, d)


def _attn_bwd(qn, kn, vv, do, dl, gi):
    d = DILATIONS[gi]
    s = qn.shape[0]
    nb = s // d // ATTN_STEPS
    W = ATTN_GROUP_W
    q_d, k_d, v_d, do_d, dl_d = _dilate(qn, d), _dilate(kn, d), _dilate(vv, d), _dilate(do, d), _dilate(dl, d)

    def body(q_ref, kp_ref, kc_ref, vp_ref, vc_ref, do_ref, dl_ref, dq_ref, dk_ref, dv_ref, ck_ref, cv_ref):
        bi = pl.program_id(1)

        @pl.when(bi == 0)
        def _():
            ck_ref[...] = jnp.zeros_like(ck_ref)
            cv_ref[...] = jnp.zeros_like(cv_ref)

        has_prev = bi < nb - 1
        f = functools.partial(_attn_block, has_prev=has_prev, gi=gi)
        _, vjp = jax.vjp(f, q_ref[...], kp_ref[...], kc_ref[...], vp_ref[...], vc_ref[...])
        hs = [slice(h * ATTN_HEAD_DIM, (h + 1) * ATTN_HEAD_DIM) for h in range(4)]
        dq, dkp, dkc, dvp, dvc = vjp((tuple(do_ref[:, sl] for sl in hs), tuple(dl_ref[:, sl] for sl in hs)))
        dq_ref[...] = dq
        dk_ref[...] = dkc + ck_ref[...]
        dv_ref[...] = dvc + cv_ref[...]
        ck_ref[...] = dkp
        cv_ref[...] = dvp

    cur = pl.BlockSpec((ATTN_STEPS, W), lambda r, b: (nb - 1 - b, r * 3 + gi))
    prev = pl.BlockSpec((ATTN_STEPS, W), lambda r, b: (jnp.maximum(nb - 2 - b, 0), r * 3 + gi))
    out = pl.BlockSpec((ATTN_STEPS, W), lambda r, b: (nb - 1 - b, r))
    res = pl.pallas_call(
        body, grid=(d, nb), in_specs=[cur, prev, cur, prev, cur, out, out], out_specs=[out, out, out],
        out_shape=[jax.ShapeDtypeStruct((s // d, d * W), F32)] * 3,
        scratch_shapes=[pltpu.VMEM((ATTN_STEPS, W), F32)] * 2,
        compiler_params=_cp("parallel", "arbitrary"), name=f"attn_bwd_g{gi}")(q_d, k_d, k_d, v_d, v_d, do_d, dl_d)
    return [_undilate(t, d) for t in res]


def _layer_fwd(x, mod, W):
    sh1, sc1, g1, sh2, sc2, g2 = (mod[i:i + 1] for i in range(6))
    (h,) = _rowwise_fwd("norm1", _f_norm, [(x, 1024, 0)], [W["norm1_w"], sc1, sh1], [(1024, BF16)])
    proj = _matmul("in_proj", h, W["w_in"], "nn")
    xbc = _conv_fwd(proj, W["conv_w"], W["conv_b"])
    y, hist = _ssd_fwd(xbc, proj, W["dt_bias"], W["a_log"], W["d_skip"])
    (yn,) = _rowwise_fwd("ssd_gate", _f_ssdgate, [(y, 1024, 0), (proj, 1024, P_Z // 1024)], [W["ssd_norm_w"]], [(1024, BF16)])
    y_ssd = _matmul("ssd_out", yn, W["w_ssd_out"], "nn")
    qn, kn, vv = _rowwise_fwd("qk_norm", _f_qknorm, [(proj, 1536, P_Q // 1536), (proj, 1536, P_K // 1536), (proj, 1536, P_V // 1536)],
                              [W["q_norm_w"], W["k_norm_w"]], [(1536, F32)] * 3)
    ol = [_attn_fwd(qn, kn, vv, gi) for gi in range(3)]
    (o,) = _rowwise_fwd("attn_combine", _f_combine, [(t[0], 512, 0) for t in ol] + [(t[1], 512, 0) for t in ol], [], [(512, BF16)])
    y_attn = _matmul("attn_out", o, W["w_attn_out"], "nn")
    pooled = _pool_fwd(proj)
    pm = _group_matmul("pool_mix", pooled, W["w_pool_mix"], "nn")
    (ps,) = _rowwise_fwd("pool_scale", _f_poolscale, [(pm, 1024, 0)], [W["pool_scale"]], [(1024, BF16)])
    y_pool = _matmul("pool_out", ps, W["w_pool_out"], "nn")
    (merged,) = _rowwise_fwd("merge", _f_merge, [(proj, 3072, P_GATES // 3072), (y_ssd, 1024, 0), (y_attn, 1024, 0), (y_pool, 1024, 0)],
                             [], [(1024, BF16)])
    mo = _matmul("mix_out", merged, W["w_out"], "nn")
    (x1,) = _rowwise_fwd("resid1", _f_resid, [(x, 1024, 0), (mo, 1024, 0)], [g1], [(1024, F32)])
    (h2,) = _rowwise_fwd("norm2", _f_norm, [(x1, 1024, 0)], [W["norm2_w"], sc2, sh2], [(1024, BF16)])
    a = _matmul("ff1", h2, W["w_ff1"], "nn")
    (r,) = _rowwise_fwd("relu2", _f_relu2, [(a, D_FF, 0)], [], [(D_FF, BF16)], tb=128)
    ff = _matmul("ff2", r, W["w_ff2"], "nn")
    (x2,) = _rowwise_fwd("resid2", _f_resid, [(x1, 1024, 0), (ff, 1024, 0)], [g2], [(1024, F32)])
    saved = dict(x=x, h=h, proj=proj, xbc=xbc, y=y, hist=hist, yn=yn, y_ssd=y_ssd, qn=qn, kn=kn, vv=vv, ol=ol, o=o,
                 y_attn=y_attn, pooled=pooled, pm=pm, ps=ps, y_pool=y_pool, merged=merged, mo=mo, x1=x1, h2=h2, a=a, r=r, ff=ff)
    return x2, saved


def _layer_bwd(dx2, mod, W, sv):
    sh1, sc1, g1, sh2, sc2, g2 = (mod[i:i + 1] for i in range(6))
    g = {}
    dx1a, dff, dg2 = _rowwise_bwd("resid2_bwd", _f_resid, [(sv["x1"], 1024, 0), (sv["ff"], 1024, 0)], [g2], [(dx2, 1024, 0)], [True, True])
    g["w_ff2"] = _matmul("ff2_dw", sv["r"], dff, "tn")
    dr = _matmul("ff2_dx", dff, W["w_ff2"], "nt")
    (da,) = _rowwise_bwd("relu2_bwd", _f_relu2, [(sv["a"], D_FF, 0)], [], [(dr, D_FF, 0)], [True], tb=128)
    g["w_ff1"] = _matmul("ff1_dw", sv["h2"], da, "tn")
    dh2 = _matmul("ff1_dx", da, W["w_ff1"], "nt")
    dx1, g["norm2_w"], dsc2, dsh2 = _rowwise_bwd("norm2_bwd", _f_norm, [(sv["x1"], 1024, 0)], [W["norm2_w"], sc2, sh2],
                                                 [(dh2, 1024, 0)], [True], add=(0, dx1a))
    dxa, dmo, dg1 = _rowwise_bwd("resid1_bwd", _f_resid, [(sv["x"], 1024, 0), (sv["mo"], 1024, 0)], [g1], [(dx1, 1024, 0)], [True, True])
    g["w_out"] = _matmul("mix_out_dw", sv["merged"], dmo, "tn")
    dmerged = _matmul("mix_out_dx", dmo, W["w_out"], "nt")
    proj = sv["proj"]
    dgates, dy_ssd, dy_attn, dy_pool = _rowwise_bwd(
        "merge_bwd", _f_merge, [(proj, 3072, P_GATES // 3072), (sv["y_ssd"], 1024, 0), (sv["y_attn"], 1024, 0), (sv["y_pool"], 1024, 0)],
        [], [(dmerged, 1024, 0)], [True] * 4)
    g["w_pool_out"] = _matmul("pool_out_dw", sv["ps"], dy_pool, "tn")
    dps = _matmul("pool_out_dx", dy_pool, W["w_pool_out"], "nt")
    dpm, g["pool_scale"] = _rowwise_bwd("pool_scale_bwd", _f_poolscale, [(sv["pm"], 1024, 0)], [W["pool_scale"]], [(dps, 1024, 0)], [True])
    g["w_pool_mix"] = _group_matmul("pool_mix_dw", sv["pooled"], dpm, "tn")
    dpooled = _group_matmul("pool_mix_dx", dpm, W["w_pool_mix"], "nt")
    du = _pool_bwd(dpooled)
    g["w_attn_out"] = _matmul("attn_out_dw", sv["o"], dy_attn, "tn")
    do = _matmul("attn_out_dx", dy_attn, W["w_attn_out"], "nt")
    ol = sv["ol"]
    dol = _rowwise_bwd("attn_combine_bwd", _f_combine, [(t[0], 512, 0) for t in ol] + [(t[1], 512, 0) for t in ol], [],
                       [(do, 512, 0)], [True] * 6)
    dqs, dks, dvs = zip(*[_attn_bwd(sv["qn"], sv["kn"], sv["vv"], dol[gi], dol[3 + gi], gi) for gi in range(3)])
    dqn, dkn, dvv = (jnp.concatenate(t, axis=1) for t in (dqs, dks, dvs))
    dq, dk, dv, g["q_norm_w"], g["k_norm_w"] = _rowwise_bwd(
        "qk_norm_bwd", _f_qknorm, [(proj, 1536, P_Q // 1536), (proj, 1536, P_K // 1536), (proj, 1536, P_V // 1536)],
        [W["q_norm_w"], W["k_norm_w"]], [(dqn, 1536, 0), (dkn, 1536, 0), (dvv, 1536, 0)], [True] * 3)
    g["w_ssd_out"] = _matmul("ssd_out_dw", sv["yn"], dy_ssd, "tn")
    dyn = _matmul("ssd_out_dx", dy_ssd, W["w_ssd_out"], "nt")
    dy, dz, g["ssd_norm_w"] = _rowwise_bwd("ssd_gate_bwd", _f_ssdgate, [(sv["y"], 1024, 0), (proj, 1024, P_Z // 1024)], [W["ssd_norm_w"]],
                                           [(dyn, 1024, 0)], [True, True])
    dxbc, ddt, g["dt_bias"], g["a_log"], g["d_skip"] = _ssd_bwd(sv["xbc"], proj, sv["hist"], W["dt_bias"], W["a_log"], W["d_skip"], dy)
    dxbc_raw, g["conv_w"], g["conv_b"] = _conv_bwd(proj, W["conv_w"], W["conv_b"], dxbc)
    dproj = jnp.concatenate([dxbc_raw, dq, dk, dv, dgates, dz, du, ddt], axis=1)
    g["w_in"] = _matmul("in_proj_dw", sv["h"], dproj, "tn")
    dh = _matmul("in_proj_dx", dproj, W["w_in"], "nt")
    dx, g["norm1_w"], dsc1, dsh1 = _rowwise_bwd("norm1_bwd", _f_norm, [(sv["x"], 1024, 0)], [W["norm1_w"], sc1, sh1],
                                                [(dh, 1024, 0)], [True], add=(0, dxa))
    dmod = jnp.concatenate([dsh1, dsc1, dg1, dsh2, dsc2, dg2], axis=0)
    return dx, dmod, g


def _expand_heads(t):
    return jnp.repeat(t, SSD_HEAD_DIM, axis=-1)


def _reduce_heads(t):
    return t.reshape(t.shape[:-1] + (SSD_HEADS, SSD_HEAD_DIM)).sum(-1)


_IN_SPLITS = np.cumsum((0,) + IN_SIZES)


def _w_in_to_layout(w):
    z, xbc, dt, q, k, v, u, gates = (w[:, _IN_SPLITS[i]:_IN_SPLITS[i + 1]] for i in range(8))
    return jnp.concatenate([xbc, q, k, v, gates, z, u, _expand_heads(dt)], axis=1)


def _w_in_from_layout(g):
    xbc, q, k, v = (g[:, o:o + 1536] for o in (P_XBC, P_Q, P_K, P_V))
    gates, z, u, dt = g[:, P_GATES:P_GATES + 3072], g[:, P_Z:P_Z + 1024], g[:, P_U:P_U + 1024], g[:, P_DT:P_DT + 1024]
    return jnp.concatenate([z, xbc, _reduce_heads(dt), q, k, v, u, gates], axis=1)


_MATS = ("w_in", "w_ssd_out", "w_attn_out", "w_pool_mix", "w_pool_out", "w_out", "w_ff1", "w_ff2")
_ROWS = ("norm1_w", "norm2_w", "conv_b", "ssd_norm_w", "q_norm_w", "k_norm_w", "pool_scale")
_HEAD_ROWS = ("dt_bias", "a_log", "d_skip")


def _layer_weights(full, l):
    W = {k: full[k][l].astype(BF16) for k in _MATS if k != "w_in"}
    W["w_in"] = _w_in_to_layout(full["w_in"][l]).astype(BF16)
    W["conv_w"] = full["conv_w"][l].astype(F32)
    for k in _ROWS:
        W[k] = full[k][l][None, :]
    for k in _HEAD_ROWS:
        W[k] = _expand_heads(full[k][l])[None, :]
    return W


def _layer_grads_to_reference_layout(g):
    out = dict(g)
    out["w_in"] = _w_in_from_layout(g["w_in"])
    for k in _ROWS:
        out[k] = g[k][0]
    for k in _HEAD_ROWS:
        out[k] = _reduce_heads(g[k][0])
    return out


def _local_step(x, tgt, mods, full):
    saved, Ws = [], []
    for l in range(DEPTH):
        W = _layer_weights(full, l)
        x, sv = _layer_fwd(x, mods[l], W)
        saved.append(sv)
        Ws.append(W)
    dx, loss = _loss_and_grad(x, tgt)
    dmods, grads = [None] * DEPTH, [None] * DEPTH
    for l in reversed(range(DEPTH)):
        dx, dmods[l], g = _layer_bwd(dx, mods[l], Ws[l], saved[l])
        grads[l] = _layer_grads_to_reference_layout(g)
    return loss, dx, jnp.stack(dmods), grads


ANY = pl.BlockSpec(memory_space=pl.ANY)


def _place():
    x, y, c = lax.axis_index("x"), lax.axis_index("y"), lax.axis_index("c")
    return x, y, c, (x, y, 1 - c), [(1 - x, y), (x, 1 - y), (1 - x, 1 - y)]


def _allgather8(name, blk):
    m_per, n = blk.shape

    def body(x_ref, out_ref, send_sems, recv_sems, local_sem):
        x, y, c, sibling, chips = _place()
        me = (x, y, c)

        def rows(px, py, pc):
            return out_ref.at[pl.ds((4 * px + 2 * py + pc) * m_per, m_per), :]

        def copy(k, block, to, src=None):
            return pltpu.make_async_remote_copy(
                src_ref=rows(*block) if src is None else src, dst_ref=rows(*block),
                send_sem=send_sems.at[k], recv_sem=recv_sems.at[k], device_id=to, device_id_type=MESH)

        mine = pltpu.make_async_copy(x_ref, rows(*me), local_sem)
        mine.start()
        first = [copy(0, me, sibling, src=x_ref)]
        first += [copy(1 + j, me, (*chip, c), src=x_ref) for j, chip in enumerate(chips)]
        for cp in first:
            cp.start()
        passed = [copy(4 + j, (*chip, c), sibling) for j, chip in enumerate(chips)]
        for j, chip in enumerate(chips):
            copy(1 + j, (*chip, c), me).wait_recv()
            passed[j].start()
        copy(0, sibling, me).wait_recv()
        for j, chip in enumerate(chips):
            copy(4 + j, (*chip, 1 - c), me).wait_recv()
        for cp in first + passed:
            cp.wait_send()
        mine.wait()

    return pl.pallas_call(
        body, out_shape=jax.ShapeDtypeStruct((N_DEV * m_per, n), blk.dtype),
        in_specs=[pl.BlockSpec(memory_space=pltpu.VMEM)], out_specs=pl.BlockSpec(memory_space=pltpu.VMEM),
        scratch_shapes=[pltpu.SemaphoreType.DMA((7,)), pltpu.SemaphoreType.DMA((7,)), pltpu.SemaphoreType.DMA],
        name=name)(blk)


def _gather_weights(pack):
    _, rh, cw = pack.shape

    def body(p_ref, out_ref, send_sems, recv_sems, local_sem):
        x, y, c, sibling, chips = _place()

        def slot(px, py, h):
            return out_ref.at[2 * px + py, h]

        def copy(k, block, to, src=None):
            return pltpu.make_async_remote_copy(
                src_ref=slot(*block) if src is None else src, dst_ref=slot(*block),
                send_sem=send_sems.at[k], recv_sem=recv_sems.at[k], device_id=to, device_id_type=MESH)

        mine = pltpu.make_async_copy(p_ref, out_ref.at[2 * x + y], local_sem)
        mine.start()
        first = [copy(j, (x, y, c), (*chip, c), src=p_ref.at[c]) for j, chip in enumerate(chips)]
        for cp in first:
            cp.start()
        passed = [copy(3 + j, (*chip, c), sibling) for j, chip in enumerate(chips)]
        for j, chip in enumerate(chips):
            copy(j, (*chip, c), (x, y, c)).wait_recv()
            passed[j].start()
        for j, chip in enumerate(chips):
            copy(3 + j, (*chip, 1 - c), (x, y, c)).wait_recv()
        for cp in first + passed:
            cp.wait_send()
        mine.wait()

    return pl.pallas_call(
        body, out_shape=jax.ShapeDtypeStruct((N_CHIPS, 2, rh, cw), pack.dtype), in_specs=[ANY], out_specs=ANY,
        scratch_shapes=[pltpu.SemaphoreType.DMA((6,)), pltpu.SemaphoreType.DMA((6,)), pltpu.SemaphoreType.DMA],
        name="gather_weights")(pack)


def _swap_halves(g):
    _, nch, rh, cw = g.shape

    def body(g_ref, out_ref, send_sem, recv_sem):
        x, y, c, sibling, _ = _place()
        cp = pltpu.make_async_remote_copy(src_ref=g_ref.at[1 - c], dst_ref=out_ref, send_sem=send_sem, recv_sem=recv_sem,
                                          device_id=sibling, device_id_type=MESH)
        cp.start()
        cp.wait()

    return pl.pallas_call(
        body, out_shape=jax.ShapeDtypeStruct((nch, rh, cw), g.dtype), in_specs=[ANY], out_specs=ANY,
        scratch_shapes=[pltpu.SemaphoreType.DMA, pltpu.SemaphoreType.DMA], name="grad_swap_halves")(g)


def _exchange_chips(p):
    nch, rh, cw = p.shape

    def body(p_ref, out_ref, send_sems, recv_sems, local_sem):
        x, y, c, _, chips = _place()
        mine = pltpu.make_async_copy(p_ref.at[2 * x + y], out_ref.at[3], local_sem)
        mine.start()
        cps = [pltpu.make_async_remote_copy(src_ref=p_ref.at[2 * cx + cy], dst_ref=out_ref.at[k], send_sem=send_sems.at[k],
                                            recv_sem=recv_sems.at[k], device_id=(cx, cy, c), device_id_type=MESH)
               for k, (cx, cy) in enumerate(chips)]
        for cp in cps:
            cp.start()
        for cp in cps:
            cp.wait_recv()
        for cp in cps:
            cp.wait_send()
        mine.wait()

    return pl.pallas_call(
        body, out_shape=jax.ShapeDtypeStruct((nch, rh, cw), p.dtype), in_specs=[ANY], out_specs=ANY,
        scratch_shapes=[pltpu.SemaphoreType.DMA((3,)), pltpu.SemaphoreType.DMA((3,)), pltpu.SemaphoreType.DMA],
        name="grad_exchange_chips")(p)


def _share_halves(t):
    rh, cw = t.shape

    def body(t_ref, out_ref, send_sem, recv_sem, local_sem):
        x, y, c, sibling, _ = _place()
        mine = pltpu.make_async_copy(t_ref, out_ref.at[c], local_sem)
        mine.start()
        send = pltpu.make_async_remote_copy(src_ref=t_ref, dst_ref=out_ref.at[c], send_sem=send_sem, recv_sem=recv_sem,
                                            device_id=sibling, device_id_type=MESH)
        send.start()
        pltpu.make_async_remote_copy(src_ref=t_ref, dst_ref=out_ref.at[1 - c], send_sem=send_sem, recv_sem=recv_sem,
                                     device_id=sibling, device_id_type=MESH).wait_recv()
        send.wait_send()
        mine.wait()

    return pl.pallas_call(
        body, out_shape=jax.ShapeDtypeStruct((2, rh, cw), t.dtype), in_specs=[ANY], out_specs=ANY,
        scratch_shapes=[pltpu.SemaphoreType.DMA, pltpu.SemaphoreType.DMA, pltpu.SemaphoreType.DMA],
        name="grad_share_halves")(t)


PACK_W = 1024
PACK_TB = 512


def _sum_rows(name, parts, out_dtype):
    def f(*vals):
        acc = vals[0]
        for v in vals[1:]:
            acc = acc + v
        return (acc,)

    return _rowwise_fwd(name, f, [(p, PACK_W, 0) for p in parts], [], [(PACK_W, out_dtype)], tb=_tile(parts[0].shape[0], PACK_TB))[0]


def _adamw(name, w, g, m, v):
    r, cw = w.shape
    tb = r
    while tb * cw > 400_000 and tb % 16 == 0:
        tb //= 2
    c1 = 1.0 / (1.0 - ADAM_B1 ** ADAM_STEP)
    c2 = 1.0 / (1.0 - ADAM_B2 ** ADAM_STEP)

    def body(w_ref, g_ref, m_ref, v_ref, d_ref, mo_ref, vo_ref):
        gg = g_ref[...]
        mn = ADAM_B1 * m_ref[...] + (1.0 - ADAM_B1) * gg
        vn = ADAM_B2 * v_ref[...] + (1.0 - ADAM_B2) * jnp.square(gg)
        d_ref[...] = -ADAM_LR * ((mn * c1) / (jnp.sqrt(vn * c2) + ADAM_EPS) + ADAM_WD * w_ref[...])
        mo_ref[...] = mn
        vo_ref[...] = vn

    spec = pl.BlockSpec((tb, cw), lambda i: (i, 0))
    return pl.pallas_call(
        body, grid=(r // tb,), in_specs=[spec] * 4, out_specs=[spec] * 3,
        out_shape=[jax.ShapeDtypeStruct((r, cw), F32)] * 3, compiler_params=_cp("parallel"), name=name)(w, g, m, v)


def _silu_rows(c):
    def body(c_ref, o_ref):
        rows = lax.broadcasted_iota(jnp.int32, o_ref.shape, 0)
        o_ref[...] = jnp.where(rows == 0, jnp.broadcast_to(_silu(c_ref[...]), o_ref.shape), 0.0)

    return pl.pallas_call(body, out_shape=jax.ShapeDtypeStruct((8, c.shape[1]), F32), name="cond_silu")(c)


_SHARDED = (("w_in", 2), ("conv_w", 2), ("w_ssd_out", 1), ("w_attn_out", 2), ("w_pool_mix", 2), ("w_pool_out", 1),
            ("w_out", 1), ("w_ff1", 2), ("w_ff2", 1))
_SMALL = ("b_ada", "norm1_w", "norm2_w", "conv_b", "dt_bias", "a_log", "d_skip", "ssd_norm_w", "q_norm_w", "k_norm_w",
          "pool_scale")
_ORDER = ("w_ada", "b_ada", "norm1_w", "norm2_w", "w_in", "conv_w", "conv_b", "dt_bias", "a_log", "d_skip", "ssd_norm_w",
          "w_ssd_out", "q_norm_w", "k_norm_w", "w_attn_out", "w_pool_mix", "pool_scale", "w_pool_out", "w_out", "w_ff1", "w_ff2")
PACK_RH = 11776


def _pack_flat(arrs, rows, dtype):
    flat = jnp.concatenate([a.reshape(-1).astype(dtype) for a in arrs])
    return jnp.pad(flat, (0, rows * PACK_W - flat.shape[0])).reshape(rows, PACK_W)


def _unpack_flat(buf, shapes):
    flat = buf.reshape(-1)
    out, off = [], 0
    for shp in shapes:
        n = int(np.prod(shp))
        out.append(flat[off:off + n].reshape(shp))
        off += n
    return out


def _small_rows(n_elems):
    return -(-n_elems // (8 * PACK_W)) * 8


def kernel(x, c, w_ada, b_ada, norm1_w, norm2_w, w_in, conv_w, conv_b, dt_bias, a_log, d_skip, ssd_norm_w, w_ssd_out, q_norm_w, k_norm_w, w_attn_out, w_pool_mix, pool_scale, w_pool_out, w_out, w_ff1, w_ff2, loss_target, m_w_ada, m_b_ada, m_norm1_w, m_norm2_w, m_w_in, m_conv_w, m_conv_b, m_dt_bias, m_a_log, m_d_skip, m_ssd_norm_w, m_w_ssd_out, m_q_norm_w, m_k_norm_w, m_w_attn_out, m_w_pool_mix, m_pool_scale, m_w_pool_out, m_w_out, m_w_ff1, m_w_ff2, v_w_ada, v_b_ada, v_norm1_w, v_norm2_w, v_w_in, v_conv_w, v_conv_b, v_dt_bias, v_a_log, v_d_skip, v_ssd_norm_w, v_w_ssd_out, v_q_norm_w, v_k_norm_w, v_w_attn_out, v_w_pool_mix, v_pool_scale, v_w_pool_out, v_w_out, v_w_ff1, v_w_ff2):
    w = dict(w_ada=w_ada, b_ada=b_ada, norm1_w=norm1_w, norm2_w=norm2_w, w_in=w_in, conv_w=conv_w, conv_b=conv_b, dt_bias=dt_bias, a_log=a_log, d_skip=d_skip, ssd_norm_w=ssd_norm_w, w_ssd_out=w_ssd_out, q_norm_w=q_norm_w, k_norm_w=k_norm_w, w_attn_out=w_attn_out, w_pool_mix=w_pool_mix, pool_scale=pool_scale, w_pool_out=w_pool_out, w_out=w_out, w_ff1=w_ff1, w_ff2=w_ff2)
    m = dict(w_ada=m_w_ada, b_ada=m_b_ada, norm1_w=m_norm1_w, norm2_w=m_norm2_w, w_in=m_w_in, conv_w=m_conv_w, conv_b=m_conv_b, dt_bias=m_dt_bias, a_log=m_a_log, d_skip=m_d_skip, ssd_norm_w=m_ssd_norm_w, w_ssd_out=m_w_ssd_out, q_norm_w=m_q_norm_w, k_norm_w=m_k_norm_w, w_attn_out=m_w_attn_out, w_pool_mix=m_w_pool_mix, pool_scale=m_pool_scale, w_pool_out=m_w_pool_out, w_out=m_w_out, w_ff1=m_w_ff1, w_ff2=m_w_ff2)
    v = dict(w_ada=v_w_ada, b_ada=v_b_ada, norm1_w=v_norm1_w, norm2_w=v_norm2_w, w_in=v_w_in, conv_w=v_conv_w, conv_b=v_conv_b, dt_bias=v_dt_bias, a_log=v_a_log, d_skip=v_d_skip, ssd_norm_w=v_ssd_norm_w, w_ssd_out=v_w_ssd_out, q_norm_w=v_q_norm_w, k_norm_w=v_k_norm_w, w_attn_out=v_w_attn_out, w_pool_mix=v_w_pool_mix, pool_scale=v_pool_scale, w_pool_out=v_w_pool_out, w_out=v_w_out, w_ff1=v_w_ff1, w_ff2=v_w_ff2)
    chip = 2 * lax.axis_index("x") + lax.axis_index("y")
    dev = 2 * chip + lax.axis_index("c")
    ada_cols = w_ada.shape[2]

    cond_all = _allgather8("gather_cond", _silu_rows(c))[::8]
    b_cols = lax.dynamic_slice_in_dim(b_ada, chip * ada_cols, ada_cols, axis=1)
    mod_cols = jnp.stack([_matmul("ada_fwd", cond_all, w_ada[l], "nn", precise=True) + b_cols[l][None, :] for l in range(DEPTH)])
    mod_all = _allgather8("gather_mod", mod_cols.reshape(-1, PACK_W)).reshape(N_DEV, DEPTH, N_DEV, ada_cols)
    mine = lax.dynamic_index_in_dim(mod_all[0::2], dev, axis=2, keepdims=False)
    mods = jnp.moveaxis(mine, 0, 1).reshape(DEPTH, 6, D_MODEL)

    shard_shapes = [w[k].shape for k, _ in _SHARDED]
    pack = _pack_flat([w[k] for k, _ in _SHARDED], 2 * PACK_RH, BF16).reshape(2, PACK_RH, PACK_W)
    gathered = _gather_weights(pack).reshape(N_CHIPS, 2 * PACK_RH, PACK_W)
    per_chip = [_unpack_flat(gathered[j], shard_shapes) for j in range(N_CHIPS)]
    full = {k: jnp.concatenate([per_chip[j][i] for j in range(N_CHIPS)], axis=ax) for i, (k, ax) in enumerate(_SHARDED)}
    for k in _SMALL[1:]:
        full[k] = w[k]

    loss, grad_x, dmods, grads = _local_step(x[0], loss_target[0], mods, full)

    small = [dmods] + [jnp.stack([grads[l][k] for l in range(DEPTH)]) for k in _SMALL[1:]] + [loss[:, :1]]
    n_small = sum(int(np.prod(a.shape)) for a in small)
    rows_small = _small_rows(n_small)
    small_all = _allgather8("gather_small", _pack_flat(small, rows_small, F32))
    parts = [small_all[d * rows_small:(d + 1) * rows_small] for d in range(N_DEV)]
    small_sum = _unpack_flat(_sum_rows("sum_small", parts, F32), [a.shape for a in small])
    g_out = {"b_ada": small_sum[0].reshape(DEPTH, 6 * D_MODEL)}
    for k, t in zip(_SMALL[1:], small_sum[1:-1]):
        g_out[k] = t
    loss_out = small_sum[-1][0, 0]
    dmod_all = jnp.stack([p[:DEPTH * 6].reshape(DEPTH, 6 * D_MODEL) for p in parts])
    dmod_cols = lax.dynamic_slice_in_dim(dmod_all, chip * ada_cols, ada_cols, axis=2)
    g_out["w_ada"] = jnp.stack([_matmul("ada_dw", cond_all, dmod_cols[:, l], "tn", precise=True) for l in range(DEPTH)])

    def chip_pack(j):
        pieces = []
        for k, ax in _SHARDED:
            t = jnp.stack([grads[l][k] for l in range(DEPTH)])
            n = t.shape[ax] // N_CHIPS
            pieces.append(lax.slice_in_dim(t, j * n, (j + 1) * n, axis=ax))
        return _pack_flat(pieces, 2 * PACK_RH, BF16).reshape(2, PACK_RH, PACK_W)

    gp = jnp.stack([chip_pack(j) for j in range(N_CHIPS)], axis=1)
    from_sibling = _swap_halves(gp)
    mine_half = lax.dynamic_index_in_dim(gp, lax.axis_index("c"), axis=0, keepdims=False)
    pair = _sum_rows("sum_pair", [mine_half.reshape(-1, PACK_W), from_sibling.reshape(-1, PACK_W)], BF16)
    partials = _exchange_chips(pair.reshape(N_CHIPS, PACK_RH, PACK_W))
    total = _sum_rows("sum_chips", [partials[3], partials[0], partials[1], partials[2]], F32)
    g_shard = _unpack_flat(_share_halves(total), shard_shapes)
    for (k, _), t in zip(_SHARDED, g_shard):
        g_out[k] = t

    deltas, new_m, new_v = {}, {}, {}
    for k in ("w_ada",) + tuple(k for k, _ in _SHARDED):
        shp = w[k].shape
        two_d = (int(np.prod(shp[:-1])), shp[-1])
        res = _adamw("adamw_" + k, *(t.reshape(two_d) for t in (w[k], g_out[k], m[k], v[k])))
        deltas[k], new_m[k], new_v[k] = (t.reshape(shp) for t in res)
    small_shapes = [w[k].shape for k in _SMALL]
    n_sm = sum(int(np.prod(s)) for s in small_shapes)
    res = _adamw("adamw_small", *[_pack_flat([t[k] for k in _SMALL], _small_rows(n_sm), F32) for t in (w, g_out, m, v)])
    for name_map, buf in zip((deltas, new_m, new_v), res):
        for k, t in zip(_SMALL, _unpack_flat(buf, small_shapes)):
            name_map[k] = t

    return (loss_out, grad_x[None], *[g_out[k] for k in _ORDER], *[deltas[k] for k in _ORDER],
            *[new_m[k] for k in _ORDER], *[new_v[k] for k in _ORDER])
```

```python
import functools
import math

import numpy as np
import jax
import jax.numpy as jnp
from jax import lax
from jax.experimental import pallas as pl
from jax.experimental.pallas import tpu as pltpu

F32, BF16 = jnp.float32, jnp.bfloat16
MESH = pl.DeviceIdType.MESH

D_MODEL = 1024
DEPTH = 4
N_CHIPS = 4
N_DEV = 8
SSD_HEADS = 16
SSD_HEAD_DIM = 64
SSD_STATE = 128
SSD_CHUNK = 128
SSD_CONV = 4
CONV_DIM = 1536
ATTN_HEAD_DIM = 128
ATTN_GROUP_W = 512
DILATIONS = (1, 4, 16)
ATTN_STEPS = 128
POOL_WINDOWS = (2, 4, 8, 16)
POOL_GW = 256
D_FF = 4096
EPS = 1e-6
IN_SIZES = (1024, 1536, 16, 1536, 1536, 1536, 1024, 3072)
IN_WIDTH = sum(IN_SIZES)
P_XBC, P_Q, P_K, P_V, P_GATES, P_Z, P_U, P_DT = 0, 1536, 3072, 4608, 6144, 9216, 10240, 11264
P_WIDTH = 12288
LANES = 128
NEG = -1e30
VMEM_LIMIT = 56 * 1024 * 1024

ADAM_LR, ADAM_B1, ADAM_B2, ADAM_EPS, ADAM_WD, ADAM_STEP = 0.001, 0.9, 0.999, 1e-08, 0.01, 10


def _alibi_slopes(n):
    def pow2(k):
        start = 2.0 ** (-8.0 / k)
        return [start ** (i + 1) for i in range(k)]
    if math.log2(n).is_integer():
        s = pow2(n)
    else:
        c = 2 ** math.floor(math.log2(n))
        s = pow2(c) + pow2(2 * c)[0::2][: n - c]
    return np.sort(np.asarray(s, np.float32))[::-1].copy()


SLOPES = _alibi_slopes(12).reshape(3, 4)


def _cp(*sem):
    return pltpu.CompilerParams(dimension_semantics=sem, vmem_limit_bytes=VMEM_LIMIT)


_DIMS = {"nn": (((1,), (0,)), ((), ())), "nt": (((1,), (1,)), ((), ())), "tn": (((0,), (0,)), ((), ()))}


def _dot(a, b, mode):
    return lax.dot_general(a.astype(BF16), b.astype(BF16), _DIMS[mode], preferred_element_type=F32)


@functools.partial(jax.custom_vjp, nondiff_argnums=(2,))
def _bdot(a, b, mode):
    return _dot(a, b, mode)


def _bdot_fwd(a, b, mode):
    return _dot(a, b, mode), (a, b)


def _bdot_bwd(mode, res, ct):
    a, b = res
    if mode == "nn":
        return _dot(ct, b, "nt"), _dot(a, ct, "tn")
    if mode == "nt":
        return _dot(ct, b, "nn"), _dot(ct, a, "tn")
    return _dot(b, ct, "nt"), _dot(a, ct, "nn")


_bdot.defvjp(_bdot_fwd, _bdot_bwd)


def _hdot(a, b):
    return jnp.dot(a, b, precision=lax.Precision.HIGHEST, preferred_element_type=F32)


def _tri(n, lower):
    r = lax.broadcasted_iota(jnp.int32, (n, n), 0)
    c = lax.broadcasted_iota(jnp.int32, (n, n), 1)
    return (r >= c if lower else r <= c).astype(F32)


@jax.custom_vjp
def _csum(a):
    return _hdot(_tri(a.shape[0], True), a)


def _csum_fwd(a):
    return _csum(a), None


def _csum_bwd(_, ct):
    return (_hdot(_tri(ct.shape[0], False), ct),)


_csum.defvjp(_csum_fwd, _csum_bwd)


def _softplus(x):
    return jnp.maximum(x, 0.0) + jnp.log(1.0 + jnp.exp(-jnp.abs(x)))


def _sigmoid(x):
    return 1.0 / (1.0 + jnp.exp(-x))


def _silu(x):
    return x * _sigmoid(x)


def _tile(n, cap):
    t = min(n, cap)
    while n % t:
        t //= 2
    return t


MM_TILE, MM_KTILE = 1024, 2048


def _matmul(name, a, b, mode, out_dtype=F32, precise=False, layer=None, chips=0, out_chips=0):
    if mode == "nn":
        (m, k), n = a.shape, (4 * chips if chips else b.shape[-1])
    elif mode == "nt":
        (m, k), n = a.shape, b.shape[-2]
    else:
        (k, m), n = a.shape, b.shape[-1]
    tm = _tile(m, MM_TILE)
    tn = _tile(chips if (chips and mode == "nn") else (out_chips or n), MM_TILE)
    tk = _tile(chips if (chips and mode == "nt") else k, MM_KTILE)
    nk = k // tk
    a_spec = pl.BlockSpec((tk, tm), lambda i, j, l: (l, i)) if mode == "tn" else pl.BlockSpec((tm, tk), lambda i, j, l: (i, l))
    if chips:
        if mode == "nn":
            per = chips // tn
            b_spec = pl.BlockSpec((None, None, tk, tn), lambda i, j, l: (layer, j // per, l, j % per))
        else:
            per = chips // tk
            b_spec = pl.BlockSpec((None, None, tn, tk), lambda i, j, l: (layer, l // per, j, l % per))
    elif layer is not None:
        b_spec = (pl.BlockSpec((None, tn, tk), lambda i, j, l: (layer, j, l)) if mode == "nt"
                  else pl.BlockSpec((None, tk, tn), lambda i, j, l: (layer, l, j)))
    else:
        b_spec = pl.BlockSpec((tn, tk), lambda i, j, l: (j, l)) if mode == "nt" else pl.BlockSpec((tk, tn), lambda i, j, l: (l, j))
    if out_chips:
        per_o = out_chips // tn
        o_spec = pl.BlockSpec((None, tm, tn), lambda i, j, l: (j // per_o, i, j % per_o))
        o_shape = jax.ShapeDtypeStruct((N_CHIPS, m, out_chips), out_dtype)
    else:
        o_spec = pl.BlockSpec((tm, tn), lambda i, j, l: (i, j))
        o_shape = jax.ShapeDtypeStruct((m, n), out_dtype)

    def part(a_ref, b_ref):
        if precise:
            return lax.dot_general(a_ref[...], b_ref[...], _DIMS[mode], precision=lax.Precision.HIGHEST,
                                   preferred_element_type=F32)
        return _dot(a_ref[...], b_ref[...], mode)

    if nk == 1:
        def body(a_ref, b_ref, o_ref):
            o_ref[...] = part(a_ref, b_ref).astype(o_ref.dtype)
        scratch = []
    else:
        def body(a_ref, b_ref, o_ref, acc_ref):
            l = pl.program_id(2)
            p = part(a_ref, b_ref)

            @pl.when(l == 0)
            def _():
                acc_ref[...] = p

            @pl.when((l > 0) & (l < nk - 1))
            def _():
                acc_ref[...] += p

            @pl.when(l == nk - 1)
            def _():
                o_ref[...] = (acc_ref[...] + p).astype(o_ref.dtype)
        scratch = [pltpu.VMEM((tm, tn), F32)]

    return pl.pallas_call(
        body, grid=(m // tm, n // tn, nk), in_specs=[a_spec, b_spec], out_specs=o_spec, out_shape=o_shape,
        scratch_shapes=scratch, compiler_params=_cp("parallel", "parallel", "arbitrary"), name=name)(a, b)


def _group_matmul(name, a, w, mode, out_dtype=F32, layer=0):
    s = a.shape[0]
    tb = 512
    gw = POOL_GW
    if mode == "tn":
        def body(a_ref, b_ref, o_ref):
            part = _dot(a_ref[...], b_ref[...], "tn")

            @pl.when(pl.program_id(1) == 0)
            def _():
                o_ref[0] = part

            @pl.when(pl.program_id(1) > 0)
            def _():
                o_ref[0] += part

        return pl.pallas_call(
            body, grid=(4, s // tb),
            in_specs=[pl.BlockSpec((tb, gw), lambda g, i: (i, g)), pl.BlockSpec((tb, gw), lambda g, i: (i, g))],
            out_specs=pl.BlockSpec((1, gw, gw), lambda g, i: (g, 0, 0)),
            out_shape=jax.ShapeDtypeStruct((4, gw, gw), F32),
            compiler_params=_cp("parallel", "arbitrary"), name=name)(a, w)

    def body(a_ref, w_ref, o_ref):
        o_ref[...] = _dot(a_ref[...], w_ref[...], mode).astype(o_ref.dtype)

    return pl.pallas_call(
        body, grid=(s // tb, 4),
        in_specs=[pl.BlockSpec((tb, gw), lambda i, g: (i, g)), pl.BlockSpec((None, None, gw, gw), lambda i, g: (layer, g, 0, 0))],
        out_specs=pl.BlockSpec((tb, gw), lambda i, g: (i, g)),
        out_shape=jax.ShapeDtypeStruct((s, 4 * gw), out_dtype),
        compiler_params=_cp("parallel", "parallel"), name=name)(a, w)


def _rspec(tb, width, cb):
    return pl.BlockSpec((tb, width), lambda i: (i, cb))


def _pspec(shape):
    return pl.BlockSpec(shape, lambda i: (0, 0))


def _rowwise_fwd(name, f, rows, pars, outs, tb=256):
    s = rows[0][0].shape[0]
    nin = len(rows) + len(pars)

    def body(*refs):
        res = f(*[r[...].astype(F32) for r in refs[:nin]])
        for o, v in zip(refs[nin:], res):
            o[...] = v.astype(o.dtype)

    return pl.pallas_call(
        body, grid=(s // tb,),
        in_specs=[_rspec(tb, w, cb) for _, w, cb in rows] + [_pspec(p.shape) for p in pars],
        out_specs=[_rspec(tb, w, 0) for w, _ in outs],
        out_shape=[jax.ShapeDtypeStruct((s, w), dt) for w, dt in outs],
        compiler_params=_cp("parallel"), name=name)(*[r[0] for r in rows], *pars)


def _rowwise_bwd(name, f, rows, pars, cts, need, add=None, tb=256, gdt=None):
    s = rows[0][0].shape[0]
    nr, npar, nc = len(rows), len(pars), len(cts)
    nin = nr + npar + nc + (1 if add is not None else 0)

    def body(*refs):
        ins = [r[...].astype(F32) for r in refs[:nr + npar]]
        _, vjp = jax.vjp(f, *ins)
        g = vjp(tuple(c[...].astype(F32) for c in refs[nr + npar:nr + npar + nc]))
        outs = refs[nin:]
        k = 0
        for j in range(nr):
            if need[j]:
                v = g[j]
                if add is not None and add[0] == j:
                    v = v + refs[nin - 1][...]
                outs[k][...] = v.astype(outs[k].dtype)
                k += 1
        first = pl.program_id(0) == 0
        for j in range(npar):
            o, v = outs[k + j], g[nr + j]

            @pl.when(first)
            def _(o=o, v=v):
                o[...] = v

            @pl.when(jnp.logical_not(first))
            def _(o=o, v=v):
                o[...] += v

    in_specs = ([_rspec(tb, w, cb) for _, w, cb in rows] + [_pspec(p.shape) for p in pars]
                + [_rspec(tb, w, cb) for _, w, cb in cts])
    args = [r[0] for r in rows] + list(pars) + [c[0] for c in cts]
    if add is not None:
        in_specs.append(_rspec(tb, rows[add[0]][1], 0))
        args.append(add[1])
    gr = [(w, F32) for (_, w, _), nd in zip(rows, need) if nd]
    if gdt is not None:
        gr = [(w, dt) for (w, _), dt in zip(gr, gdt)]
    return pl.pallas_call(
        body, grid=(s // tb,), in_specs=in_specs,
        out_specs=[_rspec(tb, w, 0) for w, _ in gr] + [_pspec(p.shape) for p in pars],
        out_shape=[jax.ShapeDtypeStruct((s, w), dt) for w, dt in gr] + [jax.ShapeDtypeStruct(p.shape, F32) for p in pars],
        compiler_params=_cp("arbitrary"), name=name)(*args)


def _f_norm(x, nw, sc, sh):
    r = lax.rsqrt(jnp.mean(x * x, axis=-1, keepdims=True) + EPS)
    return ((x * r * nw) * (1.0 + sc) + sh,)


def _f_ssdgate(y, z, w):
    y2 = y * _silu(z)
    low = lax.broadcasted_iota(jnp.int32, y2.shape, 1) < 512
    sq = y2 * y2
    m0 = jnp.sum(jnp.where(low, sq, 0.0), axis=-1, keepdims=True) / 512.0
    m1 = jnp.sum(jnp.where(low, 0.0, sq), axis=-1, keepdims=True) / 512.0
    r = jnp.where(low, lax.rsqrt(m0 + EPS), lax.rsqrt(m1 + EPS))
    return (y2 * r * w,)


def _head_rms(t, w):
    outs = []
    for h in range(t.shape[1] // ATTN_HEAD_DIM):
        th = t[:, h * ATTN_HEAD_DIM:(h + 1) * ATTN_HEAD_DIM]
        outs.append(th * lax.rsqrt(jnp.mean(th * th, axis=-1, keepdims=True) + EPS) * w)
    return jnp.concatenate(outs, axis=1)


def _f_qknorm(q, k, v, qw, kw):
    return _head_rms(q, qw), _head_rms(k, kw), v


def _f_combine(o1, o2, o3, l1, l2, l3):
    m = lax.stop_gradient(jnp.maximum(jnp.maximum(l1, l2), l3))
    e1, e2, e3 = jnp.exp(l1 - m), jnp.exp(l2 - m), jnp.exp(l3 - m)
    return ((e1 * o1 + e2 * o2 + e3 * o3) / (e1 + e2 + e3),)


def _f_poolscale(pm, ps):
    return (pm * ps,)


def _f_merge(gates, ys, ya, yp):
    g = _sigmoid(gates)
    return (g[:, 0:1024] * ys + g[:, 1024:2048] * ya + g[:, 2048:3072] * yp,)


def _f_resid(x, o, g):
    return (x + g * o,)


def _f_relu2(a):
    return (jnp.square(jnp.maximum(a, 0.0)),)


def _loss_and_grad(y, tgt, tb=512):
    s, d = y.shape

    def body(y_ref, t_ref, dy_ref, l_ref):
        e = y_ref[...] - t_ref[...]
        dy_ref[...] = e * (1.0 / d)
        part = jnp.zeros((1, LANES), F32) + jnp.sum(e * e) * (0.5 / d)

        @pl.when(pl.program_id(0) == 0)
        def _():
            l_ref[...] = part

        @pl.when(pl.program_id(0) > 0)
        def _():
            l_ref[...] += part

    return pl.pallas_call(
        body, grid=(s // tb,), in_specs=[_rspec(tb, d, 0), _rspec(tb, d, 0)],
        out_specs=[_rspec(tb, d, 0), _pspec((1, LANES))],
        out_shape=[jax.ShapeDtypeStruct((s, d), F32), jax.ShapeDtypeStruct((1, LANES), F32)],
        compiler_params=_cp("arbitrary"), name="loss")(y, tgt)


def _shift_down(x, j):
    rows = lax.broadcasted_iota(jnp.int32, x.shape, 0)
    return jnp.where(rows < j, 0.0, pltpu.roll(x, j, 0))


def _shift_up(x, j):
    s = x.shape[0]
    rows = lax.broadcasted_iota(jnp.int32, x.shape, 0)
    return jnp.where(rows >= s - j, 0.0, pltpu.roll(x, s - j, 0))


CONV_CB = 256


def _conv_pre(x, w_ref, b_ref):
    acc = b_ref[...] + w_ref[SSD_CONV - 1:SSD_CONV, :] * x
    for j in range(1, SSD_CONV):
        acc = acc + w_ref[SSD_CONV - 1 - j:SSD_CONV - j, :] * _shift_down(x, j)
    return acc


def _conv_fwd(proj, cw, cb):
    s = proj.shape[0]

    def body(x_ref, w_ref, b_ref, o_ref):
        o_ref[...] = _silu(_conv_pre(x_ref[...], w_ref, b_ref))

    return pl.pallas_call(
        body, grid=(CONV_DIM // CONV_CB,),
        in_specs=[pl.BlockSpec((s, CONV_CB), lambda i: (0, P_XBC // CONV_CB + i)),
                  pl.BlockSpec((SSD_CONV, CONV_CB), lambda i: (0, i)), pl.BlockSpec((1, CONV_CB), lambda i: (0, i))],
        out_specs=pl.BlockSpec((s, CONV_CB), lambda i: (0, i)),
        out_shape=jax.ShapeDtypeStruct((s, CONV_DIM), F32), compiler_params=_cp("parallel"), name="conv_fwd")(proj, cw, cb)


def _conv_bwd(proj, cw, cb, dout):
    s = proj.shape[0]

    def body(x_ref, w_ref, b_ref, d_ref, dx_ref, dw_ref, db_ref):
        x = x_ref[...]
        a = _conv_pre(x, w_ref, b_ref)
        sg = _sigmoid(a)
        da = d_ref[...] * (sg + a * sg * (1.0 - sg))
        db_ref[...] = jnp.sum(da, axis=0, keepdims=True)
        dx = w_ref[SSD_CONV - 1:SSD_CONV, :] * da
        dw_ref[SSD_CONV - 1:SSD_CONV, :] = jnp.sum(da * x, axis=0, keepdims=True)
        for j in range(1, SSD_CONV):
            dx = dx + w_ref[SSD_CONV - 1 - j:SSD_CONV - j, :] * _shift_up(da, j)
            dw_ref[SSD_CONV - 1 - j:SSD_CONV - j, :] = jnp.sum(da * _shift_down(x, j), axis=0, keepdims=True)
        dx_ref[...] = dx.astype(dx_ref.dtype)

    return pl.pallas_call(
        body, grid=(CONV_DIM // CONV_CB,),
        in_specs=[pl.BlockSpec((s, CONV_CB), lambda i: (0, P_XBC // CONV_CB + i)),
                  pl.BlockSpec((SSD_CONV, CONV_CB), lambda i: (0, i)), pl.BlockSpec((1, CONV_CB), lambda i: (0, i)),
                  pl.BlockSpec((s, CONV_CB), lambda i: (0, i))],
        out_specs=[pl.BlockSpec((s, CONV_CB), lambda i: (0, i)), pl.BlockSpec((SSD_CONV, CONV_CB), lambda i: (0, i)),
                   pl.BlockSpec((1, CONV_CB), lambda i: (0, i))],
        out_shape=[jax.ShapeDtypeStruct((s, CONV_DIM), BF16), jax.ShapeDtypeStruct((SSD_CONV, CONV_DIM), F32),
                   jax.ShapeDtypeStruct((1, CONV_DIM), F32)],
        compiler_params=_cp("parallel"), name="conv_bwd")(proj, cw, cb, dout)


def _pool_window_sum(x, g, shift):
    s2 = x + shift(x, 1)
    s4 = s2 + shift(s2, 2)
    s8 = s4 + shift(s4, 4)
    s16 = s8 + shift(s8, 8)
    return jnp.where(g == 0, s2, jnp.where(g == 1, s4, jnp.where(g == 2, s8, s16)))


def _pool_count(shape, g):
    rows = lax.broadcasted_iota(jnp.int32, shape, 0)
    return jnp.minimum(rows + 1, jnp.left_shift(2, g)).astype(F32)


def _pool_fwd(proj):
    s = proj.shape[0]

    def body(u_ref, o_ref):
        g = pl.program_id(0)
        u = u_ref[...]
        o_ref[...] = (_pool_window_sum(u, g, _shift_down) / _pool_count(u.shape, g) - u).astype(o_ref.dtype)

    return pl.pallas_call(
        body, grid=(4,), in_specs=[pl.BlockSpec((s, POOL_GW), lambda g: (0, P_U // POOL_GW + g))],
        out_specs=pl.BlockSpec((s, POOL_GW), lambda g: (0, g)),
        out_shape=jax.ShapeDtypeStruct((s, 4 * POOL_GW), BF16), compiler_params=_cp("parallel"), name="pool_fwd")(proj)


def _pool_bwd(dp):
    s = dp.shape[0]

    def body(d_ref, o_ref):
        g = pl.program_id(0)
        d = d_ref[...]
        o_ref[...] = (_pool_window_sum(d / _pool_count(d.shape, g), g, _shift_up) - d).astype(o_ref.dtype)

    return pl.pallas_call(
        body, grid=(4,), in_specs=[pl.BlockSpec((s, POOL_GW), lambda g: (0, g))],
        out_specs=pl.BlockSpec((s, POOL_GW), lambda g: (0, g)),
        out_shape=jax.ShapeDtypeStruct((s, 4 * POOL_GW), BF16), compiler_params=_cp("parallel"), name="pool_bwd")(dp)


N_PAIRS = SSD_HEADS // 2
STATE_ROWS = N_PAIRS * SSD_STATE


def _ssd_chunk(xbc, dtr, hprev, dtb, alog, dsk):
    L = xbc.shape[0]
    xs, bm, cm = xbc[:, 0:1024], xbc[:, 1024:1280], xbc[:, 1280:1536]
    dt = _softplus(dtr + dtb)
    a = dt * (-jnp.exp(alog))
    acum = _csum(a)
    alast = jnp.sum(a, axis=0, keepdims=True)
    xdt = xs * dt
    xdecay = xdt * jnp.exp(alast - acum)
    eacum = jnp.exp(acum)
    elast = jnp.exp(alast)
    cb = [_bdot(cm[:, g * 128:(g + 1) * 128], bm[:, g * 128:(g + 1) * 128], "nt") for g in range(2)]
    rows = lax.broadcasted_iota(jnp.int32, (L, L), 0)
    cols = lax.broadcasted_iota(jnp.int32, (L, L), 1)
    causal = rows >= cols
    lane = lax.broadcasted_iota(jnp.int32, (L, LANES), 1)
    sub = lax.broadcasted_iota(jnp.int32, (LANES, L), 0)
    ys, hs = [], []
    for p in range(N_PAIRS):
        g = p // (N_PAIRS // 2)
        sl = slice(p * LANES, (p + 1) * LANES)
        ac = acum[:, sl]
        act = ac.T
        xp = xdt[:, sl]
        hp = hprev[p * SSD_STATE:(p + 1) * SSD_STATE, :]
        y = _bdot(cm[:, g * 128:(g + 1) * 128], hp, "nn") * eacum[:, sl] + dsk[:, sl] * xs[:, sl]
        for half in range(2):
            l0 = half * SSD_HEAD_DIM
            col = jnp.sum(jnp.where(lane == l0, ac, 0.0), axis=1, keepdims=True)
            row = jnp.sum(jnp.where(sub == l0, act, 0.0), axis=0, keepdims=True)
            decay = jnp.exp(jnp.where(causal, col - row, NEG))
            xh = jnp.where((lane >= l0) & (lane < l0 + SSD_HEAD_DIM), xp, 0.0)
            y = y + _bdot(cb[g] * decay, xh, "nn")
        ys.append(y)
        hs.append(elast[:, sl] * hp + _bdot(bm[:, g * 128:(g + 1) * 128], xdecay[:, sl], "tn"))
    return tuple(ys), tuple(hs)


def _ssd_fwd(xbc, proj, dtb, alog, dsk):
    s = xbc.shape[0]
    nc = s // SSD_CHUNK

    def body(x_ref, dt_ref, b_ref, a_ref, d_ref, y_ref, hist_ref, h_ref):
        @pl.when(pl.program_id(0) == 0)
        def _():
            h_ref[...] = jnp.zeros_like(h_ref)

        hprev = h_ref[...]
        hist_ref[...] = hprev
        ys, hs = _ssd_chunk(x_ref[...], dt_ref[...], hprev, b_ref[...], a_ref[...], d_ref[...])
        for p in range(N_PAIRS):
            y_ref[:, p * LANES:(p + 1) * LANES] = ys[p]
            h_ref[p * SSD_STATE:(p + 1) * SSD_STATE, :] = hs[p]

    return pl.pallas_call(
        body, grid=(nc,),
        in_specs=[pl.BlockSpec((SSD_CHUNK, CONV_DIM), lambda i: (i, 0)),
                  pl.BlockSpec((SSD_CHUNK, 1024), lambda i: (i, P_DT // 1024)),
                  _pspec((1, 1024)), _pspec((1, 1024)), _pspec((1, 1024))],
        out_specs=[pl.BlockSpec((SSD_CHUNK, 1024), lambda i: (i, 0)), pl.BlockSpec((STATE_ROWS, LANES), lambda i: (i, 0))],
        out_shape=[jax.ShapeDtypeStruct((s, 1024), F32), jax.ShapeDtypeStruct((nc * STATE_ROWS, LANES), F32)],
        scratch_shapes=[pltpu.VMEM((STATE_ROWS, LANES), F32)],
        compiler_params=_cp("arbitrary"), name="ssd_fwd")(xbc, proj, dtb, alog, dsk)


def _ssd_bwd(xbc, proj, hist, dtb, alog, dsk, dy):
    s = xbc.shape[0]
    nc = s // SSD_CHUNK

    def body(x_ref, dt_ref, hist_ref, b_ref, a_ref, d_ref, dy_ref, dx_ref, ddt_ref, db_ref, da_ref, dd_ref, dh_ref):
        first = pl.program_id(0) == 0

        @pl.when(first)
        def _():
            dh_ref[...] = jnp.zeros_like(dh_ref)

        _, vjp = jax.vjp(_ssd_chunk, x_ref[...], dt_ref[...], hist_ref[...], b_ref[...], a_ref[...], d_ref[...])
        dys = tuple(dy_ref[:, p * LANES:(p + 1) * LANES] for p in range(N_PAIRS))
        dhs = tuple(dh_ref[p * SSD_STATE:(p + 1) * SSD_STATE, :] for p in range(N_PAIRS))
        dx, ddt, dhp, db, da, dd = vjp((dys, dhs))
        dx_ref[...] = dx
        ddt_ref[...] = ddt.astype(ddt_ref.dtype)
        dh_ref[...] = dhp
        for o, v in ((db_ref, db), (da_ref, da), (dd_ref, dd)):
            @pl.when(first)
            def _(o=o, v=v):
                o[...] = v

            @pl.when(jnp.logical_not(first))
            def _(o=o, v=v):
                o[...] += v

    rev = lambda i: (nc - 1 - i, 0)
    return pl.pallas_call(
        body, grid=(nc,),
        in_specs=[pl.BlockSpec((SSD_CHUNK, CONV_DIM), rev),
                  pl.BlockSpec((SSD_CHUNK, 1024), lambda i: (nc - 1 - i, P_DT // 1024)),
                  pl.BlockSpec((STATE_ROWS, LANES), rev),
                  _pspec((1, 1024)), _pspec((1, 1024)), _pspec((1, 1024)),
                  pl.BlockSpec((SSD_CHUNK, 1024), rev)],
        out_specs=[pl.BlockSpec((SSD_CHUNK, CONV_DIM), rev), pl.BlockSpec((SSD_CHUNK, 1024), rev),
                   _pspec((1, 1024)), _pspec((1, 1024)), _pspec((1, 1024))],
        out_shape=[jax.ShapeDtypeStruct((s, CONV_DIM), F32), jax.ShapeDtypeStruct((s, 1024), BF16)]
        + [jax.ShapeDtypeStruct((1, 1024), F32)] * 3,
        scratch_shapes=[pltpu.VMEM((STATE_ROWS, LANES), F32)],
        compiler_params=_cp("arbitrary"), name="ssd_bwd")(xbc, proj, hist, dtb, alog, dsk, dy)


def _attn_head(q, kp, kc, vp, vc, has_prev, slope):
    scale = ATTN_HEAD_DIM ** -0.5
    n = ATTN_STEPS
    qi = lax.broadcasted_iota(jnp.int32, (n, n), 0)
    kj = lax.broadcasted_iota(jnp.int32, (n, n), 1)
    sp = jnp.where((kj >= qi) & has_prev, _bdot(q, kp, "nt") * scale - slope * (qi + n - kj).astype(F32), NEG)
    sc = jnp.where(kj <= qi, _bdot(q, kc, "nt") * scale - slope * (qi - kj).astype(F32), NEG)
    m = lax.stop_gradient(jnp.maximum(jnp.max(sp, axis=1, keepdims=True), jnp.max(sc, axis=1, keepdims=True)))
    pp, pc = jnp.exp(sp - m), jnp.exp(sc - m)
    den = jnp.sum(pp, axis=1, keepdims=True) + jnp.sum(pc, axis=1, keepdims=True)
    o = (_bdot(pp, vp, "nn") + _bdot(pc, vc, "nn")) / den
    return o, jnp.broadcast_to(m + jnp.log(den), (n, ATTN_HEAD_DIM))


def _head_slope(gi, h):
    s = [float(v) * DILATIONS[gi] for v in SLOPES[gi]]
    return jnp.where(h == 0, s[0], jnp.where(h == 1, s[1], jnp.where(h == 2, s[2], s[3])))


def _residues(ref, d):
    return [ref[pl.ds(r, ATTN_STEPS, stride=d), :] for r in range(d)]


def _attn_fwd(qn, kn, vv, gi):
    d = DILATIONS[gi]
    s = qn.shape[0]
    span = ATTN_STEPS * d
    nb = s // span

    def body(q_ref, k_ref, v_ref, o_ref, l_ref, sq, sk, sv, so, sl):
        h, b = pl.program_id(0), pl.program_id(1)
        cur, prev = b % 2, (b + 1) % 2

        @pl.when(b == 0)
        def _():
            sk[prev] = jnp.zeros(sk.shape[1:], F32)
            sv[prev] = jnp.zeros(sv.shape[1:], F32)

        for r, (qr, kr, vr) in enumerate(zip(_residues(q_ref, d), _residues(k_ref, d), _residues(v_ref, d))):
            sq[r] = qr
            sk[cur, r] = kr
            sv[cur, r] = vr
        slope = _head_slope(gi, h)

        def step(r, carry):
            so[r], sl[r] = _attn_head(sq[r], sk[prev, r], sk[cur, r], sv[prev, r], sv[cur, r], b > 0, slope)
            return carry

        lax.fori_loop(0, d, step, 0)
        for r in range(d):
            o_ref[pl.ds(r, ATTN_STEPS, stride=d), :] = so[r]
            l_ref[pl.ds(r, ATTN_STEPS, stride=d), :] = sl[r]

    blk = pl.BlockSpec((span, ATTN_HEAD_DIM), lambda h, b: (b, gi * 4 + h))
    out = pl.BlockSpec((span, ATTN_HEAD_DIM), lambda h, b: (b, h))
    res = (d, ATTN_STEPS, ATTN_HEAD_DIM)
    return pl.pallas_call(
        body, grid=(4, nb), in_specs=[blk, blk, blk], out_specs=[out, out],
        out_shape=[jax.ShapeDtypeStruct((s, ATTN_GROUP_W), F32)] * 2,
        scratch_shapes=[pltpu.VMEM(res, F32), pltpu.VMEM((2,) + res, F32), pltpu.VMEM((2,) + res, F32),
                        pltpu.VMEM(res, F32), pltpu.VMEM(res, F32)],
        compiler_params=_cp("parallel", "arbitrary"), name=f"attn_fwd_g{gi}")(qn, kn, vv)


def _attn_bwd(qn, kn, vv, do, dl, gi):
    d = DILATIONS[gi]
    s = qn.shape[0]
    span = ATTN_STEPS * d
    nb = s // span

    def body(q_ref, kp_ref, kc_ref, vp_ref, vc_ref, do_ref, dl_ref, dq_ref, dk_ref, dv_ref, sin, sout, ck, cv):
        h, bi = pl.program_id(0), pl.program_id(1)

        @pl.when(bi == 0)
        def _():
            ck[...] = jnp.zeros_like(ck)
            cv[...] = jnp.zeros_like(cv)

        for i, ref in enumerate((q_ref, kp_ref, kc_ref, vp_ref, vc_ref, do_ref, dl_ref)):
            for r, val in enumerate(_residues(ref, d)):
                sin[i, r] = val
        f = functools.partial(_attn_head, has_prev=bi < nb - 1, slope=_head_slope(gi, h))

        def step(r, carry):
            _, vjp = jax.vjp(f, sin[0, r], sin[1, r], sin[2, r], sin[3, r], sin[4, r])
            dq, dkp, dkc, dvp, dvc = vjp((sin[5, r], sin[6, r]))
            sout[0, r] = dq
            sout[1, r] = dkc + ck[r]
            sout[2, r] = dvc + cv[r]
            ck[r] = dkp
            cv[r] = dvp
            return carry

        lax.fori_loop(0, d, step, 0)
        for i, ref in enumerate((dq_ref, dk_ref, dv_ref)):
            for r in range(d):
                ref[pl.ds(r, ATTN_STEPS, stride=d), :] = sout[i, r]

    cur = pl.BlockSpec((span, ATTN_HEAD_DIM), lambda h, b: (nb - 1 - b, gi * 4 + h))
    prev = pl.BlockSpec((span, ATTN_HEAD_DIM), lambda h, b: (jnp.maximum(nb - 2 - b, 0), gi * 4 + h))
    out = pl.BlockSpec((span, ATTN_HEAD_DIM), lambda h, b: (nb - 1 - b, h))
    res = (d, ATTN_STEPS, ATTN_HEAD_DIM)
    return pl.pallas_call(
        body, grid=(4, nb), in_specs=[cur, prev, cur, prev, cur, out, out], out_specs=[out, out, out],
        out_shape=[jax.ShapeDtypeStruct((s, ATTN_GROUP_W), F32)] * 3,
        scratch_shapes=[pltpu.VMEM((7,) + res, F32), pltpu.VMEM((3,) + res, F32), pltpu.VMEM(res, F32), pltpu.VMEM(res, F32)],
        compiler_params=_cp("parallel", "arbitrary"), name=f"attn_bwd_g{gi}")(qn, kn, kn, vv, vv, do, dl)


def _layer_fwd(x, mod, W, l):
    sh1, sc1, g1, sh2, sc2, g2 = (mod[i:i + 1] for i in range(6))
    (h,) = _rowwise_fwd("norm1", _f_norm, [(x, 1024, 0)], [W["norm1_w"], sc1, sh1], [(1024, BF16)])
    proj = _matmul("in_proj", h, W["w_in"], "nn")
    xbc = _conv_fwd(proj, W["conv_w"], W["conv_b"])
    y, hist = _ssd_fwd(xbc, proj, W["dt_bias"], W["a_log"], W["d_skip"])
    (yn,) = _rowwise_fwd("ssd_gate", _f_ssdgate, [(y, 1024, 0), (proj, 1024, P_Z // 1024)], [W["ssd_norm_w"]], [(1024, BF16)])
    y_ssd = _matmul("ssd_out", yn, W["w_ssd_out"], "nn", layer=l)
    qn, kn, vv = _rowwise_fwd("qk_norm", _f_qknorm, [(proj, 1536, P_Q // 1536), (proj, 1536, P_K // 1536), (proj, 1536, P_V // 1536)],
                              [W["q_norm_w"], W["k_norm_w"]], [(1536, F32)] * 3)
    ol = [_attn_fwd(qn, kn, vv, gi) for gi in range(3)]
    (o,) = _rowwise_fwd("attn_combine", _f_combine, [(t[0], 512, 0) for t in ol] + [(t[1], 512, 0) for t in ol], [], [(512, BF16)])
    y_attn = _matmul("attn_out", o, W["w_attn_out"], "nn", layer=l, chips=256)
    pooled = _pool_fwd(proj)
    pm = _group_matmul("pool_mix", pooled, W["w_pool_mix"], "nn", layer=l)
    (ps,) = _rowwise_fwd("pool_scale", _f_poolscale, [(pm, 1024, 0)], [W["pool_scale"]], [(1024, BF16)])
    y_pool = _matmul("pool_out", ps, W["w_pool_out"], "nn", layer=l)
    (merged,) = _rowwise_fwd("merge", _f_merge, [(proj, 3072, P_GATES // 3072), (y_ssd, 1024, 0), (y_attn, 1024, 0), (y_pool, 1024, 0)],
                             [], [(1024, BF16)])
    mo = _matmul("mix_out", merged, W["w_out"], "nn", layer=l)
    (x1,) = _rowwise_fwd("resid1", _f_resid, [(x, 1024, 0), (mo, 1024, 0)], [g1], [(1024, F32)])
    (h2,) = _rowwise_fwd("norm2", _f_norm, [(x1, 1024, 0)], [W["norm2_w"], sc2, sh2], [(1024, BF16)])
    a = _matmul("ff1", h2, W["w_ff1"], "nn", layer=l, chips=1024)
    (r,) = _rowwise_fwd("relu2", _f_relu2, [(a, D_FF, 0)], [], [(D_FF, BF16)], tb=128)
    ff = _matmul("ff2", r, W["w_ff2"], "nn", layer=l)
    (x2,) = _rowwise_fwd("resid2", _f_resid, [(x1, 1024, 0), (ff, 1024, 0)], [g2], [(1024, F32)])
    saved = dict(x=x, h=h, proj=proj, xbc=xbc, y=y, hist=hist, yn=yn, y_ssd=y_ssd, qn=qn, kn=kn, vv=vv, ol=ol, o=o,
                 y_attn=y_attn, pooled=pooled, pm=pm, ps=ps, y_pool=y_pool, merged=merged, mo=mo, x1=x1, h2=h2, a=a, r=r, ff=ff)
    return x2, saved


def _layer_bwd(dx2, mod, W, sv, l):
    sh1, sc1, g1, sh2, sc2, g2 = (mod[i:i + 1] for i in range(6))
    g = {}
    dx1a, dff, dg2 = _rowwise_bwd("resid2_bwd", _f_resid, [(sv["x1"], 1024, 0), (sv["ff"], 1024, 0)], [g2], [(dx2, 1024, 0)],
                                  [True, True], gdt=[F32, BF16])
    g["w_ff2"] = _matmul("ff2_dw", sv["r"], dff, "tn", BF16).reshape(N_CHIPS, D_FF // N_CHIPS, D_MODEL)
    dr = _matmul("ff2_dx", dff, W["w_ff2"], "nt", layer=l)
    (da,) = _rowwise_bwd("relu2_bwd", _f_relu2, [(sv["a"], D_FF, 0)], [], [(dr, D_FF, 0)], [True], tb=128, gdt=[BF16])
    g["w_ff1"] = _matmul("ff1_dw", sv["h2"], da, "tn", BF16, out_chips=1024)
    dh2 = _matmul("ff1_dx", da, W["w_ff1"], "nt", layer=l, chips=1024)
    dx1, g["norm2_w"], dsc2, dsh2 = _rowwise_bwd("norm2_bwd", _f_norm, [(sv["x1"], 1024, 0)], [W["norm2_w"], sc2, sh2],
                                                 [(dh2, 1024, 0)], [True], add=(0, dx1a))
    dxa, dmo, dg1 = _rowwise_bwd("resid1_bwd", _f_resid, [(sv["x"], 1024, 0), (sv["mo"], 1024, 0)], [g1], [(dx1, 1024, 0)],
                                 [True, True], gdt=[F32, BF16])
    g["w_out"] = _matmul("mix_out_dw", sv["merged"], dmo, "tn", BF16).reshape(N_CHIPS, D_MODEL // N_CHIPS, D_MODEL)
    dmerged = _matmul("mix_out_dx", dmo, W["w_out"], "nt", layer=l)
    proj = sv["proj"]
    dgates, dy_ssd, dy_attn, dy_pool = _rowwise_bwd(
        "merge_bwd", _f_merge, [(proj, 3072, P_GATES // 3072), (sv["y_ssd"], 1024, 0), (sv["y_attn"], 1024, 0), (sv["y_pool"], 1024, 0)],
        [], [(dmerged, 1024, 0)], [True] * 4, gdt=[BF16] * 4)
    g["w_pool_out"] = _matmul("pool_out_dw", sv["ps"], dy_pool, "tn", BF16).reshape(N_CHIPS, D_MODEL // N_CHIPS, D_MODEL)
    dps = _matmul("pool_out_dx", dy_pool, W["w_pool_out"], "nt", layer=l)
    dpm, g["pool_scale"] = _rowwise_bwd("pool_scale_bwd", _f_poolscale, [(sv["pm"], 1024, 0)], [W["pool_scale"]], [(dps, 1024, 0)],
                                        [True], gdt=[BF16])
    dmix = _group_matmul("pool_mix_dw", sv["pooled"], dpm, "tn")
    g["w_pool_mix"] = jnp.moveaxis(dmix.reshape(4, N_CHIPS, POOL_GW // N_CHIPS, POOL_GW), 1, 0).astype(BF16)
    dpooled = _group_matmul("pool_mix_dx", dpm, W["w_pool_mix"], "nt", layer=l)
    du = _pool_bwd(dpooled)
    g["w_attn_out"] = _matmul("attn_out_dw", sv["o"], dy_attn, "tn", BF16, out_chips=256)
    do = _matmul("attn_out_dx", dy_attn, W["w_attn_out"], "nt", layer=l, chips=256)
    ol = sv["ol"]
    dol = _rowwise_bwd("attn_combine_bwd", _f_combine, [(t[0], 512, 0) for t in ol] + [(t[1], 512, 0) for t in ol], [],
                       [(do, 512, 0)], [True] * 6)
    dqs, dks, dvs = zip(*[_attn_bwd(sv["qn"], sv["kn"], sv["vv"], dol[gi], dol[3 + gi], gi) for gi in range(3)])
    dqn, dkn, dvv = (jnp.concatenate(t, axis=1) for t in (dqs, dks, dvs))
    dq, dk, dv, g["q_norm_w"], g["k_norm_w"] = _rowwise_bwd(
        "qk_norm_bwd", _f_qknorm, [(proj, 1536, P_Q // 1536), (proj, 1536, P_K // 1536), (proj, 1536, P_V // 1536)],
        [W["q_norm_w"], W["k_norm_w"]], [(dqn, 1536, 0), (dkn, 1536, 0), (dvv, 1536, 0)], [True] * 3, gdt=[BF16] * 3)
    g["w_ssd_out"] = _matmul("ssd_out_dw", sv["yn"], dy_ssd, "tn", BF16).reshape(N_CHIPS, D_MODEL // N_CHIPS, D_MODEL)
    dyn = _matmul("ssd_out_dx", dy_ssd, W["w_ssd_out"], "nt", layer=l)
    dy, dz, g["ssd_norm_w"] = _rowwise_bwd("ssd_gate_bwd", _f_ssdgate, [(sv["y"], 1024, 0), (proj, 1024, P_Z // 1024)], [W["ssd_norm_w"]],
                                           [(dyn, 1024, 0)], [True, True], gdt=[F32, BF16])
    dxbc, ddt, g["dt_bias"], g["a_log"], g["d_skip"] = _ssd_bwd(sv["xbc"], proj, sv["hist"], W["dt_bias"], W["a_log"], W["d_skip"], dy)
    dxbc_raw, g["conv_w"], g["conv_b"] = _conv_bwd(proj, W["conv_w"], W["conv_b"], dxbc)
    dproj = jnp.concatenate([dxbc_raw, dq, dk, dv, dgates, dz, du, ddt], axis=1)
    g["w_in"] = _matmul("in_proj_dw", sv["h"], dproj, "tn", BF16)
    dh = _matmul("in_proj_dx", dproj, W["w_in"], "nt")
    dx, g["norm1_w"], dsc1, dsh1 = _rowwise_bwd("norm1_bwd", _f_norm, [(sv["x"], 1024, 0)], [W["norm1_w"], sc1, sh1],
                                                [(dh, 1024, 0)], [True], add=(0, dxa))
    dmod = jnp.concatenate([dsh1, dsc1, dg1, dsh2, dsc2, dg2], axis=0)
    return dx, dmod, g


def _expand_heads(t):
    return jnp.repeat(t, SSD_HEAD_DIM, axis=-1)


def _reduce_heads(t):
    return t.reshape(t.shape[:-1] + (SSD_HEADS, SSD_HEAD_DIM)).sum(-1)


_IN_SPLITS = np.cumsum((0,) + IN_SIZES)


def _w_in_to_layout(w):
    z, xbc, dt, q, k, v, u, gates = (w[:, _IN_SPLITS[i]:_IN_SPLITS[i + 1]] for i in range(8))
    return jnp.concatenate([xbc, q, k, v, gates, z, u, _expand_heads(dt)], axis=1)


def _w_in_from_layout(g):
    xbc, q, k, v = (g[:, o:o + 1536] for o in (P_XBC, P_Q, P_K, P_V))
    gates, z, u, dt = g[:, P_GATES:P_GATES + 3072], g[:, P_Z:P_Z + 1024], g[:, P_U:P_U + 1024], g[:, P_DT:P_DT + 1024]
    return jnp.concatenate([z, xbc, _reduce_heads(dt.astype(F32)).astype(g.dtype), q, k, v, u, gates], axis=1)


_STACKED = ("w_ssd_out", "w_attn_out", "w_pool_mix", "w_pool_out", "w_out", "w_ff1", "w_ff2")
_ROWS = ("norm1_w", "norm2_w", "conv_b", "ssd_norm_w", "q_norm_w", "k_norm_w", "pool_scale")
_HEAD_ROWS = ("dt_bias", "a_log", "d_skip")


def _layer_weights(ws, l):
    W = {k: ws[k] for k in _STACKED}
    w_in = jnp.concatenate([ws["w_in"][l, j] for j in range(N_CHIPS)], axis=1)
    W["w_in"] = _w_in_to_layout(w_in)
    W["conv_w"] = ws["conv_w"][l]
    for k in _ROWS:
        W[k] = ws[k][l][None, :]
    for k in _HEAD_ROWS:
        W[k] = _expand_heads(ws[k][l])[None, :]
    return W


def _layer_grads_by_chip(g):
    out = dict(g)
    w_in = _w_in_from_layout(g["w_in"])
    out["w_in"] = jnp.moveaxis(w_in.reshape(D_MODEL, N_CHIPS, IN_WIDTH // N_CHIPS), 1, 0)
    for k in _ROWS:
        out[k] = g[k][0]
    for k in _HEAD_ROWS:
        out[k] = _reduce_heads(g[k][0])
    return out


def _local_step(x, tgt, mods, ws):
    saved, Ws = [], []
    for l in range(DEPTH):
        W = _layer_weights(ws, l)
        x, sv = _layer_fwd(x, mods[l], W, l)
        saved.append(sv)
        Ws.append(W)
    dx, loss = _loss_and_grad(x, tgt)
    dmods, grads = [None] * DEPTH, [None] * DEPTH
    for l in reversed(range(DEPTH)):
        dx, dmods[l], g = _layer_bwd(dx, mods[l], Ws[l], saved[l], l)
        grads[l] = _layer_grads_by_chip(g)
    return loss, dx, jnp.stack(dmods), grads


ANY = pl.BlockSpec(memory_space=pl.ANY)


def _place():
    x, y, c = lax.axis_index("x"), lax.axis_index("y"), lax.axis_index("c")
    return x, y, c, (x, y, 1 - c), [(1 - x, y), (x, 1 - y), (1 - x, 1 - y)]


def _allgather8(name, blk):
    m_per, n = blk.shape

    def body(x_ref, out_ref, send_sems, recv_sems, local_sem):
        x, y, c, sibling, chips = _place()
        me = (x, y, c)

        def rows(px, py, pc):
            return out_ref.at[pl.ds((4 * px + 2 * py + pc) * m_per, m_per), :]

        def copy(k, block, to, src=None):
            return pltpu.make_async_remote_copy(
                src_ref=rows(*block) if src is None else src, dst_ref=rows(*block),
                send_sem=send_sems.at[k], recv_sem=recv_sems.at[k], device_id=to, device_id_type=MESH)

        mine = pltpu.make_async_copy(x_ref, rows(*me), local_sem)
        mine.start()
        first = [copy(0, me, sibling, src=x_ref)]
        first += [copy(1 + j, me, (*chip, c), src=x_ref) for j, chip in enumerate(chips)]
        for cp in first:
            cp.start()
        passed = [copy(4 + j, (*chip, c), sibling) for j, chip in enumerate(chips)]
        for j, chip in enumerate(chips):
            copy(1 + j, (*chip, c), me).wait_recv()
            passed[j].start()
        copy(0, sibling, me).wait_recv()
        for j, chip in enumerate(chips):
            copy(4 + j, (*chip, 1 - c), me).wait_recv()
        for cp in first + passed:
            cp.wait_send()
        mine.wait()

    return pl.pallas_call(
        body, out_shape=jax.ShapeDtypeStruct((N_DEV * m_per, n), blk.dtype),
        in_specs=[pl.BlockSpec(memory_space=pltpu.VMEM)], out_specs=pl.BlockSpec(memory_space=pltpu.VMEM),
        scratch_shapes=[pltpu.SemaphoreType.DMA((7,)), pltpu.SemaphoreType.DMA((7,)), pltpu.SemaphoreType.DMA],
        name=name)(blk)


HALF_LAYERS = DEPTH // 2


def _dma_sems(n):
    return [pltpu.SemaphoreType.DMA((n,)), pltpu.SemaphoreType.DMA((n,))]


def _gather_weights(shards):
    n = len(shards)

    def body(*refs):
        ins, outs = refs[:n], refs[n:2 * n]
        send_sems, recv_sems, local_sems = refs[2 * n:]
        x, y, c, sibling, chips = _place()
        me = 2 * x + y
        half, other = pl.ds(HALF_LAYERS * c, HALF_LAYERS), pl.ds(HALF_LAYERS * (1 - c), HALF_LAYERS)

        def copy(k, i, layers, chip, to, src=None):
            dst = outs[i].at[layers, chip]
            return pltpu.make_async_remote_copy(src_ref=dst if src is None else src, dst_ref=dst, send_sem=send_sems.at[k],
                                                recv_sem=recv_sems.at[k], device_id=to, device_id_type=MESH)

        local = [pltpu.make_async_copy(ins[i], outs[i].at[:, me], local_sems.at[i]) for i in range(n)]
        for cp in local:
            cp.start()
        first = [copy(3 * i + j, i, half, me, (cx, cy, c), src=ins[i].at[half]) for j, (cx, cy) in enumerate(chips) for i in range(n)]
        for cp in first:
            cp.start()
        passed = []
        for j, (cx, cy) in enumerate(chips):
            for i in range(n):
                copy(3 * i + j, i, half, 2 * cx + cy, (x, y, c)).wait_recv()
                passed.append(copy(3 * n + 3 * i + j, i, half, 2 * cx + cy, sibling))
                passed[-1].start()
        for j, (cx, cy) in enumerate(chips):
            for i in range(n):
                copy(3 * n + 3 * i + j, i, other, 2 * cx + cy, (x, y, c)).wait_recv()
        for cp in first + passed:
            cp.wait_send()
        for cp in local:
            cp.wait()

    return pl.pallas_call(
        body, out_shape=[jax.ShapeDtypeStruct((t.shape[0], N_CHIPS) + t.shape[1:], t.dtype) for t in shards],
        in_specs=[ANY] * n, out_specs=[ANY] * n,
        scratch_shapes=_dma_sems(6 * n) + [pltpu.SemaphoreType.DMA((n,))], name="gather_weights")(*shards)


def _swap_halves(gs):
    n = len(gs)

    def body(*refs):
        ins, got, own = refs[:n], refs[n:2 * n], refs[2 * n:3 * n]
        send_sems, recv_sems, local_sems = refs[3 * n:]
        x, y, c, sibling, _ = _place()
        half, other = pl.ds(HALF_LAYERS * c, HALF_LAYERS), pl.ds(HALF_LAYERS * (1 - c), HALF_LAYERS)
        local = [pltpu.make_async_copy(ins[i].at[:, half], own[i], local_sems.at[i]) for i in range(n)]
        sends = [pltpu.make_async_remote_copy(src_ref=ins[i].at[:, other], dst_ref=got[i], send_sem=send_sems.at[i],
                                              recv_sem=recv_sems.at[i], device_id=sibling, device_id_type=MESH) for i in range(n)]
        for cp in local + sends:
            cp.start()
        for cp in sends:
            cp.wait_recv()
        for cp in sends:
            cp.wait_send()
        for cp in local:
            cp.wait()

    shapes = [jax.ShapeDtypeStruct((N_CHIPS, HALF_LAYERS) + t.shape[2:], t.dtype) for t in gs]
    res = pl.pallas_call(body, out_shape=shapes + shapes, in_specs=[ANY] * n, out_specs=[ANY] * (2 * n),
                         scratch_shapes=_dma_sems(n) + [pltpu.SemaphoreType.DMA((n,))], name="grad_swap_halves")(*gs)
    return res[:n], res[n:]


def _exchange_chips(ps):
    n = len(ps)

    def body(*refs):
        ins, outs = refs[:n], refs[n:2 * n]
        send_sems, recv_sems, local_sems = refs[2 * n:]
        x, y, c, _, chips = _place()
        local = [pltpu.make_async_copy(ins[i].at[2 * x + y], outs[i].at[3], local_sems.at[i]) for i in range(n)]
        sends = [pltpu.make_async_remote_copy(src_ref=ins[i].at[2 * cx + cy], dst_ref=outs[i].at[j], send_sem=send_sems.at[3 * i + j],
                                              recv_sem=recv_sems.at[3 * i + j], device_id=(cx, cy, c), device_id_type=MESH)
                 for j, (cx, cy) in enumerate(chips) for i in range(n)]
        for cp in local + sends:
            cp.start()
        for cp in sends:
            cp.wait_recv()
        for cp in sends:
            cp.wait_send()
        for cp in local:
            cp.wait()

    return pl.pallas_call(
        body, out_shape=[jax.ShapeDtypeStruct(t.shape, t.dtype) for t in ps], in_specs=[ANY] * n, out_specs=[ANY] * n,
        scratch_shapes=_dma_sems(3 * n) + [pltpu.SemaphoreType.DMA((n,))], name="grad_exchange_chips")(*ps)


def _share_halves(ts):
    n = len(ts)

    def body(*refs):
        ins, outs = refs[:n], refs[n:2 * n]
        send_sems, recv_sems, local_sems = refs[2 * n:]
        x, y, c, sibling, _ = _place()
        half, other = pl.ds(HALF_LAYERS * c, HALF_LAYERS), pl.ds(HALF_LAYERS * (1 - c), HALF_LAYERS)
        local = [pltpu.make_async_copy(ins[i], outs[i].at[half], local_sems.at[i]) for i in range(n)]
        sends = [pltpu.make_async_remote_copy(src_ref=ins[i], dst_ref=outs[i].at[half], send_sem=send_sems.at[i],
                                              recv_sem=recv_sems.at[i], device_id=sibling, device_id_type=MESH) for i in range(n)]
        for cp in local + sends:
            cp.start()
        for i in range(n):
            pltpu.make_async_remote_copy(src_ref=ins[i], dst_ref=outs[i].at[other], send_sem=send_sems.at[i],
                                         recv_sem=recv_sems.at[i], device_id=sibling, device_id_type=MESH).wait_recv()
        for cp in sends:
            cp.wait_send()
        for cp in local:
            cp.wait()

    return pl.pallas_call(
        body, out_shape=[jax.ShapeDtypeStruct((DEPTH,) + t.shape[1:], t.dtype) for t in ts], in_specs=[ANY] * n, out_specs=[ANY] * n,
        scratch_shapes=_dma_sems(n) + [pltpu.SemaphoreType.DMA((n,))], name="grad_share_halves")(*ts)


PACK_W = 1024
PACK_TB = 512


def _sum_rows(name, parts, out_dtype):
    def f(*vals):
        acc = vals[0]
        for v in vals[1:]:
            acc = acc + v
        return (acc,)

    return _rowwise_fwd(name, f, [(p, PACK_W, 0) for p in parts], [], [(PACK_W, out_dtype)], tb=_tile(parts[0].shape[0], PACK_TB))[0]


def _sum_slots(name, ops, rows, out_dtype):
    cw = ops[0][0].shape[1]
    tb = rows
    while tb * cw > 300_000 and tb % 32 == 0:
        tb //= 2
    per = rows // tb

    def body(*refs):
        acc = refs[0][...].astype(F32)
        for r in refs[1:-1]:
            acc = acc + r[...].astype(F32)
        refs[-1][...] = acc.astype(refs[-1].dtype)

    return pl.pallas_call(
        body, grid=(per,), in_specs=[pl.BlockSpec((tb, cw), lambda i, s=s: (s * per + i, 0)) for _, s in ops],
        out_specs=pl.BlockSpec((tb, cw), lambda i: (i, 0)), out_shape=jax.ShapeDtypeStruct((rows, cw), out_dtype),
        compiler_params=_cp("parallel"), name=name)(*[a for a, _ in ops])


def _adamw(name, w, g, m, v):
    r, cw = w.shape
    tb = r
    while tb * cw > 400_000 and tb % 16 == 0:
        tb //= 2
    c1 = 1.0 / (1.0 - ADAM_B1 ** ADAM_STEP)
    c2 = 1.0 / (1.0 - ADAM_B2 ** ADAM_STEP)

    def body(w_ref, g_ref, m_ref, v_ref, d_ref, mo_ref, vo_ref):
        gg = g_ref[...]
        mn = ADAM_B1 * m_ref[...] + (1.0 - ADAM_B1) * gg
        vn = ADAM_B2 * v_ref[...] + (1.0 - ADAM_B2) * jnp.square(gg)
        d_ref[...] = -ADAM_LR * ((mn * c1) / (jnp.sqrt(vn * c2) + ADAM_EPS) + ADAM_WD * w_ref[...])
        mo_ref[...] = mn
        vo_ref[...] = vn

    spec = pl.BlockSpec((tb, cw), lambda i: (i, 0))
    return pl.pallas_call(
        body, grid=(r // tb,), in_specs=[spec] * 4, out_specs=[spec] * 3,
        out_shape=[jax.ShapeDtypeStruct((r, cw), F32)] * 3, compiler_params=_cp("parallel"), name=name)(w, g, m, v)


def _silu_rows(c):
    def body(c_ref, o_ref):
        rows = lax.broadcasted_iota(jnp.int32, o_ref.shape, 0)
        o_ref[...] = jnp.where(rows == 0, jnp.broadcast_to(_silu(c_ref[...]), o_ref.shape), 0.0)

    return pl.pallas_call(body, out_shape=jax.ShapeDtypeStruct((8, c.shape[1]), F32), name="cond_silu")(c)


_KINDS = ("w_in", "w_ssd_out", "w_attn_out", "w_pool_mix", "w_pool_out", "w_out", "w_ff1", "w_ff2")
_SMALL = ("b_ada", "norm1_w", "norm2_w", "conv_b", "dt_bias", "a_log", "d_skip", "ssd_norm_w", "q_norm_w", "k_norm_w",
          "pool_scale")
_ORDER = ("w_ada", "b_ada", "norm1_w", "norm2_w", "w_in", "conv_w", "conv_b", "dt_bias", "a_log", "d_skip", "ssd_norm_w",
          "w_ssd_out", "q_norm_w", "k_norm_w", "w_attn_out", "w_pool_mix", "pool_scale", "w_pool_out", "w_out", "w_ff1", "w_ff2")


def _pack_flat(arrs, rows, dtype):
    flat = jnp.concatenate([a.reshape(-1).astype(dtype) for a in arrs])
    return jnp.pad(flat, (0, rows * PACK_W - flat.shape[0])).reshape(rows, PACK_W)


def _unpack_flat(buf, shapes):
    flat = buf.reshape(-1)
    out, off = [], 0
    for shp in shapes:
        n = int(np.prod(shp))
        out.append(flat[off:off + n].reshape(shp))
        off += n
    return out


def _small_rows(n_elems):
    return -(-n_elems // (8 * PACK_W)) * 8


def kernel(x, c, w_ada, b_ada, norm1_w, norm2_w, w_in, conv_w, conv_b, dt_bias, a_log, d_skip, ssd_norm_w, w_ssd_out, q_norm_w, k_norm_w, w_attn_out, w_pool_mix, pool_scale, w_pool_out, w_out, w_ff1, w_ff2, loss_target, m_w_ada, m_b_ada, m_norm1_w, m_norm2_w, m_w_in, m_conv_w, m_conv_b, m_dt_bias, m_a_log, m_d_skip, m_ssd_norm_w, m_w_ssd_out, m_q_norm_w, m_k_norm_w, m_w_attn_out, m_w_pool_mix, m_pool_scale, m_w_pool_out, m_w_out, m_w_ff1, m_w_ff2, v_w_ada, v_b_ada, v_norm1_w, v_norm2_w, v_w_in, v_conv_w, v_conv_b, v_dt_bias, v_a_log, v_d_skip, v_ssd_norm_w, v_w_ssd_out, v_q_norm_w, v_k_norm_w, v_w_attn_out, v_w_pool_mix, v_pool_scale, v_w_pool_out, v_w_out, v_w_ff1, v_w_ff2):
    w = dict(w_ada=w_ada, b_ada=b_ada, norm1_w=norm1_w, norm2_w=norm2_w, w_in=w_in, conv_w=conv_w, conv_b=conv_b, dt_bias=dt_bias, a_log=a_log, d_skip=d_skip, ssd_norm_w=ssd_norm_w, w_ssd_out=w_ssd_out, q_norm_w=q_norm_w, k_norm_w=k_norm_w, w_attn_out=w_attn_out, w_pool_mix=w_pool_mix, pool_scale=pool_scale, w_pool_out=w_pool_out, w_out=w_out, w_ff1=w_ff1, w_ff2=w_ff2)
    m = dict(w_ada=m_w_ada, b_ada=m_b_ada, norm1_w=m_norm1_w, norm2_w=m_norm2_w, w_in=m_w_in, conv_w=m_conv_w, conv_b=m_conv_b, dt_bias=m_dt_bias, a_log=m_a_log, d_skip=m_d_skip, ssd_norm_w=m_ssd_norm_w, w_ssd_out=m_w_ssd_out, q_norm_w=m_q_norm_w, k_norm_w=m_k_norm_w, w_attn_out=m_w_attn_out, w_pool_mix=m_w_pool_mix, pool_scale=m_pool_scale, w_pool_out=m_w_pool_out, w_out=m_w_out, w_ff1=m_w_ff1, w_ff2=m_w_ff2)
    v = dict(w_ada=v_w_ada, b_ada=v_b_ada, norm1_w=v_norm1_w, norm2_w=v_norm2_w, w_in=v_w_in, conv_w=v_conv_w, conv_b=v_conv_b, dt_bias=v_dt_bias, a_log=v_a_log, d_skip=v_d_skip, ssd_norm_w=v_ssd_norm_w, w_ssd_out=v_w_ssd_out, q_norm_w=v_q_norm_w, k_norm_w=v_k_norm_w, w_attn_out=v_w_attn_out, w_pool_mix=v_w_pool_mix, pool_scale=v_pool_scale, w_pool_out=v_w_pool_out, w_out=v_w_out, w_ff1=v_w_ff1, w_ff2=v_w_ff2)
    chip = 2 * lax.axis_index("x") + lax.axis_index("y")
    dev = 2 * chip + lax.axis_index("c")
    ada_cols = w_ada.shape[2]

    n_conv = conv_w.size // PACK_W
    rows1 = _small_rows((1 + n_conv) * PACK_W)
    blk = jnp.concatenate([_silu_rows(c)[:1], conv_w.reshape(n_conv, PACK_W), jnp.zeros((rows1 - 1 - n_conv, PACK_W), F32)])
    first = _allgather8("gather_cond", blk).reshape(N_DEV, rows1, PACK_W)
    cond_all = first[:, 0]
    conv_all = first[0::2, 1:1 + n_conv].reshape((N_CHIPS,) + conv_w.shape)
    conv_full = jnp.moveaxis(conv_all, 0, 2).reshape(DEPTH, SSD_CONV, CONV_DIM)
    b_cols = lax.dynamic_slice_in_dim(b_ada, chip * ada_cols, ada_cols, axis=1)
    mod_cols = jnp.stack([_matmul("ada_fwd", cond_all, w_ada[l], "nn", precise=True) + b_cols[l][None, :] for l in range(DEPTH)])
    mod_all = _allgather8("gather_mod", mod_cols.reshape(-1, PACK_W)).reshape(N_DEV, DEPTH, N_DEV, ada_cols)
    mine = lax.dynamic_index_in_dim(mod_all[0::2], dev, axis=2, keepdims=False)
    mods = jnp.moveaxis(mine, 0, 1).reshape(DEPTH, 6, D_MODEL)

    ws = dict(zip(_KINDS, _gather_weights([w[k].astype(BF16) for k in _KINDS])))
    for k in ("w_ssd_out", "w_pool_out", "w_out", "w_ff2"):
        ws[k] = ws[k].reshape(DEPTH, -1, D_MODEL)
    ws["w_pool_mix"] = jnp.moveaxis(ws["w_pool_mix"], 1, 2).reshape(DEPTH, 4, POOL_GW, POOL_GW)
    ws["conv_w"] = conv_full
    for k in _SMALL[1:]:
        ws[k] = w[k]

    loss, grad_x, dmods, grads = _local_step(x[0], loss_target[0], mods, ws)

    small = ([dmods] + [jnp.stack([grads[l][k] for l in range(DEPTH)]) for k in _SMALL[1:] + ("conv_w",)] + [loss[:, :1]])
    n_small = sum(int(np.prod(a.shape)) for a in small)
    rows_small = _small_rows(n_small)
    small_all = _allgather8("gather_small", _pack_flat(small, rows_small, F32))
    parts = [small_all[d * rows_small:(d + 1) * rows_small] for d in range(N_DEV)]
    small_sum = _unpack_flat(_sum_rows("sum_small", parts, F32), [a.shape for a in small])
    g_out = {"b_ada": small_sum[0].reshape(DEPTH, 6 * D_MODEL)}
    for k, t in zip(_SMALL[1:], small_sum[1:-2]):
        g_out[k] = t
    g_out["conv_w"] = lax.dynamic_slice_in_dim(small_sum[-2], chip * conv_w.shape[2], conv_w.shape[2], axis=2)
    loss_out = small_sum[-1][0, 0]
    dmod_all = jnp.stack([p[:DEPTH * 6].reshape(DEPTH, 6 * D_MODEL) for p in parts])
    dmod_cols = lax.dynamic_slice_in_dim(dmod_all, chip * ada_cols, ada_cols, axis=2)
    g_out["w_ada"] = jnp.stack([_matmul("ada_dw", cond_all, dmod_cols[:, l], "tn", precise=True) for l in range(DEPTH)])

    gs = [jnp.stack([grads[l][k] for l in range(DEPTH)], axis=1) for k in _KINDS]
    got, own = _swap_halves(gs)

    def flat(t):
        return t.reshape(-1, t.shape[-1])

    pairs = [_sum_slots("sum_pair_" + k, [(flat(a), 0), (flat(b), 0)], flat(a).shape[0], BF16).reshape(a.shape)
             for k, a, b in zip(_KINDS, own, got)]
    partials = _exchange_chips(pairs)
    totals = [_sum_slots("sum_chips_" + k, [(flat(p), s) for s in (3, 0, 1, 2)], flat(p).shape[0] // N_CHIPS, F32).reshape(p.shape[1:])
              for k, p in zip(_KINDS, partials)]
    for k, t in zip(_KINDS, _share_halves(totals)):
        g_out[k] = t

    deltas, new_m, new_v = {}, {}, {}
    for k in ("w_ada", "conv_w") + _KINDS:
        shp = w[k].shape
        two_d = (int(np.prod(shp[:-1])), shp[-1])
        res = _adamw("adamw_" + k, *(t.reshape(two_d) for t in (w[k], g_out[k], m[k], v[k])))
        deltas[k], new_m[k], new_v[k] = (t.reshape(shp) for t in res)
    small_shapes = [w[k].shape for k in _SMALL]
    n_sm = sum(int(np.prod(s)) for s in small_shapes)
    res = _adamw("adamw_small", *[_pack_flat([t[k] for k in _SMALL], _small_rows(n_sm), F32) for t in (w, g_out, m, v)])
    for name_map, buf in zip((deltas, new_m, new_v), res):
        for k, t in zip(_SMALL, _unpack_flat(buf, small_shapes)):
            name_map[k] = t

    return (loss_out, grad_x[None], *[g_out[k] for k in _ORDER], *[deltas[k] for k in _ORDER],
            *[new_m[k] for k in _ORDER], *[new_v[k] for k in _ORDER])
```

```python
import functools
import math

import numpy as np
import jax
import jax.numpy as jnp
from jax import lax
from jax.experimental import pallas as pl
from jax.experimental.pallas import tpu as pltpu

F32, BF16 = jnp.float32, jnp.bfloat16
MESH = pl.DeviceIdType.MESH

D_MODEL = 1024
DEPTH = 4
N_CHIPS = 4
N_DEV = 8
SSD_HEADS = 16
SSD_HEAD_DIM = 64
SSD_STATE = 128
SSD_CHUNK = 128
SSD_CONV = 4
CONV_DIM = 1536
ATTN_HEAD_DIM = 128
ATTN_GROUP_W = 512
DILATIONS = (1, 4, 16)
ATTN_STEPS = 128
POOL_WINDOWS = (2, 4, 8, 16)
POOL_GW = 256
D_FF = 4096
EPS = 1e-6
IN_SIZES = (1024, 1536, 16, 1536, 1536, 1536, 1024, 3072)
IN_WIDTH = sum(IN_SIZES)
P_XBC, P_Q, P_K, P_V, P_GATES, P_Z, P_U, P_DT = 0, 1536, 3072, 4608, 6144, 9216, 10240, 11264
P_WIDTH = 12288
LANES = 128
NEG = -1e30
VMEM_LIMIT = 56 * 1024 * 1024

ADAM_LR, ADAM_B1, ADAM_B2, ADAM_EPS, ADAM_WD, ADAM_STEP = 0.001, 0.9, 0.999, 1e-08, 0.01, 10


def _alibi_slopes(n):
    def pow2(k):
        start = 2.0 ** (-8.0 / k)
        return [start ** (i + 1) for i in range(k)]
    if math.log2(n).is_integer():
        s = pow2(n)
    else:
        c = 2 ** math.floor(math.log2(n))
        s = pow2(c) + pow2(2 * c)[0::2][: n - c]
    return np.sort(np.asarray(s, np.float32))[::-1].copy()


SLOPES = _alibi_slopes(12).reshape(3, 4)


def _cp(*sem):
    return pltpu.CompilerParams(dimension_semantics=sem, vmem_limit_bytes=VMEM_LIMIT)


_DIMS = {"nn": (((1,), (0,)), ((), ())), "nt": (((1,), (1,)), ((), ())), "tn": (((0,), (0,)), ((), ()))}


def _dot(a, b, mode):
    return lax.dot_general(a.astype(BF16), b.astype(BF16), _DIMS[mode], preferred_element_type=F32)


@functools.partial(jax.custom_vjp, nondiff_argnums=(2,))
def _bdot(a, b, mode):
    return _dot(a, b, mode)


def _bdot_fwd(a, b, mode):
    return _dot(a, b, mode), (a, b)


def _bdot_bwd(mode, res, ct):
    a, b = res
    if mode == "nn":
        return _dot(ct, b, "nt"), _dot(a, ct, "tn")
    if mode == "nt":
        return _dot(ct, b, "nn"), _dot(ct, a, "tn")
    return _dot(b, ct, "nt"), _dot(a, ct, "nn")


_bdot.defvjp(_bdot_fwd, _bdot_bwd)


def _hdot(a, b):
    return jnp.dot(a, b, precision=lax.Precision.HIGHEST, preferred_element_type=F32)


def _tri(n, lower):
    r = lax.broadcasted_iota(jnp.int32, (n, n), 0)
    c = lax.broadcasted_iota(jnp.int32, (n, n), 1)
    return (r >= c if lower else r <= c).astype(F32)


@jax.custom_vjp
def _csum(a):
    return _hdot(_tri(a.shape[0], True), a)


def _csum_fwd(a):
    return _csum(a), None


def _csum_bwd(_, ct):
    return (_hdot(_tri(ct.shape[0], False), ct),)


_csum.defvjp(_csum_fwd, _csum_bwd)


def _softplus(x):
    return jnp.maximum(x, 0.0) + jnp.log(1.0 + jnp.exp(-jnp.abs(x)))


def _sigmoid(x):
    return 1.0 / (1.0 + jnp.exp(-x))


def _silu(x):
    return x * _sigmoid(x)


def _tile(n, cap):
    t = min(n, cap)
    while n % t:
        t //= 2
    return t


MM_TILE, MM_KTILE = 1024, 2048


def _matmul(name, a, b, mode, out_dtype=F32, precise=False, layer=None, chips=0, out_chips=0):
    if mode == "nn":
        (m, k), n = a.shape, (4 * chips if chips else b.shape[-1])
    elif mode == "nt":
        (m, k), n = a.shape, b.shape[-2]
    else:
        (k, m), n = a.shape, b.shape[-1]
    tm = _tile(m, MM_TILE)
    tn = _tile(chips if (chips and mode == "nn") else (out_chips or n), MM_TILE)
    tk = _tile(chips if (chips and mode == "nt") else k, MM_KTILE)
    nk = k // tk
    a_spec = pl.BlockSpec((tk, tm), lambda i, j, l: (l, i)) if mode == "tn" else pl.BlockSpec((tm, tk), lambda i, j, l: (i, l))
    if chips:
        if mode == "nn":
            per = chips // tn
            b_spec = pl.BlockSpec((None, None, tk, tn), lambda i, j, l: (layer, j // per, l, j % per))
        else:
            per = chips // tk
            b_spec = pl.BlockSpec((None, None, tn, tk), lambda i, j, l: (layer, l // per, j, l % per))
    elif layer is not None:
        b_spec = (pl.BlockSpec((None, tn, tk), lambda i, j, l: (layer, j, l)) if mode == "nt"
                  else pl.BlockSpec((None, tk, tn), lambda i, j, l: (layer, l, j)))
    else:
        b_spec = pl.BlockSpec((tn, tk), lambda i, j, l: (j, l)) if mode == "nt" else pl.BlockSpec((tk, tn), lambda i, j, l: (l, j))
    if out_chips:
        per_o = out_chips // tn
        o_spec = pl.BlockSpec((None, tm, tn), lambda i, j, l: (j // per_o, i, j % per_o))
        o_shape = jax.ShapeDtypeStruct((N_CHIPS, m, out_chips), out_dtype)
    else:
        o_spec = pl.BlockSpec((tm, tn), lambda i, j, l: (i, j))
        o_shape = jax.ShapeDtypeStruct((m, n), out_dtype)

    def part(a_ref, b_ref):
        if precise:
            return lax.dot_general(a_ref[...], b_ref[...], _DIMS[mode], precision=lax.Precision.HIGHEST,
                                   preferred_element_type=F32)
        return _dot(a_ref[...], b_ref[...], mode)

    if nk == 1:
        def body(a_ref, b_ref, o_ref):
            o_ref[...] = part(a_ref, b_ref).astype(o_ref.dtype)
        scratch = []
    else:
        def body(a_ref, b_ref, o_ref, acc_ref):
            l = pl.program_id(2)
            p = part(a_ref, b_ref)

            @pl.when(l == 0)
            def _():
                acc_ref[...] = p

            @pl.when((l > 0) & (l < nk - 1))
            def _():
                acc_ref[...] += p

            @pl.when(l == nk - 1)
            def _():
                o_ref[...] = (acc_ref[...] + p).astype(o_ref.dtype)
        scratch = [pltpu.VMEM((tm, tn), F32)]

    return pl.pallas_call(
        body, grid=(m // tm, n // tn, nk), in_specs=[a_spec, b_spec], out_specs=o_spec, out_shape=o_shape,
        scratch_shapes=scratch, compiler_params=_cp("parallel", "parallel", "arbitrary"), name=name)(a, b)


def _group_matmul(name, a, w, mode, out_dtype=F32, layer=0):
    s = a.shape[0]
    tb = 512
    gw = POOL_GW
    if mode == "tn":
        def body(a_ref, b_ref, o_ref):
            part = _dot(a_ref[...], b_ref[...], "tn")

            @pl.when(pl.program_id(1) == 0)
            def _():
                o_ref[0] = part

            @pl.when(pl.program_id(1) > 0)
            def _():
                o_ref[0] += part

        return pl.pallas_call(
            body, grid=(4, s // tb),
            in_specs=[pl.BlockSpec((tb, gw), lambda g, i: (i, g)), pl.BlockSpec((tb, gw), lambda g, i: (i, g))],
            out_specs=pl.BlockSpec((1, gw, gw), lambda g, i: (g, 0, 0)),
            out_shape=jax.ShapeDtypeStruct((4, gw, gw), F32),
            compiler_params=_cp("parallel", "arbitrary"), name=name)(a, w)

    def body(a_ref, w_ref, o_ref):
        o_ref[...] = _dot(a_ref[...], w_ref[...], mode).astype(o_ref.dtype)

    return pl.pallas_call(
        body, grid=(s // tb, 4),
        in_specs=[pl.BlockSpec((tb, gw), lambda i, g: (i, g)), pl.BlockSpec((None, None, gw, gw), lambda i, g: (layer, g, 0, 0))],
        out_specs=pl.BlockSpec((tb, gw), lambda i, g: (i, g)),
        out_shape=jax.ShapeDtypeStruct((s, 4 * gw), out_dtype),
        compiler_params=_cp("parallel", "parallel"), name=name)(a, w)


def _rspec(tb, width, cb):
    return pl.BlockSpec((tb, width), lambda i: (i, cb))


def _pspec(shape):
    return pl.BlockSpec(shape, lambda i: (0, 0))


def _rowwise_fwd(name, f, rows, pars, outs, tb=256):
    s = rows[0][0].shape[0]
    nin = len(rows) + len(pars)

    def body(*refs):
        res = f(*[r[...].astype(F32) for r in refs[:nin]])
        for o, v in zip(refs[nin:], res):
            o[...] = v.astype(o.dtype)

    return pl.pallas_call(
        body, grid=(s // tb,),
        in_specs=[_rspec(tb, w, cb) for _, w, cb in rows] + [_pspec(p.shape) for p in pars],
        out_specs=[_rspec(tb, w, 0) for w, _ in outs],
        out_shape=[jax.ShapeDtypeStruct((s, w), dt) for w, dt in outs],
        compiler_params=_cp("parallel"), name=name)(*[r[0] for r in rows], *pars)


def _rowwise_bwd(name, f, rows, pars, cts, need, add=None, tb=256, gdt=None):
    s = rows[0][0].shape[0]
    nr, npar, nc = len(rows), len(pars), len(cts)
    nin = nr + npar + nc + (1 if add is not None else 0)

    def body(*refs):
        ins = [r[...].astype(F32) for r in refs[:nr + npar]]
        _, vjp = jax.vjp(f, *ins)
        g = vjp(tuple(c[...].astype(F32) for c in refs[nr + npar:nr + npar + nc]))
        outs = refs[nin:]
        k = 0
        for j in range(nr):
            if need[j]:
                v = g[j]
                if add is not None and add[0] == j:
                    v = v + refs[nin - 1][...]
                outs[k][...] = v.astype(outs[k].dtype)
                k += 1
        first = pl.program_id(0) == 0
        for j in range(npar):
            o, v = outs[k + j], g[nr + j]

            @pl.when(first)
            def _(o=o, v=v):
                o[...] = v

            @pl.when(jnp.logical_not(first))
            def _(o=o, v=v):
                o[...] += v

    in_specs = ([_rspec(tb, w, cb) for _, w, cb in rows] + [_pspec(p.shape) for p in pars]
                + [_rspec(tb, w, cb) for _, w, cb in cts])
    args = [r[0] for r in rows] + list(pars) + [c[0] for c in cts]
    if add is not None:
        in_specs.append(_rspec(tb, rows[add[0]][1], 0))
        args.append(add[1])
    gr = [(w, F32) for (_, w, _), nd in zip(rows, need) if nd]
    if gdt is not None:
        gr = [(w, dt) for (w, _), dt in zip(gr, gdt)]
    return pl.pallas_call(
        body, grid=(s // tb,), in_specs=in_specs,
        out_specs=[_rspec(tb, w, 0) for w, _ in gr] + [_pspec(p.shape) for p in pars],
        out_shape=[jax.ShapeDtypeStruct((s, w), dt) for w, dt in gr] + [jax.ShapeDtypeStruct(p.shape, F32) for p in pars],
        compiler_params=_cp("arbitrary"), name=name)(*args)


def _f_norm(x, nw, sc, sh):
    r = lax.rsqrt(jnp.mean(x * x, axis=-1, keepdims=True) + EPS)
    return ((x * r * nw) * (1.0 + sc) + sh,)


def _f_ssdgate(y, z, w):
    y2 = y * _silu(z)
    low = lax.broadcasted_iota(jnp.int32, y2.shape, 1) < 512
    sq = y2 * y2
    m0 = jnp.sum(jnp.where(low, sq, 0.0), axis=-1, keepdims=True) / 512.0
    m1 = jnp.sum(jnp.where(low, 0.0, sq), axis=-1, keepdims=True) / 512.0
    r = jnp.where(low, lax.rsqrt(m0 + EPS), lax.rsqrt(m1 + EPS))
    return (y2 * r * w,)


def _head_rms(t, w):
    outs = []
    for h in range(t.shape[1] // ATTN_HEAD_DIM):
        th = t[:, h * ATTN_HEAD_DIM:(h + 1) * ATTN_HEAD_DIM]
        outs.append(th * lax.rsqrt(jnp.mean(th * th, axis=-1, keepdims=True) + EPS) * w)
    return jnp.concatenate(outs, axis=1)


def _f_qknorm(q, k, v, qw, kw):
    return _head_rms(q, qw), _head_rms(k, kw), v


def _f_combine(o1, o2, o3, l1, l2, l3):
    m = lax.stop_gradient(jnp.maximum(jnp.maximum(l1, l2), l3))
    e1, e2, e3 = jnp.exp(l1 - m), jnp.exp(l2 - m), jnp.exp(l3 - m)
    return ((e1 * o1 + e2 * o2 + e3 * o3) / (e1 + e2 + e3),)


def _f_poolscale(pm, ps):
    return (pm * ps,)


def _f_merge(gates, ys, ya, yp):
    g = _sigmoid(gates)
    return (g[:, 0:1024] * ys + g[:, 1024:2048] * ya + g[:, 2048:3072] * yp,)


def _f_resid(x, o, g):
    return (x + g * o,)


def _f_relu2(a):
    return (jnp.square(jnp.maximum(a, 0.0)),)


def _loss_and_grad(y, tgt, tb=512):
    s, d = y.shape

    def body(y_ref, t_ref, dy_ref, l_ref):
        e = y_ref[...] - t_ref[...]
        dy_ref[...] = e * (1.0 / d)
        part = jnp.zeros((1, LANES), F32) + jnp.sum(e * e) * (0.5 / d)

        @pl.when(pl.program_id(0) == 0)
        def _():
            l_ref[...] = part

        @pl.when(pl.program_id(0) > 0)
        def _():
            l_ref[...] += part

    return pl.pallas_call(
        body, grid=(s // tb,), in_specs=[_rspec(tb, d, 0), _rspec(tb, d, 0)],
        out_specs=[_rspec(tb, d, 0), _pspec((1, LANES))],
        out_shape=[jax.ShapeDtypeStruct((s, d), F32), jax.ShapeDtypeStruct((1, LANES), F32)],
        compiler_params=_cp("arbitrary"), name="loss")(y, tgt)


def _shift_down(x, j):
    rows = lax.broadcasted_iota(jnp.int32, x.shape, 0)
    return jnp.where(rows < j, 0.0, pltpu.roll(x, j, 0))


def _shift_up(x, j):
    s = x.shape[0]
    rows = lax.broadcasted_iota(jnp.int32, x.shape, 0)
    return jnp.where(rows >= s - j, 0.0, pltpu.roll(x, s - j, 0))


CONV_CB = 256


def _conv_pre(x, w_ref, b_ref):
    acc = b_ref[...] + w_ref[SSD_CONV - 1:SSD_CONV, :] * x
    for j in range(1, SSD_CONV):
        acc = acc + w_ref[SSD_CONV - 1 - j:SSD_CONV - j, :] * _shift_down(x, j)
    return acc


def _conv_fwd(proj, cw, cb):
    s = proj.shape[0]

    def body(x_ref, w_ref, b_ref, o_ref):
        o_ref[...] = _silu(_conv_pre(x_ref[...], w_ref, b_ref))

    return pl.pallas_call(
        body, grid=(CONV_DIM // CONV_CB,),
        in_specs=[pl.BlockSpec((s, CONV_CB), lambda i: (0, P_XBC // CONV_CB + i)),
                  pl.BlockSpec((SSD_CONV, CONV_CB), lambda i: (0, i)), pl.BlockSpec((1, CONV_CB), lambda i: (0, i))],
        out_specs=pl.BlockSpec((s, CONV_CB), lambda i: (0, i)),
        out_shape=jax.ShapeDtypeStruct((s, CONV_DIM), F32), compiler_params=_cp("parallel"), name="conv_fwd")(proj, cw, cb)


def _conv_bwd(proj, cw, cb, dout):
    s = proj.shape[0]

    def body(x_ref, w_ref, b_ref, d_ref, dx_ref, dw_ref, db_ref):
        x = x_ref[...]
        a = _conv_pre(x, w_ref, b_ref)
        sg = _sigmoid(a)
        da = d_ref[...] * (sg + a * sg * (1.0 - sg))
        db_ref[...] = jnp.sum(da, axis=0, keepdims=True)
        dx = w_ref[SSD_CONV - 1:SSD_CONV, :] * da
        dw_ref[SSD_CONV - 1:SSD_CONV, :] = jnp.sum(da * x, axis=0, keepdims=True)
        for j in range(1, SSD_CONV):
            dx = dx + w_ref[SSD_CONV - 1 - j:SSD_CONV - j, :] * _shift_up(da, j)
            dw_ref[SSD_CONV - 1 - j:SSD_CONV - j, :] = jnp.sum(da * _shift_down(x, j), axis=0, keepdims=True)
        dx_ref[...] = dx.astype(dx_ref.dtype)

    return pl.pallas_call(
        body, grid=(CONV_DIM // CONV_CB,),
        in_specs=[pl.BlockSpec((s, CONV_CB), lambda i: (0, P_XBC // CONV_CB + i)),
                  pl.BlockSpec((SSD_CONV, CONV_CB), lambda i: (0, i)), pl.BlockSpec((1, CONV_CB), lambda i: (0, i)),
                  pl.BlockSpec((s, CONV_CB), lambda i: (0, i))],
        out_specs=[pl.BlockSpec((s, CONV_CB), lambda i: (0, i)), pl.BlockSpec((SSD_CONV, CONV_CB), lambda i: (0, i)),
                   pl.BlockSpec((1, CONV_CB), lambda i: (0, i))],
        out_shape=[jax.ShapeDtypeStruct((s, CONV_DIM), BF16), jax.ShapeDtypeStruct((SSD_CONV, CONV_DIM), F32),
                   jax.ShapeDtypeStruct((1, CONV_DIM), F32)],
        compiler_params=_cp("parallel"), name="conv_bwd")(proj, cw, cb, dout)


def _pool_window_sum(x, g, shift):
    s2 = x + shift(x, 1)
    s4 = s2 + shift(s2, 2)
    s8 = s4 + shift(s4, 4)
    s16 = s8 + shift(s8, 8)
    return jnp.where(g == 0, s2, jnp.where(g == 1, s4, jnp.where(g == 2, s8, s16)))


def _pool_count(shape, g):
    rows = lax.broadcasted_iota(jnp.int32, shape, 0)
    return jnp.minimum(rows + 1, jnp.left_shift(2, g)).astype(F32)


def _pool_fwd(proj):
    s = proj.shape[0]

    def body(u_ref, o_ref):
        g = pl.program_id(0)
        u = u_ref[...]
        o_ref[...] = (_pool_window_sum(u, g, _shift_down) / _pool_count(u.shape, g) - u).astype(o_ref.dtype)

    return pl.pallas_call(
        body, grid=(4,), in_specs=[pl.BlockSpec((s, POOL_GW), lambda g: (0, P_U // POOL_GW + g))],
        out_specs=pl.BlockSpec((s, POOL_GW), lambda g: (0, g)),
        out_shape=jax.ShapeDtypeStruct((s, 4 * POOL_GW), BF16), compiler_params=_cp("parallel"), name="pool_fwd")(proj)


def _pool_bwd(dp):
    s = dp.shape[0]

    def body(d_ref, o_ref):
        g = pl.program_id(0)
        d = d_ref[...]
        o_ref[...] = (_pool_window_sum(d / _pool_count(d.shape, g), g, _shift_up) - d).astype(o_ref.dtype)

    return pl.pallas_call(
        body, grid=(4,), in_specs=[pl.BlockSpec((s, POOL_GW), lambda g: (0, g))],
        out_specs=pl.BlockSpec((s, POOL_GW), lambda g: (0, g)),
        out_shape=jax.ShapeDtypeStruct((s, 4 * POOL_GW), BF16), compiler_params=_cp("parallel"), name="pool_bwd")(dp)


N_PAIRS = SSD_HEADS // 2
STATE_ROWS = N_PAIRS * SSD_STATE


def _ssd_chunk(xbc, dtr, hprev, dtb, alog, dsk):
    L = xbc.shape[0]
    xs, bm, cm = xbc[:, 0:1024], xbc[:, 1024:1280], xbc[:, 1280:1536]
    dt = _softplus(dtr + dtb)
    a = dt * (-jnp.exp(alog))
    acum = _csum(a)
    alast = jnp.sum(a, axis=0, keepdims=True)
    xdt = xs * dt
    xdecay = xdt * jnp.exp(alast - acum)
    eacum = jnp.exp(acum)
    elast = jnp.exp(alast)
    cb = [_bdot(cm[:, g * 128:(g + 1) * 128], bm[:, g * 128:(g + 1) * 128], "nt") for g in range(2)]
    rows = lax.broadcasted_iota(jnp.int32, (L, L), 0)
    cols = lax.broadcasted_iota(jnp.int32, (L, L), 1)
    causal = rows >= cols
    lane = lax.broadcasted_iota(jnp.int32, (L, LANES), 1)
    sub = lax.broadcasted_iota(jnp.int32, (LANES, L), 0)
    ys, hs = [], []
    for p in range(N_PAIRS):
        g = p // (N_PAIRS // 2)
        sl = slice(p * LANES, (p + 1) * LANES)
        ac = acum[:, sl]
        act = ac.T
        xp = xdt[:, sl]
        hp = hprev[p * SSD_STATE:(p + 1) * SSD_STATE, :]
        y = _bdot(cm[:, g * 128:(g + 1) * 128], hp, "nn") * eacum[:, sl] + dsk[:, sl] * xs[:, sl]
        for half in range(2):
            l0 = half * SSD_HEAD_DIM
            col = jnp.sum(jnp.where(lane == l0, ac, 0.0), axis=1, keepdims=True)
            row = jnp.sum(jnp.where(sub == l0, act, 0.0), axis=0, keepdims=True)
            decay = jnp.exp(jnp.where(causal, col - row, NEG))
            xh = jnp.where((lane >= l0) & (lane < l0 + SSD_HEAD_DIM), xp, 0.0)
            y = y + _bdot(cb[g] * decay, xh, "nn")
        ys.append(y)
        hs.append(elast[:, sl] * hp + _bdot(bm[:, g * 128:(g + 1) * 128], xdecay[:, sl], "tn"))
    return tuple(ys), tuple(hs)


def _ssd_fwd(xbc, proj, dtb, alog, dsk):
    s = xbc.shape[0]
    nc = s // SSD_CHUNK

    def body(x_ref, dt_ref, b_ref, a_ref, d_ref, y_ref, hist_ref, h_ref):
        @pl.when(pl.program_id(0) == 0)
        def _():
            h_ref[...] = jnp.zeros_like(h_ref)

        hprev = h_ref[...]
        hist_ref[...] = hprev
        ys, hs = _ssd_chunk(x_ref[...], dt_ref[...], hprev, b_ref[...], a_ref[...], d_ref[...])
        for p in range(N_PAIRS):
            y_ref[:, p * LANES:(p + 1) * LANES] = ys[p]
            h_ref[p * SSD_STATE:(p + 1) * SSD_STATE, :] = hs[p]

    return pl.pallas_call(
        body, grid=(nc,),
        in_specs=[pl.BlockSpec((SSD_CHUNK, CONV_DIM), lambda i: (i, 0)),
                  pl.BlockSpec((SSD_CHUNK, 1024), lambda i: (i, P_DT // 1024)),
                  _pspec((1, 1024)), _pspec((1, 1024)), _pspec((1, 1024))],
        out_specs=[pl.BlockSpec((SSD_CHUNK, 1024), lambda i: (i, 0)), pl.BlockSpec((STATE_ROWS, LANES), lambda i: (i, 0))],
        out_shape=[jax.ShapeDtypeStruct((s, 1024), F32), jax.ShapeDtypeStruct((nc * STATE_ROWS, LANES), F32)],
        scratch_shapes=[pltpu.VMEM((STATE_ROWS, LANES), F32)],
        compiler_params=_cp("arbitrary"), name="ssd_fwd")(xbc, proj, dtb, alog, dsk)


def _ssd_bwd(xbc, proj, hist, dtb, alog, dsk, dy):
    s = xbc.shape[0]
    nc = s // SSD_CHUNK

    def body(x_ref, dt_ref, hist_ref, b_ref, a_ref, d_ref, dy_ref, dx_ref, ddt_ref, db_ref, da_ref, dd_ref, dh_ref):
        first = pl.program_id(0) == 0

        @pl.when(first)
        def _():
            dh_ref[...] = jnp.zeros_like(dh_ref)

        _, vjp = jax.vjp(_ssd_chunk, x_ref[...], dt_ref[...], hist_ref[...], b_ref[...], a_ref[...], d_ref[...])
        dys = tuple(dy_ref[:, p * LANES:(p + 1) * LANES] for p in range(N_PAIRS))
        dhs = tuple(dh_ref[p * SSD_STATE:(p + 1) * SSD_STATE, :] for p in range(N_PAIRS))
        dx, ddt, dhp, db, da, dd = vjp((dys, dhs))
        dx_ref[...] = dx
        ddt_ref[...] = ddt.astype(ddt_ref.dtype)
        dh_ref[...] = dhp
        for o, v in ((db_ref, db), (da_ref, da), (dd_ref, dd)):
            @pl.when(first)
            def _(o=o, v=v):
                o[...] = v

            @pl.when(jnp.logical_not(first))
            def _(o=o, v=v):
                o[...] += v

    rev = lambda i: (nc - 1 - i, 0)
    return pl.pallas_call(
        body, grid=(nc,),
        in_specs=[pl.BlockSpec((SSD_CHUNK, CONV_DIM), rev),
                  pl.BlockSpec((SSD_CHUNK, 1024), lambda i: (nc - 1 - i, P_DT // 1024)),
                  pl.BlockSpec((STATE_ROWS, LANES), rev),
                  _pspec((1, 1024)), _pspec((1, 1024)), _pspec((1, 1024)),
                  pl.BlockSpec((SSD_CHUNK, 1024), rev)],
        out_specs=[pl.BlockSpec((SSD_CHUNK, CONV_DIM), rev), pl.BlockSpec((SSD_CHUNK, 1024), rev),
                   _pspec((1, 1024)), _pspec((1, 1024)), _pspec((1, 1024))],
        out_shape=[jax.ShapeDtypeStruct((s, CONV_DIM), F32), jax.ShapeDtypeStruct((s, 1024), BF16)]
        + [jax.ShapeDtypeStruct((1, 1024), F32)] * 3,
        scratch_shapes=[pltpu.VMEM((STATE_ROWS, LANES), F32)],
        compiler_params=_cp("arbitrary"), name="ssd_bwd")(xbc, proj, hist, dtb, alog, dsk, dy)


def _attn_head(q, kp, kc, vp, vc, has_prev, slope):
    scale = ATTN_HEAD_DIM ** -0.5
    n = ATTN_STEPS
    qi = lax.broadcasted_iota(jnp.int32, (n, n), 0)
    kj = lax.broadcasted_iota(jnp.int32, (n, n), 1)
    sp = jnp.where((kj >= qi) & has_prev, _bdot(q, kp, "nt") * scale - slope * (qi + n - kj).astype(F32), NEG)
    sc = jnp.where(kj <= qi, _bdot(q, kc, "nt") * scale - slope * (qi - kj).astype(F32), NEG)
    m = lax.stop_gradient(jnp.maximum(jnp.max(sp, axis=1, keepdims=True), jnp.max(sc, axis=1, keepdims=True)))
    pp, pc = jnp.exp(sp - m), jnp.exp(sc - m)
    den = jnp.sum(pp, axis=1, keepdims=True) + jnp.sum(pc, axis=1, keepdims=True)
    o = (_bdot(pp, vp, "nn") + _bdot(pc, vc, "nn")) / den
    return o, jnp.broadcast_to(m + jnp.log(den), (n, ATTN_HEAD_DIM))


def _head_slope(gi, h):
    s = [float(v) * DILATIONS[gi] for v in SLOPES[gi]]
    return jnp.where(h == 0, s[0], jnp.where(h == 1, s[1], jnp.where(h == 2, s[2], s[3])))


ATTN_UNROLL = 4


def _attn_heads_per_block(d):
    return 4 if d == 1 else 1


def _units(ref, d, hb):
    if d == 1:
        return [ref[:, h * ATTN_HEAD_DIM:(h + 1) * ATTN_HEAD_DIM] for h in range(hb)]
    return [ref[pl.ds(r, ATTN_STEPS, stride=d), :] for r in range(d)]


def _store_units(ref, src, d, hb):
    if d == 1:
        for h in range(hb):
            ref[:, h * ATTN_HEAD_DIM:(h + 1) * ATTN_HEAD_DIM] = src[h]
    else:
        for r in range(d):
            ref[pl.ds(r, ATTN_STEPS, stride=d), :] = src[r]


def _attn_fwd(qn, kn, vv, gi):
    d = DILATIONS[gi]
    s = qn.shape[0]
    span = ATTN_STEPS * d
    nb = s // span

    hb, units = _attn_heads_per_block(d), _attn_heads_per_block(d) * d

    def body(q_ref, k_ref, v_ref, o_ref, l_ref, sq, sk, sv, so, sl):
        h0, b = pl.program_id(0) * hb, pl.program_id(1)
        cur, prev = b % 2, (b + 1) % 2

        @pl.when(b == 0)
        def _():
            sk[prev] = jnp.zeros(sk.shape[1:], F32)
            sv[prev] = jnp.zeros(sv.shape[1:], F32)

        for u, (qr, kr, vr) in enumerate(zip(_units(q_ref, d, hb), _units(k_ref, d, hb), _units(v_ref, d, hb))):
            sq[u] = qr
            sk[cur, u] = kr
            sv[cur, u] = vr

        def step(i, carry):
            for j in range(ATTN_UNROLL):
                u = i * ATTN_UNROLL + j
                so[u], sl[u] = _attn_head(sq[u], sk[prev, u], sk[cur, u], sv[prev, u], sv[cur, u], b > 0,
                                          _head_slope(gi, h0 + u // d))
            return carry

        lax.fori_loop(0, units // ATTN_UNROLL, step, 0)
        _store_units(o_ref, so, d, hb)
        _store_units(l_ref, sl, d, hb)

    blk = pl.BlockSpec((span, hb * ATTN_HEAD_DIM), lambda h, b: (b, (gi * 4) // hb + h))
    out = pl.BlockSpec((span, hb * ATTN_HEAD_DIM), lambda h, b: (b, h))
    res = (units, ATTN_STEPS, ATTN_HEAD_DIM)
    return pl.pallas_call(
        body, grid=(4 // hb, nb), in_specs=[blk, blk, blk], out_specs=[out, out],
        out_shape=[jax.ShapeDtypeStruct((s, ATTN_GROUP_W), F32)] * 2,
        scratch_shapes=[pltpu.VMEM(res, F32), pltpu.VMEM((2,) + res, F32), pltpu.VMEM((2,) + res, F32),
                        pltpu.VMEM(res, F32), pltpu.VMEM(res, F32)],
        compiler_params=_cp("parallel", "arbitrary"), name=f"attn_fwd_g{gi}")(qn, kn, vv)


def _attn_bwd(qn, kn, vv, do, dl, gi):
    d = DILATIONS[gi]
    s = qn.shape[0]
    span = ATTN_STEPS * d
    nb = s // span

    hb, units = _attn_heads_per_block(d), _attn_heads_per_block(d) * d

    def body(q_ref, kp_ref, kc_ref, vp_ref, vc_ref, do_ref, dl_ref, dq_ref, dk_ref, dv_ref, sin, sout, ck, cv):
        h0, bi = pl.program_id(0) * hb, pl.program_id(1)

        @pl.when(bi == 0)
        def _():
            ck[...] = jnp.zeros_like(ck)
            cv[...] = jnp.zeros_like(cv)

        for i, ref in enumerate((q_ref, kp_ref, kc_ref, vp_ref, vc_ref, do_ref, dl_ref)):
            for u, val in enumerate(_units(ref, d, hb)):
                sin[i, u] = val
        has_prev = bi < nb - 1

        def step(i, carry):
            for j in range(ATTN_UNROLL):
                u = i * ATTN_UNROLL + j
                f = functools.partial(_attn_head, has_prev=has_prev, slope=_head_slope(gi, h0 + u // d))
                _, vjp = jax.vjp(f, sin[0, u], sin[1, u], sin[2, u], sin[3, u], sin[4, u])
                dq, dkp, dkc, dvp, dvc = vjp((sin[5, u], sin[6, u]))
                sout[0, u] = dq
                sout[1, u] = dkc + ck[u]
                sout[2, u] = dvc + cv[u]
                ck[u] = dkp
                cv[u] = dvp
            return carry

        lax.fori_loop(0, units // ATTN_UNROLL, step, 0)
        for i, ref in enumerate((dq_ref, dk_ref, dv_ref)):
            _store_units(ref, sout.at[i], d, hb)

    w = hb * ATTN_HEAD_DIM
    cur = pl.BlockSpec((span, w), lambda h, b: (nb - 1 - b, (gi * 4) // hb + h))
    prev = pl.BlockSpec((span, w), lambda h, b: (jnp.maximum(nb - 2 - b, 0), (gi * 4) // hb + h))
    out = pl.BlockSpec((span, w), lambda h, b: (nb - 1 - b, h))
    res = (units, ATTN_STEPS, ATTN_HEAD_DIM)
    return pl.pallas_call(
        body, grid=(4 // hb, nb), in_specs=[cur, prev, cur, prev, cur, out, out], out_specs=[out, out, out],
        out_shape=[jax.ShapeDtypeStruct((s, ATTN_GROUP_W), F32)] * 3,
        scratch_shapes=[pltpu.VMEM((7,) + res, F32), pltpu.VMEM((3,) + res, F32), pltpu.VMEM(res, F32), pltpu.VMEM(res, F32)],
        compiler_params=_cp("parallel", "arbitrary"), name=f"attn_bwd_g{gi}")(qn, kn, kn, vv, vv, do, dl)


def _layer_fwd(x, mod, W, l):
    sh1, sc1, g1, sh2, sc2, g2 = (mod[i:i + 1] for i in range(6))
    (h,) = _rowwise_fwd("norm1", _f_norm, [(x, 1024, 0)], [W["norm1_w"], sc1, sh1], [(1024, BF16)])
    proj = _matmul("in_proj", h, W["w_in"], "nn")
    xbc = _conv_fwd(proj, W["conv_w"], W["conv_b"])
    y, hist = _ssd_fwd(xbc, proj, W["dt_bias"], W["a_log"], W["d_skip"])
    (yn,) = _rowwise_fwd("ssd_gate", _f_ssdgate, [(y, 1024, 0), (proj, 1024, P_Z // 1024)], [W["ssd_norm_w"]], [(1024, BF16)])
    y_ssd = _matmul("ssd_out", yn, W["w_ssd_out"], "nn", layer=l)
    qn, kn, vv = _rowwise_fwd("qk_norm", _f_qknorm, [(proj, 1536, P_Q // 1536), (proj, 1536, P_K // 1536), (proj, 1536, P_V // 1536)],
                              [W["q_norm_w"], W["k_norm_w"]], [(1536, F32)] * 3)
    ol = [_attn_fwd(qn, kn, vv, gi) for gi in range(3)]
    (o,) = _rowwise_fwd("attn_combine", _f_combine, [(t[0], 512, 0) for t in ol] + [(t[1], 512, 0) for t in ol], [], [(512, BF16)])
    y_attn = _matmul("attn_out", o, W["w_attn_out"], "nn", layer=l, chips=256)
    pooled = _pool_fwd(proj)
    pm = _group_matmul("pool_mix", pooled, W["w_pool_mix"], "nn", layer=l)
    (ps,) = _rowwise_fwd("pool_scale", _f_poolscale, [(pm, 1024, 0)], [W["pool_scale"]], [(1024, BF16)])
    y_pool = _matmul("pool_out", ps, W["w_pool_out"], "nn", layer=l)
    (merged,) = _rowwise_fwd("merge", _f_merge, [(proj, 3072, P_GATES // 3072), (y_ssd, 1024, 0), (y_attn, 1024, 0), (y_pool, 1024, 0)],
                             [], [(1024, BF16)])
    mo = _matmul("mix_out", merged, W["w_out"], "nn", layer=l)
    (x1,) = _rowwise_fwd("resid1", _f_resid, [(x, 1024, 0), (mo, 1024, 0)], [g1], [(1024, F32)])
    (h2,) = _rowwise_fwd("norm2", _f_norm, [(x1, 1024, 0)], [W["norm2_w"], sc2, sh2], [(1024, BF16)])
    a = _matmul("ff1", h2, W["w_ff1"], "nn", layer=l, chips=1024)
    (r,) = _rowwise_fwd("relu2", _f_relu2, [(a, D_FF, 0)], [], [(D_FF, BF16)], tb=128)
    ff = _matmul("ff2", r, W["w_ff2"], "nn", layer=l)
    (x2,) = _rowwise_fwd("resid2", _f_resid, [(x1, 1024, 0), (ff, 1024, 0)], [g2], [(1024, F32)])
    saved = dict(x=x, h=h, proj=proj, xbc=xbc, y=y, hist=hist, yn=yn, y_ssd=y_ssd, qn=qn, kn=kn, vv=vv, ol=ol, o=o,
                 y_attn=y_attn, pooled=pooled, pm=pm, ps=ps, y_pool=y_pool, merged=merged, mo=mo, x1=x1, h2=h2, a=a, r=r, ff=ff)
    return x2, saved


def _layer_bwd(dx2, mod, W, sv, l):
    sh1, sc1, g1, sh2, sc2, g2 = (mod[i:i + 1] for i in range(6))
    g = {}
    dx1a, dff, dg2 = _rowwise_bwd("resid2_bwd", _f_resid, [(sv["x1"], 1024, 0), (sv["ff"], 1024, 0)], [g2], [(dx2, 1024, 0)],
                                  [True, True], gdt=[F32, BF16])
    g["w_ff2"] = _matmul("ff2_dw", sv["r"], dff, "tn", BF16).reshape(N_CHIPS, D_FF // N_CHIPS, D_MODEL)
    dr = _matmul("ff2_dx", dff, W["w_ff2"], "nt", layer=l)
    (da,) = _rowwise_bwd("relu2_bwd", _f_relu2, [(sv["a"], D_FF, 0)], [], [(dr, D_FF, 0)], [True], tb=128, gdt=[BF16])
    g["w_ff1"] = _matmul("ff1_dw", sv["h2"], da, "tn", BF16, out_chips=1024)
    dh2 = _matmul("ff1_dx", da, W["w_ff1"], "nt", layer=l, chips=1024)
    dx1, g["norm2_w"], dsc2, dsh2 = _rowwise_bwd("norm2_bwd", _f_norm, [(sv["x1"], 1024, 0)], [W["norm2_w"], sc2, sh2],
                                                 [(dh2, 1024, 0)], [True], add=(0, dx1a))
    dxa, dmo, dg1 = _rowwise_bwd("resid1_bwd", _f_resid, [(sv["x"], 1024, 0), (sv["mo"], 1024, 0)], [g1], [(dx1, 1024, 0)],
                                 [True, True], gdt=[F32, BF16])
    g["w_out"] = _matmul("mix_out_dw", sv["merged"], dmo, "tn", BF16).reshape(N_CHIPS, D_MODEL // N_CHIPS, D_MODEL)
    dmerged = _matmul("mix_out_dx", dmo, W["w_out"], "nt", layer=l)
    proj = sv["proj"]
    dgates, dy_ssd, dy_attn, dy_pool = _rowwise_bwd(
        "merge_bwd", _f_merge, [(proj, 3072, P_GATES // 3072), (sv["y_ssd"], 1024, 0), (sv["y_attn"], 1024, 0), (sv["y_pool"], 1024, 0)],
        [], [(dmerged, 1024, 0)], [True] * 4, gdt=[BF16] * 4)
    g["w_pool_out"] = _matmul("pool_out_dw", sv["ps"], dy_pool, "tn", BF16).reshape(N_CHIPS, D_MODEL // N_CHIPS, D_MODEL)
    dps = _matmul("pool_out_dx", dy_pool, W["w_pool_out"], "nt", layer=l)
    dpm, g["pool_scale"] = _rowwise_bwd("pool_scale_bwd", _f_poolscale, [(sv["pm"], 1024, 0)], [W["pool_scale"]], [(dps, 1024, 0)],
                                        [True], gdt=[BF16])
    dmix = _group_matmul("pool_mix_dw", sv["pooled"], dpm, "tn")
    g["w_pool_mix"] = jnp.moveaxis(dmix.reshape(4, N_CHIPS, POOL_GW // N_CHIPS, POOL_GW), 1, 0).astype(BF16)
    dpooled = _group_matmul("pool_mix_dx", dpm, W["w_pool_mix"], "nt", layer=l)
    du = _pool_bwd(dpooled)
    g["w_attn_out"] = _matmul("attn_out_dw", sv["o"], dy_attn, "tn", BF16, out_chips=256)
    do = _matmul("attn_out_dx", dy_attn, W["w_attn_out"], "nt", layer=l, chips=256)
    ol = sv["ol"]
    dol = _rowwise_bwd("attn_combine_bwd", _f_combine, [(t[0], 512, 0) for t in ol] + [(t[1], 512, 0) for t in ol], [],
                       [(do, 512, 0)], [True] * 6)
    dqs, dks, dvs = zip(*[_attn_bwd(sv["qn"], sv["kn"], sv["vv"], dol[gi], dol[3 + gi], gi) for gi in range(3)])
    dqn, dkn, dvv = (jnp.concatenate(t, axis=1) for t in (dqs, dks, dvs))
    dq, dk, dv, g["q_norm_w"], g["k_norm_w"] = _rowwise_bwd(
        "qk_norm_bwd", _f_qknorm, [(proj, 1536, P_Q // 1536), (proj, 1536, P_K // 1536), (proj, 1536, P_V // 1536)],
        [W["q_norm_w"], W["k_norm_w"]], [(dqn, 1536, 0), (dkn, 1536, 0), (dvv, 1536, 0)], [True] * 3, gdt=[BF16] * 3)
    g["w_ssd_out"] = _matmul("ssd_out_dw", sv["yn"], dy_ssd, "tn", BF16).reshape(N_CHIPS, D_MODEL // N_CHIPS, D_MODEL)
    dyn = _matmul("ssd_out_dx", dy_ssd, W["w_ssd_out"], "nt", layer=l)
    dy, dz, g["ssd_norm_w"] = _rowwise_bwd("ssd_gate_bwd", _f_ssdgate, [(sv["y"], 1024, 0), (proj, 1024, P_Z // 1024)], [W["ssd_norm_w"]],
                                           [(dyn, 1024, 0)], [True, True], gdt=[F32, BF16])
    dxbc, ddt, g["dt_bias"], g["a_log"], g["d_skip"] = _ssd_bwd(sv["xbc"], proj, sv["hist"], W["dt_bias"], W["a_log"], W["d_skip"], dy)
    dxbc_raw, g["conv_w"], g["conv_b"] = _conv_bwd(proj, W["conv_w"], W["conv_b"], dxbc)
    dproj = jnp.concatenate([dxbc_raw, dq, dk, dv, dgates, dz, du, ddt], axis=1)
    g["w_in"] = _matmul("in_proj_dw", sv["h"], dproj, "tn", BF16)
    dh = _matmul("in_proj_dx", dproj, W["w_in"], "nt")
    dx, g["norm1_w"], dsc1, dsh1 = _rowwise_bwd("norm1_bwd", _f_norm, [(sv["x"], 1024, 0)], [W["norm1_w"], sc1, sh1],
                                                [(dh, 1024, 0)], [True], add=(0, dxa))
    dmod = jnp.concatenate([dsh1, dsc1, dg1, dsh2, dsc2, dg2], axis=0)
    return dx, dmod, g


def _expand_heads(t):
    return jnp.repeat(t, SSD_HEAD_DIM, axis=-1)


def _reduce_heads(t):
    return t.reshape(t.shape[:-1] + (SSD_HEADS, SSD_HEAD_DIM)).sum(-1)


_IN_SPLITS = np.cumsum((0,) + IN_SIZES)


def _w_in_to_layout(w):
    z, xbc, dt, q, k, v, u, gates = (w[:, _IN_SPLITS[i]:_IN_SPLITS[i + 1]] for i in range(8))
    return jnp.concatenate([xbc, q, k, v, gates, z, u, _expand_heads(dt)], axis=1)


def _w_in_from_layout(g):
    xbc, q, k, v = (g[:, o:o + 1536] for o in (P_XBC, P_Q, P_K, P_V))
    gates, z, u, dt = g[:, P_GATES:P_GATES + 3072], g[:, P_Z:P_Z + 1024], g[:, P_U:P_U + 1024], g[:, P_DT:P_DT + 1024]
    return jnp.concatenate([z, xbc, _reduce_heads(dt.astype(F32)).astype(g.dtype), q, k, v, u, gates], axis=1)


_STACKED = ("w_ssd_out", "w_attn_out", "w_pool_mix", "w_pool_out", "w_out", "w_ff1", "w_ff2")
_ROWS = ("norm1_w", "norm2_w", "conv_b", "ssd_norm_w", "q_norm_w", "k_norm_w", "pool_scale")
_HEAD_ROWS = ("dt_bias", "a_log", "d_skip")


def _layer_weights(ws, l):
    W = {k: ws[k] for k in _STACKED}
    w_in = jnp.concatenate([ws["w_in"][l, j] for j in range(N_CHIPS)], axis=1)
    W["w_in"] = _w_in_to_layout(w_in)
    W["conv_w"] = ws["conv_w"][l]
    for k in _ROWS:
        W[k] = ws[k][l][None, :]
    for k in _HEAD_ROWS:
        W[k] = _expand_heads(ws[k][l])[None, :]
    return W


def _layer_grads_by_chip(g):
    out = dict(g)
    w_in = _w_in_from_layout(g["w_in"])
    out["w_in"] = jnp.moveaxis(w_in.reshape(D_MODEL, N_CHIPS, IN_WIDTH // N_CHIPS), 1, 0)
    for k in _ROWS:
        out[k] = g[k][0]
    for k in _HEAD_ROWS:
        out[k] = _reduce_heads(g[k][0])
    return out


def _local_step(x, tgt, mods, ws):
    saved, Ws = [], []
    for l in range(DEPTH):
        W = _layer_weights(ws, l)
        x, sv = _layer_fwd(x, mods[l], W, l)
        saved.append(sv)
        Ws.append(W)
    dx, loss = _loss_and_grad(x, tgt)
    dmods, grads = [None] * DEPTH, [None] * DEPTH
    for l in reversed(range(DEPTH)):
        dx, dmods[l], g = _layer_bwd(dx, mods[l], Ws[l], saved[l], l)
        grads[l] = _layer_grads_by_chip(g)
    return loss, dx, jnp.stack(dmods), grads


ANY = pl.BlockSpec(memory_space=pl.ANY)


def _place():
    x, y, c = lax.axis_index("x"), lax.axis_index("y"), lax.axis_index("c")
    return x, y, c, (x, y, 1 - c), [(1 - x, y), (x, 1 - y), (1 - x, 1 - y)]


def _allgather8(name, blk):
    m_per, n = blk.shape

    def body(x_ref, out_ref, send_sems, recv_sems, local_sem):
        x, y, c, sibling, chips = _place()
        me = (x, y, c)

        def rows(px, py, pc):
            return out_ref.at[pl.ds((4 * px + 2 * py + pc) * m_per, m_per), :]

        def copy(k, block, to, src=None):
            return pltpu.make_async_remote_copy(
                src_ref=rows(*block) if src is None else src, dst_ref=rows(*block),
                send_sem=send_sems.at[k], recv_sem=recv_sems.at[k], device_id=to, device_id_type=MESH)

        mine = pltpu.make_async_copy(x_ref, rows(*me), local_sem)
        mine.start()
        first = [copy(0, me, sibling, src=x_ref)]
        first += [copy(1 + j, me, (*chip, c), src=x_ref) for j, chip in enumerate(chips)]
        for cp in first:
            cp.start()
        passed = [copy(4 + j, (*chip, c), sibling) for j, chip in enumerate(chips)]
        for j, chip in enumerate(chips):
            copy(1 + j, (*chip, c), me).wait_recv()
            passed[j].start()
        copy(0, sibling, me).wait_recv()
        for j, chip in enumerate(chips):
            copy(4 + j, (*chip, 1 - c), me).wait_recv()
        for cp in first + passed:
            cp.wait_send()
        mine.wait()

    return pl.pallas_call(
        body, out_shape=jax.ShapeDtypeStruct((N_DEV * m_per, n), blk.dtype),
        in_specs=[pl.BlockSpec(memory_space=pltpu.VMEM)], out_specs=pl.BlockSpec(memory_space=pltpu.VMEM),
        scratch_shapes=[pltpu.SemaphoreType.DMA((7,)), pltpu.SemaphoreType.DMA((7,)), pltpu.SemaphoreType.DMA],
        name=name)(blk)


HALF_LAYERS = DEPTH // 2


def _dma_sems(n):
    return [pltpu.SemaphoreType.DMA((n,)), pltpu.SemaphoreType.DMA((n,))]


def _gather_weights(shards):
    n = len(shards)

    def body(*refs):
        ins, outs = refs[:n], refs[n:2 * n]
        send_sems, recv_sems = refs[2 * n:]
        x, y, c, sibling, chips = _place()
        me = 2 * x + y
        half, other = pl.ds(HALF_LAYERS * c, HALF_LAYERS), pl.ds(HALF_LAYERS * (1 - c), HALF_LAYERS)

        def copy(k, i, layers, chip, to, src=None):
            dst = outs[i].at[layers, chip]
            return pltpu.make_async_remote_copy(src_ref=dst if src is None else src, dst_ref=dst, send_sem=send_sems.at[k],
                                                recv_sem=recv_sems.at[k], device_id=to, device_id_type=MESH)

        first = [copy(3 * i + j, i, half, me, (cx, cy, c), src=ins[i].at[half]) for j, (cx, cy) in enumerate(chips) for i in range(n)]
        for cp in first:
            cp.start()
        passed = []
        for j, (cx, cy) in enumerate(chips):
            for i in range(n):
                copy(3 * i + j, i, half, 2 * cx + cy, (x, y, c)).wait_recv()
                passed.append(copy(3 * n + 3 * i + j, i, half, 2 * cx + cy, sibling))
                passed[-1].start()
        for j, (cx, cy) in enumerate(chips):
            for i in range(n):
                copy(3 * n + 3 * i + j, i, other, 2 * cx + cy, (x, y, c)).wait_recv()
        for cp in first + passed:
            cp.wait_send()

    return pl.pallas_call(
        body, out_shape=[jax.ShapeDtypeStruct((t.shape[0], N_CHIPS) + t.shape[1:], t.dtype) for t in shards],
        in_specs=[ANY] * n, out_specs=[ANY] * n, scratch_shapes=_dma_sems(6 * n), name="gather_weights")(*shards)


def _swap_halves(gs):
    n = len(gs)

    def body(*refs):
        ins, got = refs[:n], refs[n:2 * n]
        send_sems, recv_sems = refs[2 * n:]
        x, y, c, sibling, _ = _place()
        other = pl.ds(HALF_LAYERS * (1 - c), HALF_LAYERS)
        sends = [pltpu.make_async_remote_copy(src_ref=ins[i].at[:, other], dst_ref=got[i], send_sem=send_sems.at[i],
                                              recv_sem=recv_sems.at[i], device_id=sibling, device_id_type=MESH) for i in range(n)]
        for cp in sends:
            cp.start()
        for cp in sends:
            cp.wait_recv()
        for cp in sends:
            cp.wait_send()

    return pl.pallas_call(
        body, out_shape=[jax.ShapeDtypeStruct((N_CHIPS, HALF_LAYERS) + t.shape[2:], t.dtype) for t in gs],
        in_specs=[ANY] * n, out_specs=[ANY] * n, scratch_shapes=_dma_sems(n), name="grad_swap_halves")(*gs)


def _exchange_chips(ps):
    n = len(ps)

    def body(*refs):
        ins, outs = refs[:n], refs[n:2 * n]
        send_sems, recv_sems = refs[2 * n:]
        x, y, c, _, chips = _place()
        sends = [pltpu.make_async_remote_copy(src_ref=ins[i].at[2 * cx + cy], dst_ref=outs[i].at[j], send_sem=send_sems.at[3 * i + j],
                                              recv_sem=recv_sems.at[3 * i + j], device_id=(cx, cy, c), device_id_type=MESH)
                 for j, (cx, cy) in enumerate(chips) for i in range(n)]
        for cp in sends:
            cp.start()
        for cp in sends:
            cp.wait_recv()
        for cp in sends:
            cp.wait_send()

    return pl.pallas_call(
        body, out_shape=[jax.ShapeDtypeStruct((3,) + t.shape[1:], t.dtype) for t in ps], in_specs=[ANY] * n, out_specs=[ANY] * n,
        scratch_shapes=_dma_sems(3 * n), name="grad_exchange_chips")(*ps)


def _share_halves(ts):
    n = len(ts)

    def body(*refs):
        ins, outs = refs[:n], refs[n:2 * n]
        send_sems, recv_sems = refs[2 * n:]
        x, y, c, sibling, _ = _place()
        half, other = pl.ds(HALF_LAYERS * c, HALF_LAYERS), pl.ds(HALF_LAYERS * (1 - c), HALF_LAYERS)
        sends = [pltpu.make_async_remote_copy(src_ref=ins[i], dst_ref=outs[i].at[half], send_sem=send_sems.at[i],
                                              recv_sem=recv_sems.at[i], device_id=sibling, device_id_type=MESH) for i in range(n)]
        for cp in sends:
            cp.start()
        for i in range(n):
            pltpu.make_async_remote_copy(src_ref=ins[i], dst_ref=outs[i].at[other], send_sem=send_sems.at[i],
                                         recv_sem=recv_sems.at[i], device_id=sibling, device_id_type=MESH).wait_recv()
        for cp in sends:
            cp.wait_send()

    return pl.pallas_call(
        body, out_shape=[jax.ShapeDtypeStruct((DEPTH,) + t.shape[1:], t.dtype) for t in ts], in_specs=[ANY] * n, out_specs=[ANY] * n,
        scratch_shapes=_dma_sems(n), name="grad_share_halves")(*ts)


PACK_W = 1024
PACK_TB = 512


def _sum_rows(name, parts, out_dtype):
    def f(*vals):
        acc = vals[0]
        for v in vals[1:]:
            acc = acc + v
        return (acc,)

    return _rowwise_fwd(name, f, [(p, PACK_W, 0) for p in parts], [], [(PACK_W, out_dtype)], tb=_tile(parts[0].shape[0], PACK_TB))[0]


def _sum_slots(name, ops, rows, out_dtype):
    cw = ops[0][0].shape[1]
    tb = rows
    while tb * cw > 300_000 and tb % 32 == 0:
        tb //= 2
    per = rows // tb

    def body(*refs):
        acc = refs[0][...].astype(F32)
        for r in refs[1:-1]:
            acc = acc + r[...].astype(F32)
        refs[-1][...] = acc.astype(refs[-1].dtype)

    return pl.pallas_call(
        body, grid=(per,), in_specs=[pl.BlockSpec((tb, cw), lambda i, s=s: (s * per + i, 0)) for _, s in ops],
        out_specs=pl.BlockSpec((tb, cw), lambda i: (i, 0)), out_shape=jax.ShapeDtypeStruct((rows, cw), out_dtype),
        compiler_params=_cp("parallel"), name=name)(*[a for a, _ in ops])


def _adamw(name, w, g, m, v):
    r, cw = w.shape
    tb = r
    while tb * cw > 400_000 and tb % 16 == 0:
        tb //= 2
    c1 = 1.0 / (1.0 - ADAM_B1 ** ADAM_STEP)
    c2 = 1.0 / (1.0 - ADAM_B2 ** ADAM_STEP)

    def body(w_ref, g_ref, m_ref, v_ref, d_ref, mo_ref, vo_ref):
        gg = g_ref[...]
        mn = ADAM_B1 * m_ref[...] + (1.0 - ADAM_B1) * gg
        vn = ADAM_B2 * v_ref[...] + (1.0 - ADAM_B2) * jnp.square(gg)
        d_ref[...] = -ADAM_LR * ((mn * c1) / (jnp.sqrt(vn * c2) + ADAM_EPS) + ADAM_WD * w_ref[...])
        mo_ref[...] = mn
        vo_ref[...] = vn

    spec = pl.BlockSpec((tb, cw), lambda i: (i, 0))
    return pl.pallas_call(
        body, grid=(r // tb,), in_specs=[spec] * 4, out_specs=[spec] * 3,
        out_shape=[jax.ShapeDtypeStruct((r, cw), F32)] * 3, compiler_params=_cp("parallel"), name=name)(w, g, m, v)


def _silu_rows(c):
    def body(c_ref, o_ref):
        rows = lax.broadcasted_iota(jnp.int32, o_ref.shape, 0)
        o_ref[...] = jnp.where(rows == 0, jnp.broadcast_to(_silu(c_ref[...]), o_ref.shape), 0.0)

    return pl.pallas_call(body, out_shape=jax.ShapeDtypeStruct((8, c.shape[1]), F32), name="cond_silu")(c)


_KINDS = ("w_in", "w_ssd_out", "w_attn_out", "w_pool_mix", "w_pool_out", "w_out", "w_ff1", "w_ff2")
_SMALL = ("b_ada", "norm1_w", "norm2_w", "conv_b", "dt_bias", "a_log", "d_skip", "ssd_norm_w", "q_norm_w", "k_norm_w",
          "pool_scale")
_ORDER = ("w_ada", "b_ada", "norm1_w", "norm2_w", "w_in", "conv_w", "conv_b", "dt_bias", "a_log", "d_skip", "ssd_norm_w",
          "w_ssd_out", "q_norm_w", "k_norm_w", "w_attn_out", "w_pool_mix", "pool_scale", "w_pool_out", "w_out", "w_ff1", "w_ff2")


def _pack_flat(arrs, rows, dtype):
    flat = jnp.concatenate([a.reshape(-1).astype(dtype) for a in arrs])
    return jnp.pad(flat, (0, rows * PACK_W - flat.shape[0])).reshape(rows, PACK_W)


def _unpack_flat(buf, shapes):
    flat = buf.reshape(-1)
    out, off = [], 0
    for shp in shapes:
        n = int(np.prod(shp))
        out.append(flat[off:off + n].reshape(shp))
        off += n
    return out


def _small_rows(n_elems):
    return -(-n_elems // (8 * PACK_W)) * 8


def kernel(x, c, w_ada, b_ada, norm1_w, norm2_w, w_in, conv_w, conv_b, dt_bias, a_log, d_skip, ssd_norm_w, w_ssd_out, q_norm_w, k_norm_w, w_attn_out, w_pool_mix, pool_scale, w_pool_out, w_out, w_ff1, w_ff2, loss_target, m_w_ada, m_b_ada, m_norm1_w, m_norm2_w, m_w_in, m_conv_w, m_conv_b, m_dt_bias, m_a_log, m_d_skip, m_ssd_norm_w, m_w_ssd_out, m_q_norm_w, m_k_norm_w, m_w_attn_out, m_w_pool_mix, m_pool_scale, m_w_pool_out, m_w_out, m_w_ff1, m_w_ff2, v_w_ada, v_b_ada, v_norm1_w, v_norm2_w, v_w_in, v_conv_w, v_conv_b, v_dt_bias, v_a_log, v_d_skip, v_ssd_norm_w, v_w_ssd_out, v_q_norm_w, v_k_norm_w, v_w_attn_out, v_w_pool_mix, v_pool_scale, v_w_pool_out, v_w_out, v_w_ff1, v_w_ff2):
    w = dict(w_ada=w_ada, b_ada=b_ada, norm1_w=norm1_w, norm2_w=norm2_w, w_in=w_in, conv_w=conv_w, conv_b=conv_b, dt_bias=dt_bias, a_log=a_log, d_skip=d_skip, ssd_norm_w=ssd_norm_w, w_ssd_out=w_ssd_out, q_norm_w=q_norm_w, k_norm_w=k_norm_w, w_attn_out=w_attn_out, w_pool_mix=w_pool_mix, pool_scale=pool_scale, w_pool_out=w_pool_out, w_out=w_out, w_ff1=w_ff1, w_ff2=w_ff2)
    m = dict(w_ada=m_w_ada, b_ada=m_b_ada, norm1_w=m_norm1_w, norm2_w=m_norm2_w, w_in=m_w_in, conv_w=m_conv_w, conv_b=m_conv_b, dt_bias=m_dt_bias, a_log=m_a_log, d_skip=m_d_skip, ssd_norm_w=m_ssd_norm_w, w_ssd_out=m_w_ssd_out, q_norm_w=m_q_norm_w, k_norm_w=m_k_norm_w, w_attn_out=m_w_attn_out, w_pool_mix=m_w_pool_mix, pool_scale=m_pool_scale, w_pool_out=m_w_pool_out, w_out=m_w_out, w_ff1=m_w_ff1, w_ff2=m_w_ff2)
    v = dict(w_ada=v_w_ada, b_ada=v_b_ada, norm1_w=v_norm1_w, norm2_w=v_norm2_w, w_in=v_w_in, conv_w=v_conv_w, conv_b=v_conv_b, dt_bias=v_dt_bias, a_log=v_a_log, d_skip=v_d_skip, ssd_norm_w=v_ssd_norm_w, w_ssd_out=v_w_ssd_out, q_norm_w=v_q_norm_w, k_norm_w=v_k_norm_w, w_attn_out=v_w_attn_out, w_pool_mix=v_w_pool_mix, pool_scale=v_pool_scale, w_pool_out=v_w_pool_out, w_out=v_w_out, w_ff1=v_w_ff1, w_ff2=v_w_ff2)
    chip = 2 * lax.axis_index("x") + lax.axis_index("y")
    dev = 2 * chip + lax.axis_index("c")
    ada_cols = w_ada.shape[2]

    n_conv = conv_w.size // PACK_W
    rows1 = _small_rows((1 + n_conv) * PACK_W)
    blk = jnp.concatenate([_silu_rows(c)[:1], conv_w.reshape(n_conv, PACK_W), jnp.zeros((rows1 - 1 - n_conv, PACK_W), F32)])
    first = _allgather8("gather_cond", blk).reshape(N_DEV, rows1, PACK_W)
    cond_all = first[:, 0]
    conv_all = first[0::2, 1:1 + n_conv].reshape((N_CHIPS,) + conv_w.shape)
    conv_full = jnp.moveaxis(conv_all, 0, 2).reshape(DEPTH, SSD_CONV, CONV_DIM)
    b_cols = lax.dynamic_slice_in_dim(b_ada, chip * ada_cols, ada_cols, axis=1)
    mod_cols = jnp.stack([_matmul("ada_fwd", cond_all, w_ada[l], "nn", precise=True) + b_cols[l][None, :] for l in range(DEPTH)])
    mod_all = _allgather8("gather_mod", mod_cols.reshape(-1, PACK_W)).reshape(N_DEV, DEPTH, N_DEV, ada_cols)
    mine = lax.dynamic_index_in_dim(mod_all[0::2], dev, axis=2, keepdims=False)
    mods = jnp.moveaxis(mine, 0, 1).reshape(DEPTH, 6, D_MODEL)

    shards = [w[k].astype(BF16) for k in _KINDS]
    ws = {k: lax.dynamic_update_slice_in_dim(t, s[:, None], chip, axis=1)
          for k, t, s in zip(_KINDS, _gather_weights(shards), shards)}
    for k in ("w_ssd_out", "w_pool_out", "w_out", "w_ff2"):
        ws[k] = ws[k].reshape(DEPTH, -1, D_MODEL)
    ws["w_pool_mix"] = jnp.moveaxis(ws["w_pool_mix"], 1, 2).reshape(DEPTH, 4, POOL_GW, POOL_GW)
    ws["conv_w"] = conv_full
    for k in _SMALL[1:]:
        ws[k] = w[k]

    loss, grad_x, dmods, grads = _local_step(x[0], loss_target[0], mods, ws)

    small = ([dmods] + [jnp.stack([grads[l][k] for l in range(DEPTH)]) for k in _SMALL[1:] + ("conv_w",)] + [loss[:, :1]])
    n_small = sum(int(np.prod(a.shape)) for a in small)
    rows_small = _small_rows(n_small)
    small_all = _allgather8("gather_small", _pack_flat(small, rows_small, F32))
    parts = [small_all[d * rows_small:(d + 1) * rows_small] for d in range(N_DEV)]
    small_sum = _unpack_flat(_sum_rows("sum_small", parts, F32), [a.shape for a in small])
    g_out = {"b_ada": small_sum[0].reshape(DEPTH, 6 * D_MODEL)}
    for k, t in zip(_SMALL[1:], small_sum[1:-2]):
        g_out[k] = t
    g_out["conv_w"] = lax.dynamic_slice_in_dim(small_sum[-2], chip * conv_w.shape[2], conv_w.shape[2], axis=2)
    loss_out = small_sum[-1][0, 0]
    dmod_all = jnp.stack([p[:DEPTH * 6].reshape(DEPTH, 6 * D_MODEL) for p in parts])
    dmod_cols = lax.dynamic_slice_in_dim(dmod_all, chip * ada_cols, ada_cols, axis=2)
    g_out["w_ada"] = jnp.stack([_matmul("ada_dw", cond_all, dmod_cols[:, l], "tn", precise=True) for l in range(DEPTH)])

    gs = [jnp.stack([grads[l][k] for l in range(DEPTH)], axis=1) for k in _KINDS]
    core = lax.axis_index("c")
    got = _swap_halves(gs)
    own = [lax.dynamic_slice_in_dim(t, HALF_LAYERS * core, HALF_LAYERS, axis=1) for t in gs]

    def flat(t):
        return t.reshape(-1, t.shape[-1])

    pairs = [_sum_slots("sum_pair_" + k, [(flat(a), 0), (flat(b), 0)], flat(a).shape[0], BF16).reshape(a.shape)
             for k, a, b in zip(_KINDS, own, got)]
    partials = _exchange_chips(pairs)
    mine = [lax.dynamic_index_in_dim(p, chip, axis=0, keepdims=False) for p in pairs]
    totals = [_sum_slots("sum_chips_" + k, [(flat(a), 0)] + [(flat(p), s) for s in range(3)], flat(a).shape[0], F32).reshape(a.shape)
              for k, a, p in zip(_KINDS, mine, partials)]
    for k, t, mine_t in zip(_KINDS, _share_halves(totals), totals):
        g_out[k] = lax.dynamic_update_slice_in_dim(t, mine_t, HALF_LAYERS * core, axis=0)

    deltas, new_m, new_v = {}, {}, {}
    for k in ("w_ada", "conv_w") + _KINDS:
        shp = w[k].shape
        two_d = (int(np.prod(shp[:-1])), shp[-1])
        res = _adamw("adamw_" + k, *(t.reshape(two_d) for t in (w[k], g_out[k], m[k], v[k])))
        deltas[k], new_m[k], new_v[k] = (t.reshape(shp) for t in res)
    small_shapes = [w[k].shape for k in _SMALL]
    n_sm = sum(int(np.prod(s)) for s in small_shapes)
    res = _adamw("adamw_small", *[_pack_flat([t[k] for k in _SMALL], _small_rows(n_sm), F32) for t in (w, g_out, m, v)])
    for name_map, buf in zip((deltas, new_m, new_v), res):
        for k, t in zip(_SMALL, _unpack_flat(buf, small_shapes)):
            name_map[k] = t

    return (loss_out, grad_x[None], *[g_out[k] for k in _ORDER], *[deltas[k] for k in _ORDER],
            *[new_m[k] for k in _ORDER], *[new_v[k] for k in _ORDER])
```

```python
import functools
import math

import numpy as np
import jax
import jax.numpy as jnp
from jax import lax
from jax.experimental import pallas as pl
from jax.experimental.pallas import tpu as pltpu

F32, BF16 = jnp.float32, jnp.bfloat16
MESH = pl.DeviceIdType.MESH

D_MODEL = 1024
DEPTH = 4
N_CHIPS = 4
N_DEV = 8
SSD_HEADS = 16
SSD_HEAD_DIM = 64
SSD_STATE = 128
SSD_CHUNK = 128
SSD_CONV = 4
CONV_DIM = 1536
ATTN_HEAD_DIM = 128
ATTN_GROUP_W = 512
DILATIONS = (1, 4, 16)
ATTN_STEPS = 128
POOL_WINDOWS = (2, 4, 8, 16)
POOL_GW = 256
D_FF = 4096
EPS = 1e-6
IN_SIZES = (1024, 1536, 16, 1536, 1536, 1536, 1024, 3072)
IN_WIDTH = sum(IN_SIZES)
P_XBC, P_Q, P_K, P_V, P_GATES, P_Z, P_U, P_DT = 0, 1536, 3072, 4608, 6144, 9216, 10240, 11264
P_WIDTH = 12288
LANES = 128
NEG = -1e30
VMEM_LIMIT = 56 * 1024 * 1024

ADAM_LR, ADAM_B1, ADAM_B2, ADAM_EPS, ADAM_WD, ADAM_STEP = 0.001, 0.9, 0.999, 1e-08, 0.01, 10


def _alibi_slopes(n):
    def pow2(k):
        start = 2.0 ** (-8.0 / k)
        return [start ** (i + 1) for i in range(k)]
    if math.log2(n).is_integer():
        s = pow2(n)
    else:
        c = 2 ** math.floor(math.log2(n))
        s = pow2(c) + pow2(2 * c)[0::2][: n - c]
    return np.sort(np.asarray(s, np.float32))[::-1].copy()


SLOPES = _alibi_slopes(12).reshape(3, 4)


def _cp(*sem):
    return pltpu.CompilerParams(dimension_semantics=sem, vmem_limit_bytes=VMEM_LIMIT)


_DIMS = {"nn": (((1,), (0,)), ((), ())), "nt": (((1,), (1,)), ((), ())), "tn": (((0,), (0,)), ((), ()))}


def _dot(a, b, mode):
    return lax.dot_general(a.astype(BF16), b.astype(BF16), _DIMS[mode], preferred_element_type=F32)


@functools.partial(jax.custom_vjp, nondiff_argnums=(2,))
def _bdot(a, b, mode):
    return _dot(a, b, mode)


def _bdot_fwd(a, b, mode):
    return _dot(a, b, mode), (a, b)


def _bdot_bwd(mode, res, ct):
    a, b = res
    if mode == "nn":
        return _dot(ct, b, "nt"), _dot(a, ct, "tn")
    if mode == "nt":
        return _dot(ct, b, "nn"), _dot(ct, a, "tn")
    return _dot(b, ct, "nt"), _dot(a, ct, "nn")


_bdot.defvjp(_bdot_fwd, _bdot_bwd)


def _hdot(a, b):
    return jnp.dot(a, b, precision=lax.Precision.HIGHEST, preferred_element_type=F32)


def _tri(n, lower):
    r = lax.broadcasted_iota(jnp.int32, (n, n), 0)
    c = lax.broadcasted_iota(jnp.int32, (n, n), 1)
    return (r >= c if lower else r <= c).astype(F32)


@jax.custom_vjp
def _csum(a):
    return _hdot(_tri(a.shape[0], True), a)


def _csum_fwd(a):
    return _csum(a), None


def _csum_bwd(_, ct):
    return (_hdot(_tri(ct.shape[0], False), ct),)


_csum.defvjp(_csum_fwd, _csum_bwd)


def _softplus(x):
    return jnp.maximum(x, 0.0) + jnp.log(1.0 + jnp.exp(-jnp.abs(x)))


def _sigmoid(x):
    return 1.0 / (1.0 + jnp.exp(-x))


def _silu(x):
    return x * _sigmoid(x)


def _tile(n, cap):
    t = min(n, cap)
    while n % t:
        t //= 2
    return t


MM_TILE, MM_KTILE = 1024, 2048


def _matmul(name, a, b, mode, out_dtype=F32, precise=False, layer=None, chips=0, out_chips=0):
    if mode == "nn":
        (m, k), n = a.shape, (4 * chips if chips else b.shape[-1])
    elif mode == "nt":
        (m, k), n = a.shape, b.shape[-2]
    else:
        (k, m), n = a.shape, b.shape[-1]
    tm = _tile(m, MM_TILE)
    tn = _tile(chips if (chips and mode == "nn") else (out_chips or n), MM_TILE)
    tk = _tile(chips if (chips and mode == "nt") else k, MM_KTILE)
    nk = k // tk
    a_spec = pl.BlockSpec((tk, tm), lambda i, j, l: (l, i)) if mode == "tn" else pl.BlockSpec((tm, tk), lambda i, j, l: (i, l))
    if chips:
        if mode == "nn":
            per = chips // tn
            b_spec = pl.BlockSpec((None, None, tk, tn), lambda i, j, l: (layer, j // per, l, j % per))
        else:
            per = chips // tk
            b_spec = pl.BlockSpec((None, None, tn, tk), lambda i, j, l: (layer, l // per, j, l % per))
    elif layer is not None:
        b_spec = (pl.BlockSpec((None, tn, tk), lambda i, j, l: (layer, j, l)) if mode == "nt"
                  else pl.BlockSpec((None, tk, tn), lambda i, j, l: (layer, l, j)))
    else:
        b_spec = pl.BlockSpec((tn, tk), lambda i, j, l: (j, l)) if mode == "nt" else pl.BlockSpec((tk, tn), lambda i, j, l: (l, j))
    if out_chips:
        per_o = out_chips // tn
        o_spec = pl.BlockSpec((None, tm, tn), lambda i, j, l: (j // per_o, i, j % per_o))
        o_shape = jax.ShapeDtypeStruct((N_CHIPS, m, out_chips), out_dtype)
    else:
        o_spec = pl.BlockSpec((tm, tn), lambda i, j, l: (i, j))
        o_shape = jax.ShapeDtypeStruct((m, n), out_dtype)

    def part(a_ref, b_ref):
        if precise:
            return lax.dot_general(a_ref[...], b_ref[...], _DIMS[mode], precision=lax.Precision.HIGHEST,
                                   preferred_element_type=F32)
        return _dot(a_ref[...], b_ref[...], mode)

    if nk == 1:
        def body(a_ref, b_ref, o_ref):
            o_ref[...] = part(a_ref, b_ref).astype(o_ref.dtype)
        scratch = []
    else:
        def body(a_ref, b_ref, o_ref, acc_ref):
            l = pl.program_id(2)
            p = part(a_ref, b_ref)

            @pl.when(l == 0)
            def _():
                acc_ref[...] = p

            @pl.when((l > 0) & (l < nk - 1))
            def _():
                acc_ref[...] += p

            @pl.when(l == nk - 1)
            def _():
                o_ref[...] = (acc_ref[...] + p).astype(o_ref.dtype)
        scratch = [pltpu.VMEM((tm, tn), F32)]

    return pl.pallas_call(
        body, grid=(m // tm, n // tn, nk), in_specs=[a_spec, b_spec], out_specs=o_spec, out_shape=o_shape,
        scratch_shapes=scratch, compiler_params=_cp("parallel", "parallel", "arbitrary"), name=name)(a, b)


def _group_matmul(name, a, w, mode, out_dtype=F32, layer=0):
    s = a.shape[0]
    tb = 512
    gw = POOL_GW
    if mode == "tn":
        def body(a_ref, b_ref, o_ref):
            part = _dot(a_ref[...], b_ref[...], "tn")

            @pl.when(pl.program_id(1) == 0)
            def _():
                o_ref[0] = part

            @pl.when(pl.program_id(1) > 0)
            def _():
                o_ref[0] += part

        return pl.pallas_call(
            body, grid=(4, s // tb),
            in_specs=[pl.BlockSpec((tb, gw), lambda g, i: (i, g)), pl.BlockSpec((tb, gw), lambda g, i: (i, g))],
            out_specs=pl.BlockSpec((1, gw, gw), lambda g, i: (g, 0, 0)),
            out_shape=jax.ShapeDtypeStruct((4, gw, gw), F32),
            compiler_params=_cp("parallel", "arbitrary"), name=name)(a, w)

    def body(a_ref, w_ref, o_ref):
        o_ref[...] = _dot(a_ref[...], w_ref[...], mode).astype(o_ref.dtype)

    return pl.pallas_call(
        body, grid=(s // tb, 4),
        in_specs=[pl.BlockSpec((tb, gw), lambda i, g: (i, g)), pl.BlockSpec((None, None, gw, gw), lambda i, g: (layer, g, 0, 0))],
        out_specs=pl.BlockSpec((tb, gw), lambda i, g: (i, g)),
        out_shape=jax.ShapeDtypeStruct((s, 4 * gw), out_dtype),
        compiler_params=_cp("parallel", "parallel"), name=name)(a, w)


def _rspec(tb, width, cb):
    return pl.BlockSpec((tb, width), lambda i: (i, cb))


def _pspec(shape):
    return pl.BlockSpec(shape, lambda i: (0, 0))


def _rowwise_fwd(name, f, rows, pars, outs, tb=256):
    s = rows[0][0].shape[0]
    nin = len(rows) + len(pars)

    def body(*refs):
        res = f(*[r[...].astype(F32) for r in refs[:nin]])
        for o, v in zip(refs[nin:], res):
            o[...] = v.astype(o.dtype)

    return pl.pallas_call(
        body, grid=(s // tb,),
        in_specs=[_rspec(tb, w, cb) for _, w, cb in rows] + [_pspec(p.shape) for p in pars],
        out_specs=[_rspec(tb, w, 0) for w, _ in outs],
        out_shape=[jax.ShapeDtypeStruct((s, w), dt) for w, dt in outs],
        compiler_params=_cp("parallel"), name=name)(*[r[0] for r in rows], *pars)


def _rowwise_bwd(name, f, rows, pars, cts, need, add=None, tb=256, gdt=None):
    s = rows[0][0].shape[0]
    nr, npar, nc = len(rows), len(pars), len(cts)
    nin = nr + npar + nc + (1 if add is not None else 0)

    def body(*refs):
        ins = [r[...].astype(F32) for r in refs[:nr + npar]]
        _, vjp = jax.vjp(f, *ins)
        g = vjp(tuple(c[...].astype(F32) for c in refs[nr + npar:nr + npar + nc]))
        outs = refs[nin:]
        k = 0
        for j in range(nr):
            if need[j]:
                v = g[j]
                if add is not None and add[0] == j:
                    v = v + refs[nin - 1][...]
                outs[k][...] = v.astype(outs[k].dtype)
                k += 1
        first = pl.program_id(0) == 0
        for j in range(npar):
            o, v = outs[k + j], g[nr + j]

            @pl.when(first)
            def _(o=o, v=v):
                o[...] = v

            @pl.when(jnp.logical_not(first))
            def _(o=o, v=v):
                o[...] += v

    in_specs = ([_rspec(tb, w, cb) for _, w, cb in rows] + [_pspec(p.shape) for p in pars]
                + [_rspec(tb, w, cb) for _, w, cb in cts])
    args = [r[0] for r in rows] + list(pars) + [c[0] for c in cts]
    if add is not None:
        in_specs.append(_rspec(tb, rows[add[0]][1], 0))
        args.append(add[1])
    gr = [(w, F32) for (_, w, _), nd in zip(rows, need) if nd]
    if gdt is not None:
        gr = [(w, dt) for (w, _), dt in zip(gr, gdt)]
    return pl.pallas_call(
        body, grid=(s // tb,), in_specs=in_specs,
        out_specs=[_rspec(tb, w, 0) for w, _ in gr] + [_pspec(p.shape) for p in pars],
        out_shape=[jax.ShapeDtypeStruct((s, w), dt) for w, dt in gr] + [jax.ShapeDtypeStruct(p.shape, F32) for p in pars],
        compiler_params=_cp("arbitrary"), name=name)(*args)


def _f_norm(x, nw, sc, sh):
    r = lax.rsqrt(jnp.mean(x * x, axis=-1, keepdims=True) + EPS)
    return ((x * r * nw) * (1.0 + sc) + sh,)


def _f_ssdgate(y, z, w):
    y2 = y * _silu(z)
    low = lax.broadcasted_iota(jnp.int32, y2.shape, 1) < 512
    sq = y2 * y2
    m0 = jnp.sum(jnp.where(low, sq, 0.0), axis=-1, keepdims=True) / 512.0
    m1 = jnp.sum(jnp.where(low, 0.0, sq), axis=-1, keepdims=True) / 512.0
    r = jnp.where(low, lax.rsqrt(m0 + EPS), lax.rsqrt(m1 + EPS))
    return (y2 * r * w,)


def _head_rms(t, w):
    outs = []
    for h in range(t.shape[1] // ATTN_HEAD_DIM):
        th = t[:, h * ATTN_HEAD_DIM:(h + 1) * ATTN_HEAD_DIM]
        outs.append(th * lax.rsqrt(jnp.mean(th * th, axis=-1, keepdims=True) + EPS) * w)
    return jnp.concatenate(outs, axis=1)


def _f_qknorm(q, k, v, qw, kw):
    return _head_rms(q, qw), _head_rms(k, kw), v


def _f_combine(o1, o2, o3, l1, l2, l3):
    m = lax.stop_gradient(jnp.maximum(jnp.maximum(l1, l2), l3))
    e1, e2, e3 = jnp.exp(l1 - m), jnp.exp(l2 - m), jnp.exp(l3 - m)
    return ((e1 * o1 + e2 * o2 + e3 * o3) / (e1 + e2 + e3),)


def _f_poolscale(pm, ps):
    return (pm * ps,)


def _f_merge(gates, ys, ya, yp):
    g = _sigmoid(gates)
    return (g[:, 0:1024] * ys + g[:, 1024:2048] * ya + g[:, 2048:3072] * yp,)


def _f_resid(x, o, g):
    return (x + g * o,)


def _f_relu2(a):
    return (jnp.square(jnp.maximum(a, 0.0)),)


def _loss_and_grad(y, tgt, tb=512):
    s, d = y.shape

    def body(y_ref, t_ref, dy_ref, l_ref):
        e = y_ref[...] - t_ref[...]
        dy_ref[...] = e * (1.0 / d)
        part = jnp.zeros((1, LANES), F32) + jnp.sum(e * e) * (0.5 / d)

        @pl.when(pl.program_id(0) == 0)
        def _():
            l_ref[...] = part

        @pl.when(pl.program_id(0) > 0)
        def _():
            l_ref[...] += part

    return pl.pallas_call(
        body, grid=(s // tb,), in_specs=[_rspec(tb, d, 0), _rspec(tb, d, 0)],
        out_specs=[_rspec(tb, d, 0), _pspec((1, LANES))],
        out_shape=[jax.ShapeDtypeStruct((s, d), F32), jax.ShapeDtypeStruct((1, LANES), F32)],
        compiler_params=_cp("arbitrary"), name="loss")(y, tgt)


def _shift_down(x, j):
    rows = lax.broadcasted_iota(jnp.int32, x.shape, 0)
    return jnp.where(rows < j, 0.0, pltpu.roll(x, j, 0))


def _shift_up(x, j):
    s = x.shape[0]
    rows = lax.broadcasted_iota(jnp.int32, x.shape, 0)
    return jnp.where(rows >= s - j, 0.0, pltpu.roll(x, s - j, 0))


CONV_CB = 256


def _conv_pre(x, w_ref, b_ref):
    acc = b_ref[...] + w_ref[SSD_CONV - 1:SSD_CONV, :] * x
    for j in range(1, SSD_CONV):
        acc = acc + w_ref[SSD_CONV - 1 - j:SSD_CONV - j, :] * _shift_down(x, j)
    return acc


def _conv_fwd(proj, cw, cb):
    s = proj.shape[0]

    def body(x_ref, w_ref, b_ref, o_ref):
        o_ref[...] = _silu(_conv_pre(x_ref[...], w_ref, b_ref))

    return pl.pallas_call(
        body, grid=(CONV_DIM // CONV_CB,),
        in_specs=[pl.BlockSpec((s, CONV_CB), lambda i: (0, P_XBC // CONV_CB + i)),
                  pl.BlockSpec((SSD_CONV, CONV_CB), lambda i: (0, i)), pl.BlockSpec((1, CONV_CB), lambda i: (0, i))],
        out_specs=pl.BlockSpec((s, CONV_CB), lambda i: (0, i)),
        out_shape=jax.ShapeDtypeStruct((s, CONV_DIM), F32), compiler_params=_cp("parallel"), name="conv_fwd")(proj, cw, cb)


def _conv_bwd(proj, cw, cb, dout):
    s = proj.shape[0]

    def body(x_ref, w_ref, b_ref, d_ref, dx_ref, dw_ref, db_ref):
        x = x_ref[...]
        a = _conv_pre(x, w_ref, b_ref)
        sg = _sigmoid(a)
        da = d_ref[...] * (sg + a * sg * (1.0 - sg))
        db_ref[...] = jnp.sum(da, axis=0, keepdims=True)
        dx = w_ref[SSD_CONV - 1:SSD_CONV, :] * da
        dw_ref[SSD_CONV - 1:SSD_CONV, :] = jnp.sum(da * x, axis=0, keepdims=True)
        for j in range(1, SSD_CONV):
            dx = dx + w_ref[SSD_CONV - 1 - j:SSD_CONV - j, :] * _shift_up(da, j)
            dw_ref[SSD_CONV - 1 - j:SSD_CONV - j, :] = jnp.sum(da * _shift_down(x, j), axis=0, keepdims=True)
        dx_ref[...] = dx.astype(dx_ref.dtype)

    return pl.pallas_call(
        body, grid=(CONV_DIM // CONV_CB,),
        in_specs=[pl.BlockSpec((s, CONV_CB), lambda i: (0, P_XBC // CONV_CB + i)),
                  pl.BlockSpec((SSD_CONV, CONV_CB), lambda i: (0, i)), pl.BlockSpec((1, CONV_CB), lambda i: (0, i)),
                  pl.BlockSpec((s, CONV_CB), lambda i: (0, i))],
        out_specs=[pl.BlockSpec((s, CONV_CB), lambda i: (0, i)), pl.BlockSpec((SSD_CONV, CONV_CB), lambda i: (0, i)),
                   pl.BlockSpec((1, CONV_CB), lambda i: (0, i))],
        out_shape=[jax.ShapeDtypeStruct((s, CONV_DIM), BF16), jax.ShapeDtypeStruct((SSD_CONV, CONV_DIM), F32),
                   jax.ShapeDtypeStruct((1, CONV_DIM), F32)],
        compiler_params=_cp("parallel"), name="conv_bwd")(proj, cw, cb, dout)


def _pool_window_sum(x, g, shift):
    s2 = x + shift(x, 1)
    s4 = s2 + shift(s2, 2)
    s8 = s4 + shift(s4, 4)
    s16 = s8 + shift(s8, 8)
    return jnp.where(g == 0, s2, jnp.where(g == 1, s4, jnp.where(g == 2, s8, s16)))


def _pool_count(shape, g):
    rows = lax.broadcasted_iota(jnp.int32, shape, 0)
    return jnp.minimum(rows + 1, jnp.left_shift(2, g)).astype(F32)


def _pool_fwd(proj):
    s = proj.shape[0]

    def body(u_ref, o_ref):
        g = pl.program_id(0)
        u = u_ref[...]
        o_ref[...] = (_pool_window_sum(u, g, _shift_down) / _pool_count(u.shape, g) - u).astype(o_ref.dtype)

    return pl.pallas_call(
        body, grid=(4,), in_specs=[pl.BlockSpec((s, POOL_GW), lambda g: (0, P_U // POOL_GW + g))],
        out_specs=pl.BlockSpec((s, POOL_GW), lambda g: (0, g)),
        out_shape=jax.ShapeDtypeStruct((s, 4 * POOL_GW), BF16), compiler_params=_cp("parallel"), name="pool_fwd")(proj)


def _pool_bwd(dp):
    s = dp.shape[0]

    def body(d_ref, o_ref):
        g = pl.program_id(0)
        d = d_ref[...]
        o_ref[...] = (_pool_window_sum(d / _pool_count(d.shape, g), g, _shift_up) - d).astype(o_ref.dtype)

    return pl.pallas_call(
        body, grid=(4,), in_specs=[pl.BlockSpec((s, POOL_GW), lambda g: (0, g))],
        out_specs=pl.BlockSpec((s, POOL_GW), lambda g: (0, g)),
        out_shape=jax.ShapeDtypeStruct((s, 4 * POOL_GW), BF16), compiler_params=_cp("parallel"), name="pool_bwd")(dp)


N_PAIRS = SSD_HEADS // 2
STATE_ROWS = N_PAIRS * SSD_STATE


def _ssd_chunk(xbc, dtr, hprev, dtb, alog, dsk):
    L = xbc.shape[0]
    xs, bm, cm = xbc[:, 0:1024], xbc[:, 1024:1280], xbc[:, 1280:1536]
    dt = _softplus(dtr + dtb)
    a = dt * (-jnp.exp(alog))
    acum = _csum(a)
    alast = jnp.sum(a, axis=0, keepdims=True)
    xdt = xs * dt
    xdecay = xdt * jnp.exp(alast - acum)
    eacum = jnp.exp(acum)
    elast = jnp.exp(alast)
    cb = [_bdot(cm[:, g * 128:(g + 1) * 128], bm[:, g * 128:(g + 1) * 128], "nt") for g in range(2)]
    rows = lax.broadcasted_iota(jnp.int32, (L, L), 0)
    cols = lax.broadcasted_iota(jnp.int32, (L, L), 1)
    causal = rows >= cols
    lane = lax.broadcasted_iota(jnp.int32, (L, LANES), 1)
    sub = lax.broadcasted_iota(jnp.int32, (LANES, L), 0)
    ys, hs = [], []
    for p in range(N_PAIRS):
        g = p // (N_PAIRS // 2)
        sl = slice(p * LANES, (p + 1) * LANES)
        ac = acum[:, sl]
        act = ac.T
        xp = xdt[:, sl]
        hp = hprev[p * SSD_STATE:(p + 1) * SSD_STATE, :]
        y = _bdot(cm[:, g * 128:(g + 1) * 128], hp, "nn") * eacum[:, sl] + dsk[:, sl] * xs[:, sl]
        for half in range(2):
            l0 = half * SSD_HEAD_DIM
            col = jnp.sum(jnp.where(lane == l0, ac, 0.0), axis=1, keepdims=True)
            row = jnp.sum(jnp.where(sub == l0, act, 0.0), axis=0, keepdims=True)
            decay = jnp.exp(jnp.where(causal, col - row, NEG))
            xh = jnp.where((lane >= l0) & (lane < l0 + SSD_HEAD_DIM), xp, 0.0)
            y = y + _bdot(cb[g] * decay, xh, "nn")
        ys.append(y)
        hs.append(elast[:, sl] * hp + _bdot(bm[:, g * 128:(g + 1) * 128], xdecay[:, sl], "tn"))
    return tuple(ys), tuple(hs)


def _ssd_fwd(xbc, proj, dtb, alog, dsk):
    s = xbc.shape[0]
    nc = s // SSD_CHUNK

    def body(x_ref, dt_ref, b_ref, a_ref, d_ref, y_ref, hist_ref, h_ref):
        @pl.when(pl.program_id(0) == 0)
        def _():
            h_ref[...] = jnp.zeros_like(h_ref)

        hprev = h_ref[...]
        hist_ref[...] = hprev
        ys, hs = _ssd_chunk(x_ref[...], dt_ref[...], hprev, b_ref[...], a_ref[...], d_ref[...])
        for p in range(N_PAIRS):
            y_ref[:, p * LANES:(p + 1) * LANES] = ys[p]
            h_ref[p * SSD_STATE:(p + 1) * SSD_STATE, :] = hs[p]

    return pl.pallas_call(
        body, grid=(nc,),
        in_specs=[pl.BlockSpec((SSD_CHUNK, CONV_DIM), lambda i: (i, 0)),
                  pl.BlockSpec((SSD_CHUNK, 1024), lambda i: (i, P_DT // 1024)),
                  _pspec((1, 1024)), _pspec((1, 1024)), _pspec((1, 1024))],
        out_specs=[pl.BlockSpec((SSD_CHUNK, 1024), lambda i: (i, 0)), pl.BlockSpec((STATE_ROWS, LANES), lambda i: (i, 0))],
        out_shape=[jax.ShapeDtypeStruct((s, 1024), F32), jax.ShapeDtypeStruct((nc * STATE_ROWS, LANES), F32)],
        scratch_shapes=[pltpu.VMEM((STATE_ROWS, LANES), F32)],
        compiler_params=_cp("arbitrary"), name="ssd_fwd")(xbc, proj, dtb, alog, dsk)


def _ssd_bwd(xbc, proj, hist, dtb, alog, dsk, dy):
    s = xbc.shape[0]
    nc = s // SSD_CHUNK

    def body(x_ref, dt_ref, hist_ref, b_ref, a_ref, d_ref, dy_ref, dx_ref, ddt_ref, db_ref, da_ref, dd_ref, dh_ref):
        first = pl.program_id(0) == 0

        @pl.when(first)
        def _():
            dh_ref[...] = jnp.zeros_like(dh_ref)

        _, vjp = jax.vjp(_ssd_chunk, x_ref[...], dt_ref[...], hist_ref[...], b_ref[...], a_ref[...], d_ref[...])
        dys = tuple(dy_ref[:, p * LANES:(p + 1) * LANES] for p in range(N_PAIRS))
        dhs = tuple(dh_ref[p * SSD_STATE:(p + 1) * SSD_STATE, :] for p in range(N_PAIRS))
        dx, ddt, dhp, db, da, dd = vjp((dys, dhs))
        dx_ref[...] = dx
        ddt_ref[...] = ddt.astype(ddt_ref.dtype)
        dh_ref[...] = dhp
        for o, v in ((db_ref, db), (da_ref, da), (dd_ref, dd)):
            @pl.when(first)
            def _(o=o, v=v):
                o[...] = v

            @pl.when(jnp.logical_not(first))
            def _(o=o, v=v):
                o[...] += v

    rev = lambda i: (nc - 1 - i, 0)
    return pl.pallas_call(
        body, grid=(nc,),
        in_specs=[pl.BlockSpec((SSD_CHUNK, CONV_DIM), rev),
                  pl.BlockSpec((SSD_CHUNK, 1024), lambda i: (nc - 1 - i, P_DT // 1024)),
                  pl.BlockSpec((STATE_ROWS, LANES), rev),
                  _pspec((1, 1024)), _pspec((1, 1024)), _pspec((1, 1024)),
                  pl.BlockSpec((SSD_CHUNK, 1024), rev)],
        out_specs=[pl.BlockSpec((SSD_CHUNK, CONV_DIM), rev), pl.BlockSpec((SSD_CHUNK, 1024), rev),
                   _pspec((1, 1024)), _pspec((1, 1024)), _pspec((1, 1024))],
        out_shape=[jax.ShapeDtypeStruct((s, CONV_DIM), F32), jax.ShapeDtypeStruct((s, 1024), BF16)]
        + [jax.ShapeDtypeStruct((1, 1024), F32)] * 3,
        scratch_shapes=[pltpu.VMEM((STATE_ROWS, LANES), F32)],
        compiler_params=_cp("arbitrary"), name="ssd_bwd")(xbc, proj, hist, dtb, alog, dsk, dy)


def _attn_head(q, kp, kc, vp, vc, has_prev, slope):
    scale = ATTN_HEAD_DIM ** -0.5
    n = ATTN_STEPS
    qi = lax.broadcasted_iota(jnp.int32, (n, n), 0)
    kj = lax.broadcasted_iota(jnp.int32, (n, n), 1)
    sp = jnp.where((kj >= qi) & has_prev, _bdot(q, kp, "nt") * scale - slope * (qi + n - kj).astype(F32), NEG)
    sc = jnp.where(kj <= qi, _bdot(q, kc, "nt") * scale - slope * (qi - kj).astype(F32), NEG)
    m = lax.stop_gradient(jnp.maximum(jnp.max(sp, axis=1, keepdims=True), jnp.max(sc, axis=1, keepdims=True)))
    pp, pc = jnp.exp(sp - m), jnp.exp(sc - m)
    den = jnp.sum(pp, axis=1, keepdims=True) + jnp.sum(pc, axis=1, keepdims=True)
    o = (_bdot(pp, vp, "nn") + _bdot(pc, vc, "nn")) / den
    return o, jnp.broadcast_to(m + jnp.log(den), (n, ATTN_HEAD_DIM))


def _head_slope(gi, h):
    s = [float(v) * DILATIONS[gi] for v in SLOPES[gi]]
    return jnp.where(h == 0, s[0], jnp.where(h == 1, s[1], jnp.where(h == 2, s[2], s[3])))


ATTN_UNROLL = 4


def _attn_heads_per_block(d):
    return 4 if d == 1 else 1


def _units(ref, d, hb):
    if d == 1:
        return [ref[:, h * ATTN_HEAD_DIM:(h + 1) * ATTN_HEAD_DIM] for h in range(hb)]
    return [ref[pl.ds(r, ATTN_STEPS, stride=d), :] for r in range(d)]


def _store_units(ref, src, d, hb):
    if d == 1:
        for h in range(hb):
            ref[:, h * ATTN_HEAD_DIM:(h + 1) * ATTN_HEAD_DIM] = src[h]
    else:
        for r in range(d):
            ref[pl.ds(r, ATTN_STEPS, stride=d), :] = src[r]


def _attn_fwd(qn, kn, vv, gi):
    d = DILATIONS[gi]
    s = qn.shape[0]
    span = ATTN_STEPS * d
    nb = s // span

    hb, units = _attn_heads_per_block(d), _attn_heads_per_block(d) * d

    def body(q_ref, k_ref, v_ref, o_ref, l_ref, sq, sk, sv, so, sl):
        h0, b = pl.program_id(0) * hb, pl.program_id(1)
        cur, prev = b % 2, (b + 1) % 2

        @pl.when(b == 0)
        def _():
            sk[prev] = jnp.zeros(sk.shape[1:], F32)
            sv[prev] = jnp.zeros(sv.shape[1:], F32)

        for u, (qr, kr, vr) in enumerate(zip(_units(q_ref, d, hb), _units(k_ref, d, hb), _units(v_ref, d, hb))):
            sq[u] = qr
            sk[cur, u] = kr
            sv[cur, u] = vr

        def step(i, carry):
            for j in range(ATTN_UNROLL):
                u = i * ATTN_UNROLL + j
                so[u], sl[u] = _attn_head(sq[u], sk[prev, u], sk[cur, u], sv[prev, u], sv[cur, u], b > 0,
                                          _head_slope(gi, h0 + u // d))
            return carry

        lax.fori_loop(0, units // ATTN_UNROLL, step, 0)
        _store_units(o_ref, so, d, hb)
        _store_units(l_ref, sl, d, hb)

    blk = pl.BlockSpec((span, hb * ATTN_HEAD_DIM), lambda h, b: (b, (gi * 4) // hb + h))
    out = pl.BlockSpec((span, hb * ATTN_HEAD_DIM), lambda h, b: (b, h))
    res = (units, ATTN_STEPS, ATTN_HEAD_DIM)
    return pl.pallas_call(
        body, grid=(4 // hb, nb), in_specs=[blk, blk, blk], out_specs=[out, out],
        out_shape=[jax.ShapeDtypeStruct((s, ATTN_GROUP_W), F32)] * 2,
        scratch_shapes=[pltpu.VMEM(res, F32), pltpu.VMEM((2,) + res, F32), pltpu.VMEM((2,) + res, F32),
                        pltpu.VMEM(res, F32), pltpu.VMEM(res, F32)],
        compiler_params=_cp("parallel", "arbitrary"), name=f"attn_fwd_g{gi}")(qn, kn, vv)


def _attn_bwd(qn, kn, vv, do, dl, gi):
    d = DILATIONS[gi]
    s = qn.shape[0]
    span = ATTN_STEPS * d
    nb = s // span

    hb, units = _attn_heads_per_block(d), _attn_heads_per_block(d) * d

    def body(q_ref, kp_ref, kc_ref, vp_ref, vc_ref, do_ref, dl_ref, dq_ref, dk_ref, dv_ref, sin, sout, ck, cv):
        h0, bi = pl.program_id(0) * hb, pl.program_id(1)

        @pl.when(bi == 0)
        def _():
            ck[...] = jnp.zeros_like(ck)
            cv[...] = jnp.zeros_like(cv)

        for i, ref in enumerate((q_ref, kp_ref, kc_ref, vp_ref, vc_ref, do_ref, dl_ref)):
            for u, val in enumerate(_units(ref, d, hb)):
                sin[i, u] = val
        has_prev = bi < nb - 1

        def step(i, carry):
            for j in range(ATTN_UNROLL):
                u = i * ATTN_UNROLL + j
                f = functools.partial(_attn_head, has_prev=has_prev, slope=_head_slope(gi, h0 + u // d))
                _, vjp = jax.vjp(f, sin[0, u], sin[1, u], sin[2, u], sin[3, u], sin[4, u])
                dq, dkp, dkc, dvp, dvc = vjp((sin[5, u], sin[6, u]))
                sout[0, u] = dq
                sout[1, u] = dkc + ck[u]
                sout[2, u] = dvc + cv[u]
                ck[u] = dkp
                cv[u] = dvp
            return carry

        lax.fori_loop(0, units // ATTN_UNROLL, step, 0)
        for i, ref in enumerate((dq_ref, dk_ref, dv_ref)):
            _store_units(ref, sout.at[i], d, hb)

    w = hb * ATTN_HEAD_DIM
    cur = pl.BlockSpec((span, w), lambda h, b: (nb - 1 - b, (gi * 4) // hb + h))
    prev = pl.BlockSpec((span, w), lambda h, b: (jnp.maximum(nb - 2 - b, 0), (gi * 4) // hb + h))
    out = pl.BlockSpec((span, w), lambda h, b: (nb - 1 - b, h))
    res = (units, ATTN_STEPS, ATTN_HEAD_DIM)
    return pl.pallas_call(
        body, grid=(4 // hb, nb), in_specs=[cur, prev, cur, prev, cur, out, out], out_specs=[out, out, out],
        out_shape=[jax.ShapeDtypeStruct((s, ATTN_GROUP_W), F32)] * 3,
        scratch_shapes=[pltpu.VMEM((7,) + res, F32), pltpu.VMEM((3,) + res, F32), pltpu.VMEM(res, F32), pltpu.VMEM(res, F32)],
        compiler_params=_cp("parallel", "arbitrary"), name=f"attn_bwd_g{gi}")(qn, kn, kn, vv, vv, do, dl)


def _layer_fwd(x, mod, W, l):
    sh1, sc1, g1, sh2, sc2, g2 = (mod[i:i + 1] for i in range(6))
    (h,) = _rowwise_fwd("norm1", _f_norm, [(x, 1024, 0)], [W["norm1_w"], sc1, sh1], [(1024, BF16)])
    proj = _matmul("in_proj", h, W["w_in"], "nn")
    xbc = _conv_fwd(proj, W["conv_w"], W["conv_b"])
    y, hist = _ssd_fwd(xbc, proj, W["dt_bias"], W["a_log"], W["d_skip"])
    (yn,) = _rowwise_fwd("ssd_gate", _f_ssdgate, [(y, 1024, 0), (proj, 1024, P_Z // 1024)], [W["ssd_norm_w"]], [(1024, BF16)])
    y_ssd = _matmul("ssd_out", yn, W["w_ssd_out"], "nn", layer=l)
    qn, kn, vv = _rowwise_fwd("qk_norm", _f_qknorm, [(proj, 1536, P_Q // 1536), (proj, 1536, P_K // 1536), (proj, 1536, P_V // 1536)],
                              [W["q_norm_w"], W["k_norm_w"]], [(1536, F32)] * 3)
    ol = [_attn_fwd(qn, kn, vv, gi) for gi in range(3)]
    (o,) = _rowwise_fwd("attn_combine", _f_combine, [(t[0], 512, 0) for t in ol] + [(t[1], 512, 0) for t in ol], [], [(512, BF16)])
    y_attn = _matmul("attn_out", o, W["w_attn_out"], "nn", layer=l, chips=256)
    pooled = _pool_fwd(proj)
    pm = _group_matmul("pool_mix", pooled, W["w_pool_mix"], "nn", layer=l)
    (ps,) = _rowwise_fwd("pool_scale", _f_poolscale, [(pm, 1024, 0)], [W["pool_scale"]], [(1024, BF16)])
    y_pool = _matmul("pool_out", ps, W["w_pool_out"], "nn", layer=l)
    (merged,) = _rowwise_fwd("merge", _f_merge, [(proj, 3072, P_GATES // 3072), (y_ssd, 1024, 0), (y_attn, 1024, 0), (y_pool, 1024, 0)],
                             [], [(1024, BF16)])
    mo = _matmul("mix_out", merged, W["w_out"], "nn", layer=l)
    (x1,) = _rowwise_fwd("resid1", _f_resid, [(x, 1024, 0), (mo, 1024, 0)], [g1], [(1024, F32)])
    (h2,) = _rowwise_fwd("norm2", _f_norm, [(x1, 1024, 0)], [W["norm2_w"], sc2, sh2], [(1024, BF16)])
    a = _matmul("ff1", h2, W["w_ff1"], "nn", layer=l, chips=1024)
    (r,) = _rowwise_fwd("relu2", _f_relu2, [(a, D_FF, 0)], [], [(D_FF, BF16)], tb=128)
    ff = _matmul("ff2", r, W["w_ff2"], "nn", layer=l)
    (x2,) = _rowwise_fwd("resid2", _f_resid, [(x1, 1024, 0), (ff, 1024, 0)], [g2], [(1024, F32)])
    saved = dict(x=x, h=h, proj=proj, xbc=xbc, y=y, hist=hist, yn=yn, y_ssd=y_ssd, qn=qn, kn=kn, vv=vv, ol=ol, o=o,
                 y_attn=y_attn, pooled=pooled, pm=pm, ps=ps, y_pool=y_pool, merged=merged, mo=mo, x1=x1, h2=h2, a=a, r=r, ff=ff)
    return x2, saved


def _layer_bwd(dx2, mod, W, sv, l):
    sh1, sc1, g1, sh2, sc2, g2 = (mod[i:i + 1] for i in range(6))
    g = {}
    dx1a, dff, dg2 = _rowwise_bwd("resid2_bwd", _f_resid, [(sv["x1"], 1024, 0), (sv["ff"], 1024, 0)], [g2], [(dx2, 1024, 0)],
                                  [True, True], gdt=[F32, BF16])
    g["w_ff2"] = _matmul("ff2_dw", sv["r"], dff, "tn", BF16).reshape(N_CHIPS, D_FF // N_CHIPS, D_MODEL)
    dr = _matmul("ff2_dx", dff, W["w_ff2"], "nt", layer=l)
    (da,) = _rowwise_bwd("relu2_bwd", _f_relu2, [(sv["a"], D_FF, 0)], [], [(dr, D_FF, 0)], [True], tb=128, gdt=[BF16])
    g["w_ff1"] = _matmul("ff1_dw", sv["h2"], da, "tn", BF16, out_chips=1024)
    dh2 = _matmul("ff1_dx", da, W["w_ff1"], "nt", layer=l, chips=1024)
    dx1, g["norm2_w"], dsc2, dsh2 = _rowwise_bwd("norm2_bwd", _f_norm, [(sv["x1"], 1024, 0)], [W["norm2_w"], sc2, sh2],
                                                 [(dh2, 1024, 0)], [True], add=(0, dx1a))
    dxa, dmo, dg1 = _rowwise_bwd("resid1_bwd", _f_resid, [(sv["x"], 1024, 0), (sv["mo"], 1024, 0)], [g1], [(dx1, 1024, 0)],
                                 [True, True], gdt=[F32, BF16])
    g["w_out"] = _matmul("mix_out_dw", sv["merged"], dmo, "tn", BF16).reshape(N_CHIPS, D_MODEL // N_CHIPS, D_MODEL)
    dmerged = _matmul("mix_out_dx", dmo, W["w_out"], "nt", layer=l)
    proj = sv["proj"]
    dgates, dy_ssd, dy_attn, dy_pool = _rowwise_bwd(
        "merge_bwd", _f_merge, [(proj, 3072, P_GATES // 3072), (sv["y_ssd"], 1024, 0), (sv["y_attn"], 1024, 0), (sv["y_pool"], 1024, 0)],
        [], [(dmerged, 1024, 0)], [True] * 4, gdt=[BF16] * 4)
    g["w_pool_out"] = _matmul("pool_out_dw", sv["ps"], dy_pool, "tn", BF16).reshape(N_CHIPS, D_MODEL // N_CHIPS, D_MODEL)
    dps = _matmul("pool_out_dx", dy_pool, W["w_pool_out"], "nt", layer=l)
    dpm, g["pool_scale"] = _rowwise_bwd("pool_scale_bwd", _f_poolscale, [(sv["pm"], 1024, 0)], [W["pool_scale"]], [(dps, 1024, 0)],
                                        [True], gdt=[BF16])
    dmix = _group_matmul("pool_mix_dw", sv["pooled"], dpm, "tn")
    g["w_pool_mix"] = jnp.moveaxis(dmix.reshape(4, N_CHIPS, POOL_GW // N_CHIPS, POOL_GW), 1, 0).astype(BF16)
    dpooled = _group_matmul("pool_mix_dx", dpm, W["w_pool_mix"], "nt", layer=l)
    du = _pool_bwd(dpooled)
    g["w_attn_out"] = _matmul("attn_out_dw", sv["o"], dy_attn, "tn", BF16, out_chips=256)
    do = _matmul("attn_out_dx", dy_attn, W["w_attn_out"], "nt", layer=l, chips=256)
    ol = sv["ol"]
    dol = _rowwise_bwd("attn_combine_bwd", _f_combine, [(t[0], 512, 0) for t in ol] + [(t[1], 512, 0) for t in ol], [],
                       [(do, 512, 0)], [True] * 6)
    dqs, dks, dvs = zip(*[_attn_bwd(sv["qn"], sv["kn"], sv["vv"], dol[gi], dol[3 + gi], gi) for gi in range(3)])
    dqn, dkn, dvv = (jnp.concatenate(t, axis=1) for t in (dqs, dks, dvs))
    dq, dk, dv, g["q_norm_w"], g["k_norm_w"] = _rowwise_bwd(
        "qk_norm_bwd", _f_qknorm, [(proj, 1536, P_Q // 1536), (proj, 1536, P_K // 1536), (proj, 1536, P_V // 1536)],
        [W["q_norm_w"], W["k_norm_w"]], [(dqn, 1536, 0), (dkn, 1536, 0), (dvv, 1536, 0)], [True] * 3, gdt=[BF16] * 3)
    g["w_ssd_out"] = _matmul("ssd_out_dw", sv["yn"], dy_ssd, "tn", BF16).reshape(N_CHIPS, D_MODEL // N_CHIPS, D_MODEL)
    dyn = _matmul("ssd_out_dx", dy_ssd, W["w_ssd_out"], "nt", layer=l)
    dy, dz, g["ssd_norm_w"] = _rowwise_bwd("ssd_gate_bwd", _f_ssdgate, [(sv["y"], 1024, 0), (proj, 1024, P_Z // 1024)], [W["ssd_norm_w"]],
                                           [(dyn, 1024, 0)], [True, True], gdt=[F32, BF16])
    dxbc, ddt, g["dt_bias"], g["a_log"], g["d_skip"] = _ssd_bwd(sv["xbc"], proj, sv["hist"], W["dt_bias"], W["a_log"], W["d_skip"], dy)
    dxbc_raw, g["conv_w"], g["conv_b"] = _conv_bwd(proj, W["conv_w"], W["conv_b"], dxbc)
    dproj = jnp.concatenate([dxbc_raw, dq, dk, dv, dgates, dz, du, ddt], axis=1)
    g["w_in"] = _matmul("in_proj_dw", sv["h"], dproj, "tn", BF16)
    dh = _matmul("in_proj_dx", dproj, W["w_in"], "nt")
    dx, g["norm1_w"], dsc1, dsh1 = _rowwise_bwd("norm1_bwd", _f_norm, [(sv["x"], 1024, 0)], [W["norm1_w"], sc1, sh1],
                                                [(dh, 1024, 0)], [True], add=(0, dxa))
    dmod = jnp.concatenate([dsh1, dsc1, dg1, dsh2, dsc2, dg2], axis=0)
    return dx, dmod, g


def _expand_heads(t):
    return jnp.repeat(t, SSD_HEAD_DIM, axis=-1)


def _reduce_heads(t):
    return t.reshape(t.shape[:-1] + (SSD_HEADS, SSD_HEAD_DIM)).sum(-1)


_IN_SPLITS = np.cumsum((0,) + IN_SIZES)


def _w_in_to_layout(w):
    z, xbc, dt, q, k, v, u, gates = (w[:, _IN_SPLITS[i]:_IN_SPLITS[i + 1]] for i in range(8))
    return jnp.concatenate([xbc, q, k, v, gates, z, u, _expand_heads(dt)], axis=1)


def _w_in_from_layout(g):
    xbc, q, k, v = (g[:, o:o + 1536] for o in (P_XBC, P_Q, P_K, P_V))
    gates, z, u, dt = g[:, P_GATES:P_GATES + 3072], g[:, P_Z:P_Z + 1024], g[:, P_U:P_U + 1024], g[:, P_DT:P_DT + 1024]
    return jnp.concatenate([z, xbc, _reduce_heads(dt.astype(F32)).astype(g.dtype), q, k, v, u, gates], axis=1)


_STACKED = ("w_ssd_out", "w_attn_out", "w_pool_mix", "w_pool_out", "w_out", "w_ff1", "w_ff2")
_ROWS = ("norm1_w", "norm2_w", "conv_b", "ssd_norm_w", "q_norm_w", "k_norm_w", "pool_scale")
_HEAD_ROWS = ("dt_bias", "a_log", "d_skip")


def _layer_weights(wg, lg, small, l):
    W = {k: wg[k] for k in _STACKED}
    w_in = jnp.concatenate([wg["w_in"][lg, j] for j in range(N_CHIPS)], axis=1)
    W["w_in"] = _w_in_to_layout(w_in)
    W["conv_w"] = small["conv_w"][l]
    for k in _ROWS:
        W[k] = small[k][l][None, :]
    for k in _HEAD_ROWS:
        W[k] = _expand_heads(small[k][l])[None, :]
    return W


def _layer_grads_by_chip(g):
    out = dict(g)
    w_in = _w_in_from_layout(g["w_in"])
    out["w_in"] = jnp.moveaxis(w_in.reshape(D_MODEL, N_CHIPS, IN_WIDTH // N_CHIPS), 1, 0)
    for k in _ROWS:
        out[k] = g[k][0]
    for k in _HEAD_ROWS:
        out[k] = _reduce_heads(g[k][0])
    return out


ANY = pl.BlockSpec(memory_space=pl.ANY)


def _place():
    x, y, c = lax.axis_index("x"), lax.axis_index("y"), lax.axis_index("c")
    return x, y, c, (x, y, 1 - c), [(1 - x, y), (x, 1 - y), (1 - x, 1 - y)]


def _allgather8(name, blk):
    m_per, n = blk.shape

    def body(x_ref, out_ref, send_sems, recv_sems, local_sem):
        x, y, c, sibling, chips = _place()
        me = (x, y, c)

        def rows(px, py, pc):
            return out_ref.at[pl.ds((4 * px + 2 * py + pc) * m_per, m_per), :]

        def copy(k, block, to, src=None):
            return pltpu.make_async_remote_copy(
                src_ref=rows(*block) if src is None else src, dst_ref=rows(*block),
                send_sem=send_sems.at[k], recv_sem=recv_sems.at[k], device_id=to, device_id_type=MESH)

        mine = pltpu.make_async_copy(x_ref, rows(*me), local_sem)
        mine.start()
        first = [copy(0, me, sibling, src=x_ref)]
        first += [copy(1 + j, me, (*chip, c), src=x_ref) for j, chip in enumerate(chips)]
        for cp in first:
            cp.start()
        passed = [copy(4 + j, (*chip, c), sibling) for j, chip in enumerate(chips)]
        for j, chip in enumerate(chips):
            copy(1 + j, (*chip, c), me).wait_recv()
            passed[j].start()
        copy(0, sibling, me).wait_recv()
        for j, chip in enumerate(chips):
            copy(4 + j, (*chip, 1 - c), me).wait_recv()
        for cp in first + passed:
            cp.wait_send()
        mine.wait()

    return pl.pallas_call(
        body, out_shape=jax.ShapeDtypeStruct((N_DEV * m_per, n), blk.dtype),
        in_specs=[pl.BlockSpec(memory_space=pltpu.VMEM)], out_specs=pl.BlockSpec(memory_space=pltpu.VMEM),
        scratch_shapes=[pltpu.SemaphoreType.DMA((7,)), pltpu.SemaphoreType.DMA((7,)), pltpu.SemaphoreType.DMA],
        name=name)(blk)


HBM_SPEC = pl.BlockSpec(memory_space=pltpu.HBM)
SEM_SPEC = pl.BlockSpec(memory_space=pltpu.SEMAPHORE)
SIDE_EFFECT = pltpu.SideEffectType.DATAFLOW_SIDE_EFFECTING


def _dma_sems(n):
    return [pltpu.SemaphoreType.DMA((n,)), pltpu.SemaphoreType.DMA((n,))]


def _halves(ref, axis, c):
    r2 = ref.shape[axis] // 2
    lead = (slice(None),) * axis
    return ref.at[lead + (pl.ds(r2 * c, r2),)], ref.at[lead + (pl.ds(r2 * (1 - c), r2),)]


def _gather_copies(srcs, lands, send_sems, recv_sems):
    x, y, c, _, chips = _place()
    sends, lands_here = [], []
    for j, (cx, cy) in enumerate(chips):
        for i, (s, t) in enumerate(zip(srcs, lands)):
            k = 3 * i + j
            mine = _halves(t.at[:, 2 * x + y], 1, c)[0]
            theirs = _halves(t.at[:, 2 * cx + cy], 1, c)[0]
            sends.append(pltpu.make_async_remote_copy(src_ref=_halves(s, 1, c)[0], dst_ref=mine, send_sem=send_sems.at[k],
                                                      recv_sem=recv_sems.at[k], device_id=(cx, cy, c), device_id_type=MESH))
            lands_here.append(pltpu.make_async_remote_copy(src_ref=theirs, dst_ref=theirs, send_sem=send_sems.at[k],
                                                           recv_sem=recv_sems.at[k], device_id=(cx, cy, c), device_id_type=MESH))
    return sends, lands_here


def _exchange_copies(srcs, lands, send_sems, recv_sems):
    x, y, c, _, chips = _place()
    sends = [pltpu.make_async_remote_copy(src_ref=s.at[2 * cx + cy], dst_ref=t.at[j], send_sem=send_sems.at[3 * i + j],
                                          recv_sem=recv_sems.at[3 * i + j], device_id=(cx, cy, c), device_id_type=MESH)
             for j, (cx, cy) in enumerate(chips) for i, (s, t) in enumerate(zip(srcs, lands))]
    return sends, sends


def _split_start(name, copies, srcs, lands, after):
    ns, nl = len(srcs), len(lands)
    n_copies = 3 * ns

    def body(*refs):
        send_sems, recv_sems = refs[ns + nl + 1], refs[ns + nl + 2]
        for cp in copies(refs[:ns], refs[ns:ns + nl], send_sems, recv_sems)[0]:
            cp.start()
        refs[-1][...] = jnp.zeros_like(refs[-1])

    arrs = list(srcs) + list(lands)
    res = pl.pallas_call(
        body, name=name,
        out_shape=(pltpu.SemaphoreType.DMA((n_copies,)), pltpu.SemaphoreType.DMA((n_copies,)))
        + tuple(pltpu.HBM(a.shape, a.dtype) for a in arrs) + (jax.ShapeDtypeStruct((8, LANES), F32),),
        in_specs=[HBM_SPEC] * (ns + nl) + [ANY],
        out_specs=(SEM_SPEC, SEM_SPEC) + (HBM_SPEC,) * (ns + nl) + (pl.BlockSpec(memory_space=pltpu.VMEM),),
        input_output_aliases={i: 2 + i for i in range(ns + nl)},
        compiler_params=pltpu.CompilerParams(has_side_effects=SIDE_EFFECT),
    )(*[pltpu.with_memory_space_constraint(a, pltpu.HBM) for a in arrs], after)
    return res[0], res[1], list(res[2:2 + ns]), list(res[2 + ns:2 + ns + nl]), res[-1]


def _split_wait(name, copies, send_sems, recv_sems, srcs, lands, after):
    ns, nl = len(srcs), len(lands)

    def body(*refs):
        sends, lands_here = copies(refs[:ns], refs[ns:ns + nl], refs[ns + nl], refs[ns + nl + 1])
        for cp in sends:
            cp.wait_send()
        for cp in lands_here:
            cp.wait_recv()

    arrs = list(srcs) + list(lands)
    res = pl.pallas_call(
        body, name=name, out_shape=tuple(pltpu.HBM(a.shape, a.dtype) for a in arrs),
        in_specs=[HBM_SPEC] * (ns + nl) + [SEM_SPEC, SEM_SPEC, ANY], out_specs=(HBM_SPEC,) * (ns + nl),
        input_output_aliases={i: i for i in range(ns + nl)},
        compiler_params=pltpu.CompilerParams(has_side_effects=SIDE_EFFECT),
    )(*arrs, send_sems, recv_sems, after)
    return list(res[:ns]), list(res[ns:])


def _gather_forward(name, lands):
    n = len(lands)

    def body(*refs):
        ins, outs = refs[:n], refs[n:2 * n]
        send_sems, recv_sems = refs[2 * n:]
        x, y, c, sibling, chips = _place()
        sends, arrivals = [], []
        for j, (cx, cy) in enumerate(chips):
            for i in range(n):
                k = 3 * i + j
                src = _halves(ins[i].at[:, 2 * cx + cy], 1, c)[0]
                dst, theirs = _halves(outs[i].at[:, 2 * cx + cy], 1, c)
                sends.append(pltpu.make_async_remote_copy(src_ref=src, dst_ref=dst, send_sem=send_sems.at[k], recv_sem=recv_sems.at[k],
                                                          device_id=sibling, device_id_type=MESH))
                arrivals.append(pltpu.make_async_remote_copy(src_ref=theirs, dst_ref=theirs, send_sem=send_sems.at[k],
                                                             recv_sem=recv_sems.at[k], device_id=sibling, device_id_type=MESH))
        for cp in sends:
            cp.start()
        for cp in arrivals:
            cp.wait_recv()
        for cp in sends:
            cp.wait_send()

    return pl.pallas_call(
        body, out_shape=[jax.ShapeDtypeStruct(t.shape, t.dtype) for t in lands], in_specs=[ANY] * n, out_specs=[ANY] * n,
        input_output_aliases={i: i for i in range(n)}, scratch_shapes=_dma_sems(3 * n), name=name)(*lands)


def _swap_halves(name, gs):
    n = len(gs)

    def body(*refs):
        ins, got = refs[:n], refs[n:2 * n]
        send_sems, recv_sems = refs[2 * n:]
        x, y, c, sibling, _ = _place()
        sends = [pltpu.make_async_remote_copy(src_ref=_halves(ins[i], 2, c)[1], dst_ref=got[i], send_sem=send_sems.at[i],
                                              recv_sem=recv_sems.at[i], device_id=sibling, device_id_type=MESH) for i in range(n)]
        for cp in sends:
            cp.start()
        for cp in sends:
            cp.wait_recv()
        for cp in sends:
            cp.wait_send()

    return pl.pallas_call(
        body, out_shape=[jax.ShapeDtypeStruct(t.shape[:2] + (t.shape[2] // 2,) + t.shape[3:], t.dtype) for t in gs],
        in_specs=[ANY] * n, out_specs=[ANY] * n, scratch_shapes=_dma_sems(n), name=name)(*gs)


def _share_halves(name, ts):
    n = len(ts)

    def body(*refs):
        ins, outs = refs[:n], refs[n:2 * n]
        send_sems, recv_sems = refs[2 * n:]
        x, y, c, sibling, _ = _place()
        sends, arrivals = [], []
        for i in range(n):
            mine, theirs = _halves(outs[i], 1, c)
            sends.append(pltpu.make_async_remote_copy(src_ref=ins[i], dst_ref=mine, send_sem=send_sems.at[i], recv_sem=recv_sems.at[i],
                                                      device_id=sibling, device_id_type=MESH))
            arrivals.append(pltpu.make_async_remote_copy(src_ref=ins[i], dst_ref=theirs, send_sem=send_sems.at[i],
                                                         recv_sem=recv_sems.at[i], device_id=sibling, device_id_type=MESH))
        for cp in sends:
            cp.start()
        for cp in arrivals:
            cp.wait_recv()
        for cp in sends:
            cp.wait_send()

    return pl.pallas_call(
        body, out_shape=[jax.ShapeDtypeStruct((t.shape[0], 2 * t.shape[1]) + t.shape[2:], t.dtype) for t in ts],
        in_specs=[ANY] * n, out_specs=[ANY] * n, scratch_shapes=_dma_sems(n), name=name)(*ts)


PACK_W = 1024
PACK_TB = 512


def _sum_rows(name, parts, out_dtype):
    def f(*vals):
        acc = vals[0]
        for v in vals[1:]:
            acc = acc + v
        return (acc,)

    return _rowwise_fwd(name, f, [(p, PACK_W, 0) for p in parts], [], [(PACK_W, out_dtype)], tb=_tile(parts[0].shape[0], PACK_TB))[0]


def _sum_slots(name, ops, rows, out_dtype):
    cw = ops[0][0].shape[1]
    tb = rows
    while tb * cw > 300_000 and tb % 32 == 0:
        tb //= 2
    per = rows // tb

    def body(*refs):
        acc = refs[0][...].astype(F32)
        for r in refs[1:-1]:
            acc = acc + r[...].astype(F32)
        refs[-1][...] = acc.astype(refs[-1].dtype)

    return pl.pallas_call(
        body, grid=(per,), in_specs=[pl.BlockSpec((tb, cw), lambda i, s=s: (s * per + i, 0)) for _, s in ops],
        out_specs=pl.BlockSpec((tb, cw), lambda i: (i, 0)), out_shape=jax.ShapeDtypeStruct((rows, cw), out_dtype),
        compiler_params=_cp("parallel"), name=name)(*[a for a, _ in ops])


def _adamw(name, w, g, m, v):
    r, cw = w.shape
    tb = r
    while tb * cw > 400_000 and tb % 16 == 0:
        tb //= 2
    c1 = 1.0 / (1.0 - ADAM_B1 ** ADAM_STEP)
    c2 = 1.0 / (1.0 - ADAM_B2 ** ADAM_STEP)

    def body(w_ref, g_ref, m_ref, v_ref, d_ref, mo_ref, vo_ref):
        gg = g_ref[...]
        mn = ADAM_B1 * m_ref[...] + (1.0 - ADAM_B1) * gg
        vn = ADAM_B2 * v_ref[...] + (1.0 - ADAM_B2) * jnp.square(gg)
        d_ref[...] = -ADAM_LR * ((mn * c1) / (jnp.sqrt(vn * c2) + ADAM_EPS) + ADAM_WD * w_ref[...])
        mo_ref[...] = mn
        vo_ref[...] = vn

    spec = pl.BlockSpec((tb, cw), lambda i: (i, 0))
    return pl.pallas_call(
        body, grid=(r // tb,), in_specs=[spec] * 4, out_specs=[spec] * 3,
        out_shape=[jax.ShapeDtypeStruct((r, cw), F32)] * 3, compiler_params=_cp("parallel"), name=name)(w, g, m, v)


def _silu_rows(c):
    def body(c_ref, o_ref):
        rows = lax.broadcasted_iota(jnp.int32, o_ref.shape, 0)
        o_ref[...] = jnp.where(rows == 0, jnp.broadcast_to(_silu(c_ref[...]), o_ref.shape), 0.0)

    return pl.pallas_call(body, out_shape=jax.ShapeDtypeStruct((8, c.shape[1]), F32), name="cond_silu")(c)


_KINDS = ("w_in", "w_ssd_out", "w_attn_out", "w_pool_mix", "w_pool_out", "w_out", "w_ff1", "w_ff2")
_SMALL = ("b_ada", "norm1_w", "norm2_w", "conv_b", "dt_bias", "a_log", "d_skip", "ssd_norm_w", "q_norm_w", "k_norm_w",
          "pool_scale")
_ORDER = ("w_ada", "b_ada", "norm1_w", "norm2_w", "w_in", "conv_w", "conv_b", "dt_bias", "a_log", "d_skip", "ssd_norm_w",
          "w_ssd_out", "q_norm_w", "k_norm_w", "w_attn_out", "w_pool_mix", "pool_scale", "w_pool_out", "w_out", "w_ff1", "w_ff2")


def _pack_flat(arrs, rows, dtype):
    flat = jnp.concatenate([a.reshape(-1).astype(dtype) for a in arrs])
    return jnp.pad(flat, (0, rows * PACK_W - flat.shape[0])).reshape(rows, PACK_W)


def _unpack_flat(buf, shapes):
    flat = buf.reshape(-1)
    out, off = [], 0
    for shp in shapes:
        n = int(np.prod(shp))
        out.append(flat[off:off + n].reshape(shp))
        off += n
    return out


def _small_rows(n_elems):
    return -(-n_elems // (8 * PACK_W)) * 8


def kernel(x, c, w_ada, b_ada, norm1_w, norm2_w, w_in, conv_w, conv_b, dt_bias, a_log, d_skip, ssd_norm_w, w_ssd_out, q_norm_w, k_norm_w, w_attn_out, w_pool_mix, pool_scale, w_pool_out, w_out, w_ff1, w_ff2, loss_target, m_w_ada, m_b_ada, m_norm1_w, m_norm2_w, m_w_in, m_conv_w, m_conv_b, m_dt_bias, m_a_log, m_d_skip, m_ssd_norm_w, m_w_ssd_out, m_q_norm_w, m_k_norm_w, m_w_attn_out, m_w_pool_mix, m_pool_scale, m_w_pool_out, m_w_out, m_w_ff1, m_w_ff2, v_w_ada, v_b_ada, v_norm1_w, v_norm2_w, v_w_in, v_conv_w, v_conv_b, v_dt_bias, v_a_log, v_d_skip, v_ssd_norm_w, v_w_ssd_out, v_q_norm_w, v_k_norm_w, v_w_attn_out, v_w_pool_mix, v_pool_scale, v_w_pool_out, v_w_out, v_w_ff1, v_w_ff2):
    w = dict(w_ada=w_ada, b_ada=b_ada, norm1_w=norm1_w, norm2_w=norm2_w, w_in=w_in, conv_w=conv_w, conv_b=conv_b, dt_bias=dt_bias, a_log=a_log, d_skip=d_skip, ssd_norm_w=ssd_norm_w, w_ssd_out=w_ssd_out, q_norm_w=q_norm_w, k_norm_w=k_norm_w, w_attn_out=w_attn_out, w_pool_mix=w_pool_mix, pool_scale=pool_scale, w_pool_out=w_pool_out, w_out=w_out, w_ff1=w_ff1, w_ff2=w_ff2)
    m = dict(w_ada=m_w_ada, b_ada=m_b_ada, norm1_w=m_norm1_w, norm2_w=m_norm2_w, w_in=m_w_in, conv_w=m_conv_w, conv_b=m_conv_b, dt_bias=m_dt_bias, a_log=m_a_log, d_skip=m_d_skip, ssd_norm_w=m_ssd_norm_w, w_ssd_out=m_w_ssd_out, q_norm_w=m_q_norm_w, k_norm_w=m_k_norm_w, w_attn_out=m_w_attn_out, w_pool_mix=m_w_pool_mix, pool_scale=m_pool_scale, w_pool_out=m_w_pool_out, w_out=m_w_out, w_ff1=m_w_ff1, w_ff2=m_w_ff2)
    v = dict(w_ada=v_w_ada, b_ada=v_b_ada, norm1_w=v_norm1_w, norm2_w=v_norm2_w, w_in=v_w_in, conv_w=v_conv_w, conv_b=v_conv_b, dt_bias=v_dt_bias, a_log=v_a_log, d_skip=v_d_skip, ssd_norm_w=v_ssd_norm_w, w_ssd_out=v_w_ssd_out, q_norm_w=v_q_norm_w, k_norm_w=v_k_norm_w, w_attn_out=v_w_attn_out, w_pool_mix=v_w_pool_mix, pool_scale=v_pool_scale, w_pool_out=v_w_pool_out, w_out=v_w_out, w_ff1=v_w_ff1, w_ff2=v_w_ff2)
    chip = 2 * lax.axis_index("x") + lax.axis_index("y")
    dev = 2 * chip + lax.axis_index("c")
    ada_cols = w_ada.shape[2]

    n_conv = conv_w.size // PACK_W
    rows1 = _small_rows((1 + n_conv) * PACK_W)
    blk = jnp.concatenate([_silu_rows(c)[:1], conv_w.reshape(n_conv, PACK_W), jnp.zeros((rows1 - 1 - n_conv, PACK_W), F32)])
    first = _allgather8("gather_cond", blk).reshape(N_DEV, rows1, PACK_W)
    cond_all = first[:, 0]
    conv_all = first[0::2, 1:1 + n_conv].reshape((N_CHIPS,) + conv_w.shape)
    conv_full = jnp.moveaxis(conv_all, 0, 2).reshape(DEPTH, SSD_CONV, CONV_DIM)
    b_cols = lax.dynamic_slice_in_dim(b_ada, chip * ada_cols, ada_cols, axis=1)
    mod_cols = jnp.stack([_matmul("ada_fwd", cond_all, w_ada[l], "nn", precise=True) + b_cols[l][None, :] for l in range(DEPTH)])
    mod_all = _allgather8("gather_mod", mod_cols.reshape(-1, PACK_W)).reshape(N_DEV, DEPTH, N_DEV, ada_cols)
    mine = lax.dynamic_index_in_dim(mod_all[0::2], dev, axis=2, keepdims=False)
    mods = jnp.moveaxis(mine, 0, 1).reshape(DEPTH, 6, D_MODEL)

    core = lax.axis_index("c")
    small_w = dict({k: w[k] for k in _SMALL[1:]}, conv_w=conv_full)

    def start_gather(tag, lo, n, after):
        shards = [w[k][lo:lo + n].astype(BF16) for k in _KINDS]
        lands = [lax.empty((n, N_CHIPS) + s.shape[1:], BF16) for s in shards]
        return _split_start("gather_start_" + tag, _gather_copies, shards, lands, after)

    def finish_gather(tag, handle, after):
        shards, lands = _split_wait("gather_wait_" + tag, _gather_copies, handle[0], handle[1], handle[2], handle[3], after)
        lands = _gather_forward("gather_forward_" + tag, lands)
        wg = {k: lax.dynamic_update_slice_in_dim(t, s[:, None], chip, axis=1) for k, t, s in zip(_KINDS, lands, shards)}
        n = shards[0].shape[0]
        for k in ("w_ssd_out", "w_pool_out", "w_out", "w_ff2"):
            wg[k] = wg[k].reshape(n, -1, D_MODEL)
        wg["w_pool_mix"] = jnp.moveaxis(wg["w_pool_mix"], 1, 2).reshape(n, 4, POOL_GW, POOL_GW)
        return wg

    gather_a = start_gather("a", 0, 1, mods)
    wg_a = finish_gather("a", gather_a, gather_a[4])
    gather_b = start_gather("b", 1, DEPTH - 1, wg_a["w_in"])
    mods = mods + gather_b[4][0, 0]

    xc = x[0]
    Ws, saved = [None] * DEPTH, [None] * DEPTH
    Ws[0] = _layer_weights(wg_a, 0, small_w, 0)
    xc, saved[0] = _layer_fwd(xc, mods[0], Ws[0], 0)
    wg_b = finish_gather("b", gather_b, xc)
    for l in range(1, DEPTH):
        Ws[l] = _layer_weights(wg_b, l - 1, small_w, l)
        xc, saved[l] = _layer_fwd(xc, mods[l], Ws[l], l - 1)
    dx, loss = _loss_and_grad(xc, loss_target[0])
    dmods, grads = [None] * DEPTH, [None] * DEPTH
    for l in reversed(range(1, DEPTH)):
        dx, dmods[l], g = _layer_bwd(dx, mods[l], Ws[l], saved[l], l - 1)
        grads[l] = _layer_grads_by_chip(g)

    def flat(t):
        return t.reshape(-1, t.shape[-1])

    def start_exchange(tag, layers, after):
        gs = [jnp.stack([grads[l][k] for l in layers], axis=1) for k in _KINDS]
        got = _swap_halves("grad_swap_" + tag, gs)
        own = [lax.dynamic_slice_in_dim(t, (t.shape[2] // 2) * core, t.shape[2] // 2, axis=2) for t in gs]
        pairs = [_sum_slots(f"sum_pair_{tag}_{k}", [(flat(a), 0), (flat(b), 0)], flat(a).shape[0], BF16).reshape(a.shape)
                 for k, a, b in zip(_KINDS, own, got)]
        lands = [lax.empty((3,) + p.shape[1:], BF16) for p in pairs]
        return _split_start("exchange_start_" + tag, _exchange_copies, pairs, lands, after)

    def finish_exchange(tag, handle, after):
        pairs, partials = _split_wait("exchange_wait_" + tag, _exchange_copies, handle[0], handle[1], handle[2], handle[3], after)
        mine = [lax.dynamic_index_in_dim(p, chip, axis=0, keepdims=False) for p in pairs]
        totals = [_sum_slots(f"sum_chips_{tag}_{k}", [(flat(a), 0)] + [(flat(p), s) for s in range(3)], flat(a).shape[0], F32).reshape(a.shape)
                  for k, a, p in zip(_KINDS, mine, partials)]
        return [lax.dynamic_update_slice_in_dim(t, mine_t, mine_t.shape[1] * core, axis=1)
                for t, mine_t in zip(_share_halves("grad_share_" + tag, totals), totals)]

    exchange_b = start_exchange("b", range(1, DEPTH), dx)
    dx, dmods[0], g = _layer_bwd(dx, mods[0] + exchange_b[4][0, 0], Ws[0], saved[0], 0)
    grads[0] = _layer_grads_by_chip(g)
    g_b = finish_exchange("b", exchange_b, dx)
    exchange_a = start_exchange("a", range(0, 1), g_b[0])
    g_a = finish_exchange("a", exchange_a, exchange_a[4])
    grad_x, dmods = dx, jnp.stack(dmods)

    small = ([dmods] + [jnp.stack([grads[l][k] for l in range(DEPTH)]) for k in _SMALL[1:] + ("conv_w",)] + [loss[:, :1]])
    n_small = sum(int(np.prod(a.shape)) for a in small)
    rows_small = _small_rows(n_small)
    small_all = _allgather8("gather_small", _pack_flat(small, rows_small, F32))
    parts = [small_all[d * rows_small:(d + 1) * rows_small] for d in range(N_DEV)]
    small_sum = _unpack_flat(_sum_rows("sum_small", parts, F32), [a.shape for a in small])
    g_out = {"b_ada": small_sum[0].reshape(DEPTH, 6 * D_MODEL)}
    for k, t in zip(_SMALL[1:], small_sum[1:-2]):
        g_out[k] = t
    g_out["conv_w"] = lax.dynamic_slice_in_dim(small_sum[-2], chip * conv_w.shape[2], conv_w.shape[2], axis=2)
    loss_out = small_sum[-1][0, 0]
    dmod_all = jnp.stack([p[:DEPTH * 6].reshape(DEPTH, 6 * D_MODEL) for p in parts])
    dmod_cols = lax.dynamic_slice_in_dim(dmod_all, chip * ada_cols, ada_cols, axis=2)
    g_out["w_ada"] = jnp.stack([_matmul("ada_dw", cond_all, dmod_cols[:, l], "tn", precise=True) for l in range(DEPTH)])

    for k, ta, tb in zip(_KINDS, g_a, g_b):
        g_out[k] = jnp.concatenate([ta, tb], axis=0)

    deltas, new_m, new_v = {}, {}, {}
    for k in ("w_ada", "conv_w") + _KINDS:
        shp = w[k].shape
        two_d = (int(np.prod(shp[:-1])), shp[-1])
        res = _adamw("adamw_" + k, *(t.reshape(two_d) for t in (w[k], g_out[k], m[k], v[k])))
        deltas[k], new_m[k], new_v[k] = (t.reshape(shp) for t in res)
    small_shapes = [w[k].shape for k in _SMALL]
    n_sm = sum(int(np.prod(s)) for s in small_shapes)
    res = _adamw("adamw_small", *[_pack_flat([t[k] for k in _SMALL], _small_rows(n_sm), F32) for t in (w, g_out, m, v)])
    for name_map, buf in zip((deltas, new_m, new_v), res):
        for k, t in zip(_SMALL, _unpack_flat(buf, small_shapes)):
            name_map[k] = t

    return (loss_out, grad_x[None], *[g_out[k] for k in _ORDER], *[deltas[k] for k in _ORDER],
            *[new_m[k] for k in _ORDER], *[new_v[k] for k in _ORDER])
```

```python
import functools
import math

import numpy as np
import jax
import jax.numpy as jnp
from jax import lax
from jax.experimental import pallas as pl
from jax.experimental.pallas import tpu as pltpu

F32, BF16 = jnp.float32, jnp.bfloat16
MESH = pl.DeviceIdType.MESH

D_MODEL = 1024
DEPTH = 4
N_CHIPS = 4
N_DEV = 8
SSD_HEADS = 16
SSD_HEAD_DIM = 64
SSD_STATE = 128
SSD_CHUNK = 128
SSD_CONV = 4
CONV_DIM = 1536
ATTN_HEAD_DIM = 128
ATTN_GROUP_W = 512
DILATIONS = (1, 4, 16)
ATTN_STEPS = 128
POOL_WINDOWS = (2, 4, 8, 16)
POOL_GW = 256
D_FF = 4096
EPS = 1e-6
IN_SIZES = (1024, 1536, 16, 1536, 1536, 1536, 1024, 3072)
IN_WIDTH = sum(IN_SIZES)
P_XBC, P_Q, P_K, P_V, P_GATES, P_Z, P_U, P_DT = 0, 1536, 3072, 4608, 6144, 9216, 10240, 11264
P_WIDTH = 12288
LANES = 128
NEG = -1e30
VMEM_LIMIT = 56 * 1024 * 1024

ADAM_LR, ADAM_B1, ADAM_B2, ADAM_EPS, ADAM_WD, ADAM_STEP = 0.001, 0.9, 0.999, 1e-08, 0.01, 10


def _alibi_slopes(n):
    def pow2(k):
        start = 2.0 ** (-8.0 / k)
        return [start ** (i + 1) for i in range(k)]
    if math.log2(n).is_integer():
        s = pow2(n)
    else:
        c = 2 ** math.floor(math.log2(n))
        s = pow2(c) + pow2(2 * c)[0::2][: n - c]
    return np.sort(np.asarray(s, np.float32))[::-1].copy()


SLOPES = _alibi_slopes(12).reshape(3, 4)


def _cp(*sem):
    return pltpu.CompilerParams(dimension_semantics=sem, vmem_limit_bytes=VMEM_LIMIT)


_DIMS = {"nn": (((1,), (0,)), ((), ())), "nt": (((1,), (1,)), ((), ())), "tn": (((0,), (0,)), ((), ()))}


def _dot(a, b, mode):
    return lax.dot_general(a.astype(BF16), b.astype(BF16), _DIMS[mode], preferred_element_type=F32)


@functools.partial(jax.custom_vjp, nondiff_argnums=(2,))
def _bdot(a, b, mode):
    return _dot(a, b, mode)


def _bdot_fwd(a, b, mode):
    return _dot(a, b, mode), (a, b)


def _bdot_bwd(mode, res, ct):
    a, b = res
    if mode == "nn":
        return _dot(ct, b, "nt"), _dot(a, ct, "tn")
    if mode == "nt":
        return _dot(ct, b, "nn"), _dot(ct, a, "tn")
    return _dot(b, ct, "nt"), _dot(a, ct, "nn")


_bdot.defvjp(_bdot_fwd, _bdot_bwd)


def _hdot(a, b):
    return jnp.dot(a, b, precision=lax.Precision.HIGHEST, preferred_element_type=F32)


def _tri(n, lower):
    r = lax.broadcasted_iota(jnp.int32, (n, n), 0)
    c = lax.broadcasted_iota(jnp.int32, (n, n), 1)
    return (r >= c if lower else r <= c).astype(F32)


@jax.custom_vjp
def _csum(a):
    return _hdot(_tri(a.shape[0], True), a)


def _csum_fwd(a):
    return _csum(a), None


def _csum_bwd(_, ct):
    return (_hdot(_tri(ct.shape[0], False), ct),)


_csum.defvjp(_csum_fwd, _csum_bwd)


def _softplus(x):
    return jnp.maximum(x, 0.0) + jnp.log(1.0 + jnp.exp(-jnp.abs(x)))


def _sigmoid(x):
    return 1.0 / (1.0 + jnp.exp(-x))


def _silu(x):
    return x * _sigmoid(x)


def _tile(n, cap):
    t = min(n, cap)
    while n % t:
        t //= 2
    return t


MM_TILE, MM_KTILE = 1024, 2048


def _matmul(name, a, b, mode, out_dtype=F32, precise=False, layer=None, chips=0, out_chips=0, into=None):
    if mode == "nn":
        (m, k), n = a.shape, (4 * chips if chips else b.shape[-1])
    elif mode == "nt":
        (m, k), n = a.shape, b.shape[-2]
    else:
        (k, m), n = a.shape, b.shape[-1]
    tm = _tile(m // N_CHIPS if (into is not None and not out_chips) else m, MM_TILE)
    tn = _tile(chips if (chips and mode == "nn") else (out_chips or n), MM_TILE)
    tk = _tile(chips if (chips and mode == "nt") else k, MM_KTILE)
    nk = k // tk
    a_spec = pl.BlockSpec((tk, tm), lambda i, j, l: (l, i)) if mode == "tn" else pl.BlockSpec((tm, tk), lambda i, j, l: (i, l))
    if chips:
        if mode == "nn":
            per = chips // tn
            b_spec = pl.BlockSpec((None, None, tk, tn), lambda i, j, l: (layer, j // per, l, j % per))
        else:
            per = chips // tk
            b_spec = pl.BlockSpec((None, None, tn, tk), lambda i, j, l: (layer, l // per, j, l % per))
    elif layer is not None:
        b_spec = (pl.BlockSpec((None, tn, tk), lambda i, j, l: (layer, j, l)) if mode == "nt"
                  else pl.BlockSpec((None, tk, tn), lambda i, j, l: (layer, l, j)))
    else:
        b_spec = pl.BlockSpec((tn, tk), lambda i, j, l: (j, l)) if mode == "nt" else pl.BlockSpec((tk, tn), lambda i, j, l: (l, j))
    if into is not None:
        buf, slot = into
        if out_chips:
            per_o = out_chips // tn
            o_spec = pl.BlockSpec((None, None, tm, tn), lambda i, j, l: (j // per_o, slot, i, j % per_o))
        else:
            per_r = m // N_CHIPS // tm
            o_spec = pl.BlockSpec((None, None, tm, tn), lambda i, j, l: (i // per_r, slot, i % per_r, j))
        o_shape = jax.ShapeDtypeStruct(buf.shape, buf.dtype)
    elif out_chips:
        per_o = out_chips // tn
        o_spec = pl.BlockSpec((None, tm, tn), lambda i, j, l: (j // per_o, i, j % per_o))
        o_shape = jax.ShapeDtypeStruct((N_CHIPS, m, out_chips), out_dtype)
    else:
        o_spec = pl.BlockSpec((tm, tn), lambda i, j, l: (i, j))
        o_shape = jax.ShapeDtypeStruct((m, n), out_dtype)

    def part(a_ref, b_ref):
        if precise:
            return lax.dot_general(a_ref[...], b_ref[...], _DIMS[mode], precision=lax.Precision.HIGHEST,
                                   preferred_element_type=F32)
        return _dot(a_ref[...], b_ref[...], mode)

    n_in = 2 if into is None else 3

    if nk == 1:
        def body(*refs):
            o_ref = refs[n_in]
            o_ref[...] = part(refs[0], refs[1]).astype(o_ref.dtype)
        scratch = []
    else:
        def body(*refs):
            o_ref, acc_ref = refs[n_in], refs[n_in + 1]
            l = pl.program_id(2)
            p = part(refs[0], refs[1])

            @pl.when(l == 0)
            def _():
                acc_ref[...] = p

            @pl.when((l > 0) & (l < nk - 1))
            def _():
                acc_ref[...] += p

            @pl.when(l == nk - 1)
            def _():
                o_ref[...] = (acc_ref[...] + p).astype(o_ref.dtype)
        scratch = [pltpu.VMEM((tm, tn), F32)]

    extra = {} if into is None else dict(input_output_aliases={2: 0})
    return pl.pallas_call(
        body, grid=(m // tm, n // tn, nk), in_specs=[a_spec, b_spec] + ([] if into is None else [pl.BlockSpec(memory_space=pl.ANY)]),
        out_specs=o_spec, out_shape=o_shape, scratch_shapes=scratch, compiler_params=_cp("parallel", "parallel", "arbitrary"),
        name=name, **extra)(*((a, b) if into is None else (a, b, into[0])))


def _group_matmul(name, a, w, mode, out_dtype=F32, layer=0):
    s = a.shape[0]
    tb = 512
    gw = POOL_GW
    if mode == "tn":
        def body(a_ref, b_ref, o_ref):
            part = _dot(a_ref[...], b_ref[...], "tn")

            @pl.when(pl.program_id(1) == 0)
            def _():
                o_ref[0] = part

            @pl.when(pl.program_id(1) > 0)
            def _():
                o_ref[0] += part

        return pl.pallas_call(
            body, grid=(4, s // tb),
            in_specs=[pl.BlockSpec((tb, gw), lambda g, i: (i, g)), pl.BlockSpec((tb, gw), lambda g, i: (i, g))],
            out_specs=pl.BlockSpec((1, gw, gw), lambda g, i: (g, 0, 0)),
            out_shape=jax.ShapeDtypeStruct((4, gw, gw), F32),
            compiler_params=_cp("parallel", "arbitrary"), name=name)(a, w)

    def body(a_ref, w_ref, o_ref):
        o_ref[...] = _dot(a_ref[...], w_ref[...], mode).astype(o_ref.dtype)

    return pl.pallas_call(
        body, grid=(s // tb, 4),
        in_specs=[pl.BlockSpec((tb, gw), lambda i, g: (i, g)), pl.BlockSpec((None, None, gw, gw), lambda i, g: (layer, g, 0, 0))],
        out_specs=pl.BlockSpec((tb, gw), lambda i, g: (i, g)),
        out_shape=jax.ShapeDtypeStruct((s, 4 * gw), out_dtype),
        compiler_params=_cp("parallel", "parallel"), name=name)(a, w)


def _rspec(tb, width, cb):
    return pl.BlockSpec((tb, width), lambda i: (i, cb))


def _pspec(shape):
    return pl.BlockSpec(shape, lambda i: (0, 0))


def _rowwise_fwd(name, f, rows, pars, outs, tb=256):
    s = rows[0][0].shape[0]
    nin = len(rows) + len(pars)

    def body(*refs):
        res = f(*[r[...].astype(F32) for r in refs[:nin]])
        for o, v in zip(refs[nin:], res):
            o[...] = v.astype(o.dtype)

    return pl.pallas_call(
        body, grid=(s // tb,),
        in_specs=[_rspec(tb, w, cb) for _, w, cb in rows] + [_pspec(p.shape) for p in pars],
        out_specs=[_rspec(tb, w, 0) for w, _ in outs],
        out_shape=[jax.ShapeDtypeStruct((s, w), dt) for w, dt in outs],
        compiler_params=_cp("parallel"), name=name)(*[r[0] for r in rows], *pars)


def _rowwise_bwd(name, f, rows, pars, cts, need, add=None, tb=256, gdt=None):
    s = rows[0][0].shape[0]
    nr, npar, nc = len(rows), len(pars), len(cts)
    nin = nr + npar + nc + (1 if add is not None else 0)

    def body(*refs):
        ins = [r[...].astype(F32) for r in refs[:nr + npar]]
        _, vjp = jax.vjp(f, *ins)
        g = vjp(tuple(c[...].astype(F32) for c in refs[nr + npar:nr + npar + nc]))
        outs = refs[nin:]
        k = 0
        for j in range(nr):
            if need[j]:
                v = g[j]
                if add is not None and add[0] == j:
                    v = v + refs[nin - 1][...]
                outs[k][...] = v.astype(outs[k].dtype)
                k += 1
        first = pl.program_id(0) == 0
        for j in range(npar):
            o, v = outs[k + j], g[nr + j]

            @pl.when(first)
            def _(o=o, v=v):
                o[...] = v

            @pl.when(jnp.logical_not(first))
            def _(o=o, v=v):
                o[...] += v

    in_specs = ([_rspec(tb, w, cb) for _, w, cb in rows] + [_pspec(p.shape) for p in pars]
                + [_rspec(tb, w, cb) for _, w, cb in cts])
    args = [r[0] for r in rows] + list(pars) + [c[0] for c in cts]
    if add is not None:
        in_specs.append(_rspec(tb, rows[add[0]][1], 0))
        args.append(add[1])
    gr = [(w, F32) for (_, w, _), nd in zip(rows, need) if nd]
    if gdt is not None:
        gr = [(w, dt) for (w, _), dt in zip(gr, gdt)]
    return pl.pallas_call(
        body, grid=(s // tb,), in_specs=in_specs,
        out_specs=[_rspec(tb, w, 0) for w, _ in gr] + [_pspec(p.shape) for p in pars],
        out_shape=[jax.ShapeDtypeStruct((s, w), dt) for w, dt in gr] + [jax.ShapeDtypeStruct(p.shape, F32) for p in pars],
        compiler_params=_cp("arbitrary"), name=name)(*args)


def _f_norm(x, nw, sc, sh):
    r = lax.rsqrt(jnp.mean(x * x, axis=-1, keepdims=True) + EPS)
    return ((x * r * nw) * (1.0 + sc) + sh,)


def _f_ssdgate(y, z, w):
    y2 = y * _silu(z)
    low = lax.broadcasted_iota(jnp.int32, y2.shape, 1) < 512
    sq = y2 * y2
    m0 = jnp.sum(jnp.where(low, sq, 0.0), axis=-1, keepdims=True) / 512.0
    m1 = jnp.sum(jnp.where(low, 0.0, sq), axis=-1, keepdims=True) / 512.0
    r = jnp.where(low, lax.rsqrt(m0 + EPS), lax.rsqrt(m1 + EPS))
    return (y2 * r * w,)


def _head_rms(t, w):
    outs = []
    for h in range(t.shape[1] // ATTN_HEAD_DIM):
        th = t[:, h * ATTN_HEAD_DIM:(h + 1) * ATTN_HEAD_DIM]
        outs.append(th * lax.rsqrt(jnp.mean(th * th, axis=-1, keepdims=True) + EPS) * w)
    return jnp.concatenate(outs, axis=1)


def _f_qknorm(q, k, v, qw, kw):
    return _head_rms(q, qw), _head_rms(k, kw), v


def _f_combine(o1, o2, o3, l1, l2, l3):
    m = lax.stop_gradient(jnp.maximum(jnp.maximum(l1, l2), l3))
    e1, e2, e3 = jnp.exp(l1 - m), jnp.exp(l2 - m), jnp.exp(l3 - m)
    return ((e1 * o1 + e2 * o2 + e3 * o3) / (e1 + e2 + e3),)


def _f_poolscale(pm, ps):
    return (pm * ps,)


def _f_merge(gates, ys, ya, yp):
    g = _sigmoid(gates)
    return (g[:, 0:1024] * ys + g[:, 1024:2048] * ya + g[:, 2048:3072] * yp,)


def _f_resid(x, o, g):
    return (x + g * o,)


def _f_relu2(a):
    return (jnp.square(jnp.maximum(a, 0.0)),)


def _loss_and_grad(y, tgt, tb=512):
    s, d = y.shape

    def body(y_ref, t_ref, dy_ref, l_ref):
        e = y_ref[...] - t_ref[...]
        dy_ref[...] = e * (1.0 / d)
        part = jnp.zeros((1, LANES), F32) + jnp.sum(e * e) * (0.5 / d)

        @pl.when(pl.program_id(0) == 0)
        def _():
            l_ref[...] = part

        @pl.when(pl.program_id(0) > 0)
        def _():
            l_ref[...] += part

    return pl.pallas_call(
        body, grid=(s // tb,), in_specs=[_rspec(tb, d, 0), _rspec(tb, d, 0)],
        out_specs=[_rspec(tb, d, 0), _pspec((1, LANES))],
        out_shape=[jax.ShapeDtypeStruct((s, d), F32), jax.ShapeDtypeStruct((1, LANES), F32)],
        compiler_params=_cp("arbitrary"), name="loss")(y, tgt)


def _shift_down(x, j):
    rows = lax.broadcasted_iota(jnp.int32, x.shape, 0)
    return jnp.where(rows < j, 0.0, pltpu.roll(x, j, 0))


def _shift_up(x, j):
    s = x.shape[0]
    rows = lax.broadcasted_iota(jnp.int32, x.shape, 0)
    return jnp.where(rows >= s - j, 0.0, pltpu.roll(x, s - j, 0))


CONV_CB = 256


def _conv_pre(x, w_ref, b_ref):
    acc = b_ref[...] + w_ref[SSD_CONV - 1:SSD_CONV, :] * x
    for j in range(1, SSD_CONV):
        acc = acc + w_ref[SSD_CONV - 1 - j:SSD_CONV - j, :] * _shift_down(x, j)
    return acc


def _conv_fwd(proj, cw, cb):
    s = proj.shape[0]

    def body(x_ref, w_ref, b_ref, o_ref):
        o_ref[...] = _silu(_conv_pre(x_ref[...], w_ref, b_ref))

    return pl.pallas_call(
        body, grid=(CONV_DIM // CONV_CB,),
        in_specs=[pl.BlockSpec((s, CONV_CB), lambda i: (0, P_XBC // CONV_CB + i)),
                  pl.BlockSpec((SSD_CONV, CONV_CB), lambda i: (0, i)), pl.BlockSpec((1, CONV_CB), lambda i: (0, i))],
        out_specs=pl.BlockSpec((s, CONV_CB), lambda i: (0, i)),
        out_shape=jax.ShapeDtypeStruct((s, CONV_DIM), F32), compiler_params=_cp("parallel"), name="conv_fwd")(proj, cw, cb)


def _conv_bwd(proj, cw, cb, dout):
    s = proj.shape[0]

    def body(x_ref, w_ref, b_ref, d_ref, dx_ref, dw_ref, db_ref):
        x = x_ref[...]
        a = _conv_pre(x, w_ref, b_ref)
        sg = _sigmoid(a)
        da = d_ref[...] * (sg + a * sg * (1.0 - sg))
        db_ref[...] = jnp.sum(da, axis=0, keepdims=True)
        dx = w_ref[SSD_CONV - 1:SSD_CONV, :] * da
        dw_ref[SSD_CONV - 1:SSD_CONV, :] = jnp.sum(da * x, axis=0, keepdims=True)
        for j in range(1, SSD_CONV):
            dx = dx + w_ref[SSD_CONV - 1 - j:SSD_CONV - j, :] * _shift_up(da, j)
            dw_ref[SSD_CONV - 1 - j:SSD_CONV - j, :] = jnp.sum(da * _shift_down(x, j), axis=0, keepdims=True)
        dx_ref[...] = dx.astype(dx_ref.dtype)

    return pl.pallas_call(
        body, grid=(CONV_DIM // CONV_CB,),
        in_specs=[pl.BlockSpec((s, CONV_CB), lambda i: (0, P_XBC // CONV_CB + i)),
                  pl.BlockSpec((SSD_CONV, CONV_CB), lambda i: (0, i)), pl.BlockSpec((1, CONV_CB), lambda i: (0, i)),
                  pl.BlockSpec((s, CONV_CB), lambda i: (0, i))],
        out_specs=[pl.BlockSpec((s, CONV_CB), lambda i: (0, i)), pl.BlockSpec((SSD_CONV, CONV_CB), lambda i: (0, i)),
                   pl.BlockSpec((1, CONV_CB), lambda i: (0, i))],
        out_shape=[jax.ShapeDtypeStruct((s, CONV_DIM), BF16), jax.ShapeDtypeStruct((SSD_CONV, CONV_DIM), F32),
                   jax.ShapeDtypeStruct((1, CONV_DIM), F32)],
        compiler_params=_cp("parallel"), name="conv_bwd")(proj, cw, cb, dout)


def _pool_window_sum(x, g, shift):
    s2 = x + shift(x, 1)
    s4 = s2 + shift(s2, 2)
    s8 = s4 + shift(s4, 4)
    s16 = s8 + shift(s8, 8)
    return jnp.where(g == 0, s2, jnp.where(g == 1, s4, jnp.where(g == 2, s8, s16)))


def _pool_count(shape, g):
    rows = lax.broadcasted_iota(jnp.int32, shape, 0)
    return jnp.minimum(rows + 1, jnp.left_shift(2, g)).astype(F32)


def _pool_fwd(proj):
    s = proj.shape[0]

    def body(u_ref, o_ref):
        g = pl.program_id(0)
        u = u_ref[...]
        o_ref[...] = (_pool_window_sum(u, g, _shift_down) / _pool_count(u.shape, g) - u).astype(o_ref.dtype)

    return pl.pallas_call(
        body, grid=(4,), in_specs=[pl.BlockSpec((s, POOL_GW), lambda g: (0, P_U // POOL_GW + g))],
        out_specs=pl.BlockSpec((s, POOL_GW), lambda g: (0, g)),
        out_shape=jax.ShapeDtypeStruct((s, 4 * POOL_GW), BF16), compiler_params=_cp("parallel"), name="pool_fwd")(proj)


def _pool_bwd(dp):
    s = dp.shape[0]

    def body(d_ref, o_ref):
        g = pl.program_id(0)
        d = d_ref[...]
        o_ref[...] = (_pool_window_sum(d / _pool_count(d.shape, g), g, _shift_up) - d).astype(o_ref.dtype)

    return pl.pallas_call(
        body, grid=(4,), in_specs=[pl.BlockSpec((s, POOL_GW), lambda g: (0, g))],
        out_specs=pl.BlockSpec((s, POOL_GW), lambda g: (0, g)),
        out_shape=jax.ShapeDtypeStruct((s, 4 * POOL_GW), BF16), compiler_params=_cp("parallel"), name="pool_bwd")(dp)


N_PAIRS = SSD_HEADS // 2
STATE_ROWS = N_PAIRS * SSD_STATE


def _ssd_chunk(xbc, dtr, hprev, dtb, alog, dsk):
    L = xbc.shape[0]
    xs, bm, cm = xbc[:, 0:1024], xbc[:, 1024:1280], xbc[:, 1280:1536]
    dt = _softplus(dtr + dtb)
    a = dt * (-jnp.exp(alog))
    acum = _csum(a)
    alast = jnp.sum(a, axis=0, keepdims=True)
    xdt = xs * dt
    xdecay = xdt * jnp.exp(alast - acum)
    eacum = jnp.exp(acum)
    elast = jnp.exp(alast)
    cb = [_bdot(cm[:, g * 128:(g + 1) * 128], bm[:, g * 128:(g + 1) * 128], "nt") for g in range(2)]
    rows = lax.broadcasted_iota(jnp.int32, (L, L), 0)
    cols = lax.broadcasted_iota(jnp.int32, (L, L), 1)
    causal = rows >= cols
    lane = lax.broadcasted_iota(jnp.int32, (L, LANES), 1)
    sub = lax.broadcasted_iota(jnp.int32, (LANES, L), 0)
    ys, hs = [], []
    for p in range(N_PAIRS):
        g = p // (N_PAIRS // 2)
        sl = slice(p * LANES, (p + 1) * LANES)
        ac = acum[:, sl]
        act = ac.T
        xp = xdt[:, sl]
        hp = hprev[p * SSD_STATE:(p + 1) * SSD_STATE, :]
        y = _bdot(cm[:, g * 128:(g + 1) * 128], hp, "nn") * eacum[:, sl] + dsk[:, sl] * xs[:, sl]
        for half in range(2):
            l0 = half * SSD_HEAD_DIM
            col = jnp.sum(jnp.where(lane == l0, ac, 0.0), axis=1, keepdims=True)
            row = jnp.sum(jnp.where(sub == l0, act, 0.0), axis=0, keepdims=True)
            decay = jnp.exp(jnp.where(causal, col - row, NEG))
            xh = jnp.where((lane >= l0) & (lane < l0 + SSD_HEAD_DIM), xp, 0.0)
            y = y + _bdot(cb[g] * decay, xh, "nn")
        ys.append(y)
        hs.append(elast[:, sl] * hp + _bdot(bm[:, g * 128:(g + 1) * 128], xdecay[:, sl], "tn"))
    return tuple(ys), tuple(hs)


def _ssd_fwd(xbc, proj, dtb, alog, dsk):
    s = xbc.shape[0]
    nc = s // SSD_CHUNK

    def body(x_ref, dt_ref, b_ref, a_ref, d_ref, y_ref, hist_ref, h_ref):
        @pl.when(pl.program_id(0) == 0)
        def _():
            h_ref[...] = jnp.zeros_like(h_ref)

        hprev = h_ref[...]
        hist_ref[...] = hprev
        ys, hs = _ssd_chunk(x_ref[...], dt_ref[...], hprev, b_ref[...], a_ref[...], d_ref[...])
        for p in range(N_PAIRS):
            y_ref[:, p * LANES:(p + 1) * LANES] = ys[p]
            h_ref[p * SSD_STATE:(p + 1) * SSD_STATE, :] = hs[p]

    return pl.pallas_call(
        body, grid=(nc,),
        in_specs=[pl.BlockSpec((SSD_CHUNK, CONV_DIM), lambda i: (i, 0)),
                  pl.BlockSpec((SSD_CHUNK, 1024), lambda i: (i, P_DT // 1024)),
                  _pspec((1, 1024)), _pspec((1, 1024)), _pspec((1, 1024))],
        out_specs=[pl.BlockSpec((SSD_CHUNK, 1024), lambda i: (i, 0)), pl.BlockSpec((STATE_ROWS, LANES), lambda i: (i, 0))],
        out_shape=[jax.ShapeDtypeStruct((s, 1024), F32), jax.ShapeDtypeStruct((nc * STATE_ROWS, LANES), F32)],
        scratch_shapes=[pltpu.VMEM((STATE_ROWS, LANES), F32)],
        compiler_params=_cp("arbitrary"), name="ssd_fwd")(xbc, proj, dtb, alog, dsk)


def _ssd_bwd(xbc, proj, hist, dtb, alog, dsk, dy):
    s = xbc.shape[0]
    nc = s // SSD_CHUNK

    def body(x_ref, dt_ref, hist_ref, b_ref, a_ref, d_ref, dy_ref, dx_ref, ddt_ref, db_ref, da_ref, dd_ref, dh_ref):
        first = pl.program_id(0) == 0

        @pl.when(first)
        def _():
            dh_ref[...] = jnp.zeros_like(dh_ref)

        _, vjp = jax.vjp(_ssd_chunk, x_ref[...], dt_ref[...], hist_ref[...], b_ref[...], a_ref[...], d_ref[...])
        dys = tuple(dy_ref[:, p * LANES:(p + 1) * LANES] for p in range(N_PAIRS))
        dhs = tuple(dh_ref[p * SSD_STATE:(p + 1) * SSD_STATE, :] for p in range(N_PAIRS))
        dx, ddt, dhp, db, da, dd = vjp((dys, dhs))
        dx_ref[...] = dx
        ddt_ref[...] = ddt.astype(ddt_ref.dtype)
        dh_ref[...] = dhp
        for o, v in ((db_ref, db), (da_ref, da), (dd_ref, dd)):
            @pl.when(first)
            def _(o=o, v=v):
                o[...] = v

            @pl.when(jnp.logical_not(first))
            def _(o=o, v=v):
                o[...] += v

    rev = lambda i: (nc - 1 - i, 0)
    return pl.pallas_call(
        body, grid=(nc,),
        in_specs=[pl.BlockSpec((SSD_CHUNK, CONV_DIM), rev),
                  pl.BlockSpec((SSD_CHUNK, 1024), lambda i: (nc - 1 - i, P_DT // 1024)),
                  pl.BlockSpec((STATE_ROWS, LANES), rev),
                  _pspec((1, 1024)), _pspec((1, 1024)), _pspec((1, 1024)),
                  pl.BlockSpec((SSD_CHUNK, 1024), rev)],
        out_specs=[pl.BlockSpec((SSD_CHUNK, CONV_DIM), rev), pl.BlockSpec((SSD_CHUNK, 1024), rev),
                   _pspec((1, 1024)), _pspec((1, 1024)), _pspec((1, 1024))],
        out_shape=[jax.ShapeDtypeStruct((s, CONV_DIM), F32), jax.ShapeDtypeStruct((s, 1024), BF16)]
        + [jax.ShapeDtypeStruct((1, 1024), F32)] * 3,
        scratch_shapes=[pltpu.VMEM((STATE_ROWS, LANES), F32)],
        compiler_params=_cp("arbitrary"), name="ssd_bwd")(xbc, proj, hist, dtb, alog, dsk, dy)


def _attn_head(q, kp, kc, vp, vc, has_prev, slope):
    scale = ATTN_HEAD_DIM ** -0.5
    n = ATTN_STEPS
    qi = lax.broadcasted_iota(jnp.int32, (n, n), 0)
    kj = lax.broadcasted_iota(jnp.int32, (n, n), 1)
    sp = jnp.where((kj >= qi) & has_prev, _bdot(q, kp, "nt") * scale - slope * (qi + n - kj).astype(F32), NEG)
    sc = jnp.where(kj <= qi, _bdot(q, kc, "nt") * scale - slope * (qi - kj).astype(F32), NEG)
    m = lax.stop_gradient(jnp.maximum(jnp.max(sp, axis=1, keepdims=True), jnp.max(sc, axis=1, keepdims=True)))
    pp, pc = jnp.exp(sp - m), jnp.exp(sc - m)
    den = jnp.sum(pp, axis=1, keepdims=True) + jnp.sum(pc, axis=1, keepdims=True)
    o = (_bdot(pp, vp, "nn") + _bdot(pc, vc, "nn")) / den
    return o, jnp.broadcast_to(m + jnp.log(den), (n, ATTN_HEAD_DIM))


def _head_slope(gi, h):
    s = [float(v) * DILATIONS[gi] for v in SLOPES[gi]]
    return jnp.where(h == 0, s[0], jnp.where(h == 1, s[1], jnp.where(h == 2, s[2], s[3])))


ATTN_UNROLL = 4


def _attn_heads_per_block(d):
    return 4 if d == 1 else 1


def _units(ref, d, hb):
    if d == 1:
        return [ref[:, h * ATTN_HEAD_DIM:(h + 1) * ATTN_HEAD_DIM] for h in range(hb)]
    return [ref[pl.ds(r, ATTN_STEPS, stride=d), :] for r in range(d)]


def _store_units(ref, src, d, hb):
    if d == 1:
        for h in range(hb):
            ref[:, h * ATTN_HEAD_DIM:(h + 1) * ATTN_HEAD_DIM] = src[h]
    else:
        for r in range(d):
            ref[pl.ds(r, ATTN_STEPS, stride=d), :] = src[r]


def _attn_fwd(qn, kn, vv, gi):
    d = DILATIONS[gi]
    s = qn.shape[0]
    span = ATTN_STEPS * d
    nb = s // span

    hb, units = _attn_heads_per_block(d), _attn_heads_per_block(d) * d

    def body(q_ref, k_ref, v_ref, o_ref, l_ref, sq, sk, sv, so, sl):
        h0, b = pl.program_id(0) * hb, pl.program_id(1)
        cur, prev = b % 2, (b + 1) % 2

        @pl.when(b == 0)
        def _():
            sk[prev] = jnp.zeros(sk.shape[1:], F32)
            sv[prev] = jnp.zeros(sv.shape[1:], F32)

        for u, (qr, kr, vr) in enumerate(zip(_units(q_ref, d, hb), _units(k_ref, d, hb), _units(v_ref, d, hb))):
            sq[u] = qr
            sk[cur, u] = kr
            sv[cur, u] = vr

        def step(i, carry):
            for j in range(ATTN_UNROLL):
                u = i * ATTN_UNROLL + j
                so[u], sl[u] = _attn_head(sq[u], sk[prev, u], sk[cur, u], sv[prev, u], sv[cur, u], b > 0,
                                          _head_slope(gi, h0 + u // d))
            return carry

        lax.fori_loop(0, units // ATTN_UNROLL, step, 0)
        _store_units(o_ref, so, d, hb)
        _store_units(l_ref, sl, d, hb)

    blk = pl.BlockSpec((span, hb * ATTN_HEAD_DIM), lambda h, b: (b, (gi * 4) // hb + h))
    out = pl.BlockSpec((span, hb * ATTN_HEAD_DIM), lambda h, b: (b, h))
    res = (units, ATTN_STEPS, ATTN_HEAD_DIM)
    return pl.pallas_call(
        body, grid=(4 // hb, nb), in_specs=[blk, blk, blk], out_specs=[out, out],
        out_shape=[jax.ShapeDtypeStruct((s, ATTN_GROUP_W), F32)] * 2,
        scratch_shapes=[pltpu.VMEM(res, F32), pltpu.VMEM((2,) + res, F32), pltpu.VMEM((2,) + res, F32),
                        pltpu.VMEM(res, F32), pltpu.VMEM(res, F32)],
        compiler_params=_cp("parallel", "arbitrary"), name=f"attn_fwd_g{gi}")(qn, kn, vv)


def _attn_bwd(qn, kn, vv, do, dl, gi):
    d = DILATIONS[gi]
    s = qn.shape[0]
    span = ATTN_STEPS * d
    nb = s // span

    hb, units = _attn_heads_per_block(d), _attn_heads_per_block(d) * d

    def body(q_ref, kp_ref, kc_ref, vp_ref, vc_ref, do_ref, dl_ref, dq_ref, dk_ref, dv_ref, sin, sout, ck, cv):
        h0, bi = pl.program_id(0) * hb, pl.program_id(1)

        @pl.when(bi == 0)
        def _():
            ck[...] = jnp.zeros_like(ck)
            cv[...] = jnp.zeros_like(cv)

        for i, ref in enumerate((q_ref, kp_ref, kc_ref, vp_ref, vc_ref, do_ref, dl_ref)):
            for u, val in enumerate(_units(ref, d, hb)):
                sin[i, u] = val
        has_prev = bi < nb - 1

        def step(i, carry):
            for j in range(ATTN_UNROLL):
                u = i * ATTN_UNROLL + j
                f = functools.partial(_attn_head, has_prev=has_prev, slope=_head_slope(gi, h0 + u // d))
                _, vjp = jax.vjp(f, sin[0, u], sin[1, u], sin[2, u], sin[3, u], sin[4, u])
                dq, dkp, dkc, dvp, dvc = vjp((sin[5, u], sin[6, u]))
                sout[0, u] = dq
                sout[1, u] = dkc + ck[u]
                sout[2, u] = dvc + cv[u]
                ck[u] = dkp
                cv[u] = dvp
            return carry

        lax.fori_loop(0, units // ATTN_UNROLL, step, 0)
        for i, ref in enumerate((dq_ref, dk_ref, dv_ref)):
            _store_units(ref, sout.at[i], d, hb)

    w = hb * ATTN_HEAD_DIM
    cur = pl.BlockSpec((span, w), lambda h, b: (nb - 1 - b, (gi * 4) // hb + h))
    prev = pl.BlockSpec((span, w), lambda h, b: (jnp.maximum(nb - 2 - b, 0), (gi * 4) // hb + h))
    out = pl.BlockSpec((span, w), lambda h, b: (nb - 1 - b, h))
    res = (units, ATTN_STEPS, ATTN_HEAD_DIM)
    return pl.pallas_call(
        body, grid=(4 // hb, nb), in_specs=[cur, prev, cur, prev, cur, out, out], out_specs=[out, out, out],
        out_shape=[jax.ShapeDtypeStruct((s, ATTN_GROUP_W), F32)] * 3,
        scratch_shapes=[pltpu.VMEM((7,) + res, F32), pltpu.VMEM((3,) + res, F32), pltpu.VMEM(res, F32), pltpu.VMEM(res, F32)],
        compiler_params=_cp("parallel", "arbitrary"), name=f"attn_bwd_g{gi}")(qn, kn, kn, vv, vv, do, dl)


def _layer_fwd(x, mod, W, l):
    sh1, sc1, g1, sh2, sc2, g2 = (mod[i:i + 1] for i in range(6))
    (h,) = _rowwise_fwd("norm1", _f_norm, [(x, 1024, 0)], [W["norm1_w"], sc1, sh1], [(1024, BF16)])
    proj = _matmul("in_proj", h, W["w_in"], "nn")
    xbc = _conv_fwd(proj, W["conv_w"], W["conv_b"])
    y, hist = _ssd_fwd(xbc, proj, W["dt_bias"], W["a_log"], W["d_skip"])
    (yn,) = _rowwise_fwd("ssd_gate", _f_ssdgate, [(y, 1024, 0), (proj, 1024, P_Z // 1024)], [W["ssd_norm_w"]], [(1024, BF16)])
    y_ssd = _matmul("ssd_out", yn, W["w_ssd_out"], "nn", layer=l)
    qn, kn, vv = _rowwise_fwd("qk_norm", _f_qknorm, [(proj, 1536, P_Q // 1536), (proj, 1536, P_K // 1536), (proj, 1536, P_V // 1536)],
                              [W["q_norm_w"], W["k_norm_w"]], [(1536, F32)] * 3)
    ol = [_attn_fwd(qn, kn, vv, gi) for gi in range(3)]
    (o,) = _rowwise_fwd("attn_combine", _f_combine, [(t[0], 512, 0) for t in ol] + [(t[1], 512, 0) for t in ol], [], [(512, BF16)])
    y_attn = _matmul("attn_out", o, W["w_attn_out"], "nn", layer=l, chips=256)
    pooled = _pool_fwd(proj)
    pm = _group_matmul("pool_mix", pooled, W["w_pool_mix"], "nn", layer=l)
    (ps,) = _rowwise_fwd("pool_scale", _f_poolscale, [(pm, 1024, 0)], [W["pool_scale"]], [(1024, BF16)])
    y_pool = _matmul("pool_out", ps, W["w_pool_out"], "nn", layer=l)
    (merged,) = _rowwise_fwd("merge", _f_merge, [(proj, 3072, P_GATES // 3072), (y_ssd, 1024, 0), (y_attn, 1024, 0), (y_pool, 1024, 0)],
                             [], [(1024, BF16)])
    mo = _matmul("mix_out", merged, W["w_out"], "nn", layer=l)
    (x1,) = _rowwise_fwd("resid1", _f_resid, [(x, 1024, 0), (mo, 1024, 0)], [g1], [(1024, F32)])
    (h2,) = _rowwise_fwd("norm2", _f_norm, [(x1, 1024, 0)], [W["norm2_w"], sc2, sh2], [(1024, BF16)])
    a = _matmul("ff1", h2, W["w_ff1"], "nn", layer=l, chips=1024)
    (r,) = _rowwise_fwd("relu2", _f_relu2, [(a, D_FF, 0)], [], [(D_FF, BF16)], tb=128)
    ff = _matmul("ff2", r, W["w_ff2"], "nn", layer=l)
    (x2,) = _rowwise_fwd("resid2", _f_resid, [(x1, 1024, 0), (ff, 1024, 0)], [g2], [(1024, F32)])
    saved = dict(x=x, h=h, proj=proj, xbc=xbc, y=y, hist=hist, yn=yn, y_ssd=y_ssd, qn=qn, kn=kn, vv=vv, ol=ol, o=o,
                 y_attn=y_attn, pooled=pooled, pm=pm, ps=ps, y_pool=y_pool, merged=merged, mo=mo, x1=x1, h2=h2, a=a, r=r, ff=ff)
    return x2, saved


def _layer_bwd(dx2, mod, W, sv, l, bufs):
    sh1, sc1, g1, sh2, sc2, g2 = (mod[i:i + 1] for i in range(6))
    g = {}
    dx1a, dff, dg2 = _rowwise_bwd("resid2_bwd", _f_resid, [(sv["x1"], 1024, 0), (sv["ff"], 1024, 0)], [g2], [(dx2, 1024, 0)],
                                  [True, True], gdt=[F32, BF16])
    g["w_ff2"] = _matmul("ff2_dw", sv["r"], dff, "tn", BF16, into=(bufs["w_ff2"], l))
    dr = _matmul("ff2_dx", dff, W["w_ff2"], "nt", layer=l)
    (da,) = _rowwise_bwd("relu2_bwd", _f_relu2, [(sv["a"], D_FF, 0)], [], [(dr, D_FF, 0)], [True], tb=128, gdt=[BF16])
    g["w_ff1"] = _matmul("ff1_dw", sv["h2"], da, "tn", BF16, out_chips=1024, into=(bufs["w_ff1"], l))
    dh2 = _matmul("ff1_dx", da, W["w_ff1"], "nt", layer=l, chips=1024)
    dx1, g["norm2_w"], dsc2, dsh2 = _rowwise_bwd("norm2_bwd", _f_norm, [(sv["x1"], 1024, 0)], [W["norm2_w"], sc2, sh2],
                                                 [(dh2, 1024, 0)], [True], add=(0, dx1a))
    dxa, dmo, dg1 = _rowwise_bwd("resid1_bwd", _f_resid, [(sv["x"], 1024, 0), (sv["mo"], 1024, 0)], [g1], [(dx1, 1024, 0)],
                                 [True, True], gdt=[F32, BF16])
    g["w_out"] = _matmul("mix_out_dw", sv["merged"], dmo, "tn", BF16, into=(bufs["w_out"], l))
    dmerged = _matmul("mix_out_dx", dmo, W["w_out"], "nt", layer=l)
    proj = sv["proj"]
    dgates, dy_ssd, dy_attn, dy_pool = _rowwise_bwd(
        "merge_bwd", _f_merge, [(proj, 3072, P_GATES // 3072), (sv["y_ssd"], 1024, 0), (sv["y_attn"], 1024, 0), (sv["y_pool"], 1024, 0)],
        [], [(dmerged, 1024, 0)], [True] * 4, gdt=[BF16] * 4)
    g["w_pool_out"] = _matmul("pool_out_dw", sv["ps"], dy_pool, "tn", BF16, into=(bufs["w_pool_out"], l))
    dps = _matmul("pool_out_dx", dy_pool, W["w_pool_out"], "nt", layer=l)
    dpm, g["pool_scale"] = _rowwise_bwd("pool_scale_bwd", _f_poolscale, [(sv["pm"], 1024, 0)], [W["pool_scale"]], [(dps, 1024, 0)],
                                        [True], gdt=[BF16])
    dmix = _group_matmul("pool_mix_dw", sv["pooled"], dpm, "tn")
    g["w_pool_mix"] = bufs["w_pool_mix"].at[:, l].set(
        jnp.moveaxis(dmix.reshape(4, N_CHIPS, POOL_GW // N_CHIPS, POOL_GW), 1, 0).astype(BF16))
    dpooled = _group_matmul("pool_mix_dx", dpm, W["w_pool_mix"], "nt", layer=l)
    du = _pool_bwd(dpooled)
    g["w_attn_out"] = _matmul("attn_out_dw", sv["o"], dy_attn, "tn", BF16, out_chips=256, into=(bufs["w_attn_out"], l))
    do = _matmul("attn_out_dx", dy_attn, W["w_attn_out"], "nt", layer=l, chips=256)
    ol = sv["ol"]
    dol = _rowwise_bwd("attn_combine_bwd", _f_combine, [(t[0], 512, 0) for t in ol] + [(t[1], 512, 0) for t in ol], [],
                       [(do, 512, 0)], [True] * 6)
    dqs, dks, dvs = zip(*[_attn_bwd(sv["qn"], sv["kn"], sv["vv"], dol[gi], dol[3 + gi], gi) for gi in range(3)])
    dqn, dkn, dvv = (jnp.concatenate(t, axis=1) for t in (dqs, dks, dvs))
    dq, dk, dv, g["q_norm_w"], g["k_norm_w"] = _rowwise_bwd(
        "qk_norm_bwd", _f_qknorm, [(proj, 1536, P_Q // 1536), (proj, 1536, P_K // 1536), (proj, 1536, P_V // 1536)],
        [W["q_norm_w"], W["k_norm_w"]], [(dqn, 1536, 0), (dkn, 1536, 0), (dvv, 1536, 0)], [True] * 3, gdt=[BF16] * 3)
    g["w_ssd_out"] = _matmul("ssd_out_dw", sv["yn"], dy_ssd, "tn", BF16, into=(bufs["w_ssd_out"], l))
    dyn = _matmul("ssd_out_dx", dy_ssd, W["w_ssd_out"], "nt", layer=l)
    dy, dz, g["ssd_norm_w"] = _rowwise_bwd("ssd_gate_bwd", _f_ssdgate, [(sv["y"], 1024, 0), (proj, 1024, P_Z // 1024)], [W["ssd_norm_w"]],
                                           [(dyn, 1024, 0)], [True, True], gdt=[F32, BF16])
    dxbc, ddt, g["dt_bias"], g["a_log"], g["d_skip"] = _ssd_bwd(sv["xbc"], proj, sv["hist"], W["dt_bias"], W["a_log"], W["d_skip"], dy)
    dxbc_raw, g["conv_w"], g["conv_b"] = _conv_bwd(proj, W["conv_w"], W["conv_b"], dxbc)
    dproj = jnp.concatenate([dxbc_raw, dq, dk, dv, dgates, dz, du, ddt], axis=1)
    g["w_in"] = _matmul("in_proj_dw", sv["h"], dproj, "tn", BF16)
    dh = _matmul("in_proj_dx", dproj, W["w_in"], "nt")
    dx, g["norm1_w"], dsc1, dsh1 = _rowwise_bwd("norm1_bwd", _f_norm, [(sv["x"], 1024, 0)], [W["norm1_w"], sc1, sh1],
                                                [(dh, 1024, 0)], [True], add=(0, dxa))
    dmod = jnp.concatenate([dsh1, dsc1, dg1, dsh2, dsc2, dg2], axis=0)
    return dx, dmod, g


def _expand_heads(t):
    return jnp.repeat(t, SSD_HEAD_DIM, axis=-1)


def _reduce_heads(t):
    return t.reshape(t.shape[:-1] + (SSD_HEADS, SSD_HEAD_DIM)).sum(-1)


_IN_SPLITS = np.cumsum((0,) + IN_SIZES)


def _w_in_to_layout(w):
    z, xbc, dt, q, k, v, u, gates = (w[:, _IN_SPLITS[i]:_IN_SPLITS[i + 1]] for i in range(8))
    return jnp.concatenate([xbc, q, k, v, gates, z, u, _expand_heads(dt)], axis=1)


def _w_in_from_layout(g):
    xbc, q, k, v = (g[:, o:o + 1536] for o in (P_XBC, P_Q, P_K, P_V))
    gates, z, u, dt = g[:, P_GATES:P_GATES + 3072], g[:, P_Z:P_Z + 1024], g[:, P_U:P_U + 1024], g[:, P_DT:P_DT + 1024]
    return jnp.concatenate([z, xbc, _reduce_heads(dt.astype(F32)).astype(g.dtype), q, k, v, u, gates], axis=1)


_STACKED = ("w_ssd_out", "w_attn_out", "w_pool_mix", "w_pool_out", "w_out", "w_ff1", "w_ff2")
_ROWS = ("norm1_w", "norm2_w", "conv_b", "ssd_norm_w", "q_norm_w", "k_norm_w", "pool_scale")
_HEAD_ROWS = ("dt_bias", "a_log", "d_skip")


def _layer_weights(wg, lg, small, l):
    W = {k: wg[k] for k in _STACKED}
    w_in = jnp.concatenate([wg["w_in"][lg, j] for j in range(N_CHIPS)], axis=1)
    W["w_in"] = _w_in_to_layout(w_in)
    W["conv_w"] = small["conv_w"][l]
    for k in _ROWS:
        W[k] = small[k][l][None, :]
    for k in _HEAD_ROWS:
        W[k] = _expand_heads(small[k][l])[None, :]
    return W


def _layer_grads_by_chip(g, w_in_buf, l):
    out = dict(g)
    w_in = _w_in_from_layout(g["w_in"])
    out["w_in"] = w_in_buf.at[:, l].set(jnp.moveaxis(w_in.reshape(D_MODEL, N_CHIPS, IN_WIDTH // N_CHIPS), 1, 0))
    for k in _ROWS:
        out[k] = g[k][0]
    for k in _HEAD_ROWS:
        out[k] = _reduce_heads(g[k][0])
    return out


ANY = pl.BlockSpec(memory_space=pl.ANY)


def _place():
    x, y, c = lax.axis_index("x"), lax.axis_index("y"), lax.axis_index("c")
    return x, y, c, (x, y, 1 - c), [(1 - x, y), (x, 1 - y), (1 - x, 1 - y)]


def _allgather8(name, blk):
    m_per, n = blk.shape

    def body(x_ref, out_ref, send_sems, recv_sems, local_sem):
        x, y, c, sibling, chips = _place()
        me = (x, y, c)

        def rows(px, py, pc):
            return out_ref.at[pl.ds((4 * px + 2 * py + pc) * m_per, m_per), :]

        def copy(k, block, to, src=None):
            return pltpu.make_async_remote_copy(
                src_ref=rows(*block) if src is None else src, dst_ref=rows(*block),
                send_sem=send_sems.at[k], recv_sem=recv_sems.at[k], device_id=to, device_id_type=MESH)

        mine = pltpu.make_async_copy(x_ref, rows(*me), local_sem)
        mine.start()
        first = [copy(0, me, sibling, src=x_ref)]
        first += [copy(1 + j, me, (*chip, c), src=x_ref) for j, chip in enumerate(chips)]
        for cp in first:
            cp.start()
        passed = [copy(4 + j, (*chip, c), sibling) for j, chip in enumerate(chips)]
        for j, chip in enumerate(chips):
            copy(1 + j, (*chip, c), me).wait_recv()
            passed[j].start()
        copy(0, sibling, me).wait_recv()
        for j, chip in enumerate(chips):
            copy(4 + j, (*chip, 1 - c), me).wait_recv()
        for cp in first + passed:
            cp.wait_send()
        mine.wait()

    return pl.pallas_call(
        body, out_shape=jax.ShapeDtypeStruct((N_DEV * m_per, n), blk.dtype),
        in_specs=[pl.BlockSpec(memory_space=pltpu.VMEM)], out_specs=pl.BlockSpec(memory_space=pltpu.VMEM),
        scratch_shapes=[pltpu.SemaphoreType.DMA((7,)), pltpu.SemaphoreType.DMA((7,)), pltpu.SemaphoreType.DMA],
        name=name)(blk)


HBM_SPEC = pl.BlockSpec(memory_space=pltpu.HBM)
SEM_SPEC = pl.BlockSpec(memory_space=pltpu.SEMAPHORE)
SIDE_EFFECT = pltpu.SideEffectType.DATAFLOW_SIDE_EFFECTING


def _dma_sems(n):
    return [pltpu.SemaphoreType.DMA((n,)), pltpu.SemaphoreType.DMA((n,))]


def _halves(ref, axis, c):
    r2 = ref.shape[axis] // 2
    lead = (slice(None),) * axis
    return ref.at[lead + (pl.ds(r2 * c, r2),)], ref.at[lead + (pl.ds(r2 * (1 - c), r2),)]


def _gather_copies(srcs, lands, send_sems, recv_sems):
    x, y, c, _, chips = _place()
    sends, lands_here = [], []
    for j, (cx, cy) in enumerate(chips):
        for i, (s, t) in enumerate(zip(srcs, lands)):
            k = 3 * i + j
            mine = _halves(t.at[:, 2 * x + y], 1, c)[0]
            theirs = _halves(t.at[:, 2 * cx + cy], 1, c)[0]
            sends.append(pltpu.make_async_remote_copy(src_ref=_halves(s, 1, c)[0], dst_ref=mine, send_sem=send_sems.at[k],
                                                      recv_sem=recv_sems.at[k], device_id=(cx, cy, c), device_id_type=MESH))
            lands_here.append(pltpu.make_async_remote_copy(src_ref=theirs, dst_ref=theirs, send_sem=send_sems.at[k],
                                                           recv_sem=recv_sems.at[k], device_id=(cx, cy, c), device_id_type=MESH))
    return sends, lands_here


def _exchange_copies(srcs, lands, send_sems, recv_sems):
    x, y, c, _, chips = _place()
    sends = [pltpu.make_async_remote_copy(src_ref=s.at[2 * cx + cy], dst_ref=t.at[j], send_sem=send_sems.at[3 * i + j],
                                          recv_sem=recv_sems.at[3 * i + j], device_id=(cx, cy, c), device_id_type=MESH)
             for j, (cx, cy) in enumerate(chips) for i, (s, t) in enumerate(zip(srcs, lands))]
    return sends, sends


def _split_start(name, copies, srcs, lands, after):
    ns, nl = len(srcs), len(lands)
    n_copies = 3 * ns

    def body(*refs):
        send_sems, recv_sems = refs[ns + nl + 1], refs[ns + nl + 2]
        for cp in copies(refs[:ns], refs[ns:ns + nl], send_sems, recv_sems)[0]:
            cp.start()
        refs[-1][...] = jnp.zeros_like(refs[-1])

    arrs = list(srcs) + list(lands)
    res = pl.pallas_call(
        body, name=name,
        out_shape=(pltpu.SemaphoreType.DMA((n_copies,)), pltpu.SemaphoreType.DMA((n_copies,)))
        + tuple(pltpu.HBM(a.shape, a.dtype) for a in arrs) + (jax.ShapeDtypeStruct((8, LANES), F32),),
        in_specs=[HBM_SPEC] * (ns + nl) + [ANY],
        out_specs=(SEM_SPEC, SEM_SPEC) + (HBM_SPEC,) * (ns + nl) + (pl.BlockSpec(memory_space=pltpu.VMEM),),
        input_output_aliases={i: 2 + i for i in range(ns + nl)},
        compiler_params=pltpu.CompilerParams(has_side_effects=SIDE_EFFECT),
    )(*[pltpu.with_memory_space_constraint(a, pltpu.HBM) for a in arrs], after)
    return res[0], res[1], list(res[2:2 + ns]), list(res[2 + ns:2 + ns + nl]), res[-1]


def _split_wait(name, copies, send_sems, recv_sems, srcs, lands, after):
    ns, nl = len(srcs), len(lands)

    def body(*refs):
        sends, lands_here = copies(refs[:ns], refs[ns:ns + nl], refs[ns + nl], refs[ns + nl + 1])
        for cp in sends:
            cp.wait_send()
        for cp in lands_here:
            cp.wait_recv()

    arrs = list(srcs) + list(lands)
    res = pl.pallas_call(
        body, name=name, out_shape=tuple(pltpu.HBM(a.shape, a.dtype) for a in arrs),
        in_specs=[HBM_SPEC] * (ns + nl) + [SEM_SPEC, SEM_SPEC, ANY], out_specs=(HBM_SPEC,) * (ns + nl),
        input_output_aliases={i: i for i in range(ns + nl)},
        compiler_params=pltpu.CompilerParams(has_side_effects=SIDE_EFFECT),
    )(*arrs, send_sems, recv_sems, after)
    return list(res[:ns]), list(res[ns:])


def _gather_forward(name, lands):
    n = len(lands)

    def body(*refs):
        ins, outs = refs[:n], refs[n:2 * n]
        send_sems, recv_sems = refs[2 * n:]
        x, y, c, sibling, chips = _place()
        sends, arrivals = [], []
        for j, (cx, cy) in enumerate(chips):
            for i in range(n):
                k = 3 * i + j
                src = _halves(ins[i].at[:, 2 * cx + cy], 1, c)[0]
                dst, theirs = _halves(outs[i].at[:, 2 * cx + cy], 1, c)
                sends.append(pltpu.make_async_remote_copy(src_ref=src, dst_ref=dst, send_sem=send_sems.at[k], recv_sem=recv_sems.at[k],
                                                          device_id=sibling, device_id_type=MESH))
                arrivals.append(pltpu.make_async_remote_copy(src_ref=theirs, dst_ref=theirs, send_sem=send_sems.at[k],
                                                             recv_sem=recv_sems.at[k], device_id=sibling, device_id_type=MESH))
        for cp in sends:
            cp.start()
        for cp in arrivals:
            cp.wait_recv()
        for cp in sends:
            cp.wait_send()

    return pl.pallas_call(
        body, out_shape=[jax.ShapeDtypeStruct(t.shape, t.dtype) for t in lands], in_specs=[ANY] * n, out_specs=[ANY] * n,
        input_output_aliases={i: i for i in range(n)}, scratch_shapes=_dma_sems(3 * n), name=name)(*lands)


def _swap_halves(name, gs):
    n = len(gs)

    def body(*refs):
        ins, got = refs[:n], refs[n:2 * n]
        send_sems, recv_sems = refs[2 * n:]
        x, y, c, sibling, _ = _place()
        sends = [pltpu.make_async_remote_copy(src_ref=_halves(ins[i], 2, c)[1], dst_ref=got[i], send_sem=send_sems.at[i],
                                              recv_sem=recv_sems.at[i], device_id=sibling, device_id_type=MESH) for i in range(n)]
        for cp in sends:
            cp.start()
        for cp in sends:
            cp.wait_recv()
        for cp in sends:
            cp.wait_send()

    return pl.pallas_call(
        body, out_shape=[jax.ShapeDtypeStruct(t.shape[:2] + (t.shape[2] // 2,) + t.shape[3:], t.dtype) for t in gs],
        in_specs=[ANY] * n, out_specs=[ANY] * n, scratch_shapes=_dma_sems(n), name=name)(*gs)


def _share_halves(name, ts):
    n = len(ts)

    def body(*refs):
        ins, outs = refs[:n], refs[n:2 * n]
        send_sems, recv_sems = refs[2 * n:]
        x, y, c, sibling, _ = _place()
        sends, arrivals = [], []
        for i in range(n):
            mine, theirs = _halves(outs[i], 1, c)
            sends.append(pltpu.make_async_remote_copy(src_ref=ins[i], dst_ref=mine, send_sem=send_sems.at[i], recv_sem=recv_sems.at[i],
                                                      device_id=sibling, device_id_type=MESH))
            arrivals.append(pltpu.make_async_remote_copy(src_ref=ins[i], dst_ref=theirs, send_sem=send_sems.at[i],
                                                         recv_sem=recv_sems.at[i], device_id=sibling, device_id_type=MESH))
        for cp in sends:
            cp.start()
        for cp in arrivals:
            cp.wait_recv()
        for cp in sends:
            cp.wait_send()

    return pl.pallas_call(
        body, out_shape=[jax.ShapeDtypeStruct((t.shape[0], 2 * t.shape[1]) + t.shape[2:], t.dtype) for t in ts],
        in_specs=[ANY] * n, out_specs=[ANY] * n, scratch_shapes=_dma_sems(n), name=name)(*ts)


PACK_W = 1024
PACK_TB = 512


def _sum_rows(name, parts, out_dtype):
    def f(*vals):
        acc = vals[0]
        for v in vals[1:]:
            acc = acc + v
        return (acc,)

    return _rowwise_fwd(name, f, [(p, PACK_W, 0) for p in parts], [], [(PACK_W, out_dtype)], tb=_tile(parts[0].shape[0], PACK_TB))[0]


def _sum_slots(name, ops, rows, out_dtype):
    cw = ops[0][0].shape[1]
    tb = rows
    while tb * cw > 300_000 and tb % 32 == 0:
        tb //= 2
    per = rows // tb

    def body(*refs):
        acc = refs[0][...].astype(F32)
        for r in refs[1:-1]:
            acc = acc + r[...].astype(F32)
        refs[-1][...] = acc.astype(refs[-1].dtype)

    return pl.pallas_call(
        body, grid=(per,), in_specs=[pl.BlockSpec((tb, cw), lambda i, s=s: (s * per + i, 0)) for _, s in ops],
        out_specs=pl.BlockSpec((tb, cw), lambda i: (i, 0)), out_shape=jax.ShapeDtypeStruct((rows, cw), out_dtype),
        compiler_params=_cp("parallel"), name=name)(*[a for a, _ in ops])


def _adamw(name, w, g, m, v, row0=0, unit=None, prev=None, dep=None):
    r, cw = w.shape
    tb = unit or r
    while tb * cw > 300_000 and tb % 16 == 0:
        tb //= 2
    off = row0 // tb
    c1 = 1.0 / (1.0 - ADAM_B1 ** ADAM_STEP)
    c2 = 1.0 / (1.0 - ADAM_B2 ** ADAM_STEP)

    def body(w_ref, g_ref, m_ref, v_ref, *rest):
        go_ref, d_ref, mo_ref, vo_ref = rest[-4:]
        gg = g_ref[...]
        mn = ADAM_B1 * m_ref[...] + (1.0 - ADAM_B1) * gg
        vn = ADAM_B2 * v_ref[...] + (1.0 - ADAM_B2) * jnp.square(gg)
        go_ref[...] = gg
        d_ref[...] = -ADAM_LR * ((mn * c1) / (jnp.sqrt(vn * c2) + ADAM_EPS) + ADAM_WD * w_ref[...])
        mo_ref[...] = mn
        vo_ref[...] = vn

    full = pl.BlockSpec((tb, cw), lambda i: (off + i, 0))
    extra = ([] if prev is None else list(prev)) + ([] if dep is None else [dep])
    n_prev = 0 if prev is None else 4
    return pl.pallas_call(
        body, grid=(g.shape[0] // tb,),
        in_specs=[full, pl.BlockSpec((tb, cw), lambda i: (i, 0)), full, full] + [pl.BlockSpec(memory_space=pl.ANY)] * len(extra),
        out_specs=[full] * 4, out_shape=[jax.ShapeDtypeStruct((r, cw), F32)] * 4,
        input_output_aliases={4 + i: i for i in range(n_prev)},
        compiler_params=_cp("parallel"), name=name)(w, g, m, v, *extra)


def _silu_rows(c):
    def body(c_ref, o_ref):
        rows = lax.broadcasted_iota(jnp.int32, o_ref.shape, 0)
        o_ref[...] = jnp.where(rows == 0, jnp.broadcast_to(_silu(c_ref[...]), o_ref.shape), 0.0)

    return pl.pallas_call(body, out_shape=jax.ShapeDtypeStruct((8, c.shape[1]), F32), name="cond_silu")(c)


_KINDS = ("w_in", "w_ssd_out", "w_attn_out", "w_pool_mix", "w_pool_out", "w_out", "w_ff1", "w_ff2")
_SMALL = ("b_ada", "norm1_w", "norm2_w", "conv_b", "dt_bias", "a_log", "d_skip", "ssd_norm_w", "q_norm_w", "k_norm_w",
          "pool_scale")
_ORDER = ("w_ada", "b_ada", "norm1_w", "norm2_w", "w_in", "conv_w", "conv_b", "dt_bias", "a_log", "d_skip", "ssd_norm_w",
          "w_ssd_out", "q_norm_w", "k_norm_w", "w_attn_out", "w_pool_mix", "pool_scale", "w_pool_out", "w_out", "w_ff1", "w_ff2")


def _pack_flat(arrs, rows, dtype):
    flat = jnp.concatenate([a.reshape(-1).astype(dtype) for a in arrs])
    return jnp.pad(flat, (0, rows * PACK_W - flat.shape[0])).reshape(rows, PACK_W)


def _unpack_flat(buf, shapes):
    flat = buf.reshape(-1)
    out, off = [], 0
    for shp in shapes:
        n = int(np.prod(shp))
        out.append(flat[off:off + n].reshape(shp))
        off += n
    return out


def _small_rows(n_elems):
    return -(-n_elems // (8 * PACK_W)) * 8


def kernel(x, c, w_ada, b_ada, norm1_w, norm2_w, w_in, conv_w, conv_b, dt_bias, a_log, d_skip, ssd_norm_w, w_ssd_out, q_norm_w, k_norm_w, w_attn_out, w_pool_mix, pool_scale, w_pool_out, w_out, w_ff1, w_ff2, loss_target, m_w_ada, m_b_ada, m_norm1_w, m_norm2_w, m_w_in, m_conv_w, m_conv_b, m_dt_bias, m_a_log, m_d_skip, m_ssd_norm_w, m_w_ssd_out, m_q_norm_w, m_k_norm_w, m_w_attn_out, m_w_pool_mix, m_pool_scale, m_w_pool_out, m_w_out, m_w_ff1, m_w_ff2, v_w_ada, v_b_ada, v_norm1_w, v_norm2_w, v_w_in, v_conv_w, v_conv_b, v_dt_bias, v_a_log, v_d_skip, v_ssd_norm_w, v_w_ssd_out, v_q_norm_w, v_k_norm_w, v_w_attn_out, v_w_pool_mix, v_pool_scale, v_w_pool_out, v_w_out, v_w_ff1, v_w_ff2):
    w = dict(w_ada=w_ada, b_ada=b_ada, norm1_w=norm1_w, norm2_w=norm2_w, w_in=w_in, conv_w=conv_w, conv_b=conv_b, dt_bias=dt_bias, a_log=a_log, d_skip=d_skip, ssd_norm_w=ssd_norm_w, w_ssd_out=w_ssd_out, q_norm_w=q_norm_w, k_norm_w=k_norm_w, w_attn_out=w_attn_out, w_pool_mix=w_pool_mix, pool_scale=pool_scale, w_pool_out=w_pool_out, w_out=w_out, w_ff1=w_ff1, w_ff2=w_ff2)
    m = dict(w_ada=m_w_ada, b_ada=m_b_ada, norm1_w=m_norm1_w, norm2_w=m_norm2_w, w_in=m_w_in, conv_w=m_conv_w, conv_b=m_conv_b, dt_bias=m_dt_bias, a_log=m_a_log, d_skip=m_d_skip, ssd_norm_w=m_ssd_norm_w, w_ssd_out=m_w_ssd_out, q_norm_w=m_q_norm_w, k_norm_w=m_k_norm_w, w_attn_out=m_w_attn_out, w_pool_mix=m_w_pool_mix, pool_scale=m_pool_scale, w_pool_out=m_w_pool_out, w_out=m_w_out, w_ff1=m_w_ff1, w_ff2=m_w_ff2)
    v = dict(w_ada=v_w_ada, b_ada=v_b_ada, norm1_w=v_norm1_w, norm2_w=v_norm2_w, w_in=v_w_in, conv_w=v_conv_w, conv_b=v_conv_b, dt_bias=v_dt_bias, a_log=v_a_log, d_skip=v_d_skip, ssd_norm_w=v_ssd_norm_w, w_ssd_out=v_w_ssd_out, q_norm_w=v_q_norm_w, k_norm_w=v_k_norm_w, w_attn_out=v_w_attn_out, w_pool_mix=v_w_pool_mix, pool_scale=v_pool_scale, w_pool_out=v_w_pool_out, w_out=v_w_out, w_ff1=v_w_ff1, w_ff2=v_w_ff2)
    chip = 2 * lax.axis_index("x") + lax.axis_index("y")
    dev = 2 * chip + lax.axis_index("c")
    ada_cols = w_ada.shape[2]

    n_conv = conv_w.size // PACK_W
    rows1 = _small_rows((1 + n_conv) * PACK_W)
    blk = jnp.concatenate([_silu_rows(c)[:1], conv_w.reshape(n_conv, PACK_W), jnp.zeros((rows1 - 1 - n_conv, PACK_W), F32)])
    first = _allgather8("gather_cond", blk).reshape(N_DEV, rows1, PACK_W)
    cond_all = first[:, 0]
    conv_all = first[0::2, 1:1 + n_conv].reshape((N_CHIPS,) + conv_w.shape)
    conv_full = jnp.moveaxis(conv_all, 0, 2).reshape(DEPTH, SSD_CONV, CONV_DIM)
    b_cols = lax.dynamic_slice_in_dim(b_ada, chip * ada_cols, ada_cols, axis=1)
    mod_cols = jnp.stack([_matmul("ada_fwd", cond_all, w_ada[l], "nn", precise=True) + b_cols[l][None, :] for l in range(DEPTH)])
    mod_all = _allgather8("gather_mod", mod_cols.reshape(-1, PACK_W)).reshape(N_DEV, DEPTH, N_DEV, ada_cols)
    mine = lax.dynamic_index_in_dim(mod_all[0::2], dev, axis=2, keepdims=False)
    mods = jnp.moveaxis(mine, 0, 1).reshape(DEPTH, 6, D_MODEL)

    core = lax.axis_index("c")
    small_w = dict({k: w[k] for k in _SMALL[1:]}, conv_w=conv_full)

    def start_gather(tag, lo, n, after):
        shards = [w[k][lo:lo + n].astype(BF16) for k in _KINDS]
        lands = [lax.empty((n, N_CHIPS) + s.shape[1:], BF16) for s in shards]
        return _split_start("gather_start_" + tag, _gather_copies, shards, lands, after)

    def finish_gather(tag, handle, after):
        shards, lands = _split_wait("gather_wait_" + tag, _gather_copies, handle[0], handle[1], handle[2], handle[3], after)
        lands = _gather_forward("gather_forward_" + tag, lands)
        wg = {k: lax.dynamic_update_slice_in_dim(t, s[:, None], chip, axis=1) for k, t, s in zip(_KINDS, lands, shards)}
        n = shards[0].shape[0]
        for k in ("w_ssd_out", "w_pool_out", "w_out", "w_ff2"):
            wg[k] = wg[k].reshape(n, -1, D_MODEL)
        wg["w_pool_mix"] = jnp.moveaxis(wg["w_pool_mix"], 1, 2).reshape(n, 4, POOL_GW, POOL_GW)
        return wg

    gather_a = start_gather("a", 0, 1, c)
    wg_a = finish_gather("a", gather_a, mods)
    gather_b = start_gather("b", 1, DEPTH - 1, wg_a["w_in"])
    mods = mods + gather_b[4][0, 0]

    xc = x[0]
    Ws, saved = [None] * DEPTH, [None] * DEPTH
    Ws[0] = _layer_weights(wg_a, 0, small_w, 0)
    xc, saved[0] = _layer_fwd(xc, mods[0], Ws[0], 0)
    wg_b = finish_gather("b", gather_b, xc)
    for l in range(1, DEPTH):
        Ws[l] = _layer_weights(wg_b, l - 1, small_w, l)
        xc, saved[l] = _layer_fwd(xc, mods[l], Ws[l], l - 1)
    dx, loss = _loss_and_grad(xc, loss_target[0])
    dmods, grads = [None] * DEPTH, [None] * DEPTH

    def backward(l, lg, dx, mod, bufs):
        dx, dmods[l], g = _layer_bwd(dx, mod, Ws[l], saved[l], lg, bufs)
        grads[l] = _layer_grads_by_chip(g, bufs["w_in"], lg)
        return dx, {k: grads[l][k] for k in _KINDS}

    bufs_b = {k: lax.empty((N_CHIPS, DEPTH - 1) + w[k].shape[1:], BF16) for k in _KINDS}
    for l in reversed(range(1, DEPTH)):
        dx, bufs_b = backward(l, l - 1, dx, mods[l], bufs_b)

    def flat(t):
        return t.reshape(-1, t.shape[-1])

    def start_exchange(tag, bufs, after):
        gs = [bufs[k] for k in _KINDS]
        got = _swap_halves("grad_swap_" + tag, gs)
        own = [lax.dynamic_slice_in_dim(t, (t.shape[2] // 2) * core, t.shape[2] // 2, axis=2) for t in gs]
        pairs = [_sum_slots(f"sum_pair_{tag}_{k}", [(flat(a), 0), (flat(b), 0)], flat(a).shape[0], BF16).reshape(a.shape)
                 for k, a, b in zip(_KINDS, own, got)]
        lands = [lax.empty((3,) + p.shape[1:], BF16) for p in pairs]
        return _split_start("exchange_start_" + tag, _exchange_copies, pairs, lands, after)

    def finish_exchange(tag, handle, after):
        pairs, partials = _split_wait("exchange_wait_" + tag, _exchange_copies, handle[0], handle[1], handle[2], handle[3], after)
        mine = [lax.dynamic_index_in_dim(p, chip, axis=0, keepdims=False) for p in pairs]
        totals = [_sum_slots(f"sum_chips_{tag}_{k}", [(flat(a), 0)] + [(flat(p), s) for s in range(3)], flat(a).shape[0], F32).reshape(a.shape)
                  for k, a, p in zip(_KINDS, mine, partials)]
        return [lax.dynamic_update_slice_in_dim(t, mine_t, mine_t.shape[1] * core, axis=1)
                for t, mine_t in zip(_share_halves("grad_share_" + tag, totals), totals)]

    def adamw_group(tag, lo, gs, prev, dep):
        out = {}
        for k, gk in zip(_KINDS, gs):
            shp = w[k].shape
            unit = int(np.prod(shp[1:-1]))
            two_d = lambda t: t.reshape(-1, shp[-1])
            out[k] = _adamw(f"adamw_{tag}_{k}", two_d(w[k]), two_d(gk), two_d(m[k]), two_d(v[k]), row0=lo * unit, unit=unit,
                            prev=None if prev is None else prev[k], dep=dep)
        return out

    exchange_b = start_exchange("b", bufs_b, dx)
    bufs_a = {k: lax.empty((N_CHIPS, 1) + w[k].shape[1:], BF16) for k in _KINDS}
    dx, bufs_a = backward(0, 0, dx, mods[0] + exchange_b[4][0, 0], bufs_a)
    g_b = finish_exchange("b", exchange_b, dx)
    exchange_a = start_exchange("a", bufs_a, g_b[0])
    adam_b = adamw_group("b", 1, g_b, None, exchange_a[4])
    g_a = finish_exchange("a", exchange_a, adam_b[_KINDS[-1]][0])
    adam = adamw_group("a", 0, g_a, adam_b, None)
    grad_x, dmods = dx, jnp.stack(dmods)

    small = ([dmods] + [jnp.stack([grads[l][k] for l in range(DEPTH)]) for k in _SMALL[1:] + ("conv_w",)] + [loss[:, :1]])
    n_small = sum(int(np.prod(a.shape)) for a in small)
    rows_small = _small_rows(n_small)
    small_all = _allgather8("gather_small", _pack_flat(small, rows_small, F32))
    parts = [small_all[d * rows_small:(d + 1) * rows_small] for d in range(N_DEV)]
    small_sum = _unpack_flat(_sum_rows("sum_small", parts, F32), [a.shape for a in small])
    g_out = {"b_ada": small_sum[0].reshape(DEPTH, 6 * D_MODEL)}
    for k, t in zip(_SMALL[1:], small_sum[1:-2]):
        g_out[k] = t
    g_out["conv_w"] = lax.dynamic_slice_in_dim(small_sum[-2], chip * conv_w.shape[2], conv_w.shape[2], axis=2)
    loss_out = small_sum[-1][0, 0]
    dmod_all = jnp.stack([p[:DEPTH * 6].reshape(DEPTH, 6 * D_MODEL) for p in parts])
    dmod_cols = lax.dynamic_slice_in_dim(dmod_all, chip * ada_cols, ada_cols, axis=2)
    g_out["w_ada"] = jnp.stack([_matmul("ada_dw", cond_all, dmod_cols[:, l], "tn", precise=True) for l in range(DEPTH)])

    deltas, new_m, new_v = {}, {}, {}
    for k in _KINDS:
        g_out[k], deltas[k], new_m[k], new_v[k] = (t.reshape(w[k].shape) for t in adam[k])
    for k in ("w_ada", "conv_w"):
        shp = w[k].shape
        two_d = (int(np.prod(shp[:-1])), shp[-1])
        res = _adamw("adamw_" + k, *(t.reshape(two_d) for t in (w[k], g_out[k], m[k], v[k])))
        deltas[k], new_m[k], new_v[k] = (t.reshape(shp) for t in res[1:])
    small_shapes = [w[k].shape for k in _SMALL]
    n_sm = sum(int(np.prod(s)) for s in small_shapes)
    res = _adamw("adamw_small", *[_pack_flat([t[k] for k in _SMALL], _small_rows(n_sm), F32) for t in (w, g_out, m, v)])[1:]
    for name_map, buf in zip((deltas, new_m, new_v), res):
        for k, t in zip(_SMALL, _unpack_flat(buf, small_shapes)):
            name_map[k] = t

    return (loss_out, grad_x[None], *[g_out[k] for k in _ORDER], *[deltas[k] for k in _ORDER],
            *[new_m[k] for k in _ORDER], *[new_v[k] for k in _ORDER])
```

```python
import functools
import math

import numpy as np
import jax
import jax.numpy as jnp
from jax import lax
from jax.experimental import pallas as pl
from jax.experimental.pallas import tpu as pltpu

F32, BF16 = jnp.float32, jnp.bfloat16
MESH = pl.DeviceIdType.MESH

D_MODEL = 1024
DEPTH = 4
N_CHIPS = 4
N_DEV = 8
SSD_HEADS = 16
SSD_HEAD_DIM = 64
SSD_STATE = 128
SSD_CHUNK = 128
SSD_CONV = 4
CONV_DIM = 1536
ATTN_HEAD_DIM = 128
ATTN_GROUP_W = 512
DILATIONS = (1, 4, 16)
ATTN_STEPS = 128
POOL_WINDOWS = (2, 4, 8, 16)
POOL_GW = 256
D_FF = 4096
EPS = 1e-6
IN_SIZES = (1024, 1536, 16, 1536, 1536, 1536, 1024, 3072)
IN_WIDTH = sum(IN_SIZES)
P_XBC, P_Q, P_K, P_V, P_GATES, P_Z, P_U, P_DT = 0, 1536, 3072, 4608, 6144, 9216, 10240, 11264
P_WIDTH = 12288
LANES = 128
NEG = -1e30
VMEM_LIMIT = 56 * 1024 * 1024

ADAM_LR, ADAM_B1, ADAM_B2, ADAM_EPS, ADAM_WD, ADAM_STEP = 0.001, 0.9, 0.999, 1e-08, 0.01, 10


def _alibi_slopes(n):
    def pow2(k):
        start = 2.0 ** (-8.0 / k)
        return [start ** (i + 1) for i in range(k)]
    if math.log2(n).is_integer():
        s = pow2(n)
    else:
        c = 2 ** math.floor(math.log2(n))
        s = pow2(c) + pow2(2 * c)[0::2][: n - c]
    return np.sort(np.asarray(s, np.float32))[::-1].copy()


SLOPES = _alibi_slopes(12).reshape(3, 4)


def _cp(*sem):
    return pltpu.CompilerParams(dimension_semantics=sem, vmem_limit_bytes=VMEM_LIMIT)


_DIMS = {"nn": (((1,), (0,)), ((), ())), "nt": (((1,), (1,)), ((), ())), "tn": (((0,), (0,)), ((), ()))}


def _dot(a, b, mode):
    return lax.dot_general(a.astype(BF16), b.astype(BF16), _DIMS[mode], preferred_element_type=F32)


@functools.partial(jax.custom_vjp, nondiff_argnums=(2,))
def _bdot(a, b, mode):
    return _dot(a, b, mode)


def _bdot_fwd(a, b, mode):
    return _dot(a, b, mode), (a, b)


def _bdot_bwd(mode, res, ct):
    a, b = res
    if mode == "nn":
        return _dot(ct, b, "nt"), _dot(a, ct, "tn")
    if mode == "nt":
        return _dot(ct, b, "nn"), _dot(ct, a, "tn")
    return _dot(b, ct, "nt"), _dot(a, ct, "nn")


_bdot.defvjp(_bdot_fwd, _bdot_bwd)


def _hdot(a, b):
    return jnp.dot(a, b, precision=lax.Precision.HIGHEST, preferred_element_type=F32)


def _tri(n, lower):
    r = lax.broadcasted_iota(jnp.int32, (n, n), 0)
    c = lax.broadcasted_iota(jnp.int32, (n, n), 1)
    return (r >= c if lower else r <= c).astype(F32)


@jax.custom_vjp
def _csum(a):
    return _hdot(_tri(a.shape[0], True), a)


def _csum_fwd(a):
    return _csum(a), None


def _csum_bwd(_, ct):
    return (_hdot(_tri(ct.shape[0], False), ct),)


_csum.defvjp(_csum_fwd, _csum_bwd)


def _softplus(x):
    return jnp.maximum(x, 0.0) + jnp.log(1.0 + jnp.exp(-jnp.abs(x)))


def _sigmoid(x):
    return 1.0 / (1.0 + jnp.exp(-x))


def _silu(x):
    return x * _sigmoid(x)


def _tile(n, cap):
    t = min(n, cap)
    while n % t:
        t //= 2
    return t


MM_TILE, MM_KTILE = 1024, 2048


def _matmul(name, a, b, mode, out_dtype=F32, precise=False, layer=None, chips=0, out_chips=0, into=None):
    if mode == "nn":
        (m, k), n = a.shape, (4 * chips if chips else b.shape[-1])
    elif mode == "nt":
        (m, k), n = a.shape, b.shape[-2]
    else:
        (k, m), n = a.shape, b.shape[-1]
    tm = _tile(m // N_CHIPS if (into is not None and not out_chips) else m, MM_TILE)
    tn = _tile(chips if (chips and mode == "nn") else (out_chips or n), MM_TILE)
    tk = _tile(chips if (chips and mode == "nt") else k, MM_KTILE)
    nk = k // tk
    a_spec = pl.BlockSpec((tk, tm), lambda i, j, l: (l, i)) if mode == "tn" else pl.BlockSpec((tm, tk), lambda i, j, l: (i, l))
    if chips:
        if mode == "nn":
            per = chips // tn
            b_spec = pl.BlockSpec((None, None, tk, tn), lambda i, j, l: (layer, j // per, l, j % per))
        else:
            per = chips // tk
            b_spec = pl.BlockSpec((None, None, tn, tk), lambda i, j, l: (layer, l // per, j, l % per))
    elif layer is not None:
        b_spec = (pl.BlockSpec((None, tn, tk), lambda i, j, l: (layer, j, l)) if mode == "nt"
                  else pl.BlockSpec((None, tk, tn), lambda i, j, l: (layer, l, j)))
    else:
        b_spec = pl.BlockSpec((tn, tk), lambda i, j, l: (j, l)) if mode == "nt" else pl.BlockSpec((tk, tn), lambda i, j, l: (l, j))
    if into is not None:
        buf, slot = into
        if out_chips:
            per_o = out_chips // tn
            o_spec = pl.BlockSpec((None, None, tm, tn), lambda i, j, l: (j // per_o, slot, i, j % per_o))
        else:
            per_r = m // N_CHIPS // tm
            o_spec = pl.BlockSpec((None, None, tm, tn), lambda i, j, l: (i // per_r, slot, i % per_r, j))
        o_shape = jax.ShapeDtypeStruct(buf.shape, buf.dtype)
    elif out_chips:
        per_o = out_chips // tn
        o_spec = pl.BlockSpec((None, tm, tn), lambda i, j, l: (j // per_o, i, j % per_o))
        o_shape = jax.ShapeDtypeStruct((N_CHIPS, m, out_chips), out_dtype)
    else:
        o_spec = pl.BlockSpec((tm, tn), lambda i, j, l: (i, j))
        o_shape = jax.ShapeDtypeStruct((m, n), out_dtype)

    def part(a_ref, b_ref):
        if precise:
            return lax.dot_general(a_ref[...], b_ref[...], _DIMS[mode], precision=lax.Precision.HIGHEST,
                                   preferred_element_type=F32)
        return _dot(a_ref[...], b_ref[...], mode)

    n_in = 2 if into is None else 3

    if nk == 1:
        def body(*refs):
            o_ref = refs[n_in]
            o_ref[...] = part(refs[0], refs[1]).astype(o_ref.dtype)
        scratch = []
    else:
        def body(*refs):
            o_ref, acc_ref = refs[n_in], refs[n_in + 1]
            l = pl.program_id(2)
            p = part(refs[0], refs[1])

            @pl.when(l == 0)
            def _():
                acc_ref[...] = p

            @pl.when((l > 0) & (l < nk - 1))
            def _():
                acc_ref[...] += p

            @pl.when(l == nk - 1)
            def _():
                o_ref[...] = (acc_ref[...] + p).astype(o_ref.dtype)
        scratch = [pltpu.VMEM((tm, tn), F32)]

    extra = {} if into is None else dict(input_output_aliases={2: 0})
    return pl.pallas_call(
        body, grid=(m // tm, n // tn, nk), in_specs=[a_spec, b_spec] + ([] if into is None else [pl.BlockSpec(memory_space=pl.ANY)]),
        out_specs=o_spec, out_shape=o_shape, scratch_shapes=scratch, compiler_params=_cp("parallel", "parallel", "arbitrary"),
        name=name, **extra)(*((a, b) if into is None else (a, b, into[0])))


def _group_matmul(name, a, w, mode, out_dtype=F32, layer=0):
    s = a.shape[0]
    tb = 512
    gw = POOL_GW
    if mode == "tn":
        def body(a_ref, b_ref, o_ref):
            part = _dot(a_ref[...], b_ref[...], "tn")

            @pl.when(pl.program_id(1) == 0)
            def _():
                o_ref[0] = part

            @pl.when(pl.program_id(1) > 0)
            def _():
                o_ref[0] += part

        return pl.pallas_call(
            body, grid=(4, s // tb),
            in_specs=[pl.BlockSpec((tb, gw), lambda g, i: (i, g)), pl.BlockSpec((tb, gw), lambda g, i: (i, g))],
            out_specs=pl.BlockSpec((1, gw, gw), lambda g, i: (g, 0, 0)),
            out_shape=jax.ShapeDtypeStruct((4, gw, gw), F32),
            compiler_params=_cp("parallel", "arbitrary"), name=name)(a, w)

    def body(a_ref, w_ref, o_ref):
        o_ref[...] = _dot(a_ref[...], w_ref[...], mode).astype(o_ref.dtype)

    return pl.pallas_call(
        body, grid=(s // tb, 4),
        in_specs=[pl.BlockSpec((tb, gw), lambda i, g: (i, g)), pl.BlockSpec((None, None, gw, gw), lambda i, g: (layer, g, 0, 0))],
        out_specs=pl.BlockSpec((tb, gw), lambda i, g: (i, g)),
        out_shape=jax.ShapeDtypeStruct((s, 4 * gw), out_dtype),
        compiler_params=_cp("parallel", "parallel"), name=name)(a, w)


def _rspec(tb, width, cb):
    return pl.BlockSpec((tb, width), lambda i: (i, cb))


def _pspec(shape):
    return pl.BlockSpec(shape, lambda i: (0, 0))


def _rowwise_fwd(name, f, rows, pars, outs, tb=256):
    s = rows[0][0].shape[0]
    nin = len(rows) + len(pars)

    def body(*refs):
        res = f(*[r[...].astype(F32) for r in refs[:nin]])
        for o, v in zip(refs[nin:], res):
            o[...] = v.astype(o.dtype)

    return pl.pallas_call(
        body, grid=(s // tb,),
        in_specs=[_rspec(tb, w, cb) for _, w, cb in rows] + [_pspec(p.shape) for p in pars],
        out_specs=[_rspec(tb, w, 0) for w, _ in outs],
        out_shape=[jax.ShapeDtypeStruct((s, w), dt) for w, dt in outs],
        compiler_params=_cp("parallel"), name=name)(*[r[0] for r in rows], *pars)


def _rowwise_bwd(name, f, rows, pars, cts, need, add=None, tb=256, gdt=None):
    s = rows[0][0].shape[0]
    nr, npar, nc = len(rows), len(pars), len(cts)
    nin = nr + npar + nc + (1 if add is not None else 0)

    def body(*refs):
        ins = [r[...].astype(F32) for r in refs[:nr + npar]]
        _, vjp = jax.vjp(f, *ins)
        g = vjp(tuple(c[...].astype(F32) for c in refs[nr + npar:nr + npar + nc]))
        outs = refs[nin:]
        k = 0
        for j in range(nr):
            if need[j]:
                v = g[j]
                if add is not None and add[0] == j:
                    v = v + refs[nin - 1][...]
                outs[k][...] = v.astype(outs[k].dtype)
                k += 1
        first = pl.program_id(0) == 0
        for j in range(npar):
            o, v = outs[k + j], g[nr + j]

            @pl.when(first)
            def _(o=o, v=v):
                o[...] = v

            @pl.when(jnp.logical_not(first))
            def _(o=o, v=v):
                o[...] += v

    in_specs = ([_rspec(tb, w, cb) for _, w, cb in rows] + [_pspec(p.shape) for p in pars]
                + [_rspec(tb, w, cb) for _, w, cb in cts])
    args = [r[0] for r in rows] + list(pars) + [c[0] for c in cts]
    if add is not None:
        in_specs.append(_rspec(tb, rows[add[0]][1], 0))
        args.append(add[1])
    gr = [(w, F32) for (_, w, _), nd in zip(rows, need) if nd]
    if gdt is not None:
        gr = [(w, dt) for (w, _), dt in zip(gr, gdt)]
    return pl.pallas_call(
        body, grid=(s // tb,), in_specs=in_specs,
        out_specs=[_rspec(tb, w, 0) for w, _ in gr] + [_pspec(p.shape) for p in pars],
        out_shape=[jax.ShapeDtypeStruct((s, w), dt) for w, dt in gr] + [jax.ShapeDtypeStruct(p.shape, F32) for p in pars],
        compiler_params=_cp("arbitrary"), name=name)(*args)


def _f_norm(x, nw, sc, sh):
    r = lax.rsqrt(jnp.mean(x * x, axis=-1, keepdims=True) + EPS)
    return ((x * r * nw) * (1.0 + sc) + sh,)


def _f_ssdgate(y, z, w):
    y2 = y * _silu(z)
    low = lax.broadcasted_iota(jnp.int32, y2.shape, 1) < 512
    sq = y2 * y2
    m0 = jnp.sum(jnp.where(low, sq, 0.0), axis=-1, keepdims=True) / 512.0
    m1 = jnp.sum(jnp.where(low, 0.0, sq), axis=-1, keepdims=True) / 512.0
    r = jnp.where(low, lax.rsqrt(m0 + EPS), lax.rsqrt(m1 + EPS))
    return (y2 * r * w,)


def _head_rms(t, w):
    outs = []
    for h in range(t.shape[1] // ATTN_HEAD_DIM):
        th = t[:, h * ATTN_HEAD_DIM:(h + 1) * ATTN_HEAD_DIM]
        outs.append(th * lax.rsqrt(jnp.mean(th * th, axis=-1, keepdims=True) + EPS) * w)
    return jnp.concatenate(outs, axis=1)


def _f_qknorm(q, k, v, qw, kw):
    return _head_rms(q, qw), _head_rms(k, kw), v


def _f_combine(o1, o2, o3, l1, l2, l3):
    m = lax.stop_gradient(jnp.maximum(jnp.maximum(l1, l2), l3))
    e1, e2, e3 = jnp.exp(l1 - m), jnp.exp(l2 - m), jnp.exp(l3 - m)
    return ((e1 * o1 + e2 * o2 + e3 * o3) / (e1 + e2 + e3),)


def _f_poolscale(pm, ps):
    return (pm * ps,)


def _f_merge(gates, ys, ya, yp):
    g = _sigmoid(gates)
    return (g[:, 0:1024] * ys + g[:, 1024:2048] * ya + g[:, 2048:3072] * yp,)


def _f_resid(x, o, g):
    return (x + g * o,)


def _f_relu2(a):
    return (jnp.square(jnp.maximum(a, 0.0)),)


def _loss_and_grad(y, tgt, tb=512):
    s, d = y.shape

    def body(y_ref, t_ref, dy_ref, l_ref):
        e = y_ref[...] - t_ref[...]
        dy_ref[...] = e * (1.0 / d)
        part = jnp.zeros((1, LANES), F32) + jnp.sum(e * e) * (0.5 / d)

        @pl.when(pl.program_id(0) == 0)
        def _():
            l_ref[...] = part

        @pl.when(pl.program_id(0) > 0)
        def _():
            l_ref[...] += part

    return pl.pallas_call(
        body, grid=(s // tb,), in_specs=[_rspec(tb, d, 0), _rspec(tb, d, 0)],
        out_specs=[_rspec(tb, d, 0), _pspec((1, LANES))],
        out_shape=[jax.ShapeDtypeStruct((s, d), F32), jax.ShapeDtypeStruct((1, LANES), F32)],
        compiler_params=_cp("arbitrary"), name="loss")(y, tgt)


def _shift_down(x, j):
    rows = lax.broadcasted_iota(jnp.int32, x.shape, 0)
    return jnp.where(rows < j, 0.0, pltpu.roll(x, j, 0))


def _shift_up(x, j):
    s = x.shape[0]
    rows = lax.broadcasted_iota(jnp.int32, x.shape, 0)
    return jnp.where(rows >= s - j, 0.0, pltpu.roll(x, s - j, 0))


CONV_CB = 256


def _conv_pre(x, w_ref, b_ref):
    acc = b_ref[...] + w_ref[SSD_CONV - 1:SSD_CONV, :] * x
    for j in range(1, SSD_CONV):
        acc = acc + w_ref[SSD_CONV - 1 - j:SSD_CONV - j, :] * _shift_down(x, j)
    return acc


def _conv_fwd(proj, cw, cb):
    s = proj.shape[0]

    def body(x_ref, w_ref, b_ref, o_ref):
        o_ref[...] = _silu(_conv_pre(x_ref[...], w_ref, b_ref))

    return pl.pallas_call(
        body, grid=(CONV_DIM // CONV_CB,),
        in_specs=[pl.BlockSpec((s, CONV_CB), lambda i: (0, P_XBC // CONV_CB + i)),
                  pl.BlockSpec((SSD_CONV, CONV_CB), lambda i: (0, i)), pl.BlockSpec((1, CONV_CB), lambda i: (0, i))],
        out_specs=pl.BlockSpec((s, CONV_CB), lambda i: (0, i)),
        out_shape=jax.ShapeDtypeStruct((s, CONV_DIM), F32), compiler_params=_cp("parallel"), name="conv_fwd")(proj, cw, cb)


def _conv_bwd(proj, cw, cb, dout):
    s = proj.shape[0]

    def body(x_ref, w_ref, b_ref, d_ref, dx_ref, dw_ref, db_ref):
        x = x_ref[...]
        a = _conv_pre(x, w_ref, b_ref)
        sg = _sigmoid(a)
        da = d_ref[...] * (sg + a * sg * (1.0 - sg))
        db_ref[...] = jnp.sum(da, axis=0, keepdims=True)
        dx = w_ref[SSD_CONV - 1:SSD_CONV, :] * da
        dw_ref[SSD_CONV - 1:SSD_CONV, :] = jnp.sum(da * x, axis=0, keepdims=True)
        for j in range(1, SSD_CONV):
            dx = dx + w_ref[SSD_CONV - 1 - j:SSD_CONV - j, :] * _shift_up(da, j)
            dw_ref[SSD_CONV - 1 - j:SSD_CONV - j, :] = jnp.sum(da * _shift_down(x, j), axis=0, keepdims=True)
        dx_ref[...] = dx.astype(dx_ref.dtype)

    return pl.pallas_call(
        body, grid=(CONV_DIM // CONV_CB,),
        in_specs=[pl.BlockSpec((s, CONV_CB), lambda i: (0, P_XBC // CONV_CB + i)),
                  pl.BlockSpec((SSD_CONV, CONV_CB), lambda i: (0, i)), pl.BlockSpec((1, CONV_CB), lambda i: (0, i)),
                  pl.BlockSpec((s, CONV_CB), lambda i: (0, i))],
        out_specs=[pl.BlockSpec((s, CONV_CB), lambda i: (0, i)), pl.BlockSpec((SSD_CONV, CONV_CB), lambda i: (0, i)),
                   pl.BlockSpec((1, CONV_CB), lambda i: (0, i))],
        out_shape=[jax.ShapeDtypeStruct((s, CONV_DIM), BF16), jax.ShapeDtypeStruct((SSD_CONV, CONV_DIM), F32),
                   jax.ShapeDtypeStruct((1, CONV_DIM), F32)],
        compiler_params=_cp("parallel"), name="conv_bwd")(proj, cw, cb, dout)


def _pool_window_sum(x, g, shift):
    s2 = x + shift(x, 1)
    s4 = s2 + shift(s2, 2)
    s8 = s4 + shift(s4, 4)
    s16 = s8 + shift(s8, 8)
    return jnp.where(g == 0, s2, jnp.where(g == 1, s4, jnp.where(g == 2, s8, s16)))


def _pool_count(shape, g):
    rows = lax.broadcasted_iota(jnp.int32, shape, 0)
    return jnp.minimum(rows + 1, jnp.left_shift(2, g)).astype(F32)


def _pool_fwd(proj):
    s = proj.shape[0]

    def body(u_ref, o_ref):
        g = pl.program_id(0)
        u = u_ref[...]
        o_ref[...] = (_pool_window_sum(u, g, _shift_down) / _pool_count(u.shape, g) - u).astype(o_ref.dtype)

    return pl.pallas_call(
        body, grid=(4,), in_specs=[pl.BlockSpec((s, POOL_GW), lambda g: (0, P_U // POOL_GW + g))],
        out_specs=pl.BlockSpec((s, POOL_GW), lambda g: (0, g)),
        out_shape=jax.ShapeDtypeStruct((s, 4 * POOL_GW), BF16), compiler_params=_cp("parallel"), name="pool_fwd")(proj)


def _pool_bwd(dp):
    s = dp.shape[0]

    def body(d_ref, o_ref):
        g = pl.program_id(0)
        d = d_ref[...]
        o_ref[...] = (_pool_window_sum(d / _pool_count(d.shape, g), g, _shift_up) - d).astype(o_ref.dtype)

    return pl.pallas_call(
        body, grid=(4,), in_specs=[pl.BlockSpec((s, POOL_GW), lambda g: (0, g))],
        out_specs=pl.BlockSpec((s, POOL_GW), lambda g: (0, g)),
        out_shape=jax.ShapeDtypeStruct((s, 4 * POOL_GW), BF16), compiler_params=_cp("parallel"), name="pool_bwd")(dp)


N_PAIRS = SSD_HEADS // 2
STATE_ROWS = N_PAIRS * SSD_STATE


def _ssd_chunk(xbc, dtr, hprev, dtb, alog, dsk):
    L = xbc.shape[0]
    xs, bm, cm = xbc[:, 0:1024], xbc[:, 1024:1280], xbc[:, 1280:1536]
    dt = _softplus(dtr + dtb)
    a = dt * (-jnp.exp(alog))
    acum = _csum(a)
    alast = jnp.sum(a, axis=0, keepdims=True)
    xdt = xs * dt
    xdecay = xdt * jnp.exp(alast - acum)
    eacum = jnp.exp(acum)
    elast = jnp.exp(alast)
    cb = [_bdot(cm[:, g * 128:(g + 1) * 128], bm[:, g * 128:(g + 1) * 128], "nt") for g in range(2)]
    rows = lax.broadcasted_iota(jnp.int32, (L, L), 0)
    cols = lax.broadcasted_iota(jnp.int32, (L, L), 1)
    causal = rows >= cols
    lane = lax.broadcasted_iota(jnp.int32, (L, LANES), 1)
    sub = lax.broadcasted_iota(jnp.int32, (LANES, L), 0)
    ys, hs = [], []
    for p in range(N_PAIRS):
        g = p // (N_PAIRS // 2)
        sl = slice(p * LANES, (p + 1) * LANES)
        ac = acum[:, sl]
        act = ac.T
        xp = xdt[:, sl]
        hp = hprev[p * SSD_STATE:(p + 1) * SSD_STATE, :]
        y = _bdot(cm[:, g * 128:(g + 1) * 128], hp, "nn") * eacum[:, sl] + dsk[:, sl] * xs[:, sl]
        for half in range(2):
            l0 = half * SSD_HEAD_DIM
            col = jnp.sum(jnp.where(lane == l0, ac, 0.0), axis=1, keepdims=True)
            row = jnp.sum(jnp.where(sub == l0, act, 0.0), axis=0, keepdims=True)
            decay = jnp.exp(jnp.where(causal, col - row, NEG))
            xh = jnp.where((lane >= l0) & (lane < l0 + SSD_HEAD_DIM), xp, 0.0)
            y = y + _bdot(cb[g] * decay, xh, "nn")
        ys.append(y)
        hs.append(elast[:, sl] * hp + _bdot(bm[:, g * 128:(g + 1) * 128], xdecay[:, sl], "tn"))
    return tuple(ys), tuple(hs)


def _ssd_fwd(xbc, proj, dtb, alog, dsk):
    s = xbc.shape[0]
    nc = s // SSD_CHUNK

    def body(x_ref, dt_ref, b_ref, a_ref, d_ref, y_ref, hist_ref, h_ref):
        @pl.when(pl.program_id(0) == 0)
        def _():
            h_ref[...] = jnp.zeros_like(h_ref)

        hprev = h_ref[...]
        hist_ref[...] = hprev
        ys, hs = _ssd_chunk(x_ref[...], dt_ref[...], hprev, b_ref[...], a_ref[...], d_ref[...])
        for p in range(N_PAIRS):
            y_ref[:, p * LANES:(p + 1) * LANES] = ys[p]
            h_ref[p * SSD_STATE:(p + 1) * SSD_STATE, :] = hs[p]

    return pl.pallas_call(
        body, grid=(nc,),
        in_specs=[pl.BlockSpec((SSD_CHUNK, CONV_DIM), lambda i: (i, 0)),
                  pl.BlockSpec((SSD_CHUNK, 1024), lambda i: (i, P_DT // 1024)),
                  _pspec((1, 1024)), _pspec((1, 1024)), _pspec((1, 1024))],
        out_specs=[pl.BlockSpec((SSD_CHUNK, 1024), lambda i: (i, 0)), pl.BlockSpec((STATE_ROWS, LANES), lambda i: (i, 0))],
        out_shape=[jax.ShapeDtypeStruct((s, 1024), F32), jax.ShapeDtypeStruct((nc * STATE_ROWS, LANES), F32)],
        scratch_shapes=[pltpu.VMEM((STATE_ROWS, LANES), F32)],
        compiler_params=_cp("arbitrary"), name="ssd_fwd")(xbc, proj, dtb, alog, dsk)


def _ssd_bwd(xbc, proj, hist, dtb, alog, dsk, dy):
    s = xbc.shape[0]
    nc = s // SSD_CHUNK

    def body(x_ref, dt_ref, hist_ref, b_ref, a_ref, d_ref, dy_ref, dx_ref, ddt_ref, db_ref, da_ref, dd_ref, dh_ref):
        first = pl.program_id(0) == 0

        @pl.when(first)
        def _():
            dh_ref[...] = jnp.zeros_like(dh_ref)

        _, vjp = jax.vjp(_ssd_chunk, x_ref[...], dt_ref[...], hist_ref[...], b_ref[...], a_ref[...], d_ref[...])
        dys = tuple(dy_ref[:, p * LANES:(p + 1) * LANES] for p in range(N_PAIRS))
        dhs = tuple(dh_ref[p * SSD_STATE:(p + 1) * SSD_STATE, :] for p in range(N_PAIRS))
        dx, ddt, dhp, db, da, dd = vjp((dys, dhs))
        dx_ref[...] = dx
        ddt_ref[...] = ddt.astype(ddt_ref.dtype)
        dh_ref[...] = dhp
        for o, v in ((db_ref, db), (da_ref, da), (dd_ref, dd)):
            @pl.when(first)
            def _(o=o, v=v):
                o[...] = v

            @pl.when(jnp.logical_not(first))
            def _(o=o, v=v):
                o[...] += v

    rev = lambda i: (nc - 1 - i, 0)
    return pl.pallas_call(
        body, grid=(nc,),
        in_specs=[pl.BlockSpec((SSD_CHUNK, CONV_DIM), rev),
                  pl.BlockSpec((SSD_CHUNK, 1024), lambda i: (nc - 1 - i, P_DT // 1024)),
                  pl.BlockSpec((STATE_ROWS, LANES), rev),
                  _pspec((1, 1024)), _pspec((1, 1024)), _pspec((1, 1024)),
                  pl.BlockSpec((SSD_CHUNK, 1024), rev)],
        out_specs=[pl.BlockSpec((SSD_CHUNK, CONV_DIM), rev), pl.BlockSpec((SSD_CHUNK, 1024), rev),
                   _pspec((1, 1024)), _pspec((1, 1024)), _pspec((1, 1024))],
        out_shape=[jax.ShapeDtypeStruct((s, CONV_DIM), F32), jax.ShapeDtypeStruct((s, 1024), BF16)]
        + [jax.ShapeDtypeStruct((1, 1024), F32)] * 3,
        scratch_shapes=[pltpu.VMEM((STATE_ROWS, LANES), F32)],
        compiler_params=_cp("arbitrary"), name="ssd_bwd")(xbc, proj, hist, dtb, alog, dsk, dy)


def _attn_head(q, kp, kc, vp, vc, has_prev, slope):
    scale = ATTN_HEAD_DIM ** -0.5
    n = ATTN_STEPS
    qi = lax.broadcasted_iota(jnp.int32, (n, n), 0)
    kj = lax.broadcasted_iota(jnp.int32, (n, n), 1)
    sp = jnp.where((kj >= qi) & has_prev, _bdot(q, kp, "nt") * scale - slope * (qi + n - kj).astype(F32), NEG)
    sc = jnp.where(kj <= qi, _bdot(q, kc, "nt") * scale - slope * (qi - kj).astype(F32), NEG)
    m = lax.stop_gradient(jnp.maximum(jnp.max(sp, axis=1, keepdims=True), jnp.max(sc, axis=1, keepdims=True)))
    pp, pc = jnp.exp(sp - m), jnp.exp(sc - m)
    den = jnp.sum(pp, axis=1, keepdims=True) + jnp.sum(pc, axis=1, keepdims=True)
    o = (_bdot(pp, vp, "nn") + _bdot(pc, vc, "nn")) / den
    return o, jnp.broadcast_to(m + jnp.log(den), (n, ATTN_HEAD_DIM))


def _head_slope(gi, h):
    s = [float(v) * DILATIONS[gi] for v in SLOPES[gi]]
    return jnp.where(h == 0, s[0], jnp.where(h == 1, s[1], jnp.where(h == 2, s[2], s[3])))


ATTN_UNROLL = 4


def _attn_heads_per_block(d):
    return 4 if d == 1 else 1


def _units(ref, d, hb):
    if d == 1:
        return [ref[:, h * ATTN_HEAD_DIM:(h + 1) * ATTN_HEAD_DIM] for h in range(hb)]
    return [ref[pl.ds(r, ATTN_STEPS, stride=d), :] for r in range(d)]


def _store_units(ref, src, d, hb):
    if d == 1:
        for h in range(hb):
            ref[:, h * ATTN_HEAD_DIM:(h + 1) * ATTN_HEAD_DIM] = src[h]
    else:
        for r in range(d):
            ref[pl.ds(r, ATTN_STEPS, stride=d), :] = src[r]


def _attn_fwd(qn, kn, vv, gi):
    d = DILATIONS[gi]
    s = qn.shape[0]
    span = ATTN_STEPS * d
    nb = s // span

    hb, units = _attn_heads_per_block(d), _attn_heads_per_block(d) * d

    def body(q_ref, k_ref, v_ref, o_ref, l_ref, sq, sk, sv, so, sl):
        h0, b = pl.program_id(0) * hb, pl.program_id(1)
        cur, prev = b % 2, (b + 1) % 2

        @pl.when(b == 0)
        def _():
            sk[prev] = jnp.zeros(sk.shape[1:], F32)
            sv[prev] = jnp.zeros(sv.shape[1:], F32)

        for u, (qr, kr, vr) in enumerate(zip(_units(q_ref, d, hb), _units(k_ref, d, hb), _units(v_ref, d, hb))):
            sq[u] = qr
            sk[cur, u] = kr
            sv[cur, u] = vr

        def step(i, carry):
            for j in range(ATTN_UNROLL):
                u = i * ATTN_UNROLL + j
                so[u], sl[u] = _attn_head(sq[u], sk[prev, u], sk[cur, u], sv[prev, u], sv[cur, u], b > 0,
                                          _head_slope(gi, h0 + u // d))
            return carry

        lax.fori_loop(0, units // ATTN_UNROLL, step, 0)
        _store_units(o_ref, so, d, hb)
        _store_units(l_ref, sl, d, hb)

    blk = pl.BlockSpec((span, hb * ATTN_HEAD_DIM), lambda h, b: (b, (gi * 4) // hb + h))
    out = pl.BlockSpec((span, hb * ATTN_HEAD_DIM), lambda h, b: (b, h))
    res = (units, ATTN_STEPS, ATTN_HEAD_DIM)
    return pl.pallas_call(
        body, grid=(4 // hb, nb), in_specs=[blk, blk, blk], out_specs=[out, out],
        out_shape=[jax.ShapeDtypeStruct((s, ATTN_GROUP_W), F32)] * 2,
        scratch_shapes=[pltpu.VMEM(res, F32), pltpu.VMEM((2,) + res, F32), pltpu.VMEM((2,) + res, F32),
                        pltpu.VMEM(res, F32), pltpu.VMEM(res, F32)],
        compiler_params=_cp("parallel", "arbitrary"), name=f"attn_fwd_g{gi}")(qn, kn, vv)


def _attn_bwd(qn, kn, vv, do, dl, gi):
    d = DILATIONS[gi]
    s = qn.shape[0]
    span = ATTN_STEPS * d
    nb = s // span

    hb, units = _attn_heads_per_block(d), _attn_heads_per_block(d) * d

    def body(q_ref, kp_ref, kc_ref, vp_ref, vc_ref, do_ref, dl_ref, dq_ref, dk_ref, dv_ref, sin, sout, ck, cv):
        h0, bi = pl.program_id(0) * hb, pl.program_id(1)

        @pl.when(bi == 0)
        def _():
            ck[...] = jnp.zeros_like(ck)
            cv[...] = jnp.zeros_like(cv)

        for i, ref in enumerate((q_ref, kp_ref, kc_ref, vp_ref, vc_ref, do_ref, dl_ref)):
            for u, val in enumerate(_units(ref, d, hb)):
                sin[i, u] = val
        has_prev = bi < nb - 1

        def step(i, carry):
            for j in range(ATTN_UNROLL):
                u = i * ATTN_UNROLL + j
                f = functools.partial(_attn_head, has_prev=has_prev, slope=_head_slope(gi, h0 + u // d))
                _, vjp = jax.vjp(f, sin[0, u], sin[1, u], sin[2, u], sin[3, u], sin[4, u])
                dq, dkp, dkc, dvp, dvc = vjp((sin[5, u], sin[6, u]))
                sout[0, u] = dq
                sout[1, u] = dkc + ck[u]
                sout[2, u] = dvc + cv[u]
                ck[u] = dkp
                cv[u] = dvp
            return carry

        lax.fori_loop(0, units // ATTN_UNROLL, step, 0)
        for i, ref in enumerate((dq_ref, dk_ref, dv_ref)):
            _store_units(ref, sout.at[i], d, hb)

    w = hb * ATTN_HEAD_DIM
    cur = pl.BlockSpec((span, w), lambda h, b: (nb - 1 - b, (gi * 4) // hb + h))
    prev = pl.BlockSpec((span, w), lambda h, b: (jnp.maximum(nb - 2 - b, 0), (gi * 4) // hb + h))
    out = pl.BlockSpec((span, w), lambda h, b: (nb - 1 - b, h))
    res = (units, ATTN_STEPS, ATTN_HEAD_DIM)
    return pl.pallas_call(
        body, grid=(4 // hb, nb), in_specs=[cur, prev, cur, prev, cur, out, out], out_specs=[out, out, out],
        out_shape=[jax.ShapeDtypeStruct((s, ATTN_GROUP_W), F32)] * 3,
        scratch_shapes=[pltpu.VMEM((7,) + res, F32), pltpu.VMEM((3,) + res, F32), pltpu.VMEM(res, F32), pltpu.VMEM(res, F32)],
        compiler_params=_cp("parallel", "arbitrary"), name=f"attn_bwd_g{gi}")(qn, kn, kn, vv, vv, do, dl)


def _layer_fwd(x, mod, W, l):
    sh1, sc1, g1, sh2, sc2, g2 = (mod[i:i + 1] for i in range(6))
    (h,) = _rowwise_fwd("norm1", _f_norm, [(x, 1024, 0)], [W["norm1_w"], sc1, sh1], [(1024, BF16)])
    proj = _matmul("in_proj", h, W["w_in"], "nn")
    xbc = _conv_fwd(proj, W["conv_w"], W["conv_b"])
    y, hist = _ssd_fwd(xbc, proj, W["dt_bias"], W["a_log"], W["d_skip"])
    (yn,) = _rowwise_fwd("ssd_gate", _f_ssdgate, [(y, 1024, 0), (proj, 1024, P_Z // 1024)], [W["ssd_norm_w"]], [(1024, BF16)])
    y_ssd = _matmul("ssd_out", yn, W["w_ssd_out"], "nn", layer=l)
    qn, kn, vv = _rowwise_fwd("qk_norm", _f_qknorm, [(proj, 1536, P_Q // 1536), (proj, 1536, P_K // 1536), (proj, 1536, P_V // 1536)],
                              [W["q_norm_w"], W["k_norm_w"]], [(1536, F32)] * 3)
    ol = [_attn_fwd(qn, kn, vv, gi) for gi in range(3)]
    (o,) = _rowwise_fwd("attn_combine", _f_combine, [(t[0], 512, 0) for t in ol] + [(t[1], 512, 0) for t in ol], [], [(512, BF16)])
    y_attn = _matmul("attn_out", o, W["w_attn_out"], "nn", layer=l, chips=256)
    pooled = _pool_fwd(proj)
    pm = _group_matmul("pool_mix", pooled, W["w_pool_mix"], "nn", layer=l)
    (ps,) = _rowwise_fwd("pool_scale", _f_poolscale, [(pm, 1024, 0)], [W["pool_scale"]], [(1024, BF16)])
    y_pool = _matmul("pool_out", ps, W["w_pool_out"], "nn", layer=l)
    (merged,) = _rowwise_fwd("merge", _f_merge, [(proj, 3072, P_GATES // 3072), (y_ssd, 1024, 0), (y_attn, 1024, 0), (y_pool, 1024, 0)],
                             [], [(1024, BF16)])
    mo = _matmul("mix_out", merged, W["w_out"], "nn", layer=l)
    (x1,) = _rowwise_fwd("resid1", _f_resid, [(x, 1024, 0), (mo, 1024, 0)], [g1], [(1024, F32)])
    (h2,) = _rowwise_fwd("norm2", _f_norm, [(x1, 1024, 0)], [W["norm2_w"], sc2, sh2], [(1024, BF16)])
    a = _matmul("ff1", h2, W["w_ff1"], "nn", layer=l, chips=1024)
    (r,) = _rowwise_fwd("relu2", _f_relu2, [(a, D_FF, 0)], [], [(D_FF, BF16)], tb=128)
    ff = _matmul("ff2", r, W["w_ff2"], "nn", layer=l)
    (x2,) = _rowwise_fwd("resid2", _f_resid, [(x1, 1024, 0), (ff, 1024, 0)], [g2], [(1024, F32)])
    saved = dict(x=x, h=h, proj=proj, xbc=xbc, y=y, hist=hist, yn=yn, y_ssd=y_ssd, qn=qn, kn=kn, vv=vv, ol=ol, o=o,
                 y_attn=y_attn, pooled=pooled, pm=pm, ps=ps, y_pool=y_pool, merged=merged, mo=mo, x1=x1, h2=h2, a=a, r=r, ff=ff)
    return x2, saved


def _layer_bwd(dx2, mod, W, sv, l, bufs):
    sh1, sc1, g1, sh2, sc2, g2 = (mod[i:i + 1] for i in range(6))
    g = {}
    dx1a, dff, dg2 = _rowwise_bwd("resid2_bwd", _f_resid, [(sv["x1"], 1024, 0), (sv["ff"], 1024, 0)], [g2], [(dx2, 1024, 0)],
                                  [True, True], gdt=[F32, BF16])
    g["w_ff2"] = _matmul("ff2_dw", sv["r"], dff, "tn", BF16, into=(bufs["w_ff2"], l))
    dr = _matmul("ff2_dx", dff, W["w_ff2"], "nt", layer=l)
    (da,) = _rowwise_bwd("relu2_bwd", _f_relu2, [(sv["a"], D_FF, 0)], [], [(dr, D_FF, 0)], [True], tb=128, gdt=[BF16])
    g["w_ff1"] = _matmul("ff1_dw", sv["h2"], da, "tn", BF16, out_chips=1024, into=(bufs["w_ff1"], l))
    dh2 = _matmul("ff1_dx", da, W["w_ff1"], "nt", layer=l, chips=1024)
    dx1, g["norm2_w"], dsc2, dsh2 = _rowwise_bwd("norm2_bwd", _f_norm, [(sv["x1"], 1024, 0)], [W["norm2_w"], sc2, sh2],
                                                 [(dh2, 1024, 0)], [True], add=(0, dx1a))
    dxa, dmo, dg1 = _rowwise_bwd("resid1_bwd", _f_resid, [(sv["x"], 1024, 0), (sv["mo"], 1024, 0)], [g1], [(dx1, 1024, 0)],
                                 [True, True], gdt=[F32, BF16])
    g["w_out"] = _matmul("mix_out_dw", sv["merged"], dmo, "tn", BF16, into=(bufs["w_out"], l))
    dmerged = _matmul("mix_out_dx", dmo, W["w_out"], "nt", layer=l)
    proj = sv["proj"]
    dgates, dy_ssd, dy_attn, dy_pool = _rowwise_bwd(
        "merge_bwd", _f_merge, [(proj, 3072, P_GATES // 3072), (sv["y_ssd"], 1024, 0), (sv["y_attn"], 1024, 0), (sv["y_pool"], 1024, 0)],
        [], [(dmerged, 1024, 0)], [True] * 4, gdt=[BF16] * 4)
    g["w_pool_out"] = _matmul("pool_out_dw", sv["ps"], dy_pool, "tn", BF16, into=(bufs["w_pool_out"], l))
    dps = _matmul("pool_out_dx", dy_pool, W["w_pool_out"], "nt", layer=l)
    dpm, g["pool_scale"] = _rowwise_bwd("pool_scale_bwd", _f_poolscale, [(sv["pm"], 1024, 0)], [W["pool_scale"]], [(dps, 1024, 0)],
                                        [True], gdt=[BF16])
    dmix = _group_matmul("pool_mix_dw", sv["pooled"], dpm, "tn")
    g["w_pool_mix"] = bufs["w_pool_mix"].at[:, l].set(
        jnp.moveaxis(dmix.reshape(4, N_CHIPS, POOL_GW // N_CHIPS, POOL_GW), 1, 0).astype(BF16))
    dpooled = _group_matmul("pool_mix_dx", dpm, W["w_pool_mix"], "nt", layer=l)
    du = _pool_bwd(dpooled)
    g["w_attn_out"] = _matmul("attn_out_dw", sv["o"], dy_attn, "tn", BF16, out_chips=256, into=(bufs["w_attn_out"], l))
    do = _matmul("attn_out_dx", dy_attn, W["w_attn_out"], "nt", layer=l, chips=256)
    ol = sv["ol"]
    dol = _rowwise_bwd("attn_combine_bwd", _f_combine, [(t[0], 512, 0) for t in ol] + [(t[1], 512, 0) for t in ol], [],
                       [(do, 512, 0)], [True] * 6)
    dqs, dks, dvs = zip(*[_attn_bwd(sv["qn"], sv["kn"], sv["vv"], dol[gi], dol[3 + gi], gi) for gi in range(3)])
    dqn, dkn, dvv = (jnp.concatenate(t, axis=1) for t in (dqs, dks, dvs))
    dq, dk, dv, g["q_norm_w"], g["k_norm_w"] = _rowwise_bwd(
        "qk_norm_bwd", _f_qknorm, [(proj, 1536, P_Q // 1536), (proj, 1536, P_K // 1536), (proj, 1536, P_V // 1536)],
        [W["q_norm_w"], W["k_norm_w"]], [(dqn, 1536, 0), (dkn, 1536, 0), (dvv, 1536, 0)], [True] * 3, gdt=[BF16] * 3)
    g["w_ssd_out"] = _matmul("ssd_out_dw", sv["yn"], dy_ssd, "tn", BF16, into=(bufs["w_ssd_out"], l))
    dyn = _matmul("ssd_out_dx", dy_ssd, W["w_ssd_out"], "nt", layer=l)
    dy, dz, g["ssd_norm_w"] = _rowwise_bwd("ssd_gate_bwd", _f_ssdgate, [(sv["y"], 1024, 0), (proj, 1024, P_Z // 1024)], [W["ssd_norm_w"]],
                                           [(dyn, 1024, 0)], [True, True], gdt=[F32, BF16])
    dxbc, ddt, g["dt_bias"], g["a_log"], g["d_skip"] = _ssd_bwd(sv["xbc"], proj, sv["hist"], W["dt_bias"], W["a_log"], W["d_skip"], dy)
    dxbc_raw, g["conv_w"], g["conv_b"] = _conv_bwd(proj, W["conv_w"], W["conv_b"], dxbc)
    dproj = jnp.concatenate([dxbc_raw, dq, dk, dv, dgates, dz, du, ddt], axis=1)
    g["w_in"] = _matmul("in_proj_dw", sv["h"], dproj, "tn", BF16)
    dh = _matmul("in_proj_dx", dproj, W["w_in"], "nt")
    dx, g["norm1_w"], dsc1, dsh1 = _rowwise_bwd("norm1_bwd", _f_norm, [(sv["x"], 1024, 0)], [W["norm1_w"], sc1, sh1],
                                                [(dh, 1024, 0)], [True], add=(0, dxa))
    dmod = jnp.concatenate([dsh1, dsc1, dg1, dsh2, dsc2, dg2], axis=0)
    return dx, dmod, g


def _expand_heads(t):
    return jnp.repeat(t, SSD_HEAD_DIM, axis=-1)


def _reduce_heads(t):
    return t.reshape(t.shape[:-1] + (SSD_HEADS, SSD_HEAD_DIM)).sum(-1)


_IN_SPLITS = np.cumsum((0,) + IN_SIZES)


def _w_in_to_layout(w):
    z, xbc, dt, q, k, v, u, gates = (w[:, _IN_SPLITS[i]:_IN_SPLITS[i + 1]] for i in range(8))
    return jnp.concatenate([xbc, q, k, v, gates, z, u, _expand_heads(dt)], axis=1)


def _w_in_from_layout(g):
    xbc, q, k, v = (g[:, o:o + 1536] for o in (P_XBC, P_Q, P_K, P_V))
    gates, z, u, dt = g[:, P_GATES:P_GATES + 3072], g[:, P_Z:P_Z + 1024], g[:, P_U:P_U + 1024], g[:, P_DT:P_DT + 1024]
    return jnp.concatenate([z, xbc, _reduce_heads(dt.astype(F32)).astype(g.dtype), q, k, v, u, gates], axis=1)


_STACKED = ("w_ssd_out", "w_attn_out", "w_pool_mix", "w_pool_out", "w_out", "w_ff1", "w_ff2")
_ROWS = ("norm1_w", "norm2_w", "conv_b", "ssd_norm_w", "q_norm_w", "k_norm_w", "pool_scale")
_HEAD_ROWS = ("dt_bias", "a_log", "d_skip")


def _layer_weights(wg, lg, small, l):
    W = {k: wg[k] for k in _STACKED}
    w_in = jnp.concatenate([wg["w_in"][lg, j] for j in range(N_CHIPS)], axis=1)
    W["w_in"] = _w_in_to_layout(w_in)
    W["conv_w"] = small["conv_w"][l]
    for k in _ROWS:
        W[k] = small[k][l][None, :]
    for k in _HEAD_ROWS:
        W[k] = _expand_heads(small[k][l])[None, :]
    return W


def _layer_grads_by_chip(g, w_in_buf, l):
    out = dict(g)
    w_in = _w_in_from_layout(g["w_in"])
    out["w_in"] = w_in_buf.at[:, l].set(jnp.moveaxis(w_in.reshape(D_MODEL, N_CHIPS, IN_WIDTH // N_CHIPS), 1, 0))
    for k in _ROWS:
        out[k] = g[k][0]
    for k in _HEAD_ROWS:
        out[k] = _reduce_heads(g[k][0])
    return out


ANY = pl.BlockSpec(memory_space=pl.ANY)


def _place():
    x, y, c = lax.axis_index("x"), lax.axis_index("y"), lax.axis_index("c")
    return x, y, c, (x, y, 1 - c), [(1 - x, y), (x, 1 - y), (1 - x, 1 - y)]


def _allgather8(name, blk):
    m_per, n = blk.shape

    def body(x_ref, out_ref, send_sems, recv_sems, local_sem):
        x, y, c, sibling, chips = _place()
        me = (x, y, c)

        def rows(px, py, pc):
            return out_ref.at[pl.ds((4 * px + 2 * py + pc) * m_per, m_per), :]

        def copy(k, block, to, src=None):
            return pltpu.make_async_remote_copy(
                src_ref=rows(*block) if src is None else src, dst_ref=rows(*block),
                send_sem=send_sems.at[k], recv_sem=recv_sems.at[k], device_id=to, device_id_type=MESH)

        mine = pltpu.make_async_copy(x_ref, rows(*me), local_sem)
        mine.start()
        first = [copy(0, me, sibling, src=x_ref)]
        first += [copy(1 + j, me, (*chip, c), src=x_ref) for j, chip in enumerate(chips)]
        for cp in first:
            cp.start()
        passed = [copy(4 + j, (*chip, c), sibling) for j, chip in enumerate(chips)]
        for j, chip in enumerate(chips):
            copy(1 + j, (*chip, c), me).wait_recv()
            passed[j].start()
        copy(0, sibling, me).wait_recv()
        for j, chip in enumerate(chips):
            copy(4 + j, (*chip, 1 - c), me).wait_recv()
        for cp in first + passed:
            cp.wait_send()
        mine.wait()

    return pl.pallas_call(
        body, out_shape=jax.ShapeDtypeStruct((N_DEV * m_per, n), blk.dtype),
        in_specs=[pl.BlockSpec(memory_space=pltpu.VMEM)], out_specs=pl.BlockSpec(memory_space=pltpu.VMEM),
        scratch_shapes=[pltpu.SemaphoreType.DMA((7,)), pltpu.SemaphoreType.DMA((7,)), pltpu.SemaphoreType.DMA],
        name=name)(blk)


HBM_SPEC = pl.BlockSpec(memory_space=pltpu.HBM)
SEM_SPEC = pl.BlockSpec(memory_space=pltpu.SEMAPHORE)
SIDE_EFFECT = pltpu.SideEffectType.DATAFLOW_SIDE_EFFECTING


def _dma_sems(n):
    return [pltpu.SemaphoreType.DMA((n,)), pltpu.SemaphoreType.DMA((n,))]


def _halves(ref, axis, c):
    r2 = ref.shape[axis] // 2
    lead = (slice(None),) * axis
    return ref.at[lead + (pl.ds(r2 * c, r2),)], ref.at[lead + (pl.ds(r2 * (1 - c), r2),)]


def _gather_copies(srcs, lands, send_sems, recv_sems):
    x, y, c, _, chips = _place()
    sends, lands_here = [], []
    for j, (cx, cy) in enumerate(chips):
        for i, (s, t) in enumerate(zip(srcs, lands)):
            k = 3 * i + j
            mine = _halves(t.at[:, 2 * x + y], 1, c)[0]
            theirs = _halves(t.at[:, 2 * cx + cy], 1, c)[0]
            sends.append(pltpu.make_async_remote_copy(src_ref=_halves(s, 1, c)[0], dst_ref=mine, send_sem=send_sems.at[k],
                                                      recv_sem=recv_sems.at[k], device_id=(cx, cy, c), device_id_type=MESH))
            lands_here.append(pltpu.make_async_remote_copy(src_ref=theirs, dst_ref=theirs, send_sem=send_sems.at[k],
                                                           recv_sem=recv_sems.at[k], device_id=(cx, cy, c), device_id_type=MESH))
    return sends, lands_here


def _exchange_copies(srcs, lands, send_sems, recv_sems):
    x, y, c, _, chips = _place()
    sends = [pltpu.make_async_remote_copy(src_ref=s.at[2 * cx + cy], dst_ref=t.at[j], send_sem=send_sems.at[3 * i + j],
                                          recv_sem=recv_sems.at[3 * i + j], device_id=(cx, cy, c), device_id_type=MESH)
             for j, (cx, cy) in enumerate(chips) for i, (s, t) in enumerate(zip(srcs, lands))]
    return sends, sends


def _split_start(name, copies, srcs, lands, after):
    ns, nl = len(srcs), len(lands)
    n_copies = 3 * ns

    def body(*refs):
        send_sems, recv_sems = refs[ns + nl + 1], refs[ns + nl + 2]
        for cp in copies(refs[:ns], refs[ns:ns + nl], send_sems, recv_sems)[0]:
            cp.start()
        refs[-1][...] = jnp.zeros_like(refs[-1])

    arrs = list(srcs) + list(lands)
    res = pl.pallas_call(
        body, name=name,
        out_shape=(pltpu.SemaphoreType.DMA((n_copies,)), pltpu.SemaphoreType.DMA((n_copies,)))
        + tuple(pltpu.HBM(a.shape, a.dtype) for a in arrs) + (jax.ShapeDtypeStruct((8, LANES), F32),),
        in_specs=[HBM_SPEC] * (ns + nl) + [ANY],
        out_specs=(SEM_SPEC, SEM_SPEC) + (HBM_SPEC,) * (ns + nl) + (pl.BlockSpec(memory_space=pltpu.VMEM),),
        input_output_aliases={i: 2 + i for i in range(ns + nl)},
        compiler_params=pltpu.CompilerParams(has_side_effects=SIDE_EFFECT),
    )(*[pltpu.with_memory_space_constraint(a, pltpu.HBM) for a in arrs], after)
    return res[0], res[1], list(res[2:2 + ns]), list(res[2 + ns:2 + ns + nl]), res[-1]


def _split_wait(name, copies, send_sems, recv_sems, srcs, lands, after):
    ns, nl = len(srcs), len(lands)

    def body(*refs):
        sends, lands_here = copies(refs[:ns], refs[ns:ns + nl], refs[ns + nl], refs[ns + nl + 1])
        for cp in sends:
            cp.wait_send()
        for cp in lands_here:
            cp.wait_recv()

    arrs = list(srcs) + list(lands)
    res = pl.pallas_call(
        body, name=name, out_shape=tuple(pltpu.HBM(a.shape, a.dtype) for a in arrs),
        in_specs=[HBM_SPEC] * (ns + nl) + [SEM_SPEC, SEM_SPEC, ANY], out_specs=(HBM_SPEC,) * (ns + nl),
        input_output_aliases={i: i for i in range(ns + nl)},
        compiler_params=pltpu.CompilerParams(has_side_effects=SIDE_EFFECT),
    )(*arrs, send_sems, recv_sems, after)
    return list(res[:ns]), list(res[ns:])


def _gather_forward(name, lands):
    n = len(lands)

    def body(*refs):
        ins, outs = refs[:n], refs[n:2 * n]
        send_sems, recv_sems = refs[2 * n:]
        x, y, c, sibling, chips = _place()
        sends, arrivals = [], []
        for j, (cx, cy) in enumerate(chips):
            for i in range(n):
                k = 3 * i + j
                src = _halves(ins[i].at[:, 2 * cx + cy], 1, c)[0]
                dst, theirs = _halves(outs[i].at[:, 2 * cx + cy], 1, c)
                sends.append(pltpu.make_async_remote_copy(src_ref=src, dst_ref=dst, send_sem=send_sems.at[k], recv_sem=recv_sems.at[k],
                                                          device_id=sibling, device_id_type=MESH))
                arrivals.append(pltpu.make_async_remote_copy(src_ref=theirs, dst_ref=theirs, send_sem=send_sems.at[k],
                                                             recv_sem=recv_sems.at[k], device_id=sibling, device_id_type=MESH))
        for cp in sends:
            cp.start()
        for cp in arrivals:
            cp.wait_recv()
        for cp in sends:
            cp.wait_send()

    return pl.pallas_call(
        body, out_shape=[jax.ShapeDtypeStruct(t.shape, t.dtype) for t in lands], in_specs=[ANY] * n, out_specs=[ANY] * n,
        input_output_aliases={i: i for i in range(n)}, scratch_shapes=_dma_sems(3 * n), name=name)(*lands)


def _swap_halves(name, gs):
    n = len(gs)

    def body(*refs):
        ins, got = refs[:n], refs[n:2 * n]
        send_sems, recv_sems = refs[2 * n:]
        x, y, c, sibling, _ = _place()
        sends = [pltpu.make_async_remote_copy(src_ref=_halves(ins[i], 2, c)[1], dst_ref=got[i], send_sem=send_sems.at[i],
                                              recv_sem=recv_sems.at[i], device_id=sibling, device_id_type=MESH) for i in range(n)]
        for cp in sends:
            cp.start()
        for cp in sends:
            cp.wait_recv()
        for cp in sends:
            cp.wait_send()

    return pl.pallas_call(
        body, out_shape=[jax.ShapeDtypeStruct(t.shape[:2] + (t.shape[2] // 2,) + t.shape[3:], t.dtype) for t in gs],
        in_specs=[ANY] * n, out_specs=[ANY] * n, scratch_shapes=_dma_sems(n), name=name)(*gs)


def _share_halves(name, ts):
    n = len(ts)

    def body(*refs):
        ins, outs = refs[:n], refs[n:2 * n]
        send_sems, recv_sems = refs[2 * n:]
        x, y, c, sibling, _ = _place()
        sends, arrivals = [], []
        for i in range(n):
            mine, theirs = _halves(outs[i], 1, c)
            sends.append(pltpu.make_async_remote_copy(src_ref=ins[i], dst_ref=mine, send_sem=send_sems.at[i], recv_sem=recv_sems.at[i],
                                                      device_id=sibling, device_id_type=MESH))
            arrivals.append(pltpu.make_async_remote_copy(src_ref=ins[i], dst_ref=theirs, send_sem=send_sems.at[i],
                                                         recv_sem=recv_sems.at[i], device_id=sibling, device_id_type=MESH))
        for cp in sends:
            cp.start()
        for cp in arrivals:
            cp.wait_recv()
        for cp in sends:
            cp.wait_send()

    return pl.pallas_call(
        body, out_shape=[jax.ShapeDtypeStruct((t.shape[0], 2 * t.shape[1]) + t.shape[2:], t.dtype) for t in ts],
        in_specs=[ANY] * n, out_specs=[ANY] * n, scratch_shapes=_dma_sems(n), name=name)(*ts)


PACK_W = 1024
PACK_TB = 512


def _sum_rows(name, parts, out_dtype):
    def f(*vals):
        acc = vals[0]
        for v in vals[1:]:
            acc = acc + v
        return (acc,)

    return _rowwise_fwd(name, f, [(p, PACK_W, 0) for p in parts], [], [(PACK_W, out_dtype)], tb=_tile(parts[0].shape[0], PACK_TB))[0]


def _sum_slots(name, ops, rows, out_dtype):
    cw = ops[0][0].shape[1]
    tb = rows
    while tb * cw > 300_000 and tb % 32 == 0:
        tb //= 2
    per = rows // tb

    def body(*refs):
        acc = refs[0][...].astype(F32)
        for r in refs[1:-1]:
            acc = acc + r[...].astype(F32)
        refs[-1][...] = acc.astype(refs[-1].dtype)

    return pl.pallas_call(
        body, grid=(per,), in_specs=[pl.BlockSpec((tb, cw), lambda i, s=s: (s * per + i, 0)) for _, s in ops],
        out_specs=pl.BlockSpec((tb, cw), lambda i: (i, 0)), out_shape=jax.ShapeDtypeStruct((rows, cw), out_dtype),
        compiler_params=_cp("parallel"), name=name)(*[a for a, _ in ops])


def _adamw(name, w, g, m, v, row0=0, unit=None, prev=None, dep=None):
    r, cw = w.shape
    tb = unit or r
    while tb * cw > 300_000 and tb % 16 == 0:
        tb //= 2
    off = row0 // tb
    c1 = 1.0 / (1.0 - ADAM_B1 ** ADAM_STEP)
    c2 = 1.0 / (1.0 - ADAM_B2 ** ADAM_STEP)

    def body(w_ref, g_ref, m_ref, v_ref, *rest):
        go_ref, d_ref, mo_ref, vo_ref = rest[-4:]
        gg = g_ref[...]
        mn = ADAM_B1 * m_ref[...] + (1.0 - ADAM_B1) * gg
        vn = ADAM_B2 * v_ref[...] + (1.0 - ADAM_B2) * jnp.square(gg)
        go_ref[...] = gg
        d_ref[...] = -ADAM_LR * ((mn * c1) / (jnp.sqrt(vn * c2) + ADAM_EPS) + ADAM_WD * w_ref[...])
        mo_ref[...] = mn
        vo_ref[...] = vn

    full = pl.BlockSpec((tb, cw), lambda i: (off + i, 0))
    extra = ([] if prev is None else list(prev)) + ([] if dep is None else [dep])
    n_prev = 0 if prev is None else 4
    return pl.pallas_call(
        body, grid=(g.shape[0] // tb,),
        in_specs=[full, pl.BlockSpec((tb, cw), lambda i: (i, 0)), full, full] + [pl.BlockSpec(memory_space=pl.ANY)] * len(extra),
        out_specs=[full] * 4, out_shape=[jax.ShapeDtypeStruct((r, cw), F32)] * 4,
        input_output_aliases={4 + i: i for i in range(n_prev)},
        compiler_params=_cp("parallel"), name=name)(w, g, m, v, *extra)


def _silu_rows(c):
    def body(c_ref, o_ref):
        rows = lax.broadcasted_iota(jnp.int32, o_ref.shape, 0)
        o_ref[...] = jnp.where(rows == 0, jnp.broadcast_to(_silu(c_ref[...]), o_ref.shape), 0.0)

    return pl.pallas_call(body, out_shape=jax.ShapeDtypeStruct((8, c.shape[1]), F32), name="cond_silu")(c)


_KINDS = ("w_in", "w_ssd_out", "w_attn_out", "w_pool_mix", "w_pool_out", "w_out", "w_ff1", "w_ff2")
_SMALL = ("b_ada", "norm1_w", "norm2_w", "conv_b", "dt_bias", "a_log", "d_skip", "ssd_norm_w", "q_norm_w", "k_norm_w",
          "pool_scale")
_ORDER = ("w_ada", "b_ada", "norm1_w", "norm2_w", "w_in", "conv_w", "conv_b", "dt_bias", "a_log", "d_skip", "ssd_norm_w",
          "w_ssd_out", "q_norm_w", "k_norm_w", "w_attn_out", "w_pool_mix", "pool_scale", "w_pool_out", "w_out", "w_ff1", "w_ff2")


def _pack_flat(arrs, rows, dtype):
    flat = jnp.concatenate([a.reshape(-1).astype(dtype) for a in arrs])
    return jnp.pad(flat, (0, rows * PACK_W - flat.shape[0])).reshape(rows, PACK_W)


def _unpack_flat(buf, shapes):
    flat = buf.reshape(-1)
    out, off = [], 0
    for shp in shapes:
        n = int(np.prod(shp))
        out.append(flat[off:off + n].reshape(shp))
        off += n
    return out


def _small_rows(n_elems):
    return -(-n_elems // (8 * PACK_W)) * 8


def kernel(x, c, w_ada, b_ada, norm1_w, norm2_w, w_in, conv_w, conv_b, dt_bias, a_log, d_skip, ssd_norm_w, w_ssd_out, q_norm_w, k_norm_w, w_attn_out, w_pool_mix, pool_scale, w_pool_out, w_out, w_ff1, w_ff2, loss_target, m_w_ada, m_b_ada, m_norm1_w, m_norm2_w, m_w_in, m_conv_w, m_conv_b, m_dt_bias, m_a_log, m_d_skip, m_ssd_norm_w, m_w_ssd_out, m_q_norm_w, m_k_norm_w, m_w_attn_out, m_w_pool_mix, m_pool_scale, m_w_pool_out, m_w_out, m_w_ff1, m_w_ff2, v_w_ada, v_b_ada, v_norm1_w, v_norm2_w, v_w_in, v_conv_w, v_conv_b, v_dt_bias, v_a_log, v_d_skip, v_ssd_norm_w, v_w_ssd_out, v_q_norm_w, v_k_norm_w, v_w_attn_out, v_w_pool_mix, v_pool_scale, v_w_pool_out, v_w_out, v_w_ff1, v_w_ff2):
    w = dict(w_ada=w_ada, b_ada=b_ada, norm1_w=norm1_w, norm2_w=norm2_w, w_in=w_in, conv_w=conv_w, conv_b=conv_b, dt_bias=dt_bias, a_log=a_log, d_skip=d_skip, ssd_norm_w=ssd_norm_w, w_ssd_out=w_ssd_out, q_norm_w=q_norm_w, k_norm_w=k_norm_w, w_attn_out=w_attn_out, w_pool_mix=w_pool_mix, pool_scale=pool_scale, w_pool_out=w_pool_out, w_out=w_out, w_ff1=w_ff1, w_ff2=w_ff2)
    m = dict(w_ada=m_w_ada, b_ada=m_b_ada, norm1_w=m_norm1_w, norm2_w=m_norm2_w, w_in=m_w_in, conv_w=m_conv_w, conv_b=m_conv_b, dt_bias=m_dt_bias, a_log=m_a_log, d_skip=m_d_skip, ssd_norm_w=m_ssd_norm_w, w_ssd_out=m_w_ssd_out, q_norm_w=m_q_norm_w, k_norm_w=m_k_norm_w, w_attn_out=m_w_attn_out, w_pool_mix=m_w_pool_mix, pool_scale=m_pool_scale, w_pool_out=m_w_pool_out, w_out=m_w_out, w_ff1=m_w_ff1, w_ff2=m_w_ff2)
    v = dict(w_ada=v_w_ada, b_ada=v_b_ada, norm1_w=v_norm1_w, norm2_w=v_norm2_w, w_in=v_w_in, conv_w=v_conv_w, conv_b=v_conv_b, dt_bias=v_dt_bias, a_log=v_a_log, d_skip=v_d_skip, ssd_norm_w=v_ssd_norm_w, w_ssd_out=v_w_ssd_out, q_norm_w=v_q_norm_w, k_norm_w=v_k_norm_w, w_attn_out=v_w_attn_out, w_pool_mix=v_w_pool_mix, pool_scale=v_pool_scale, w_pool_out=v_w_pool_out, w_out=v_w_out, w_ff1=v_w_ff1, w_ff2=v_w_ff2)
    chip = 2 * lax.axis_index("x") + lax.axis_index("y")
    dev = 2 * chip + lax.axis_index("c")
    ada_cols = w_ada.shape[2]

    def start_gather(tag, lo, n, after):
        shards = [w[k][lo:lo + n].astype(BF16) for k in _KINDS]
        lands = [lax.empty((n, N_CHIPS) + s.shape[1:], BF16) for s in shards]
        return _split_start("gather_start_" + tag, _gather_copies, shards, lands, after)

    gather_a = start_gather("a", 0, 1, c)
    c = c + gather_a[4][0, 0]

    n_conv = conv_w.size // PACK_W
    rows1 = _small_rows((1 + n_conv) * PACK_W)
    blk = jnp.concatenate([_silu_rows(c)[:1], conv_w.reshape(n_conv, PACK_W), jnp.zeros((rows1 - 1 - n_conv, PACK_W), F32)])
    first = _allgather8("gather_cond", blk).reshape(N_DEV, rows1, PACK_W)
    cond_all = first[:, 0]
    conv_all = first[0::2, 1:1 + n_conv].reshape((N_CHIPS,) + conv_w.shape)
    conv_full = jnp.moveaxis(conv_all, 0, 2).reshape(DEPTH, SSD_CONV, CONV_DIM)
    b_cols = lax.dynamic_slice_in_dim(b_ada, chip * ada_cols, ada_cols, axis=1)
    mod_cols = jnp.stack([_matmul("ada_fwd", cond_all, w_ada[l], "nn", precise=True) + b_cols[l][None, :] for l in range(DEPTH)])
    mod_all = _allgather8("gather_mod", mod_cols.reshape(-1, PACK_W)).reshape(N_DEV, DEPTH, N_DEV, ada_cols)
    mine = lax.dynamic_index_in_dim(mod_all[0::2], dev, axis=2, keepdims=False)
    mods = jnp.moveaxis(mine, 0, 1).reshape(DEPTH, 6, D_MODEL)

    core = lax.axis_index("c")
    small_w = dict({k: w[k] for k in _SMALL[1:]}, conv_w=conv_full)

    def finish_gather(tag, handle, after):
        shards, lands = _split_wait("gather_wait_" + tag, _gather_copies, handle[0], handle[1], handle[2], handle[3], after)
        lands = _gather_forward("gather_forward_" + tag, lands)
        wg = {k: lax.dynamic_update_slice_in_dim(t, s[:, None], chip, axis=1) for k, t, s in zip(_KINDS, lands, shards)}
        n = shards[0].shape[0]
        for k in ("w_ssd_out", "w_pool_out", "w_out", "w_ff2"):
            wg[k] = wg[k].reshape(n, -1, D_MODEL)
        wg["w_pool_mix"] = jnp.moveaxis(wg["w_pool_mix"], 1, 2).reshape(n, 4, POOL_GW, POOL_GW)
        return wg

    wg_a = finish_gather("a", gather_a, mods)
    gather_b = start_gather("b", 1, DEPTH - 1, wg_a["w_in"])
    mods = mods + gather_b[4][0, 0]

    xc = x[0]
    Ws, saved = [None] * DEPTH, [None] * DEPTH
    Ws[0] = _layer_weights(wg_a, 0, small_w, 0)
    xc, saved[0] = _layer_fwd(xc, mods[0], Ws[0], 0)
    wg_b = finish_gather("b", gather_b, xc)
    for l in range(1, DEPTH):
        Ws[l] = _layer_weights(wg_b, l - 1, small_w, l)
        xc, saved[l] = _layer_fwd(xc, mods[l], Ws[l], l - 1)
    dx, loss = _loss_and_grad(xc, loss_target[0])
    dmods, grads = [None] * DEPTH, [None] * DEPTH

    def backward(l, lg, dx, mod, bufs):
        dx, dmods[l], g = _layer_bwd(dx, mod, Ws[l], saved[l], lg, bufs)
        grads[l] = _layer_grads_by_chip(g, bufs["w_in"], lg)
        return dx, {k: grads[l][k] for k in _KINDS}

    bufs_b = {k: lax.empty((N_CHIPS, DEPTH - 1) + w[k].shape[1:], BF16) for k in _KINDS}
    for l in reversed(range(1, DEPTH)):
        dx, bufs_b = backward(l, l - 1, dx, mods[l], bufs_b)

    def flat(t):
        return t.reshape(-1, t.shape[-1])

    def start_exchange(tag, bufs, after):
        gs = [bufs[k] for k in _KINDS]
        got = _swap_halves("grad_swap_" + tag, gs)
        own = [lax.dynamic_slice_in_dim(t, (t.shape[2] // 2) * core, t.shape[2] // 2, axis=2) for t in gs]
        pairs = [_sum_slots(f"sum_pair_{tag}_{k}", [(flat(a), 0), (flat(b), 0)], flat(a).shape[0], BF16).reshape(a.shape)
                 for k, a, b in zip(_KINDS, own, got)]
        lands = [lax.empty((3,) + p.shape[1:], BF16) for p in pairs]
        return _split_start("exchange_start_" + tag, _exchange_copies, pairs, lands, after)

    def finish_exchange(tag, handle, after):
        pairs, partials = _split_wait("exchange_wait_" + tag, _exchange_copies, handle[0], handle[1], handle[2], handle[3], after)
        mine = [lax.dynamic_index_in_dim(p, chip, axis=0, keepdims=False) for p in pairs]
        totals = [_sum_slots(f"sum_chips_{tag}_{k}", [(flat(a), 0)] + [(flat(p), s) for s in range(3)], flat(a).shape[0], F32).reshape(a.shape)
                  for k, a, p in zip(_KINDS, mine, partials)]
        return [lax.dynamic_update_slice_in_dim(t, mine_t, mine_t.shape[1] * core, axis=1)
                for t, mine_t in zip(_share_halves("grad_share_" + tag, totals), totals)]

    def adamw_group(tag, lo, gs, prev, dep):
        out = {}
        for k, gk in zip(_KINDS, gs):
            shp = w[k].shape
            unit = int(np.prod(shp[1:-1]))
            two_d = lambda t: t.reshape(-1, shp[-1])
            out[k] = _adamw(f"adamw_{tag}_{k}", two_d(w[k]), two_d(gk), two_d(m[k]), two_d(v[k]), row0=lo * unit, unit=unit,
                            prev=None if prev is None else prev[k], dep=dep)
            dep = out[k][1]
        return out

    exchange_b = start_exchange("b", bufs_b, dx)
    bufs_a = {k: lax.empty((N_CHIPS, 1) + w[k].shape[1:], BF16) for k in _KINDS}
    dx, bufs_a = backward(0, 0, dx, mods[0] + exchange_b[4][0, 0], bufs_a)
    g_b = finish_exchange("b", exchange_b, dx)
    grad_x, dmods = dx, jnp.stack(dmods)
    exchange_a = start_exchange("a", bufs_a, g_b[0])
    adam_b = adamw_group("b", 1, g_b, None, exchange_a[4])

    small = ([dmods] + [jnp.stack([grads[l][k] for l in range(DEPTH)]) for k in _SMALL[1:] + ("conv_w",)] + [loss[:, :1]])
    n_small = sum(int(np.prod(a.shape)) for a in small)
    rows_small = _small_rows(n_small)
    small_all = _allgather8("gather_small", _pack_flat(small, rows_small, F32))
    parts = [small_all[d * rows_small:(d + 1) * rows_small] for d in range(N_DEV)]
    small_sum = _unpack_flat(_sum_rows("sum_small", parts, F32), [a.shape for a in small])
    g_out = {"b_ada": small_sum[0].reshape(DEPTH, 6 * D_MODEL)}
    for k, t in zip(_SMALL[1:], small_sum[1:-2]):
        g_out[k] = t
    g_out["conv_w"] = lax.dynamic_slice_in_dim(small_sum[-2], chip * conv_w.shape[2], conv_w.shape[2], axis=2)
    loss_out = small_sum[-1][0, 0]
    dmod_all = jnp.stack([p[:DEPTH * 6].reshape(DEPTH, 6 * D_MODEL) for p in parts])
    dmod_cols = lax.dynamic_slice_in_dim(dmod_all, chip * ada_cols, ada_cols, axis=2)
    g_out["w_ada"] = jnp.stack([_matmul("ada_dw", cond_all, dmod_cols[:, l], "tn", precise=True) for l in range(DEPTH)])

    deltas, new_m, new_v = {}, {}, {}
    dep = adam_b[_KINDS[-1]][1]
    for k in ("w_ada", "conv_w"):
        shp = w[k].shape
        two_d = (int(np.prod(shp[:-1])), shp[-1])
        res = _adamw("adamw_" + k, *(t.reshape(two_d) for t in (w[k], g_out[k], m[k], v[k])), dep=dep)
        deltas[k], new_m[k], new_v[k] = (t.reshape(shp) for t in res[1:])
        dep = res[1]
    small_shapes = [w[k].shape for k in _SMALL]
    n_sm = sum(int(np.prod(s)) for s in small_shapes)
    res = _adamw("adamw_small", *[_pack_flat([t[k] for k in _SMALL], _small_rows(n_sm), F32) for t in (w, g_out, m, v)], dep=dep)[1:]
    for name_map, buf in zip((deltas, new_m, new_v), res):
        for k, t in zip(_SMALL, _unpack_flat(buf, small_shapes)):
            name_map[k] = t
    g_a = finish_exchange("a", exchange_a, res[0])
    for k, t in adamw_group("a", 0, g_a, adam_b, None).items():
        g_out[k], deltas[k], new_m[k], new_v[k] = (u.reshape(w[k].shape) for u in t)

    return (loss_out, grad_x[None], *[g_out[k] for k in _ORDER], *[deltas[k] for k in _ORDER],
            *[new_m[k] for k in _ORDER], *[new_v[k] for k in _ORDER])
```

```python
import functools
import math

import numpy as np
import jax
import jax.numpy as jnp
from jax import lax
from jax.experimental import pallas as pl
from jax.experimental.pallas import tpu as pltpu

F32, BF16 = jnp.float32, jnp.bfloat16
MESH = pl.DeviceIdType.MESH

D_MODEL = 1024
DEPTH = 4
N_CHIPS = 4
N_DEV = 8
SSD_HEADS = 16
SSD_HEAD_DIM = 64
SSD_STATE = 128
SSD_CHUNK = 128
SSD_CONV = 4
CONV_DIM = 1536
ATTN_HEAD_DIM = 128
ATTN_GROUP_W = 512
DILATIONS = (1, 4, 16)
ATTN_STEPS = 128
POOL_WINDOWS = (2, 4, 8, 16)
POOL_GW = 256
D_FF = 4096
EPS = 1e-6
IN_SIZES = (1024, 1536, 16, 1536, 1536, 1536, 1024, 3072)
IN_WIDTH = sum(IN_SIZES)
P_XBC, P_Q, P_K, P_V, P_GATES, P_Z, P_U, P_DT = 0, 1536, 3072, 4608, 6144, 9216, 10240, 11264
P_WIDTH = 12288
LANES = 128
NEG = -1e30
VMEM_LIMIT = 56 * 1024 * 1024

ADAM_LR, ADAM_B1, ADAM_B2, ADAM_EPS, ADAM_WD, ADAM_STEP = 0.001, 0.9, 0.999, 1e-08, 0.01, 10


def _alibi_slopes(n):
    def pow2(k):
        start = 2.0 ** (-8.0 / k)
        return [start ** (i + 1) for i in range(k)]
    if math.log2(n).is_integer():
        s = pow2(n)
    else:
        c = 2 ** math.floor(math.log2(n))
        s = pow2(c) + pow2(2 * c)[0::2][: n - c]
    return np.sort(np.asarray(s, np.float32))[::-1].copy()


SLOPES = _alibi_slopes(12).reshape(3, 4)


def _cp(*sem):
    return pltpu.CompilerParams(dimension_semantics=sem, vmem_limit_bytes=VMEM_LIMIT)


_DIMS = {"nn": (((1,), (0,)), ((), ())), "nt": (((1,), (1,)), ((), ())), "tn": (((0,), (0,)), ((), ()))}


def _dot(a, b, mode):
    return lax.dot_general(a.astype(BF16), b.astype(BF16), _DIMS[mode], preferred_element_type=F32)


@functools.partial(jax.custom_vjp, nondiff_argnums=(2,))
def _bdot(a, b, mode):
    return _dot(a, b, mode)


def _bdot_fwd(a, b, mode):
    return _dot(a, b, mode), (a, b)


def _bdot_bwd(mode, res, ct):
    a, b = res
    if mode == "nn":
        return _dot(ct, b, "nt"), _dot(a, ct, "tn")
    if mode == "nt":
        return _dot(ct, b, "nn"), _dot(ct, a, "tn")
    return _dot(b, ct, "nt"), _dot(a, ct, "nn")


_bdot.defvjp(_bdot_fwd, _bdot_bwd)


def _hdot(a, b):
    return jnp.dot(a, b, precision=lax.Precision.HIGHEST, preferred_element_type=F32)


def _tri(n, lower):
    r = lax.broadcasted_iota(jnp.int32, (n, n), 0)
    c = lax.broadcasted_iota(jnp.int32, (n, n), 1)
    return (r >= c if lower else r <= c).astype(F32)


@jax.custom_vjp
def _csum(a):
    return _hdot(_tri(a.shape[0], True), a)


def _csum_fwd(a):
    return _csum(a), None


def _csum_bwd(_, ct):
    return (_hdot(_tri(ct.shape[0], False), ct),)


_csum.defvjp(_csum_fwd, _csum_bwd)


def _softplus(x):
    return jnp.maximum(x, 0.0) + jnp.log(1.0 + jnp.exp(-jnp.abs(x)))


def _sigmoid(x):
    return 1.0 / (1.0 + jnp.exp(-x))


def _silu(x):
    return x * _sigmoid(x)


def _tile(n, cap):
    t = min(n, cap)
    while n % t:
        t //= 2
    return t


MM_TILE, MM_KTILE = 1024, 2048


def _matmul(name, a, b, mode, out_dtype=F32, precise=False, layer=None, chips=0, out_chips=0, into=None):
    if mode == "nn":
        (m, k), n = a.shape, (4 * chips if chips else b.shape[-1])
    elif mode == "nt":
        (m, k), n = a.shape, b.shape[-2]
    else:
        (k, m), n = a.shape, b.shape[-1]
    tm = _tile(m // N_CHIPS if (into is not None and not out_chips) else m, MM_TILE)
    tn = _tile(chips if (chips and mode == "nn") else (out_chips or n), MM_TILE)
    tk = _tile(chips if (chips and mode == "nt") else k, MM_KTILE)
    nk = k // tk
    a_spec = pl.BlockSpec((tk, tm), lambda i, j, l: (l, i)) if mode == "tn" else pl.BlockSpec((tm, tk), lambda i, j, l: (i, l))
    if chips:
        if mode == "nn":
            per = chips // tn
            b_spec = pl.BlockSpec((None, None, tk, tn), lambda i, j, l: (layer, j // per, l, j % per))
        else:
            per = chips // tk
            b_spec = pl.BlockSpec((None, None, tn, tk), lambda i, j, l: (layer, l // per, j, l % per))
    elif layer is not None:
        b_spec = (pl.BlockSpec((None, tn, tk), lambda i, j, l: (layer, j, l)) if mode == "nt"
                  else pl.BlockSpec((None, tk, tn), lambda i, j, l: (layer, l, j)))
    else:
        b_spec = pl.BlockSpec((tn, tk), lambda i, j, l: (j, l)) if mode == "nt" else pl.BlockSpec((tk, tn), lambda i, j, l: (l, j))
    if into is not None:
        buf, slot = into
        if out_chips:
            per_o = out_chips // tn
            o_spec = pl.BlockSpec((None, None, tm, tn), lambda i, j, l: (j // per_o, slot, i, j % per_o))
        else:
            per_r = m // N_CHIPS // tm
            o_spec = pl.BlockSpec((None, None, tm, tn), lambda i, j, l: (i // per_r, slot, i % per_r, j))
        o_shape = jax.ShapeDtypeStruct(buf.shape, buf.dtype)
    elif out_chips:
        per_o = out_chips // tn
        o_spec = pl.BlockSpec((None, tm, tn), lambda i, j, l: (j // per_o, i, j % per_o))
        o_shape = jax.ShapeDtypeStruct((N_CHIPS, m, out_chips), out_dtype)
    else:
        o_spec = pl.BlockSpec((tm, tn), lambda i, j, l: (i, j))
        o_shape = jax.ShapeDtypeStruct((m, n), out_dtype)

    def part(a_ref, b_ref):
        if precise:
            return lax.dot_general(a_ref[...], b_ref[...], _DIMS[mode], precision=lax.Precision.HIGHEST,
                                   preferred_element_type=F32)
        return _dot(a_ref[...], b_ref[...], mode)

    n_in = 2 if into is None else 3

    if nk == 1:
        def body(*refs):
            o_ref = refs[n_in]
            o_ref[...] = part(refs[0], refs[1]).astype(o_ref.dtype)
        scratch = []
    else:
        def body(*refs):
            o_ref, acc_ref = refs[n_in], refs[n_in + 1]
            l = pl.program_id(2)
            p = part(refs[0], refs[1])

            @pl.when(l == 0)
            def _():
                acc_ref[...] = p

            @pl.when((l > 0) & (l < nk - 1))
            def _():
                acc_ref[...] += p

            @pl.when(l == nk - 1)
            def _():
                o_ref[...] = (acc_ref[...] + p).astype(o_ref.dtype)
        scratch = [pltpu.VMEM((tm, tn), F32)]

    extra = {} if into is None else dict(input_output_aliases={2: 0})
    return pl.pallas_call(
        body, grid=(m // tm, n // tn, nk), in_specs=[a_spec, b_spec] + ([] if into is None else [pl.BlockSpec(memory_space=pl.ANY)]),
        out_specs=o_spec, out_shape=o_shape, scratch_shapes=scratch, compiler_params=_cp("parallel", "parallel", "arbitrary"),
        name=name, **extra)(*((a, b) if into is None else (a, b, into[0])))


def _group_matmul(name, a, w, mode, out_dtype=F32, layer=0):
    s = a.shape[0]
    tb = 512
    gw = POOL_GW
    if mode == "tn":
        def body(a_ref, b_ref, o_ref):
            part = _dot(a_ref[...], b_ref[...], "tn")

            @pl.when(pl.program_id(1) == 0)
            def _():
                o_ref[0] = part

            @pl.when(pl.program_id(1) > 0)
            def _():
                o_ref[0] += part

        return pl.pallas_call(
            body, grid=(4, s // tb),
            in_specs=[pl.BlockSpec((tb, gw), lambda g, i: (i, g)), pl.BlockSpec((tb, gw), lambda g, i: (i, g))],
            out_specs=pl.BlockSpec((1, gw, gw), lambda g, i: (g, 0, 0)),
            out_shape=jax.ShapeDtypeStruct((4, gw, gw), F32),
            compiler_params=_cp("parallel", "arbitrary"), name=name)(a, w)

    def body(a_ref, w_ref, o_ref):
        o_ref[...] = _dot(a_ref[...], w_ref[...], mode).astype(o_ref.dtype)

    return pl.pallas_call(
        body, grid=(s // tb, 4),
        in_specs=[pl.BlockSpec((tb, gw), lambda i, g: (i, g)), pl.BlockSpec((None, None, gw, gw), lambda i, g: (layer, g, 0, 0))],
        out_specs=pl.BlockSpec((tb, gw), lambda i, g: (i, g)),
        out_shape=jax.ShapeDtypeStruct((s, 4 * gw), out_dtype),
        compiler_params=_cp("parallel", "parallel"), name=name)(a, w)


def _rspec(tb, width, cb):
    return pl.BlockSpec((tb, width), lambda i: (i, cb))


def _pspec(shape):
    return pl.BlockSpec(shape, lambda i: (0, 0))


def _rowwise_fwd(name, f, rows, pars, outs, tb=256):
    s = rows[0][0].shape[0]
    nin = len(rows) + len(pars)

    def body(*refs):
        res = f(*[r[...].astype(F32) for r in refs[:nin]])
        for o, v in zip(refs[nin:], res):
            o[...] = v.astype(o.dtype)

    return pl.pallas_call(
        body, grid=(s // tb,),
        in_specs=[_rspec(tb, w, cb) for _, w, cb in rows] + [_pspec(p.shape) for p in pars],
        out_specs=[_rspec(tb, w, 0) for w, _ in outs],
        out_shape=[jax.ShapeDtypeStruct((s, w), dt) for w, dt in outs],
        compiler_params=_cp("parallel"), name=name)(*[r[0] for r in rows], *pars)


def _rowwise_bwd(name, f, rows, pars, cts, need, add=None, tb=256, gdt=None):
    s = rows[0][0].shape[0]
    nr, npar, nc = len(rows), len(pars), len(cts)
    nin = nr + npar + nc + (1 if add is not None else 0)

    def body(*refs):
        ins = [r[...].astype(F32) for r in refs[:nr + npar]]
        _, vjp = jax.vjp(f, *ins)
        g = vjp(tuple(c[...].astype(F32) for c in refs[nr + npar:nr + npar + nc]))
        outs = refs[nin:]
        k = 0
        for j in range(nr):
            if need[j]:
                v = g[j]
                if add is not None and add[0] == j:
                    v = v + refs[nin - 1][...]
                outs[k][...] = v.astype(outs[k].dtype)
                k += 1
        first = pl.program_id(0) == 0
        for j in range(npar):
            o, v = outs[k + j], g[nr + j]

            @pl.when(first)
            def _(o=o, v=v):
                o[...] = v

            @pl.when(jnp.logical_not(first))
            def _(o=o, v=v):
                o[...] += v

    in_specs = ([_rspec(tb, w, cb) for _, w, cb in rows] + [_pspec(p.shape) for p in pars]
                + [_rspec(tb, w, cb) for _, w, cb in cts])
    args = [r[0] for r in rows] + list(pars) + [c[0] for c in cts]
    if add is not None:
        in_specs.append(_rspec(tb, rows[add[0]][1], 0))
        args.append(add[1])
    gr = [(w, F32) for (_, w, _), nd in zip(rows, need) if nd]
    if gdt is not None:
        gr = [(w, dt) for (w, _), dt in zip(gr, gdt)]
    return pl.pallas_call(
        body, grid=(s // tb,), in_specs=in_specs,
        out_specs=[_rspec(tb, w, 0) for w, _ in gr] + [_pspec(p.shape) for p in pars],
        out_shape=[jax.ShapeDtypeStruct((s, w), dt) for w, dt in gr] + [jax.ShapeDtypeStruct(p.shape, F32) for p in pars],
        compiler_params=_cp("arbitrary"), name=name)(*args)


def _f_norm(x, nw, sc, sh):
    r = lax.rsqrt(jnp.mean(x * x, axis=-1, keepdims=True) + EPS)
    return ((x * r * nw) * (1.0 + sc) + sh,)


def _f_ssdgate(y, z, w):
    y2 = y * _silu(z)
    low = lax.broadcasted_iota(jnp.int32, y2.shape, 1) < 512
    sq = y2 * y2
    m0 = jnp.sum(jnp.where(low, sq, 0.0), axis=-1, keepdims=True) / 512.0
    m1 = jnp.sum(jnp.where(low, 0.0, sq), axis=-1, keepdims=True) / 512.0
    r = jnp.where(low, lax.rsqrt(m0 + EPS), lax.rsqrt(m1 + EPS))
    return (y2 * r * w,)


def _head_rms(t, w):
    outs = []
    for h in range(t.shape[1] // ATTN_HEAD_DIM):
        th = t[:, h * ATTN_HEAD_DIM:(h + 1) * ATTN_HEAD_DIM]
        outs.append(th * lax.rsqrt(jnp.mean(th * th, axis=-1, keepdims=True) + EPS) * w)
    return jnp.concatenate(outs, axis=1)


def _f_qknorm(q, k, v, qw, kw):
    return _head_rms(q, qw), _head_rms(k, kw), v


def _f_combine(o1, o2, o3, l1, l2, l3):
    m = lax.stop_gradient(jnp.maximum(jnp.maximum(l1, l2), l3))
    e1, e2, e3 = jnp.exp(l1 - m), jnp.exp(l2 - m), jnp.exp(l3 - m)
    return ((e1 * o1 + e2 * o2 + e3 * o3) / (e1 + e2 + e3),)


def _f_poolscale(pm, ps):
    return (pm * ps,)


def _f_merge(gates, ys, ya, yp):
    g = _sigmoid(gates)
    return (g[:, 0:1024] * ys + g[:, 1024:2048] * ya + g[:, 2048:3072] * yp,)


def _f_resid(x, o, g):
    return (x + g * o,)


def _f_relu2(a):
    return (jnp.square(jnp.maximum(a, 0.0)),)


def _loss_and_grad(y, tgt, tb=512):
    s, d = y.shape

    def body(y_ref, t_ref, dy_ref, l_ref):
        e = y_ref[...] - t_ref[...]
        dy_ref[...] = e * (1.0 / d)
        part = jnp.zeros((1, LANES), F32) + jnp.sum(e * e) * (0.5 / d)

        @pl.when(pl.program_id(0) == 0)
        def _():
            l_ref[...] = part

        @pl.when(pl.program_id(0) > 0)
        def _():
            l_ref[...] += part

    return pl.pallas_call(
        body, grid=(s // tb,), in_specs=[_rspec(tb, d, 0), _rspec(tb, d, 0)],
        out_specs=[_rspec(tb, d, 0), _pspec((1, LANES))],
        out_shape=[jax.ShapeDtypeStruct((s, d), F32), jax.ShapeDtypeStruct((1, LANES), F32)],
        compiler_params=_cp("arbitrary"), name="loss")(y, tgt)


def _shift_down(x, j):
    rows = lax.broadcasted_iota(jnp.int32, x.shape, 0)
    return jnp.where(rows < j, 0.0, pltpu.roll(x, j, 0))


def _shift_up(x, j):
    s = x.shape[0]
    rows = lax.broadcasted_iota(jnp.int32, x.shape, 0)
    return jnp.where(rows >= s - j, 0.0, pltpu.roll(x, s - j, 0))


CONV_CB = 256


def _conv_pre(x, w_ref, b_ref):
    acc = b_ref[...] + w_ref[SSD_CONV - 1:SSD_CONV, :] * x
    for j in range(1, SSD_CONV):
        acc = acc + w_ref[SSD_CONV - 1 - j:SSD_CONV - j, :] * _shift_down(x, j)
    return acc


def _conv_fwd(proj, cw, cb):
    s = proj.shape[0]

    def body(x_ref, w_ref, b_ref, o_ref):
        o_ref[...] = _silu(_conv_pre(x_ref[...], w_ref, b_ref))

    return pl.pallas_call(
        body, grid=(CONV_DIM // CONV_CB,),
        in_specs=[pl.BlockSpec((s, CONV_CB), lambda i: (0, P_XBC // CONV_CB + i)),
                  pl.BlockSpec((SSD_CONV, CONV_CB), lambda i: (0, i)), pl.BlockSpec((1, CONV_CB), lambda i: (0, i))],
        out_specs=pl.BlockSpec((s, CONV_CB), lambda i: (0, i)),
        out_shape=jax.ShapeDtypeStruct((s, CONV_DIM), F32), compiler_params=_cp("parallel"), name="conv_fwd")(proj, cw, cb)


def _conv_bwd(proj, cw, cb, dout):
    s = proj.shape[0]

    def body(x_ref, w_ref, b_ref, d_ref, dx_ref, dw_ref, db_ref):
        x = x_ref[...]
        a = _conv_pre(x, w_ref, b_ref)
        sg = _sigmoid(a)
        da = d_ref[...] * (sg + a * sg * (1.0 - sg))
        db_ref[...] = jnp.sum(da, axis=0, keepdims=True)
        dx = w_ref[SSD_CONV - 1:SSD_CONV, :] * da
        dw_ref[SSD_CONV - 1:SSD_CONV, :] = jnp.sum(da * x, axis=0, keepdims=True)
        for j in range(1, SSD_CONV):
            dx = dx + w_ref[SSD_CONV - 1 - j:SSD_CONV - j, :] * _shift_up(da, j)
            dw_ref[SSD_CONV - 1 - j:SSD_CONV - j, :] = jnp.sum(da * _shift_down(x, j), axis=0, keepdims=True)
        dx_ref[...] = dx.astype(dx_ref.dtype)

    return pl.pallas_call(
        body, grid=(CONV_DIM // CONV_CB,),
        in_specs=[pl.BlockSpec((s, CONV_CB), lambda i: (0, P_XBC // CONV_CB + i)),
                  pl.BlockSpec((SSD_CONV, CONV_CB), lambda i: (0, i)), pl.BlockSpec((1, CONV_CB), lambda i: (0, i)),
                  pl.BlockSpec((s, CONV_CB), lambda i: (0, i))],
        out_specs=[pl.BlockSpec((s, CONV_CB), lambda i: (0, i)), pl.BlockSpec((SSD_CONV, CONV_CB), lambda i: (0, i)),
                   pl.BlockSpec((1, CONV_CB), lambda i: (0, i))],
        out_shape=[jax.ShapeDtypeStruct((s, CONV_DIM), BF16), jax.ShapeDtypeStruct((SSD_CONV, CONV_DIM), F32),
                   jax.ShapeDtypeStruct((1, CONV_DIM), F32)],
        compiler_params=_cp("parallel"), name="conv_bwd")(proj, cw, cb, dout)


def _pool_window_sum(x, g, shift):
    s2 = x + shift(x, 1)
    s4 = s2 + shift(s2, 2)
    s8 = s4 + shift(s4, 4)
    s16 = s8 + shift(s8, 8)
    return jnp.where(g == 0, s2, jnp.where(g == 1, s4, jnp.where(g == 2, s8, s16)))


def _pool_count(shape, g):
    rows = lax.broadcasted_iota(jnp.int32, shape, 0)
    return jnp.minimum(rows + 1, jnp.left_shift(2, g)).astype(F32)


def _pool_fwd(proj):
    s = proj.shape[0]

    def body(u_ref, o_ref):
        g = pl.program_id(0)
        u = u_ref[...]
        o_ref[...] = (_pool_window_sum(u, g, _shift_down) / _pool_count(u.shape, g) - u).astype(o_ref.dtype)

    return pl.pallas_call(
        body, grid=(4,), in_specs=[pl.BlockSpec((s, POOL_GW), lambda g: (0, P_U // POOL_GW + g))],
        out_specs=pl.BlockSpec((s, POOL_GW), lambda g: (0, g)),
        out_shape=jax.ShapeDtypeStruct((s, 4 * POOL_GW), BF16), compiler_params=_cp("parallel"), name="pool_fwd")(proj)


def _pool_bwd(dp):
    s = dp.shape[0]

    def body(d_ref, o_ref):
        g = pl.program_id(0)
        d = d_ref[...]
        o_ref[...] = (_pool_window_sum(d / _pool_count(d.shape, g), g, _shift_up) - d).astype(o_ref.dtype)

    return pl.pallas_call(
        body, grid=(4,), in_specs=[pl.BlockSpec((s, POOL_GW), lambda g: (0, g))],
        out_specs=pl.BlockSpec((s, POOL_GW), lambda g: (0, g)),
        out_shape=jax.ShapeDtypeStruct((s, 4 * POOL_GW), BF16), compiler_params=_cp("parallel"), name="pool_bwd")(dp)


N_PAIRS = SSD_HEADS // 2
STATE_ROWS = N_PAIRS * SSD_STATE


def _ssd_chunk(xbc, dtr, hprev, dtb, alog, dsk):
    L = xbc.shape[0]
    xs, bm, cm = xbc[:, 0:1024], xbc[:, 1024:1280], xbc[:, 1280:1536]
    dt = _softplus(dtr + dtb)
    a = dt * (-jnp.exp(alog))
    acum = _csum(a)
    alast = jnp.sum(a, axis=0, keepdims=True)
    xdt = xs * dt
    xdecay = xdt * jnp.exp(alast - acum)
    eacum = jnp.exp(acum)
    elast = jnp.exp(alast)
    cb = [_bdot(cm[:, g * 128:(g + 1) * 128], bm[:, g * 128:(g + 1) * 128], "nt") for g in range(2)]
    rows = lax.broadcasted_iota(jnp.int32, (L, L), 0)
    cols = lax.broadcasted_iota(jnp.int32, (L, L), 1)
    causal = rows >= cols
    lane = lax.broadcasted_iota(jnp.int32, (L, LANES), 1)
    sub = lax.broadcasted_iota(jnp.int32, (LANES, L), 0)
    ys, hs = [], []
    for p in range(N_PAIRS):
        g = p // (N_PAIRS // 2)
        sl = slice(p * LANES, (p + 1) * LANES)
        ac = acum[:, sl]
        act = ac.T
        xp = xdt[:, sl]
        hp = hprev[p * SSD_STATE:(p + 1) * SSD_STATE, :]
        y = _bdot(cm[:, g * 128:(g + 1) * 128], hp, "nn") * eacum[:, sl] + dsk[:, sl] * xs[:, sl]
        for half in range(2):
            l0 = half * SSD_HEAD_DIM
            col = jnp.sum(jnp.where(lane == l0, ac, 0.0), axis=1, keepdims=True)
            row = jnp.sum(jnp.where(sub == l0, act, 0.0), axis=0, keepdims=True)
            decay = jnp.exp(jnp.where(causal, col - row, NEG))
            xh = jnp.where((lane >= l0) & (lane < l0 + SSD_HEAD_DIM), xp, 0.0)
            y = y + _bdot(cb[g] * decay, xh, "nn")
        ys.append(y)
        hs.append(elast[:, sl] * hp + _bdot(bm[:, g * 128:(g + 1) * 128], xdecay[:, sl], "tn"))
    return tuple(ys), tuple(hs)


def _ssd_fwd(xbc, proj, dtb, alog, dsk):
    s = xbc.shape[0]
    nc = s // SSD_CHUNK

    def body(x_ref, dt_ref, b_ref, a_ref, d_ref, y_ref, hist_ref, h_ref):
        @pl.when(pl.program_id(0) == 0)
        def _():
            h_ref[...] = jnp.zeros_like(h_ref)

        hprev = h_ref[...]
        hist_ref[...] = hprev
        ys, hs = _ssd_chunk(x_ref[...], dt_ref[...], hprev, b_ref[...], a_ref[...], d_ref[...])
        for p in range(N_PAIRS):
            y_ref[:, p * LANES:(p + 1) * LANES] = ys[p]
            h_ref[p * SSD_STATE:(p + 1) * SSD_STATE, :] = hs[p]

    return pl.pallas_call(
        body, grid=(nc,),
        in_specs=[pl.BlockSpec((SSD_CHUNK, CONV_DIM), lambda i: (i, 0)),
                  pl.BlockSpec((SSD_CHUNK, 1024), lambda i: (i, P_DT // 1024)),
                  _pspec((1, 1024)), _pspec((1, 1024)), _pspec((1, 1024))],
        out_specs=[pl.BlockSpec((SSD_CHUNK, 1024), lambda i: (i, 0)), pl.BlockSpec((STATE_ROWS, LANES), lambda i: (i, 0))],
        out_shape=[jax.ShapeDtypeStruct((s, 1024), F32), jax.ShapeDtypeStruct((nc * STATE_ROWS, LANES), F32)],
        scratch_shapes=[pltpu.VMEM((STATE_ROWS, LANES), F32)],
        compiler_params=_cp("arbitrary"), name="ssd_fwd")(xbc, proj, dtb, alog, dsk)


def _ssd_bwd(xbc, proj, hist, dtb, alog, dsk, dy):
    s = xbc.shape[0]
    nc = s // SSD_CHUNK

    def body(x_ref, dt_ref, hist_ref, b_ref, a_ref, d_ref, dy_ref, dx_ref, ddt_ref, db_ref, da_ref, dd_ref, dh_ref):
        first = pl.program_id(0) == 0

        @pl.when(first)
        def _():
            dh_ref[...] = jnp.zeros_like(dh_ref)

        _, vjp = jax.vjp(_ssd_chunk, x_ref[...], dt_ref[...], hist_ref[...], b_ref[...], a_ref[...], d_ref[...])
        dys = tuple(dy_ref[:, p * LANES:(p + 1) * LANES] for p in range(N_PAIRS))
        dhs = tuple(dh_ref[p * SSD_STATE:(p + 1) * SSD_STATE, :] for p in range(N_PAIRS))
        dx, ddt, dhp, db, da, dd = vjp((dys, dhs))
        dx_ref[...] = dx
        ddt_ref[...] = ddt.astype(ddt_ref.dtype)
        dh_ref[...] = dhp
        for o, v in ((db_ref, db), (da_ref, da), (dd_ref, dd)):
            @pl.when(first)
            def _(o=o, v=v):
                o[...] = v

            @pl.when(jnp.logical_not(first))
            def _(o=o, v=v):
                o[...] += v

    rev = lambda i: (nc - 1 - i, 0)
    return pl.pallas_call(
        body, grid=(nc,),
        in_specs=[pl.BlockSpec((SSD_CHUNK, CONV_DIM), rev),
                  pl.BlockSpec((SSD_CHUNK, 1024), lambda i: (nc - 1 - i, P_DT // 1024)),
                  pl.BlockSpec((STATE_ROWS, LANES), rev),
                  _pspec((1, 1024)), _pspec((1, 1024)), _pspec((1, 1024)),
                  pl.BlockSpec((SSD_CHUNK, 1024), rev)],
        out_specs=[pl.BlockSpec((SSD_CHUNK, CONV_DIM), rev), pl.BlockSpec((SSD_CHUNK, 1024), rev),
                   _pspec((1, 1024)), _pspec((1, 1024)), _pspec((1, 1024))],
        out_shape=[jax.ShapeDtypeStruct((s, CONV_DIM), F32), jax.ShapeDtypeStruct((s, 1024), BF16)]
        + [jax.ShapeDtypeStruct((1, 1024), F32)] * 3,
        scratch_shapes=[pltpu.VMEM((STATE_ROWS, LANES), F32)],
        compiler_params=_cp("arbitrary"), name="ssd_bwd")(xbc, proj, hist, dtb, alog, dsk, dy)


def _attn_head(q, kp, kc, vp, vc, has_prev, slope):
    scale = ATTN_HEAD_DIM ** -0.5
    n = ATTN_STEPS
    qi = lax.broadcasted_iota(jnp.int32, (n, n), 0)
    kj = lax.broadcasted_iota(jnp.int32, (n, n), 1)
    sp = jnp.where((kj >= qi) & has_prev, _bdot(q, kp, "nt") * scale - slope * (qi + n - kj).astype(F32), NEG)
    sc = jnp.where(kj <= qi, _bdot(q, kc, "nt") * scale - slope * (qi - kj).astype(F32), NEG)
    m = lax.stop_gradient(jnp.maximum(jnp.max(sp, axis=1, keepdims=True), jnp.max(sc, axis=1, keepdims=True)))
    pp, pc = jnp.exp(sp - m), jnp.exp(sc - m)
    den = jnp.sum(pp, axis=1, keepdims=True) + jnp.sum(pc, axis=1, keepdims=True)
    o = (_bdot(pp, vp, "nn") + _bdot(pc, vc, "nn")) / den
    return o, jnp.broadcast_to(m + jnp.log(den), (n, ATTN_HEAD_DIM))


def _head_slope(gi, h):
    s = [float(v) * DILATIONS[gi] for v in SLOPES[gi]]
    return jnp.where(h == 0, s[0], jnp.where(h == 1, s[1], jnp.where(h == 2, s[2], s[3])))


ATTN_UNROLL = 4


def _attn_heads_per_block(d):
    return 4 if d == 1 else 1


def _units(ref, d, hb):
    if d == 1:
        return [ref[:, h * ATTN_HEAD_DIM:(h + 1) * ATTN_HEAD_DIM] for h in range(hb)]
    return [ref[pl.ds(r, ATTN_STEPS, stride=d), :] for r in range(d)]


def _store_units(ref, src, d, hb):
    if d == 1:
        for h in range(hb):
            ref[:, h * ATTN_HEAD_DIM:(h + 1) * ATTN_HEAD_DIM] = src[h]
    else:
        for r in range(d):
            ref[pl.ds(r, ATTN_STEPS, stride=d), :] = src[r]


def _attn_fwd(qn, kn, vv, gi):
    d = DILATIONS[gi]
    s = qn.shape[0]
    span = ATTN_STEPS * d
    nb = s // span

    hb, units = _attn_heads_per_block(d), _attn_heads_per_block(d) * d

    def body(q_ref, k_ref, v_ref, o_ref, l_ref, sq, sk, sv, so, sl):
        h0, b = pl.program_id(0) * hb, pl.program_id(1)
        cur, prev = b % 2, (b + 1) % 2

        @pl.when(b == 0)
        def _():
            sk[prev] = jnp.zeros(sk.shape[1:], F32)
            sv[prev] = jnp.zeros(sv.shape[1:], F32)

        for u, (qr, kr, vr) in enumerate(zip(_units(q_ref, d, hb), _units(k_ref, d, hb), _units(v_ref, d, hb))):
            sq[u] = qr
            sk[cur, u] = kr
            sv[cur, u] = vr

        def step(i, carry):
            for j in range(ATTN_UNROLL):
                u = i * ATTN_UNROLL + j
                so[u], sl[u] = _attn_head(sq[u], sk[prev, u], sk[cur, u], sv[prev, u], sv[cur, u], b > 0,
                                          _head_slope(gi, h0 + u // d))
            return carry

        lax.fori_loop(0, units // ATTN_UNROLL, step, 0)
        _store_units(o_ref, so, d, hb)
        _store_units(l_ref, sl, d, hb)

    blk = pl.BlockSpec((span, hb * ATTN_HEAD_DIM), lambda h, b: (b, (gi * 4) // hb + h))
    out = pl.BlockSpec((span, hb * ATTN_HEAD_DIM), lambda h, b: (b, h))
    res = (units, ATTN_STEPS, ATTN_HEAD_DIM)
    return pl.pallas_call(
        body, grid=(4 // hb, nb), in_specs=[blk, blk, blk], out_specs=[out, out],
        out_shape=[jax.ShapeDtypeStruct((s, ATTN_GROUP_W), F32)] * 2,
        scratch_shapes=[pltpu.VMEM(res, F32), pltpu.VMEM((2,) + res, F32), pltpu.VMEM((2,) + res, F32),
                        pltpu.VMEM(res, F32), pltpu.VMEM(res, F32)],
        compiler_params=_cp("parallel", "arbitrary"), name=f"attn_fwd_g{gi}")(qn, kn, vv)


def _attn_bwd(qn, kn, vv, do, dl, gi):
    d = DILATIONS[gi]
    s = qn.shape[0]
    span = ATTN_STEPS * d
    nb = s // span

    hb, units = _attn_heads_per_block(d), _attn_heads_per_block(d) * d

    def body(q_ref, kp_ref, kc_ref, vp_ref, vc_ref, do_ref, dl_ref, dq_ref, dk_ref, dv_ref, sin, sout, ck, cv):
        h0, bi = pl.program_id(0) * hb, pl.program_id(1)

        @pl.when(bi == 0)
        def _():
            ck[...] = jnp.zeros_like(ck)
            cv[...] = jnp.zeros_like(cv)

        for i, ref in enumerate((q_ref, kp_ref, kc_ref, vp_ref, vc_ref, do_ref, dl_ref)):
            for u, val in enumerate(_units(ref, d, hb)):
                sin[i, u] = val
        has_prev = bi < nb - 1

        def step(i, carry):
            for j in range(ATTN_UNROLL):
                u = i * ATTN_UNROLL + j
                f = functools.partial(_attn_head, has_prev=has_prev, slope=_head_slope(gi, h0 + u // d))
                _, vjp = jax.vjp(f, sin[0, u], sin[1, u], sin[2, u], sin[3, u], sin[4, u])
                dq, dkp, dkc, dvp, dvc = vjp((sin[5, u], sin[6, u]))
                sout[0, u] = dq
                sout[1, u] = dkc + ck[u]
                sout[2, u] = dvc + cv[u]
                ck[u] = dkp
                cv[u] = dvp
            return carry

        lax.fori_loop(0, units // ATTN_UNROLL, step, 0)
        for i, ref in enumerate((dq_ref, dk_ref, dv_ref)):
            _store_units(ref, sout.at[i], d, hb)

    w = hb * ATTN_HEAD_DIM
    cur = pl.BlockSpec((span, w), lambda h, b: (nb - 1 - b, (gi * 4) // hb + h))
    prev = pl.BlockSpec((span, w), lambda h, b: (jnp.maximum(nb - 2 - b, 0), (gi * 4) // hb + h))
    out = pl.BlockSpec((span, w), lambda h, b: (nb - 1 - b, h))
    res = (units, ATTN_STEPS, ATTN_HEAD_DIM)
    return pl.pallas_call(
        body, grid=(4 // hb, nb), in_specs=[cur, prev, cur, prev, cur, out, out], out_specs=[out, out, out],
        out_shape=[jax.ShapeDtypeStruct((s, ATTN_GROUP_W), F32)] * 3,
        scratch_shapes=[pltpu.VMEM((7,) + res, F32), pltpu.VMEM((3,) + res, F32), pltpu.VMEM(res, F32), pltpu.VMEM(res, F32)],
        compiler_params=_cp("parallel", "arbitrary"), name=f"attn_bwd_g{gi}")(qn, kn, kn, vv, vv, do, dl)


def _layer_fwd(x, mod, W, l, late=None):
    sh1, sc1, g1, sh2, sc2, g2 = (mod[i:i + 1] for i in range(6))
    (h,) = _rowwise_fwd("norm1", _f_norm, [(x, 1024, 0)], [W["norm1_w"], sc1, sh1], [(1024, BF16)])
    proj = _matmul("in_proj", h, W["w_in"], "nt")
    xbc = _conv_fwd(proj, W["conv_w"], W["conv_b"])
    y, hist = _ssd_fwd(xbc, proj, W["dt_bias"], W["a_log"], W["d_skip"])
    if late is not None:
        W.update(late(y))
    (yn,) = _rowwise_fwd("ssd_gate", _f_ssdgate, [(y, 1024, 0), (proj, 1024, P_Z // 1024)], [W["ssd_norm_w"]], [(1024, BF16)])
    y_ssd = _matmul("ssd_out", yn, W["w_ssd_out"], "nn", layer=l)
    qn, kn, vv = _rowwise_fwd("qk_norm", _f_qknorm, [(proj, 1536, P_Q // 1536), (proj, 1536, P_K // 1536), (proj, 1536, P_V // 1536)],
                              [W["q_norm_w"], W["k_norm_w"]], [(1536, F32)] * 3)
    ol = [_attn_fwd(qn, kn, vv, gi) for gi in range(3)]
    (o,) = _rowwise_fwd("attn_combine", _f_combine, [(t[0], 512, 0) for t in ol] + [(t[1], 512, 0) for t in ol], [], [(512, BF16)])
    y_attn = _matmul("attn_out", o, W["w_attn_out"], "nn", layer=l, chips=256)
    pooled = _pool_fwd(proj)
    pm = _group_matmul("pool_mix", pooled, W["w_pool_mix"], "nn", layer=l)
    (ps,) = _rowwise_fwd("pool_scale", _f_poolscale, [(pm, 1024, 0)], [W["pool_scale"]], [(1024, BF16)])
    y_pool = _matmul("pool_out", ps, W["w_pool_out"], "nn", layer=l)
    (merged,) = _rowwise_fwd("merge", _f_merge, [(proj, 3072, P_GATES // 3072), (y_ssd, 1024, 0), (y_attn, 1024, 0), (y_pool, 1024, 0)],
                             [], [(1024, BF16)])
    mo = _matmul("mix_out", merged, W["w_out"], "nn", layer=l)
    (x1,) = _rowwise_fwd("resid1", _f_resid, [(x, 1024, 0), (mo, 1024, 0)], [g1], [(1024, F32)])
    (h2,) = _rowwise_fwd("norm2", _f_norm, [(x1, 1024, 0)], [W["norm2_w"], sc2, sh2], [(1024, BF16)])
    a = _matmul("ff1", h2, W["w_ff1"], "nn", layer=l, chips=1024)
    (r,) = _rowwise_fwd("relu2", _f_relu2, [(a, D_FF, 0)], [], [(D_FF, BF16)], tb=128)
    ff = _matmul("ff2", r, W["w_ff2"], "nn", layer=l)
    (x2,) = _rowwise_fwd("resid2", _f_resid, [(x1, 1024, 0), (ff, 1024, 0)], [g2], [(1024, F32)])
    saved = dict(x=x, h=h, proj=proj, xbc=xbc, y=y, hist=hist, yn=yn, y_ssd=y_ssd, qn=qn, kn=kn, vv=vv, ol=ol, o=o,
                 y_attn=y_attn, pooled=pooled, pm=pm, ps=ps, y_pool=y_pool, merged=merged, mo=mo, x1=x1, h2=h2, a=a, r=r, ff=ff)
    return x2, saved


def _layer_bwd(dx2, mod, W, sv, l, bufs):
    sh1, sc1, g1, sh2, sc2, g2 = (mod[i:i + 1] for i in range(6))
    g = {}
    dx1a, dff, dg2 = _rowwise_bwd("resid2_bwd", _f_resid, [(sv["x1"], 1024, 0), (sv["ff"], 1024, 0)], [g2], [(dx2, 1024, 0)],
                                  [True, True], gdt=[F32, BF16])
    g["w_ff2"] = _matmul("ff2_dw", sv["r"], dff, "tn", BF16, into=(bufs["w_ff2"], l))
    dr = _matmul("ff2_dx", dff, W["w_ff2"], "nt", layer=l)
    (da,) = _rowwise_bwd("relu2_bwd", _f_relu2, [(sv["a"], D_FF, 0)], [], [(dr, D_FF, 0)], [True], tb=128, gdt=[BF16])
    g["w_ff1"] = _matmul("ff1_dw", sv["h2"], da, "tn", BF16, out_chips=1024, into=(bufs["w_ff1"], l))
    dh2 = _matmul("ff1_dx", da, W["w_ff1"], "nt", layer=l, chips=1024)
    dx1, g["norm2_w"], dsc2, dsh2 = _rowwise_bwd("norm2_bwd", _f_norm, [(sv["x1"], 1024, 0)], [W["norm2_w"], sc2, sh2],
                                                 [(dh2, 1024, 0)], [True], add=(0, dx1a))
    dxa, dmo, dg1 = _rowwise_bwd("resid1_bwd", _f_resid, [(sv["x"], 1024, 0), (sv["mo"], 1024, 0)], [g1], [(dx1, 1024, 0)],
                                 [True, True], gdt=[F32, BF16])
    g["w_out"] = _matmul("mix_out_dw", sv["merged"], dmo, "tn", BF16, into=(bufs["w_out"], l))
    dmerged = _matmul("mix_out_dx", dmo, W["w_out"], "nt", layer=l)
    proj = sv["proj"]
    dgates, dy_ssd, dy_attn, dy_pool = _rowwise_bwd(
        "merge_bwd", _f_merge, [(proj, 3072, P_GATES // 3072), (sv["y_ssd"], 1024, 0), (sv["y_attn"], 1024, 0), (sv["y_pool"], 1024, 0)],
        [], [(dmerged, 1024, 0)], [True] * 4, gdt=[BF16] * 4)
    g["w_pool_out"] = _matmul("pool_out_dw", sv["ps"], dy_pool, "tn", BF16, into=(bufs["w_pool_out"], l))
    dps = _matmul("pool_out_dx", dy_pool, W["w_pool_out"], "nt", layer=l)
    dpm, g["pool_scale"] = _rowwise_bwd("pool_scale_bwd", _f_poolscale, [(sv["pm"], 1024, 0)], [W["pool_scale"]], [(dps, 1024, 0)],
                                        [True], gdt=[BF16])
    dmix = _group_matmul("pool_mix_dw", sv["pooled"], dpm, "tn")
    g["w_pool_mix"] = bufs["w_pool_mix"].at[:, l].set(
        jnp.moveaxis(dmix.reshape(4, N_CHIPS, POOL_GW // N_CHIPS, POOL_GW), 1, 0).astype(BF16))
    dpooled = _group_matmul("pool_mix_dx", dpm, W["w_pool_mix"], "nt", layer=l)
    du = _pool_bwd(dpooled)
    g["w_attn_out"] = _matmul("attn_out_dw", sv["o"], dy_attn, "tn", BF16, out_chips=256, into=(bufs["w_attn_out"], l))
    do = _matmul("attn_out_dx", dy_attn, W["w_attn_out"], "nt", layer=l, chips=256)
    ol = sv["ol"]
    dol = _rowwise_bwd("attn_combine_bwd", _f_combine, [(t[0], 512, 0) for t in ol] + [(t[1], 512, 0) for t in ol], [],
                       [(do, 512, 0)], [True] * 6)
    dqs, dks, dvs = zip(*[_attn_bwd(sv["qn"], sv["kn"], sv["vv"], dol[gi], dol[3 + gi], gi) for gi in range(3)])
    dqn, dkn, dvv = (jnp.concatenate(t, axis=1) for t in (dqs, dks, dvs))
    dq, dk, dv, g["q_norm_w"], g["k_norm_w"] = _rowwise_bwd(
        "qk_norm_bwd", _f_qknorm, [(proj, 1536, P_Q // 1536), (proj, 1536, P_K // 1536), (proj, 1536, P_V // 1536)],
        [W["q_norm_w"], W["k_norm_w"]], [(dqn, 1536, 0), (dkn, 1536, 0), (dvv, 1536, 0)], [True] * 3, gdt=[BF16] * 3)
    g["w_ssd_out"] = _matmul("ssd_out_dw", sv["yn"], dy_ssd, "tn", BF16, into=(bufs["w_ssd_out"], l))
    dyn = _matmul("ssd_out_dx", dy_ssd, W["w_ssd_out"], "nt", layer=l)
    dy, dz, g["ssd_norm_w"] = _rowwise_bwd("ssd_gate_bwd", _f_ssdgate, [(sv["y"], 1024, 0), (proj, 1024, P_Z // 1024)], [W["ssd_norm_w"]],
                                           [(dyn, 1024, 0)], [True, True], gdt=[F32, BF16])
    dxbc, ddt, g["dt_bias"], g["a_log"], g["d_skip"] = _ssd_bwd(sv["xbc"], proj, sv["hist"], W["dt_bias"], W["a_log"], W["d_skip"], dy)
    dxbc_raw, g["conv_w"], g["conv_b"] = _conv_bwd(proj, W["conv_w"], W["conv_b"], dxbc)
    dproj = jnp.concatenate([dxbc_raw, dq, dk, dv, dgates, dz, du, ddt], axis=1)
    g["w_in"] = _matmul("in_proj_dw", dproj, sv["h"], "tn", BF16)
    dh = _matmul("in_proj_dx", dproj, W["w_in"], "nn")
    dx, g["norm1_w"], dsc1, dsh1 = _rowwise_bwd("norm1_bwd", _f_norm, [(sv["x"], 1024, 0)], [W["norm1_w"], sc1, sh1],
                                                [(dh, 1024, 0)], [True], add=(0, dxa))
    dmod = jnp.concatenate([dsh1, dsc1, dg1, dsh2, dsc2, dg2], axis=0)
    return dx, dmod, g


def _expand_heads(t):
    return jnp.repeat(t, SSD_HEAD_DIM, axis=-1)


def _reduce_heads(t):
    return t.reshape(t.shape[:-1] + (SSD_HEADS, SSD_HEAD_DIM)).sum(-1)


_IN_SPLITS = np.cumsum((0,) + IN_SIZES)


def _w_in_to_layout(wt):
    z, xbc, dt, q, k, v, u, gates = (wt[_IN_SPLITS[i]:_IN_SPLITS[i + 1]] for i in range(8))
    return jnp.concatenate([xbc, q, k, v, gates, z, u, jnp.repeat(dt, SSD_HEAD_DIM, axis=0)], axis=0)


def _w_in_from_layout(g):
    xbc, q, k, v = (g[o:o + 1536] for o in (P_XBC, P_Q, P_K, P_V))
    gates, z, u, dt = g[P_GATES:P_GATES + 3072], g[P_Z:P_Z + 1024], g[P_U:P_U + 1024], g[P_DT:P_DT + 1024]
    dt = dt.astype(F32).reshape(SSD_HEADS, SSD_HEAD_DIM, D_MODEL).sum(1).astype(g.dtype)
    return jnp.concatenate([z, xbc, dt, q, k, v, u, gates], axis=0)


_STACKED = ("w_ssd_out", "w_attn_out", "w_pool_mix", "w_pool_out", "w_out", "w_ff1", "w_ff2")
_ROWS = ("norm1_w", "norm2_w", "conv_b", "ssd_norm_w", "q_norm_w", "k_norm_w", "pool_scale")
_HEAD_ROWS = ("dt_bias", "a_log", "d_skip")


def _layer_weights(wg, lg, small, l):
    W = {k: wg[k] for k in _STACKED if k in wg}
    W["w_in"] = _w_in_to_layout(wg["w_in"][lg].reshape(IN_WIDTH, D_MODEL))
    W["conv_w"] = small["conv_w"][l]
    for k in _ROWS:
        W[k] = small[k][l][None, :]
    for k in _HEAD_ROWS:
        W[k] = _expand_heads(small[k][l])[None, :]
    return W


def _layer_grads_by_chip(g, w_in_buf, l):
    out = dict(g)
    out["w_in"] = w_in_buf.at[:, l].set(_w_in_from_layout(g["w_in"]).reshape(N_CHIPS, IN_WIDTH // N_CHIPS, D_MODEL))
    for k in _ROWS:
        out[k] = g[k][0]
    for k in _HEAD_ROWS:
        out[k] = _reduce_heads(g[k][0])
    return out


ANY = pl.BlockSpec(memory_space=pl.ANY)


def _place():
    x, y, c = lax.axis_index("x"), lax.axis_index("y"), lax.axis_index("c")
    return x, y, c, (x, y, 1 - c), [(1 - x, y), (x, 1 - y), (1 - x, 1 - y)]


def _allgather8(name, blk):
    m_per, n = blk.shape

    def body(x_ref, out_ref, send_sems, recv_sems, local_sem):
        x, y, c, sibling, chips = _place()
        me = (x, y, c)

        def rows(px, py, pc):
            return out_ref.at[pl.ds((4 * px + 2 * py + pc) * m_per, m_per), :]

        def copy(k, block, to, src=None):
            return pltpu.make_async_remote_copy(
                src_ref=rows(*block) if src is None else src, dst_ref=rows(*block),
                send_sem=send_sems.at[k], recv_sem=recv_sems.at[k], device_id=to, device_id_type=MESH)

        mine = pltpu.make_async_copy(x_ref, rows(*me), local_sem)
        mine.start()
        first = [copy(0, me, sibling, src=x_ref)]
        first += [copy(1 + j, me, (*chip, c), src=x_ref) for j, chip in enumerate(chips)]
        for cp in first:
            cp.start()
        passed = [copy(4 + j, (*chip, c), sibling) for j, chip in enumerate(chips)]
        for j, chip in enumerate(chips):
            copy(1 + j, (*chip, c), me).wait_recv()
            passed[j].start()
        copy(0, sibling, me).wait_recv()
        for j, chip in enumerate(chips):
            copy(4 + j, (*chip, 1 - c), me).wait_recv()
        for cp in first + passed:
            cp.wait_send()
        mine.wait()

    return pl.pallas_call(
        body, out_shape=jax.ShapeDtypeStruct((N_DEV * m_per, n), blk.dtype),
        in_specs=[pl.BlockSpec(memory_space=pltpu.VMEM)], out_specs=pl.BlockSpec(memory_space=pltpu.VMEM),
        scratch_shapes=[pltpu.SemaphoreType.DMA((7,)), pltpu.SemaphoreType.DMA((7,)), pltpu.SemaphoreType.DMA],
        name=name)(blk)


HBM_SPEC = pl.BlockSpec(memory_space=pltpu.HBM)
SEM_SPEC = pl.BlockSpec(memory_space=pltpu.SEMAPHORE)
SIDE_EFFECT = pltpu.SideEffectType.DATAFLOW_SIDE_EFFECTING


def _dma_sems(n):
    return [pltpu.SemaphoreType.DMA((n,)), pltpu.SemaphoreType.DMA((n,))]


def _half_axis(shape):
    return 1 if (len(shape) > 3 or (shape[1] // 2) % 16 == 0) else len(shape) - 1


def _halves(ref, axis, c):
    r2 = ref.shape[axis] // 2
    lead = (slice(None),) * axis
    return ref.at[lead + (pl.ds(r2 * c, r2),)], ref.at[lead + (pl.ds(r2 * (1 - c), r2),)]


def _gather_copies(srcs, lands, send_sems, recv_sems):
    x, y, c, _, chips = _place()
    sends, lands_here = [], []
    for j, (cx, cy) in enumerate(chips):
        for i, (s, t) in enumerate(zip(srcs, lands)):
            k = 3 * i + j
            ax = _half_axis(s.shape)
            mine = _halves(t.at[:, 2 * x + y], ax, c)[0]
            theirs = _halves(t.at[:, 2 * cx + cy], ax, c)[0]
            sends.append(pltpu.make_async_remote_copy(src_ref=_halves(s, ax, c)[0], dst_ref=mine, send_sem=send_sems.at[k],
                                                      recv_sem=recv_sems.at[k], device_id=(cx, cy, c), device_id_type=MESH))
            lands_here.append(pltpu.make_async_remote_copy(src_ref=theirs, dst_ref=theirs, send_sem=send_sems.at[k],
                                                           recv_sem=recv_sems.at[k], device_id=(cx, cy, c), device_id_type=MESH))
    return sends, lands_here


def _exchange_copies(srcs, lands, send_sems, recv_sems):
    x, y, c, _, chips = _place()
    sends = [pltpu.make_async_remote_copy(src_ref=s.at[2 * cx + cy], dst_ref=t.at[j], send_sem=send_sems.at[3 * i + j],
                                          recv_sem=recv_sems.at[3 * i + j], device_id=(cx, cy, c), device_id_type=MESH)
             for j, (cx, cy) in enumerate(chips) for i, (s, t) in enumerate(zip(srcs, lands))]
    return sends, sends


def _split_start(name, copies, srcs, lands, after):
    ns, nl = len(srcs), len(lands)
    n_copies = 3 * ns

    def body(*refs):
        send_sems, recv_sems = refs[ns + nl + 1], refs[ns + nl + 2]
        for cp in copies(refs[:ns], refs[ns:ns + nl], send_sems, recv_sems)[0]:
            cp.start()
        refs[-1][...] = jnp.zeros_like(refs[-1])

    arrs = list(srcs) + list(lands)
    res = pl.pallas_call(
        body, name=name,
        out_shape=(pltpu.SemaphoreType.DMA((n_copies,)), pltpu.SemaphoreType.DMA((n_copies,)))
        + tuple(pltpu.HBM(a.shape, a.dtype) for a in arrs) + (jax.ShapeDtypeStruct((8, LANES), F32),),
        in_specs=[HBM_SPEC] * (ns + nl) + [ANY],
        out_specs=(SEM_SPEC, SEM_SPEC) + (HBM_SPEC,) * (ns + nl) + (pl.BlockSpec(memory_space=pltpu.VMEM),),
        input_output_aliases={i: 2 + i for i in range(ns + nl)},
        compiler_params=pltpu.CompilerParams(has_side_effects=SIDE_EFFECT),
    )(*[pltpu.with_memory_space_constraint(a, pltpu.HBM) for a in arrs], after)
    return res[0], res[1], list(res[2:2 + ns]), list(res[2 + ns:2 + ns + nl]), res[-1]


def _split_wait(name, copies, send_sems, recv_sems, srcs, lands, after):
    ns, nl = len(srcs), len(lands)

    def body(*refs):
        sends, lands_here = copies(refs[:ns], refs[ns:ns + nl], refs[ns + nl], refs[ns + nl + 1])
        for cp in sends:
            cp.wait_send()
        for cp in lands_here:
            cp.wait_recv()

    arrs = list(srcs) + list(lands)
    res = pl.pallas_call(
        body, name=name, out_shape=tuple(pltpu.HBM(a.shape, a.dtype) for a in arrs),
        in_specs=[HBM_SPEC] * (ns + nl) + [SEM_SPEC, SEM_SPEC, ANY], out_specs=(HBM_SPEC,) * (ns + nl),
        input_output_aliases={i: i for i in range(ns + nl)},
        compiler_params=pltpu.CompilerParams(has_side_effects=SIDE_EFFECT),
    )(*arrs, send_sems, recv_sems, after)
    return list(res[:ns]), list(res[ns:])


def _gather_forward(name, lands):
    n = len(lands)

    def body(*refs):
        ins, outs = refs[:n], refs[n:2 * n]
        send_sems, recv_sems = refs[2 * n:]
        x, y, c, sibling, chips = _place()
        sends, arrivals = [], []
        for j, (cx, cy) in enumerate(chips):
            for i in range(n):
                k = 3 * i + j
                ax = _half_axis(ins[i].shape[:1] + ins[i].shape[2:])
                src = _halves(ins[i].at[:, 2 * cx + cy], ax, c)[0]
                dst, theirs = _halves(outs[i].at[:, 2 * cx + cy], ax, c)
                sends.append(pltpu.make_async_remote_copy(src_ref=src, dst_ref=dst, send_sem=send_sems.at[k], recv_sem=recv_sems.at[k],
                                                          device_id=sibling, device_id_type=MESH))
                arrivals.append(pltpu.make_async_remote_copy(src_ref=theirs, dst_ref=theirs, send_sem=send_sems.at[k],
                                                             recv_sem=recv_sems.at[k], device_id=sibling, device_id_type=MESH))
        for cp in sends:
            cp.start()
        for cp in arrivals:
            cp.wait_recv()
        for cp in sends:
            cp.wait_send()

    return pl.pallas_call(
        body, out_shape=[jax.ShapeDtypeStruct(t.shape, t.dtype) for t in lands], in_specs=[ANY] * n, out_specs=[ANY] * n,
        input_output_aliases={i: i for i in range(n)}, scratch_shapes=_dma_sems(3 * n), name=name)(*lands)


def _swap_halves(name, gs):
    n = len(gs)

    def body(*refs):
        ins, got = refs[:n], refs[n:2 * n]
        send_sems, recv_sems = refs[2 * n:]
        x, y, c, sibling, _ = _place()
        sends = [pltpu.make_async_remote_copy(src_ref=_halves(ins[i], 1 + _half_axis(ins[i].shape[1:]), c)[1], dst_ref=got[i],
                                              send_sem=send_sems.at[i], recv_sem=recv_sems.at[i], device_id=sibling, device_id_type=MESH)
                 for i in range(n)]
        for cp in sends:
            cp.start()
        for cp in sends:
            cp.wait_recv()
        for cp in sends:
            cp.wait_send()

    def half_shape(t):
        ax = 1 + _half_axis(t.shape[1:])
        return t.shape[:ax] + (t.shape[ax] // 2,) + t.shape[ax + 1:]

    return pl.pallas_call(
        body, out_shape=[jax.ShapeDtypeStruct(half_shape(t), t.dtype) for t in gs],
        in_specs=[ANY] * n, out_specs=[ANY] * n, scratch_shapes=_dma_sems(n), name=name)(*gs)


def _share_halves(name, ts, axes):
    n = len(ts)

    def body(*refs):
        ins, outs = refs[:n], refs[n:2 * n]
        send_sems, recv_sems = refs[2 * n:]
        x, y, c, sibling, _ = _place()
        sends, arrivals = [], []
        for i in range(n):
            mine, theirs = _halves(outs[i], axes[i], c)
            sends.append(pltpu.make_async_remote_copy(src_ref=ins[i], dst_ref=mine, send_sem=send_sems.at[i], recv_sem=recv_sems.at[i],
                                                      device_id=sibling, device_id_type=MESH))
            arrivals.append(pltpu.make_async_remote_copy(src_ref=ins[i], dst_ref=theirs, send_sem=send_sems.at[i],
                                                         recv_sem=recv_sems.at[i], device_id=sibling, device_id_type=MESH))
        for cp in sends:
            cp.start()
        for cp in arrivals:
            cp.wait_recv()
        for cp in sends:
            cp.wait_send()

    return pl.pallas_call(
        body, out_shape=[jax.ShapeDtypeStruct(t.shape[:ax] + (2 * t.shape[ax],) + t.shape[ax + 1:], t.dtype) for t, ax in zip(ts, axes)],
        in_specs=[ANY] * n, out_specs=[ANY] * n, scratch_shapes=_dma_sems(n), name=name)(*ts)


PACK_W = 1024
PACK_TB = 512


def _sum_rows(name, parts, out_dtype):
    def f(*vals):
        acc = vals[0]
        for v in vals[1:]:
            acc = acc + v
        return (acc,)

    return _rowwise_fwd(name, f, [(p, PACK_W, 0) for p in parts], [], [(PACK_W, out_dtype)], tb=_tile(parts[0].shape[0], PACK_TB))[0]


def _sum_slots(name, ops, count, out_dtype):
    mat = ops[0][0].shape[1:]

    def body(*refs):
        acc = refs[0][...].astype(F32)
        for r in refs[1:-1]:
            acc = acc + r[...].astype(F32)
        refs[-1][...] = acc.astype(refs[-1].dtype)

    return pl.pallas_call(
        body, grid=(count,), in_specs=[pl.BlockSpec((None,) + mat, lambda i, s=s: (s * count + i, 0, 0)) for _, s in ops],
        out_specs=pl.BlockSpec((None,) + mat, lambda i: (i, 0, 0)), out_shape=jax.ShapeDtypeStruct((count,) + mat, out_dtype),
        compiler_params=_cp("parallel"), name=name)(*[a for a, _ in ops])


W_IN_ADAM_ROWS = 47


def _adamw(name, w, g, m, v, row0=0, unit=None, prev=None, dep=None):
    r, tail = w.shape[0], w.shape[1:]
    cw = int(np.prod(tail))
    zeros = (0,) * len(tail)
    tb = unit or r
    while tb * cw > 300_000 and tb % 16 == 0:
        tb //= 2
    off = row0 // tb
    c1 = 1.0 / (1.0 - ADAM_B1 ** ADAM_STEP)
    c2 = 1.0 / (1.0 - ADAM_B2 ** ADAM_STEP)

    def body(w_ref, g_ref, m_ref, v_ref, *rest):
        go_ref, d_ref, mo_ref, vo_ref = rest[-4:]
        gg = g_ref[...]
        mn = ADAM_B1 * m_ref[...] + (1.0 - ADAM_B1) * gg
        vn = ADAM_B2 * v_ref[...] + (1.0 - ADAM_B2) * jnp.square(gg)
        go_ref[...] = gg
        d_ref[...] = -ADAM_LR * ((mn * c1) / (jnp.sqrt(vn * c2) + ADAM_EPS) + ADAM_WD * w_ref[...])
        mo_ref[...] = mn
        vo_ref[...] = vn

    full = pl.BlockSpec((tb,) + tail, lambda i: (off + i,) + zeros)
    extra = ([] if prev is None else list(prev)) + ([] if dep is None else [dep])
    n_prev = 0 if prev is None else 4
    return pl.pallas_call(
        body, grid=(g.shape[0] // tb,),
        in_specs=[full, pl.BlockSpec((tb,) + tail, lambda i: (i,) + zeros), full, full] + [pl.BlockSpec(memory_space=pl.ANY)] * len(extra),
        out_specs=[full] * 4, out_shape=[jax.ShapeDtypeStruct(w.shape, F32)] * 4,
        input_output_aliases={4 + i: i for i in range(n_prev)},
        compiler_params=_cp("parallel"), name=name)(w, g, m, v, *extra)


def _silu_rows(c):
    def body(c_ref, o_ref):
        rows = lax.broadcasted_iota(jnp.int32, o_ref.shape, 0)
        o_ref[...] = jnp.where(rows == 0, jnp.broadcast_to(_silu(c_ref[...]), o_ref.shape), 0.0)

    return pl.pallas_call(body, out_shape=jax.ShapeDtypeStruct((8, c.shape[1]), F32), name="cond_silu")(c)


_KINDS = ("w_in", "w_ssd_out", "w_attn_out", "w_pool_mix", "w_pool_out", "w_out", "w_ff1", "w_ff2")
_SMALL = ("b_ada", "norm1_w", "norm2_w", "conv_b", "dt_bias", "a_log", "d_skip", "ssd_norm_w", "q_norm_w", "k_norm_w",
          "pool_scale")
_ORDER = ("w_ada", "b_ada", "norm1_w", "norm2_w", "w_in", "conv_w", "conv_b", "dt_bias", "a_log", "d_skip", "ssd_norm_w",
          "w_ssd_out", "q_norm_w", "k_norm_w", "w_attn_out", "w_pool_mix", "pool_scale", "w_pool_out", "w_out", "w_ff1", "w_ff2")


def _pack_flat(arrs, rows, dtype):
    flat = jnp.concatenate([a.reshape(-1).astype(dtype) for a in arrs])
    return jnp.pad(flat, (0, rows * PACK_W - flat.shape[0])).reshape(rows, PACK_W)


def _unpack_flat(buf, shapes):
    flat = buf.reshape(-1)
    out, off = [], 0
    for shp in shapes:
        n = int(np.prod(shp))
        out.append(flat[off:off + n].reshape(shp))
        off += n
    return out


def _small_rows(n_elems):
    return -(-n_elems // (8 * PACK_W)) * 8


def kernel(x, c, w_ada, b_ada, norm1_w, norm2_w, w_in, conv_w, conv_b, dt_bias, a_log, d_skip, ssd_norm_w, w_ssd_out, q_norm_w, k_norm_w, w_attn_out, w_pool_mix, pool_scale, w_pool_out, w_out, w_ff1, w_ff2, loss_target, m_w_ada, m_b_ada, m_norm1_w, m_norm2_w, m_w_in, m_conv_w, m_conv_b, m_dt_bias, m_a_log, m_d_skip, m_ssd_norm_w, m_w_ssd_out, m_q_norm_w, m_k_norm_w, m_w_attn_out, m_w_pool_mix, m_pool_scale, m_w_pool_out, m_w_out, m_w_ff1, m_w_ff2, v_w_ada, v_b_ada, v_norm1_w, v_norm2_w, v_w_in, v_conv_w, v_conv_b, v_dt_bias, v_a_log, v_d_skip, v_ssd_norm_w, v_w_ssd_out, v_q_norm_w, v_k_norm_w, v_w_attn_out, v_w_pool_mix, v_pool_scale, v_w_pool_out, v_w_out, v_w_ff1, v_w_ff2):
    w = dict(w_ada=w_ada, b_ada=b_ada, norm1_w=norm1_w, norm2_w=norm2_w, w_in=w_in, conv_w=conv_w, conv_b=conv_b, dt_bias=dt_bias, a_log=a_log, d_skip=d_skip, ssd_norm_w=ssd_norm_w, w_ssd_out=w_ssd_out, q_norm_w=q_norm_w, k_norm_w=k_norm_w, w_attn_out=w_attn_out, w_pool_mix=w_pool_mix, pool_scale=pool_scale, w_pool_out=w_pool_out, w_out=w_out, w_ff1=w_ff1, w_ff2=w_ff2)
    m = dict(w_ada=m_w_ada, b_ada=m_b_ada, norm1_w=m_norm1_w, norm2_w=m_norm2_w, w_in=m_w_in, conv_w=m_conv_w, conv_b=m_conv_b, dt_bias=m_dt_bias, a_log=m_a_log, d_skip=m_d_skip, ssd_norm_w=m_ssd_norm_w, w_ssd_out=m_w_ssd_out, q_norm_w=m_q_norm_w, k_norm_w=m_k_norm_w, w_attn_out=m_w_attn_out, w_pool_mix=m_w_pool_mix, pool_scale=m_pool_scale, w_pool_out=m_w_pool_out, w_out=m_w_out, w_ff1=m_w_ff1, w_ff2=m_w_ff2)
    v = dict(w_ada=v_w_ada, b_ada=v_b_ada, norm1_w=v_norm1_w, norm2_w=v_norm2_w, w_in=v_w_in, conv_w=v_conv_w, conv_b=v_conv_b, dt_bias=v_dt_bias, a_log=v_a_log, d_skip=v_d_skip, ssd_norm_w=v_ssd_norm_w, w_ssd_out=v_w_ssd_out, q_norm_w=v_q_norm_w, k_norm_w=v_k_norm_w, w_attn_out=v_w_attn_out, w_pool_mix=v_w_pool_mix, pool_scale=v_pool_scale, w_pool_out=v_w_pool_out, w_out=v_w_out, w_ff1=v_w_ff1, w_ff2=v_w_ff2)
    chip = 2 * lax.axis_index("x") + lax.axis_index("y")
    dev = 2 * chip + lax.axis_index("c")
    ada_cols = w_ada.shape[2]

    wk = dict({k: w[k] for k in _KINDS}, w_in=jnp.transpose(w_in, (0, 2, 1)))
    rest = _KINDS[1:]

    def start_gather(tag, lo, n, kinds, after):
        shards = [wk[k][lo:lo + n].astype(BF16) for k in kinds]
        lands = [lax.empty((n, N_CHIPS) + s.shape[1:], BF16) for s in shards]
        return _split_start("gather_start_" + tag, _gather_copies, shards, lands, after)

    gather_a1 = start_gather("a1", 0, 1, _KINDS[:1], c)
    c = c + gather_a1[4][0, 0]

    n_conv = conv_w.size // PACK_W
    rows1 = _small_rows((1 + n_conv) * PACK_W)
    blk = jnp.concatenate([_silu_rows(c)[:1], conv_w.reshape(n_conv, PACK_W), jnp.zeros((rows1 - 1 - n_conv, PACK_W), F32)])
    first = _allgather8("gather_cond", blk).reshape(N_DEV, rows1, PACK_W)
    cond_all = first[:, 0]
    conv_all = first[0::2, 1:1 + n_conv].reshape((N_CHIPS,) + conv_w.shape)
    conv_full = jnp.moveaxis(conv_all, 0, 2).reshape(DEPTH, SSD_CONV, CONV_DIM)
    b_cols = lax.dynamic_slice_in_dim(b_ada, chip * ada_cols, ada_cols, axis=1)
    mod_cols = jnp.stack([_matmul("ada_fwd", cond_all, w_ada[l], "nn", precise=True) + b_cols[l][None, :] for l in range(DEPTH)])
    mod_all = _allgather8("gather_mod", mod_cols.reshape(-1, PACK_W)).reshape(N_DEV, DEPTH, N_DEV, ada_cols)
    mine = lax.dynamic_index_in_dim(mod_all[0::2], dev, axis=2, keepdims=False)
    mods = jnp.moveaxis(mine, 0, 1).reshape(DEPTH, 6, D_MODEL)

    core = lax.axis_index("c")
    small_w = dict({k: w[k] for k in _SMALL[1:]}, conv_w=conv_full)

    def finish_gather(tag, handle, kinds, after):
        shards, lands = _split_wait("gather_wait_" + tag, _gather_copies, handle[0], handle[1], handle[2], handle[3], after)
        lands = _gather_forward("gather_forward_" + tag, lands)
        wg = {k: lax.dynamic_update_slice_in_dim(t, s[:, None], chip, axis=1) for k, t, s in zip(kinds, lands, shards)}
        n = shards[0].shape[0]
        for k in ("w_ssd_out", "w_pool_out", "w_out", "w_ff2"):
            if k in wg:
                wg[k] = wg[k].reshape(n, -1, D_MODEL)
        if "w_pool_mix" in wg:
            wg["w_pool_mix"] = jnp.moveaxis(wg["w_pool_mix"], 1, 2).reshape(n, 4, POOL_GW, POOL_GW)
        return wg

    wg_a1 = finish_gather("a1", gather_a1, _KINDS[:1], mods)
    gather_a2 = start_gather("a2", 0, 1, rest, wg_a1["w_in"])
    gather_b = start_gather("b", 1, DEPTH - 1, _KINDS, gather_a2[4])
    mods = mods + gather_b[4][0, 0]

    xc = x[0]
    Ws, saved = [None] * DEPTH, [None] * DEPTH
    Ws[0] = _layer_weights(wg_a1, 0, small_w, 0)
    xc, saved[0] = _layer_fwd(xc, mods[0], Ws[0], 0, late=lambda y: finish_gather("a2", gather_a2, rest, y))
    wg_b = finish_gather("b", gather_b, _KINDS, xc)
    for l in range(1, DEPTH):
        Ws[l] = _layer_weights(wg_b, l - 1, small_w, l)
        xc, saved[l] = _layer_fwd(xc, mods[l], Ws[l], l - 1)
    dx, loss = _loss_and_grad(xc, loss_target[0])
    dmods, grads = [None] * DEPTH, [None] * DEPTH

    def backward(l, lg, dx, mod, bufs):
        dx, dmods[l], g = _layer_bwd(dx, mod, Ws[l], saved[l], lg, bufs)
        grads[l] = _layer_grads_by_chip(g, bufs["w_in"], lg)
        return dx, {k: grads[l][k] for k in _KINDS}

    bufs_b = {k: lax.empty((N_CHIPS, DEPTH - 1) + wk[k].shape[1:], BF16) for k in _KINDS}
    for l in reversed(range(1, DEPTH)):
        dx, bufs_b = backward(l, l - 1, dx, mods[l], bufs_b)

    def flat(t):
        return t.reshape((-1,) + t.shape[-2:])

    def start_exchange(tag, bufs, after):
        gs = [bufs[k] for k in _KINDS]
        got = _swap_halves("grad_swap_" + tag, gs)
        axes = [1 + _half_axis(t.shape[1:]) for t in gs]
        own = [lax.dynamic_slice_in_dim(t, (t.shape[ax] // 2) * core, t.shape[ax] // 2, axis=ax) for t, ax in zip(gs, axes)]
        pairs = [_sum_slots(f"sum_pair_{tag}_{k}", [(flat(a), 0), (flat(b), 0)], flat(a).shape[0], BF16).reshape(a.shape)
                 for k, a, b in zip(_KINDS, own, got)]
        lands = [lax.empty((3,) + p.shape[1:], BF16) for p in pairs]
        return _split_start("exchange_start_" + tag, _exchange_copies, pairs, lands, after)

    def finish_exchange(tag, handle, after):
        pairs, partials = _split_wait("exchange_wait_" + tag, _exchange_copies, handle[0], handle[1], handle[2], handle[3], after)
        mine = [lax.dynamic_index_in_dim(p, chip, axis=0, keepdims=False) for p in pairs]
        totals = [_sum_slots(f"sum_chips_{tag}_{k}", [(flat(a), 0)] + [(flat(p), s) for s in range(3)], flat(a).shape[0], F32).reshape(a.shape)
                  for k, a, p in zip(_KINDS, mine, partials)]
        axes = [_half_axis((1,) + wk[k].shape[1:]) for k in _KINDS]
        return [lax.dynamic_update_slice_in_dim(t, mine_t, mine_t.shape[ax] * core, axis=ax)
                for t, mine_t, ax in zip(_share_halves("grad_share_" + tag, totals, axes), totals, axes)]

    def adamw_group(tag, lo, gs, prev, dep):
        out = {}
        for k, gk in zip(rest, gs[1:]):
            shp = w[k].shape
            unit = int(np.prod(shp[1:-1]))
            two_d = lambda t: t.reshape(-1, shp[-1])
            out[k] = _adamw(f"adamw_{tag}_{k}", two_d(w[k]), two_d(gk), two_d(m[k]), two_d(v[k]), row0=lo * unit, unit=unit,
                            prev=None if prev is None else prev[k], dep=dep)
            dep = out[k][1]
        return out

    exchange_b = start_exchange("b", bufs_b, dx)
    bufs_a = {k: lax.empty((N_CHIPS, 1) + wk[k].shape[1:], BF16) for k in _KINDS}
    dx, bufs_a = backward(0, 0, dx, mods[0] + exchange_b[4][0, 0], bufs_a)
    g_b = finish_exchange("b", exchange_b, dx)
    grad_x, dmods = dx, jnp.stack(dmods)
    exchange_a = start_exchange("a", bufs_a, g_b[0])
    adam_b = adamw_group("b", 1, g_b, None, exchange_a[4])

    small = ([dmods] + [jnp.stack([grads[l][k] for l in range(DEPTH)]) for k in _SMALL[1:] + ("conv_w",)] + [loss[:, :1]])
    n_small = sum(int(np.prod(a.shape)) for a in small)
    rows_small = _small_rows(n_small)
    small_all = _allgather8("gather_small", _pack_flat(small, rows_small, F32))
    parts = [small_all[d * rows_small:(d + 1) * rows_small] for d in range(N_DEV)]
    small_sum = _unpack_flat(_sum_rows("sum_small", parts, F32), [a.shape for a in small])
    g_out = {"b_ada": small_sum[0].reshape(DEPTH, 6 * D_MODEL)}
    for k, t in zip(_SMALL[1:], small_sum[1:-2]):
        g_out[k] = t
    g_out["conv_w"] = lax.dynamic_slice_in_dim(small_sum[-2], chip * conv_w.shape[2], conv_w.shape[2], axis=2)
    loss_out = small_sum[-1][0, 0]
    dmod_all = jnp.stack([p[:DEPTH * 6].reshape(DEPTH, 6 * D_MODEL) for p in parts])
    dmod_cols = lax.dynamic_slice_in_dim(dmod_all, chip * ada_cols, ada_cols, axis=2)
    g_out["w_ada"] = jnp.stack([_matmul("ada_dw", cond_all, dmod_cols[:, l], "tn", precise=True) for l in range(DEPTH)])

    deltas, new_m, new_v = {}, {}, {}
    dep = adam_b[_KINDS[-1]][1]
    for k in ("w_ada", "conv_w"):
        shp = w[k].shape
        two_d = (int(np.prod(shp[:-1])), shp[-1])
        res = _adamw("adamw_" + k, *(t.reshape(two_d) for t in (w[k], g_out[k], m[k], v[k])), dep=dep)
        deltas[k], new_m[k], new_v[k] = (t.reshape(shp) for t in res[1:])
        dep = res[1]
    small_shapes = [w[k].shape for k in _SMALL]
    n_sm = sum(int(np.prod(s)) for s in small_shapes)
    res = _adamw("adamw_small", *[_pack_flat([t[k] for k in _SMALL], _small_rows(n_sm), F32) for t in (w, g_out, m, v)], dep=dep)[1:]
    for name_map, buf in zip((deltas, new_m, new_v), res):
        for k, t in zip(_SMALL, _unpack_flat(buf, small_shapes)):
            name_map[k] = t
    g_a = finish_exchange("a", exchange_a, res[0])
    for k, t in adamw_group("a", 0, g_a, adam_b, None).items():
        g_out[k], deltas[k], new_m[k], new_v[k] = (u.reshape(w[k].shape) for u in t)
    g_t = jnp.transpose(jnp.concatenate([g_a[0], g_b[0]], axis=0), (1, 0, 2))
    res = _adamw("adamw_w_in", jnp.transpose(w_in, (2, 0, 1)), g_t, jnp.transpose(m_w_in, (2, 0, 1)), jnp.transpose(v_w_in, (2, 0, 1)),
                 unit=W_IN_ADAM_ROWS)
    g_out["w_in"], deltas["w_in"], new_m["w_in"], new_v["w_in"] = (jnp.transpose(t, (1, 2, 0)) for t in res)

    return (loss_out, grad_x[None], *[g_out[k] for k in _ORDER], *[deltas[k] for k in _ORDER],
            *[new_m[k] for k in _ORDER], *[new_v[k] for k in _ORDER])
```

```python
import functools
import math

import numpy as np
import jax
import jax.numpy as jnp
from jax import lax
from jax.experimental import pallas as pl
from jax.experimental.pallas import tpu as pltpu

F32, BF16 = jnp.float32, jnp.bfloat16
MESH = pl.DeviceIdType.MESH

D_MODEL = 1024
DEPTH = 4
N_CHIPS = 4
N_DEV = 8
SSD_HEADS = 16
SSD_HEAD_DIM = 64
SSD_STATE = 128
SSD_CHUNK = 128
SSD_CONV = 4
CONV_DIM = 1536
ATTN_HEAD_DIM = 128
ATTN_GROUP_W = 512
DILATIONS = (1, 4, 16)
ATTN_STEPS = 128
POOL_WINDOWS = (2, 4, 8, 16)
POOL_GW = 256
D_FF = 4096
EPS = 1e-6
IN_SIZES = (1024, 1536, 16, 1536, 1536, 1536, 1024, 3072)
IN_WIDTH = sum(IN_SIZES)
P_XBC, P_Q, P_K, P_V, P_GATES, P_Z, P_U, P_DT = 0, 1536, 3072, 4608, 6144, 9216, 10240, 11264
P_WIDTH = 12288
LANES = 128
NEG = -1e30
VMEM_LIMIT = 56 * 1024 * 1024

ADAM_LR, ADAM_B1, ADAM_B2, ADAM_EPS, ADAM_WD, ADAM_STEP = 0.001, 0.9, 0.999, 1e-08, 0.01, 10


def _alibi_slopes(n):
    def pow2(k):
        start = 2.0 ** (-8.0 / k)
        return [start ** (i + 1) for i in range(k)]
    if math.log2(n).is_integer():
        s = pow2(n)
    else:
        c = 2 ** math.floor(math.log2(n))
        s = pow2(c) + pow2(2 * c)[0::2][: n - c]
    return np.sort(np.asarray(s, np.float32))[::-1].copy()


SLOPES = _alibi_slopes(12).reshape(3, 4)


def _cp(*sem):
    return pltpu.CompilerParams(dimension_semantics=sem, vmem_limit_bytes=VMEM_LIMIT)


_DIMS = {"nn": (((1,), (0,)), ((), ())), "nt": (((1,), (1,)), ((), ())), "tn": (((0,), (0,)), ((), ()))}


def _dot(a, b, mode):
    return lax.dot_general(a.astype(BF16), b.astype(BF16), _DIMS[mode], preferred_element_type=F32)


@functools.partial(jax.custom_vjp, nondiff_argnums=(2,))
def _bdot(a, b, mode):
    return _dot(a, b, mode)


def _bdot_fwd(a, b, mode):
    return _dot(a, b, mode), (a, b)


def _bdot_bwd(mode, res, ct):
    a, b = res
    if mode == "nn":
        return _dot(ct, b, "nt"), _dot(a, ct, "tn")
    if mode == "nt":
        return _dot(ct, b, "nn"), _dot(ct, a, "tn")
    return _dot(b, ct, "nt"), _dot(a, ct, "nn")


_bdot.defvjp(_bdot_fwd, _bdot_bwd)


def _hdot(a, b):
    return jnp.dot(a, b, precision=lax.Precision.HIGHEST, preferred_element_type=F32)


def _tri(n, lower):
    r = lax.broadcasted_iota(jnp.int32, (n, n), 0)
    c = lax.broadcasted_iota(jnp.int32, (n, n), 1)
    return (r >= c if lower else r <= c).astype(F32)


@jax.custom_vjp
def _csum(a):
    return _hdot(_tri(a.shape[0], True), a)


def _csum_fwd(a):
    return _csum(a), None


def _csum_bwd(_, ct):
    return (_hdot(_tri(ct.shape[0], False), ct),)


_csum.defvjp(_csum_fwd, _csum_bwd)


def _softplus(x):
    return jnp.maximum(x, 0.0) + jnp.log(1.0 + jnp.exp(-jnp.abs(x)))


def _sigmoid(x):
    return 1.0 / (1.0 + jnp.exp(-x))


def _silu(x):
    return x * _sigmoid(x)


def _tile(n, cap):
    t = min(n, cap)
    while n % t:
        t //= 2
    return t


MM_TILE, MM_KTILE = 1024, 2048


def _matmul(name, a, b, mode, out_dtype=F32, precise=False, layer=None, chips=0, out_chips=0, into=None):
    if mode == "nn":
        (m, k), n = a.shape, (4 * chips if chips else b.shape[-1])
    elif mode == "nt":
        (m, k), n = a.shape, b.shape[-2]
    else:
        (k, m), n = a.shape, b.shape[-1]
    tm = _tile(m // N_CHIPS if (into is not None and not out_chips) else m, MM_TILE)
    tn = _tile(chips if (chips and mode == "nn") else (out_chips or n), MM_TILE)
    tk = _tile(chips if (chips and mode == "nt") else k, MM_KTILE)
    nk = k // tk
    a_spec = pl.BlockSpec((tk, tm), lambda i, j, l: (l, i)) if mode == "tn" else pl.BlockSpec((tm, tk), lambda i, j, l: (i, l))
    if chips:
        if mode == "nn":
            per = chips // tn
            b_spec = pl.BlockSpec((None, None, tk, tn), lambda i, j, l: (layer, j // per, l, j % per))
        else:
            per = chips // tk
            b_spec = pl.BlockSpec((None, None, tn, tk), lambda i, j, l: (layer, l // per, j, l % per))
    elif layer is not None:
        b_spec = (pl.BlockSpec((None, tn, tk), lambda i, j, l: (layer, j, l)) if mode == "nt"
                  else pl.BlockSpec((None, tk, tn), lambda i, j, l: (layer, l, j)))
    else:
        b_spec = pl.BlockSpec((tn, tk), lambda i, j, l: (j, l)) if mode == "nt" else pl.BlockSpec((tk, tn), lambda i, j, l: (l, j))
    if into is not None:
        buf, slot = into
        if out_chips:
            per_o = out_chips // tn
            o_spec = pl.BlockSpec((None, None, tm, tn), lambda i, j, l: (j // per_o, slot, i, j % per_o))
        else:
            per_r = m // N_CHIPS // tm
            o_spec = pl.BlockSpec((None, None, tm, tn), lambda i, j, l: (i // per_r, slot, i % per_r, j))
        o_shape = jax.ShapeDtypeStruct(buf.shape, buf.dtype)
    elif out_chips:
        per_o = out_chips // tn
        o_spec = pl.BlockSpec((None, tm, tn), lambda i, j, l: (j // per_o, i, j % per_o))
        o_shape = jax.ShapeDtypeStruct((N_CHIPS, m, out_chips), out_dtype)
    else:
        o_spec = pl.BlockSpec((tm, tn), lambda i, j, l: (i, j))
        o_shape = jax.ShapeDtypeStruct((m, n), out_dtype)

    def part(a_ref, b_ref):
        if precise:
            return lax.dot_general(a_ref[...], b_ref[...], _DIMS[mode], precision=lax.Precision.HIGHEST,
                                   preferred_element_type=F32)
        return _dot(a_ref[...], b_ref[...], mode)

    n_in = 2 if into is None else 3

    if nk == 1:
        def body(*refs):
            o_ref = refs[n_in]
            o_ref[...] = part(refs[0], refs[1]).astype(o_ref.dtype)
        scratch = []
    else:
        def body(*refs):
            o_ref, acc_ref = refs[n_in], refs[n_in + 1]
            l = pl.program_id(2)
            p = part(refs[0], refs[1])

            @pl.when(l == 0)
            def _():
                acc_ref[...] = p

            @pl.when((l > 0) & (l < nk - 1))
            def _():
                acc_ref[...] += p

            @pl.when(l == nk - 1)
            def _():
                o_ref[...] = (acc_ref[...] + p).astype(o_ref.dtype)
        scratch = [pltpu.VMEM((tm, tn), F32)]

    extra = {} if into is None else dict(input_output_aliases={2: 0})
    return pl.pallas_call(
        body, grid=(m // tm, n // tn, nk), in_specs=[a_spec, b_spec] + ([] if into is None else [pl.BlockSpec(memory_space=pl.ANY)]),
        out_specs=o_spec, out_shape=o_shape, scratch_shapes=scratch, compiler_params=_cp("parallel", "parallel", "arbitrary"),
        name=name, **extra)(*((a, b) if into is None else (a, b, into[0])))


def _group_matmul(name, a, w, mode, out_dtype=F32, layer=0):
    s = a.shape[0]
    tb = 512
    gw = POOL_GW
    if mode == "tn":
        def body(a_ref, b_ref, o_ref):
            part = _dot(a_ref[...], b_ref[...], "tn")

            @pl.when(pl.program_id(1) == 0)
            def _():
                o_ref[0] = part

            @pl.when(pl.program_id(1) > 0)
            def _():
                o_ref[0] += part

        return pl.pallas_call(
            body, grid=(4, s // tb),
            in_specs=[pl.BlockSpec((tb, gw), lambda g, i: (i, g)), pl.BlockSpec((tb, gw), lambda g, i: (i, g))],
            out_specs=pl.BlockSpec((1, gw, gw), lambda g, i: (g, 0, 0)),
            out_shape=jax.ShapeDtypeStruct((4, gw, gw), F32),
            compiler_params=_cp("parallel", "arbitrary"), name=name)(a, w)

    def body(a_ref, w_ref, o_ref):
        o_ref[...] = _dot(a_ref[...], w_ref[...], mode).astype(o_ref.dtype)

    return pl.pallas_call(
        body, grid=(s // tb, 4),
        in_specs=[pl.BlockSpec((tb, gw), lambda i, g: (i, g)), pl.BlockSpec((None, None, gw, gw), lambda i, g: (layer, g, 0, 0))],
        out_specs=pl.BlockSpec((tb, gw), lambda i, g: (i, g)),
        out_shape=jax.ShapeDtypeStruct((s, 4 * gw), out_dtype),
        compiler_params=_cp("parallel", "parallel"), name=name)(a, w)


def _rspec(tb, width, cb):
    return pl.BlockSpec((tb, width), lambda i: (i, cb))


def _pspec(shape):
    return pl.BlockSpec(shape, lambda i: (0, 0))


def _rowwise_fwd(name, f, rows, pars, outs, tb=256):
    s = rows[0][0].shape[0]
    nin = len(rows) + len(pars)

    def body(*refs):
        res = f(*[r[...].astype(F32) for r in refs[:nin]])
        for o, v in zip(refs[nin:], res):
            o[...] = v.astype(o.dtype)

    return pl.pallas_call(
        body, grid=(s // tb,),
        in_specs=[_rspec(tb, w, cb) for _, w, cb in rows] + [_pspec(p.shape) for p in pars],
        out_specs=[_rspec(tb, w, 0) for w, _ in outs],
        out_shape=[jax.ShapeDtypeStruct((s, w), dt) for w, dt in outs],
        compiler_params=_cp("parallel"), name=name)(*[r[0] for r in rows], *pars)


def _rowwise_bwd(name, f, rows, pars, cts, need, add=None, tb=256, gdt=None):
    s = rows[0][0].shape[0]
    nr, npar, nc = len(rows), len(pars), len(cts)
    nin = nr + npar + nc + (1 if add is not None else 0)

    def body(*refs):
        ins = [r[...].astype(F32) for r in refs[:nr + npar]]
        _, vjp = jax.vjp(f, *ins)
        g = vjp(tuple(c[...].astype(F32) for c in refs[nr + npar:nr + npar + nc]))
        outs = refs[nin:]
        k = 0
        for j in range(nr):
            if need[j]:
                v = g[j]
                if add is not None and add[0] == j:
                    v = v + refs[nin - 1][...]
                outs[k][...] = v.astype(outs[k].dtype)
                k += 1
        first = pl.program_id(0) == 0
        for j in range(npar):
            o, v = outs[k + j], g[nr + j]

            @pl.when(first)
            def _(o=o, v=v):
                o[...] = v

            @pl.when(jnp.logical_not(first))
            def _(o=o, v=v):
                o[...] += v

    in_specs = ([_rspec(tb, w, cb) for _, w, cb in rows] + [_pspec(p.shape) for p in pars]
                + [_rspec(tb, w, cb) for _, w, cb in cts])
    args = [r[0] for r in rows] + list(pars) + [c[0] for c in cts]
    if add is not None:
        in_specs.append(_rspec(tb, rows[add[0]][1], 0))
        args.append(add[1])
    gr = [(w, F32) for (_, w, _), nd in zip(rows, need) if nd]
    if gdt is not None:
        gr = [(w, dt) for (w, _), dt in zip(gr, gdt)]
    return pl.pallas_call(
        body, grid=(s // tb,), in_specs=in_specs,
        out_specs=[_rspec(tb, w, 0) for w, _ in gr] + [_pspec(p.shape) for p in pars],
        out_shape=[jax.ShapeDtypeStruct((s, w), dt) for w, dt in gr] + [jax.ShapeDtypeStruct(p.shape, F32) for p in pars],
        compiler_params=_cp("arbitrary"), name=name)(*args)


def _f_norm(x, nw, sc, sh):
    r = lax.rsqrt(jnp.mean(x * x, axis=-1, keepdims=True) + EPS)
    return ((x * r * nw) * (1.0 + sc) + sh,)


def _f_ssdgate(y, z, w):
    y2 = y * _silu(z)
    low = lax.broadcasted_iota(jnp.int32, y2.shape, 1) < 512
    sq = y2 * y2
    m0 = jnp.sum(jnp.where(low, sq, 0.0), axis=-1, keepdims=True) / 512.0
    m1 = jnp.sum(jnp.where(low, 0.0, sq), axis=-1, keepdims=True) / 512.0
    r = jnp.where(low, lax.rsqrt(m0 + EPS), lax.rsqrt(m1 + EPS))
    return (y2 * r * w,)


def _head_rms(t, w):
    outs = []
    for h in range(t.shape[1] // ATTN_HEAD_DIM):
        th = t[:, h * ATTN_HEAD_DIM:(h + 1) * ATTN_HEAD_DIM]
        outs.append(th * lax.rsqrt(jnp.mean(th * th, axis=-1, keepdims=True) + EPS) * w)
    return jnp.concatenate(outs, axis=1)


def _f_qknorm(q, k, v, qw, kw):
    return _head_rms(q, qw), _head_rms(k, kw), v


def _f_combine(o1, o2, o3, l1, l2, l3):
    m = lax.stop_gradient(jnp.maximum(jnp.maximum(l1, l2), l3))
    e1, e2, e3 = jnp.exp(l1 - m), jnp.exp(l2 - m), jnp.exp(l3 - m)
    return ((e1 * o1 + e2 * o2 + e3 * o3) / (e1 + e2 + e3),)


def _f_poolscale(pm, ps):
    return (pm * ps,)


def _f_merge(gates, ys, ya, yp):
    g = _sigmoid(gates)
    return (g[:, 0:1024] * ys + g[:, 1024:2048] * ya + g[:, 2048:3072] * yp,)


def _f_resid(x, o, g):
    return (x + g * o,)


def _f_relu2(a):
    return (jnp.square(jnp.maximum(a, 0.0)),)


def _loss_and_grad(y, tgt, tb=512):
    s, d = y.shape

    def body(y_ref, t_ref, dy_ref, l_ref):
        e = y_ref[...] - t_ref[...]
        dy_ref[...] = e * (1.0 / d)
        part = jnp.zeros((1, LANES), F32) + jnp.sum(e * e) * (0.5 / d)

        @pl.when(pl.program_id(0) == 0)
        def _():
            l_ref[...] = part

        @pl.when(pl.program_id(0) > 0)
        def _():
            l_ref[...] += part

    return pl.pallas_call(
        body, grid=(s // tb,), in_specs=[_rspec(tb, d, 0), _rspec(tb, d, 0)],
        out_specs=[_rspec(tb, d, 0), _pspec((1, LANES))],
        out_shape=[jax.ShapeDtypeStruct((s, d), F32), jax.ShapeDtypeStruct((1, LANES), F32)],
        compiler_params=_cp("arbitrary"), name="loss")(y, tgt)


def _shift_down(x, j):
    rows = lax.broadcasted_iota(jnp.int32, x.shape, 0)
    return jnp.where(rows < j, 0.0, pltpu.roll(x, j, 0))


def _shift_up(x, j):
    s = x.shape[0]
    rows = lax.broadcasted_iota(jnp.int32, x.shape, 0)
    return jnp.where(rows >= s - j, 0.0, pltpu.roll(x, s - j, 0))


CONV_CB = 256


def _conv_pre(x, w_ref, b_ref):
    acc = b_ref[...] + w_ref[SSD_CONV - 1:SSD_CONV, :] * x
    for j in range(1, SSD_CONV):
        acc = acc + w_ref[SSD_CONV - 1 - j:SSD_CONV - j, :] * _shift_down(x, j)
    return acc


def _conv_fwd(proj, cw, cb):
    s = proj.shape[0]

    def body(x_ref, w_ref, b_ref, o_ref):
        o_ref[...] = _silu(_conv_pre(x_ref[...], w_ref, b_ref))

    return pl.pallas_call(
        body, grid=(CONV_DIM // CONV_CB,),
        in_specs=[pl.BlockSpec((s, CONV_CB), lambda i: (0, P_XBC // CONV_CB + i)),
                  pl.BlockSpec((SSD_CONV, CONV_CB), lambda i: (0, i)), pl.BlockSpec((1, CONV_CB), lambda i: (0, i))],
        out_specs=pl.BlockSpec((s, CONV_CB), lambda i: (0, i)),
        out_shape=jax.ShapeDtypeStruct((s, CONV_DIM), F32), compiler_params=_cp("parallel"), name="conv_fwd")(proj, cw, cb)


def _conv_bwd(proj, cw, cb, dout):
    s = proj.shape[0]

    def body(x_ref, w_ref, b_ref, d_ref, dx_ref, dw_ref, db_ref):
        x = x_ref[...]
        a = _conv_pre(x, w_ref, b_ref)
        sg = _sigmoid(a)
        da = d_ref[...] * (sg + a * sg * (1.0 - sg))
        db_ref[...] = jnp.sum(da, axis=0, keepdims=True)
        dx = w_ref[SSD_CONV - 1:SSD_CONV, :] * da
        dw_ref[SSD_CONV - 1:SSD_CONV, :] = jnp.sum(da * x, axis=0, keepdims=True)
        for j in range(1, SSD_CONV):
            dx = dx + w_ref[SSD_CONV - 1 - j:SSD_CONV - j, :] * _shift_up(da, j)
            dw_ref[SSD_CONV - 1 - j:SSD_CONV - j, :] = jnp.sum(da * _shift_down(x, j), axis=0, keepdims=True)
        dx_ref[...] = dx.astype(dx_ref.dtype)

    return pl.pallas_call(
        body, grid=(CONV_DIM // CONV_CB,),
        in_specs=[pl.BlockSpec((s, CONV_CB), lambda i: (0, P_XBC // CONV_CB + i)),
                  pl.BlockSpec((SSD_CONV, CONV_CB), lambda i: (0, i)), pl.BlockSpec((1, CONV_CB), lambda i: (0, i)),
                  pl.BlockSpec((s, CONV_CB), lambda i: (0, i))],
        out_specs=[pl.BlockSpec((s, CONV_CB), lambda i: (0, i)), pl.BlockSpec((SSD_CONV, CONV_CB), lambda i: (0, i)),
                   pl.BlockSpec((1, CONV_CB), lambda i: (0, i))],
        out_shape=[jax.ShapeDtypeStruct((s, CONV_DIM), BF16), jax.ShapeDtypeStruct((SSD_CONV, CONV_DIM), F32),
                   jax.ShapeDtypeStruct((1, CONV_DIM), F32)],
        compiler_params=_cp("parallel"), name="conv_bwd")(proj, cw, cb, dout)


def _pool_window_sum(x, g, shift):
    s2 = x + shift(x, 1)
    s4 = s2 + shift(s2, 2)
    s8 = s4 + shift(s4, 4)
    s16 = s8 + shift(s8, 8)
    return jnp.where(g == 0, s2, jnp.where(g == 1, s4, jnp.where(g == 2, s8, s16)))


def _pool_count(shape, g):
    rows = lax.broadcasted_iota(jnp.int32, shape, 0)
    return jnp.minimum(rows + 1, jnp.left_shift(2, g)).astype(F32)


def _pool_fwd(proj):
    s = proj.shape[0]

    def body(u_ref, o_ref):
        g = pl.program_id(0)
        u = u_ref[...]
        o_ref[...] = (_pool_window_sum(u, g, _shift_down) / _pool_count(u.shape, g) - u).astype(o_ref.dtype)

    return pl.pallas_call(
        body, grid=(4,), in_specs=[pl.BlockSpec((s, POOL_GW), lambda g: (0, P_U // POOL_GW + g))],
        out_specs=pl.BlockSpec((s, POOL_GW), lambda g: (0, g)),
        out_shape=jax.ShapeDtypeStruct((s, 4 * POOL_GW), BF16), compiler_params=_cp("parallel"), name="pool_fwd")(proj)


def _pool_bwd(dp):
    s = dp.shape[0]

    def body(d_ref, o_ref):
        g = pl.program_id(0)
        d = d_ref[...]
        o_ref[...] = (_pool_window_sum(d / _pool_count(d.shape, g), g, _shift_up) - d).astype(o_ref.dtype)

    return pl.pallas_call(
        body, grid=(4,), in_specs=[pl.BlockSpec((s, POOL_GW), lambda g: (0, g))],
        out_specs=pl.BlockSpec((s, POOL_GW), lambda g: (0, g)),
        out_shape=jax.ShapeDtypeStruct((s, 4 * POOL_GW), BF16), compiler_params=_cp("parallel"), name="pool_bwd")(dp)


N_PAIRS = SSD_HEADS // 2
STATE_ROWS = N_PAIRS * SSD_STATE


def _ssd_chunk(xbc, dtr, hprev, dtb, alog, dsk):
    L = xbc.shape[0]
    xs, bm, cm = xbc[:, 0:1024], xbc[:, 1024:1280], xbc[:, 1280:1536]
    dt = _softplus(dtr + dtb)
    a = dt * (-jnp.exp(alog))
    acum = _csum(a)
    alast = jnp.sum(a, axis=0, keepdims=True)
    xdt = xs * dt
    xdecay = xdt * jnp.exp(alast - acum)
    eacum = jnp.exp(acum)
    elast = jnp.exp(alast)
    cb = [_bdot(cm[:, g * 128:(g + 1) * 128], bm[:, g * 128:(g + 1) * 128], "nt") for g in range(2)]
    rows = lax.broadcasted_iota(jnp.int32, (L, L), 0)
    cols = lax.broadcasted_iota(jnp.int32, (L, L), 1)
    causal = rows >= cols
    lane = lax.broadcasted_iota(jnp.int32, (L, LANES), 1)
    sub = lax.broadcasted_iota(jnp.int32, (LANES, L), 0)
    ys, hs = [], []
    for p in range(N_PAIRS):
        g = p // (N_PAIRS // 2)
        sl = slice(p * LANES, (p + 1) * LANES)
        ac = acum[:, sl]
        act = ac.T
        xp = xdt[:, sl]
        hp = hprev[p * SSD_STATE:(p + 1) * SSD_STATE, :]
        y = _bdot(cm[:, g * 128:(g + 1) * 128], hp, "nn") * eacum[:, sl] + dsk[:, sl] * xs[:, sl]
        for half in range(2):
            l0 = half * SSD_HEAD_DIM
            col = jnp.sum(jnp.where(lane == l0, ac, 0.0), axis=1, keepdims=True)
            row = jnp.sum(jnp.where(sub == l0, act, 0.0), axis=0, keepdims=True)
            decay = jnp.exp(jnp.where(causal, col - row, NEG))
            xh = jnp.where((lane >= l0) & (lane < l0 + SSD_HEAD_DIM), xp, 0.0)
            y = y + _bdot(cb[g] * decay, xh, "nn")
        ys.append(y)
        hs.append(elast[:, sl] * hp + _bdot(bm[:, g * 128:(g + 1) * 128], xdecay[:, sl], "tn"))
    return tuple(ys), tuple(hs)


def _ssd_fwd(xbc, proj, dtb, alog, dsk):
    s = xbc.shape[0]
    nc = s // SSD_CHUNK

    def body(x_ref, dt_ref, b_ref, a_ref, d_ref, y_ref, hist_ref, h_ref):
        @pl.when(pl.program_id(0) == 0)
        def _():
            h_ref[...] = jnp.zeros_like(h_ref)

        hprev = h_ref[...]
        hist_ref[...] = hprev
        ys, hs = _ssd_chunk(x_ref[...], dt_ref[...], hprev, b_ref[...], a_ref[...], d_ref[...])
        for p in range(N_PAIRS):
            y_ref[:, p * LANES:(p + 1) * LANES] = ys[p]
            h_ref[p * SSD_STATE:(p + 1) * SSD_STATE, :] = hs[p]

    return pl.pallas_call(
        body, grid=(nc,),
        in_specs=[pl.BlockSpec((SSD_CHUNK, CONV_DIM), lambda i: (i, 0)),
                  pl.BlockSpec((SSD_CHUNK, 1024), lambda i: (i, P_DT // 1024)),
                  _pspec((1, 1024)), _pspec((1, 1024)), _pspec((1, 1024))],
        out_specs=[pl.BlockSpec((SSD_CHUNK, 1024), lambda i: (i, 0)), pl.BlockSpec((STATE_ROWS, LANES), lambda i: (i, 0))],
        out_shape=[jax.ShapeDtypeStruct((s, 1024), F32), jax.ShapeDtypeStruct((nc * STATE_ROWS, LANES), F32)],
        scratch_shapes=[pltpu.VMEM((STATE_ROWS, LANES), F32)],
        compiler_params=_cp("arbitrary"), name="ssd_fwd")(xbc, proj, dtb, alog, dsk)


def _ssd_bwd(xbc, proj, hist, dtb, alog, dsk, dy):
    s = xbc.shape[0]
    nc = s // SSD_CHUNK

    def body(x_ref, dt_ref, hist_ref, b_ref, a_ref, d_ref, dy_ref, dx_ref, ddt_ref, db_ref, da_ref, dd_ref, dh_ref):
        first = pl.program_id(0) == 0

        @pl.when(first)
        def _():
            dh_ref[...] = jnp.zeros_like(dh_ref)

        _, vjp = jax.vjp(_ssd_chunk, x_ref[...], dt_ref[...], hist_ref[...], b_ref[...], a_ref[...], d_ref[...])
        dys = tuple(dy_ref[:, p * LANES:(p + 1) * LANES] for p in range(N_PAIRS))
        dhs = tuple(dh_ref[p * SSD_STATE:(p + 1) * SSD_STATE, :] for p in range(N_PAIRS))
        dx, ddt, dhp, db, da, dd = vjp((dys, dhs))
        dx_ref[...] = dx
        ddt_ref[...] = ddt.astype(ddt_ref.dtype)
        dh_ref[...] = dhp
        for o, v in ((db_ref, db), (da_ref, da), (dd_ref, dd)):
            @pl.when(first)
            def _(o=o, v=v):
                o[...] = v

            @pl.when(jnp.logical_not(first))
            def _(o=o, v=v):
                o[...] += v

    rev = lambda i: (nc - 1 - i, 0)
    return pl.pallas_call(
        body, grid=(nc,),
        in_specs=[pl.BlockSpec((SSD_CHUNK, CONV_DIM), rev),
                  pl.BlockSpec((SSD_CHUNK, 1024), lambda i: (nc - 1 - i, P_DT // 1024)),
                  pl.BlockSpec((STATE_ROWS, LANES), rev),
                  _pspec((1, 1024)), _pspec((1, 1024)), _pspec((1, 1024)),
                  pl.BlockSpec((SSD_CHUNK, 1024), rev)],
        out_specs=[pl.BlockSpec((SSD_CHUNK, CONV_DIM), rev), pl.BlockSpec((SSD_CHUNK, 1024), rev),
                   _pspec((1, 1024)), _pspec((1, 1024)), _pspec((1, 1024))],
        out_shape=[jax.ShapeDtypeStruct((s, CONV_DIM), F32), jax.ShapeDtypeStruct((s, 1024), BF16)]
        + [jax.ShapeDtypeStruct((1, 1024), F32)] * 3,
        scratch_shapes=[pltpu.VMEM((STATE_ROWS, LANES), F32)],
        compiler_params=_cp("arbitrary"), name="ssd_bwd")(xbc, proj, hist, dtb, alog, dsk, dy)


def _attn_head(q, kp, kc, vp, vc, has_prev, slope):
    scale = ATTN_HEAD_DIM ** -0.5
    n = ATTN_STEPS
    qi = lax.broadcasted_iota(jnp.int32, (n, n), 0)
    kj = lax.broadcasted_iota(jnp.int32, (n, n), 1)
    sp = jnp.where((kj >= qi) & has_prev, _bdot(q, kp, "nt") * scale - slope * (qi + n - kj).astype(F32), NEG)
    sc = jnp.where(kj <= qi, _bdot(q, kc, "nt") * scale - slope * (qi - kj).astype(F32), NEG)
    m = lax.stop_gradient(jnp.maximum(jnp.max(sp, axis=1, keepdims=True), jnp.max(sc, axis=1, keepdims=True)))
    pp, pc = jnp.exp(sp - m), jnp.exp(sc - m)
    den = jnp.sum(pp, axis=1, keepdims=True) + jnp.sum(pc, axis=1, keepdims=True)
    o = (_bdot(pp, vp, "nn") + _bdot(pc, vc, "nn")) / den
    return o, jnp.broadcast_to(m + jnp.log(den), (n, ATTN_HEAD_DIM))


def _head_slope(gi, h):
    s = [float(v) * DILATIONS[gi] for v in SLOPES[gi]]
    return jnp.where(h == 0, s[0], jnp.where(h == 1, s[1], jnp.where(h == 2, s[2], s[3])))


ATTN_UNROLL = 4


def _attn_heads_per_block(d):
    return 4 if d == 1 else 1


def _units(ref, d, hb):
    if d == 1:
        return [ref[:, h * ATTN_HEAD_DIM:(h + 1) * ATTN_HEAD_DIM] for h in range(hb)]
    return [ref[pl.ds(r, ATTN_STEPS, stride=d), :] for r in range(d)]


def _store_units(ref, src, d, hb):
    if d == 1:
        for h in range(hb):
            ref[:, h * ATTN_HEAD_DIM:(h + 1) * ATTN_HEAD_DIM] = src[h]
    else:
        for r in range(d):
            ref[pl.ds(r, ATTN_STEPS, stride=d), :] = src[r]


def _attn_fwd(qn, kn, vv, gi):
    d = DILATIONS[gi]
    s = qn.shape[0]
    span = ATTN_STEPS * d
    nb = s // span

    hb, units = _attn_heads_per_block(d), _attn_heads_per_block(d) * d

    def body(q_ref, k_ref, v_ref, o_ref, l_ref, sq, sk, sv, so, sl):
        h0, b = pl.program_id(0) * hb, pl.program_id(1)
        cur, prev = b % 2, (b + 1) % 2

        @pl.when(b == 0)
        def _():
            sk[prev] = jnp.zeros(sk.shape[1:], F32)
            sv[prev] = jnp.zeros(sv.shape[1:], F32)

        for u, (qr, kr, vr) in enumerate(zip(_units(q_ref, d, hb), _units(k_ref, d, hb), _units(v_ref, d, hb))):
            sq[u] = qr
            sk[cur, u] = kr
            sv[cur, u] = vr

        def step(i, carry):
            for j in range(ATTN_UNROLL):
                u = i * ATTN_UNROLL + j
                so[u], sl[u] = _attn_head(sq[u], sk[prev, u], sk[cur, u], sv[prev, u], sv[cur, u], b > 0,
                                          _head_slope(gi, h0 + u // d))
            return carry

        lax.fori_loop(0, units // ATTN_UNROLL, step, 0)
        _store_units(o_ref, so, d, hb)
        _store_units(l_ref, sl, d, hb)

    blk = pl.BlockSpec((span, hb * ATTN_HEAD_DIM), lambda h, b: (b, (gi * 4) // hb + h))
    out = pl.BlockSpec((span, hb * ATTN_HEAD_DIM), lambda h, b: (b, h))
    res = (units, ATTN_STEPS, ATTN_HEAD_DIM)
    return pl.pallas_call(
        body, grid=(4 // hb, nb), in_specs=[blk, blk, blk], out_specs=[out, out],
        out_shape=[jax.ShapeDtypeStruct((s, ATTN_GROUP_W), F32)] * 2,
        scratch_shapes=[pltpu.VMEM(res, F32), pltpu.VMEM((2,) + res, F32), pltpu.VMEM((2,) + res, F32),
                        pltpu.VMEM(res, F32), pltpu.VMEM(res, F32)],
        compiler_params=_cp("parallel", "arbitrary"), name=f"attn_fwd_g{gi}")(qn, kn, vv)


def _attn_bwd(qn, kn, vv, do, dl, gi):
    d = DILATIONS[gi]
    s = qn.shape[0]
    span = ATTN_STEPS * d
    nb = s // span

    hb, units = _attn_heads_per_block(d), _attn_heads_per_block(d) * d

    def body(q_ref, kp_ref, kc_ref, vp_ref, vc_ref, do_ref, dl_ref, dq_ref, dk_ref, dv_ref, sin, sout, ck, cv):
        h0, bi = pl.program_id(0) * hb, pl.program_id(1)

        @pl.when(bi == 0)
        def _():
            ck[...] = jnp.zeros_like(ck)
            cv[...] = jnp.zeros_like(cv)

        for i, ref in enumerate((q_ref, kp_ref, kc_ref, vp_ref, vc_ref, do_ref, dl_ref)):
            for u, val in enumerate(_units(ref, d, hb)):
                sin[i, u] = val
        has_prev = bi < nb - 1

        def step(i, carry):
            for j in range(ATTN_UNROLL):
                u = i * ATTN_UNROLL + j
                f = functools.partial(_attn_head, has_prev=has_prev, slope=_head_slope(gi, h0 + u // d))
                _, vjp = jax.vjp(f, sin[0, u], sin[1, u], sin[2, u], sin[3, u], sin[4, u])
                dq, dkp, dkc, dvp, dvc = vjp((sin[5, u], sin[6, u]))
                sout[0, u] = dq
                sout[1, u] = dkc + ck[u]
                sout[2, u] = dvc + cv[u]
                ck[u] = dkp
                cv[u] = dvp
            return carry

        lax.fori_loop(0, units // ATTN_UNROLL, step, 0)
        for i, ref in enumerate((dq_ref, dk_ref, dv_ref)):
            _store_units(ref, sout.at[i], d, hb)

    w = hb * ATTN_HEAD_DIM
    cur = pl.BlockSpec((span, w), lambda h, b: (nb - 1 - b, (gi * 4) // hb + h))
    prev = pl.BlockSpec((span, w), lambda h, b: (jnp.maximum(nb - 2 - b, 0), (gi * 4) // hb + h))
    out = pl.BlockSpec((span, w), lambda h, b: (nb - 1 - b, h))
    res = (units, ATTN_STEPS, ATTN_HEAD_DIM)
    return pl.pallas_call(
        body, grid=(4 // hb, nb), in_specs=[cur, prev, cur, prev, cur, out, out], out_specs=[out, out, out],
        out_shape=[jax.ShapeDtypeStruct((s, ATTN_GROUP_W), F32)] * 3,
        scratch_shapes=[pltpu.VMEM((7,) + res, F32), pltpu.VMEM((3,) + res, F32), pltpu.VMEM(res, F32), pltpu.VMEM(res, F32)],
        compiler_params=_cp("parallel", "arbitrary"), name=f"attn_bwd_g{gi}")(qn, kn, kn, vv, vv, do, dl)


def _layer_fwd(x, mod, W, l, late=None):
    sh1, sc1, g1, sh2, sc2, g2 = (mod[i:i + 1] for i in range(6))
    (h,) = _rowwise_fwd("norm1", _f_norm, [(x, 1024, 0)], [W["norm1_w"], sc1, sh1], [(1024, BF16)])
    proj = _matmul("in_proj", h, W["w_in"], "nt")
    xbc = _conv_fwd(proj, W["conv_w"], W["conv_b"])
    y, hist = _ssd_fwd(xbc, proj, W["dt_bias"], W["a_log"], W["d_skip"])
    if late is not None:
        W.update(late(y))
    (yn,) = _rowwise_fwd("ssd_gate", _f_ssdgate, [(y, 1024, 0), (proj, 1024, P_Z // 1024)], [W["ssd_norm_w"]], [(1024, BF16)])
    y_ssd = _matmul("ssd_out", yn, W["w_ssd_out"], "nn", layer=l)
    qn, kn, vv = _rowwise_fwd("qk_norm", _f_qknorm, [(proj, 1536, P_Q // 1536), (proj, 1536, P_K // 1536), (proj, 1536, P_V // 1536)],
                              [W["q_norm_w"], W["k_norm_w"]], [(1536, F32)] * 3)
    ol = [_attn_fwd(qn, kn, vv, gi) for gi in range(3)]
    (o,) = _rowwise_fwd("attn_combine", _f_combine, [(t[0], 512, 0) for t in ol] + [(t[1], 512, 0) for t in ol], [], [(512, BF16)])
    y_attn = _matmul("attn_out", o, W["w_attn_out"], "nn", layer=l, chips=256)
    pooled = _pool_fwd(proj)
    pm = _group_matmul("pool_mix", pooled, W["w_pool_mix"], "nn", layer=l)
    (ps,) = _rowwise_fwd("pool_scale", _f_poolscale, [(pm, 1024, 0)], [W["pool_scale"]], [(1024, BF16)])
    y_pool = _matmul("pool_out", ps, W["w_pool_out"], "nn", layer=l)
    (merged,) = _rowwise_fwd("merge", _f_merge, [(proj, 3072, P_GATES // 3072), (y_ssd, 1024, 0), (y_attn, 1024, 0), (y_pool, 1024, 0)],
                             [], [(1024, BF16)])
    mo = _matmul("mix_out", merged, W["w_out"], "nn", layer=l)
    (x1,) = _rowwise_fwd("resid1", _f_resid, [(x, 1024, 0), (mo, 1024, 0)], [g1], [(1024, F32)])
    (h2,) = _rowwise_fwd("norm2", _f_norm, [(x1, 1024, 0)], [W["norm2_w"], sc2, sh2], [(1024, BF16)])
    a = _matmul("ff1", h2, W["w_ff1"], "nn", layer=l, chips=1024)
    (r,) = _rowwise_fwd("relu2", _f_relu2, [(a, D_FF, 0)], [], [(D_FF, BF16)], tb=128)
    ff = _matmul("ff2", r, W["w_ff2"], "nn", layer=l)
    (x2,) = _rowwise_fwd("resid2", _f_resid, [(x1, 1024, 0), (ff, 1024, 0)], [g2], [(1024, F32)])
    saved = dict(x=x, h=h, proj=proj, xbc=xbc, y=y, hist=hist, yn=yn, y_ssd=y_ssd, qn=qn, kn=kn, vv=vv, ol=ol, o=o,
                 y_attn=y_attn, pooled=pooled, pm=pm, ps=ps, y_pool=y_pool, merged=merged, mo=mo, x1=x1, h2=h2, a=a, r=r, ff=ff)
    return x2, saved


def _layer_bwd(dx2, mod, W, sv, l, bufs):
    sh1, sc1, g1, sh2, sc2, g2 = (mod[i:i + 1] for i in range(6))
    g = {}
    dx1a, dff, dg2 = _rowwise_bwd("resid2_bwd", _f_resid, [(sv["x1"], 1024, 0), (sv["ff"], 1024, 0)], [g2], [(dx2, 1024, 0)],
                                  [True, True], gdt=[F32, BF16])
    g["w_ff2"] = _matmul("ff2_dw", sv["r"], dff, "tn", BF16, into=(bufs["w_ff2"], l))
    dr = _matmul("ff2_dx", dff, W["w_ff2"], "nt", layer=l)
    (da,) = _rowwise_bwd("relu2_bwd", _f_relu2, [(sv["a"], D_FF, 0)], [], [(dr, D_FF, 0)], [True], tb=128, gdt=[BF16])
    g["w_ff1"] = _matmul("ff1_dw", sv["h2"], da, "tn", BF16, out_chips=1024, into=(bufs["w_ff1"], l))
    dh2 = _matmul("ff1_dx", da, W["w_ff1"], "nt", layer=l, chips=1024)
    dx1, g["norm2_w"], dsc2, dsh2 = _rowwise_bwd("norm2_bwd", _f_norm, [(sv["x1"], 1024, 0)], [W["norm2_w"], sc2, sh2],
                                                 [(dh2, 1024, 0)], [True], add=(0, dx1a))
    dxa, dmo, dg1 = _rowwise_bwd("resid1_bwd", _f_resid, [(sv["x"], 1024, 0), (sv["mo"], 1024, 0)], [g1], [(dx1, 1024, 0)],
                                 [True, True], gdt=[F32, BF16])
    g["w_out"] = _matmul("mix_out_dw", sv["merged"], dmo, "tn", BF16, into=(bufs["w_out"], l))
    dmerged = _matmul("mix_out_dx", dmo, W["w_out"], "nt", layer=l)
    proj = sv["proj"]
    dgates, dy_ssd, dy_attn, dy_pool = _rowwise_bwd(
        "merge_bwd", _f_merge, [(proj, 3072, P_GATES // 3072), (sv["y_ssd"], 1024, 0), (sv["y_attn"], 1024, 0), (sv["y_pool"], 1024, 0)],
        [], [(dmerged, 1024, 0)], [True] * 4, gdt=[BF16] * 4)
    g["w_pool_out"] = _matmul("pool_out_dw", sv["ps"], dy_pool, "tn", BF16, into=(bufs["w_pool_out"], l))
    dps = _matmul("pool_out_dx", dy_pool, W["w_pool_out"], "nt", layer=l)
    dpm, g["pool_scale"] = _rowwise_bwd("pool_scale_bwd", _f_poolscale, [(sv["pm"], 1024, 0)], [W["pool_scale"]], [(dps, 1024, 0)],
                                        [True], gdt=[BF16])
    dmix = _group_matmul("pool_mix_dw", sv["pooled"], dpm, "tn")
    g["w_pool_mix"] = bufs["w_pool_mix"].at[:, l].set(
        jnp.moveaxis(dmix.reshape(4, N_CHIPS, POOL_GW // N_CHIPS, POOL_GW), 1, 0).astype(BF16))
    dpooled = _group_matmul("pool_mix_dx", dpm, W["w_pool_mix"], "nt", layer=l)
    du = _pool_bwd(dpooled)
    g["w_attn_out"] = _matmul("attn_out_dw", sv["o"], dy_attn, "tn", BF16, out_chips=256, into=(bufs["w_attn_out"], l))
    do = _matmul("attn_out_dx", dy_attn, W["w_attn_out"], "nt", layer=l, chips=256)
    ol = sv["ol"]
    dol = _rowwise_bwd("attn_combine_bwd", _f_combine, [(t[0], 512, 0) for t in ol] + [(t[1], 512, 0) for t in ol], [],
                       [(do, 512, 0)], [True] * 6)
    dqs, dks, dvs = zip(*[_attn_bwd(sv["qn"], sv["kn"], sv["vv"], dol[gi], dol[3 + gi], gi) for gi in range(3)])
    dqn, dkn, dvv = (jnp.concatenate(t, axis=1) for t in (dqs, dks, dvs))
    dq, dk, dv, g["q_norm_w"], g["k_norm_w"] = _rowwise_bwd(
        "qk_norm_bwd", _f_qknorm, [(proj, 1536, P_Q // 1536), (proj, 1536, P_K // 1536), (proj, 1536, P_V // 1536)],
        [W["q_norm_w"], W["k_norm_w"]], [(dqn, 1536, 0), (dkn, 1536, 0), (dvv, 1536, 0)], [True] * 3, gdt=[BF16] * 3)
    g["w_ssd_out"] = _matmul("ssd_out_dw", sv["yn"], dy_ssd, "tn", BF16, into=(bufs["w_ssd_out"], l))
    dyn = _matmul("ssd_out_dx", dy_ssd, W["w_ssd_out"], "nt", layer=l)
    dy, dz, g["ssd_norm_w"] = _rowwise_bwd("ssd_gate_bwd", _f_ssdgate, [(sv["y"], 1024, 0), (proj, 1024, P_Z // 1024)], [W["ssd_norm_w"]],
                                           [(dyn, 1024, 0)], [True, True], gdt=[F32, BF16])
    dxbc, ddt, g["dt_bias"], g["a_log"], g["d_skip"] = _ssd_bwd(sv["xbc"], proj, sv["hist"], W["dt_bias"], W["a_log"], W["d_skip"], dy)
    dxbc_raw, g["conv_w"], g["conv_b"] = _conv_bwd(proj, W["conv_w"], W["conv_b"], dxbc)
    dproj = jnp.concatenate([dxbc_raw, dq, dk, dv, dgates, dz, du, ddt], axis=1)
    g["w_in"] = _matmul("in_proj_dw", dproj, sv["h"], "tn", BF16)
    dh = _matmul("in_proj_dx", dproj, W["w_in"], "nn")
    dx, g["norm1_w"], dsc1, dsh1 = _rowwise_bwd("norm1_bwd", _f_norm, [(sv["x"], 1024, 0)], [W["norm1_w"], sc1, sh1],
                                                [(dh, 1024, 0)], [True], add=(0, dxa))
    dmod = jnp.concatenate([dsh1, dsc1, dg1, dsh2, dsc2, dg2], axis=0)
    return dx, dmod, g


def _expand_heads(t):
    return jnp.repeat(t, SSD_HEAD_DIM, axis=-1)


def _reduce_heads(t):
    return t.reshape(t.shape[:-1] + (SSD_HEADS, SSD_HEAD_DIM)).sum(-1)


_IN_SPLITS = np.cumsum((0,) + IN_SIZES)


def _w_in_to_layout(wt):
    z, xbc, dt, q, k, v, u, gates = (wt[_IN_SPLITS[i]:_IN_SPLITS[i + 1]] for i in range(8))
    return jnp.concatenate([xbc, q, k, v, gates, z, u, jnp.repeat(dt, SSD_HEAD_DIM, axis=0)], axis=0)


def _w_in_from_layout(g):
    xbc, q, k, v = (g[o:o + 1536] for o in (P_XBC, P_Q, P_K, P_V))
    gates, z, u, dt = g[P_GATES:P_GATES + 3072], g[P_Z:P_Z + 1024], g[P_U:P_U + 1024], g[P_DT:P_DT + 1024]
    dt = dt.astype(F32).reshape(SSD_HEADS, SSD_HEAD_DIM, D_MODEL).sum(1).astype(g.dtype)
    return jnp.concatenate([z, xbc, dt, q, k, v, u, gates], axis=0)


_STACKED = ("w_ssd_out", "w_attn_out", "w_pool_mix", "w_pool_out", "w_out", "w_ff1", "w_ff2")
_ROWS = ("norm1_w", "norm2_w", "conv_b", "ssd_norm_w", "q_norm_w", "k_norm_w", "pool_scale")
_HEAD_ROWS = ("dt_bias", "a_log", "d_skip")


def _layer_weights(wg, lg, small, l):
    W = {k: wg[k] for k in _STACKED if k in wg}
    W["w_in"] = _w_in_to_layout(wg["w_in"][lg].reshape(IN_WIDTH, D_MODEL))
    W["conv_w"] = small["conv_w"][l]
    for k in _ROWS:
        W[k] = small[k][l][None, :]
    for k in _HEAD_ROWS:
        W[k] = _expand_heads(small[k][l])[None, :]
    return W


def _layer_grads_by_chip(g, w_in_buf, l):
    out = dict(g)
    out["w_in"] = w_in_buf.at[:, l].set(_w_in_from_layout(g["w_in"]).reshape(N_CHIPS, IN_WIDTH // N_CHIPS, D_MODEL))
    for k in _ROWS:
        out[k] = g[k][0]
    for k in _HEAD_ROWS:
        out[k] = _reduce_heads(g[k][0])
    return out


ANY = pl.BlockSpec(memory_space=pl.ANY)


def _place():
    x, y, c = lax.axis_index("x"), lax.axis_index("y"), lax.axis_index("c")
    return x, y, c, (x, y, 1 - c), [(1 - x, y), (x, 1 - y), (1 - x, 1 - y)]


def _allgather8(name, blk):
    m_per, n = blk.shape

    def body(x_ref, out_ref, send_sems, recv_sems, local_sem):
        x, y, c, sibling, chips = _place()
        me = (x, y, c)

        def rows(px, py, pc):
            return out_ref.at[pl.ds((4 * px + 2 * py + pc) * m_per, m_per), :]

        def copy(k, block, to, src=None):
            return pltpu.make_async_remote_copy(
                src_ref=rows(*block) if src is None else src, dst_ref=rows(*block),
                send_sem=send_sems.at[k], recv_sem=recv_sems.at[k], device_id=to, device_id_type=MESH)

        mine = pltpu.make_async_copy(x_ref, rows(*me), local_sem)
        mine.start()
        first = [copy(0, me, sibling, src=x_ref)]
        first += [copy(1 + j, me, (*chip, c), src=x_ref) for j, chip in enumerate(chips)]
        for cp in first:
            cp.start()
        passed = [copy(4 + j, (*chip, c), sibling) for j, chip in enumerate(chips)]
        for j, chip in enumerate(chips):
            copy(1 + j, (*chip, c), me).wait_recv()
            passed[j].start()
        copy(0, sibling, me).wait_recv()
        for j, chip in enumerate(chips):
            copy(4 + j, (*chip, 1 - c), me).wait_recv()
        for cp in first + passed:
            cp.wait_send()
        mine.wait()

    return pl.pallas_call(
        body, out_shape=jax.ShapeDtypeStruct((N_DEV * m_per, n), blk.dtype),
        in_specs=[pl.BlockSpec(memory_space=pltpu.VMEM)], out_specs=pl.BlockSpec(memory_space=pltpu.VMEM),
        scratch_shapes=[pltpu.SemaphoreType.DMA((7,)), pltpu.SemaphoreType.DMA((7,)), pltpu.SemaphoreType.DMA],
        name=name)(blk)


HBM_SPEC = pl.BlockSpec(memory_space=pltpu.HBM)
SEM_SPEC = pl.BlockSpec(memory_space=pltpu.SEMAPHORE)
SIDE_EFFECT = pltpu.SideEffectType.DATAFLOW_SIDE_EFFECTING


def _dma_sems(n):
    return [pltpu.SemaphoreType.DMA((n,)), pltpu.SemaphoreType.DMA((n,))]


def _half_axis(shape):
    return 1 if (len(shape) > 3 or (shape[1] // 2) % 16 == 0) else len(shape) - 1


def _halves(ref, axis, c):
    r2 = ref.shape[axis] // 2
    lead = (slice(None),) * axis
    return ref.at[lead + (pl.ds(r2 * c, r2),)], ref.at[lead + (pl.ds(r2 * (1 - c), r2),)]


def _gather_copies(srcs, lands, send_sems, recv_sems):
    x, y, c, _, chips = _place()
    sends, lands_here = [], []
    for j, (cx, cy) in enumerate(chips):
        for i, (s, t) in enumerate(zip(srcs, lands)):
            k = 3 * i + j
            ax = _half_axis(s.shape)
            mine = _halves(t.at[:, 2 * x + y], ax, c)[0]
            theirs = _halves(t.at[:, 2 * cx + cy], ax, c)[0]
            sends.append(pltpu.make_async_remote_copy(src_ref=_halves(s, ax, c)[0], dst_ref=mine, send_sem=send_sems.at[k],
                                                      recv_sem=recv_sems.at[k], device_id=(cx, cy, c), device_id_type=MESH))
            lands_here.append(pltpu.make_async_remote_copy(src_ref=theirs, dst_ref=theirs, send_sem=send_sems.at[k],
                                                           recv_sem=recv_sems.at[k], device_id=(cx, cy, c), device_id_type=MESH))
    return sends, lands_here


def _exchange_copies(srcs, lands, send_sems, recv_sems):
    x, y, c, _, chips = _place()
    sends = [pltpu.make_async_remote_copy(src_ref=s.at[2 * cx + cy], dst_ref=t.at[j], send_sem=send_sems.at[3 * i + j],
                                          recv_sem=recv_sems.at[3 * i + j], device_id=(cx, cy, c), device_id_type=MESH)
             for j, (cx, cy) in enumerate(chips) for i, (s, t) in enumerate(zip(srcs, lands))]
    return sends, sends


def _split_start(name, copies, srcs, lands, after):
    ns, nl = len(srcs), len(lands)
    n_copies = 3 * ns

    def body(*refs):
        send_sems, recv_sems = refs[ns + nl + 1], refs[ns + nl + 2]
        for cp in copies(refs[:ns], refs[ns:ns + nl], send_sems, recv_sems)[0]:
            cp.start()
        refs[-1][...] = jnp.zeros_like(refs[-1])

    arrs = list(srcs) + list(lands)
    res = pl.pallas_call(
        body, name=name,
        out_shape=(pltpu.SemaphoreType.DMA((n_copies,)), pltpu.SemaphoreType.DMA((n_copies,)))
        + tuple(pltpu.HBM(a.shape, a.dtype) for a in arrs) + (jax.ShapeDtypeStruct((8, LANES), F32),),
        in_specs=[HBM_SPEC] * (ns + nl) + [ANY],
        out_specs=(SEM_SPEC, SEM_SPEC) + (HBM_SPEC,) * (ns + nl) + (pl.BlockSpec(memory_space=pltpu.VMEM),),
        input_output_aliases={i: 2 + i for i in range(ns + nl)},
        compiler_params=pltpu.CompilerParams(has_side_effects=SIDE_EFFECT),
    )(*[pltpu.with_memory_space_constraint(a, pltpu.HBM) for a in arrs], after)
    return res[0], res[1], list(res[2:2 + ns]), list(res[2 + ns:2 + ns + nl]), res[-1]


def _split_wait(name, copies, send_sems, recv_sems, srcs, lands, after):
    ns, nl = len(srcs), len(lands)

    def body(*refs):
        sends, lands_here = copies(refs[:ns], refs[ns:ns + nl], refs[ns + nl], refs[ns + nl + 1])
        for cp in sends:
            cp.wait_send()
        for cp in lands_here:
            cp.wait_recv()

    arrs = list(srcs) + list(lands)
    res = pl.pallas_call(
        body, name=name, out_shape=tuple(pltpu.HBM(a.shape, a.dtype) for a in arrs),
        in_specs=[HBM_SPEC] * (ns + nl) + [SEM_SPEC, SEM_SPEC, ANY], out_specs=(HBM_SPEC,) * (ns + nl),
        input_output_aliases={i: i for i in range(ns + nl)},
        compiler_params=pltpu.CompilerParams(has_side_effects=SIDE_EFFECT),
    )(*arrs, send_sems, recv_sems, after)
    return list(res[:ns]), list(res[ns:])


def _gather_forward(name, lands):
    n = len(lands)

    def body(*refs):
        ins, outs = refs[:n], refs[n:2 * n]
        send_sems, recv_sems = refs[2 * n:]
        x, y, c, sibling, chips = _place()
        sends, arrivals = [], []
        for j, (cx, cy) in enumerate(chips):
            for i in range(n):
                k = 3 * i + j
                ax = _half_axis(ins[i].shape[:1] + ins[i].shape[2:])
                src = _halves(ins[i].at[:, 2 * cx + cy], ax, c)[0]
                dst, theirs = _halves(outs[i].at[:, 2 * cx + cy], ax, c)
                sends.append(pltpu.make_async_remote_copy(src_ref=src, dst_ref=dst, send_sem=send_sems.at[k], recv_sem=recv_sems.at[k],
                                                          device_id=sibling, device_id_type=MESH))
                arrivals.append(pltpu.make_async_remote_copy(src_ref=theirs, dst_ref=theirs, send_sem=send_sems.at[k],
                                                             recv_sem=recv_sems.at[k], device_id=sibling, device_id_type=MESH))
        for cp in sends:
            cp.start()
        for cp in arrivals:
            cp.wait_recv()
        for cp in sends:
            cp.wait_send()

    return pl.pallas_call(
        body, out_shape=[jax.ShapeDtypeStruct(t.shape, t.dtype) for t in lands], in_specs=[ANY] * n, out_specs=[ANY] * n,
        input_output_aliases={i: i for i in range(n)}, scratch_shapes=_dma_sems(3 * n), name=name)(*lands)


def _swap_halves(name, gs):
    n = len(gs)

    def body(*refs):
        ins, got = refs[:n], refs[n:2 * n]
        send_sems, recv_sems = refs[2 * n:]
        x, y, c, sibling, _ = _place()
        sends = [pltpu.make_async_remote_copy(src_ref=_halves(ins[i], 1 + _half_axis(ins[i].shape[1:]), c)[1], dst_ref=got[i],
                                              send_sem=send_sems.at[i], recv_sem=recv_sems.at[i], device_id=sibling, device_id_type=MESH)
                 for i in range(n)]
        for cp in sends:
            cp.start()
        for cp in sends:
            cp.wait_recv()
        for cp in sends:
            cp.wait_send()

    def half_shape(t):
        ax = 1 + _half_axis(t.shape[1:])
        return t.shape[:ax] + (t.shape[ax] // 2,) + t.shape[ax + 1:]

    return pl.pallas_call(
        body, out_shape=[jax.ShapeDtypeStruct(half_shape(t), t.dtype) for t in gs],
        in_specs=[ANY] * n, out_specs=[ANY] * n, scratch_shapes=_dma_sems(n), name=name)(*gs)


def _share_halves(name, ts, axes):
    n = len(ts)

    def body(*refs):
        ins, outs = refs[:n], refs[n:2 * n]
        send_sems, recv_sems = refs[2 * n:]
        x, y, c, sibling, _ = _place()
        sends, arrivals = [], []
        for i in range(n):
            mine, theirs = _halves(outs[i], axes[i], c)
            sends.append(pltpu.make_async_remote_copy(src_ref=ins[i], dst_ref=mine, send_sem=send_sems.at[i], recv_sem=recv_sems.at[i],
                                                      device_id=sibling, device_id_type=MESH))
            arrivals.append(pltpu.make_async_remote_copy(src_ref=ins[i], dst_ref=theirs, send_sem=send_sems.at[i],
                                                         recv_sem=recv_sems.at[i], device_id=sibling, device_id_type=MESH))
        for cp in sends:
            cp.start()
        for cp in arrivals:
            cp.wait_recv()
        for cp in sends:
            cp.wait_send()

    return pl.pallas_call(
        body, out_shape=[jax.ShapeDtypeStruct(t.shape[:ax] + (2 * t.shape[ax],) + t.shape[ax + 1:], t.dtype) for t, ax in zip(ts, axes)],
        in_specs=[ANY] * n, out_specs=[ANY] * n, scratch_shapes=_dma_sems(n), name=name)(*ts)


PACK_W = 1024
PACK_TB = 512


def _sum_rows(name, parts, out_dtype):
    def f(*vals):
        acc = vals[0]
        for v in vals[1:]:
            acc = acc + v
        return (acc,)

    return _rowwise_fwd(name, f, [(p, PACK_W, 0) for p in parts], [], [(PACK_W, out_dtype)], tb=_tile(parts[0].shape[0], PACK_TB))[0]


def _sum_slots(name, ops, count, out_dtype):
    mat = ops[0][0].shape[1:]

    def body(*refs):
        acc = refs[0][...].astype(F32)
        for r in refs[1:-1]:
            acc = acc + r[...].astype(F32)
        refs[-1][...] = acc.astype(refs[-1].dtype)

    return pl.pallas_call(
        body, grid=(count,), in_specs=[pl.BlockSpec((None,) + mat, lambda i, s=s: (s * count + i, 0, 0)) for _, s in ops],
        out_specs=pl.BlockSpec((None,) + mat, lambda i: (i, 0, 0)), out_shape=jax.ShapeDtypeStruct((count,) + mat, out_dtype),
        compiler_params=_cp("parallel"), name=name)(*[a for a, _ in ops])


def _adamw_update(w_ref, g_ref, m_ref, v_ref, go_ref, d_ref, mo_ref, vo_ref):
    c1 = 1.0 / (1.0 - ADAM_B1 ** ADAM_STEP)
    c2 = 1.0 / (1.0 - ADAM_B2 ** ADAM_STEP)
    gg = g_ref[...]
    mn = ADAM_B1 * m_ref[...] + (1.0 - ADAM_B1) * gg
    vn = ADAM_B2 * v_ref[...] + (1.0 - ADAM_B2) * jnp.square(gg)
    go_ref[...] = gg
    d_ref[...] = -ADAM_LR * ((mn * c1) / (jnp.sqrt(vn * c2) + ADAM_EPS) + ADAM_WD * w_ref[...])
    mo_ref[...] = mn
    vo_ref[...] = vn


def _adamw_layers(name, w, g, m, v, lo, prev=None, dep=None):
    _, r, cw = w.shape

    def body(w_ref, g_ref, m_ref, v_ref, *rest):
        _adamw_update(w_ref, g_ref, m_ref, v_ref, *rest[-4:])

    full = pl.BlockSpec((None, r, LANES), lambda l, i: (lo + l, 0, i))
    extra = ([] if prev is None else list(prev)) + ([] if dep is None else [dep])
    return pl.pallas_call(
        body, grid=(g.shape[0], cw // LANES),
        in_specs=[full, pl.BlockSpec((None, r, LANES), lambda l, i: (l, 0, i)), full, full] + [pl.BlockSpec(memory_space=pl.ANY)] * len(extra),
        out_specs=[full] * 4, out_shape=[jax.ShapeDtypeStruct(w.shape, F32)] * 4,
        input_output_aliases={4 + i: i for i in range(0 if prev is None else 4)},
        compiler_params=_cp("parallel", "parallel"), name=name)(w, g, m, v, *extra)


def _adamw(name, w, g, m, v, row0=0, unit=None, prev=None, dep=None):
    r, cw = w.shape
    tb = unit or r
    while tb * cw > 300_000 and tb % 16 == 0:
        tb //= 2
    off = row0 // tb

    def body(w_ref, g_ref, m_ref, v_ref, *rest):
        _adamw_update(w_ref, g_ref, m_ref, v_ref, *rest[-4:])

    full = pl.BlockSpec((tb, cw), lambda i: (off + i, 0))
    extra = ([] if prev is None else list(prev)) + ([] if dep is None else [dep])
    n_prev = 0 if prev is None else 4
    return pl.pallas_call(
        body, grid=(g.shape[0] // tb,),
        in_specs=[full, pl.BlockSpec((tb, cw), lambda i: (i, 0)), full, full] + [pl.BlockSpec(memory_space=pl.ANY)] * len(extra),
        out_specs=[full] * 4, out_shape=[jax.ShapeDtypeStruct((r, cw), F32)] * 4,
        input_output_aliases={4 + i: i for i in range(n_prev)},
        compiler_params=_cp("parallel"), name=name)(w, g, m, v, *extra)


def _silu_rows(c):
    def body(c_ref, o_ref):
        rows = lax.broadcasted_iota(jnp.int32, o_ref.shape, 0)
        o_ref[...] = jnp.where(rows == 0, jnp.broadcast_to(_silu(c_ref[...]), o_ref.shape), 0.0)

    return pl.pallas_call(body, out_shape=jax.ShapeDtypeStruct((8, c.shape[1]), F32), name="cond_silu")(c)


_KINDS = ("w_in", "w_ssd_out", "w_attn_out", "w_pool_mix", "w_pool_out", "w_out", "w_ff1", "w_ff2")
_SMALL = ("b_ada", "norm1_w", "norm2_w", "conv_b", "dt_bias", "a_log", "d_skip", "ssd_norm_w", "q_norm_w", "k_norm_w",
          "pool_scale")
_ORDER = ("w_ada", "b_ada", "norm1_w", "norm2_w", "w_in", "conv_w", "conv_b", "dt_bias", "a_log", "d_skip", "ssd_norm_w",
          "w_ssd_out", "q_norm_w", "k_norm_w", "w_attn_out", "w_pool_mix", "pool_scale", "w_pool_out", "w_out", "w_ff1", "w_ff2")


def _pack_flat(arrs, rows, dtype):
    flat = jnp.concatenate([a.reshape(-1).astype(dtype) for a in arrs])
    return jnp.pad(flat, (0, rows * PACK_W - flat.shape[0])).reshape(rows, PACK_W)


def _unpack_flat(buf, shapes):
    flat = buf.reshape(-1)
    out, off = [], 0
    for shp in shapes:
        n = int(np.prod(shp))
        out.append(flat[off:off + n].reshape(shp))
        off += n
    return out


def _small_rows(n_elems):
    return -(-n_elems // (8 * PACK_W)) * 8


def kernel(x, c, w_ada, b_ada, norm1_w, norm2_w, w_in, conv_w, conv_b, dt_bias, a_log, d_skip, ssd_norm_w, w_ssd_out, q_norm_w, k_norm_w, w_attn_out, w_pool_mix, pool_scale, w_pool_out, w_out, w_ff1, w_ff2, loss_target, m_w_ada, m_b_ada, m_norm1_w, m_norm2_w, m_w_in, m_conv_w, m_conv_b, m_dt_bias, m_a_log, m_d_skip, m_ssd_norm_w, m_w_ssd_out, m_q_norm_w, m_k_norm_w, m_w_attn_out, m_w_pool_mix, m_pool_scale, m_w_pool_out, m_w_out, m_w_ff1, m_w_ff2, v_w_ada, v_b_ada, v_norm1_w, v_norm2_w, v_w_in, v_conv_w, v_conv_b, v_dt_bias, v_a_log, v_d_skip, v_ssd_norm_w, v_w_ssd_out, v_q_norm_w, v_k_norm_w, v_w_attn_out, v_w_pool_mix, v_pool_scale, v_w_pool_out, v_w_out, v_w_ff1, v_w_ff2):
    w = dict(w_ada=w_ada, b_ada=b_ada, norm1_w=norm1_w, norm2_w=norm2_w, w_in=w_in, conv_w=conv_w, conv_b=conv_b, dt_bias=dt_bias, a_log=a_log, d_skip=d_skip, ssd_norm_w=ssd_norm_w, w_ssd_out=w_ssd_out, q_norm_w=q_norm_w, k_norm_w=k_norm_w, w_attn_out=w_attn_out, w_pool_mix=w_pool_mix, pool_scale=pool_scale, w_pool_out=w_pool_out, w_out=w_out, w_ff1=w_ff1, w_ff2=w_ff2)
    m = dict(w_ada=m_w_ada, b_ada=m_b_ada, norm1_w=m_norm1_w, norm2_w=m_norm2_w, w_in=m_w_in, conv_w=m_conv_w, conv_b=m_conv_b, dt_bias=m_dt_bias, a_log=m_a_log, d_skip=m_d_skip, ssd_norm_w=m_ssd_norm_w, w_ssd_out=m_w_ssd_out, q_norm_w=m_q_norm_w, k_norm_w=m_k_norm_w, w_attn_out=m_w_attn_out, w_pool_mix=m_w_pool_mix, pool_scale=m_pool_scale, w_pool_out=m_w_pool_out, w_out=m_w_out, w_ff1=m_w_ff1, w_ff2=m_w_ff2)
    v = dict(w_ada=v_w_ada, b_ada=v_b_ada, norm1_w=v_norm1_w, norm2_w=v_norm2_w, w_in=v_w_in, conv_w=v_conv_w, conv_b=v_conv_b, dt_bias=v_dt_bias, a_log=v_a_log, d_skip=v_d_skip, ssd_norm_w=v_ssd_norm_w, w_ssd_out=v_w_ssd_out, q_norm_w=v_q_norm_w, k_norm_w=v_k_norm_w, w_attn_out=v_w_attn_out, w_pool_mix=v_w_pool_mix, pool_scale=v_pool_scale, w_pool_out=v_w_pool_out, w_out=v_w_out, w_ff1=v_w_ff1, w_ff2=v_w_ff2)
    chip = 2 * lax.axis_index("x") + lax.axis_index("y")
    dev = 2 * chip + lax.axis_index("c")
    ada_cols = w_ada.shape[2]

    wk = dict({k: w[k] for k in _KINDS}, w_in=jnp.transpose(w_in, (0, 2, 1)))
    mk, vk = {"w_in": jnp.transpose(m_w_in, (0, 2, 1))}, {"w_in": jnp.transpose(v_w_in, (0, 2, 1))}
    rest = _KINDS[1:]

    def start_gather(tag, lo, n, kinds, after):
        shards = [wk[k][lo:lo + n].astype(BF16) for k in kinds]
        lands = [lax.empty((n, N_CHIPS) + s.shape[1:], BF16) for s in shards]
        return _split_start("gather_start_" + tag, _gather_copies, shards, lands, after)

    gather_a1 = start_gather("a1", 0, 1, _KINDS[:1], c)
    c = c + gather_a1[4][0, 0]

    n_conv = conv_w.size // PACK_W
    rows1 = _small_rows((1 + n_conv) * PACK_W)
    blk = jnp.concatenate([_silu_rows(c)[:1], conv_w.reshape(n_conv, PACK_W), jnp.zeros((rows1 - 1 - n_conv, PACK_W), F32)])
    first = _allgather8("gather_cond", blk).reshape(N_DEV, rows1, PACK_W)
    cond_all = first[:, 0]
    conv_all = first[0::2, 1:1 + n_conv].reshape((N_CHIPS,) + conv_w.shape)
    conv_full = jnp.moveaxis(conv_all, 0, 2).reshape(DEPTH, SSD_CONV, CONV_DIM)
    b_cols = lax.dynamic_slice_in_dim(b_ada, chip * ada_cols, ada_cols, axis=1)
    mod_cols = jnp.stack([_matmul("ada_fwd", cond_all, w_ada[l], "nn", precise=True) + b_cols[l][None, :] for l in range(DEPTH)])
    mod_all = _allgather8("gather_mod", mod_cols.reshape(-1, PACK_W)).reshape(N_DEV, DEPTH, N_DEV, ada_cols)
    mine = lax.dynamic_index_in_dim(mod_all[0::2], dev, axis=2, keepdims=False)
    mods = jnp.moveaxis(mine, 0, 1).reshape(DEPTH, 6, D_MODEL)

    core = lax.axis_index("c")
    small_w = dict({k: w[k] for k in _SMALL[1:]}, conv_w=conv_full)

    def finish_gather(tag, handle, kinds, after):
        shards, lands = _split_wait("gather_wait_" + tag, _gather_copies, handle[0], handle[1], handle[2], handle[3], after)
        lands = _gather_forward("gather_forward_" + tag, lands)
        wg = {k: lax.dynamic_update_slice_in_dim(t, s[:, None], chip, axis=1) for k, t, s in zip(kinds, lands, shards)}
        n = shards[0].shape[0]
        for k in ("w_ssd_out", "w_pool_out", "w_out", "w_ff2"):
            if k in wg:
                wg[k] = wg[k].reshape(n, -1, D_MODEL)
        if "w_pool_mix" in wg:
            wg["w_pool_mix"] = jnp.moveaxis(wg["w_pool_mix"], 1, 2).reshape(n, 4, POOL_GW, POOL_GW)
        return wg

    wg_a1 = finish_gather("a1", gather_a1, _KINDS[:1], mods)
    gather_a2 = start_gather("a2", 0, 1, rest, wg_a1["w_in"])
    gather_b = start_gather("b", 1, DEPTH - 1, _KINDS, gather_a2[4])
    mods = mods + gather_b[4][0, 0]

    xc = x[0]
    Ws, saved = [None] * DEPTH, [None] * DEPTH
    Ws[0] = _layer_weights(wg_a1, 0, small_w, 0)
    xc, saved[0] = _layer_fwd(xc, mods[0], Ws[0], 0, late=lambda y: finish_gather("a2", gather_a2, rest, y))
    wg_b = finish_gather("b", gather_b, _KINDS, xc)
    for l in range(1, DEPTH):
        Ws[l] = _layer_weights(wg_b, l - 1, small_w, l)
        xc, saved[l] = _layer_fwd(xc, mods[l], Ws[l], l - 1)
    dx, loss = _loss_and_grad(xc, loss_target[0])
    dmods, grads = [None] * DEPTH, [None] * DEPTH

    def backward(l, lg, dx, mod, bufs):
        dx, dmods[l], g = _layer_bwd(dx, mod, Ws[l], saved[l], lg, bufs)
        grads[l] = _layer_grads_by_chip(g, bufs["w_in"], lg)
        return dx, {k: grads[l][k] for k in _KINDS}

    bufs_b = {k: lax.empty((N_CHIPS, DEPTH - 1) + wk[k].shape[1:], BF16) for k in _KINDS}
    for l in reversed(range(1, DEPTH)):
        dx, bufs_b = backward(l, l - 1, dx, mods[l], bufs_b)

    def flat(t):
        return t.reshape((-1,) + t.shape[-2:])

    def start_exchange(tag, bufs, after):
        gs = [bufs[k] for k in _KINDS]
        got = _swap_halves("grad_swap_" + tag, gs)
        axes = [1 + _half_axis(t.shape[1:]) for t in gs]
        own = [lax.dynamic_slice_in_dim(t, (t.shape[ax] // 2) * core, t.shape[ax] // 2, axis=ax) for t, ax in zip(gs, axes)]
        pairs = [_sum_slots(f"sum_pair_{tag}_{k}", [(flat(a), 0), (flat(b), 0)], flat(a).shape[0], BF16).reshape(a.shape)
                 for k, a, b in zip(_KINDS, own, got)]
        lands = [lax.empty((3,) + p.shape[1:], BF16) for p in pairs]
        return _split_start("exchange_start_" + tag, _exchange_copies, pairs, lands, after)

    def finish_exchange(tag, handle, after):
        pairs, partials = _split_wait("exchange_wait_" + tag, _exchange_copies, handle[0], handle[1], handle[2], handle[3], after)
        mine = [lax.dynamic_index_in_dim(p, chip, axis=0, keepdims=False) for p in pairs]
        totals = [_sum_slots(f"sum_chips_{tag}_{k}", [(flat(a), 0)] + [(flat(p), s) for s in range(3)], flat(a).shape[0], F32).reshape(a.shape)
                  for k, a, p in zip(_KINDS, mine, partials)]
        axes = [_half_axis((1,) + wk[k].shape[1:]) for k in _KINDS]
        return [lax.dynamic_update_slice_in_dim(t, mine_t, mine_t.shape[ax] * core, axis=ax)
                for t, mine_t, ax in zip(_share_halves("grad_share_" + tag, totals, axes), totals, axes)]

    def adamw_group(tag, lo, gs, prev, dep):
        out = {}
        out["w_in"] = _adamw_layers(f"adamw_{tag}_w_in", wk["w_in"], gs[0], mk["w_in"], vk["w_in"], lo,
                                    prev=None if prev is None else prev["w_in"], dep=dep)
        dep = out["w_in"][1]
        for k, gk in zip(rest, gs[1:]):
            shp = w[k].shape
            unit = int(np.prod(shp[1:-1]))
            two_d = lambda t: t.reshape(-1, shp[-1])
            out[k] = _adamw(f"adamw_{tag}_{k}", two_d(w[k]), two_d(gk), two_d(m[k]), two_d(v[k]), row0=lo * unit, unit=unit,
                            prev=None if prev is None else prev[k], dep=dep)
            dep = out[k][1]
        return out

    exchange_b = start_exchange("b", bufs_b, dx)
    bufs_a = {k: lax.empty((N_CHIPS, 1) + wk[k].shape[1:], BF16) for k in _KINDS}
    dx, bufs_a = backward(0, 0, dx, mods[0] + exchange_b[4][0, 0], bufs_a)
    g_b = finish_exchange("b", exchange_b, dx)
    grad_x, dmods = dx, jnp.stack(dmods)
    exchange_a = start_exchange("a", bufs_a, g_b[0])
    adam_b = adamw_group("b", 1, g_b, None, exchange_a[4])

    small = ([dmods] + [jnp.stack([grads[l][k] for l in range(DEPTH)]) for k in _SMALL[1:] + ("conv_w",)] + [loss[:, :1]])
    n_small = sum(int(np.prod(a.shape)) for a in small)
    rows_small = _small_rows(n_small)
    small_all = _allgather8("gather_small", _pack_flat(small, rows_small, F32))
    parts = [small_all[d * rows_small:(d + 1) * rows_small] for d in range(N_DEV)]
    small_sum = _unpack_flat(_sum_rows("sum_small", parts, F32), [a.shape for a in small])
    g_out = {"b_ada": small_sum[0].reshape(DEPTH, 6 * D_MODEL)}
    for k, t in zip(_SMALL[1:], small_sum[1:-2]):
        g_out[k] = t
    g_out["conv_w"] = lax.dynamic_slice_in_dim(small_sum[-2], chip * conv_w.shape[2], conv_w.shape[2], axis=2)
    loss_out = small_sum[-1][0, 0]
    dmod_all = jnp.stack([p[:DEPTH * 6].reshape(DEPTH, 6 * D_MODEL) for p in parts])
    dmod_cols = lax.dynamic_slice_in_dim(dmod_all, chip * ada_cols, ada_cols, axis=2)
    g_out["w_ada"] = jnp.stack([_matmul("ada_dw", cond_all, dmod_cols[:, l], "tn", precise=True) for l in range(DEPTH)])

    deltas, new_m, new_v = {}, {}, {}
    dep = adam_b[_KINDS[-1]][1]
    for k in ("w_ada", "conv_w"):
        shp = w[k].shape
        two_d = (int(np.prod(shp[:-1])), shp[-1])
        res = _adamw("adamw_" + k, *(t.reshape(two_d) for t in (w[k], g_out[k], m[k], v[k])), dep=dep)
        deltas[k], new_m[k], new_v[k] = (t.reshape(shp) for t in res[1:])
        dep = res[1]
    small_shapes = [w[k].shape for k in _SMALL]
    n_sm = sum(int(np.prod(s)) for s in small_shapes)
    res = _adamw("adamw_small", *[_pack_flat([t[k] for k in _SMALL], _small_rows(n_sm), F32) for t in (w, g_out, m, v)], dep=dep)[1:]
    for name_map, buf in zip((deltas, new_m, new_v), res):
        for k, t in zip(_SMALL, _unpack_flat(buf, small_shapes)):
            name_map[k] = t
    g_a = finish_exchange("a", exchange_a, res[0])
    for k, t in adamw_group("a", 0, g_a, adam_b, None).items():
        if k == "w_in":
            g_out[k], deltas[k], new_m[k], new_v[k] = (jnp.transpose(u, (0, 2, 1)) for u in t)
        else:
            g_out[k], deltas[k], new_m[k], new_v[k] = (u.reshape(w[k].shape) for u in t)

    return (loss_out, grad_x[None], *[g_out[k] for k in _ORDER], *[deltas[k] for k in _ORDER],
            *[new_m[k] for k in _ORDER], *[new_v[k] for k in _ORDER])
```

```python
import functools
import math

import numpy as np
import jax
import jax.numpy as jnp
from jax import lax
from jax.experimental import pallas as pl
from jax.experimental.pallas import tpu as pltpu

F32, BF16 = jnp.float32, jnp.bfloat16
MESH = pl.DeviceIdType.MESH

D_MODEL = 1024
DEPTH = 4
N_CHIPS = 4
N_DEV = 8
SSD_HEADS = 16
SSD_HEAD_DIM = 64
SSD_STATE = 128
SSD_CHUNK = 128
SSD_CONV = 4
CONV_DIM = 1536
ATTN_HEAD_DIM = 128
ATTN_GROUP_W = 512
DILATIONS = (1, 4, 16)
ATTN_STEPS = 128
POOL_WINDOWS = (2, 4, 8, 16)
POOL_GW = 256
D_FF = 4096
EPS = 1e-6
IN_SIZES = (1024, 1536, 16, 1536, 1536, 1536, 1024, 3072)
IN_WIDTH = sum(IN_SIZES)
P_XBC, P_Q, P_K, P_V, P_GATES, P_Z, P_U, P_DT = 0, 1536, 3072, 4608, 6144, 9216, 10240, 11264
P_WIDTH = 12288
LANES = 128
NEG = -1e30
VMEM_LIMIT = 56 * 1024 * 1024

ADAM_LR, ADAM_B1, ADAM_B2, ADAM_EPS, ADAM_WD, ADAM_STEP = 0.001, 0.9, 0.999, 1e-08, 0.01, 10


def _alibi_slopes(n):
    def pow2(k):
        start = 2.0 ** (-8.0 / k)
        return [start ** (i + 1) for i in range(k)]
    if math.log2(n).is_integer():
        s = pow2(n)
    else:
        c = 2 ** math.floor(math.log2(n))
        s = pow2(c) + pow2(2 * c)[0::2][: n - c]
    return np.sort(np.asarray(s, np.float32))[::-1].copy()


SLOPES = _alibi_slopes(12).reshape(3, 4)


def _cp(*sem):
    return pltpu.CompilerParams(dimension_semantics=sem, vmem_limit_bytes=VMEM_LIMIT)


_DIMS = {"nn": (((1,), (0,)), ((), ())), "nt": (((1,), (1,)), ((), ())), "tn": (((0,), (0,)), ((), ()))}


def _dot(a, b, mode):
    return lax.dot_general(a.astype(BF16), b.astype(BF16), _DIMS[mode], preferred_element_type=F32)


@functools.partial(jax.custom_vjp, nondiff_argnums=(2,))
def _bdot(a, b, mode):
    return _dot(a, b, mode)


def _bdot_fwd(a, b, mode):
    return _dot(a, b, mode), (a, b)


def _bdot_bwd(mode, res, ct):
    a, b = res
    if mode == "nn":
        return _dot(ct, b, "nt"), _dot(a, ct, "tn")
    if mode == "nt":
        return _dot(ct, b, "nn"), _dot(ct, a, "tn")
    return _dot(b, ct, "nt"), _dot(a, ct, "nn")


_bdot.defvjp(_bdot_fwd, _bdot_bwd)


def _hdot(a, b):
    return jnp.dot(a, b, precision=lax.Precision.HIGHEST, preferred_element_type=F32)


def _tri(n, lower):
    r = lax.broadcasted_iota(jnp.int32, (n, n), 0)
    c = lax.broadcasted_iota(jnp.int32, (n, n), 1)
    return (r >= c if lower else r <= c).astype(F32)


@jax.custom_vjp
def _csum(a):
    return _hdot(_tri(a.shape[0], True), a)


def _csum_fwd(a):
    return _csum(a), None


def _csum_bwd(_, ct):
    return (_hdot(_tri(ct.shape[0], False), ct),)


_csum.defvjp(_csum_fwd, _csum_bwd)


def _softplus(x):
    return jnp.maximum(x, 0.0) + jnp.log(1.0 + jnp.exp(-jnp.abs(x)))


def _sigmoid(x):
    return 1.0 / (1.0 + jnp.exp(-x))


def _silu(x):
    return x * _sigmoid(x)


def _tile(n, cap):
    t = min(n, cap)
    while n % t:
        t //= 2
    return t


MM_TILE, MM_KTILE = 1024, 2048


def _matmul(name, a, b, mode, out_dtype=F32, precise=False, layer=None, chips=0, out_chips=0, into=None):
    if mode == "nn":
        (m, k), n = a.shape, (4 * chips if chips else b.shape[-1])
    elif mode == "nt":
        (m, k), n = a.shape, b.shape[-2]
    else:
        (k, m), n = a.shape, b.shape[-1]
    tm = _tile(m // N_CHIPS if (into is not None and not out_chips) else m, MM_TILE)
    tn = _tile(chips if (chips and mode == "nn") else (out_chips or n), MM_TILE)
    tk = _tile(chips if (chips and mode == "nt") else k, MM_KTILE)
    nk = k // tk
    a_spec = pl.BlockSpec((tk, tm), lambda i, j, l: (l, i)) if mode == "tn" else pl.BlockSpec((tm, tk), lambda i, j, l: (i, l))
    if chips:
        if mode == "nn":
            per = chips // tn
            b_spec = pl.BlockSpec((None, None, tk, tn), lambda i, j, l: (layer, j // per, l, j % per))
        else:
            per = chips // tk
            b_spec = pl.BlockSpec((None, None, tn, tk), lambda i, j, l: (layer, l // per, j, l % per))
    elif layer is not None:
        b_spec = (pl.BlockSpec((None, tn, tk), lambda i, j, l: (layer, j, l)) if mode == "nt"
                  else pl.BlockSpec((None, tk, tn), lambda i, j, l: (layer, l, j)))
    else:
        b_spec = pl.BlockSpec((tn, tk), lambda i, j, l: (j, l)) if mode == "nt" else pl.BlockSpec((tk, tn), lambda i, j, l: (l, j))
    if into is not None:
        buf, slot = into
        if out_chips:
            per_o = out_chips // tn
            o_spec = pl.BlockSpec((None, None, tm, tn), lambda i, j, l: (j // per_o, slot, i, j % per_o))
        else:
            per_r = m // N_CHIPS // tm
            o_spec = pl.BlockSpec((None, None, tm, tn), lambda i, j, l: (i // per_r, slot, i % per_r, j))
        o_shape = jax.ShapeDtypeStruct(buf.shape, buf.dtype)
    elif out_chips:
        per_o = out_chips // tn
        o_spec = pl.BlockSpec((None, tm, tn), lambda i, j, l: (j // per_o, i, j % per_o))
        o_shape = jax.ShapeDtypeStruct((N_CHIPS, m, out_chips), out_dtype)
    else:
        o_spec = pl.BlockSpec((tm, tn), lambda i, j, l: (i, j))
        o_shape = jax.ShapeDtypeStruct((m, n), out_dtype)

    def part(a_ref, b_ref):
        if precise:
            return lax.dot_general(a_ref[...], b_ref[...], _DIMS[mode], precision=lax.Precision.HIGHEST,
                                   preferred_element_type=F32)
        return _dot(a_ref[...], b_ref[...], mode)

    n_in = 2 if into is None else 3

    if nk == 1:
        def body(*refs):
            o_ref = refs[n_in]
            o_ref[...] = part(refs[0], refs[1]).astype(o_ref.dtype)
        scratch = []
    else:
        def body(*refs):
            o_ref, acc_ref = refs[n_in], refs[n_in + 1]
            l = pl.program_id(2)
            p = part(refs[0], refs[1])

            @pl.when(l == 0)
            def _():
                acc_ref[...] = p

            @pl.when((l > 0) & (l < nk - 1))
            def _():
                acc_ref[...] += p

            @pl.when(l == nk - 1)
            def _():
                o_ref[...] = (acc_ref[...] + p).astype(o_ref.dtype)
        scratch = [pltpu.VMEM((tm, tn), F32)]

    extra = {} if into is None else dict(input_output_aliases={2: 0})
    return pl.pallas_call(
        body, grid=(m // tm, n // tn, nk), in_specs=[a_spec, b_spec] + ([] if into is None else [pl.BlockSpec(memory_space=pl.ANY)]),
        out_specs=o_spec, out_shape=o_shape, scratch_shapes=scratch, compiler_params=_cp("parallel", "parallel", "arbitrary"),
        name=name, **extra)(*((a, b) if into is None else (a, b, into[0])))


def _group_matmul(name, a, w, mode, out_dtype=F32, layer=0):
    s = a.shape[0]
    tb = 512
    gw = POOL_GW
    if mode == "tn":
        def body(a_ref, b_ref, o_ref):
            part = _dot(a_ref[...], b_ref[...], "tn")

            @pl.when(pl.program_id(1) == 0)
            def _():
                o_ref[0] = part

            @pl.when(pl.program_id(1) > 0)
            def _():
                o_ref[0] += part

        return pl.pallas_call(
            body, grid=(4, s // tb),
            in_specs=[pl.BlockSpec((tb, gw), lambda g, i: (i, g)), pl.BlockSpec((tb, gw), lambda g, i: (i, g))],
            out_specs=pl.BlockSpec((1, gw, gw), lambda g, i: (g, 0, 0)),
            out_shape=jax.ShapeDtypeStruct((4, gw, gw), F32),
            compiler_params=_cp("parallel", "arbitrary"), name=name)(a, w)

    def body(a_ref, w_ref, o_ref):
        o_ref[...] = _dot(a_ref[...], w_ref[...], mode).astype(o_ref.dtype)

    return pl.pallas_call(
        body, grid=(s // tb, 4),
        in_specs=[pl.BlockSpec((tb, gw), lambda i, g: (i, g)), pl.BlockSpec((None, None, gw, gw), lambda i, g: (layer, g, 0, 0))],
        out_specs=pl.BlockSpec((tb, gw), lambda i, g: (i, g)),
        out_shape=jax.ShapeDtypeStruct((s, 4 * gw), out_dtype),
        compiler_params=_cp("parallel", "parallel"), name=name)(a, w)


def _rspec(tb, width, cb):
    return pl.BlockSpec((tb, width), lambda i: (i, cb))


def _pspec(shape):
    return pl.BlockSpec(shape, lambda i: (0, 0))


def _rowwise_fwd(name, f, rows, pars, outs, tb=256):
    s = rows[0][0].shape[0]
    nin = len(rows) + len(pars)

    def body(*refs):
        res = f(*[r[...].astype(F32) for r in refs[:nin]])
        for o, v in zip(refs[nin:], res):
            o[...] = v.astype(o.dtype)

    return pl.pallas_call(
        body, grid=(s // tb,),
        in_specs=[_rspec(tb, w, cb) for _, w, cb in rows] + [_pspec(p.shape) for p in pars],
        out_specs=[_rspec(tb, w, 0) for w, _ in outs],
        out_shape=[jax.ShapeDtypeStruct((s, w), dt) for w, dt in outs],
        compiler_params=_cp("parallel"), name=name)(*[r[0] for r in rows], *pars)


def _rowwise_bwd(name, f, rows, pars, cts, need, add=None, tb=256, gdt=None):
    s = rows[0][0].shape[0]
    nr, npar, nc = len(rows), len(pars), len(cts)
    nin = nr + npar + nc + (1 if add is not None else 0)

    def body(*refs):
        ins = [r[...].astype(F32) for r in refs[:nr + npar]]
        _, vjp = jax.vjp(f, *ins)
        g = vjp(tuple(c[...].astype(F32) for c in refs[nr + npar:nr + npar + nc]))
        outs = refs[nin:]
        k = 0
        for j in range(nr):
            if need[j]:
                v = g[j]
                if add is not None and add[0] == j:
                    v = v + refs[nin - 1][...]
                outs[k][...] = v.astype(outs[k].dtype)
                k += 1
        first = pl.program_id(0) == 0
        for j in range(npar):
            o, v = outs[k + j], g[nr + j]

            @pl.when(first)
            def _(o=o, v=v):
                o[...] = v

            @pl.when(jnp.logical_not(first))
            def _(o=o, v=v):
                o[...] += v

    in_specs = ([_rspec(tb, w, cb) for _, w, cb in rows] + [_pspec(p.shape) for p in pars]
                + [_rspec(tb, w, cb) for _, w, cb in cts])
    args = [r[0] for r in rows] + list(pars) + [c[0] for c in cts]
    if add is not None:
        in_specs.append(_rspec(tb, rows[add[0]][1], 0))
        args.append(add[1])
    gr = [(w, F32) for (_, w, _), nd in zip(rows, need) if nd]
    if gdt is not None:
        gr = [(w, dt) for (w, _), dt in zip(gr, gdt)]
    return pl.pallas_call(
        body, grid=(s // tb,), in_specs=in_specs,
        out_specs=[_rspec(tb, w, 0) for w, _ in gr] + [_pspec(p.shape) for p in pars],
        out_shape=[jax.ShapeDtypeStruct((s, w), dt) for w, dt in gr] + [jax.ShapeDtypeStruct(p.shape, F32) for p in pars],
        compiler_params=_cp("arbitrary"), name=name)(*args)


def _f_norm(x, nw, sc, sh):
    r = lax.rsqrt(jnp.mean(x * x, axis=-1, keepdims=True) + EPS)
    return ((x * r * nw) * (1.0 + sc) + sh,)


def _f_ssdgate(y, z, w):
    y2 = y * _silu(z)
    low = lax.broadcasted_iota(jnp.int32, y2.shape, 1) < 512
    sq = y2 * y2
    m0 = jnp.sum(jnp.where(low, sq, 0.0), axis=-1, keepdims=True) / 512.0
    m1 = jnp.sum(jnp.where(low, 0.0, sq), axis=-1, keepdims=True) / 512.0
    r = jnp.where(low, lax.rsqrt(m0 + EPS), lax.rsqrt(m1 + EPS))
    return (y2 * r * w,)


def _head_rms(t, w):
    outs = []
    for h in range(t.shape[1] // ATTN_HEAD_DIM):
        th = t[:, h * ATTN_HEAD_DIM:(h + 1) * ATTN_HEAD_DIM]
        outs.append(th * lax.rsqrt(jnp.mean(th * th, axis=-1, keepdims=True) + EPS) * w)
    return jnp.concatenate(outs, axis=1)


def _f_qknorm(q, k, v, qw, kw):
    return _head_rms(q, qw), _head_rms(k, kw), v


def _f_combine(o1, o2, o3, l1, l2, l3):
    m = lax.stop_gradient(jnp.maximum(jnp.maximum(l1, l2), l3))
    e1, e2, e3 = jnp.exp(l1 - m), jnp.exp(l2 - m), jnp.exp(l3 - m)
    return ((e1 * o1 + e2 * o2 + e3 * o3) / (e1 + e2 + e3),)


def _f_poolscale(pm, ps):
    return (pm * ps,)


def _f_merge(gates, ys, ya, yp):
    g = _sigmoid(gates)
    return (g[:, 0:1024] * ys + g[:, 1024:2048] * ya + g[:, 2048:3072] * yp,)


def _f_resid(x, o, g):
    return (x + g * o,)


def _f_relu2(a):
    return (jnp.square(jnp.maximum(a, 0.0)),)


def _loss_and_grad(y, tgt, tb=512):
    s, d = y.shape

    def body(y_ref, t_ref, dy_ref, l_ref):
        e = y_ref[...] - t_ref[...]
        dy_ref[...] = e * (1.0 / d)
        part = jnp.zeros((1, LANES), F32) + jnp.sum(e * e) * (0.5 / d)

        @pl.when(pl.program_id(0) == 0)
        def _():
            l_ref[...] = part

        @pl.when(pl.program_id(0) > 0)
        def _():
            l_ref[...] += part

    return pl.pallas_call(
        body, grid=(s // tb,), in_specs=[_rspec(tb, d, 0), _rspec(tb, d, 0)],
        out_specs=[_rspec(tb, d, 0), _pspec((1, LANES))],
        out_shape=[jax.ShapeDtypeStruct((s, d), F32), jax.ShapeDtypeStruct((1, LANES), F32)],
        compiler_params=_cp("arbitrary"), name="loss")(y, tgt)


def _shift_down(x, j):
    rows = lax.broadcasted_iota(jnp.int32, x.shape, 0)
    return jnp.where(rows < j, 0.0, pltpu.roll(x, j, 0))


def _shift_up(x, j):
    s = x.shape[0]
    rows = lax.broadcasted_iota(jnp.int32, x.shape, 0)
    return jnp.where(rows >= s - j, 0.0, pltpu.roll(x, s - j, 0))


CONV_CB = 256


def _conv_pre(x, w_ref, b_ref):
    acc = b_ref[...] + w_ref[SSD_CONV - 1:SSD_CONV, :] * x
    for j in range(1, SSD_CONV):
        acc = acc + w_ref[SSD_CONV - 1 - j:SSD_CONV - j, :] * _shift_down(x, j)
    return acc


def _conv_fwd(proj, cw, cb):
    s = proj.shape[0]

    def body(x_ref, w_ref, b_ref, o_ref):
        o_ref[...] = _silu(_conv_pre(x_ref[...], w_ref, b_ref))

    return pl.pallas_call(
        body, grid=(CONV_DIM // CONV_CB,),
        in_specs=[pl.BlockSpec((s, CONV_CB), lambda i: (0, P_XBC // CONV_CB + i)),
                  pl.BlockSpec((SSD_CONV, CONV_CB), lambda i: (0, i)), pl.BlockSpec((1, CONV_CB), lambda i: (0, i))],
        out_specs=pl.BlockSpec((s, CONV_CB), lambda i: (0, i)),
        out_shape=jax.ShapeDtypeStruct((s, CONV_DIM), F32), compiler_params=_cp("parallel"), name="conv_fwd")(proj, cw, cb)


def _conv_bwd(proj, cw, cb, dout):
    s = proj.shape[0]

    def body(x_ref, w_ref, b_ref, d_ref, dx_ref, dw_ref, db_ref):
        x = x_ref[...]
        a = _conv_pre(x, w_ref, b_ref)
        sg = _sigmoid(a)
        da = d_ref[...] * (sg + a * sg * (1.0 - sg))
        db_ref[...] = jnp.sum(da, axis=0, keepdims=True)
        dx = w_ref[SSD_CONV - 1:SSD_CONV, :] * da
        dw_ref[SSD_CONV - 1:SSD_CONV, :] = jnp.sum(da * x, axis=0, keepdims=True)
        for j in range(1, SSD_CONV):
            dx = dx + w_ref[SSD_CONV - 1 - j:SSD_CONV - j, :] * _shift_up(da, j)
            dw_ref[SSD_CONV - 1 - j:SSD_CONV - j, :] = jnp.sum(da * _shift_down(x, j), axis=0, keepdims=True)
        dx_ref[...] = dx.astype(dx_ref.dtype)

    return pl.pallas_call(
        body, grid=(CONV_DIM // CONV_CB,),
        in_specs=[pl.BlockSpec((s, CONV_CB), lambda i: (0, P_XBC // CONV_CB + i)),
                  pl.BlockSpec((SSD_CONV, CONV_CB), lambda i: (0, i)), pl.BlockSpec((1, CONV_CB), lambda i: (0, i)),
                  pl.BlockSpec((s, CONV_CB), lambda i: (0, i))],
        out_specs=[pl.BlockSpec((s, CONV_CB), lambda i: (0, i)), pl.BlockSpec((SSD_CONV, CONV_CB), lambda i: (0, i)),
                   pl.BlockSpec((1, CONV_CB), lambda i: (0, i))],
        out_shape=[jax.ShapeDtypeStruct((s, CONV_DIM), BF16), jax.ShapeDtypeStruct((SSD_CONV, CONV_DIM), F32),
                   jax.ShapeDtypeStruct((1, CONV_DIM), F32)],
        compiler_params=_cp("parallel"), name="conv_bwd")(proj, cw, cb, dout)


def _pool_window_sum(x, g, shift):
    s2 = x + shift(x, 1)
    s4 = s2 + shift(s2, 2)
    s8 = s4 + shift(s4, 4)
    s16 = s8 + shift(s8, 8)
    return jnp.where(g == 0, s2, jnp.where(g == 1, s4, jnp.where(g == 2, s8, s16)))


def _pool_count(shape, g):
    rows = lax.broadcasted_iota(jnp.int32, shape, 0)
    return jnp.minimum(rows + 1, jnp.left_shift(2, g)).astype(F32)


def _pool_fwd(proj):
    s = proj.shape[0]

    def body(u_ref, o_ref):
        g = pl.program_id(0)
        u = u_ref[...]
        o_ref[...] = (_pool_window_sum(u, g, _shift_down) / _pool_count(u.shape, g) - u).astype(o_ref.dtype)

    return pl.pallas_call(
        body, grid=(4,), in_specs=[pl.BlockSpec((s, POOL_GW), lambda g: (0, P_U // POOL_GW + g))],
        out_specs=pl.BlockSpec((s, POOL_GW), lambda g: (0, g)),
        out_shape=jax.ShapeDtypeStruct((s, 4 * POOL_GW), BF16), compiler_params=_cp("parallel"), name="pool_fwd")(proj)


def _pool_bwd(dp):
    s = dp.shape[0]

    def body(d_ref, o_ref):
        g = pl.program_id(0)
        d = d_ref[...]
        o_ref[...] = (_pool_window_sum(d / _pool_count(d.shape, g), g, _shift_up) - d).astype(o_ref.dtype)

    return pl.pallas_call(
        body, grid=(4,), in_specs=[pl.BlockSpec((s, POOL_GW), lambda g: (0, g))],
        out_specs=pl.BlockSpec((s, POOL_GW), lambda g: (0, g)),
        out_shape=jax.ShapeDtypeStruct((s, 4 * POOL_GW), BF16), compiler_params=_cp("parallel"), name="pool_bwd")(dp)


N_PAIRS = SSD_HEADS // 2
STATE_ROWS = N_PAIRS * SSD_STATE


def _ssd_chunk(xbc, dtr, hprev, dtb, alog, dsk):
    L = xbc.shape[0]
    xs, bm, cm = xbc[:, 0:1024], xbc[:, 1024:1280], xbc[:, 1280:1536]
    dt = _softplus(dtr + dtb)
    a = dt * (-jnp.exp(alog))
    acum = _csum(a)
    alast = jnp.sum(a, axis=0, keepdims=True)
    xdt = xs * dt
    xdecay = xdt * jnp.exp(alast - acum)
    eacum = jnp.exp(acum)
    elast = jnp.exp(alast)
    cb = [_bdot(cm[:, g * 128:(g + 1) * 128], bm[:, g * 128:(g + 1) * 128], "nt") for g in range(2)]
    rows = lax.broadcasted_iota(jnp.int32, (L, L), 0)
    cols = lax.broadcasted_iota(jnp.int32, (L, L), 1)
    causal = rows >= cols
    lane = lax.broadcasted_iota(jnp.int32, (L, LANES), 1)
    sub = lax.broadcasted_iota(jnp.int32, (LANES, L), 0)
    ys, hs = [], []
    for p in range(N_PAIRS):
        g = p // (N_PAIRS // 2)
        sl = slice(p * LANES, (p + 1) * LANES)
        ac = acum[:, sl]
        act = ac.T
        xp = xdt[:, sl]
        hp = hprev[p * SSD_STATE:(p + 1) * SSD_STATE, :]
        y = _bdot(cm[:, g * 128:(g + 1) * 128], hp, "nn") * eacum[:, sl] + dsk[:, sl] * xs[:, sl]
        for half in range(2):
            l0 = half * SSD_HEAD_DIM
            col = jnp.sum(jnp.where(lane == l0, ac, 0.0), axis=1, keepdims=True)
            row = jnp.sum(jnp.where(sub == l0, act, 0.0), axis=0, keepdims=True)
            decay = jnp.exp(jnp.where(causal, col - row, NEG))
            xh = jnp.where((lane >= l0) & (lane < l0 + SSD_HEAD_DIM), xp, 0.0)
            y = y + _bdot(cb[g] * decay, xh, "nn")
        ys.append(y)
        hs.append(elast[:, sl] * hp + _bdot(bm[:, g * 128:(g + 1) * 128], xdecay[:, sl], "tn"))
    return tuple(ys), tuple(hs)


def _ssd_fwd(xbc, proj, dtb, alog, dsk):
    s = xbc.shape[0]
    nc = s // SSD_CHUNK

    def body(x_ref, dt_ref, b_ref, a_ref, d_ref, y_ref, hist_ref, h_ref):
        @pl.when(pl.program_id(0) == 0)
        def _():
            h_ref[...] = jnp.zeros_like(h_ref)

        hprev = h_ref[...]
        hist_ref[...] = hprev
        ys, hs = _ssd_chunk(x_ref[...], dt_ref[...], hprev, b_ref[...], a_ref[...], d_ref[...])
        for p in range(N_PAIRS):
            y_ref[:, p * LANES:(p + 1) * LANES] = ys[p]
            h_ref[p * SSD_STATE:(p + 1) * SSD_STATE, :] = hs[p]

    return pl.pallas_call(
        body, grid=(nc,),
        in_specs=[pl.BlockSpec((SSD_CHUNK, CONV_DIM), lambda i: (i, 0)),
                  pl.BlockSpec((SSD_CHUNK, 1024), lambda i: (i, P_DT // 1024)),
                  _pspec((1, 1024)), _pspec((1, 1024)), _pspec((1, 1024))],
        out_specs=[pl.BlockSpec((SSD_CHUNK, 1024), lambda i: (i, 0)), pl.BlockSpec((STATE_ROWS, LANES), lambda i: (i, 0))],
        out_shape=[jax.ShapeDtypeStruct((s, 1024), F32), jax.ShapeDtypeStruct((nc * STATE_ROWS, LANES), F32)],
        scratch_shapes=[pltpu.VMEM((STATE_ROWS, LANES), F32)],
        compiler_params=_cp("arbitrary"), name="ssd_fwd")(xbc, proj, dtb, alog, dsk)


def _ssd_bwd(xbc, proj, hist, dtb, alog, dsk, dy):
    s = xbc.shape[0]
    nc = s // SSD_CHUNK

    def body(x_ref, dt_ref, hist_ref, b_ref, a_ref, d_ref, dy_ref, dx_ref, ddt_ref, db_ref, da_ref, dd_ref, dh_ref):
        first = pl.program_id(0) == 0

        @pl.when(first)
        def _():
            dh_ref[...] = jnp.zeros_like(dh_ref)

        _, vjp = jax.vjp(_ssd_chunk, x_ref[...], dt_ref[...], hist_ref[...], b_ref[...], a_ref[...], d_ref[...])
        dys = tuple(dy_ref[:, p * LANES:(p + 1) * LANES] for p in range(N_PAIRS))
        dhs = tuple(dh_ref[p * SSD_STATE:(p + 1) * SSD_STATE, :] for p in range(N_PAIRS))
        dx, ddt, dhp, db, da, dd = vjp((dys, dhs))
        dx_ref[...] = dx
        ddt_ref[...] = ddt.astype(ddt_ref.dtype)
        dh_ref[...] = dhp
        for o, v in ((db_ref, db), (da_ref, da), (dd_ref, dd)):
            @pl.when(first)
            def _(o=o, v=v):
                o[...] = v

            @pl.when(jnp.logical_not(first))
            def _(o=o, v=v):
                o[...] += v

    rev = lambda i: (nc - 1 - i, 0)
    return pl.pallas_call(
        body, grid=(nc,),
        in_specs=[pl.BlockSpec((SSD_CHUNK, CONV_DIM), rev),
                  pl.BlockSpec((SSD_CHUNK, 1024), lambda i: (nc - 1 - i, P_DT // 1024)),
                  pl.BlockSpec((STATE_ROWS, LANES), rev),
                  _pspec((1, 1024)), _pspec((1, 1024)), _pspec((1, 1024)),
                  pl.BlockSpec((SSD_CHUNK, 1024), rev)],
        out_specs=[pl.BlockSpec((SSD_CHUNK, CONV_DIM), rev), pl.BlockSpec((SSD_CHUNK, 1024), rev),
                   _pspec((1, 1024)), _pspec((1, 1024)), _pspec((1, 1024))],
        out_shape=[jax.ShapeDtypeStruct((s, CONV_DIM), F32), jax.ShapeDtypeStruct((s, 1024), BF16)]
        + [jax.ShapeDtypeStruct((1, 1024), F32)] * 3,
        scratch_shapes=[pltpu.VMEM((STATE_ROWS, LANES), F32)],
        compiler_params=_cp("arbitrary"), name="ssd_bwd")(xbc, proj, hist, dtb, alog, dsk, dy)


def _attn_head(q, kp, kc, vp, vc, has_prev, slope):
    scale = ATTN_HEAD_DIM ** -0.5
    n = ATTN_STEPS
    qi = lax.broadcasted_iota(jnp.int32, (n, n), 0)
    kj = lax.broadcasted_iota(jnp.int32, (n, n), 1)
    sp = jnp.where((kj >= qi) & has_prev, _bdot(q, kp, "nt") * scale - slope * (qi + n - kj).astype(F32), NEG)
    sc = jnp.where(kj <= qi, _bdot(q, kc, "nt") * scale - slope * (qi - kj).astype(F32), NEG)
    m = lax.stop_gradient(jnp.maximum(jnp.max(sp, axis=1, keepdims=True), jnp.max(sc, axis=1, keepdims=True)))
    pp, pc = jnp.exp(sp - m), jnp.exp(sc - m)
    den = jnp.sum(pp, axis=1, keepdims=True) + jnp.sum(pc, axis=1, keepdims=True)
    o = (_bdot(pp, vp, "nn") + _bdot(pc, vc, "nn")) / den
    return o, jnp.broadcast_to(m + jnp.log(den), (n, ATTN_HEAD_DIM))


def _head_slope(gi, h):
    s = [float(v) * DILATIONS[gi] for v in SLOPES[gi]]
    return jnp.where(h == 0, s[0], jnp.where(h == 1, s[1], jnp.where(h == 2, s[2], s[3])))


ATTN_UNROLL = 4


def _attn_heads_per_block(d):
    return 4 if d == 1 else 1


def _units(ref, d, hb):
    if d == 1:
        return [ref[:, h * ATTN_HEAD_DIM:(h + 1) * ATTN_HEAD_DIM] for h in range(hb)]
    return [ref[pl.ds(r, ATTN_STEPS, stride=d), :] for r in range(d)]


def _store_units(ref, src, d, hb):
    if d == 1:
        for h in range(hb):
            ref[:, h * ATTN_HEAD_DIM:(h + 1) * ATTN_HEAD_DIM] = src[h]
    else:
        for r in range(d):
            ref[pl.ds(r, ATTN_STEPS, stride=d), :] = src[r]


def _attn_fwd(qn, kn, vv, gi):
    d = DILATIONS[gi]
    s = qn.shape[0]
    span = ATTN_STEPS * d
    nb = s // span

    hb, units = _attn_heads_per_block(d), _attn_heads_per_block(d) * d

    def body(q_ref, k_ref, v_ref, o_ref, l_ref, sq, sk, sv, so, sl):
        h0, b = pl.program_id(0) * hb, pl.program_id(1)
        cur, prev = b % 2, (b + 1) % 2

        @pl.when(b == 0)
        def _():
            sk[prev] = jnp.zeros(sk.shape[1:], F32)
            sv[prev] = jnp.zeros(sv.shape[1:], F32)

        for u, (qr, kr, vr) in enumerate(zip(_units(q_ref, d, hb), _units(k_ref, d, hb), _units(v_ref, d, hb))):
            sq[u] = qr
            sk[cur, u] = kr
            sv[cur, u] = vr

        def step(i, carry):
            for j in range(ATTN_UNROLL):
                u = i * ATTN_UNROLL + j
                so[u], sl[u] = _attn_head(sq[u], sk[prev, u], sk[cur, u], sv[prev, u], sv[cur, u], b > 0,
                                          _head_slope(gi, h0 + u // d))
            return carry

        lax.fori_loop(0, units // ATTN_UNROLL, step, 0)
        _store_units(o_ref, so, d, hb)
        _store_units(l_ref, sl, d, hb)

    blk = pl.BlockSpec((span, hb * ATTN_HEAD_DIM), lambda h, b: (b, (gi * 4) // hb + h))
    out = pl.BlockSpec((span, hb * ATTN_HEAD_DIM), lambda h, b: (b, h))
    res = (units, ATTN_STEPS, ATTN_HEAD_DIM)
    return pl.pallas_call(
        body, grid=(4 // hb, nb), in_specs=[blk, blk, blk], out_specs=[out, out],
        out_shape=[jax.ShapeDtypeStruct((s, ATTN_GROUP_W), F32)] * 2,
        scratch_shapes=[pltpu.VMEM(res, F32), pltpu.VMEM((2,) + res, F32), pltpu.VMEM((2,) + res, F32),
                        pltpu.VMEM(res, F32), pltpu.VMEM(res, F32)],
        compiler_params=_cp("parallel", "arbitrary"), name=f"attn_fwd_g{gi}")(qn, kn, vv)


def _attn_bwd(qn, kn, vv, do, dl, gi):
    d = DILATIONS[gi]
    s = qn.shape[0]
    span = ATTN_STEPS * d
    nb = s // span

    hb, units = _attn_heads_per_block(d), _attn_heads_per_block(d) * d

    def body(q_ref, kp_ref, kc_ref, vp_ref, vc_ref, do_ref, dl_ref, dq_ref, dk_ref, dv_ref, sin, sout, ck, cv):
        h0, bi = pl.program_id(0) * hb, pl.program_id(1)

        @pl.when(bi == 0)
        def _():
            ck[...] = jnp.zeros_like(ck)
            cv[...] = jnp.zeros_like(cv)

        for i, ref in enumerate((q_ref, kp_ref, kc_ref, vp_ref, vc_ref, do_ref, dl_ref)):
            for u, val in enumerate(_units(ref, d, hb)):
                sin[i, u] = val
        has_prev = bi < nb - 1

        def step(i, carry):
            for j in range(ATTN_UNROLL):
                u = i * ATTN_UNROLL + j
                f = functools.partial(_attn_head, has_prev=has_prev, slope=_head_slope(gi, h0 + u // d))
                _, vjp = jax.vjp(f, sin[0, u], sin[1, u], sin[2, u], sin[3, u], sin[4, u])
                dq, dkp, dkc, dvp, dvc = vjp((sin[5, u], sin[6, u]))
                sout[0, u] = dq
                sout[1, u] = dkc + ck[u]
                sout[2, u] = dvc + cv[u]
                ck[u] = dkp
                cv[u] = dvp
            return carry

        lax.fori_loop(0, units // ATTN_UNROLL, step, 0)
        for i, ref in enumerate((dq_ref, dk_ref, dv_ref)):
            _store_units(ref, sout.at[i], d, hb)

    w = hb * ATTN_HEAD_DIM
    cur = pl.BlockSpec((span, w), lambda h, b: (nb - 1 - b, (gi * 4) // hb + h))
    prev = pl.BlockSpec((span, w), lambda h, b: (jnp.maximum(nb - 2 - b, 0), (gi * 4) // hb + h))
    out = pl.BlockSpec((span, w), lambda h, b: (nb - 1 - b, h))
    res = (units, ATTN_STEPS, ATTN_HEAD_DIM)
    return pl.pallas_call(
        body, grid=(4 // hb, nb), in_specs=[cur, prev, cur, prev, cur, out, out], out_specs=[out, out, out],
        out_shape=[jax.ShapeDtypeStruct((s, ATTN_GROUP_W), F32)] * 3,
        scratch_shapes=[pltpu.VMEM((7,) + res, F32), pltpu.VMEM((3,) + res, F32), pltpu.VMEM(res, F32), pltpu.VMEM(res, F32)],
        compiler_params=_cp("parallel", "arbitrary"), name=f"attn_bwd_g{gi}")(qn, kn, kn, vv, vv, do, dl)


def _layer_fwd(x, mod, W, l, late=None):
    sh1, sc1, g1, sh2, sc2, g2 = (mod[i:i + 1] for i in range(6))
    (h,) = _rowwise_fwd("norm1", _f_norm, [(x, 1024, 0)], [W["norm1_w"], sc1, sh1], [(1024, BF16)])
    proj = _matmul("in_proj", h, W["w_in"], "nt")
    xbc = _conv_fwd(proj, W["conv_w"], W["conv_b"])
    y, hist = _ssd_fwd(xbc, proj, W["dt_bias"], W["a_log"], W["d_skip"])
    if late is not None:
        W.update(late(y))
    (yn,) = _rowwise_fwd("ssd_gate", _f_ssdgate, [(y, 1024, 0), (proj, 1024, P_Z // 1024)], [W["ssd_norm_w"]], [(1024, BF16)])
    y_ssd = _matmul("ssd_out", yn, W["w_ssd_out"], "nn", layer=l)
    qn, kn, vv = _rowwise_fwd("qk_norm", _f_qknorm, [(proj, 1536, P_Q // 1536), (proj, 1536, P_K // 1536), (proj, 1536, P_V // 1536)],
                              [W["q_norm_w"], W["k_norm_w"]], [(1536, F32)] * 3)
    ol = [_attn_fwd(qn, kn, vv, gi) for gi in range(3)]
    (o,) = _rowwise_fwd("attn_combine", _f_combine, [(t[0], 512, 0) for t in ol] + [(t[1], 512, 0) for t in ol], [], [(512, BF16)])
    y_attn = _matmul("attn_out", o, W["w_attn_out"], "nn", layer=l, chips=256)
    pooled = _pool_fwd(proj)
    pm = _group_matmul("pool_mix", pooled, W["w_pool_mix"], "nn", layer=l)
    (ps,) = _rowwise_fwd("pool_scale", _f_poolscale, [(pm, 1024, 0)], [W["pool_scale"]], [(1024, BF16)])
    y_pool = _matmul("pool_out", ps, W["w_pool_out"], "nn", layer=l)
    (merged,) = _rowwise_fwd("merge", _f_merge, [(proj, 3072, P_GATES // 3072), (y_ssd, 1024, 0), (y_attn, 1024, 0), (y_pool, 1024, 0)],
                             [], [(1024, BF16)])
    mo = _matmul("mix_out", merged, W["w_out"], "nn", layer=l)
    (x1,) = _rowwise_fwd("resid1", _f_resid, [(x, 1024, 0), (mo, 1024, 0)], [g1], [(1024, F32)])
    (h2,) = _rowwise_fwd("norm2", _f_norm, [(x1, 1024, 0)], [W["norm2_w"], sc2, sh2], [(1024, BF16)])
    a = _matmul("ff1", h2, W["w_ff1"], "nn", layer=l, chips=1024)
    (r,) = _rowwise_fwd("relu2", _f_relu2, [(a, D_FF, 0)], [], [(D_FF, BF16)], tb=128)
    ff = _matmul("ff2", r, W["w_ff2"], "nn", layer=l)
    (x2,) = _rowwise_fwd("resid2", _f_resid, [(x1, 1024, 0), (ff, 1024, 0)], [g2], [(1024, F32)])
    saved = dict(x=x, h=h, proj=proj, xbc=xbc, y=y, hist=hist, yn=yn, y_ssd=y_ssd, qn=qn, kn=kn, vv=vv, ol=ol, o=o,
                 y_attn=y_attn, pooled=pooled, pm=pm, ps=ps, y_pool=y_pool, merged=merged, mo=mo, x1=x1, h2=h2, a=a, r=r, ff=ff)
    return x2, saved


def _layer_bwd(dx2, mod, W, sv, l, bufs):
    sh1, sc1, g1, sh2, sc2, g2 = (mod[i:i + 1] for i in range(6))
    g = {}
    dx1a, dff, dg2 = _rowwise_bwd("resid2_bwd", _f_resid, [(sv["x1"], 1024, 0), (sv["ff"], 1024, 0)], [g2], [(dx2, 1024, 0)],
                                  [True, True], gdt=[F32, BF16])
    g["w_ff2"] = _matmul("ff2_dw", sv["r"], dff, "tn", BF16, into=(bufs["w_ff2"], l))
    dr = _matmul("ff2_dx", dff, W["w_ff2"], "nt", layer=l)
    (da,) = _rowwise_bwd("relu2_bwd", _f_relu2, [(sv["a"], D_FF, 0)], [], [(dr, D_FF, 0)], [True], tb=128, gdt=[BF16])
    g["w_ff1"] = _matmul("ff1_dw", sv["h2"], da, "tn", BF16, out_chips=1024, into=(bufs["w_ff1"], l))
    dh2 = _matmul("ff1_dx", da, W["w_ff1"], "nt", layer=l, chips=1024)
    dx1, g["norm2_w"], dsc2, dsh2 = _rowwise_bwd("norm2_bwd", _f_norm, [(sv["x1"], 1024, 0)], [W["norm2_w"], sc2, sh2],
                                                 [(dh2, 1024, 0)], [True], add=(0, dx1a))
    dxa, dmo, dg1 = _rowwise_bwd("resid1_bwd", _f_resid, [(sv["x"], 1024, 0), (sv["mo"], 1024, 0)], [g1], [(dx1, 1024, 0)],
                                 [True, True], gdt=[F32, BF16])
    g["w_out"] = _matmul("mix_out_dw", sv["merged"], dmo, "tn", BF16, into=(bufs["w_out"], l))
    dmerged = _matmul("mix_out_dx", dmo, W["w_out"], "nt", layer=l)
    proj = sv["proj"]
    dgates, dy_ssd, dy_attn, dy_pool = _rowwise_bwd(
        "merge_bwd", _f_merge, [(proj, 3072, P_GATES // 3072), (sv["y_ssd"], 1024, 0), (sv["y_attn"], 1024, 0), (sv["y_pool"], 1024, 0)],
        [], [(dmerged, 1024, 0)], [True] * 4, gdt=[BF16] * 4)
    g["w_pool_out"] = _matmul("pool_out_dw", sv["ps"], dy_pool, "tn", BF16, into=(bufs["w_pool_out"], l))
    dps = _matmul("pool_out_dx", dy_pool, W["w_pool_out"], "nt", layer=l)
    dpm, g["pool_scale"] = _rowwise_bwd("pool_scale_bwd", _f_poolscale, [(sv["pm"], 1024, 0)], [W["pool_scale"]], [(dps, 1024, 0)],
                                        [True], gdt=[BF16])
    dmix = _group_matmul("pool_mix_dw", sv["pooled"], dpm, "tn")
    g["w_pool_mix"] = bufs["w_pool_mix"].at[:, l].set(
        jnp.moveaxis(dmix.reshape(4, N_CHIPS, POOL_GW // N_CHIPS, POOL_GW), 1, 0).astype(BF16))
    dpooled = _group_matmul("pool_mix_dx", dpm, W["w_pool_mix"], "nt", layer=l)
    du = _pool_bwd(dpooled)
    g["w_attn_out"] = _matmul("attn_out_dw", sv["o"], dy_attn, "tn", BF16, out_chips=256, into=(bufs["w_attn_out"], l))
    do = _matmul("attn_out_dx", dy_attn, W["w_attn_out"], "nt", layer=l, chips=256)
    ol = sv["ol"]
    dol = _rowwise_bwd("attn_combine_bwd", _f_combine, [(t[0], 512, 0) for t in ol] + [(t[1], 512, 0) for t in ol], [],
                       [(do, 512, 0)], [True] * 6)
    dqs, dks, dvs = zip(*[_attn_bwd(sv["qn"], sv["kn"], sv["vv"], dol[gi], dol[3 + gi], gi) for gi in range(3)])
    dqn, dkn, dvv = (jnp.concatenate(t, axis=1) for t in (dqs, dks, dvs))
    dq, dk, dv, g["q_norm_w"], g["k_norm_w"] = _rowwise_bwd(
        "qk_norm_bwd", _f_qknorm, [(proj, 1536, P_Q // 1536), (proj, 1536, P_K // 1536), (proj, 1536, P_V // 1536)],
        [W["q_norm_w"], W["k_norm_w"]], [(dqn, 1536, 0), (dkn, 1536, 0), (dvv, 1536, 0)], [True] * 3, gdt=[BF16] * 3)
    g["w_ssd_out"] = _matmul("ssd_out_dw", sv["yn"], dy_ssd, "tn", BF16, into=(bufs["w_ssd_out"], l))
    dyn = _matmul("ssd_out_dx", dy_ssd, W["w_ssd_out"], "nt", layer=l)
    dy, dz, g["ssd_norm_w"] = _rowwise_bwd("ssd_gate_bwd", _f_ssdgate, [(sv["y"], 1024, 0), (proj, 1024, P_Z // 1024)], [W["ssd_norm_w"]],
                                           [(dyn, 1024, 0)], [True, True], gdt=[F32, BF16])
    dxbc, ddt, g["dt_bias"], g["a_log"], g["d_skip"] = _ssd_bwd(sv["xbc"], proj, sv["hist"], W["dt_bias"], W["a_log"], W["d_skip"], dy)
    dxbc_raw, g["conv_w"], g["conv_b"] = _conv_bwd(proj, W["conv_w"], W["conv_b"], dxbc)
    dproj = jnp.concatenate([dxbc_raw, dq, dk, dv, dgates, dz, du, ddt], axis=1)
    g["w_in"] = _matmul("in_proj_dw", dproj, sv["h"], "tn", BF16)
    dh = _matmul("in_proj_dx", dproj, W["w_in"], "nn")
    dx, g["norm1_w"], dsc1, dsh1 = _rowwise_bwd("norm1_bwd", _f_norm, [(sv["x"], 1024, 0)], [W["norm1_w"], sc1, sh1],
                                                [(dh, 1024, 0)], [True], add=(0, dxa))
    dmod = jnp.concatenate([dsh1, dsc1, dg1, dsh2, dsc2, dg2], axis=0)
    return dx, dmod, g


def _expand_heads(t):
    return jnp.repeat(t, SSD_HEAD_DIM, axis=-1)


def _reduce_heads(t):
    return t.reshape(t.shape[:-1] + (SSD_HEADS, SSD_HEAD_DIM)).sum(-1)


_IN_SPLITS = np.cumsum((0,) + IN_SIZES)


def _regroup(pieces, sizes):
    out, cur, need = [], [], list(sizes)
    for p in pieces:
        off = 0
        while off < p.shape[0]:
            take = min(p.shape[0] - off, need[0] - sum(t.shape[0] for t in cur))
            cur.append(p[off:off + take])
            off += take
            if sum(t.shape[0] for t in cur) == need[0]:
                out.append(cur[0] if len(cur) == 1 else jnp.concatenate(cur, axis=0))
                cur, need = [], need[1:]
    return out


def _w_in_to_layout(chips):
    z, xbc, dt, q, k, v, u, gates = _regroup([chips[j] for j in range(N_CHIPS)], IN_SIZES)
    return jnp.concatenate([xbc, q, k, v, gates, z, u, jnp.repeat(dt, SSD_HEAD_DIM, axis=0)], axis=0)


def _w_in_from_layout(g):
    xbc, q, k, v = (g[o:o + 1536] for o in (P_XBC, P_Q, P_K, P_V))
    gates, z, u, dt = g[P_GATES:P_GATES + 3072], g[P_Z:P_Z + 1024], g[P_U:P_U + 1024], g[P_DT:P_DT + 1024]
    dt = dt.astype(F32).reshape(SSD_HEADS, SSD_HEAD_DIM, D_MODEL).sum(1).astype(g.dtype)
    return jnp.stack(_regroup([z, xbc, dt, q, k, v, u, gates], [IN_WIDTH // N_CHIPS] * N_CHIPS))


_STACKED = ("w_ssd_out", "w_attn_out", "w_pool_mix", "w_pool_out", "w_out", "w_ff1", "w_ff2")
_ROWS = ("norm1_w", "norm2_w", "conv_b", "ssd_norm_w", "q_norm_w", "k_norm_w", "pool_scale")
_HEAD_ROWS = ("dt_bias", "a_log", "d_skip")


def _layer_weights(wg, lg, small, l):
    W = {k: wg[k] for k in _STACKED if k in wg}
    W["w_in"] = _w_in_to_layout(wg["w_in"][lg])
    W["conv_w"] = small["conv_w"][l]
    for k in _ROWS:
        W[k] = small[k][l][None, :]
    for k in _HEAD_ROWS:
        W[k] = _expand_heads(small[k][l])[None, :]
    return W


def _layer_grads_by_chip(g, w_in_buf, l):
    out = dict(g)
    out["w_in"] = w_in_buf.at[:, l].set(_w_in_from_layout(g["w_in"]))
    for k in _ROWS:
        out[k] = g[k][0]
    for k in _HEAD_ROWS:
        out[k] = _reduce_heads(g[k][0])
    return out


ANY = pl.BlockSpec(memory_space=pl.ANY)


def _place():
    x, y, c = lax.axis_index("x"), lax.axis_index("y"), lax.axis_index("c")
    return x, y, c, (x, y, 1 - c), [(1 - x, y), (x, 1 - y), (1 - x, 1 - y)]


def _allgather8(name, blk):
    m_per, n = blk.shape

    def body(x_ref, out_ref, send_sems, recv_sems, local_sem):
        x, y, c, sibling, chips = _place()
        me = (x, y, c)

        def rows(px, py, pc):
            return out_ref.at[pl.ds((4 * px + 2 * py + pc) * m_per, m_per), :]

        def copy(k, block, to, src=None):
            return pltpu.make_async_remote_copy(
                src_ref=rows(*block) if src is None else src, dst_ref=rows(*block),
                send_sem=send_sems.at[k], recv_sem=recv_sems.at[k], device_id=to, device_id_type=MESH)

        mine = pltpu.make_async_copy(x_ref, rows(*me), local_sem)
        mine.start()
        first = [copy(0, me, sibling, src=x_ref)]
        first += [copy(1 + j, me, (*chip, c), src=x_ref) for j, chip in enumerate(chips)]
        for cp in first:
            cp.start()
        passed = [copy(4 + j, (*chip, c), sibling) for j, chip in enumerate(chips)]
        for j, chip in enumerate(chips):
            copy(1 + j, (*chip, c), me).wait_recv()
            passed[j].start()
        copy(0, sibling, me).wait_recv()
        for j, chip in enumerate(chips):
            copy(4 + j, (*chip, 1 - c), me).wait_recv()
        for cp in first + passed:
            cp.wait_send()
        mine.wait()

    return pl.pallas_call(
        body, out_shape=jax.ShapeDtypeStruct((N_DEV * m_per, n), blk.dtype),
        in_specs=[pl.BlockSpec(memory_space=pltpu.VMEM)], out_specs=pl.BlockSpec(memory_space=pltpu.VMEM),
        scratch_shapes=[pltpu.SemaphoreType.DMA((7,)), pltpu.SemaphoreType.DMA((7,)), pltpu.SemaphoreType.DMA],
        name=name)(blk)


HBM_SPEC = pl.BlockSpec(memory_space=pltpu.HBM)
SEM_SPEC = pl.BlockSpec(memory_space=pltpu.SEMAPHORE)
SIDE_EFFECT = pltpu.SideEffectType.DATAFLOW_SIDE_EFFECTING


def _dma_sems(n):
    return [pltpu.SemaphoreType.DMA((n,)), pltpu.SemaphoreType.DMA((n,))]


def _half_axis(shape):
    return 1 if (len(shape) > 3 or (shape[1] // 2) % 16 == 0) else len(shape) - 1


def _halves(ref, axis, c):
    r2 = ref.shape[axis] // 2
    lead = (slice(None),) * axis
    return ref.at[lead + (pl.ds(r2 * c, r2),)], ref.at[lead + (pl.ds(r2 * (1 - c), r2),)]


def _gather_copies(srcs, lands, send_sems, recv_sems):
    x, y, c, _, chips = _place()
    sends, lands_here = [], []
    for j, (cx, cy) in enumerate(chips):
        for i, (s, t) in enumerate(zip(srcs, lands)):
            k = 3 * i + j
            ax = _half_axis(s.shape)
            mine = _halves(t.at[:, 2 * x + y], ax, c)[0]
            theirs = _halves(t.at[:, 2 * cx + cy], ax, c)[0]
            sends.append(pltpu.make_async_remote_copy(src_ref=_halves(s, ax, c)[0], dst_ref=mine, send_sem=send_sems.at[k],
                                                      recv_sem=recv_sems.at[k], device_id=(cx, cy, c), device_id_type=MESH))
            lands_here.append(pltpu.make_async_remote_copy(src_ref=theirs, dst_ref=theirs, send_sem=send_sems.at[k],
                                                           recv_sem=recv_sems.at[k], device_id=(cx, cy, c), device_id_type=MESH))
    return sends, lands_here


def _exchange_copies(srcs, lands, send_sems, recv_sems):
    x, y, c, _, chips = _place()
    sends = [pltpu.make_async_remote_copy(src_ref=s.at[2 * cx + cy], dst_ref=t.at[j], send_sem=send_sems.at[3 * i + j],
                                          recv_sem=recv_sems.at[3 * i + j], device_id=(cx, cy, c), device_id_type=MESH)
             for j, (cx, cy) in enumerate(chips) for i, (s, t) in enumerate(zip(srcs, lands))]
    return sends, sends


def _split_start(name, copies, srcs, lands, after):
    ns, nl = len(srcs), len(lands)
    n_copies = 3 * ns

    def body(*refs):
        send_sems, recv_sems = refs[ns + nl + 1], refs[ns + nl + 2]
        for cp in copies(refs[:ns], refs[ns:ns + nl], send_sems, recv_sems)[0]:
            cp.start()
        refs[-1][...] = jnp.zeros_like(refs[-1])

    arrs = list(srcs) + list(lands)
    res = pl.pallas_call(
        body, name=name,
        out_shape=(pltpu.SemaphoreType.DMA((n_copies,)), pltpu.SemaphoreType.DMA((n_copies,)))
        + tuple(pltpu.HBM(a.shape, a.dtype) for a in arrs) + (jax.ShapeDtypeStruct((8, LANES), F32),),
        in_specs=[HBM_SPEC] * (ns + nl) + [ANY],
        out_specs=(SEM_SPEC, SEM_SPEC) + (HBM_SPEC,) * (ns + nl) + (pl.BlockSpec(memory_space=pltpu.VMEM),),
        input_output_aliases={i: 2 + i for i in range(ns + nl)},
        compiler_params=pltpu.CompilerParams(has_side_effects=SIDE_EFFECT),
    )(*[pltpu.with_memory_space_constraint(a, pltpu.HBM) for a in arrs], after)
    return res[0], res[1], list(res[2:2 + ns]), list(res[2 + ns:2 + ns + nl]), res[-1]


def _split_wait(name, copies, send_sems, recv_sems, srcs, lands, after):
    ns, nl = len(srcs), len(lands)

    def body(*refs):
        sends, lands_here = copies(refs[:ns], refs[ns:ns + nl], refs[ns + nl], refs[ns + nl + 1])
        for cp in sends:
            cp.wait_send()
        for cp in lands_here:
            cp.wait_recv()

    arrs = list(srcs) + list(lands)
    res = pl.pallas_call(
        body, name=name, out_shape=tuple(pltpu.HBM(a.shape, a.dtype) for a in arrs),
        in_specs=[HBM_SPEC] * (ns + nl) + [SEM_SPEC, SEM_SPEC, ANY], out_specs=(HBM_SPEC,) * (ns + nl),
        input_output_aliases={i: i for i in range(ns + nl)},
        compiler_params=pltpu.CompilerParams(has_side_effects=SIDE_EFFECT),
    )(*arrs, send_sems, recv_sems, after)
    return list(res[:ns]), list(res[ns:])


def _gather_forward(name, lands):
    n = len(lands)

    def body(*refs):
        ins, outs = refs[:n], refs[n:2 * n]
        send_sems, recv_sems = refs[2 * n:]
        x, y, c, sibling, chips = _place()
        sends, arrivals = [], []
        for j, (cx, cy) in enumerate(chips):
            for i in range(n):
                k = 3 * i + j
                ax = _half_axis(ins[i].shape[:1] + ins[i].shape[2:])
                src = _halves(ins[i].at[:, 2 * cx + cy], ax, c)[0]
                dst, theirs = _halves(outs[i].at[:, 2 * cx + cy], ax, c)
                sends.append(pltpu.make_async_remote_copy(src_ref=src, dst_ref=dst, send_sem=send_sems.at[k], recv_sem=recv_sems.at[k],
                                                          device_id=sibling, device_id_type=MESH))
                arrivals.append(pltpu.make_async_remote_copy(src_ref=theirs, dst_ref=theirs, send_sem=send_sems.at[k],
                                                             recv_sem=recv_sems.at[k], device_id=sibling, device_id_type=MESH))
        for cp in sends:
            cp.start()
        for cp in arrivals:
            cp.wait_recv()
        for cp in sends:
            cp.wait_send()

    return pl.pallas_call(
        body, out_shape=[jax.ShapeDtypeStruct(t.shape, t.dtype) for t in lands], in_specs=[ANY] * n, out_specs=[ANY] * n,
        input_output_aliases={i: i for i in range(n)}, scratch_shapes=_dma_sems(3 * n), name=name)(*lands)


def _swap_halves(name, gs):
    n = len(gs)

    def body(*refs):
        ins, got = refs[:n], refs[n:2 * n]
        send_sems, recv_sems = refs[2 * n:]
        x, y, c, sibling, _ = _place()
        sends = [pltpu.make_async_remote_copy(src_ref=_halves(ins[i], 1 + _half_axis(ins[i].shape[1:]), c)[1], dst_ref=got[i],
                                              send_sem=send_sems.at[i], recv_sem=recv_sems.at[i], device_id=sibling, device_id_type=MESH)
                 for i in range(n)]
        for cp in sends:
            cp.start()
        for cp in sends:
            cp.wait_recv()
        for cp in sends:
            cp.wait_send()

    def half_shape(t):
        ax = 1 + _half_axis(t.shape[1:])
        return t.shape[:ax] + (t.shape[ax] // 2,) + t.shape[ax + 1:]

    return pl.pallas_call(
        body, out_shape=[jax.ShapeDtypeStruct(half_shape(t), t.dtype) for t in gs],
        in_specs=[ANY] * n, out_specs=[ANY] * n, scratch_shapes=_dma_sems(n), name=name)(*gs)


def _share_halves(name, ts, axes):
    n = len(ts)

    def body(*refs):
        ins, outs = refs[:n], refs[n:2 * n]
        send_sems, recv_sems = refs[2 * n:]
        x, y, c, sibling, _ = _place()
        sends, arrivals = [], []
        for i in range(n):
            mine, theirs = _halves(outs[i], axes[i], c)
            sends.append(pltpu.make_async_remote_copy(src_ref=ins[i], dst_ref=mine, send_sem=send_sems.at[i], recv_sem=recv_sems.at[i],
                                                      device_id=sibling, device_id_type=MESH))
            arrivals.append(pltpu.make_async_remote_copy(src_ref=ins[i], dst_ref=theirs, send_sem=send_sems.at[i],
                                                         recv_sem=recv_sems.at[i], device_id=sibling, device_id_type=MESH))
        for cp in sends:
            cp.start()
        for cp in arrivals:
            cp.wait_recv()
        for cp in sends:
            cp.wait_send()

    return pl.pallas_call(
        body, out_shape=[jax.ShapeDtypeStruct(t.shape[:ax] + (2 * t.shape[ax],) + t.shape[ax + 1:], t.dtype) for t, ax in zip(ts, axes)],
        in_specs=[ANY] * n, out_specs=[ANY] * n, scratch_shapes=_dma_sems(n), name=name)(*ts)


PACK_W = 1024
PACK_TB = 512


def _sum_rows(name, parts, out_dtype):
    def f(*vals):
        acc = vals[0]
        for v in vals[1:]:
            acc = acc + v
        return (acc,)

    return _rowwise_fwd(name, f, [(p, PACK_W, 0) for p in parts], [], [(PACK_W, out_dtype)], tb=_tile(parts[0].shape[0], PACK_TB))[0]


def _sum_slots(name, ops, count, out_dtype):
    mat = ops[0][0].shape[1:]

    def body(*refs):
        acc = refs[0][...].astype(F32)
        for r in refs[1:-1]:
            acc = acc + r[...].astype(F32)
        refs[-1][...] = acc.astype(refs[-1].dtype)

    return pl.pallas_call(
        body, grid=(count,), in_specs=[pl.BlockSpec((None,) + mat, lambda i, s=s: (s * count + i, 0, 0)) for _, s in ops],
        out_specs=pl.BlockSpec((None,) + mat, lambda i: (i, 0, 0)), out_shape=jax.ShapeDtypeStruct((count,) + mat, out_dtype),
        compiler_params=_cp("parallel"), name=name)(*[a for a, _ in ops])


def _adamw_update(w_ref, g_ref, m_ref, v_ref, go_ref, d_ref, mo_ref, vo_ref):
    c1 = 1.0 / (1.0 - ADAM_B1 ** ADAM_STEP)
    c2 = 1.0 / (1.0 - ADAM_B2 ** ADAM_STEP)
    gg = g_ref[...]
    mn = ADAM_B1 * m_ref[...] + (1.0 - ADAM_B1) * gg
    vn = ADAM_B2 * v_ref[...] + (1.0 - ADAM_B2) * jnp.square(gg)
    go_ref[...] = gg
    d_ref[...] = -ADAM_LR * ((mn * c1) / (jnp.sqrt(vn * c2) + ADAM_EPS) + ADAM_WD * w_ref[...])
    mo_ref[...] = mn
    vo_ref[...] = vn


def _adamw_layers(name, w, g, m, v, lo, prev=None, dep=None):
    _, r, cw = w.shape

    def body(w_ref, g_ref, m_ref, v_ref, *rest):
        _adamw_update(w_ref, g_ref, m_ref, v_ref, *rest[-4:])

    full = pl.BlockSpec((None, r, LANES), lambda l, i: (lo + l, 0, i))
    extra = ([] if prev is None else list(prev)) + ([] if dep is None else [dep])
    return pl.pallas_call(
        body, grid=(g.shape[0], cw // LANES),
        in_specs=[full, pl.BlockSpec((None, r, LANES), lambda l, i: (l, 0, i)), full, full] + [pl.BlockSpec(memory_space=pl.ANY)] * len(extra),
        out_specs=[full] * 4, out_shape=[jax.ShapeDtypeStruct(w.shape, F32)] * 4,
        input_output_aliases={4 + i: i for i in range(0 if prev is None else 4)},
        compiler_params=_cp("parallel", "parallel"), name=name)(w, g, m, v, *extra)


def _adamw(name, w, g, m, v, row0=0, unit=None, prev=None, dep=None):
    r, cw = w.shape
    tb = unit or r
    while tb * cw > 300_000 and tb % 16 == 0:
        tb //= 2
    off = row0 // tb

    def body(w_ref, g_ref, m_ref, v_ref, *rest):
        _adamw_update(w_ref, g_ref, m_ref, v_ref, *rest[-4:])

    full = pl.BlockSpec((tb, cw), lambda i: (off + i, 0))
    extra = ([] if prev is None else list(prev)) + ([] if dep is None else [dep])
    n_prev = 0 if prev is None else 4
    return pl.pallas_call(
        body, grid=(g.shape[0] // tb,),
        in_specs=[full, pl.BlockSpec((tb, cw), lambda i: (i, 0)), full, full] + [pl.BlockSpec(memory_space=pl.ANY)] * len(extra),
        out_specs=[full] * 4, out_shape=[jax.ShapeDtypeStruct((r, cw), F32)] * 4,
        input_output_aliases={4 + i: i for i in range(n_prev)},
        compiler_params=_cp("parallel"), name=name)(w, g, m, v, *extra)


def _silu_rows(c):
    def body(c_ref, o_ref):
        rows = lax.broadcasted_iota(jnp.int32, o_ref.shape, 0)
        o_ref[...] = jnp.where(rows == 0, jnp.broadcast_to(_silu(c_ref[...]), o_ref.shape), 0.0)

    return pl.pallas_call(body, out_shape=jax.ShapeDtypeStruct((8, c.shape[1]), F32), name="cond_silu")(c)


_KINDS = ("w_in", "w_ssd_out", "w_attn_out", "w_pool_mix", "w_pool_out", "w_out", "w_ff1", "w_ff2")
_SMALL = ("b_ada", "norm1_w", "norm2_w", "conv_b", "dt_bias", "a_log", "d_skip", "ssd_norm_w", "q_norm_w", "k_norm_w",
          "pool_scale")
_ORDER = ("w_ada", "b_ada", "norm1_w", "norm2_w", "w_in", "conv_w", "conv_b", "dt_bias", "a_log", "d_skip", "ssd_norm_w",
          "w_ssd_out", "q_norm_w", "k_norm_w", "w_attn_out", "w_pool_mix", "pool_scale", "w_pool_out", "w_out", "w_ff1", "w_ff2")


def _pack_flat(arrs, rows, dtype):
    flat = jnp.concatenate([a.reshape(-1).astype(dtype) for a in arrs])
    return jnp.pad(flat, (0, rows * PACK_W - flat.shape[0])).reshape(rows, PACK_W)


def _unpack_flat(buf, shapes):
    flat = buf.reshape(-1)
    out, off = [], 0
    for shp in shapes:
        n = int(np.prod(shp))
        out.append(flat[off:off + n].reshape(shp))
        off += n
    return out


def _small_rows(n_elems):
    return -(-n_elems // (8 * PACK_W)) * 8


def kernel(x, c, w_ada, b_ada, norm1_w, norm2_w, w_in, conv_w, conv_b, dt_bias, a_log, d_skip, ssd_norm_w, w_ssd_out, q_norm_w, k_norm_w, w_attn_out, w_pool_mix, pool_scale, w_pool_out, w_out, w_ff1, w_ff2, loss_target, m_w_ada, m_b_ada, m_norm1_w, m_norm2_w, m_w_in, m_conv_w, m_conv_b, m_dt_bias, m_a_log, m_d_skip, m_ssd_norm_w, m_w_ssd_out, m_q_norm_w, m_k_norm_w, m_w_attn_out, m_w_pool_mix, m_pool_scale, m_w_pool_out, m_w_out, m_w_ff1, m_w_ff2, v_w_ada, v_b_ada, v_norm1_w, v_norm2_w, v_w_in, v_conv_w, v_conv_b, v_dt_bias, v_a_log, v_d_skip, v_ssd_norm_w, v_w_ssd_out, v_q_norm_w, v_k_norm_w, v_w_attn_out, v_w_pool_mix, v_pool_scale, v_w_pool_out, v_w_out, v_w_ff1, v_w_ff2):
    w = dict(w_ada=w_ada, b_ada=b_ada, norm1_w=norm1_w, norm2_w=norm2_w, w_in=w_in, conv_w=conv_w, conv_b=conv_b, dt_bias=dt_bias, a_log=a_log, d_skip=d_skip, ssd_norm_w=ssd_norm_w, w_ssd_out=w_ssd_out, q_norm_w=q_norm_w, k_norm_w=k_norm_w, w_attn_out=w_attn_out, w_pool_mix=w_pool_mix, pool_scale=pool_scale, w_pool_out=w_pool_out, w_out=w_out, w_ff1=w_ff1, w_ff2=w_ff2)
    m = dict(w_ada=m_w_ada, b_ada=m_b_ada, norm1_w=m_norm1_w, norm2_w=m_norm2_w, w_in=m_w_in, conv_w=m_conv_w, conv_b=m_conv_b, dt_bias=m_dt_bias, a_log=m_a_log, d_skip=m_d_skip, ssd_norm_w=m_ssd_norm_w, w_ssd_out=m_w_ssd_out, q_norm_w=m_q_norm_w, k_norm_w=m_k_norm_w, w_attn_out=m_w_attn_out, w_pool_mix=m_w_pool_mix, pool_scale=m_pool_scale, w_pool_out=m_w_pool_out, w_out=m_w_out, w_ff1=m_w_ff1, w_ff2=m_w_ff2)
    v = dict(w_ada=v_w_ada, b_ada=v_b_ada, norm1_w=v_norm1_w, norm2_w=v_norm2_w, w_in=v_w_in, conv_w=v_conv_w, conv_b=v_conv_b, dt_bias=v_dt_bias, a_log=v_a_log, d_skip=v_d_skip, ssd_norm_w=v_ssd_norm_w, w_ssd_out=v_w_ssd_out, q_norm_w=v_q_norm_w, k_norm_w=v_k_norm_w, w_attn_out=v_w_attn_out, w_pool_mix=v_w_pool_mix, pool_scale=v_pool_scale, w_pool_out=v_w_pool_out, w_out=v_w_out, w_ff1=v_w_ff1, w_ff2=v_w_ff2)
    chip = 2 * lax.axis_index("x") + lax.axis_index("y")
    dev = 2 * chip + lax.axis_index("c")
    ada_cols = w_ada.shape[2]

    wk = dict({k: w[k] for k in _KINDS}, w_in=jnp.transpose(w_in, (0, 2, 1)))
    mk, vk = {"w_in": jnp.transpose(m_w_in, (0, 2, 1))}, {"w_in": jnp.transpose(v_w_in, (0, 2, 1))}
    rest = _KINDS[1:]

    def start_gather(tag, lo, n, kinds, after):
        shards = [wk[k][lo:lo + n].astype(BF16) for k in kinds]
        lands = [lax.empty((n, N_CHIPS) + s.shape[1:], BF16) for s in shards]
        return _split_start("gather_start_" + tag, _gather_copies, shards, lands, after)

    gather_a1 = start_gather("a1", 0, 1, _KINDS[:1], c)
    c = c + gather_a1[4][0, 0]

    n_conv = conv_w.size // PACK_W
    rows1 = _small_rows((1 + n_conv) * PACK_W)
    blk = jnp.concatenate([_silu_rows(c)[:1], conv_w.reshape(n_conv, PACK_W), jnp.zeros((rows1 - 1 - n_conv, PACK_W), F32)])
    first = _allgather8("gather_cond", blk).reshape(N_DEV, rows1, PACK_W)
    cond_all = first[:, 0]
    conv_all = first[0::2, 1:1 + n_conv].reshape((N_CHIPS,) + conv_w.shape)
    conv_full = jnp.moveaxis(conv_all, 0, 2).reshape(DEPTH, SSD_CONV, CONV_DIM)
    b_cols = lax.dynamic_slice_in_dim(b_ada, chip * ada_cols, ada_cols, axis=1)
    mod_cols = jnp.stack([_matmul("ada_fwd", cond_all, w_ada[l], "nn", precise=True) + b_cols[l][None, :] for l in range(DEPTH)])
    mod_all = _allgather8("gather_mod", mod_cols.reshape(-1, PACK_W)).reshape(N_DEV, DEPTH, N_DEV, ada_cols)
    mine = lax.dynamic_index_in_dim(mod_all[0::2], dev, axis=2, keepdims=False)
    mods = jnp.moveaxis(mine, 0, 1).reshape(DEPTH, 6, D_MODEL)

    core = lax.axis_index("c")
    small_w = dict({k: w[k] for k in _SMALL[1:]}, conv_w=conv_full)

    def finish_gather(tag, handle, kinds, after):
        shards, lands = _split_wait("gather_wait_" + tag, _gather_copies, handle[0], handle[1], handle[2], handle[3], after)
        lands = _gather_forward("gather_forward_" + tag, lands)
        wg = {k: lax.dynamic_update_slice_in_dim(t, s[:, None], chip, axis=1) for k, t, s in zip(kinds, lands, shards)}
        n = shards[0].shape[0]
        for k in ("w_ssd_out", "w_pool_out", "w_out", "w_ff2"):
            if k in wg:
                wg[k] = wg[k].reshape(n, -1, D_MODEL)
        if "w_pool_mix" in wg:
            wg["w_pool_mix"] = jnp.moveaxis(wg["w_pool_mix"], 1, 2).reshape(n, 4, POOL_GW, POOL_GW)
        return wg

    wg_a1 = finish_gather("a1", gather_a1, _KINDS[:1], mods)
    gather_a2 = start_gather("a2", 0, 1, rest, wg_a1["w_in"])
    gather_b = start_gather("b", 1, DEPTH - 1, _KINDS, gather_a2[4])
    mods = mods + gather_b[4][0, 0]

    xc = x[0]
    Ws, saved = [None] * DEPTH, [None] * DEPTH
    Ws[0] = _layer_weights(wg_a1, 0, small_w, 0)
    xc, saved[0] = _layer_fwd(xc, mods[0], Ws[0], 0, late=lambda y: finish_gather("a2", gather_a2, rest, y))
    wg_b = finish_gather("b", gather_b, _KINDS, xc)
    for l in range(1, DEPTH):
        Ws[l] = _layer_weights(wg_b, l - 1, small_w, l)
        xc, saved[l] = _layer_fwd(xc, mods[l], Ws[l], l - 1)
    dx, loss = _loss_and_grad(xc, loss_target[0])
    dmods, grads = [None] * DEPTH, [None] * DEPTH

    def backward(l, lg, dx, mod, bufs):
        dx, dmods[l], g = _layer_bwd(dx, mod, Ws[l], saved[l], lg, bufs)
        grads[l] = _layer_grads_by_chip(g, bufs["w_in"], lg)
        return dx, {k: grads[l][k] for k in _KINDS}

    bufs_b = {k: lax.empty((N_CHIPS, DEPTH - 1) + wk[k].shape[1:], BF16) for k in _KINDS}
    for l in reversed(range(1, DEPTH)):
        dx, bufs_b = backward(l, l - 1, dx, mods[l], bufs_b)

    def flat(t):
        return t.reshape((-1,) + t.shape[-2:])

    def start_exchange(tag, bufs, after):
        gs = [bufs[k] for k in _KINDS]
        got = _swap_halves("grad_swap_" + tag, gs)
        axes = [1 + _half_axis(t.shape[1:]) for t in gs]
        own = [lax.dynamic_slice_in_dim(t, (t.shape[ax] // 2) * core, t.shape[ax] // 2, axis=ax) for t, ax in zip(gs, axes)]
        pairs = [_sum_slots(f"sum_pair_{tag}_{k}", [(flat(a), 0), (flat(b), 0)], flat(a).shape[0], BF16).reshape(a.shape)
                 for k, a, b in zip(_KINDS, own, got)]
        lands = [lax.empty((3,) + p.shape[1:], BF16) for p in pairs]
        return _split_start("exchange_start_" + tag, _exchange_copies, pairs, lands, after)

    def finish_exchange(tag, handle, after):
        pairs, partials = _split_wait("exchange_wait_" + tag, _exchange_copies, handle[0], handle[1], handle[2], handle[3], after)
        mine = [lax.dynamic_index_in_dim(p, chip, axis=0, keepdims=False) for p in pairs]
        totals = [_sum_slots(f"sum_chips_{tag}_{k}", [(flat(a), 0)] + [(flat(p), s) for s in range(3)], flat(a).shape[0], F32).reshape(a.shape)
                  for k, a, p in zip(_KINDS, mine, partials)]
        axes = [_half_axis((1,) + wk[k].shape[1:]) for k in _KINDS]
        return [lax.dynamic_update_slice_in_dim(t, mine_t, mine_t.shape[ax] * core, axis=ax)
                for t, mine_t, ax in zip(_share_halves("grad_share_" + tag, totals, axes), totals, axes)]

    def adamw_group(tag, lo, gs, prev, dep):
        out = {}
        out["w_in"] = _adamw_layers(f"adamw_{tag}_w_in", wk["w_in"], gs[0], mk["w_in"], vk["w_in"], lo,
                                    prev=None if prev is None else prev["w_in"], dep=dep)
        dep = out["w_in"][1]
        for k, gk in zip(rest, gs[1:]):
            shp = w[k].shape
            unit = int(np.prod(shp[1:-1]))
            two_d = lambda t: t.reshape(-1, shp[-1])
            out[k] = _adamw(f"adamw_{tag}_{k}", two_d(w[k]), two_d(gk), two_d(m[k]), two_d(v[k]), row0=lo * unit, unit=unit,
                            prev=None if prev is None else prev[k], dep=dep)
            dep = out[k][1]
        return out

    exchange_b = start_exchange("b", bufs_b, dx)
    bufs_a = {k: lax.empty((N_CHIPS, 1) + wk[k].shape[1:], BF16) for k in _KINDS}
    dx, bufs_a = backward(0, 0, dx, mods[0] + exchange_b[4][0, 0], bufs_a)
    g_b = finish_exchange("b", exchange_b, dx)
    grad_x, dmods = dx, jnp.stack(dmods)
    exchange_a = start_exchange("a", bufs_a, g_b[0])
    adam_b = adamw_group("b", 1, g_b, None, exchange_a[4])

    small = ([dmods] + [jnp.stack([grads[l][k] for l in range(DEPTH)]) for k in _SMALL[1:] + ("conv_w",)] + [loss[:, :1]])
    n_small = sum(int(np.prod(a.shape)) for a in small)
    rows_small = _small_rows(n_small)
    small_all = _allgather8("gather_small", _pack_flat(small, rows_small, F32))
    parts = [small_all[d * rows_small:(d + 1) * rows_small] for d in range(N_DEV)]
    small_sum = _unpack_flat(_sum_rows("sum_small", parts, F32), [a.shape for a in small])
    g_out = {"b_ada": small_sum[0].reshape(DEPTH, 6 * D_MODEL)}
    for k, t in zip(_SMALL[1:], small_sum[1:-2]):
        g_out[k] = t
    g_out["conv_w"] = lax.dynamic_slice_in_dim(small_sum[-2], chip * conv_w.shape[2], conv_w.shape[2], axis=2)
    loss_out = small_sum[-1][0, 0]
    dmod_all = jnp.stack([p[:DEPTH * 6].reshape(DEPTH, 6 * D_MODEL) for p in parts])
    dmod_cols = lax.dynamic_slice_in_dim(dmod_all, chip * ada_cols, ada_cols, axis=2)
    g_out["w_ada"] = jnp.stack([_matmul("ada_dw", cond_all, dmod_cols[:, l], "tn", precise=True) for l in range(DEPTH)])

    deltas, new_m, new_v = {}, {}, {}
    dep = adam_b[_KINDS[-1]][1]
    for k in ("w_ada", "conv_w"):
        shp = w[k].shape
        two_d = (int(np.prod(shp[:-1])), shp[-1])
        res = _adamw("adamw_" + k, *(t.reshape(two_d) for t in (w[k], g_out[k], m[k], v[k])), dep=dep)
        deltas[k], new_m[k], new_v[k] = (t.reshape(shp) for t in res[1:])
        dep = res[1]
    small_shapes = [w[k].shape for k in _SMALL]
    n_sm = sum(int(np.prod(s)) for s in small_shapes)
    res = _adamw("adamw_small", *[_pack_flat([t[k] for k in _SMALL], _small_rows(n_sm), F32) for t in (w, g_out, m, v)], dep=dep)[1:]
    for name_map, buf in zip((deltas, new_m, new_v), res):
        for k, t in zip(_SMALL, _unpack_flat(buf, small_shapes)):
            name_map[k] = t
    g_a = finish_exchange("a", exchange_a, res[0])
    for k, t in adamw_group("a", 0, g_a, adam_b, None).items():
        if k == "w_in":
            g_out[k], deltas[k], new_m[k], new_v[k] = (jnp.transpose(u, (0, 2, 1)) for u in t)
        else:
            g_out[k], deltas[k], new_m[k], new_v[k] = (u.reshape(w[k].shape) for u in t)

    return (loss_out, grad_x[None], *[g_out[k] for k in _ORDER], *[deltas[k] for k in _ORDER],
            *[new_m[k] for k in _ORDER], *[new_v[k] for k in _ORDER])
```

```python
import functools
import math

import numpy as np
import jax
import jax.numpy as jnp
from jax import lax
from jax.experimental import pallas as pl
from jax.experimental.pallas import tpu as pltpu

F32, BF16 = jnp.float32, jnp.bfloat16
MESH = pl.DeviceIdType.MESH

D_MODEL = 1024
DEPTH = 4
N_CHIPS = 4
N_DEV = 8
SSD_HEADS = 16
SSD_HEAD_DIM = 64
SSD_STATE = 128
SSD_CHUNK = 128
SSD_CONV = 4
CONV_DIM = 1536
ATTN_HEAD_DIM = 128
ATTN_GROUP_W = 512
DILATIONS = (1, 4, 16)
ATTN_STEPS = 128
POOL_WINDOWS = (2, 4, 8, 16)
POOL_GW = 256
D_FF = 4096
EPS = 1e-6
IN_SIZES = (1024, 1536, 16, 1536, 1536, 1536, 1024, 3072)
IN_WIDTH = sum(IN_SIZES)
P_XBC, P_Q, P_K, P_V, P_GATES, P_Z, P_U, P_DT = 0, 1536, 3072, 4608, 6144, 9216, 10240, 11264
P_WIDTH = 12288
LANES = 128
NEG = -1e30
VMEM_LIMIT = 56 * 1024 * 1024

ADAM_LR, ADAM_B1, ADAM_B2, ADAM_EPS, ADAM_WD, ADAM_STEP = 0.001, 0.9, 0.999, 1e-08, 0.01, 10


def _alibi_slopes(n):
    def pow2(k):
        start = 2.0 ** (-8.0 / k)
        return [start ** (i + 1) for i in range(k)]
    if math.log2(n).is_integer():
        s = pow2(n)
    else:
        c = 2 ** math.floor(math.log2(n))
        s = pow2(c) + pow2(2 * c)[0::2][: n - c]
    return np.sort(np.asarray(s, np.float32))[::-1].copy()


SLOPES = _alibi_slopes(12).reshape(3, 4)


def _cp(*sem):
    return pltpu.CompilerParams(dimension_semantics=sem, vmem_limit_bytes=VMEM_LIMIT)


_DIMS = {"nn": (((1,), (0,)), ((), ())), "nt": (((1,), (1,)), ((), ())), "tn": (((0,), (0,)), ((), ()))}


def _dot(a, b, mode):
    return lax.dot_general(a.astype(BF16), b.astype(BF16), _DIMS[mode], preferred_element_type=F32)


@functools.partial(jax.custom_vjp, nondiff_argnums=(2,))
def _bdot(a, b, mode):
    return _dot(a, b, mode)


def _bdot_fwd(a, b, mode):
    return _dot(a, b, mode), (a, b)


def _bdot_bwd(mode, res, ct):
    a, b = res
    if mode == "nn":
        return _dot(ct, b, "nt"), _dot(a, ct, "tn")
    if mode == "nt":
        return _dot(ct, b, "nn"), _dot(ct, a, "tn")
    return _dot(b, ct, "nt"), _dot(a, ct, "nn")


_bdot.defvjp(_bdot_fwd, _bdot_bwd)


def _hdot(a, b):
    return jnp.dot(a, b, precision=lax.Precision.HIGHEST, preferred_element_type=F32)


def _tri(n, lower):
    r = lax.broadcasted_iota(jnp.int32, (n, n), 0)
    c = lax.broadcasted_iota(jnp.int32, (n, n), 1)
    return (r >= c if lower else r <= c).astype(F32)


@jax.custom_vjp
def _csum(a):
    return _hdot(_tri(a.shape[0], True), a)


def _csum_fwd(a):
    return _csum(a), None


def _csum_bwd(_, ct):
    return (_hdot(_tri(ct.shape[0], False), ct),)


_csum.defvjp(_csum_fwd, _csum_bwd)


def _softplus(x):
    return jnp.maximum(x, 0.0) + jnp.log(1.0 + jnp.exp(-jnp.abs(x)))


def _sigmoid(x):
    return 1.0 / (1.0 + jnp.exp(-x))


def _silu(x):
    return x * _sigmoid(x)


def _tile(n, cap):
    t = min(n, cap)
    while n % t:
        t //= 2
    return t


MM_TILE, MM_KTILE = 1024, 2048


def _matmul(name, a, b, mode, out_dtype=F32, precise=False, layer=None, chips=0, out_chips=0, into=None):
    if mode == "nn":
        (m, k), n = a.shape, (4 * chips if chips else b.shape[-1])
    elif mode == "nt":
        (m, k), n = a.shape, b.shape[-2]
    else:
        (k, m), n = a.shape, b.shape[-1]
    tm = _tile(m // N_CHIPS if (into is not None and not out_chips) else m, MM_TILE)
    tn = _tile(chips if (chips and mode == "nn") else (out_chips or n), MM_TILE)
    tk = _tile(chips if (chips and mode == "nt") else k, MM_KTILE)
    nk = k // tk
    a_spec = pl.BlockSpec((tk, tm), lambda i, j, l: (l, i)) if mode == "tn" else pl.BlockSpec((tm, tk), lambda i, j, l: (i, l))
    if chips:
        if mode == "nn":
            per = chips // tn
            b_spec = pl.BlockSpec((None, None, tk, tn), lambda i, j, l: (layer, j // per, l, j % per))
        else:
            per = chips // tk
            b_spec = pl.BlockSpec((None, None, tn, tk), lambda i, j, l: (layer, l // per, j, l % per))
    elif layer is not None:
        b_spec = (pl.BlockSpec((None, tn, tk), lambda i, j, l: (layer, j, l)) if mode == "nt"
                  else pl.BlockSpec((None, tk, tn), lambda i, j, l: (layer, l, j)))
    else:
        b_spec = pl.BlockSpec((tn, tk), lambda i, j, l: (j, l)) if mode == "nt" else pl.BlockSpec((tk, tn), lambda i, j, l: (l, j))
    if into is not None:
        buf, slot = into
        if out_chips:
            per_o = out_chips // tn
            o_spec = pl.BlockSpec((None, None, tm, tn), lambda i, j, l: (j // per_o, slot, i, j % per_o))
        else:
            per_r = m // N_CHIPS // tm
            o_spec = pl.BlockSpec((None, None, tm, tn), lambda i, j, l: (i // per_r, slot, i % per_r, j))
        o_shape = jax.ShapeDtypeStruct(buf.shape, buf.dtype)
    elif out_chips:
        per_o = out_chips // tn
        o_spec = pl.BlockSpec((None, tm, tn), lambda i, j, l: (j // per_o, i, j % per_o))
        o_shape = jax.ShapeDtypeStruct((N_CHIPS, m, out_chips), out_dtype)
    else:
        o_spec = pl.BlockSpec((tm, tn), lambda i, j, l: (i, j))
        o_shape = jax.ShapeDtypeStruct((m, n), out_dtype)

    def part(a_ref, b_ref):
        if precise:
            return lax.dot_general(a_ref[...], b_ref[...], _DIMS[mode], precision=lax.Precision.HIGHEST,
                                   preferred_element_type=F32)
        return _dot(a_ref[...], b_ref[...], mode)

    n_in = 2 if into is None else 3

    if nk == 1:
        def body(*refs):
            o_ref = refs[n_in]
            o_ref[...] = part(refs[0], refs[1]).astype(o_ref.dtype)
        scratch = []
    else:
        def body(*refs):
            o_ref, acc_ref = refs[n_in], refs[n_in + 1]
            l = pl.program_id(2)
            p = part(refs[0], refs[1])

            @pl.when(l == 0)
            def _():
                acc_ref[...] = p

            @pl.when((l > 0) & (l < nk - 1))
            def _():
                acc_ref[...] += p

            @pl.when(l == nk - 1)
            def _():
                o_ref[...] = (acc_ref[...] + p).astype(o_ref.dtype)
        scratch = [pltpu.VMEM((tm, tn), F32)]

    extra = {} if into is None else dict(input_output_aliases={2: 0})
    return pl.pallas_call(
        body, grid=(m // tm, n // tn, nk), in_specs=[a_spec, b_spec] + ([] if into is None else [pl.BlockSpec(memory_space=pl.ANY)]),
        out_specs=o_spec, out_shape=o_shape, scratch_shapes=scratch, compiler_params=_cp("parallel", "parallel", "arbitrary"),
        name=name, **extra)(*((a, b) if into is None else (a, b, into[0])))


def _group_matmul(name, a, w, mode, out_dtype=F32, layer=0):
    s = a.shape[0]
    tb = 512
    gw = POOL_GW
    if mode == "tn":
        def body(a_ref, b_ref, o_ref):
            part = _dot(a_ref[...], b_ref[...], "tn")

            @pl.when(pl.program_id(1) == 0)
            def _():
                o_ref[0] = part

            @pl.when(pl.program_id(1) > 0)
            def _():
                o_ref[0] += part

        return pl.pallas_call(
            body, grid=(4, s // tb),
            in_specs=[pl.BlockSpec((tb, gw), lambda g, i: (i, g)), pl.BlockSpec((tb, gw), lambda g, i: (i, g))],
            out_specs=pl.BlockSpec((1, gw, gw), lambda g, i: (g, 0, 0)),
            out_shape=jax.ShapeDtypeStruct((4, gw, gw), F32),
            compiler_params=_cp("parallel", "arbitrary"), name=name)(a, w)

    def body(a_ref, w_ref, o_ref):
        o_ref[...] = _dot(a_ref[...], w_ref[...], mode).astype(o_ref.dtype)

    return pl.pallas_call(
        body, grid=(s // tb, 4),
        in_specs=[pl.BlockSpec((tb, gw), lambda i, g: (i, g)), pl.BlockSpec((None, None, gw, gw), lambda i, g: (layer, g, 0, 0))],
        out_specs=pl.BlockSpec((tb, gw), lambda i, g: (i, g)),
        out_shape=jax.ShapeDtypeStruct((s, 4 * gw), out_dtype),
        compiler_params=_cp("parallel", "parallel"), name=name)(a, w)


def _rspec(tb, width, cb):
    return pl.BlockSpec((tb, width), lambda i: (i, cb))


def _pspec(shape):
    return pl.BlockSpec(shape, lambda i: (0, 0))


def _rowwise_fwd(name, f, rows, pars, outs, tb=256):
    s = rows[0][0].shape[0]
    nin = len(rows) + len(pars)

    def body(*refs):
        res = f(*[r[...].astype(F32) for r in refs[:nin]])
        for o, v in zip(refs[nin:], res):
            o[...] = v.astype(o.dtype)

    return pl.pallas_call(
        body, grid=(s // tb,),
        in_specs=[_rspec(tb, w, cb) for _, w, cb in rows] + [_pspec(p.shape) for p in pars],
        out_specs=[_rspec(tb, w, 0) for w, _ in outs],
        out_shape=[jax.ShapeDtypeStruct((s, w), dt) for w, dt in outs],
        compiler_params=_cp("parallel"), name=name)(*[r[0] for r in rows], *pars)


def _rowwise_bwd(name, f, rows, pars, cts, need, add=None, tb=256, gdt=None):
    s = rows[0][0].shape[0]
    nr, npar, nc = len(rows), len(pars), len(cts)
    nin = nr + npar + nc + (1 if add is not None else 0)

    def body(*refs):
        ins = [r[...].astype(F32) for r in refs[:nr + npar]]
        _, vjp = jax.vjp(f, *ins)
        g = vjp(tuple(c[...].astype(F32) for c in refs[nr + npar:nr + npar + nc]))
        outs = refs[nin:]
        k = 0
        for j in range(nr):
            if need[j]:
                v = g[j]
                if add is not None and add[0] == j:
                    v = v + refs[nin - 1][...]
                outs[k][...] = v.astype(outs[k].dtype)
                k += 1
        first = pl.program_id(0) == 0
        for j in range(npar):
            o, v = outs[k + j], g[nr + j]

            @pl.when(first)
            def _(o=o, v=v):
                o[...] = v

            @pl.when(jnp.logical_not(first))
            def _(o=o, v=v):
                o[...] += v

    in_specs = ([_rspec(tb, w, cb) for _, w, cb in rows] + [_pspec(p.shape) for p in pars]
                + [_rspec(tb, w, cb) for _, w, cb in cts])
    args = [r[0] for r in rows] + list(pars) + [c[0] for c in cts]
    if add is not None:
        in_specs.append(_rspec(tb, rows[add[0]][1], 0))
        args.append(add[1])
    gr = [(w, F32) for (_, w, _), nd in zip(rows, need) if nd]
    if gdt is not None:
        gr = [(w, dt) for (w, _), dt in zip(gr, gdt)]
    return pl.pallas_call(
        body, grid=(s // tb,), in_specs=in_specs,
        out_specs=[_rspec(tb, w, 0) for w, _ in gr] + [_pspec(p.shape) for p in pars],
        out_shape=[jax.ShapeDtypeStruct((s, w), dt) for w, dt in gr] + [jax.ShapeDtypeStruct(p.shape, F32) for p in pars],
        compiler_params=_cp("arbitrary"), name=name)(*args)


def _f_norm(x, nw, sc, sh):
    r = lax.rsqrt(jnp.mean(x * x, axis=-1, keepdims=True) + EPS)
    return ((x * r * nw) * (1.0 + sc) + sh,)


def _f_ssdgate(y, z, w):
    y2 = y * _silu(z)
    low = lax.broadcasted_iota(jnp.int32, y2.shape, 1) < 512
    sq = y2 * y2
    m0 = jnp.sum(jnp.where(low, sq, 0.0), axis=-1, keepdims=True) / 512.0
    m1 = jnp.sum(jnp.where(low, 0.0, sq), axis=-1, keepdims=True) / 512.0
    r = jnp.where(low, lax.rsqrt(m0 + EPS), lax.rsqrt(m1 + EPS))
    return (y2 * r * w,)


def _head_rms(t, w):
    outs = []
    for h in range(t.shape[1] // ATTN_HEAD_DIM):
        th = t[:, h * ATTN_HEAD_DIM:(h + 1) * ATTN_HEAD_DIM]
        outs.append(th * lax.rsqrt(jnp.mean(th * th, axis=-1, keepdims=True) + EPS) * w)
    return jnp.concatenate(outs, axis=1)


def _f_qknorm(q, k, v, qw, kw):
    return _head_rms(q, qw), _head_rms(k, kw), v


def _f_combine(o1, o2, o3, l1, l2, l3):
    m = lax.stop_gradient(jnp.maximum(jnp.maximum(l1, l2), l3))
    e1, e2, e3 = jnp.exp(l1 - m), jnp.exp(l2 - m), jnp.exp(l3 - m)
    return ((e1 * o1 + e2 * o2 + e3 * o3) / (e1 + e2 + e3),)


def _f_poolscale(pm, ps):
    return (pm * ps,)


def _f_merge(gates, ys, ya, yp):
    g = _sigmoid(gates)
    return (g[:, 0:1024] * ys + g[:, 1024:2048] * ya + g[:, 2048:3072] * yp,)


def _f_resid(x, o, g):
    return (x + g * o,)


def _f_relu2(a):
    return (jnp.square(jnp.maximum(a, 0.0)),)


def _loss_and_grad(y, tgt, tb=512):
    s, d = y.shape

    def body(y_ref, t_ref, dy_ref, l_ref):
        e = y_ref[...] - t_ref[...]
        dy_ref[...] = e * (1.0 / d)
        part = jnp.zeros((1, LANES), F32) + jnp.sum(e * e) * (0.5 / d)

        @pl.when(pl.program_id(0) == 0)
        def _():
            l_ref[...] = part

        @pl.when(pl.program_id(0) > 0)
        def _():
            l_ref[...] += part

    return pl.pallas_call(
        body, grid=(s // tb,), in_specs=[_rspec(tb, d, 0), _rspec(tb, d, 0)],
        out_specs=[_rspec(tb, d, 0), _pspec((1, LANES))],
        out_shape=[jax.ShapeDtypeStruct((s, d), F32), jax.ShapeDtypeStruct((1, LANES), F32)],
        compiler_params=_cp("arbitrary"), name="loss")(y, tgt)


def _shift_down(x, j):
    rows = lax.broadcasted_iota(jnp.int32, x.shape, 0)
    return jnp.where(rows < j, 0.0, pltpu.roll(x, j, 0))


def _shift_up(x, j):
    s = x.shape[0]
    rows = lax.broadcasted_iota(jnp.int32, x.shape, 0)
    return jnp.where(rows >= s - j, 0.0, pltpu.roll(x, s - j, 0))


CONV_CB = 256


def _conv_pre(x, w_ref, b_ref):
    acc = b_ref[...] + w_ref[SSD_CONV - 1:SSD_CONV, :] * x
    for j in range(1, SSD_CONV):
        acc = acc + w_ref[SSD_CONV - 1 - j:SSD_CONV - j, :] * _shift_down(x, j)
    return acc


def _conv_fwd(proj, cw, cb):
    s = proj.shape[0]

    def body(x_ref, w_ref, b_ref, o_ref):
        o_ref[...] = _silu(_conv_pre(x_ref[...], w_ref, b_ref))

    return pl.pallas_call(
        body, grid=(CONV_DIM // CONV_CB,),
        in_specs=[pl.BlockSpec((s, CONV_CB), lambda i: (0, P_XBC // CONV_CB + i)),
                  pl.BlockSpec((SSD_CONV, CONV_CB), lambda i: (0, i)), pl.BlockSpec((1, CONV_CB), lambda i: (0, i))],
        out_specs=pl.BlockSpec((s, CONV_CB), lambda i: (0, i)),
        out_shape=jax.ShapeDtypeStruct((s, CONV_DIM), F32), compiler_params=_cp("parallel"), name="conv_fwd")(proj, cw, cb)


def _conv_bwd(proj, cw, cb, dout):
    s = proj.shape[0]

    def body(x_ref, w_ref, b_ref, d_ref, dx_ref, dw_ref, db_ref):
        x = x_ref[...]
        a = _conv_pre(x, w_ref, b_ref)
        sg = _sigmoid(a)
        da = d_ref[...] * (sg + a * sg * (1.0 - sg))
        db_ref[...] = jnp.sum(da, axis=0, keepdims=True)
        dx = w_ref[SSD_CONV - 1:SSD_CONV, :] * da
        dw_ref[SSD_CONV - 1:SSD_CONV, :] = jnp.sum(da * x, axis=0, keepdims=True)
        for j in range(1, SSD_CONV):
            dx = dx + w_ref[SSD_CONV - 1 - j:SSD_CONV - j, :] * _shift_up(da, j)
            dw_ref[SSD_CONV - 1 - j:SSD_CONV - j, :] = jnp.sum(da * _shift_down(x, j), axis=0, keepdims=True)
        dx_ref[...] = dx.astype(dx_ref.dtype)

    return pl.pallas_call(
        body, grid=(CONV_DIM // CONV_CB,),
        in_specs=[pl.BlockSpec((s, CONV_CB), lambda i: (0, P_XBC // CONV_CB + i)),
                  pl.BlockSpec((SSD_CONV, CONV_CB), lambda i: (0, i)), pl.BlockSpec((1, CONV_CB), lambda i: (0, i)),
                  pl.BlockSpec((s, CONV_CB), lambda i: (0, i))],
        out_specs=[pl.BlockSpec((s, CONV_CB), lambda i: (0, i)), pl.BlockSpec((SSD_CONV, CONV_CB), lambda i: (0, i)),
                   pl.BlockSpec((1, CONV_CB), lambda i: (0, i))],
        out_shape=[jax.ShapeDtypeStruct((s, CONV_DIM), BF16), jax.ShapeDtypeStruct((SSD_CONV, CONV_DIM), F32),
                   jax.ShapeDtypeStruct((1, CONV_DIM), F32)],
        compiler_params=_cp("parallel"), name="conv_bwd")(proj, cw, cb, dout)


def _pool_window_sum(x, g, shift):
    s2 = x + shift(x, 1)
    s4 = s2 + shift(s2, 2)
    s8 = s4 + shift(s4, 4)
    s16 = s8 + shift(s8, 8)
    return jnp.where(g == 0, s2, jnp.where(g == 1, s4, jnp.where(g == 2, s8, s16)))


def _pool_count(shape, g):
    rows = lax.broadcasted_iota(jnp.int32, shape, 0)
    return jnp.minimum(rows + 1, jnp.left_shift(2, g)).astype(F32)


def _pool_fwd(proj):
    s = proj.shape[0]

    def body(u_ref, o_ref):
        g = pl.program_id(0)
        u = u_ref[...]
        o_ref[...] = (_pool_window_sum(u, g, _shift_down) / _pool_count(u.shape, g) - u).astype(o_ref.dtype)

    return pl.pallas_call(
        body, grid=(4,), in_specs=[pl.BlockSpec((s, POOL_GW), lambda g: (0, P_U // POOL_GW + g))],
        out_specs=pl.BlockSpec((s, POOL_GW), lambda g: (0, g)),
        out_shape=jax.ShapeDtypeStruct((s, 4 * POOL_GW), BF16), compiler_params=_cp("parallel"), name="pool_fwd")(proj)


def _pool_bwd(dp):
    s = dp.shape[0]

    def body(d_ref, o_ref):
        g = pl.program_id(0)
        d = d_ref[...]
        o_ref[...] = (_pool_window_sum(d / _pool_count(d.shape, g), g, _shift_up) - d).astype(o_ref.dtype)

    return pl.pallas_call(
        body, grid=(4,), in_specs=[pl.BlockSpec((s, POOL_GW), lambda g: (0, g))],
        out_specs=pl.BlockSpec((s, POOL_GW), lambda g: (0, g)),
        out_shape=jax.ShapeDtypeStruct((s, 4 * POOL_GW), BF16), compiler_params=_cp("parallel"), name="pool_bwd")(dp)


N_PAIRS = SSD_HEADS // 2
STATE_ROWS = N_PAIRS * SSD_STATE


def _ssd_chunk(xbc, dtr, hprev, dtb, alog, dsk):
    L = xbc.shape[0]
    xs, bm, cm = xbc[:, 0:1024], xbc[:, 1024:1280], xbc[:, 1280:1536]
    dt = _softplus(dtr + dtb)
    a = dt * (-jnp.exp(alog))
    acum = _csum(a)
    alast = jnp.sum(a, axis=0, keepdims=True)
    xdt = xs * dt
    xdecay = xdt * jnp.exp(alast - acum)
    eacum = jnp.exp(acum)
    elast = jnp.exp(alast)
    cb = [_bdot(cm[:, g * 128:(g + 1) * 128], bm[:, g * 128:(g + 1) * 128], "nt") for g in range(2)]
    rows = lax.broadcasted_iota(jnp.int32, (L, L), 0)
    cols = lax.broadcasted_iota(jnp.int32, (L, L), 1)
    causal = rows >= cols
    lane = lax.broadcasted_iota(jnp.int32, (L, LANES), 1)
    sub = lax.broadcasted_iota(jnp.int32, (LANES, L), 0)
    ys, hs = [], []
    for p in range(N_PAIRS):
        g = p // (N_PAIRS // 2)
        sl = slice(p * LANES, (p + 1) * LANES)
        ac = acum[:, sl]
        act = ac.T
        xp = xdt[:, sl]
        hp = hprev[p * SSD_STATE:(p + 1) * SSD_STATE, :]
        y = _bdot(cm[:, g * 128:(g + 1) * 128], hp, "nn") * eacum[:, sl] + dsk[:, sl] * xs[:, sl]
        for half in range(2):
            l0 = half * SSD_HEAD_DIM
            col = jnp.sum(jnp.where(lane == l0, ac, 0.0), axis=1, keepdims=True)
            row = jnp.sum(jnp.where(sub == l0, act, 0.0), axis=0, keepdims=True)
            decay = jnp.exp(jnp.where(causal, col - row, NEG))
            xh = jnp.where((lane >= l0) & (lane < l0 + SSD_HEAD_DIM), xp, 0.0)
            y = y + _bdot(cb[g] * decay, xh, "nn")
        ys.append(y)
        hs.append(elast[:, sl] * hp + _bdot(bm[:, g * 128:(g + 1) * 128], xdecay[:, sl], "tn"))
    return tuple(ys), tuple(hs)


def _ssd_fwd(xbc, proj, dtb, alog, dsk):
    s = xbc.shape[0]
    nc = s // SSD_CHUNK

    def body(x_ref, dt_ref, b_ref, a_ref, d_ref, y_ref, hist_ref, h_ref):
        @pl.when(pl.program_id(0) == 0)
        def _():
            h_ref[...] = jnp.zeros_like(h_ref)

        hprev = h_ref[...]
        hist_ref[...] = hprev
        ys, hs = _ssd_chunk(x_ref[...], dt_ref[...], hprev, b_ref[...], a_ref[...], d_ref[...])
        for p in range(N_PAIRS):
            y_ref[:, p * LANES:(p + 1) * LANES] = ys[p]
            h_ref[p * SSD_STATE:(p + 1) * SSD_STATE, :] = hs[p]

    return pl.pallas_call(
        body, grid=(nc,),
        in_specs=[pl.BlockSpec((SSD_CHUNK, CONV_DIM), lambda i: (i, 0)),
                  pl.BlockSpec((SSD_CHUNK, 1024), lambda i: (i, P_DT // 1024)),
                  _pspec((1, 1024)), _pspec((1, 1024)), _pspec((1, 1024))],
        out_specs=[pl.BlockSpec((SSD_CHUNK, 1024), lambda i: (i, 0)), pl.BlockSpec((STATE_ROWS, LANES), lambda i: (i, 0))],
        out_shape=[jax.ShapeDtypeStruct((s, 1024), F32), jax.ShapeDtypeStruct((nc * STATE_ROWS, LANES), F32)],
        scratch_shapes=[pltpu.VMEM((STATE_ROWS, LANES), F32)],
        compiler_params=_cp("arbitrary"), name="ssd_fwd")(xbc, proj, dtb, alog, dsk)


def _ssd_bwd(xbc, proj, hist, dtb, alog, dsk, dy):
    s = xbc.shape[0]
    nc = s // SSD_CHUNK

    def body(x_ref, dt_ref, hist_ref, b_ref, a_ref, d_ref, dy_ref, dx_ref, ddt_ref, db_ref, da_ref, dd_ref, dh_ref):
        first = pl.program_id(0) == 0

        @pl.when(first)
        def _():
            dh_ref[...] = jnp.zeros_like(dh_ref)

        _, vjp = jax.vjp(_ssd_chunk, x_ref[...], dt_ref[...], hist_ref[...], b_ref[...], a_ref[...], d_ref[...])
        dys = tuple(dy_ref[:, p * LANES:(p + 1) * LANES] for p in range(N_PAIRS))
        dhs = tuple(dh_ref[p * SSD_STATE:(p + 1) * SSD_STATE, :] for p in range(N_PAIRS))
        dx, ddt, dhp, db, da, dd = vjp((dys, dhs))
        dx_ref[...] = dx
        ddt_ref[...] = ddt.astype(ddt_ref.dtype)
        dh_ref[...] = dhp
        for o, v in ((db_ref, db), (da_ref, da), (dd_ref, dd)):
            @pl.when(first)
            def _(o=o, v=v):
                o[...] = v

            @pl.when(jnp.logical_not(first))
            def _(o=o, v=v):
                o[...] += v

    rev = lambda i: (nc - 1 - i, 0)
    return pl.pallas_call(
        body, grid=(nc,),
        in_specs=[pl.BlockSpec((SSD_CHUNK, CONV_DIM), rev),
                  pl.BlockSpec((SSD_CHUNK, 1024), lambda i: (nc - 1 - i, P_DT // 1024)),
                  pl.BlockSpec((STATE_ROWS, LANES), rev),
                  _pspec((1, 1024)), _pspec((1, 1024)), _pspec((1, 1024)),
                  pl.BlockSpec((SSD_CHUNK, 1024), rev)],
        out_specs=[pl.BlockSpec((SSD_CHUNK, CONV_DIM), rev), pl.BlockSpec((SSD_CHUNK, 1024), rev),
                   _pspec((1, 1024)), _pspec((1, 1024)), _pspec((1, 1024))],
        out_shape=[jax.ShapeDtypeStruct((s, CONV_DIM), F32), jax.ShapeDtypeStruct((s, 1024), BF16)]
        + [jax.ShapeDtypeStruct((1, 1024), F32)] * 3,
        scratch_shapes=[pltpu.VMEM((STATE_ROWS, LANES), F32)],
        compiler_params=_cp("arbitrary"), name="ssd_bwd")(xbc, proj, hist, dtb, alog, dsk, dy)


def _attn_head(q, kp, kc, vp, vc, has_prev, slope):
    scale = ATTN_HEAD_DIM ** -0.5
    n = ATTN_STEPS
    qi = lax.broadcasted_iota(jnp.int32, (n, n), 0)
    kj = lax.broadcasted_iota(jnp.int32, (n, n), 1)
    sp = jnp.where((kj >= qi) & has_prev, _bdot(q, kp, "nt") * scale - slope * (qi + n - kj).astype(F32), NEG)
    sc = jnp.where(kj <= qi, _bdot(q, kc, "nt") * scale - slope * (qi - kj).astype(F32), NEG)
    m = lax.stop_gradient(jnp.maximum(jnp.max(sp, axis=1, keepdims=True), jnp.max(sc, axis=1, keepdims=True)))
    pp, pc = jnp.exp(sp - m), jnp.exp(sc - m)
    den = jnp.sum(pp, axis=1, keepdims=True) + jnp.sum(pc, axis=1, keepdims=True)
    o = (_bdot(pp, vp, "nn") + _bdot(pc, vc, "nn")) / den
    return o, jnp.broadcast_to(m + jnp.log(den), (n, ATTN_HEAD_DIM))


def _head_slope(gi, h):
    s = [float(v) * DILATIONS[gi] for v in SLOPES[gi]]
    return jnp.where(h == 0, s[0], jnp.where(h == 1, s[1], jnp.where(h == 2, s[2], s[3])))


ATTN_UNROLL = 4


def _attn_heads_per_block(d):
    return 4 if d == 1 else 1


def _unit(ref, u, d):
    if d == 1:
        return ref.at[:, u * ATTN_HEAD_DIM:(u + 1) * ATTN_HEAD_DIM]
    return ref.at[pl.ds(u, ATTN_STEPS, stride=d), :]


def _for_units(units, step):
    if units == ATTN_UNROLL:
        for u in range(units):
            step(u)
        return

    def body(i, carry):
        for j in range(ATTN_UNROLL):
            step(i * ATTN_UNROLL + j)
        return carry

    lax.fori_loop(0, units // ATTN_UNROLL, body, 0)


def _attn_fwd(qn, kn, vv, gi):
    d = DILATIONS[gi]
    s = qn.shape[0]
    span = ATTN_STEPS * d
    nb = s // span

    hb, units = _attn_heads_per_block(d), _attn_heads_per_block(d) * d

    def body(q_ref, kp_ref, kc_ref, vp_ref, vc_ref, o_ref, l_ref):
        h0, b = pl.program_id(0) * hb, pl.program_id(1)

        def step(u):
            o, l = _attn_head(_unit(q_ref, u, d)[...], _unit(kp_ref, u, d)[...], _unit(kc_ref, u, d)[...],
                              _unit(vp_ref, u, d)[...], _unit(vc_ref, u, d)[...], b > 0, _head_slope(gi, h0 + u // d))
            _unit(o_ref, u, d)[...] = o
            _unit(l_ref, u, d)[...] = l

        _for_units(units, step)

    w = hb * ATTN_HEAD_DIM
    cur = pl.BlockSpec((span, w), lambda h, b: (b, (gi * 4) // hb + h))
    prev = pl.BlockSpec((span, w), lambda h, b: (jnp.maximum(b - 1, 0), (gi * 4) // hb + h))
    out = pl.BlockSpec((span, w), lambda h, b: (b, h))
    return pl.pallas_call(
        body, grid=(4 // hb, nb), in_specs=[cur, prev, cur, prev, cur], out_specs=[out, out],
        out_shape=[jax.ShapeDtypeStruct((s, ATTN_GROUP_W), F32)] * 2,
        compiler_params=_cp("parallel", "parallel"), name=f"attn_fwd_g{gi}")(qn, kn, kn, vv, vv)


def _attn_bwd(qn, kn, vv, do, dl, gi):
    d = DILATIONS[gi]
    s = qn.shape[0]
    span = ATTN_STEPS * d
    nb = s // span

    hb, units = _attn_heads_per_block(d), _attn_heads_per_block(d) * d

    def body(q_ref, kp_ref, kc_ref, vp_ref, vc_ref, do_ref, dl_ref, dq_ref, dk_ref, dv_ref, ck, cv):
        h0, bi = pl.program_id(0) * hb, pl.program_id(1)

        @pl.when(bi == 0)
        def _():
            ck[...] = jnp.zeros_like(ck)
            cv[...] = jnp.zeros_like(cv)

        has_prev = bi < nb - 1

        def step(u):
            f = functools.partial(_attn_head, has_prev=has_prev, slope=_head_slope(gi, h0 + u // d))
            _, vjp = jax.vjp(f, *(_unit(r, u, d)[...] for r in (q_ref, kp_ref, kc_ref, vp_ref, vc_ref)))
            dq, dkp, dkc, dvp, dvc = vjp((_unit(do_ref, u, d)[...], _unit(dl_ref, u, d)[...]))
            _unit(dq_ref, u, d)[...] = dq
            _unit(dk_ref, u, d)[...] = dkc + ck[u]
            _unit(dv_ref, u, d)[...] = dvc + cv[u]
            ck[u] = dkp
            cv[u] = dvp

        _for_units(units, step)

    w = hb * ATTN_HEAD_DIM
    cur = pl.BlockSpec((span, w), lambda h, b: (nb - 1 - b, (gi * 4) // hb + h))
    prev = pl.BlockSpec((span, w), lambda h, b: (jnp.maximum(nb - 2 - b, 0), (gi * 4) // hb + h))
    out = pl.BlockSpec((span, w), lambda h, b: (nb - 1 - b, h))
    res = (units, ATTN_STEPS, ATTN_HEAD_DIM)
    return pl.pallas_call(
        body, grid=(4 // hb, nb), in_specs=[cur, prev, cur, prev, cur, out, out], out_specs=[out, out, out],
        out_shape=[jax.ShapeDtypeStruct((s, ATTN_GROUP_W), F32)] * 3,
        scratch_shapes=[pltpu.VMEM(res, F32), pltpu.VMEM(res, F32)],
        compiler_params=_cp("parallel", "arbitrary"), name=f"attn_bwd_g{gi}")(qn, kn, kn, vv, vv, do, dl)


def _layer_fwd(x, mod, W, l, late=None):
    sh1, sc1, g1, sh2, sc2, g2 = (mod[i:i + 1] for i in range(6))
    (h,) = _rowwise_fwd("norm1", _f_norm, [(x, 1024, 0)], [W["norm1_w"], sc1, sh1], [(1024, BF16)])
    proj = _matmul("in_proj", h, W["w_in"], "nt")
    xbc = _conv_fwd(proj, W["conv_w"], W["conv_b"])
    y, hist = _ssd_fwd(xbc, proj, W["dt_bias"], W["a_log"], W["d_skip"])
    if late is not None:
        W.update(late(y))
    (yn,) = _rowwise_fwd("ssd_gate", _f_ssdgate, [(y, 1024, 0), (proj, 1024, P_Z // 1024)], [W["ssd_norm_w"]], [(1024, BF16)])
    y_ssd = _matmul("ssd_out", yn, W["w_ssd_out"], "nn", layer=l)
    qn, kn, vv = _rowwise_fwd("qk_norm", _f_qknorm, [(proj, 1536, P_Q // 1536), (proj, 1536, P_K // 1536), (proj, 1536, P_V // 1536)],
                              [W["q_norm_w"], W["k_norm_w"]], [(1536, F32)] * 3)
    ol = [_attn_fwd(qn, kn, vv, gi) for gi in range(3)]
    (o,) = _rowwise_fwd("attn_combine", _f_combine, [(t[0], 512, 0) for t in ol] + [(t[1], 512, 0) for t in ol], [], [(512, BF16)])
    y_attn = _matmul("attn_out", o, W["w_attn_out"], "nn", layer=l, chips=256)
    pooled = _pool_fwd(proj)
    pm = _group_matmul("pool_mix", pooled, W["w_pool_mix"], "nn", layer=l)
    (ps,) = _rowwise_fwd("pool_scale", _f_poolscale, [(pm, 1024, 0)], [W["pool_scale"]], [(1024, BF16)])
    y_pool = _matmul("pool_out", ps, W["w_pool_out"], "nn", layer=l)
    (merged,) = _rowwise_fwd("merge", _f_merge, [(proj, 3072, P_GATES // 3072), (y_ssd, 1024, 0), (y_attn, 1024, 0), (y_pool, 1024, 0)],
                             [], [(1024, BF16)])
    mo = _matmul("mix_out", merged, W["w_out"], "nn", layer=l)
    (x1,) = _rowwise_fwd("resid1", _f_resid, [(x, 1024, 0), (mo, 1024, 0)], [g1], [(1024, F32)])
    (h2,) = _rowwise_fwd("norm2", _f_norm, [(x1, 1024, 0)], [W["norm2_w"], sc2, sh2], [(1024, BF16)])
    a = _matmul("ff1", h2, W["w_ff1"], "nn", layer=l, chips=1024)
    (r,) = _rowwise_fwd("relu2", _f_relu2, [(a, D_FF, 0)], [], [(D_FF, BF16)], tb=128)
    ff = _matmul("ff2", r, W["w_ff2"], "nn", layer=l)
    (x2,) = _rowwise_fwd("resid2", _f_resid, [(x1, 1024, 0), (ff, 1024, 0)], [g2], [(1024, F32)])
    saved = dict(x=x, h=h, proj=proj, xbc=xbc, y=y, hist=hist, yn=yn, y_ssd=y_ssd, qn=qn, kn=kn, vv=vv, ol=ol, o=o,
                 y_attn=y_attn, pooled=pooled, pm=pm, ps=ps, y_pool=y_pool, merged=merged, mo=mo, x1=x1, h2=h2, a=a, r=r, ff=ff)
    return x2, saved


def _layer_bwd(dx2, mod, W, sv, l, bufs):
    sh1, sc1, g1, sh2, sc2, g2 = (mod[i:i + 1] for i in range(6))
    g = {}
    dx1a, dff, dg2 = _rowwise_bwd("resid2_bwd", _f_resid, [(sv["x1"], 1024, 0), (sv["ff"], 1024, 0)], [g2], [(dx2, 1024, 0)],
                                  [True, True], gdt=[F32, BF16])
    g["w_ff2"] = _matmul("ff2_dw", sv["r"], dff, "tn", BF16, into=(bufs["w_ff2"], l))
    dr = _matmul("ff2_dx", dff, W["w_ff2"], "nt", layer=l)
    (da,) = _rowwise_bwd("relu2_bwd", _f_relu2, [(sv["a"], D_FF, 0)], [], [(dr, D_FF, 0)], [True], tb=128, gdt=[BF16])
    g["w_ff1"] = _matmul("ff1_dw", sv["h2"], da, "tn", BF16, out_chips=1024, into=(bufs["w_ff1"], l))
    dh2 = _matmul("ff1_dx", da, W["w_ff1"], "nt", layer=l, chips=1024)
    dx1, g["norm2_w"], dsc2, dsh2 = _rowwise_bwd("norm2_bwd", _f_norm, [(sv["x1"], 1024, 0)], [W["norm2_w"], sc2, sh2],
                                                 [(dh2, 1024, 0)], [True], add=(0, dx1a))
    dxa, dmo, dg1 = _rowwise_bwd("resid1_bwd", _f_resid, [(sv["x"], 1024, 0), (sv["mo"], 1024, 0)], [g1], [(dx1, 1024, 0)],
                                 [True, True], gdt=[F32, BF16])
    g["w_out"] = _matmul("mix_out_dw", sv["merged"], dmo, "tn", BF16, into=(bufs["w_out"], l))
    dmerged = _matmul("mix_out_dx", dmo, W["w_out"], "nt", layer=l)
    proj = sv["proj"]
    dgates, dy_ssd, dy_attn, dy_pool = _rowwise_bwd(
        "merge_bwd", _f_merge, [(proj, 3072, P_GATES // 3072), (sv["y_ssd"], 1024, 0), (sv["y_attn"], 1024, 0), (sv["y_pool"], 1024, 0)],
        [], [(dmerged, 1024, 0)], [True] * 4, gdt=[BF16] * 4)
    g["w_pool_out"] = _matmul("pool_out_dw", sv["ps"], dy_pool, "tn", BF16, into=(bufs["w_pool_out"], l))
    dps = _matmul("pool_out_dx", dy_pool, W["w_pool_out"], "nt", layer=l)
    dpm, g["pool_scale"] = _rowwise_bwd("pool_scale_bwd", _f_poolscale, [(sv["pm"], 1024, 0)], [W["pool_scale"]], [(dps, 1024, 0)],
                                        [True], gdt=[BF16])
    dmix = _group_matmul("pool_mix_dw", sv["pooled"], dpm, "tn")
    g["w_pool_mix"] = bufs["w_pool_mix"].at[:, l].set(
        jnp.moveaxis(dmix.reshape(4, N_CHIPS, POOL_GW // N_CHIPS, POOL_GW), 1, 0).astype(BF16))
    dpooled = _group_matmul("pool_mix_dx", dpm, W["w_pool_mix"], "nt", layer=l)
    du = _pool_bwd(dpooled)
    g["w_attn_out"] = _matmul("attn_out_dw", sv["o"], dy_attn, "tn", BF16, out_chips=256, into=(bufs["w_attn_out"], l))
    do = _matmul("attn_out_dx", dy_attn, W["w_attn_out"], "nt", layer=l, chips=256)
    ol = sv["ol"]
    dol = _rowwise_bwd("attn_combine_bwd", _f_combine, [(t[0], 512, 0) for t in ol] + [(t[1], 512, 0) for t in ol], [],
                       [(do, 512, 0)], [True] * 6)
    dqs, dks, dvs = zip(*[_attn_bwd(sv["qn"], sv["kn"], sv["vv"], dol[gi], dol[3 + gi], gi) for gi in range(3)])
    dqn, dkn, dvv = (jnp.concatenate(t, axis=1) for t in (dqs, dks, dvs))
    dq, dk, dv, g["q_norm_w"], g["k_norm_w"] = _rowwise_bwd(
        "qk_norm_bwd", _f_qknorm, [(proj, 1536, P_Q // 1536), (proj, 1536, P_K // 1536), (proj, 1536, P_V // 1536)],
        [W["q_norm_w"], W["k_norm_w"]], [(dqn, 1536, 0), (dkn, 1536, 0), (dvv, 1536, 0)], [True] * 3, gdt=[BF16] * 3)
    g["w_ssd_out"] = _matmul("ssd_out_dw", sv["yn"], dy_ssd, "tn", BF16, into=(bufs["w_ssd_out"], l))
    dyn = _matmul("ssd_out_dx", dy_ssd, W["w_ssd_out"], "nt", layer=l)
    dy, dz, g["ssd_norm_w"] = _rowwise_bwd("ssd_gate_bwd", _f_ssdgate, [(sv["y"], 1024, 0), (proj, 1024, P_Z // 1024)], [W["ssd_norm_w"]],
                                           [(dyn, 1024, 0)], [True, True], gdt=[F32, BF16])
    dxbc, ddt, g["dt_bias"], g["a_log"], g["d_skip"] = _ssd_bwd(sv["xbc"], proj, sv["hist"], W["dt_bias"], W["a_log"], W["d_skip"], dy)
    dxbc_raw, g["conv_w"], g["conv_b"] = _conv_bwd(proj, W["conv_w"], W["conv_b"], dxbc)
    dproj = jnp.concatenate([dxbc_raw, dq, dk, dv, dgates, dz, du, ddt], axis=1)
    g["w_in"] = _matmul("in_proj_dw", dproj, sv["h"], "tn", BF16)
    dh = _matmul("in_proj_dx", dproj, W["w_in"], "nn")
    dx, g["norm1_w"], dsc1, dsh1 = _rowwise_bwd("norm1_bwd", _f_norm, [(sv["x"], 1024, 0)], [W["norm1_w"], sc1, sh1],
                                                [(dh, 1024, 0)], [True], add=(0, dxa))
    dmod = jnp.concatenate([dsh1, dsc1, dg1, dsh2, dsc2, dg2], axis=0)
    return dx, dmod, g


def _expand_heads(t):
    return jnp.repeat(t, SSD_HEAD_DIM, axis=-1)


def _reduce_heads(t):
    return t.reshape(t.shape[:-1] + (SSD_HEADS, SSD_HEAD_DIM)).sum(-1)


_IN_SPLITS = np.cumsum((0,) + IN_SIZES)


def _w_in_to_layout(wt):
    z, xbc, dt, q, k, v, u, gates = (wt[_IN_SPLITS[i]:_IN_SPLITS[i + 1]] for i in range(8))
    return jnp.concatenate([xbc, q, k, v, gates, z, u, jnp.repeat(dt, SSD_HEAD_DIM, axis=0)], axis=0)


def _w_in_from_layout(g):
    xbc, q, k, v = (g[o:o + 1536] for o in (P_XBC, P_Q, P_K, P_V))
    gates, z, u, dt = g[P_GATES:P_GATES + 3072], g[P_Z:P_Z + 1024], g[P_U:P_U + 1024], g[P_DT:P_DT + 1024]
    dt = dt.astype(F32).reshape(SSD_HEADS, SSD_HEAD_DIM, D_MODEL).sum(1).astype(g.dtype)
    return jnp.concatenate([z, xbc, dt, q, k, v, u, gates], axis=0)


_STACKED = ("w_ssd_out", "w_attn_out", "w_pool_mix", "w_pool_out", "w_out", "w_ff1", "w_ff2")
_ROWS = ("norm1_w", "norm2_w", "conv_b", "ssd_norm_w", "q_norm_w", "k_norm_w", "pool_scale")
_HEAD_ROWS = ("dt_bias", "a_log", "d_skip")


def _layer_weights(wg, lg, small, l):
    W = {k: wg[k] for k in _STACKED if k in wg}
    W["w_in"] = _w_in_to_layout(wg["w_in"][lg].reshape(IN_WIDTH, D_MODEL))
    W["conv_w"] = small["conv_w"][l]
    for k in _ROWS:
        W[k] = small[k][l][None, :]
    for k in _HEAD_ROWS:
        W[k] = _expand_heads(small[k][l])[None, :]
    return W


def _layer_grads_by_chip(g, w_in_buf, l):
    out = dict(g)
    out["w_in"] = w_in_buf.at[:, l].set(_w_in_from_layout(g["w_in"]).reshape(N_CHIPS, IN_WIDTH // N_CHIPS, D_MODEL))
    for k in _ROWS:
        out[k] = g[k][0]
    for k in _HEAD_ROWS:
        out[k] = _reduce_heads(g[k][0])
    return out


ANY = pl.BlockSpec(memory_space=pl.ANY)


def _place():
    x, y, c = lax.axis_index("x"), lax.axis_index("y"), lax.axis_index("c")
    return x, y, c, (x, y, 1 - c), [(1 - x, y), (x, 1 - y), (1 - x, 1 - y)]


def _allgather8(name, blk):
    m_per, n = blk.shape

    def body(x_ref, out_ref, send_sems, recv_sems, local_sem):
        x, y, c, sibling, chips = _place()
        me = (x, y, c)

        def rows(px, py, pc):
            return out_ref.at[pl.ds((4 * px + 2 * py + pc) * m_per, m_per), :]

        def copy(k, block, to, src=None):
            return pltpu.make_async_remote_copy(
                src_ref=rows(*block) if src is None else src, dst_ref=rows(*block),
                send_sem=send_sems.at[k], recv_sem=recv_sems.at[k], device_id=to, device_id_type=MESH)

        mine = pltpu.make_async_copy(x_ref, rows(*me), local_sem)
        mine.start()
        first = [copy(0, me, sibling, src=x_ref)]
        first += [copy(1 + j, me, (*chip, c), src=x_ref) for j, chip in enumerate(chips)]
        for cp in first:
            cp.start()
        passed = [copy(4 + j, (*chip, c), sibling) for j, chip in enumerate(chips)]
        for j, chip in enumerate(chips):
            copy(1 + j, (*chip, c), me).wait_recv()
            passed[j].start()
        copy(0, sibling, me).wait_recv()
        for j, chip in enumerate(chips):
            copy(4 + j, (*chip, 1 - c), me).wait_recv()
        for cp in first + passed:
            cp.wait_send()
        mine.wait()

    return pl.pallas_call(
        body, out_shape=jax.ShapeDtypeStruct((N_DEV * m_per, n), blk.dtype),
        in_specs=[pl.BlockSpec(memory_space=pltpu.VMEM)], out_specs=pl.BlockSpec(memory_space=pltpu.VMEM),
        scratch_shapes=[pltpu.SemaphoreType.DMA((7,)), pltpu.SemaphoreType.DMA((7,)), pltpu.SemaphoreType.DMA],
        name=name)(blk)


HBM_SPEC = pl.BlockSpec(memory_space=pltpu.HBM)
SEM_SPEC = pl.BlockSpec(memory_space=pltpu.SEMAPHORE)
SIDE_EFFECT = pltpu.SideEffectType.DATAFLOW_SIDE_EFFECTING


def _dma_sems(n):
    return [pltpu.SemaphoreType.DMA((n,)), pltpu.SemaphoreType.DMA((n,))]


def _half_axis(shape):
    return 1 if (len(shape) > 3 or (shape[1] // 2) % 16 == 0) else len(shape) - 1


def _halves(ref, axis, c):
    r2 = ref.shape[axis] // 2
    lead = (slice(None),) * axis
    return ref.at[lead + (pl.ds(r2 * c, r2),)], ref.at[lead + (pl.ds(r2 * (1 - c), r2),)]


def _gather_copies(srcs, lands, send_sems, recv_sems):
    x, y, c, _, chips = _place()
    sends, lands_here = [], []
    for j, (cx, cy) in enumerate(chips):
        for i, (s, t) in enumerate(zip(srcs, lands)):
            k = 3 * i + j
            ax = _half_axis(s.shape)
            mine = _halves(t.at[:, 2 * x + y], ax, c)[0]
            theirs = _halves(t.at[:, 2 * cx + cy], ax, c)[0]
            sends.append(pltpu.make_async_remote_copy(src_ref=_halves(s, ax, c)[0], dst_ref=mine, send_sem=send_sems.at[k],
                                                      recv_sem=recv_sems.at[k], device_id=(cx, cy, c), device_id_type=MESH))
            lands_here.append(pltpu.make_async_remote_copy(src_ref=theirs, dst_ref=theirs, send_sem=send_sems.at[k],
                                                           recv_sem=recv_sems.at[k], device_id=(cx, cy, c), device_id_type=MESH))
    return sends, lands_here


def _exchange_copies(srcs, lands, send_sems, recv_sems):
    x, y, c, _, chips = _place()
    sends = [pltpu.make_async_remote_copy(src_ref=s.at[2 * cx + cy], dst_ref=t.at[j], send_sem=send_sems.at[3 * i + j],
                                          recv_sem=recv_sems.at[3 * i + j], device_id=(cx, cy, c), device_id_type=MESH)
             for j, (cx, cy) in enumerate(chips) for i, (s, t) in enumerate(zip(srcs, lands))]
    return sends, sends


def _split_start(name, copies, srcs, lands, after):
    ns, nl = len(srcs), len(lands)
    n_copies = 3 * ns

    def body(*refs):
        send_sems, recv_sems = refs[ns + nl + 1], refs[ns + nl + 2]
        for cp in copies(refs[:ns], refs[ns:ns + nl], send_sems, recv_sems)[0]:
            cp.start()
        refs[-1][...] = jnp.zeros_like(refs[-1])

    arrs = list(srcs) + list(lands)
    res = pl.pallas_call(
        body, name=name,
        out_shape=(pltpu.SemaphoreType.DMA((n_copies,)), pltpu.SemaphoreType.DMA((n_copies,)))
        + tuple(pltpu.HBM(a.shape, a.dtype) for a in arrs) + (jax.ShapeDtypeStruct((8, LANES), F32),),
        in_specs=[HBM_SPEC] * (ns + nl) + [ANY],
        out_specs=(SEM_SPEC, SEM_SPEC) + (HBM_SPEC,) * (ns + nl) + (pl.BlockSpec(memory_space=pltpu.VMEM),),
        input_output_aliases={i: 2 + i for i in range(ns + nl)},
        compiler_params=pltpu.CompilerParams(has_side_effects=SIDE_EFFECT),
    )(*[pltpu.with_memory_space_constraint(a, pltpu.HBM) for a in arrs], after)
    return res[0], res[1], list(res[2:2 + ns]), list(res[2 + ns:2 + ns + nl]), res[-1]


def _split_wait(name, copies, send_sems, recv_sems, srcs, lands, after):
    ns, nl = len(srcs), len(lands)

    def body(*refs):
        sends, lands_here = copies(refs[:ns], refs[ns:ns + nl], refs[ns + nl], refs[ns + nl + 1])
        for cp in sends:
            cp.wait_send()
        for cp in lands_here:
            cp.wait_recv()

    arrs = list(srcs) + list(lands)
    res = pl.pallas_call(
        body, name=name, out_shape=tuple(pltpu.HBM(a.shape, a.dtype) for a in arrs),
        in_specs=[HBM_SPEC] * (ns + nl) + [SEM_SPEC, SEM_SPEC, ANY], out_specs=(HBM_SPEC,) * (ns + nl),
        input_output_aliases={i: i for i in range(ns + nl)},
        compiler_params=pltpu.CompilerParams(has_side_effects=SIDE_EFFECT),
    )(*arrs, send_sems, recv_sems, after)
    return list(res[:ns]), list(res[ns:])


def _gather_forward(name, lands):
    n = len(lands)

    def body(*refs):
        ins, outs = refs[:n], refs[n:2 * n]
        send_sems, recv_sems = refs[2 * n:]
        x, y, c, sibling, chips = _place()
        sends, arrivals = [], []
        for j, (cx, cy) in enumerate(chips):
            for i in range(n):
                k = 3 * i + j
                ax = _half_axis(ins[i].shape[:1] + ins[i].shape[2:])
                src = _halves(ins[i].at[:, 2 * cx + cy], ax, c)[0]
                dst, theirs = _halves(outs[i].at[:, 2 * cx + cy], ax, c)
                sends.append(pltpu.make_async_remote_copy(src_ref=src, dst_ref=dst, send_sem=send_sems.at[k], recv_sem=recv_sems.at[k],
                                                          device_id=sibling, device_id_type=MESH))
                arrivals.append(pltpu.make_async_remote_copy(src_ref=theirs, dst_ref=theirs, send_sem=send_sems.at[k],
                                                             recv_sem=recv_sems.at[k], device_id=sibling, device_id_type=MESH))
        for cp in sends:
            cp.start()
        for cp in arrivals:
            cp.wait_recv()
        for cp in sends:
            cp.wait_send()

    return pl.pallas_call(
        body, out_shape=[jax.ShapeDtypeStruct(t.shape, t.dtype) for t in lands], in_specs=[ANY] * n, out_specs=[ANY] * n,
        input_output_aliases={i: i for i in range(n)}, scratch_shapes=_dma_sems(3 * n), name=name)(*lands)


def _swap_halves(name, gs):
    n = len(gs)

    def body(*refs):
        ins, got = refs[:n], refs[n:2 * n]
        send_sems, recv_sems = refs[2 * n:]
        x, y, c, sibling, _ = _place()
        sends = [pltpu.make_async_remote_copy(src_ref=_halves(ins[i], 1 + _half_axis(ins[i].shape[1:]), c)[1], dst_ref=got[i],
                                              send_sem=send_sems.at[i], recv_sem=recv_sems.at[i], device_id=sibling, device_id_type=MESH)
                 for i in range(n)]
        for cp in sends:
            cp.start()
        for cp in sends:
            cp.wait_recv()
        for cp in sends:
            cp.wait_send()

    def half_shape(t):
        ax = 1 + _half_axis(t.shape[1:])
        return t.shape[:ax] + (t.shape[ax] // 2,) + t.shape[ax + 1:]

    return pl.pallas_call(
        body, out_shape=[jax.ShapeDtypeStruct(half_shape(t), t.dtype) for t in gs],
        in_specs=[ANY] * n, out_specs=[ANY] * n, scratch_shapes=_dma_sems(n), name=name)(*gs)


def _share_halves(name, ts, axes):
    n = len(ts)

    def body(*refs):
        ins, outs = refs[:n], refs[n:2 * n]
        send_sems, recv_sems = refs[2 * n:]
        x, y, c, sibling, _ = _place()
        sends, arrivals = [], []
        for i in range(n):
            mine, theirs = _halves(outs[i], axes[i], c)
            sends.append(pltpu.make_async_remote_copy(src_ref=ins[i], dst_ref=mine, send_sem=send_sems.at[i], recv_sem=recv_sems.at[i],
                                                      device_id=sibling, device_id_type=MESH))
            arrivals.append(pltpu.make_async_remote_copy(src_ref=ins[i], dst_ref=theirs, send_sem=send_sems.at[i],
                                                         recv_sem=recv_sems.at[i], device_id=sibling, device_id_type=MESH))
        for cp in sends:
            cp.start()
        for cp in arrivals:
            cp.wait_recv()
        for cp in sends:
            cp.wait_send()

    return pl.pallas_call(
        body, out_shape=[jax.ShapeDtypeStruct(t.shape[:ax] + (2 * t.shape[ax],) + t.shape[ax + 1:], t.dtype) for t, ax in zip(ts, axes)],
        in_specs=[ANY] * n, out_specs=[ANY] * n, scratch_shapes=_dma_sems(n), name=name)(*ts)


PACK_W = 1024
PACK_TB = 512


def _sum_rows(name, parts, out_dtype):
    def f(*vals):
        acc = vals[0]
        for v in vals[1:]:
            acc = acc + v
        return (acc,)

    return _rowwise_fwd(name, f, [(p, PACK_W, 0) for p in parts], [], [(PACK_W, out_dtype)], tb=_tile(parts[0].shape[0], PACK_TB))[0]


def _sum_slots(name, ops, count, out_dtype):
    mat = ops[0][0].shape[1:]

    def body(*refs):
        acc = refs[0][...].astype(F32)
        for r in refs[1:-1]:
            acc = acc + r[...].astype(F32)
        refs[-1][...] = acc.astype(refs[-1].dtype)

    return pl.pallas_call(
        body, grid=(count,), in_specs=[pl.BlockSpec((None,) + mat, lambda i, s=s: (s * count + i, 0, 0)) for _, s in ops],
        out_specs=pl.BlockSpec((None,) + mat, lambda i: (i, 0, 0)), out_shape=jax.ShapeDtypeStruct((count,) + mat, out_dtype),
        compiler_params=_cp("parallel"), name=name)(*[a for a, _ in ops])


def _adamw_update(w_ref, g_ref, m_ref, v_ref, go_ref, d_ref, mo_ref, vo_ref):
    c1 = 1.0 / (1.0 - ADAM_B1 ** ADAM_STEP)
    c2 = 1.0 / (1.0 - ADAM_B2 ** ADAM_STEP)
    gg = g_ref[...]
    mn = ADAM_B1 * m_ref[...] + (1.0 - ADAM_B1) * gg
    vn = ADAM_B2 * v_ref[...] + (1.0 - ADAM_B2) * jnp.square(gg)
    go_ref[...] = gg
    d_ref[...] = -ADAM_LR * ((mn * c1) / (jnp.sqrt(vn * c2) + ADAM_EPS) + ADAM_WD * w_ref[...])
    mo_ref[...] = mn
    vo_ref[...] = vn


def _adamw_layers(name, w, g, m, v, lo, prev=None, dep=None):
    _, r, cw = w.shape

    def body(w_ref, g_ref, m_ref, v_ref, *rest):
        _adamw_update(w_ref, g_ref, m_ref, v_ref, *rest[-4:])

    full = pl.BlockSpec((None, r, LANES), lambda l, i: (lo + l, 0, i))
    extra = ([] if prev is None else list(prev)) + ([] if dep is None else [dep])
    return pl.pallas_call(
        body, grid=(g.shape[0], cw // LANES),
        in_specs=[full, pl.BlockSpec((None, r, LANES), lambda l, i: (l, 0, i)), full, full] + [pl.BlockSpec(memory_space=pl.ANY)] * len(extra),
        out_specs=[full] * 4, out_shape=[jax.ShapeDtypeStruct(w.shape, F32)] * 4,
        input_output_aliases={4 + i: i for i in range(0 if prev is None else 4)},
        compiler_params=_cp("parallel", "parallel"), name=name)(w, g, m, v, *extra)


def _adamw(name, w, g, m, v, row0=0, unit=None, prev=None, dep=None):
    r, cw = w.shape
    tb = unit or r
    while tb * cw > 300_000 and tb % 16 == 0:
        tb //= 2
    off = row0 // tb

    def body(w_ref, g_ref, m_ref, v_ref, *rest):
        _adamw_update(w_ref, g_ref, m_ref, v_ref, *rest[-4:])

    full = pl.BlockSpec((tb, cw), lambda i: (off + i, 0))
    extra = ([] if prev is None else list(prev)) + ([] if dep is None else [dep])
    n_prev = 0 if prev is None else 4
    return pl.pallas_call(
        body, grid=(g.shape[0] // tb,),
        in_specs=[full, pl.BlockSpec((tb, cw), lambda i: (i, 0)), full, full] + [pl.BlockSpec(memory_space=pl.ANY)] * len(extra),
        out_specs=[full] * 4, out_shape=[jax.ShapeDtypeStruct((r, cw), F32)] * 4,
        input_output_aliases={4 + i: i for i in range(n_prev)},
        compiler_params=_cp("parallel"), name=name)(w, g, m, v, *extra)


def _silu_rows(c):
    def body(c_ref, o_ref):
        rows = lax.broadcasted_iota(jnp.int32, o_ref.shape, 0)
        o_ref[...] = jnp.where(rows == 0, jnp.broadcast_to(_silu(c_ref[...]), o_ref.shape), 0.0)

    return pl.pallas_call(body, out_shape=jax.ShapeDtypeStruct((8, c.shape[1]), F32), name="cond_silu")(c)


_KINDS = ("w_in", "w_ssd_out", "w_attn_out", "w_pool_mix", "w_pool_out", "w_out", "w_ff1", "w_ff2")
_SMALL = ("b_ada", "norm1_w", "norm2_w", "conv_b", "dt_bias", "a_log", "d_skip", "ssd_norm_w", "q_norm_w", "k_norm_w",
          "pool_scale")
_ORDER = ("w_ada", "b_ada", "norm1_w", "norm2_w", "w_in", "conv_w", "conv_b", "dt_bias", "a_log", "d_skip", "ssd_norm_w",
          "w_ssd_out", "q_norm_w", "k_norm_w", "w_attn_out", "w_pool_mix", "pool_scale", "w_pool_out", "w_out", "w_ff1", "w_ff2")


def _pack_flat(arrs, rows, dtype):
    flat = jnp.concatenate([a.reshape(-1).astype(dtype) for a in arrs])
    return jnp.pad(flat, (0, rows * PACK_W - flat.shape[0])).reshape(rows, PACK_W)


def _unpack_flat(buf, shapes):
    flat = buf.reshape(-1)
    out, off = [], 0
    for shp in shapes:
        n = int(np.prod(shp))
        out.append(flat[off:off + n].reshape(shp))
        off += n
    return out


def _small_rows(n_elems):
    return -(-n_elems // (8 * PACK_W)) * 8


def kernel(x, c, w_ada, b_ada, norm1_w, norm2_w, w_in, conv_w, conv_b, dt_bias, a_log, d_skip, ssd_norm_w, w_ssd_out, q_norm_w, k_norm_w, w_attn_out, w_pool_mix, pool_scale, w_pool_out, w_out, w_ff1, w_ff2, loss_target, m_w_ada, m_b_ada, m_norm1_w, m_norm2_w, m_w_in, m_conv_w, m_conv_b, m_dt_bias, m_a_log, m_d_skip, m_ssd_norm_w, m_w_ssd_out, m_q_norm_w, m_k_norm_w, m_w_attn_out, m_w_pool_mix, m_pool_scale, m_w_pool_out, m_w_out, m_w_ff1, m_w_ff2, v_w_ada, v_b_ada, v_norm1_w, v_norm2_w, v_w_in, v_conv_w, v_conv_b, v_dt_bias, v_a_log, v_d_skip, v_ssd_norm_w, v_w_ssd_out, v_q_norm_w, v_k_norm_w, v_w_attn_out, v_w_pool_mix, v_pool_scale, v_w_pool_out, v_w_out, v_w_ff1, v_w_ff2):
    w = dict(w_ada=w_ada, b_ada=b_ada, norm1_w=norm1_w, norm2_w=norm2_w, w_in=w_in, conv_w=conv_w, conv_b=conv_b, dt_bias=dt_bias, a_log=a_log, d_skip=d_skip, ssd_norm_w=ssd_norm_w, w_ssd_out=w_ssd_out, q_norm_w=q_norm_w, k_norm_w=k_norm_w, w_attn_out=w_attn_out, w_pool_mix=w_pool_mix, pool_scale=pool_scale, w_pool_out=w_pool_out, w_out=w_out, w_ff1=w_ff1, w_ff2=w_ff2)
    m = dict(w_ada=m_w_ada, b_ada=m_b_ada, norm1_w=m_norm1_w, norm2_w=m_norm2_w, w_in=m_w_in, conv_w=m_conv_w, conv_b=m_conv_b, dt_bias=m_dt_bias, a_log=m_a_log, d_skip=m_d_skip, ssd_norm_w=m_ssd_norm_w, w_ssd_out=m_w_ssd_out, q_norm_w=m_q_norm_w, k_norm_w=m_k_norm_w, w_attn_out=m_w_attn_out, w_pool_mix=m_w_pool_mix, pool_scale=m_pool_scale, w_pool_out=m_w_pool_out, w_out=m_w_out, w_ff1=m_w_ff1, w_ff2=m_w_ff2)
    v = dict(w_ada=v_w_ada, b_ada=v_b_ada, norm1_w=v_norm1_w, norm2_w=v_norm2_w, w_in=v_w_in, conv_w=v_conv_w, conv_b=v_conv_b, dt_bias=v_dt_bias, a_log=v_a_log, d_skip=v_d_skip, ssd_norm_w=v_ssd_norm_w, w_ssd_out=v_w_ssd_out, q_norm_w=v_q_norm_w, k_norm_w=v_k_norm_w, w_attn_out=v_w_attn_out, w_pool_mix=v_w_pool_mix, pool_scale=v_pool_scale, w_pool_out=v_w_pool_out, w_out=v_w_out, w_ff1=v_w_ff1, w_ff2=v_w_ff2)
    chip = 2 * lax.axis_index("x") + lax.axis_index("y")
    dev = 2 * chip + lax.axis_index("c")
    ada_cols = w_ada.shape[2]

    wk = dict({k: w[k] for k in _KINDS}, w_in=jnp.transpose(w_in, (0, 2, 1)))
    mk, vk = {"w_in": jnp.transpose(m_w_in, (0, 2, 1))}, {"w_in": jnp.transpose(v_w_in, (0, 2, 1))}
    rest = _KINDS[1:]

    def start_gather(tag, lo, n, kinds, after):
        shards = [wk[k][lo:lo + n].astype(BF16) for k in kinds]
        lands = [lax.empty((n, N_CHIPS) + s.shape[1:], BF16) for s in shards]
        return _split_start("gather_start_" + tag, _gather_copies, shards, lands, after)

    gather_a1 = start_gather("a1", 0, 1, _KINDS[:1], c)
    c = c + gather_a1[4][0, 0]

    n_conv = conv_w.size // PACK_W
    rows1 = _small_rows((1 + n_conv) * PACK_W)
    blk = jnp.concatenate([_silu_rows(c)[:1], conv_w.reshape(n_conv, PACK_W), jnp.zeros((rows1 - 1 - n_conv, PACK_W), F32)])
    first = _allgather8("gather_cond", blk).reshape(N_DEV, rows1, PACK_W)
    cond_all = first[:, 0]
    conv_all = first[0::2, 1:1 + n_conv].reshape((N_CHIPS,) + conv_w.shape)
    conv_full = jnp.moveaxis(conv_all, 0, 2).reshape(DEPTH, SSD_CONV, CONV_DIM)
    b_cols = lax.dynamic_slice_in_dim(b_ada, chip * ada_cols, ada_cols, axis=1)
    mod_cols = jnp.stack([_matmul("ada_fwd", cond_all, w_ada[l], "nn", precise=True) + b_cols[l][None, :] for l in range(DEPTH)])
    mod_all = _allgather8("gather_mod", mod_cols.reshape(-1, PACK_W)).reshape(N_DEV, DEPTH, N_DEV, ada_cols)
    mine = lax.dynamic_index_in_dim(mod_all[0::2], dev, axis=2, keepdims=False)
    mods = jnp.moveaxis(mine, 0, 1).reshape(DEPTH, 6, D_MODEL)

    core = lax.axis_index("c")
    small_w = dict({k: w[k] for k in _SMALL[1:]}, conv_w=conv_full)

    def finish_gather(tag, handle, kinds, after):
        shards, lands = _split_wait("gather_wait_" + tag, _gather_copies, handle[0], handle[1], handle[2], handle[3], after)
        lands = _gather_forward("gather_forward_" + tag, lands)
        wg = {k: lax.dynamic_update_slice_in_dim(t, s[:, None], chip, axis=1) for k, t, s in zip(kinds, lands, shards)}
        n = shards[0].shape[0]
        for k in ("w_ssd_out", "w_pool_out", "w_out", "w_ff2"):
            if k in wg:
                wg[k] = wg[k].reshape(n, -1, D_MODEL)
        if "w_pool_mix" in wg:
            wg["w_pool_mix"] = jnp.moveaxis(wg["w_pool_mix"], 1, 2).reshape(n, 4, POOL_GW, POOL_GW)
        return wg

    wg_a1 = finish_gather("a1", gather_a1, _KINDS[:1], mods)
    gather_a2 = start_gather("a2", 0, 1, rest, wg_a1["w_in"])
    gather_b = start_gather("b", 1, DEPTH - 1, _KINDS, gather_a2[4])
    mods = mods + gather_b[4][0, 0]

    xc = x[0]
    Ws, saved = [None] * DEPTH, [None] * DEPTH
    Ws[0] = _layer_weights(wg_a1, 0, small_w, 0)
    xc, saved[0] = _layer_fwd(xc, mods[0], Ws[0], 0, late=lambda y: finish_gather("a2", gather_a2, rest, y))
    wg_b = finish_gather("b", gather_b, _KINDS, xc)
    for l in range(1, DEPTH):
        Ws[l] = _layer_weights(wg_b, l - 1, small_w, l)
        xc, saved[l] = _layer_fwd(xc, mods[l], Ws[l], l - 1)
    dx, loss = _loss_and_grad(xc, loss_target[0])
    dmods, grads = [None] * DEPTH, [None] * DEPTH

    def backward(l, lg, dx, mod, bufs):
        dx, dmods[l], g = _layer_bwd(dx, mod, Ws[l], saved[l], lg, bufs)
        grads[l] = _layer_grads_by_chip(g, bufs["w_in"], lg)
        return dx, {k: grads[l][k] for k in _KINDS}

    bufs_b = {k: lax.empty((N_CHIPS, DEPTH - 1) + wk[k].shape[1:], BF16) for k in _KINDS}
    for l in reversed(range(1, DEPTH)):
        dx, bufs_b = backward(l, l - 1, dx, mods[l], bufs_b)

    def flat(t):
        return t.reshape((-1,) + t.shape[-2:])

    def start_exchange(tag, bufs, after):
        gs = [bufs[k] for k in _KINDS]
        got = _swap_halves("grad_swap_" + tag, gs)
        axes = [1 + _half_axis(t.shape[1:]) for t in gs]
        own = [lax.dynamic_slice_in_dim(t, (t.shape[ax] // 2) * core, t.shape[ax] // 2, axis=ax) for t, ax in zip(gs, axes)]
        pairs = [_sum_slots(f"sum_pair_{tag}_{k}", [(flat(a), 0), (flat(b), 0)], flat(a).shape[0], BF16).reshape(a.shape)
                 for k, a, b in zip(_KINDS, own, got)]
        lands = [lax.empty((3,) + p.shape[1:], BF16) for p in pairs]
        return _split_start("exchange_start_" + tag, _exchange_copies, pairs, lands, after)

    def finish_exchange(tag, handle, after):
        pairs, partials = _split_wait("exchange_wait_" + tag, _exchange_copies, handle[0], handle[1], handle[2], handle[3], after)
        mine = [lax.dynamic_index_in_dim(p, chip, axis=0, keepdims=False) for p in pairs]
        totals = [_sum_slots(f"sum_chips_{tag}_{k}", [(flat(a), 0)] + [(flat(p), s) for s in range(3)], flat(a).shape[0], F32).reshape(a.shape)
                  for k, a, p in zip(_KINDS, mine, partials)]
        axes = [_half_axis((1,) + wk[k].shape[1:]) for k in _KINDS]
        return [lax.dynamic_update_slice_in_dim(t, mine_t, mine_t.shape[ax] * core, axis=ax)
                for t, mine_t, ax in zip(_share_halves("grad_share_" + tag, totals, axes), totals, axes)]

    def adamw_group(tag, lo, gs, prev, dep):
        out = {}
        out["w_in"] = _adamw_layers(f"adamw_{tag}_w_in", wk["w_in"], gs[0], mk["w_in"], vk["w_in"], lo,
                                    prev=None if prev is None else prev["w_in"], dep=dep)
        dep = out["w_in"][1]
        for k, gk in zip(rest, gs[1:]):
            shp = w[k].shape
            unit = int(np.prod(shp[1:-1]))
            two_d = lambda t: t.reshape(-1, shp[-1])
            out[k] = _adamw(f"adamw_{tag}_{k}", two_d(w[k]), two_d(gk), two_d(m[k]), two_d(v[k]), row0=lo * unit, unit=unit,
                            prev=None if prev is None else prev[k], dep=dep)
            dep = out[k][1]
        return out

    exchange_b = start_exchange("b", bufs_b, dx)
    bufs_a = {k: lax.empty((N_CHIPS, 1) + wk[k].shape[1:], BF16) for k in _KINDS}
    dx, bufs_a = backward(0, 0, dx, mods[0] + exchange_b[4][0, 0], bufs_a)
    g_b = finish_exchange("b", exchange_b, dx)
    grad_x, dmods = dx, jnp.stack(dmods)
    exchange_a = start_exchange("a", bufs_a, g_b[0])
    adam_b = adamw_group("b", 1, g_b, None, exchange_a[4])

    small = ([dmods] + [jnp.stack([grads[l][k] for l in range(DEPTH)]) for k in _SMALL[1:] + ("conv_w",)] + [loss[:, :1]])
    n_small = sum(int(np.prod(a.shape)) for a in small)
    rows_small = _small_rows(n_small)
    small_all = _allgather8("gather_small", _pack_flat(small, rows_small, F32))
    parts = [small_all[d * rows_small:(d + 1) * rows_small] for d in range(N_DEV)]
    small_sum = _unpack_flat(_sum_rows("sum_small", parts, F32), [a.shape for a in small])
    g_out = {"b_ada": small_sum[0].reshape(DEPTH, 6 * D_MODEL)}
    for k, t in zip(_SMALL[1:], small_sum[1:-2]):
        g_out[k] = t
    g_out["conv_w"] = lax.dynamic_slice_in_dim(small_sum[-2], chip * conv_w.shape[2], conv_w.shape[2], axis=2)
    loss_out = small_sum[-1][0, 0]
    dmod_all = jnp.stack([p[:DEPTH * 6].reshape(DEPTH, 6 * D_MODEL) for p in parts])
    dmod_cols = lax.dynamic_slice_in_dim(dmod_all, chip * ada_cols, ada_cols, axis=2)
    g_out["w_ada"] = jnp.stack([_matmul("ada_dw", cond_all, dmod_cols[:, l], "tn", precise=True) for l in range(DEPTH)])

    deltas, new_m, new_v = {}, {}, {}
    dep = adam_b[_KINDS[-1]][1]
    for k in ("w_ada", "conv_w"):
        shp = w[k].shape
        two_d = (int(np.prod(shp[:-1])), shp[-1])
        res = _adamw("adamw_" + k, *(t.reshape(two_d) for t in (w[k], g_out[k], m[k], v[k])), dep=dep)
        deltas[k], new_m[k], new_v[k] = (t.reshape(shp) for t in res[1:])
        dep = res[1]
    small_shapes = [w[k].shape for k in _SMALL]
    n_sm = sum(int(np.prod(s)) for s in small_shapes)
    res = _adamw("adamw_small", *[_pack_flat([t[k] for k in _SMALL], _small_rows(n_sm), F32) for t in (w, g_out, m, v)], dep=dep)[1:]
    for name_map, buf in zip((deltas, new_m, new_v), res):
        for k, t in zip(_SMALL, _unpack_flat(buf, small_shapes)):
            name_map[k] = t
    g_a = finish_exchange("a", exchange_a, res[0])
    for k, t in adamw_group("a", 0, g_a, adam_b, None).items():
        if k == "w_in":
            g_out[k], deltas[k], new_m[k], new_v[k] = (jnp.transpose(u, (0, 2, 1)) for u in t)
        else:
            g_out[k], deltas[k], new_m[k], new_v[k] = (u.reshape(w[k].shape) for u in t)

    return (loss_out, grad_x[None], *[g_out[k] for k in _ORDER], *[deltas[k] for k in _ORDER],
            *[new_m[k] for k in _ORDER], *[new_v[k] for k in _ORDER])
```

```python
import functools
import math

import numpy as np
import jax
import jax.numpy as jnp
from jax import lax
from jax.experimental import pallas as pl
from jax.experimental.pallas import tpu as pltpu

F32, BF16 = jnp.float32, jnp.bfloat16
MESH = pl.DeviceIdType.MESH

D_MODEL = 1024
DEPTH = 4
N_CHIPS = 4
N_DEV = 8
SSD_HEADS = 16
SSD_HEAD_DIM = 64
SSD_STATE = 128
SSD_CHUNK = 128
SSD_CONV = 4
CONV_DIM = 1536
ATTN_HEAD_DIM = 128
ATTN_GROUP_W = 512
DILATIONS = (1, 4, 16)
ATTN_STEPS = 128
POOL_WINDOWS = (2, 4, 8, 16)
POOL_GW = 256
D_FF = 4096
EPS = 1e-6
IN_SIZES = (1024, 1536, 16, 1536, 1536, 1536, 1024, 3072)
IN_WIDTH = sum(IN_SIZES)
P_XBC, P_Q, P_K, P_V, P_GATES, P_Z, P_U, P_DT = 0, 1536, 3072, 4608, 6144, 9216, 10240, 11264
P_WIDTH = 12288
LANES = 128
NEG = -1e30
VMEM_LIMIT = 56 * 1024 * 1024

ADAM_LR, ADAM_B1, ADAM_B2, ADAM_EPS, ADAM_WD, ADAM_STEP = 0.001, 0.9, 0.999, 1e-08, 0.01, 10


def _alibi_slopes(n):
    def pow2(k):
        start = 2.0 ** (-8.0 / k)
        return [start ** (i + 1) for i in range(k)]
    if math.log2(n).is_integer():
        s = pow2(n)
    else:
        c = 2 ** math.floor(math.log2(n))
        s = pow2(c) + pow2(2 * c)[0::2][: n - c]
    return np.sort(np.asarray(s, np.float32))[::-1].copy()


SLOPES = _alibi_slopes(12).reshape(3, 4)


def _cp(*sem):
    return pltpu.CompilerParams(dimension_semantics=sem, vmem_limit_bytes=VMEM_LIMIT)


_DIMS = {"nn": (((1,), (0,)), ((), ())), "nt": (((1,), (1,)), ((), ())), "tn": (((0,), (0,)), ((), ()))}


def _dot(a, b, mode):
    return lax.dot_general(a.astype(BF16), b.astype(BF16), _DIMS[mode], preferred_element_type=F32)


@functools.partial(jax.custom_vjp, nondiff_argnums=(2,))
def _bdot(a, b, mode):
    return _dot(a, b, mode)


def _bdot_fwd(a, b, mode):
    return _dot(a, b, mode), (a, b)


def _bdot_bwd(mode, res, ct):
    a, b = res
    if mode == "nn":
        return _dot(ct, b, "nt"), _dot(a, ct, "tn")
    if mode == "nt":
        return _dot(ct, b, "nn"), _dot(ct, a, "tn")
    return _dot(b, ct, "nt"), _dot(a, ct, "nn")


_bdot.defvjp(_bdot_fwd, _bdot_bwd)


def _hdot(a, b):
    return jnp.dot(a, b, precision=lax.Precision.HIGHEST, preferred_element_type=F32)


def _tri(n, lower):
    r = lax.broadcasted_iota(jnp.int32, (n, n), 0)
    c = lax.broadcasted_iota(jnp.int32, (n, n), 1)
    return (r >= c if lower else r <= c).astype(F32)


@jax.custom_vjp
def _csum(a):
    return _hdot(_tri(a.shape[0], True), a)


def _csum_fwd(a):
    return _csum(a), None


def _csum_bwd(_, ct):
    return (_hdot(_tri(ct.shape[0], False), ct),)


_csum.defvjp(_csum_fwd, _csum_bwd)


def _softplus(x):
    return jnp.maximum(x, 0.0) + jnp.log(1.0 + jnp.exp(-jnp.abs(x)))


def _sigmoid(x):
    return 1.0 / (1.0 + jnp.exp(-x))


def _silu(x):
    return x * _sigmoid(x)


def _tile(n, cap):
    t = min(n, cap)
    while n % t:
        t //= 2
    return t


MM_TILE, MM_KTILE = 1024, 2048


def _matmul(name, a, b, mode, out_dtype=F32, precise=False, layer=None, chips=0, out_chips=0, into=None):
    if mode == "nn":
        (m, k), n = a.shape, (4 * chips if chips else b.shape[-1])
    elif mode == "nt":
        (m, k), n = a.shape, b.shape[-2]
    else:
        (k, m), n = a.shape, b.shape[-1]
    tm = _tile(m // N_CHIPS if (into is not None and not out_chips) else m, MM_TILE)
    tn = _tile(chips if (chips and mode == "nn") else (out_chips or n), MM_TILE)
    tk = _tile(chips if (chips and mode == "nt") else k, MM_KTILE)
    nk = k // tk
    a_spec = pl.BlockSpec((tk, tm), lambda i, j, l: (l, i)) if mode == "tn" else pl.BlockSpec((tm, tk), lambda i, j, l: (i, l))
    if chips:
        if mode == "nn":
            per = chips // tn
            b_spec = pl.BlockSpec((None, None, tk, tn), lambda i, j, l: (layer, j // per, l, j % per))
        else:
            per = chips // tk
            b_spec = pl.BlockSpec((None, None, tn, tk), lambda i, j, l: (layer, l // per, j, l % per))
    elif layer is not None:
        b_spec = (pl.BlockSpec((None, tn, tk), lambda i, j, l: (layer, j, l)) if mode == "nt"
                  else pl.BlockSpec((None, tk, tn), lambda i, j, l: (layer, l, j)))
    else:
        b_spec = pl.BlockSpec((tn, tk), lambda i, j, l: (j, l)) if mode == "nt" else pl.BlockSpec((tk, tn), lambda i, j, l: (l, j))
    if into is not None:
        buf, slot = into
        if out_chips:
            per_o = out_chips // tn
            o_spec = pl.BlockSpec((None, None, tm, tn), lambda i, j, l: (j // per_o, slot, i, j % per_o))
        else:
            per_r = m // N_CHIPS // tm
            o_spec = pl.BlockSpec((None, None, tm, tn), lambda i, j, l: (i // per_r, slot, i % per_r, j))
        o_shape = jax.ShapeDtypeStruct(buf.shape, buf.dtype)
    elif out_chips:
        per_o = out_chips // tn
        o_spec = pl.BlockSpec((None, tm, tn), lambda i, j, l: (j // per_o, i, j % per_o))
        o_shape = jax.ShapeDtypeStruct((N_CHIPS, m, out_chips), out_dtype)
    else:
        o_spec = pl.BlockSpec((tm, tn), lambda i, j, l: (i, j))
        o_shape = jax.ShapeDtypeStruct((m, n), out_dtype)

    def part(a_ref, b_ref):
        if precise:
            return lax.dot_general(a_ref[...], b_ref[...], _DIMS[mode], precision=lax.Precision.HIGHEST,
                                   preferred_element_type=F32)
        return _dot(a_ref[...], b_ref[...], mode)

    n_in = 2 if into is None else 3

    if nk == 1:
        def body(*refs):
            o_ref = refs[n_in]
            o_ref[...] = part(refs[0], refs[1]).astype(o_ref.dtype)
        scratch = []
    else:
        def body(*refs):
            o_ref, acc_ref = refs[n_in], refs[n_in + 1]
            l = pl.program_id(2)
            p = part(refs[0], refs[1])

            @pl.when(l == 0)
            def _():
                acc_ref[...] = p

            @pl.when((l > 0) & (l < nk - 1))
            def _():
                acc_ref[...] += p

            @pl.when(l == nk - 1)
            def _():
                o_ref[...] = (acc_ref[...] + p).astype(o_ref.dtype)
        scratch = [pltpu.VMEM((tm, tn), F32)]

    extra = {} if into is None else dict(input_output_aliases={2: 0})
    return pl.pallas_call(
        body, grid=(m // tm, n // tn, nk), in_specs=[a_spec, b_spec] + ([] if into is None else [pl.BlockSpec(memory_space=pl.ANY)]),
        out_specs=o_spec, out_shape=o_shape, scratch_shapes=scratch, compiler_params=_cp("parallel", "parallel", "arbitrary"),
        name=name, **extra)(*((a, b) if into is None else (a, b, into[0])))


def _group_matmul(name, a, w, mode, out_dtype=F32, layer=0):
    s = a.shape[0]
    tb = 512
    gw = POOL_GW
    if mode == "tn":
        def body(a_ref, b_ref, o_ref):
            part = _dot(a_ref[...], b_ref[...], "tn")

            @pl.when(pl.program_id(1) == 0)
            def _():
                o_ref[0] = part

            @pl.when(pl.program_id(1) > 0)
            def _():
                o_ref[0] += part

        return pl.pallas_call(
            body, grid=(4, s // tb),
            in_specs=[pl.BlockSpec((tb, gw), lambda g, i: (i, g)), pl.BlockSpec((tb, gw), lambda g, i: (i, g))],
            out_specs=pl.BlockSpec((1, gw, gw), lambda g, i: (g, 0, 0)),
            out_shape=jax.ShapeDtypeStruct((4, gw, gw), F32),
            compiler_params=_cp("parallel", "arbitrary"), name=name)(a, w)

    def body(a_ref, w_ref, o_ref):
        o_ref[...] = _dot(a_ref[...], w_ref[...], mode).astype(o_ref.dtype)

    return pl.pallas_call(
        body, grid=(s // tb, 4),
        in_specs=[pl.BlockSpec((tb, gw), lambda i, g: (i, g)), pl.BlockSpec((None, None, gw, gw), lambda i, g: (layer, g, 0, 0))],
        out_specs=pl.BlockSpec((tb, gw), lambda i, g: (i, g)),
        out_shape=jax.ShapeDtypeStruct((s, 4 * gw), out_dtype),
        compiler_params=_cp("parallel", "parallel"), name=name)(a, w)


def _rspec(tb, width, cb):
    return pl.BlockSpec((tb, width), lambda i: (i, cb))


def _pspec(shape):
    return pl.BlockSpec(shape, lambda i: (0, 0))


def _rowwise_fwd(name, f, rows, pars, outs, tb=256):
    s = rows[0][0].shape[0]
    nin = len(rows) + len(pars)

    def body(*refs):
        res = f(*[r[...].astype(F32) for r in refs[:nin]])
        for o, v in zip(refs[nin:], res):
            o[...] = v.astype(o.dtype)

    return pl.pallas_call(
        body, grid=(s // tb,),
        in_specs=[_rspec(tb, w, cb) for _, w, cb in rows] + [_pspec(p.shape) for p in pars],
        out_specs=[_rspec(tb, w, 0) for w, _ in outs],
        out_shape=[jax.ShapeDtypeStruct((s, w), dt) for w, dt in outs],
        compiler_params=_cp("parallel"), name=name)(*[r[0] for r in rows], *pars)


def _rowwise_bwd(name, f, rows, pars, cts, need, add=None, tb=256, gdt=None):
    s = rows[0][0].shape[0]
    nr, npar, nc = len(rows), len(pars), len(cts)
    nin = nr + npar + nc + (1 if add is not None else 0)

    def body(*refs):
        ins = [r[...].astype(F32) for r in refs[:nr + npar]]
        _, vjp = jax.vjp(f, *ins)
        g = vjp(tuple(c[...].astype(F32) for c in refs[nr + npar:nr + npar + nc]))
        outs = refs[nin:]
        k = 0
        for j in range(nr):
            if need[j]:
                v = g[j]
                if add is not None and add[0] == j:
                    v = v + refs[nin - 1][...]
                outs[k][...] = v.astype(outs[k].dtype)
                k += 1
        first = pl.program_id(0) == 0
        for j in range(npar):
            o, v = outs[k + j], g[nr + j]

            @pl.when(first)
            def _(o=o, v=v):
                o[...] = v

            @pl.when(jnp.logical_not(first))
            def _(o=o, v=v):
                o[...] += v

    in_specs = ([_rspec(tb, w, cb) for _, w, cb in rows] + [_pspec(p.shape) for p in pars]
                + [_rspec(tb, w, cb) for _, w, cb in cts])
    args = [r[0] for r in rows] + list(pars) + [c[0] for c in cts]
    if add is not None:
        in_specs.append(_rspec(tb, rows[add[0]][1], 0))
        args.append(add[1])
    gr = [(w, F32) for (_, w, _), nd in zip(rows, need) if nd]
    if gdt is not None:
        gr = [(w, dt) for (w, _), dt in zip(gr, gdt)]
    return pl.pallas_call(
        body, grid=(s // tb,), in_specs=in_specs,
        out_specs=[_rspec(tb, w, 0) for w, _ in gr] + [_pspec(p.shape) for p in pars],
        out_shape=[jax.ShapeDtypeStruct((s, w), dt) for w, dt in gr] + [jax.ShapeDtypeStruct(p.shape, F32) for p in pars],
        compiler_params=_cp("arbitrary"), name=name)(*args)


def _f_norm(x, nw, sc, sh):
    r = lax.rsqrt(jnp.mean(x * x, axis=-1, keepdims=True) + EPS)
    return ((x * r * nw) * (1.0 + sc) + sh,)


def _f_ssdgate(y, z, w):
    y2 = y * _silu(z)
    low = lax.broadcasted_iota(jnp.int32, y2.shape, 1) < 512
    sq = y2 * y2
    m0 = jnp.sum(jnp.where(low, sq, 0.0), axis=-1, keepdims=True) / 512.0
    m1 = jnp.sum(jnp.where(low, 0.0, sq), axis=-1, keepdims=True) / 512.0
    r = jnp.where(low, lax.rsqrt(m0 + EPS), lax.rsqrt(m1 + EPS))
    return (y2 * r * w,)


def _head_rms(t, w):
    outs = []
    for h in range(t.shape[1] // ATTN_HEAD_DIM):
        th = t[:, h * ATTN_HEAD_DIM:(h + 1) * ATTN_HEAD_DIM]
        outs.append(th * lax.rsqrt(jnp.mean(th * th, axis=-1, keepdims=True) + EPS) * w)
    return jnp.concatenate(outs, axis=1)


def _f_qknorm(q, k, qw, kw):
    return _head_rms(q, qw), _head_rms(k, kw)


def _f_combine(o1, o2, o3, l1, l2, l3):
    m = lax.stop_gradient(jnp.maximum(jnp.maximum(l1, l2), l3))
    e1, e2, e3 = jnp.exp(l1 - m), jnp.exp(l2 - m), jnp.exp(l3 - m)
    return ((e1 * o1 + e2 * o2 + e3 * o3) / (e1 + e2 + e3),)


def _f_poolscale(pm, ps):
    return (pm * ps,)


def _f_merge(gates, ys, ya, yp):
    g = _sigmoid(gates)
    return (g[:, 0:1024] * ys + g[:, 1024:2048] * ya + g[:, 2048:3072] * yp,)


def _f_resid(x, o, g):
    return (x + g * o,)


def _f_relu2(a):
    return (jnp.square(jnp.maximum(a, 0.0)),)


def _loss_and_grad(y, tgt, tb=512):
    s, d = y.shape

    def body(y_ref, t_ref, dy_ref, l_ref):
        e = y_ref[...] - t_ref[...]
        dy_ref[...] = e * (1.0 / d)
        part = jnp.zeros((1, LANES), F32) + jnp.sum(e * e) * (0.5 / d)

        @pl.when(pl.program_id(0) == 0)
        def _():
            l_ref[...] = part

        @pl.when(pl.program_id(0) > 0)
        def _():
            l_ref[...] += part

    return pl.pallas_call(
        body, grid=(s // tb,), in_specs=[_rspec(tb, d, 0), _rspec(tb, d, 0)],
        out_specs=[_rspec(tb, d, 0), _pspec((1, LANES))],
        out_shape=[jax.ShapeDtypeStruct((s, d), F32), jax.ShapeDtypeStruct((1, LANES), F32)],
        compiler_params=_cp("arbitrary"), name="loss")(y, tgt)


def _shift_down(x, j):
    rows = lax.broadcasted_iota(jnp.int32, x.shape, 0)
    return jnp.where(rows < j, 0.0, pltpu.roll(x, j, 0))


def _shift_up(x, j):
    s = x.shape[0]
    rows = lax.broadcasted_iota(jnp.int32, x.shape, 0)
    return jnp.where(rows >= s - j, 0.0, pltpu.roll(x, s - j, 0))


CONV_CB = 256


def _conv_pre(x, w_ref, b_ref):
    acc = b_ref[...] + w_ref[SSD_CONV - 1:SSD_CONV, :] * x
    for j in range(1, SSD_CONV):
        acc = acc + w_ref[SSD_CONV - 1 - j:SSD_CONV - j, :] * _shift_down(x, j)
    return acc


def _conv_fwd(proj, cw, cb):
    s = proj.shape[0]

    def body(x_ref, w_ref, b_ref, o_ref):
        o_ref[...] = _silu(_conv_pre(x_ref[...], w_ref, b_ref))

    return pl.pallas_call(
        body, grid=(CONV_DIM // CONV_CB,),
        in_specs=[pl.BlockSpec((s, CONV_CB), lambda i: (0, P_XBC // CONV_CB + i)),
                  pl.BlockSpec((SSD_CONV, CONV_CB), lambda i: (0, i)), pl.BlockSpec((1, CONV_CB), lambda i: (0, i))],
        out_specs=pl.BlockSpec((s, CONV_CB), lambda i: (0, i)),
        out_shape=jax.ShapeDtypeStruct((s, CONV_DIM), F32), compiler_params=_cp("parallel"), name="conv_fwd")(proj, cw, cb)


def _conv_bwd(proj, cw, cb, dout):
    s = proj.shape[0]

    def body(x_ref, w_ref, b_ref, d_ref, dx_ref, dw_ref, db_ref):
        x = x_ref[...]
        a = _conv_pre(x, w_ref, b_ref)
        sg = _sigmoid(a)
        da = d_ref[...] * (sg + a * sg * (1.0 - sg))
        db_ref[...] = jnp.sum(da, axis=0, keepdims=True)
        dx = w_ref[SSD_CONV - 1:SSD_CONV, :] * da
        dw_ref[SSD_CONV - 1:SSD_CONV, :] = jnp.sum(da * x, axis=0, keepdims=True)
        for j in range(1, SSD_CONV):
            dx = dx + w_ref[SSD_CONV - 1 - j:SSD_CONV - j, :] * _shift_up(da, j)
            dw_ref[SSD_CONV - 1 - j:SSD_CONV - j, :] = jnp.sum(da * _shift_down(x, j), axis=0, keepdims=True)
        dx_ref[...] = dx.astype(dx_ref.dtype)

    return pl.pallas_call(
        body, grid=(CONV_DIM // CONV_CB,),
        in_specs=[pl.BlockSpec((s, CONV_CB), lambda i: (0, P_XBC // CONV_CB + i)),
                  pl.BlockSpec((SSD_CONV, CONV_CB), lambda i: (0, i)), pl.BlockSpec((1, CONV_CB), lambda i: (0, i)),
                  pl.BlockSpec((s, CONV_CB), lambda i: (0, i))],
        out_specs=[pl.BlockSpec((s, CONV_CB), lambda i: (0, i)), pl.BlockSpec((SSD_CONV, CONV_CB), lambda i: (0, i)),
                   pl.BlockSpec((1, CONV_CB), lambda i: (0, i))],
        out_shape=[jax.ShapeDtypeStruct((s, CONV_DIM), BF16), jax.ShapeDtypeStruct((SSD_CONV, CONV_DIM), F32),
                   jax.ShapeDtypeStruct((1, CONV_DIM), F32)],
        compiler_params=_cp("parallel"), name="conv_bwd")(proj, cw, cb, dout)


def _pool_window_sum(x, g, shift):
    s2 = x + shift(x, 1)
    s4 = s2 + shift(s2, 2)
    s8 = s4 + shift(s4, 4)
    s16 = s8 + shift(s8, 8)
    return jnp.where(g == 0, s2, jnp.where(g == 1, s4, jnp.where(g == 2, s8, s16)))


def _pool_count(shape, g):
    rows = lax.broadcasted_iota(jnp.int32, shape, 0)
    return jnp.minimum(rows + 1, jnp.left_shift(2, g)).astype(F32)


def _pool_fwd(proj):
    s = proj.shape[0]

    def body(u_ref, o_ref):
        g = pl.program_id(0)
        u = u_ref[...]
        o_ref[...] = (_pool_window_sum(u, g, _shift_down) / _pool_count(u.shape, g) - u).astype(o_ref.dtype)

    return pl.pallas_call(
        body, grid=(4,), in_specs=[pl.BlockSpec((s, POOL_GW), lambda g: (0, P_U // POOL_GW + g))],
        out_specs=pl.BlockSpec((s, POOL_GW), lambda g: (0, g)),
        out_shape=jax.ShapeDtypeStruct((s, 4 * POOL_GW), BF16), compiler_params=_cp("parallel"), name="pool_fwd")(proj)


def _pool_bwd(dp):
    s = dp.shape[0]

    def body(d_ref, o_ref):
        g = pl.program_id(0)
        d = d_ref[...]
        o_ref[...] = (_pool_window_sum(d / _pool_count(d.shape, g), g, _shift_up) - d).astype(o_ref.dtype)

    return pl.pallas_call(
        body, grid=(4,), in_specs=[pl.BlockSpec((s, POOL_GW), lambda g: (0, g))],
        out_specs=pl.BlockSpec((s, POOL_GW), lambda g: (0, g)),
        out_shape=jax.ShapeDtypeStruct((s, 4 * POOL_GW), BF16), compiler_params=_cp("parallel"), name="pool_bwd")(dp)


N_PAIRS = SSD_HEADS // 2
STATE_ROWS = N_PAIRS * SSD_STATE


def _ssd_chunk(xbc, dtr, hprev, dtb, alog, dsk):
    L = xbc.shape[0]
    xs, bm, cm = xbc[:, 0:1024], xbc[:, 1024:1280], xbc[:, 1280:1536]
    dt = _softplus(dtr + dtb)
    a = dt * (-jnp.exp(alog))
    acum = _csum(a)
    alast = jnp.sum(a, axis=0, keepdims=True)
    xdt = xs * dt
    xdecay = xdt * jnp.exp(alast - acum)
    eacum = jnp.exp(acum)
    elast = jnp.exp(alast)
    cb = [_bdot(cm[:, g * 128:(g + 1) * 128], bm[:, g * 128:(g + 1) * 128], "nt") for g in range(2)]
    rows = lax.broadcasted_iota(jnp.int32, (L, L), 0)
    cols = lax.broadcasted_iota(jnp.int32, (L, L), 1)
    causal = rows >= cols
    lane = lax.broadcasted_iota(jnp.int32, (L, LANES), 1)
    sub = lax.broadcasted_iota(jnp.int32, (LANES, L), 0)
    ys, hs = [], []
    for p in range(N_PAIRS):
        g = p // (N_PAIRS // 2)
        sl = slice(p * LANES, (p + 1) * LANES)
        ac = acum[:, sl]
        act = ac.T
        xp = xdt[:, sl]
        hp = hprev[p * SSD_STATE:(p + 1) * SSD_STATE, :]
        y = _bdot(cm[:, g * 128:(g + 1) * 128], hp, "nn") * eacum[:, sl] + dsk[:, sl] * xs[:, sl]
        for half in range(2):
            l0 = half * SSD_HEAD_DIM
            col = jnp.sum(jnp.where(lane == l0, ac, 0.0), axis=1, keepdims=True)
            row = jnp.sum(jnp.where(sub == l0, act, 0.0), axis=0, keepdims=True)
            decay = jnp.exp(jnp.where(causal, col - row, NEG))
            xh = jnp.where((lane >= l0) & (lane < l0 + SSD_HEAD_DIM), xp, 0.0)
            y = y + _bdot(cb[g] * decay, xh, "nn")
        ys.append(y)
        hs.append(elast[:, sl] * hp + _bdot(bm[:, g * 128:(g + 1) * 128], xdecay[:, sl], "tn"))
    return tuple(ys), tuple(hs)


def _ssd_fwd(xbc, proj, dtb, alog, dsk):
    s = xbc.shape[0]
    nc = s // SSD_CHUNK

    def body(x_ref, dt_ref, b_ref, a_ref, d_ref, y_ref, hist_ref, h_ref):
        @pl.when(pl.program_id(0) == 0)
        def _():
            h_ref[...] = jnp.zeros_like(h_ref)

        hprev = h_ref[...]
        hist_ref[...] = hprev
        ys, hs = _ssd_chunk(x_ref[...], dt_ref[...], hprev, b_ref[...], a_ref[...], d_ref[...])
        for p in range(N_PAIRS):
            y_ref[:, p * LANES:(p + 1) * LANES] = ys[p]
            h_ref[p * SSD_STATE:(p + 1) * SSD_STATE, :] = hs[p]

    return pl.pallas_call(
        body, grid=(nc,),
        in_specs=[pl.BlockSpec((SSD_CHUNK, CONV_DIM), lambda i: (i, 0)),
                  pl.BlockSpec((SSD_CHUNK, 1024), lambda i: (i, P_DT // 1024)),
                  _pspec((1, 1024)), _pspec((1, 1024)), _pspec((1, 1024))],
        out_specs=[pl.BlockSpec((SSD_CHUNK, 1024), lambda i: (i, 0)), pl.BlockSpec((STATE_ROWS, LANES), lambda i: (i, 0))],
        out_shape=[jax.ShapeDtypeStruct((s, 1024), F32), jax.ShapeDtypeStruct((nc * STATE_ROWS, LANES), F32)],
        scratch_shapes=[pltpu.VMEM((STATE_ROWS, LANES), F32)],
        compiler_params=_cp("arbitrary"), name="ssd_fwd")(xbc, proj, dtb, alog, dsk)


def _ssd_bwd(xbc, proj, hist, dtb, alog, dsk, dy):
    s = xbc.shape[0]
    nc = s // SSD_CHUNK

    def body(x_ref, dt_ref, hist_ref, b_ref, a_ref, d_ref, dy_ref, dx_ref, ddt_ref, db_ref, da_ref, dd_ref, dh_ref):
        first = pl.program_id(0) == 0

        @pl.when(first)
        def _():
            dh_ref[...] = jnp.zeros_like(dh_ref)

        _, vjp = jax.vjp(_ssd_chunk, x_ref[...], dt_ref[...], hist_ref[...], b_ref[...], a_ref[...], d_ref[...])
        dys = tuple(dy_ref[:, p * LANES:(p + 1) * LANES] for p in range(N_PAIRS))
        dhs = tuple(dh_ref[p * SSD_STATE:(p + 1) * SSD_STATE, :] for p in range(N_PAIRS))
        dx, ddt, dhp, db, da, dd = vjp((dys, dhs))
        dx_ref[...] = dx
        ddt_ref[...] = ddt.astype(ddt_ref.dtype)
        dh_ref[...] = dhp
        for o, v in ((db_ref, db), (da_ref, da), (dd_ref, dd)):
            @pl.when(first)
            def _(o=o, v=v):
                o[...] = v

            @pl.when(jnp.logical_not(first))
            def _(o=o, v=v):
                o[...] += v

    rev = lambda i: (nc - 1 - i, 0)
    return pl.pallas_call(
        body, grid=(nc,),
        in_specs=[pl.BlockSpec((SSD_CHUNK, CONV_DIM), rev),
                  pl.BlockSpec((SSD_CHUNK, 1024), lambda i: (nc - 1 - i, P_DT // 1024)),
                  pl.BlockSpec((STATE_ROWS, LANES), rev),
                  _pspec((1, 1024)), _pspec((1, 1024)), _pspec((1, 1024)),
                  pl.BlockSpec((SSD_CHUNK, 1024), rev)],
        out_specs=[pl.BlockSpec((SSD_CHUNK, CONV_DIM), rev), pl.BlockSpec((SSD_CHUNK, 1024), rev),
                   _pspec((1, 1024)), _pspec((1, 1024)), _pspec((1, 1024))],
        out_shape=[jax.ShapeDtypeStruct((s, CONV_DIM), F32), jax.ShapeDtypeStruct((s, 1024), BF16)]
        + [jax.ShapeDtypeStruct((1, 1024), F32)] * 3,
        scratch_shapes=[pltpu.VMEM((STATE_ROWS, LANES), F32)],
        compiler_params=_cp("arbitrary"), name="ssd_bwd")(xbc, proj, hist, dtb, alog, dsk, dy)


def _attn_head(q, kp, kc, vp, vc, has_prev, slope):
    scale = ATTN_HEAD_DIM ** -0.5
    n = ATTN_STEPS
    qi = lax.broadcasted_iota(jnp.int32, (n, n), 0)
    kj = lax.broadcasted_iota(jnp.int32, (n, n), 1)
    sp = jnp.where((kj >= qi) & has_prev, _bdot(q, kp, "nt") * scale - slope * (qi + n - kj).astype(F32), NEG)
    sc = jnp.where(kj <= qi, _bdot(q, kc, "nt") * scale - slope * (qi - kj).astype(F32), NEG)
    m = lax.stop_gradient(jnp.maximum(jnp.max(sp, axis=1, keepdims=True), jnp.max(sc, axis=1, keepdims=True)))
    pp, pc = jnp.exp(sp - m), jnp.exp(sc - m)
    den = jnp.sum(pp, axis=1, keepdims=True) + jnp.sum(pc, axis=1, keepdims=True)
    o = (_bdot(pp, vp, "nn") + _bdot(pc, vc, "nn")) / den
    return o, jnp.broadcast_to(m + jnp.log(den), (n, ATTN_HEAD_DIM))


def _head_slope(gi, h):
    s = [float(v) * DILATIONS[gi] for v in SLOPES[gi]]
    return jnp.where(h == 0, s[0], jnp.where(h == 1, s[1], jnp.where(h == 2, s[2], s[3])))


ATTN_UNROLL = 4


def _attn_heads_per_block(d):
    return 4 if d == 1 else 1


def _unit(ref, u, d):
    if d == 1:
        return ref.at[:, u * ATTN_HEAD_DIM:(u + 1) * ATTN_HEAD_DIM]
    return ref.at[pl.ds(u, ATTN_STEPS, stride=d), :]


def _for_units(units, step):
    if units == ATTN_UNROLL:
        for u in range(units):
            step(u)
        return

    def body(i, carry):
        for j in range(ATTN_UNROLL):
            step(i * ATTN_UNROLL + j)
        return carry

    lax.fori_loop(0, units // ATTN_UNROLL, body, 0)


def _attn_fwd(qn, kn, proj, gi):
    d = DILATIONS[gi]
    s = qn.shape[0]
    span = ATTN_STEPS * d
    nb = s // span

    hb, units = _attn_heads_per_block(d), _attn_heads_per_block(d) * d

    def body(q_ref, kp_ref, kc_ref, vp_ref, vc_ref, o_ref, l_ref):
        h0, b = pl.program_id(0) * hb, pl.program_id(1)

        def step(u):
            o, l = _attn_head(_unit(q_ref, u, d)[...], _unit(kp_ref, u, d)[...], _unit(kc_ref, u, d)[...],
                              _unit(vp_ref, u, d)[...], _unit(vc_ref, u, d)[...], b > 0, _head_slope(gi, h0 + u // d))
            _unit(o_ref, u, d)[...] = o
            _unit(l_ref, u, d)[...] = l

        _for_units(units, step)

    w = hb * ATTN_HEAD_DIM
    cur = pl.BlockSpec((span, w), lambda h, b: (b, (gi * 4) // hb + h))
    prev = pl.BlockSpec((span, w), lambda h, b: (jnp.maximum(b - 1, 0), (gi * 4) // hb + h))
    vcol = (P_V // ATTN_HEAD_DIM + gi * 4) // hb
    cur_v = pl.BlockSpec((span, w), lambda h, b: (b, vcol + h))
    prev_v = pl.BlockSpec((span, w), lambda h, b: (jnp.maximum(b - 1, 0), vcol + h))
    out = pl.BlockSpec((span, w), lambda h, b: (b, h))
    return pl.pallas_call(
        body, grid=(4 // hb, nb), in_specs=[cur, prev, cur, prev_v, cur_v], out_specs=[out, out],
        out_shape=[jax.ShapeDtypeStruct((s, ATTN_GROUP_W), F32)] * 2,
        compiler_params=_cp("parallel", "parallel"), name=f"attn_fwd_g{gi}")(qn, kn, kn, proj, proj)


def _attn_bwd(qn, kn, proj, do, dl, gi):
    d = DILATIONS[gi]
    s = qn.shape[0]
    span = ATTN_STEPS * d
    nb = s // span

    hb, units = _attn_heads_per_block(d), _attn_heads_per_block(d) * d

    def body(q_ref, kp_ref, kc_ref, vp_ref, vc_ref, do_ref, dl_ref, dq_ref, dk_ref, dv_ref, ck, cv):
        h0, bi = pl.program_id(0) * hb, pl.program_id(1)

        @pl.when(bi == 0)
        def _():
            ck[...] = jnp.zeros_like(ck)
            cv[...] = jnp.zeros_like(cv)

        has_prev = bi < nb - 1

        def step(u):
            f = functools.partial(_attn_head, has_prev=has_prev, slope=_head_slope(gi, h0 + u // d))
            _, vjp = jax.vjp(f, *(_unit(r, u, d)[...] for r in (q_ref, kp_ref, kc_ref, vp_ref, vc_ref)))
            dq, dkp, dkc, dvp, dvc = vjp((_unit(do_ref, u, d)[...], _unit(dl_ref, u, d)[...]))
            _unit(dq_ref, u, d)[...] = dq
            _unit(dk_ref, u, d)[...] = dkc + ck[u]
            _unit(dv_ref, u, d)[...] = dvc + cv[u]
            ck[u] = dkp
            cv[u] = dvp

        _for_units(units, step)

    w = hb * ATTN_HEAD_DIM
    cur = pl.BlockSpec((span, w), lambda h, b: (nb - 1 - b, (gi * 4) // hb + h))
    prev = pl.BlockSpec((span, w), lambda h, b: (jnp.maximum(nb - 2 - b, 0), (gi * 4) // hb + h))
    vcol = (P_V // ATTN_HEAD_DIM + gi * 4) // hb
    cur_v = pl.BlockSpec((span, w), lambda h, b: (nb - 1 - b, vcol + h))
    prev_v = pl.BlockSpec((span, w), lambda h, b: (jnp.maximum(nb - 2 - b, 0), vcol + h))
    out = pl.BlockSpec((span, w), lambda h, b: (nb - 1 - b, h))
    res = (units, ATTN_STEPS, ATTN_HEAD_DIM)
    return pl.pallas_call(
        body, grid=(4 // hb, nb), in_specs=[cur, prev, cur, prev_v, cur_v, out, out], out_specs=[out, out, out],
        out_shape=[jax.ShapeDtypeStruct((s, ATTN_GROUP_W), F32)] * 3,
        scratch_shapes=[pltpu.VMEM(res, F32), pltpu.VMEM(res, F32)],
        compiler_params=_cp("parallel", "arbitrary"), name=f"attn_bwd_g{gi}")(qn, kn, kn, proj, proj, do, dl)


def _layer_fwd(x, mod, W, l, late=None):
    sh1, sc1, g1, sh2, sc2, g2 = (mod[i:i + 1] for i in range(6))
    (h,) = _rowwise_fwd("norm1", _f_norm, [(x, 1024, 0)], [W["norm1_w"], sc1, sh1], [(1024, BF16)])
    proj = _matmul("in_proj", h, W["w_in"], "nt")
    xbc = _conv_fwd(proj, W["conv_w"], W["conv_b"])
    y, hist = _ssd_fwd(xbc, proj, W["dt_bias"], W["a_log"], W["d_skip"])
    if late is not None:
        W.update(late(y))
    (yn,) = _rowwise_fwd("ssd_gate", _f_ssdgate, [(y, 1024, 0), (proj, 1024, P_Z // 1024)], [W["ssd_norm_w"]], [(1024, BF16)])
    y_ssd = _matmul("ssd_out", yn, W["w_ssd_out"], "nn", layer=l)
    qn, kn = _rowwise_fwd("qk_norm", _f_qknorm, [(proj, 1536, P_Q // 1536), (proj, 1536, P_K // 1536)],
                          [W["q_norm_w"], W["k_norm_w"]], [(1536, F32)] * 2)
    ol = [_attn_fwd(qn, kn, proj, gi) for gi in range(3)]
    (o,) = _rowwise_fwd("attn_combine", _f_combine, [(t[0], 512, 0) for t in ol] + [(t[1], 512, 0) for t in ol], [], [(512, BF16)])
    y_attn = _matmul("attn_out", o, W["w_attn_out"], "nn", layer=l, chips=256)
    pooled = _pool_fwd(proj)
    pm = _group_matmul("pool_mix", pooled, W["w_pool_mix"], "nn", layer=l)
    (ps,) = _rowwise_fwd("pool_scale", _f_poolscale, [(pm, 1024, 0)], [W["pool_scale"]], [(1024, BF16)])
    y_pool = _matmul("pool_out", ps, W["w_pool_out"], "nn", layer=l)
    (merged,) = _rowwise_fwd("merge", _f_merge, [(proj, 3072, P_GATES // 3072), (y_ssd, 1024, 0), (y_attn, 1024, 0), (y_pool, 1024, 0)],
                             [], [(1024, BF16)])
    mo = _matmul("mix_out", merged, W["w_out"], "nn", layer=l)
    (x1,) = _rowwise_fwd("resid1", _f_resid, [(x, 1024, 0), (mo, 1024, 0)], [g1], [(1024, F32)])
    (h2,) = _rowwise_fwd("norm2", _f_norm, [(x1, 1024, 0)], [W["norm2_w"], sc2, sh2], [(1024, BF16)])
    a = _matmul("ff1", h2, W["w_ff1"], "nn", layer=l, chips=1024)
    (r,) = _rowwise_fwd("relu2", _f_relu2, [(a, D_FF, 0)], [], [(D_FF, BF16)], tb=128)
    ff = _matmul("ff2", r, W["w_ff2"], "nn", layer=l)
    (x2,) = _rowwise_fwd("resid2", _f_resid, [(x1, 1024, 0), (ff, 1024, 0)], [g2], [(1024, F32)])
    saved = dict(x=x, h=h, proj=proj, xbc=xbc, y=y, hist=hist, yn=yn, y_ssd=y_ssd, qn=qn, kn=kn, ol=ol, o=o,
                 y_attn=y_attn, pooled=pooled, pm=pm, ps=ps, y_pool=y_pool, merged=merged, mo=mo, x1=x1, h2=h2, a=a, r=r, ff=ff)
    return x2, saved


def _layer_bwd(dx2, mod, W, sv, l, bufs):
    sh1, sc1, g1, sh2, sc2, g2 = (mod[i:i + 1] for i in range(6))
    g = {}
    dx1a, dff, dg2 = _rowwise_bwd("resid2_bwd", _f_resid, [(sv["x1"], 1024, 0), (sv["ff"], 1024, 0)], [g2], [(dx2, 1024, 0)],
                                  [True, True], gdt=[F32, BF16])
    g["w_ff2"] = _matmul("ff2_dw", sv["r"], dff, "tn", BF16, into=(bufs["w_ff2"], l))
    dr = _matmul("ff2_dx", dff, W["w_ff2"], "nt", layer=l)
    (da,) = _rowwise_bwd("relu2_bwd", _f_relu2, [(sv["a"], D_FF, 0)], [], [(dr, D_FF, 0)], [True], tb=128, gdt=[BF16])
    g["w_ff1"] = _matmul("ff1_dw", sv["h2"], da, "tn", BF16, out_chips=1024, into=(bufs["w_ff1"], l))
    dh2 = _matmul("ff1_dx", da, W["w_ff1"], "nt", layer=l, chips=1024)
    dx1, g["norm2_w"], dsc2, dsh2 = _rowwise_bwd("norm2_bwd", _f_norm, [(sv["x1"], 1024, 0)], [W["norm2_w"], sc2, sh2],
                                                 [(dh2, 1024, 0)], [True], add=(0, dx1a))
    dxa, dmo, dg1 = _rowwise_bwd("resid1_bwd", _f_resid, [(sv["x"], 1024, 0), (sv["mo"], 1024, 0)], [g1], [(dx1, 1024, 0)],
                                 [True, True], gdt=[F32, BF16])
    g["w_out"] = _matmul("mix_out_dw", sv["merged"], dmo, "tn", BF16, into=(bufs["w_out"], l))
    dmerged = _matmul("mix_out_dx", dmo, W["w_out"], "nt", layer=l)
    proj = sv["proj"]
    dgates, dy_ssd, dy_attn, dy_pool = _rowwise_bwd(
        "merge_bwd", _f_merge, [(proj, 3072, P_GATES // 3072), (sv["y_ssd"], 1024, 0), (sv["y_attn"], 1024, 0), (sv["y_pool"], 1024, 0)],
        [], [(dmerged, 1024, 0)], [True] * 4, gdt=[BF16] * 4)
    g["w_pool_out"] = _matmul("pool_out_dw", sv["ps"], dy_pool, "tn", BF16, into=(bufs["w_pool_out"], l))
    dps = _matmul("pool_out_dx", dy_pool, W["w_pool_out"], "nt", layer=l)
    dpm, g["pool_scale"] = _rowwise_bwd("pool_scale_bwd", _f_poolscale, [(sv["pm"], 1024, 0)], [W["pool_scale"]], [(dps, 1024, 0)],
                                        [True], gdt=[BF16])
    dmix = _group_matmul("pool_mix_dw", sv["pooled"], dpm, "tn")
    g["w_pool_mix"] = bufs["w_pool_mix"].at[:, l].set(
        jnp.moveaxis(dmix.reshape(4, N_CHIPS, POOL_GW // N_CHIPS, POOL_GW), 1, 0).astype(BF16))
    dpooled = _group_matmul("pool_mix_dx", dpm, W["w_pool_mix"], "nt", layer=l)
    du = _pool_bwd(dpooled)
    g["w_attn_out"] = _matmul("attn_out_dw", sv["o"], dy_attn, "tn", BF16, out_chips=256, into=(bufs["w_attn_out"], l))
    do = _matmul("attn_out_dx", dy_attn, W["w_attn_out"], "nt", layer=l, chips=256)
    ol = sv["ol"]
    dol = _rowwise_bwd("attn_combine_bwd", _f_combine, [(t[0], 512, 0) for t in ol] + [(t[1], 512, 0) for t in ol], [],
                       [(do, 512, 0)], [True] * 6)
    dqs, dks, dvs = zip(*[_attn_bwd(sv["qn"], sv["kn"], proj, dol[gi], dol[3 + gi], gi) for gi in range(3)])
    dqn, dkn, dv = (jnp.concatenate(t, axis=1) for t in (dqs, dks, dvs))
    dv = dv.astype(BF16)
    dq, dk, g["q_norm_w"], g["k_norm_w"] = _rowwise_bwd(
        "qk_norm_bwd", _f_qknorm, [(proj, 1536, P_Q // 1536), (proj, 1536, P_K // 1536)],
        [W["q_norm_w"], W["k_norm_w"]], [(dqn, 1536, 0), (dkn, 1536, 0)], [True] * 2, gdt=[BF16] * 2)
    g["w_ssd_out"] = _matmul("ssd_out_dw", sv["yn"], dy_ssd, "tn", BF16, into=(bufs["w_ssd_out"], l))
    dyn = _matmul("ssd_out_dx", dy_ssd, W["w_ssd_out"], "nt", layer=l)
    dy, dz, g["ssd_norm_w"] = _rowwise_bwd("ssd_gate_bwd", _f_ssdgate, [(sv["y"], 1024, 0), (proj, 1024, P_Z // 1024)], [W["ssd_norm_w"]],
                                           [(dyn, 1024, 0)], [True, True], gdt=[F32, BF16])
    dxbc, ddt, g["dt_bias"], g["a_log"], g["d_skip"] = _ssd_bwd(sv["xbc"], proj, sv["hist"], W["dt_bias"], W["a_log"], W["d_skip"], dy)
    dxbc_raw, g["conv_w"], g["conv_b"] = _conv_bwd(proj, W["conv_w"], W["conv_b"], dxbc)
    dproj = jnp.concatenate([dxbc_raw, dq, dk, dv, dgates, dz, du, ddt], axis=1)
    g["w_in"] = _matmul("in_proj_dw", dproj, sv["h"], "tn", BF16)
    dh = _matmul("in_proj_dx", dproj, W["w_in"], "nn")
    dx, g["norm1_w"], dsc1, dsh1 = _rowwise_bwd("norm1_bwd", _f_norm, [(sv["x"], 1024, 0)], [W["norm1_w"], sc1, sh1],
                                                [(dh, 1024, 0)], [True], add=(0, dxa))
    dmod = jnp.concatenate([dsh1, dsc1, dg1, dsh2, dsc2, dg2], axis=0)
    return dx, dmod, g


def _expand_heads(t):
    return jnp.repeat(t, SSD_HEAD_DIM, axis=-1)


def _reduce_heads(t):
    return t.reshape(t.shape[:-1] + (SSD_HEADS, SSD_HEAD_DIM)).sum(-1)


_IN_SPLITS = np.cumsum((0,) + IN_SIZES)


def _w_in_to_layout(wt):
    z, xbc, dt, q, k, v, u, gates = (wt[_IN_SPLITS[i]:_IN_SPLITS[i + 1]] for i in range(8))
    return jnp.concatenate([xbc, q, k, v, gates, z, u, jnp.repeat(dt, SSD_HEAD_DIM, axis=0)], axis=0)


def _w_in_from_layout(g):
    xbc, q, k, v = (g[o:o + 1536] for o in (P_XBC, P_Q, P_K, P_V))
    gates, z, u, dt = g[P_GATES:P_GATES + 3072], g[P_Z:P_Z + 1024], g[P_U:P_U + 1024], g[P_DT:P_DT + 1024]
    dt = dt.astype(F32).reshape(SSD_HEADS, SSD_HEAD_DIM, D_MODEL).sum(1).astype(g.dtype)
    return jnp.concatenate([z, xbc, dt, q, k, v, u, gates], axis=0)


_STACKED = ("w_ssd_out", "w_attn_out", "w_pool_mix", "w_pool_out", "w_out", "w_ff1", "w_ff2")
_ROWS = ("norm1_w", "norm2_w", "conv_b", "ssd_norm_w", "q_norm_w", "k_norm_w", "pool_scale")
_HEAD_ROWS = ("dt_bias", "a_log", "d_skip")


def _layer_weights(wg, lg, small, l):
    W = {k: wg[k] for k in _STACKED if k in wg}
    W["w_in"] = _w_in_to_layout(wg["w_in"][lg].reshape(IN_WIDTH, D_MODEL))
    W["conv_w"] = small["conv_w"][l]
    for k in _ROWS:
        W[k] = small[k][l][None, :]
    for k in _HEAD_ROWS:
        W[k] = _expand_heads(small[k][l])[None, :]
    return W


def _layer_grads_by_chip(g, w_in_buf, l):
    out = dict(g)
    out["w_in"] = w_in_buf.at[:, l].set(_w_in_from_layout(g["w_in"]).reshape(N_CHIPS, IN_WIDTH // N_CHIPS, D_MODEL))
    for k in _ROWS:
        out[k] = g[k][0]
    for k in _HEAD_ROWS:
        out[k] = _reduce_heads(g[k][0])
    return out


ANY = pl.BlockSpec(memory_space=pl.ANY)


def _place():
    x, y, c = lax.axis_index("x"), lax.axis_index("y"), lax.axis_index("c")
    return x, y, c, (x, y, 1 - c), [(1 - x, y), (x, 1 - y), (1 - x, 1 - y)]


def _allgather8(name, blk):
    m_per, n = blk.shape

    def body(x_ref, out_ref, send_sems, recv_sems, local_sem):
        x, y, c, sibling, chips = _place()
        me = (x, y, c)

        def rows(px, py, pc):
            return out_ref.at[pl.ds((4 * px + 2 * py + pc) * m_per, m_per), :]

        def copy(k, block, to, src=None):
            return pltpu.make_async_remote_copy(
                src_ref=rows(*block) if src is None else src, dst_ref=rows(*block),
                send_sem=send_sems.at[k], recv_sem=recv_sems.at[k], device_id=to, device_id_type=MESH)

        mine = pltpu.make_async_copy(x_ref, rows(*me), local_sem)
        mine.start()
        first = [copy(0, me, sibling, src=x_ref)]
        first += [copy(1 + j, me, (*chip, c), src=x_ref) for j, chip in enumerate(chips)]
        for cp in first:
            cp.start()
        passed = [copy(4 + j, (*chip, c), sibling) for j, chip in enumerate(chips)]
        for j, chip in enumerate(chips):
            copy(1 + j, (*chip, c), me).wait_recv()
            passed[j].start()
        copy(0, sibling, me).wait_recv()
        for j, chip in enumerate(chips):
            copy(4 + j, (*chip, 1 - c), me).wait_recv()
        for cp in first + passed:
            cp.wait_send()
        mine.wait()

    return pl.pallas_call(
        body, out_shape=jax.ShapeDtypeStruct((N_DEV * m_per, n), blk.dtype),
        in_specs=[pl.BlockSpec(memory_space=pltpu.VMEM)], out_specs=pl.BlockSpec(memory_space=pltpu.VMEM),
        scratch_shapes=[pltpu.SemaphoreType.DMA((7,)), pltpu.SemaphoreType.DMA((7,)), pltpu.SemaphoreType.DMA],
        name=name)(blk)


HBM_SPEC = pl.BlockSpec(memory_space=pltpu.HBM)
SEM_SPEC = pl.BlockSpec(memory_space=pltpu.SEMAPHORE)
SIDE_EFFECT = pltpu.SideEffectType.DATAFLOW_SIDE_EFFECTING


def _dma_sems(n):
    return [pltpu.SemaphoreType.DMA((n,)), pltpu.SemaphoreType.DMA((n,))]


def _half_axis(shape):
    return 1 if (len(shape) > 3 or (shape[1] // 2) % 16 == 0) else len(shape) - 1


def _halves(ref, axis, c):
    r2 = ref.shape[axis] // 2
    lead = (slice(None),) * axis
    return ref.at[lead + (pl.ds(r2 * c, r2),)], ref.at[lead + (pl.ds(r2 * (1 - c), r2),)]


def _gather_copies(srcs, lands, send_sems, recv_sems):
    x, y, c, _, chips = _place()
    sends, lands_here = [], []
    for j, (cx, cy) in enumerate(chips):
        for i, (s, t) in enumerate(zip(srcs, lands)):
            k = 3 * i + j
            ax = _half_axis(s.shape)
            mine = _halves(t.at[:, 2 * x + y], ax, c)[0]
            theirs = _halves(t.at[:, 2 * cx + cy], ax, c)[0]
            sends.append(pltpu.make_async_remote_copy(src_ref=_halves(s, ax, c)[0], dst_ref=mine, send_sem=send_sems.at[k],
                                                      recv_sem=recv_sems.at[k], device_id=(cx, cy, c), device_id_type=MESH))
            lands_here.append(pltpu.make_async_remote_copy(src_ref=theirs, dst_ref=theirs, send_sem=send_sems.at[k],
                                                           recv_sem=recv_sems.at[k], device_id=(cx, cy, c), device_id_type=MESH))
    return sends, lands_here


def _exchange_copies(srcs, lands, send_sems, recv_sems):
    x, y, c, _, chips = _place()
    sends = [pltpu.make_async_remote_copy(src_ref=s.at[2 * cx + cy], dst_ref=t.at[j], send_sem=send_sems.at[3 * i + j],
                                          recv_sem=recv_sems.at[3 * i + j], device_id=(cx, cy, c), device_id_type=MESH)
             for j, (cx, cy) in enumerate(chips) for i, (s, t) in enumerate(zip(srcs, lands))]
    return sends, sends


def _split_start(name, copies, srcs, lands, after):
    ns, nl = len(srcs), len(lands)
    n_copies = 3 * ns

    def body(*refs):
        send_sems, recv_sems = refs[ns + nl + 1], refs[ns + nl + 2]
        for cp in copies(refs[:ns], refs[ns:ns + nl], send_sems, recv_sems)[0]:
            cp.start()
        refs[-1][...] = jnp.zeros_like(refs[-1])

    arrs = list(srcs) + list(lands)
    res = pl.pallas_call(
        body, name=name,
        out_shape=(pltpu.SemaphoreType.DMA((n_copies,)), pltpu.SemaphoreType.DMA((n_copies,)))
        + tuple(pltpu.HBM(a.shape, a.dtype) for a in arrs) + (jax.ShapeDtypeStruct((8, LANES), F32),),
        in_specs=[HBM_SPEC] * (ns + nl) + [ANY],
        out_specs=(SEM_SPEC, SEM_SPEC) + (HBM_SPEC,) * (ns + nl) + (pl.BlockSpec(memory_space=pltpu.VMEM),),
        input_output_aliases={i: 2 + i for i in range(ns + nl)},
        compiler_params=pltpu.CompilerParams(has_side_effects=SIDE_EFFECT),
    )(*[pltpu.with_memory_space_constraint(a, pltpu.HBM) for a in arrs], after)
    return res[0], res[1], list(res[2:2 + ns]), list(res[2 + ns:2 + ns + nl]), res[-1]


def _split_wait(name, copies, send_sems, recv_sems, srcs, lands, after):
    ns, nl = len(srcs), len(lands)

    def body(*refs):
        sends, lands_here = copies(refs[:ns], refs[ns:ns + nl], refs[ns + nl], refs[ns + nl + 1])
        for cp in sends:
            cp.wait_send()
        for cp in lands_here:
            cp.wait_recv()

    arrs = list(srcs) + list(lands)
    res = pl.pallas_call(
        body, name=name, out_shape=tuple(pltpu.HBM(a.shape, a.dtype) for a in arrs),
        in_specs=[HBM_SPEC] * (ns + nl) + [SEM_SPEC, SEM_SPEC, ANY], out_specs=(HBM_SPEC,) * (ns + nl),
        input_output_aliases={i: i for i in range(ns + nl)},
        compiler_params=pltpu.CompilerParams(has_side_effects=SIDE_EFFECT),
    )(*arrs, send_sems, recv_sems, after)
    return list(res[:ns]), list(res[ns:])


def _gather_forward(name, lands):
    n = len(lands)

    def body(*refs):
        ins, outs = refs[:n], refs[n:2 * n]
        send_sems, recv_sems = refs[2 * n:]
        x, y, c, sibling, chips = _place()
        sends, arrivals = [], []
        for j, (cx, cy) in enumerate(chips):
            for i in range(n):
                k = 3 * i + j
                ax = _half_axis(ins[i].shape[:1] + ins[i].shape[2:])
                src = _halves(ins[i].at[:, 2 * cx + cy], ax, c)[0]
                dst, theirs = _halves(outs[i].at[:, 2 * cx + cy], ax, c)
                sends.append(pltpu.make_async_remote_copy(src_ref=src, dst_ref=dst, send_sem=send_sems.at[k], recv_sem=recv_sems.at[k],
                                                          device_id=sibling, device_id_type=MESH))
                arrivals.append(pltpu.make_async_remote_copy(src_ref=theirs, dst_ref=theirs, send_sem=send_sems.at[k],
                                                             recv_sem=recv_sems.at[k], device_id=sibling, device_id_type=MESH))
        for cp in sends:
            cp.start()
        for cp in arrivals:
            cp.wait_recv()
        for cp in sends:
            cp.wait_send()

    return pl.pallas_call(
        body, out_shape=[jax.ShapeDtypeStruct(t.shape, t.dtype) for t in lands], in_specs=[ANY] * n, out_specs=[ANY] * n,
        input_output_aliases={i: i for i in range(n)}, scratch_shapes=_dma_sems(3 * n), name=name)(*lands)


def _swap_halves(name, gs):
    n = len(gs)

    def body(*refs):
        ins, got = refs[:n], refs[n:2 * n]
        send_sems, recv_sems = refs[2 * n:]
        x, y, c, sibling, _ = _place()
        sends = [pltpu.make_async_remote_copy(src_ref=_halves(ins[i], 1 + _half_axis(ins[i].shape[1:]), c)[1], dst_ref=got[i],
                                              send_sem=send_sems.at[i], recv_sem=recv_sems.at[i], device_id=sibling, device_id_type=MESH)
                 for i in range(n)]
        for cp in sends:
            cp.start()
        for cp in sends:
            cp.wait_recv()
        for cp in sends:
            cp.wait_send()

    def half_shape(t):
        ax = 1 + _half_axis(t.shape[1:])
        return t.shape[:ax] + (t.shape[ax] // 2,) + t.shape[ax + 1:]

    return pl.pallas_call(
        body, out_shape=[jax.ShapeDtypeStruct(half_shape(t), t.dtype) for t in gs],
        in_specs=[ANY] * n, out_specs=[ANY] * n, scratch_shapes=_dma_sems(n), name=name)(*gs)


def _share_halves(name, ts, axes):
    n = len(ts)

    def body(*refs):
        ins, outs = refs[:n], refs[n:2 * n]
        send_sems, recv_sems = refs[2 * n:]
        x, y, c, sibling, _ = _place()
        sends, arrivals = [], []
        for i in range(n):
            mine, theirs = _halves(outs[i], axes[i], c)
            sends.append(pltpu.make_async_remote_copy(src_ref=ins[i], dst_ref=mine, send_sem=send_sems.at[i], recv_sem=recv_sems.at[i],
                                                      device_id=sibling, device_id_type=MESH))
            arrivals.append(pltpu.make_async_remote_copy(src_ref=ins[i], dst_ref=theirs, send_sem=send_sems.at[i],
                                                         recv_sem=recv_sems.at[i], device_id=sibling, device_id_type=MESH))
        for cp in sends:
            cp.start()
        for cp in arrivals:
            cp.wait_recv()
        for cp in sends:
            cp.wait_send()

    return pl.pallas_call(
        body, out_shape=[jax.ShapeDtypeStruct(t.shape[:ax] + (2 * t.shape[ax],) + t.shape[ax + 1:], t.dtype) for t, ax in zip(ts, axes)],
        in_specs=[ANY] * n, out_specs=[ANY] * n, scratch_shapes=_dma_sems(n), name=name)(*ts)


PACK_W = 1024
PACK_TB = 512


def _sum_rows(name, parts, out_dtype):
    def f(*vals):
        acc = vals[0]
        for v in vals[1:]:
            acc = acc + v
        return (acc,)

    return _rowwise_fwd(name, f, [(p, PACK_W, 0) for p in parts], [], [(PACK_W, out_dtype)], tb=_tile(parts[0].shape[0], PACK_TB))[0]


def _sum_slots(name, ops, count, out_dtype):
    mat = ops[0][0].shape[1:]

    def body(*refs):
        acc = refs[0][...].astype(F32)
        for r in refs[1:-1]:
            acc = acc + r[...].astype(F32)
        refs[-1][...] = acc.astype(refs[-1].dtype)

    return pl.pallas_call(
        body, grid=(count,), in_specs=[pl.BlockSpec((None,) + mat, lambda i, s=s: (s * count + i, 0, 0)) for _, s in ops],
        out_specs=pl.BlockSpec((None,) + mat, lambda i: (i, 0, 0)), out_shape=jax.ShapeDtypeStruct((count,) + mat, out_dtype),
        compiler_params=_cp("parallel"), name=name)(*[a for a, _ in ops])


def _adamw_update(w_ref, g_ref, m_ref, v_ref, go_ref, d_ref, mo_ref, vo_ref):
    c1 = 1.0 / (1.0 - ADAM_B1 ** ADAM_STEP)
    c2 = 1.0 / (1.0 - ADAM_B2 ** ADAM_STEP)
    gg = g_ref[...]
    mn = ADAM_B1 * m_ref[...] + (1.0 - ADAM_B1) * gg
    vn = ADAM_B2 * v_ref[...] + (1.0 - ADAM_B2) * jnp.square(gg)
    go_ref[...] = gg
    d_ref[...] = -ADAM_LR * ((mn * c1) / (jnp.sqrt(vn * c2) + ADAM_EPS) + ADAM_WD * w_ref[...])
    mo_ref[...] = mn
    vo_ref[...] = vn


def _adamw_layers(name, w, g, m, v, lo, prev=None, dep=None):
    _, r, cw = w.shape

    def body(w_ref, g_ref, m_ref, v_ref, *rest):
        _adamw_update(w_ref, g_ref, m_ref, v_ref, *rest[-4:])

    full = pl.BlockSpec((None, r, LANES), lambda l, i: (lo + l, 0, i))
    extra = ([] if prev is None else list(prev)) + ([] if dep is None else [dep])
    return pl.pallas_call(
        body, grid=(g.shape[0], cw // LANES),
        in_specs=[full, pl.BlockSpec((None, r, LANES), lambda l, i: (l, 0, i)), full, full] + [pl.BlockSpec(memory_space=pl.ANY)] * len(extra),
        out_specs=[full] * 4, out_shape=[jax.ShapeDtypeStruct(w.shape, F32)] * 4,
        input_output_aliases={4 + i: i for i in range(0 if prev is None else 4)},
        compiler_params=_cp("parallel", "parallel"), name=name)(w, g, m, v, *extra)


def _adamw(name, w, g, m, v, row0=0, unit=None, prev=None, dep=None):
    r, cw = w.shape
    tb = unit or r
    while tb * cw > 300_000 and tb % 16 == 0:
        tb //= 2
    off = row0 // tb

    def body(w_ref, g_ref, m_ref, v_ref, *rest):
        _adamw_update(w_ref, g_ref, m_ref, v_ref, *rest[-4:])

    full = pl.BlockSpec((tb, cw), lambda i: (off + i, 0))
    extra = ([] if prev is None else list(prev)) + ([] if dep is None else [dep])
    n_prev = 0 if prev is None else 4
    return pl.pallas_call(
        body, grid=(g.shape[0] // tb,),
        in_specs=[full, pl.BlockSpec((tb, cw), lambda i: (i, 0)), full, full] + [pl.BlockSpec(memory_space=pl.ANY)] * len(extra),
        out_specs=[full] * 4, out_shape=[jax.ShapeDtypeStruct((r, cw), F32)] * 4,
        input_output_aliases={4 + i: i for i in range(n_prev)},
        compiler_params=_cp("parallel"), name=name)(w, g, m, v, *extra)


def _silu_rows(c):
    def body(c_ref, o_ref):
        rows = lax.broadcasted_iota(jnp.int32, o_ref.shape, 0)
        o_ref[...] = jnp.where(rows == 0, jnp.broadcast_to(_silu(c_ref[...]), o_ref.shape), 0.0)

    return pl.pallas_call(body, out_shape=jax.ShapeDtypeStruct((8, c.shape[1]), F32), name="cond_silu")(c)


_KINDS = ("w_in", "w_ssd_out", "w_attn_out", "w_pool_mix", "w_pool_out", "w_out", "w_ff1", "w_ff2")
_SMALL = ("b_ada", "norm1_w", "norm2_w", "conv_b", "dt_bias", "a_log", "d_skip", "ssd_norm_w", "q_norm_w", "k_norm_w",
          "pool_scale")
_ORDER = ("w_ada", "b_ada", "norm1_w", "norm2_w", "w_in", "conv_w", "conv_b", "dt_bias", "a_log", "d_skip", "ssd_norm_w",
          "w_ssd_out", "q_norm_w", "k_norm_w", "w_attn_out", "w_pool_mix", "pool_scale", "w_pool_out", "w_out", "w_ff1", "w_ff2")


def _pack_flat(arrs, rows, dtype):
    flat = jnp.concatenate([a.reshape(-1).astype(dtype) for a in arrs])
    return jnp.pad(flat, (0, rows * PACK_W - flat.shape[0])).reshape(rows, PACK_W)


def _unpack_flat(buf, shapes):
    flat = buf.reshape(-1)
    out, off = [], 0
    for shp in shapes:
        n = int(np.prod(shp))
        out.append(flat[off:off + n].reshape(shp))
        off += n
    return out


def _small_rows(n_elems):
    return -(-n_elems // (8 * PACK_W)) * 8


def kernel(x, c, w_ada, b_ada, norm1_w, norm2_w, w_in, conv_w, conv_b, dt_bias, a_log, d_skip, ssd_norm_w, w_ssd_out, q_norm_w, k_norm_w, w_attn_out, w_pool_mix, pool_scale, w_pool_out, w_out, w_ff1, w_ff2, loss_target, m_w_ada, m_b_ada, m_norm1_w, m_norm2_w, m_w_in, m_conv_w, m_conv_b, m_dt_bias, m_a_log, m_d_skip, m_ssd_norm_w, m_w_ssd_out, m_q_norm_w, m_k_norm_w, m_w_attn_out, m_w_pool_mix, m_pool_scale, m_w_pool_out, m_w_out, m_w_ff1, m_w_ff2, v_w_ada, v_b_ada, v_norm1_w, v_norm2_w, v_w_in, v_conv_w, v_conv_b, v_dt_bias, v_a_log, v_d_skip, v_ssd_norm_w, v_w_ssd_out, v_q_norm_w, v_k_norm_w, v_w_attn_out, v_w_pool_mix, v_pool_scale, v_w_pool_out, v_w_out, v_w_ff1, v_w_ff2):
    w = dict(w_ada=w_ada, b_ada=b_ada, norm1_w=norm1_w, norm2_w=norm2_w, w_in=w_in, conv_w=conv_w, conv_b=conv_b, dt_bias=dt_bias, a_log=a_log, d_skip=d_skip, ssd_norm_w=ssd_norm_w, w_ssd_out=w_ssd_out, q_norm_w=q_norm_w, k_norm_w=k_norm_w, w_attn_out=w_attn_out, w_pool_mix=w_pool_mix, pool_scale=pool_scale, w_pool_out=w_pool_out, w_out=w_out, w_ff1=w_ff1, w_ff2=w_ff2)
    m = dict(w_ada=m_w_ada, b_ada=m_b_ada, norm1_w=m_norm1_w, norm2_w=m_norm2_w, w_in=m_w_in, conv_w=m_conv_w, conv_b=m_conv_b, dt_bias=m_dt_bias, a_log=m_a_log, d_skip=m_d_skip, ssd_norm_w=m_ssd_norm_w, w_ssd_out=m_w_ssd_out, q_norm_w=m_q_norm_w, k_norm_w=m_k_norm_w, w_attn_out=m_w_attn_out, w_pool_mix=m_w_pool_mix, pool_scale=m_pool_scale, w_pool_out=m_w_pool_out, w_out=m_w_out, w_ff1=m_w_ff1, w_ff2=m_w_ff2)
    v = dict(w_ada=v_w_ada, b_ada=v_b_ada, norm1_w=v_norm1_w, norm2_w=v_norm2_w, w_in=v_w_in, conv_w=v_conv_w, conv_b=v_conv_b, dt_bias=v_dt_bias, a_log=v_a_log, d_skip=v_d_skip, ssd_norm_w=v_ssd_norm_w, w_ssd_out=v_w_ssd_out, q_norm_w=v_q_norm_w, k_norm_w=v_k_norm_w, w_attn_out=v_w_attn_out, w_pool_mix=v_w_pool_mix, pool_scale=v_pool_scale, w_pool_out=v_w_pool_out, w_out=v_w_out, w_ff1=v_w_ff1, w_ff2=v_w_ff2)
    chip = 2 * lax.axis_index("x") + lax.axis_index("y")
    dev = 2 * chip + lax.axis_index("c")
    ada_cols = w_ada.shape[2]

    wk = dict({k: w[k] for k in _KINDS}, w_in=jnp.transpose(w_in, (0, 2, 1)))
    mk, vk = {"w_in": jnp.transpose(m_w_in, (0, 2, 1))}, {"w_in": jnp.transpose(v_w_in, (0, 2, 1))}
    rest = _KINDS[1:]

    def start_gather(tag, lo, n, kinds, after):
        shards = [wk[k][lo:lo + n].astype(BF16) for k in kinds]
        lands = [lax.empty((n, N_CHIPS) + s.shape[1:], BF16) for s in shards]
        return _split_start("gather_start_" + tag, _gather_copies, shards, lands, after)

    gather_a1 = start_gather("a1", 0, 1, _KINDS[:1], c)
    c = c + gather_a1[4][0, 0]

    n_conv = conv_w.size // PACK_W
    rows1 = _small_rows((1 + n_conv) * PACK_W)
    blk = jnp.concatenate([_silu_rows(c)[:1], conv_w.reshape(n_conv, PACK_W), jnp.zeros((rows1 - 1 - n_conv, PACK_W), F32)])
    first = _allgather8("gather_cond", blk).reshape(N_DEV, rows1, PACK_W)
    cond_all = first[:, 0]
    conv_all = first[0::2, 1:1 + n_conv].reshape((N_CHIPS,) + conv_w.shape)
    conv_full = jnp.moveaxis(conv_all, 0, 2).reshape(DEPTH, SSD_CONV, CONV_DIM)
    b_cols = lax.dynamic_slice_in_dim(b_ada, chip * ada_cols, ada_cols, axis=1)
    mod_cols = jnp.stack([_matmul("ada_fwd", cond_all, w_ada[l], "nn", precise=True) + b_cols[l][None, :] for l in range(DEPTH)])
    mod_all = _allgather8("gather_mod", mod_cols.reshape(-1, PACK_W)).reshape(N_DEV, DEPTH, N_DEV, ada_cols)
    mine = lax.dynamic_index_in_dim(mod_all[0::2], dev, axis=2, keepdims=False)
    mods = jnp.moveaxis(mine, 0, 1).reshape(DEPTH, 6, D_MODEL)

    core = lax.axis_index("c")
    small_w = dict({k: w[k] for k in _SMALL[1:]}, conv_w=conv_full)

    def finish_gather(tag, handle, kinds, after):
        shards, lands = _split_wait("gather_wait_" + tag, _gather_copies, handle[0], handle[1], handle[2], handle[3], after)
        lands = _gather_forward("gather_forward_" + tag, lands)
        wg = {k: lax.dynamic_update_slice_in_dim(t, s[:, None], chip, axis=1) for k, t, s in zip(kinds, lands, shards)}
        n = shards[0].shape[0]
        for k in ("w_ssd_out", "w_pool_out", "w_out", "w_ff2"):
            if k in wg:
                wg[k] = wg[k].reshape(n, -1, D_MODEL)
        if "w_pool_mix" in wg:
            wg["w_pool_mix"] = jnp.moveaxis(wg["w_pool_mix"], 1, 2).reshape(n, 4, POOL_GW, POOL_GW)
        return wg

    wg_a1 = finish_gather("a1", gather_a1, _KINDS[:1], mods)
    gather_a2 = start_gather("a2", 0, 1, rest, wg_a1["w_in"])
    gather_b = start_gather("b", 1, DEPTH - 1, _KINDS, gather_a2[4])
    mods = mods + gather_b[4][0, 0]

    xc = x[0]
    Ws, saved = [None] * DEPTH, [None] * DEPTH
    Ws[0] = _layer_weights(wg_a1, 0, small_w, 0)
    xc, saved[0] = _layer_fwd(xc, mods[0], Ws[0], 0, late=lambda y: finish_gather("a2", gather_a2, rest, y))
    wg_b = finish_gather("b", gather_b, _KINDS, xc)
    for l in range(1, DEPTH):
        Ws[l] = _layer_weights(wg_b, l - 1, small_w, l)
        xc, saved[l] = _layer_fwd(xc, mods[l], Ws[l], l - 1)
    dx, loss = _loss_and_grad(xc, loss_target[0])
    dmods, grads = [None] * DEPTH, [None] * DEPTH

    def backward(l, lg, dx, mod, bufs):
        dx, dmods[l], g = _layer_bwd(dx, mod, Ws[l], saved[l], lg, bufs)
        grads[l] = _layer_grads_by_chip(g, bufs["w_in"], lg)
        return dx, {k: grads[l][k] for k in _KINDS}

    bufs_b = {k: lax.empty((N_CHIPS, DEPTH - 1) + wk[k].shape[1:], BF16) for k in _KINDS}
    for l in reversed(range(1, DEPTH)):
        dx, bufs_b = backward(l, l - 1, dx, mods[l], bufs_b)

    def flat(t):
        return t.reshape((-1,) + t.shape[-2:])

    def start_exchange(tag, bufs, after):
        gs = [bufs[k] for k in _KINDS]
        got = _swap_halves("grad_swap_" + tag, gs)
        axes = [1 + _half_axis(t.shape[1:]) for t in gs]
        own = [lax.dynamic_slice_in_dim(t, (t.shape[ax] // 2) * core, t.shape[ax] // 2, axis=ax) for t, ax in zip(gs, axes)]
        pairs = [_sum_slots(f"sum_pair_{tag}_{k}", [(flat(a), 0), (flat(b), 0)], flat(a).shape[0], BF16).reshape(a.shape)
                 for k, a, b in zip(_KINDS, own, got)]
        lands = [lax.empty((3,) + p.shape[1:], BF16) for p in pairs]
        return _split_start("exchange_start_" + tag, _exchange_copies, pairs, lands, after)

    def finish_exchange(tag, handle, after):
        pairs, partials = _split_wait("exchange_wait_" + tag, _exchange_copies, handle[0], handle[1], handle[2], handle[3], after)
        mine = [lax.dynamic_index_in_dim(p, chip, axis=0, keepdims=False) for p in pairs]
        totals = [_sum_slots(f"sum_chips_{tag}_{k}", [(flat(a), 0)] + [(flat(p), s) for s in range(3)], flat(a).shape[0], F32).reshape(a.shape)
                  for k, a, p in zip(_KINDS, mine, partials)]
        axes = [_half_axis((1,) + wk[k].shape[1:]) for k in _KINDS]
        return [lax.dynamic_update_slice_in_dim(t, mine_t, mine_t.shape[ax] * core, axis=ax)
                for t, mine_t, ax in zip(_share_halves("grad_share_" + tag, totals, axes), totals, axes)]

    def adamw_group(tag, lo, gs, prev, dep):
        out = {}
        out["w_in"] = _adamw_layers(f"adamw_{tag}_w_in", wk["w_in"], gs[0], mk["w_in"], vk["w_in"], lo,
                                    prev=None if prev is None else prev["w_in"], dep=dep)
        dep = out["w_in"][1]
        for k, gk in zip(rest, gs[1:]):
            shp = w[k].shape
            unit = int(np.prod(shp[1:-1]))
            two_d = lambda t: t.reshape(-1, shp[-1])
            out[k] = _adamw(f"adamw_{tag}_{k}", two_d(w[k]), two_d(gk), two_d(m[k]), two_d(v[k]), row0=lo * unit, unit=unit,
                            prev=None if prev is None else prev[k], dep=dep)
            dep = out[k][1]
        return out

    exchange_b = start_exchange("b", bufs_b, dx)
    bufs_a = {k: lax.empty((N_CHIPS, 1) + wk[k].shape[1:], BF16) for k in _KINDS}
    dx, bufs_a = backward(0, 0, dx, mods[0] + exchange_b[4][0, 0], bufs_a)
    g_b = finish_exchange("b", exchange_b, dx)
    grad_x, dmods = dx, jnp.stack(dmods)
    exchange_a = start_exchange("a", bufs_a, g_b[0])
    adam_b = adamw_group("b", 1, g_b, None, exchange_a[4])

    small = ([dmods] + [jnp.stack([grads[l][k] for l in range(DEPTH)]) for k in _SMALL[1:] + ("conv_w",)] + [loss[:, :1]])
    n_small = sum(int(np.prod(a.shape)) for a in small)
    rows_small = _small_rows(n_small)
    small_all = _allgather8("gather_small", _pack_flat(small, rows_small, F32))
    parts = [small_all[d * rows_small:(d + 1) * rows_small] for d in range(N_DEV)]
    small_sum = _unpack_flat(_sum_rows("sum_small", parts, F32), [a.shape for a in small])
    g_out = {"b_ada": small_sum[0].reshape(DEPTH, 6 * D_MODEL)}
    for k, t in zip(_SMALL[1:], small_sum[1:-2]):
        g_out[k] = t
    g_out["conv_w"] = lax.dynamic_slice_in_dim(small_sum[-2], chip * conv_w.shape[2], conv_w.shape[2], axis=2)
    loss_out = small_sum[-1][0, 0]
    dmod_all = jnp.stack([p[:DEPTH * 6].reshape(DEPTH, 6 * D_MODEL) for p in parts])
    dmod_cols = lax.dynamic_slice_in_dim(dmod_all, chip * ada_cols, ada_cols, axis=2)
    g_out["w_ada"] = jnp.stack([_matmul("ada_dw", cond_all, dmod_cols[:, l], "tn", precise=True) for l in range(DEPTH)])

    deltas, new_m, new_v = {}, {}, {}
    dep = adam_b[_KINDS[-1]][1]
    for k in ("w_ada", "conv_w"):
        shp = w[k].shape
        two_d = (int(np.prod(shp[:-1])), shp[-1])
        res = _adamw("adamw_" + k, *(t.reshape(two_d) for t in (w[k], g_out[k], m[k], v[k])), dep=dep)
        deltas[k], new_m[k], new_v[k] = (t.reshape(shp) for t in res[1:])
        dep = res[1]
    small_shapes = [w[k].shape for k in _SMALL]
    n_sm = sum(int(np.prod(s)) for s in small_shapes)
    res = _adamw("adamw_small", *[_pack_flat([t[k] for k in _SMALL], _small_rows(n_sm), F32) for t in (w, g_out, m, v)], dep=dep)[1:]
    for name_map, buf in zip((deltas, new_m, new_v), res):
        for k, t in zip(_SMALL, _unpack_flat(buf, small_shapes)):
            name_map[k] = t
    g_a = finish_exchange("a", exchange_a, res[0])
    for k, t in adamw_group("a", 0, g_a, adam_b, None).items():
        if k == "w_in":
            g_out[k], deltas[k], new_m[k], new_v[k] = (jnp.transpose(u, (0, 2, 1)) for u in t)
        else:
            g_out[k], deltas[k], new_m[k], new_v[k] = (u.reshape(w[k].shape) for u in t)

    return (loss_out, grad_x[None], *[g_out[k] for k in _ORDER], *[deltas[k] for k in _ORDER],
            *[new_m[k] for k in _ORDER], *[new_v[k] for k in _ORDER])
```

```python
import functools
import math

import numpy as np
import jax
import jax.numpy as jnp
from jax import lax
from jax.experimental import pallas as pl
from jax.experimental.pallas import tpu as pltpu

F32, BF16 = jnp.float32, jnp.bfloat16
MESH = pl.DeviceIdType.MESH

D_MODEL = 1024
DEPTH = 4
N_CHIPS = 4
N_DEV = 8
SSD_HEADS = 16
SSD_HEAD_DIM = 64
SSD_STATE = 128
SSD_CHUNK = 128
SSD_CONV = 4
CONV_DIM = 1536
ATTN_HEAD_DIM = 128
ATTN_GROUP_W = 512
DILATIONS = (1, 4, 16)
ATTN_STEPS = 128
POOL_WINDOWS = (2, 4, 8, 16)
POOL_GW = 256
D_FF = 4096
EPS = 1e-6
IN_SIZES = (1024, 1536, 16, 1536, 1536, 1536, 1024, 3072)
IN_WIDTH = sum(IN_SIZES)
P_XBC, P_Q, P_K, P_V, P_GATES, P_Z, P_U, P_DT = 0, 1536, 3072, 4608, 6144, 9216, 10240, 11264
P_WIDTH = 12288
LANES = 128
NEG = -1e30
VMEM_LIMIT = 56 * 1024 * 1024

ADAM_LR, ADAM_B1, ADAM_B2, ADAM_EPS, ADAM_WD, ADAM_STEP = 0.001, 0.9, 0.999, 1e-08, 0.01, 10


def _alibi_slopes(n):
    def pow2(k):
        start = 2.0 ** (-8.0 / k)
        return [start ** (i + 1) for i in range(k)]
    if math.log2(n).is_integer():
        s = pow2(n)
    else:
        c = 2 ** math.floor(math.log2(n))
        s = pow2(c) + pow2(2 * c)[0::2][: n - c]
    return np.sort(np.asarray(s, np.float32))[::-1].copy()


SLOPES = _alibi_slopes(12).reshape(3, 4)


def _cp(*sem):
    return pltpu.CompilerParams(dimension_semantics=sem, vmem_limit_bytes=VMEM_LIMIT)


_DIMS = {"nn": (((1,), (0,)), ((), ())), "nt": (((1,), (1,)), ((), ())), "tn": (((0,), (0,)), ((), ()))}


def _dot(a, b, mode):
    return lax.dot_general(a.astype(BF16), b.astype(BF16), _DIMS[mode], preferred_element_type=F32)


@functools.partial(jax.custom_vjp, nondiff_argnums=(2,))
def _bdot(a, b, mode):
    return _dot(a, b, mode)


def _bdot_fwd(a, b, mode):
    return _dot(a, b, mode), (a, b)


def _bdot_bwd(mode, res, ct):
    a, b = res
    if mode == "nn":
        return _dot(ct, b, "nt"), _dot(a, ct, "tn")
    if mode == "nt":
        return _dot(ct, b, "nn"), _dot(ct, a, "tn")
    return _dot(b, ct, "nt"), _dot(a, ct, "nn")


_bdot.defvjp(_bdot_fwd, _bdot_bwd)


def _hdot(a, b):
    return jnp.dot(a, b, precision=lax.Precision.HIGHEST, preferred_element_type=F32)


def _tri(n, lower):
    r = lax.broadcasted_iota(jnp.int32, (n, n), 0)
    c = lax.broadcasted_iota(jnp.int32, (n, n), 1)
    return (r >= c if lower else r <= c).astype(F32)


@jax.custom_vjp
def _csum(a):
    return _hdot(_tri(a.shape[0], True), a)


def _csum_fwd(a):
    return _csum(a), None


def _csum_bwd(_, ct):
    return (_hdot(_tri(ct.shape[0], False), ct),)


_csum.defvjp(_csum_fwd, _csum_bwd)


def _softplus(x):
    return jnp.maximum(x, 0.0) + jnp.log(1.0 + jnp.exp(-jnp.abs(x)))


def _sigmoid(x):
    return 1.0 / (1.0 + jnp.exp(-x))


def _silu(x):
    return x * _sigmoid(x)


def _tile(n, cap):
    t = min(n, cap)
    while n % t:
        t //= 2
    return t


MM_TILE, MM_KTILE = 1024, 2048


def _matmul(name, a, b, mode, out_dtype=F32, precise=False, layer=None, chips=0, out_chips=0, into=None, second=None):
    if mode == "nn":
        (m, k), n = a.shape, (4 * chips if chips else b.shape[-1])
    elif mode == "nt":
        (m, k), n = a.shape, b.shape[-2]
    else:
        (k, m), n = a.shape, b.shape[-1]
    tm = _tile(m // N_CHIPS if (into is not None and not out_chips) else m, MM_TILE)
    tn = _tile(chips if (chips and mode == "nn") else (out_chips or n), MM_TILE)
    tk = _tile(chips if (chips and mode == "nt") else k, MM_KTILE)
    nk = k // tk
    a_spec = pl.BlockSpec((tk, tm), lambda i, j, l: (l, i)) if mode == "tn" else pl.BlockSpec((tm, tk), lambda i, j, l: (i, l))
    if chips:
        if mode == "nn":
            per = chips // tn
            b_spec = pl.BlockSpec((None, None, tk, tn), lambda i, j, l: (layer, j // per, l, j % per))
        else:
            per = chips // tk
            b_spec = pl.BlockSpec((None, None, tn, tk), lambda i, j, l: (layer, l // per, j, l % per))
    elif layer is not None:
        b_spec = (pl.BlockSpec((None, tn, tk), lambda i, j, l: (layer, j, l)) if mode == "nt"
                  else pl.BlockSpec((None, tk, tn), lambda i, j, l: (layer, l, j)))
    else:
        b_spec = pl.BlockSpec((tn, tk), lambda i, j, l: (j, l)) if mode == "nt" else pl.BlockSpec((tk, tn), lambda i, j, l: (l, j))
    if into is not None:
        buf, slot = into
        if out_chips:
            per_o = out_chips // tn
            o_spec = pl.BlockSpec((None, None, tm, tn), lambda i, j, l: (j // per_o, slot, i, j % per_o))
        else:
            per_r = m // N_CHIPS // tm
            o_spec = pl.BlockSpec((None, None, tm, tn), lambda i, j, l: (i // per_r, slot, i % per_r, j))
        o_shape = jax.ShapeDtypeStruct(buf.shape, buf.dtype)
    elif out_chips:
        per_o = out_chips // tn
        o_spec = pl.BlockSpec((None, tm, tn), lambda i, j, l: (j // per_o, i, j % per_o))
        o_shape = jax.ShapeDtypeStruct((N_CHIPS, m, out_chips), out_dtype)
    else:
        o_spec = pl.BlockSpec((tm, tn), lambda i, j, l: (i, j))
        o_shape = jax.ShapeDtypeStruct((m, n), out_dtype)

    def part(a_ref, b_ref):
        if precise:
            return lax.dot_general(a_ref[...], b_ref[...], _DIMS[mode], precision=lax.Precision.HIGHEST,
                                   preferred_element_type=F32)
        return _dot(a_ref[...], b_ref[...], mode)

    n_in = 2 if into is None else 3

    if second is not None:
        assert nk == 1 and into is None and not out_chips
        fn, dt2 = second

        def body2(a_ref, b_ref, o_ref, o2_ref):
            p = part(a_ref, b_ref)
            o_ref[...] = p.astype(o_ref.dtype)
            o2_ref[...] = fn(p).astype(o2_ref.dtype)

        return pl.pallas_call(
            body2, grid=(m // tm, n // tn, nk), in_specs=[a_spec, b_spec], out_specs=[o_spec, o_spec],
            out_shape=[o_shape, jax.ShapeDtypeStruct((m, n), dt2)],
            compiler_params=_cp("parallel", "parallel", "arbitrary"), name=name)(a, b)
    if nk == 1:
        def body(*refs):
            o_ref = refs[n_in]
            o_ref[...] = part(refs[0], refs[1]).astype(o_ref.dtype)
        scratch = []
    else:
        def body(*refs):
            o_ref, acc_ref = refs[n_in], refs[n_in + 1]
            l = pl.program_id(2)
            p = part(refs[0], refs[1])

            @pl.when(l == 0)
            def _():
                acc_ref[...] = p

            @pl.when((l > 0) & (l < nk - 1))
            def _():
                acc_ref[...] += p

            @pl.when(l == nk - 1)
            def _():
                o_ref[...] = (acc_ref[...] + p).astype(o_ref.dtype)
        scratch = [pltpu.VMEM((tm, tn), F32)]

    extra = {} if into is None else dict(input_output_aliases={2: 0})
    return pl.pallas_call(
        body, grid=(m // tm, n // tn, nk), in_specs=[a_spec, b_spec] + ([] if into is None else [pl.BlockSpec(memory_space=pl.ANY)]),
        out_specs=o_spec, out_shape=o_shape, scratch_shapes=scratch, compiler_params=_cp("parallel", "parallel", "arbitrary"),
        name=name, **extra)(*((a, b) if into is None else (a, b, into[0])))


def _group_matmul(name, a, w, mode, out_dtype=F32, layer=0):
    s = a.shape[0]
    tb = 512
    gw = POOL_GW
    if mode == "tn":
        def body(a_ref, b_ref, o_ref):
            part = _dot(a_ref[...], b_ref[...], "tn")

            @pl.when(pl.program_id(1) == 0)
            def _():
                o_ref[0] = part

            @pl.when(pl.program_id(1) > 0)
            def _():
                o_ref[0] += part

        return pl.pallas_call(
            body, grid=(4, s // tb),
            in_specs=[pl.BlockSpec((tb, gw), lambda g, i: (i, g)), pl.BlockSpec((tb, gw), lambda g, i: (i, g))],
            out_specs=pl.BlockSpec((1, gw, gw), lambda g, i: (g, 0, 0)),
            out_shape=jax.ShapeDtypeStruct((4, gw, gw), F32),
            compiler_params=_cp("parallel", "arbitrary"), name=name)(a, w)

    def body(a_ref, w_ref, o_ref):
        o_ref[...] = _dot(a_ref[...], w_ref[...], mode).astype(o_ref.dtype)

    return pl.pallas_call(
        body, grid=(s // tb, 4),
        in_specs=[pl.BlockSpec((tb, gw), lambda i, g: (i, g)), pl.BlockSpec((None, None, gw, gw), lambda i, g: (layer, g, 0, 0))],
        out_specs=pl.BlockSpec((tb, gw), lambda i, g: (i, g)),
        out_shape=jax.ShapeDtypeStruct((s, 4 * gw), out_dtype),
        compiler_params=_cp("parallel", "parallel"), name=name)(a, w)


def _rspec(tb, width, cb):
    return pl.BlockSpec((tb, width), lambda i: (i, cb))


def _pspec(shape):
    return pl.BlockSpec(shape, lambda i: (0, 0))


def _rowwise_fwd(name, f, rows, pars, outs, tb=256):
    s = rows[0][0].shape[0]
    nin = len(rows) + len(pars)

    def body(*refs):
        res = f(*[r[...].astype(F32) for r in refs[:nin]])
        for o, v in zip(refs[nin:], res):
            o[...] = v.astype(o.dtype)

    return pl.pallas_call(
        body, grid=(s // tb,),
        in_specs=[_rspec(tb, w, cb) for _, w, cb in rows] + [_pspec(p.shape) for p in pars],
        out_specs=[_rspec(tb, w, 0) for w, _ in outs],
        out_shape=[jax.ShapeDtypeStruct((s, w), dt) for w, dt in outs],
        compiler_params=_cp("parallel"), name=name)(*[r[0] for r in rows], *pars)


def _rowwise_bwd(name, f, rows, pars, cts, need, add=None, tb=256, gdt=None):
    s = rows[0][0].shape[0]
    nr, npar, nc = len(rows), len(pars), len(cts)
    nin = nr + npar + nc + (1 if add is not None else 0)

    def body(*refs):
        ins = [r[...].astype(F32) for r in refs[:nr + npar]]
        _, vjp = jax.vjp(f, *ins)
        g = vjp(tuple(c[...].astype(F32) for c in refs[nr + npar:nr + npar + nc]))
        outs = refs[nin:]
        k = 0
        for j in range(nr):
            if need[j]:
                v = g[j]
                if add is not None and add[0] == j:
                    v = v + refs[nin - 1][...]
                outs[k][...] = v.astype(outs[k].dtype)
                k += 1
        first = pl.program_id(0) == 0
        for j in range(npar):
            o, v = outs[k + j], g[nr + j]

            @pl.when(first)
            def _(o=o, v=v):
                o[...] = v

            @pl.when(jnp.logical_not(first))
            def _(o=o, v=v):
                o[...] += v

    in_specs = ([_rspec(tb, w, cb) for _, w, cb in rows] + [_pspec(p.shape) for p in pars]
                + [_rspec(tb, w, cb) for _, w, cb in cts])
    args = [r[0] for r in rows] + list(pars) + [c[0] for c in cts]
    if add is not None:
        in_specs.append(_rspec(tb, rows[add[0]][1], 0))
        args.append(add[1])
    gr = [(w, F32) for (_, w, _), nd in zip(rows, need) if nd]
    if gdt is not None:
        gr = [(w, dt) for (w, _), dt in zip(gr, gdt)]
    return pl.pallas_call(
        body, grid=(s // tb,), in_specs=in_specs,
        out_specs=[_rspec(tb, w, 0) for w, _ in gr] + [_pspec(p.shape) for p in pars],
        out_shape=[jax.ShapeDtypeStruct((s, w), dt) for w, dt in gr] + [jax.ShapeDtypeStruct(p.shape, F32) for p in pars],
        compiler_params=_cp("arbitrary"), name=name)(*args)


def _f_norm(x, nw, sc, sh):
    r = lax.rsqrt(jnp.mean(x * x, axis=-1, keepdims=True) + EPS)
    return ((x * r * nw) * (1.0 + sc) + sh,)


def _f_ssdgate(y, z, w):
    y2 = y * _silu(z)
    low = lax.broadcasted_iota(jnp.int32, y2.shape, 1) < 512
    sq = y2 * y2
    m0 = jnp.sum(jnp.where(low, sq, 0.0), axis=-1, keepdims=True) / 512.0
    m1 = jnp.sum(jnp.where(low, 0.0, sq), axis=-1, keepdims=True) / 512.0
    r = jnp.where(low, lax.rsqrt(m0 + EPS), lax.rsqrt(m1 + EPS))
    return (y2 * r * w,)


def _head_rms(t, w):
    outs = []
    for h in range(t.shape[1] // ATTN_HEAD_DIM):
        th = t[:, h * ATTN_HEAD_DIM:(h + 1) * ATTN_HEAD_DIM]
        outs.append(th * lax.rsqrt(jnp.mean(th * th, axis=-1, keepdims=True) + EPS) * w)
    return jnp.concatenate(outs, axis=1)


def _f_qknorm(q, k, qw, kw):
    return _head_rms(q, qw), _head_rms(k, kw)


def _f_combine(o1, o2, o3, l1, l2, l3):
    m = lax.stop_gradient(jnp.maximum(jnp.maximum(l1, l2), l3))
    e1, e2, e3 = jnp.exp(l1 - m), jnp.exp(l2 - m), jnp.exp(l3 - m)
    return ((e1 * o1 + e2 * o2 + e3 * o3) / (e1 + e2 + e3),)


def _f_poolscale(pm, ps):
    return (pm * ps,)


def _f_merge(gates, ys, ya, yp):
    g = _sigmoid(gates)
    return (g[:, 0:1024] * ys + g[:, 1024:2048] * ya + g[:, 2048:3072] * yp,)


def _f_resid(x, o, g):
    return (x + g * o,)


def _f_relu2(a):
    return (jnp.square(jnp.maximum(a, 0.0)),)


def _loss_and_grad(y, tgt, tb=512):
    s, d = y.shape

    def body(y_ref, t_ref, dy_ref, l_ref):
        e = y_ref[...] - t_ref[...]
        dy_ref[...] = e * (1.0 / d)
        part = jnp.zeros((1, LANES), F32) + jnp.sum(e * e) * (0.5 / d)

        @pl.when(pl.program_id(0) == 0)
        def _():
            l_ref[...] = part

        @pl.when(pl.program_id(0) > 0)
        def _():
            l_ref[...] += part

    return pl.pallas_call(
        body, grid=(s // tb,), in_specs=[_rspec(tb, d, 0), _rspec(tb, d, 0)],
        out_specs=[_rspec(tb, d, 0), _pspec((1, LANES))],
        out_shape=[jax.ShapeDtypeStruct((s, d), F32), jax.ShapeDtypeStruct((1, LANES), F32)],
        compiler_params=_cp("arbitrary"), name="loss")(y, tgt)


def _shift_down(x, j):
    rows = lax.broadcasted_iota(jnp.int32, x.shape, 0)
    return jnp.where(rows < j, 0.0, pltpu.roll(x, j, 0))


def _shift_up(x, j):
    s = x.shape[0]
    rows = lax.broadcasted_iota(jnp.int32, x.shape, 0)
    return jnp.where(rows >= s - j, 0.0, pltpu.roll(x, s - j, 0))


CONV_CB = 256


def _conv_pre(x, w_ref, b_ref):
    acc = b_ref[...] + w_ref[SSD_CONV - 1:SSD_CONV, :] * x
    for j in range(1, SSD_CONV):
        acc = acc + w_ref[SSD_CONV - 1 - j:SSD_CONV - j, :] * _shift_down(x, j)
    return acc


def _conv_fwd(proj, cw, cb):
    s = proj.shape[0]

    def body(x_ref, w_ref, b_ref, o_ref):
        o_ref[...] = _silu(_conv_pre(x_ref[...], w_ref, b_ref))

    return pl.pallas_call(
        body, grid=(CONV_DIM // CONV_CB,),
        in_specs=[pl.BlockSpec((s, CONV_CB), lambda i: (0, P_XBC // CONV_CB + i)),
                  pl.BlockSpec((SSD_CONV, CONV_CB), lambda i: (0, i)), pl.BlockSpec((1, CONV_CB), lambda i: (0, i))],
        out_specs=pl.BlockSpec((s, CONV_CB), lambda i: (0, i)),
        out_shape=jax.ShapeDtypeStruct((s, CONV_DIM), F32), compiler_params=_cp("parallel"), name="conv_fwd")(proj, cw, cb)


def _conv_bwd(proj, cw, cb, dout):
    s = proj.shape[0]

    def body(x_ref, w_ref, b_ref, d_ref, dx_ref, dw_ref, db_ref):
        x = x_ref[...]
        a = _conv_pre(x, w_ref, b_ref)
        sg = _sigmoid(a)
        da = d_ref[...] * (sg + a * sg * (1.0 - sg))
        db_ref[...] = jnp.sum(da, axis=0, keepdims=True)
        dx = w_ref[SSD_CONV - 1:SSD_CONV, :] * da
        dw_ref[SSD_CONV - 1:SSD_CONV, :] = jnp.sum(da * x, axis=0, keepdims=True)
        for j in range(1, SSD_CONV):
            dx = dx + w_ref[SSD_CONV - 1 - j:SSD_CONV - j, :] * _shift_up(da, j)
            dw_ref[SSD_CONV - 1 - j:SSD_CONV - j, :] = jnp.sum(da * _shift_down(x, j), axis=0, keepdims=True)
        dx_ref[...] = dx.astype(dx_ref.dtype)

    return pl.pallas_call(
        body, grid=(CONV_DIM // CONV_CB,),
        in_specs=[pl.BlockSpec((s, CONV_CB), lambda i: (0, P_XBC // CONV_CB + i)),
                  pl.BlockSpec((SSD_CONV, CONV_CB), lambda i: (0, i)), pl.BlockSpec((1, CONV_CB), lambda i: (0, i)),
                  pl.BlockSpec((s, CONV_CB), lambda i: (0, i))],
        out_specs=[pl.BlockSpec((s, CONV_CB), lambda i: (0, i)), pl.BlockSpec((SSD_CONV, CONV_CB), lambda i: (0, i)),
                   pl.BlockSpec((1, CONV_CB), lambda i: (0, i))],
        out_shape=[jax.ShapeDtypeStruct((s, CONV_DIM), BF16), jax.ShapeDtypeStruct((SSD_CONV, CONV_DIM), F32),
                   jax.ShapeDtypeStruct((1, CONV_DIM), F32)],
        compiler_params=_cp("parallel"), name="conv_bwd")(proj, cw, cb, dout)


def _pool_window_sum(x, g, shift):
    s2 = x + shift(x, 1)
    s4 = s2 + shift(s2, 2)
    s8 = s4 + shift(s4, 4)
    s16 = s8 + shift(s8, 8)
    return jnp.where(g == 0, s2, jnp.where(g == 1, s4, jnp.where(g == 2, s8, s16)))


def _pool_count(shape, g):
    rows = lax.broadcasted_iota(jnp.int32, shape, 0)
    return jnp.minimum(rows + 1, jnp.left_shift(2, g)).astype(F32)


def _pool_fwd(proj):
    s = proj.shape[0]

    def body(u_ref, o_ref):
        g = pl.program_id(0)
        u = u_ref[...]
        o_ref[...] = (_pool_window_sum(u, g, _shift_down) / _pool_count(u.shape, g) - u).astype(o_ref.dtype)

    return pl.pallas_call(
        body, grid=(4,), in_specs=[pl.BlockSpec((s, POOL_GW), lambda g: (0, P_U // POOL_GW + g))],
        out_specs=pl.BlockSpec((s, POOL_GW), lambda g: (0, g)),
        out_shape=jax.ShapeDtypeStruct((s, 4 * POOL_GW), BF16), compiler_params=_cp("parallel"), name="pool_fwd")(proj)


def _pool_bwd(dp):
    s = dp.shape[0]

    def body(d_ref, o_ref):
        g = pl.program_id(0)
        d = d_ref[...]
        o_ref[...] = (_pool_window_sum(d / _pool_count(d.shape, g), g, _shift_up) - d).astype(o_ref.dtype)

    return pl.pallas_call(
        body, grid=(4,), in_specs=[pl.BlockSpec((s, POOL_GW), lambda g: (0, g))],
        out_specs=pl.BlockSpec((s, POOL_GW), lambda g: (0, g)),
        out_shape=jax.ShapeDtypeStruct((s, 4 * POOL_GW), BF16), compiler_params=_cp("parallel"), name="pool_bwd")(dp)


N_PAIRS = SSD_HEADS // 2
STATE_ROWS = N_PAIRS * SSD_STATE


def _ssd_chunk(xbc, dtr, hprev, dtb, alog, dsk):
    L = xbc.shape[0]
    xs, bm, cm = xbc[:, 0:1024], xbc[:, 1024:1280], xbc[:, 1280:1536]
    dt = _softplus(dtr + dtb)
    a = dt * (-jnp.exp(alog))
    acum = _csum(a)
    alast = jnp.sum(a, axis=0, keepdims=True)
    xdt = xs * dt
    xdecay = xdt * jnp.exp(alast - acum)
    eacum = jnp.exp(acum)
    elast = jnp.exp(alast)
    cb = [_bdot(cm[:, g * 128:(g + 1) * 128], bm[:, g * 128:(g + 1) * 128], "nt") for g in range(2)]
    rows = lax.broadcasted_iota(jnp.int32, (L, L), 0)
    cols = lax.broadcasted_iota(jnp.int32, (L, L), 1)
    causal = rows >= cols
    lane = lax.broadcasted_iota(jnp.int32, (L, LANES), 1)
    sub = lax.broadcasted_iota(jnp.int32, (LANES, L), 0)
    ys, hs = [], []
    for p in range(N_PAIRS):
        g = p // (N_PAIRS // 2)
        sl = slice(p * LANES, (p + 1) * LANES)
        ac = acum[:, sl]
        act = ac.T
        xp = xdt[:, sl]
        hp = hprev[p * SSD_STATE:(p + 1) * SSD_STATE, :]
        y = _bdot(cm[:, g * 128:(g + 1) * 128], hp, "nn") * eacum[:, sl] + dsk[:, sl] * xs[:, sl]
        for half in range(2):
            l0 = half * SSD_HEAD_DIM
            col = jnp.sum(jnp.where(lane == l0, ac, 0.0), axis=1, keepdims=True)
            row = jnp.sum(jnp.where(sub == l0, act, 0.0), axis=0, keepdims=True)
            decay = jnp.exp(jnp.where(causal, col - row, NEG))
            xh = jnp.where((lane >= l0) & (lane < l0 + SSD_HEAD_DIM), xp, 0.0)
            y = y + _bdot(cb[g] * decay, xh, "nn")
        ys.append(y)
        hs.append(elast[:, sl] * hp + _bdot(bm[:, g * 128:(g + 1) * 128], xdecay[:, sl], "tn"))
    return tuple(ys), tuple(hs)


def _ssd_fwd(xbc, proj, dtb, alog, dsk):
    s = xbc.shape[0]
    nc = s // SSD_CHUNK

    def body(x_ref, dt_ref, b_ref, a_ref, d_ref, y_ref, hist_ref, h_ref):
        @pl.when(pl.program_id(0) == 0)
        def _():
            h_ref[...] = jnp.zeros_like(h_ref)

        hprev = h_ref[...]
        hist_ref[...] = hprev
        ys, hs = _ssd_chunk(x_ref[...], dt_ref[...], hprev, b_ref[...], a_ref[...], d_ref[...])
        for p in range(N_PAIRS):
            y_ref[:, p * LANES:(p + 1) * LANES] = ys[p]
            h_ref[p * SSD_STATE:(p + 1) * SSD_STATE, :] = hs[p]

    return pl.pallas_call(
        body, grid=(nc,),
        in_specs=[pl.BlockSpec((SSD_CHUNK, CONV_DIM), lambda i: (i, 0)),
                  pl.BlockSpec((SSD_CHUNK, 1024), lambda i: (i, P_DT // 1024)),
                  _pspec((1, 1024)), _pspec((1, 1024)), _pspec((1, 1024))],
        out_specs=[pl.BlockSpec((SSD_CHUNK, 1024), lambda i: (i, 0)), pl.BlockSpec((STATE_ROWS, LANES), lambda i: (i, 0))],
        out_shape=[jax.ShapeDtypeStruct((s, 1024), F32), jax.ShapeDtypeStruct((nc * STATE_ROWS, LANES), F32)],
        scratch_shapes=[pltpu.VMEM((STATE_ROWS, LANES), F32)],
        compiler_params=_cp("arbitrary"), name="ssd_fwd")(xbc, proj, dtb, alog, dsk)


def _ssd_bwd(xbc, proj, hist, dtb, alog, dsk, dy):
    s = xbc.shape[0]
    nc = s // SSD_CHUNK

    def body(x_ref, dt_ref, hist_ref, b_ref, a_ref, d_ref, dy_ref, dx_ref, ddt_ref, db_ref, da_ref, dd_ref, dh_ref):
        first = pl.program_id(0) == 0

        @pl.when(first)
        def _():
            dh_ref[...] = jnp.zeros_like(dh_ref)

        _, vjp = jax.vjp(_ssd_chunk, x_ref[...], dt_ref[...], hist_ref[...], b_ref[...], a_ref[...], d_ref[...])
        dys = tuple(dy_ref[:, p * LANES:(p + 1) * LANES] for p in range(N_PAIRS))
        dhs = tuple(dh_ref[p * SSD_STATE:(p + 1) * SSD_STATE, :] for p in range(N_PAIRS))
        dx, ddt, dhp, db, da, dd = vjp((dys, dhs))
        dx_ref[...] = dx
        ddt_ref[...] = ddt.astype(ddt_ref.dtype)
        dh_ref[...] = dhp
        for o, v in ((db_ref, db), (da_ref, da), (dd_ref, dd)):
            @pl.when(first)
            def _(o=o, v=v):
                o[...] = v

            @pl.when(jnp.logical_not(first))
            def _(o=o, v=v):
                o[...] += v

    rev = lambda i: (nc - 1 - i, 0)
    return pl.pallas_call(
        body, grid=(nc,),
        in_specs=[pl.BlockSpec((SSD_CHUNK, CONV_DIM), rev),
                  pl.BlockSpec((SSD_CHUNK, 1024), lambda i: (nc - 1 - i, P_DT // 1024)),
                  pl.BlockSpec((STATE_ROWS, LANES), rev),
                  _pspec((1, 1024)), _pspec((1, 1024)), _pspec((1, 1024)),
                  pl.BlockSpec((SSD_CHUNK, 1024), rev)],
        out_specs=[pl.BlockSpec((SSD_CHUNK, CONV_DIM), rev), pl.BlockSpec((SSD_CHUNK, 1024), rev),
                   _pspec((1, 1024)), _pspec((1, 1024)), _pspec((1, 1024))],
        out_shape=[jax.ShapeDtypeStruct((s, CONV_DIM), F32), jax.ShapeDtypeStruct((s, 1024), BF16)]
        + [jax.ShapeDtypeStruct((1, 1024), F32)] * 3,
        scratch_shapes=[pltpu.VMEM((STATE_ROWS, LANES), F32)],
        compiler_params=_cp("arbitrary"), name="ssd_bwd")(xbc, proj, hist, dtb, alog, dsk, dy)


def _attn_head(q, kp, kc, vp, vc, has_prev, slope):
    scale = ATTN_HEAD_DIM ** -0.5
    n = ATTN_STEPS
    qi = lax.broadcasted_iota(jnp.int32, (n, n), 0)
    kj = lax.broadcasted_iota(jnp.int32, (n, n), 1)
    sp = jnp.where((kj >= qi) & has_prev, _bdot(q, kp, "nt") * scale - slope * (qi + n - kj).astype(F32), NEG)
    sc = jnp.where(kj <= qi, _bdot(q, kc, "nt") * scale - slope * (qi - kj).astype(F32), NEG)
    m = lax.stop_gradient(jnp.maximum(jnp.max(sp, axis=1, keepdims=True), jnp.max(sc, axis=1, keepdims=True)))
    pp, pc = jnp.exp(sp - m), jnp.exp(sc - m)
    den = jnp.sum(pp, axis=1, keepdims=True) + jnp.sum(pc, axis=1, keepdims=True)
    o = (_bdot(pp, vp, "nn") + _bdot(pc, vc, "nn")) / den
    return o, jnp.broadcast_to(m + jnp.log(den), (n, ATTN_HEAD_DIM))


def _head_slope(gi, h):
    s = [float(v) * DILATIONS[gi] for v in SLOPES[gi]]
    return jnp.where(h == 0, s[0], jnp.where(h == 1, s[1], jnp.where(h == 2, s[2], s[3])))


ATTN_UNROLL = 4


def _attn_heads_per_block(d):
    return 4 if d == 1 else 1


def _unit(ref, u, d):
    if d == 1:
        return ref.at[:, u * ATTN_HEAD_DIM:(u + 1) * ATTN_HEAD_DIM]
    return ref.at[pl.ds(u, ATTN_STEPS, stride=d), :]


def _for_units(units, step):
    if units == ATTN_UNROLL:
        for u in range(units):
            step(u)
        return

    def body(i, carry):
        for j in range(ATTN_UNROLL):
            step(i * ATTN_UNROLL + j)
        return carry

    lax.fori_loop(0, units // ATTN_UNROLL, body, 0)


def _attn_fwd(qn, kn, proj, gi):
    d = DILATIONS[gi]
    s = qn.shape[0]
    span = ATTN_STEPS * d
    nb = s // span

    hb, units = _attn_heads_per_block(d), _attn_heads_per_block(d) * d

    def body(q_ref, kp_ref, kc_ref, vp_ref, vc_ref, o_ref, l_ref):
        h0, b = pl.program_id(0) * hb, pl.program_id(1)

        def step(u):
            o, l = _attn_head(_unit(q_ref, u, d)[...], _unit(kp_ref, u, d)[...], _unit(kc_ref, u, d)[...],
                              _unit(vp_ref, u, d)[...], _unit(vc_ref, u, d)[...], b > 0, _head_slope(gi, h0 + u // d))
            _unit(o_ref, u, d)[...] = o
            _unit(l_ref, u, d)[...] = l

        _for_units(units, step)

    w = hb * ATTN_HEAD_DIM
    cur = pl.BlockSpec((span, w), lambda h, b: (b, (gi * 4) // hb + h))
    prev = pl.BlockSpec((span, w), lambda h, b: (jnp.maximum(b - 1, 0), (gi * 4) // hb + h))
    vcol = (P_V // ATTN_HEAD_DIM + gi * 4) // hb
    cur_v = pl.BlockSpec((span, w), lambda h, b: (b, vcol + h))
    prev_v = pl.BlockSpec((span, w), lambda h, b: (jnp.maximum(b - 1, 0), vcol + h))
    out = pl.BlockSpec((span, w), lambda h, b: (b, h))
    return pl.pallas_call(
        body, grid=(4 // hb, nb), in_specs=[cur, prev, cur, prev_v, cur_v], out_specs=[out, out],
        out_shape=[jax.ShapeDtypeStruct((s, ATTN_GROUP_W), F32)] * 2,
        compiler_params=_cp("parallel", "parallel"), name=f"attn_fwd_g{gi}")(qn, kn, kn, proj, proj)


def _attn_bwd(qn, kn, proj, do, dl, gi):
    d = DILATIONS[gi]
    s = qn.shape[0]
    span = ATTN_STEPS * d
    nb = s // span

    hb, units = _attn_heads_per_block(d), _attn_heads_per_block(d) * d

    def body(q_ref, kp_ref, kc_ref, vp_ref, vc_ref, do_ref, dl_ref, dq_ref, dk_ref, dv_ref, ck, cv):
        h0, bi = pl.program_id(0) * hb, pl.program_id(1)

        @pl.when(bi == 0)
        def _():
            ck[...] = jnp.zeros_like(ck)
            cv[...] = jnp.zeros_like(cv)

        has_prev = bi < nb - 1

        def step(u):
            f = functools.partial(_attn_head, has_prev=has_prev, slope=_head_slope(gi, h0 + u // d))
            _, vjp = jax.vjp(f, *(_unit(r, u, d)[...] for r in (q_ref, kp_ref, kc_ref, vp_ref, vc_ref)))
            dq, dkp, dkc, dvp, dvc = vjp((_unit(do_ref, u, d)[...], _unit(dl_ref, u, d)[...]))
            _unit(dq_ref, u, d)[...] = dq
            _unit(dk_ref, u, d)[...] = dkc + ck[u]
            _unit(dv_ref, u, d)[...] = dvc + cv[u]
            ck[u] = dkp
            cv[u] = dvp

        _for_units(units, step)

    w = hb * ATTN_HEAD_DIM
    cur = pl.BlockSpec((span, w), lambda h, b: (nb - 1 - b, (gi * 4) // hb + h))
    prev = pl.BlockSpec((span, w), lambda h, b: (jnp.maximum(nb - 2 - b, 0), (gi * 4) // hb + h))
    vcol = (P_V // ATTN_HEAD_DIM + gi * 4) // hb
    cur_v = pl.BlockSpec((span, w), lambda h, b: (nb - 1 - b, vcol + h))
    prev_v = pl.BlockSpec((span, w), lambda h, b: (jnp.maximum(nb - 2 - b, 0), vcol + h))
    out = pl.BlockSpec((span, w), lambda h, b: (nb - 1 - b, h))
    res = (units, ATTN_STEPS, ATTN_HEAD_DIM)
    return pl.pallas_call(
        body, grid=(4 // hb, nb), in_specs=[cur, prev, cur, prev_v, cur_v, out, out], out_specs=[out, out, out],
        out_shape=[jax.ShapeDtypeStruct((s, ATTN_GROUP_W), F32)] * 3,
        scratch_shapes=[pltpu.VMEM(res, F32), pltpu.VMEM(res, F32)],
        compiler_params=_cp("parallel", "arbitrary"), name=f"attn_bwd_g{gi}")(qn, kn, kn, proj, proj, do, dl)


def _layer_fwd(x, mod, W, l, late=None):
    sh1, sc1, g1, sh2, sc2, g2 = (mod[i:i + 1] for i in range(6))
    (h,) = _rowwise_fwd("norm1", _f_norm, [(x, 1024, 0)], [W["norm1_w"], sc1, sh1], [(1024, BF16)])
    proj = _matmul("in_proj", h, W["w_in"], "nt")
    xbc = _conv_fwd(proj, W["conv_w"], W["conv_b"])
    y, hist = _ssd_fwd(xbc, proj, W["dt_bias"], W["a_log"], W["d_skip"])
    if late is not None:
        W.update(late(y))
    (yn,) = _rowwise_fwd("ssd_gate", _f_ssdgate, [(y, 1024, 0), (proj, 1024, P_Z // 1024)], [W["ssd_norm_w"]], [(1024, BF16)])
    y_ssd = _matmul("ssd_out", yn, W["w_ssd_out"], "nn", layer=l)
    qn, kn = _rowwise_fwd("qk_norm", _f_qknorm, [(proj, 1536, P_Q // 1536), (proj, 1536, P_K // 1536)],
                          [W["q_norm_w"], W["k_norm_w"]], [(1536, F32)] * 2)
    ol = [_attn_fwd(qn, kn, proj, gi) for gi in range(3)]
    (o,) = _rowwise_fwd("attn_combine", _f_combine, [(t[0], 512, 0) for t in ol] + [(t[1], 512, 0) for t in ol], [], [(512, BF16)])
    y_attn = _matmul("attn_out", o, W["w_attn_out"], "nn", layer=l, chips=256)
    pooled = _pool_fwd(proj)
    pm = _group_matmul("pool_mix", pooled, W["w_pool_mix"], "nn", layer=l)
    (ps,) = _rowwise_fwd("pool_scale", _f_poolscale, [(pm, 1024, 0)], [W["pool_scale"]], [(1024, BF16)])
    y_pool = _matmul("pool_out", ps, W["w_pool_out"], "nn", layer=l)
    (merged,) = _rowwise_fwd("merge", _f_merge, [(proj, 3072, P_GATES // 3072), (y_ssd, 1024, 0), (y_attn, 1024, 0), (y_pool, 1024, 0)],
                             [], [(1024, BF16)])
    mo = _matmul("mix_out", merged, W["w_out"], "nn", layer=l)
    (x1,) = _rowwise_fwd("resid1", _f_resid, [(x, 1024, 0), (mo, 1024, 0)], [g1], [(1024, F32)])
    (h2,) = _rowwise_fwd("norm2", _f_norm, [(x1, 1024, 0)], [W["norm2_w"], sc2, sh2], [(1024, BF16)])
    a, r = _matmul("ff1", h2, W["w_ff1"], "nn", layer=l, chips=1024, second=(lambda t: _f_relu2(t)[0], BF16))
    ff = _matmul("ff2", r, W["w_ff2"], "nn", layer=l)
    (x2,) = _rowwise_fwd("resid2", _f_resid, [(x1, 1024, 0), (ff, 1024, 0)], [g2], [(1024, F32)])
    saved = dict(x=x, h=h, proj=proj, xbc=xbc, y=y, hist=hist, yn=yn, y_ssd=y_ssd, qn=qn, kn=kn, ol=ol, o=o,
                 y_attn=y_attn, pooled=pooled, pm=pm, ps=ps, y_pool=y_pool, merged=merged, mo=mo, x1=x1, h2=h2, a=a, r=r, ff=ff)
    return x2, saved


def _layer_bwd(dx2, mod, W, sv, l, bufs):
    sh1, sc1, g1, sh2, sc2, g2 = (mod[i:i + 1] for i in range(6))
    g = {}
    dx1a, dff, dg2 = _rowwise_bwd("resid2_bwd", _f_resid, [(sv["x1"], 1024, 0), (sv["ff"], 1024, 0)], [g2], [(dx2, 1024, 0)],
                                  [True, True], gdt=[F32, BF16])
    g["w_ff2"] = _matmul("ff2_dw", sv["r"], dff, "tn", BF16, into=(bufs["w_ff2"], l))
    dr = _matmul("ff2_dx", dff, W["w_ff2"], "nt", layer=l)
    (da,) = _rowwise_bwd("relu2_bwd", _f_relu2, [(sv["a"], D_FF, 0)], [], [(dr, D_FF, 0)], [True], tb=128, gdt=[BF16])
    g["w_ff1"] = _matmul("ff1_dw", sv["h2"], da, "tn", BF16, out_chips=1024, into=(bufs["w_ff1"], l))
    dh2 = _matmul("ff1_dx", da, W["w_ff1"], "nt", layer=l, chips=1024)
    dx1, g["norm2_w"], dsc2, dsh2 = _rowwise_bwd("norm2_bwd", _f_norm, [(sv["x1"], 1024, 0)], [W["norm2_w"], sc2, sh2],
                                                 [(dh2, 1024, 0)], [True], add=(0, dx1a))
    dxa, dmo, dg1 = _rowwise_bwd("resid1_bwd", _f_resid, [(sv["x"], 1024, 0), (sv["mo"], 1024, 0)], [g1], [(dx1, 1024, 0)],
                                 [True, True], gdt=[F32, BF16])
    g["w_out"] = _matmul("mix_out_dw", sv["merged"], dmo, "tn", BF16, into=(bufs["w_out"], l))
    dmerged = _matmul("mix_out_dx", dmo, W["w_out"], "nt", layer=l)
    proj = sv["proj"]
    dgates, dy_ssd, dy_attn, dy_pool = _rowwise_bwd(
        "merge_bwd", _f_merge, [(proj, 3072, P_GATES // 3072), (sv["y_ssd"], 1024, 0), (sv["y_attn"], 1024, 0), (sv["y_pool"], 1024, 0)],
        [], [(dmerged, 1024, 0)], [True] * 4, gdt=[BF16] * 4)
    g["w_pool_out"] = _matmul("pool_out_dw", sv["ps"], dy_pool, "tn", BF16, into=(bufs["w_pool_out"], l))
    dps = _matmul("pool_out_dx", dy_pool, W["w_pool_out"], "nt", layer=l)
    dpm, g["pool_scale"] = _rowwise_bwd("pool_scale_bwd", _f_poolscale, [(sv["pm"], 1024, 0)], [W["pool_scale"]], [(dps, 1024, 0)],
                                        [True], gdt=[BF16])
    dmix = _group_matmul("pool_mix_dw", sv["pooled"], dpm, "tn")
    g["w_pool_mix"] = bufs["w_pool_mix"].at[:, l].set(
        jnp.moveaxis(dmix.reshape(4, N_CHIPS, POOL_GW // N_CHIPS, POOL_GW), 1, 0).astype(BF16))
    dpooled = _group_matmul("pool_mix_dx", dpm, W["w_pool_mix"], "nt", layer=l)
    du = _pool_bwd(dpooled)
    g["w_attn_out"] = _matmul("attn_out_dw", sv["o"], dy_attn, "tn", BF16, out_chips=256, into=(bufs["w_attn_out"], l))
    do = _matmul("attn_out_dx", dy_attn, W["w_attn_out"], "nt", layer=l, chips=256)
    ol = sv["ol"]
    dol = _rowwise_bwd("attn_combine_bwd", _f_combine, [(t[0], 512, 0) for t in ol] + [(t[1], 512, 0) for t in ol], [],
                       [(do, 512, 0)], [True] * 6)
    dqs, dks, dvs = zip(*[_attn_bwd(sv["qn"], sv["kn"], proj, dol[gi], dol[3 + gi], gi) for gi in range(3)])
    dqn, dkn, dv = (jnp.concatenate(t, axis=1) for t in (dqs, dks, dvs))
    dv = dv.astype(BF16)
    dq, dk, g["q_norm_w"], g["k_norm_w"] = _rowwise_bwd(
        "qk_norm_bwd", _f_qknorm, [(proj, 1536, P_Q // 1536), (proj, 1536, P_K // 1536)],
        [W["q_norm_w"], W["k_norm_w"]], [(dqn, 1536, 0), (dkn, 1536, 0)], [True] * 2, gdt=[BF16] * 2)
    g["w_ssd_out"] = _matmul("ssd_out_dw", sv["yn"], dy_ssd, "tn", BF16, into=(bufs["w_ssd_out"], l))
    dyn = _matmul("ssd_out_dx", dy_ssd, W["w_ssd_out"], "nt", layer=l)
    dy, dz, g["ssd_norm_w"] = _rowwise_bwd("ssd_gate_bwd", _f_ssdgate, [(sv["y"], 1024, 0), (proj, 1024, P_Z // 1024)], [W["ssd_norm_w"]],
                                           [(dyn, 1024, 0)], [True, True], gdt=[F32, BF16])
    dxbc, ddt, g["dt_bias"], g["a_log"], g["d_skip"] = _ssd_bwd(sv["xbc"], proj, sv["hist"], W["dt_bias"], W["a_log"], W["d_skip"], dy)
    dxbc_raw, g["conv_w"], g["conv_b"] = _conv_bwd(proj, W["conv_w"], W["conv_b"], dxbc)
    dproj = jnp.concatenate([dxbc_raw, dq, dk, dv, dgates, dz, du, ddt], axis=1)
    g["w_in"] = _matmul("in_proj_dw", dproj, sv["h"], "tn", BF16)
    dh = _matmul("in_proj_dx", dproj, W["w_in"], "nn")
    dx, g["norm1_w"], dsc1, dsh1 = _rowwise_bwd("norm1_bwd", _f_norm, [(sv["x"], 1024, 0)], [W["norm1_w"], sc1, sh1],
                                                [(dh, 1024, 0)], [True], add=(0, dxa))
    dmod = jnp.concatenate([dsh1, dsc1, dg1, dsh2, dsc2, dg2], axis=0)
    return dx, dmod, g


def _expand_heads(t):
    return jnp.repeat(t, SSD_HEAD_DIM, axis=-1)


def _reduce_heads(t):
    return t.reshape(t.shape[:-1] + (SSD_HEADS, SSD_HEAD_DIM)).sum(-1)


_IN_SPLITS = np.cumsum((0,) + IN_SIZES)


def _w_in_to_layout(wt):
    z, xbc, dt, q, k, v, u, gates = (wt[_IN_SPLITS[i]:_IN_SPLITS[i + 1]] for i in range(8))
    return jnp.concatenate([xbc, q, k, v, gates, z, u, jnp.repeat(dt, SSD_HEAD_DIM, axis=0)], axis=0)


def _w_in_from_layout(g):
    xbc, q, k, v = (g[o:o + 1536] for o in (P_XBC, P_Q, P_K, P_V))
    gates, z, u, dt = g[P_GATES:P_GATES + 3072], g[P_Z:P_Z + 1024], g[P_U:P_U + 1024], g[P_DT:P_DT + 1024]
    dt = dt.astype(F32).reshape(SSD_HEADS, SSD_HEAD_DIM, D_MODEL).sum(1).astype(g.dtype)
    return jnp.concatenate([z, xbc, dt, q, k, v, u, gates], axis=0)


_STACKED = ("w_ssd_out", "w_attn_out", "w_pool_mix", "w_pool_out", "w_out", "w_ff1", "w_ff2")
_ROWS = ("norm1_w", "norm2_w", "conv_b", "ssd_norm_w", "q_norm_w", "k_norm_w", "pool_scale")
_HEAD_ROWS = ("dt_bias", "a_log", "d_skip")


def _layer_weights(wg, lg, small, l):
    W = {k: wg[k] for k in _STACKED if k in wg}
    W["w_in"] = _w_in_to_layout(wg["w_in"][lg].reshape(IN_WIDTH, D_MODEL))
    W["conv_w"] = small["conv_w"][l]
    for k in _ROWS:
        W[k] = small[k][l][None, :]
    for k in _HEAD_ROWS:
        W[k] = _expand_heads(small[k][l])[None, :]
    return W


def _layer_grads_by_chip(g, w_in_buf, l):
    out = dict(g)
    out["w_in"] = w_in_buf.at[:, l].set(_w_in_from_layout(g["w_in"]).reshape(N_CHIPS, IN_WIDTH // N_CHIPS, D_MODEL))
    for k in _ROWS:
        out[k] = g[k][0]
    for k in _HEAD_ROWS:
        out[k] = _reduce_heads(g[k][0])
    return out


ANY = pl.BlockSpec(memory_space=pl.ANY)


def _place():
    x, y, c = lax.axis_index("x"), lax.axis_index("y"), lax.axis_index("c")
    return x, y, c, (x, y, 1 - c), [(1 - x, y), (x, 1 - y), (1 - x, 1 - y)]


def _allgather8(name, blk):
    m_per, n = blk.shape

    def body(x_ref, out_ref, send_sems, recv_sems, local_sem):
        x, y, c, sibling, chips = _place()
        me = (x, y, c)

        def rows(px, py, pc):
            return out_ref.at[pl.ds((4 * px + 2 * py + pc) * m_per, m_per), :]

        def copy(k, block, to, src=None):
            return pltpu.make_async_remote_copy(
                src_ref=rows(*block) if src is None else src, dst_ref=rows(*block),
                send_sem=send_sems.at[k], recv_sem=recv_sems.at[k], device_id=to, device_id_type=MESH)

        mine = pltpu.make_async_copy(x_ref, rows(*me), local_sem)
        mine.start()
        first = [copy(0, me, sibling, src=x_ref)]
        first += [copy(1 + j, me, (*chip, c), src=x_ref) for j, chip in enumerate(chips)]
        for cp in first:
            cp.start()
        passed = [copy(4 + j, (*chip, c), sibling) for j, chip in enumerate(chips)]
        for j, chip in enumerate(chips):
            copy(1 + j, (*chip, c), me).wait_recv()
            passed[j].start()
        copy(0, sibling, me).wait_recv()
        for j, chip in enumerate(chips):
            copy(4 + j, (*chip, 1 - c), me).wait_recv()
        for cp in first + passed:
            cp.wait_send()
        mine.wait()

    return pl.pallas_call(
        body, out_shape=jax.ShapeDtypeStruct((N_DEV * m_per, n), blk.dtype),
        in_specs=[pl.BlockSpec(memory_space=pltpu.VMEM)], out_specs=pl.BlockSpec(memory_space=pltpu.VMEM),
        scratch_shapes=[pltpu.SemaphoreType.DMA((7,)), pltpu.SemaphoreType.DMA((7,)), pltpu.SemaphoreType.DMA],
        name=name)(blk)


HBM_SPEC = pl.BlockSpec(memory_space=pltpu.HBM)
SEM_SPEC = pl.BlockSpec(memory_space=pltpu.SEMAPHORE)
SIDE_EFFECT = pltpu.SideEffectType.DATAFLOW_SIDE_EFFECTING


def _dma_sems(n):
    return [pltpu.SemaphoreType.DMA((n,)), pltpu.SemaphoreType.DMA((n,))]


def _half_axis(shape):
    return 1 if (len(shape) > 3 or (shape[1] // 2) % 16 == 0) else len(shape) - 1


def _halves(ref, axis, c):
    r2 = ref.shape[axis] // 2
    lead = (slice(None),) * axis
    return ref.at[lead + (pl.ds(r2 * c, r2),)], ref.at[lead + (pl.ds(r2 * (1 - c), r2),)]


def _gather_copies(srcs, lands, send_sems, recv_sems):
    x, y, c, _, chips = _place()
    sends, lands_here = [], []
    for j, (cx, cy) in enumerate(chips):
        for i, (s, t) in enumerate(zip(srcs, lands)):
            k = 3 * i + j
            ax = _half_axis(s.shape)
            mine = _halves(t.at[:, 2 * x + y], ax, c)[0]
            theirs = _halves(t.at[:, 2 * cx + cy], ax, c)[0]
            sends.append(pltpu.make_async_remote_copy(src_ref=_halves(s, ax, c)[0], dst_ref=mine, send_sem=send_sems.at[k],
                                                      recv_sem=recv_sems.at[k], device_id=(cx, cy, c), device_id_type=MESH))
            lands_here.append(pltpu.make_async_remote_copy(src_ref=theirs, dst_ref=theirs, send_sem=send_sems.at[k],
                                                           recv_sem=recv_sems.at[k], device_id=(cx, cy, c), device_id_type=MESH))
    return sends, lands_here


def _exchange_copies(srcs, lands, send_sems, recv_sems):
    x, y, c, _, chips = _place()
    sends = [pltpu.make_async_remote_copy(src_ref=s.at[2 * cx + cy], dst_ref=t.at[j], send_sem=send_sems.at[3 * i + j],
                                          recv_sem=recv_sems.at[3 * i + j], device_id=(cx, cy, c), device_id_type=MESH)
             for j, (cx, cy) in enumerate(chips) for i, (s, t) in enumerate(zip(srcs, lands))]
    return sends, sends


def _split_start(name, copies, srcs, lands, after):
    ns, nl = len(srcs), len(lands)
    n_copies = 3 * ns

    def body(*refs):
        send_sems, recv_sems = refs[ns + nl + 1], refs[ns + nl + 2]
        for cp in copies(refs[:ns], refs[ns:ns + nl], send_sems, recv_sems)[0]:
            cp.start()
        refs[-1][...] = jnp.zeros_like(refs[-1])

    arrs = list(srcs) + list(lands)
    res = pl.pallas_call(
        body, name=name,
        out_shape=(pltpu.SemaphoreType.DMA((n_copies,)), pltpu.SemaphoreType.DMA((n_copies,)))
        + tuple(pltpu.HBM(a.shape, a.dtype) for a in arrs) + (jax.ShapeDtypeStruct((8, LANES), F32),),
        in_specs=[HBM_SPEC] * (ns + nl) + [ANY],
        out_specs=(SEM_SPEC, SEM_SPEC) + (HBM_SPEC,) * (ns + nl) + (pl.BlockSpec(memory_space=pltpu.VMEM),),
        input_output_aliases={i: 2 + i for i in range(ns + nl)},
        compiler_params=pltpu.CompilerParams(has_side_effects=SIDE_EFFECT),
    )(*[pltpu.with_memory_space_constraint(a, pltpu.HBM) for a in arrs], after)
    return res[0], res[1], list(res[2:2 + ns]), list(res[2 + ns:2 + ns + nl]), res[-1]


def _split_wait(name, copies, send_sems, recv_sems, srcs, lands, after):
    ns, nl = len(srcs), len(lands)

    def body(*refs):
        sends, lands_here = copies(refs[:ns], refs[ns:ns + nl], refs[ns + nl], refs[ns + nl + 1])
        for cp in sends:
            cp.wait_send()
        for cp in lands_here:
            cp.wait_recv()

    arrs = list(srcs) + list(lands)
    res = pl.pallas_call(
        body, name=name, out_shape=tuple(pltpu.HBM(a.shape, a.dtype) for a in arrs),
        in_specs=[HBM_SPEC] * (ns + nl) + [SEM_SPEC, SEM_SPEC, ANY], out_specs=(HBM_SPEC,) * (ns + nl),
        input_output_aliases={i: i for i in range(ns + nl)},
        compiler_params=pltpu.CompilerParams(has_side_effects=SIDE_EFFECT),
    )(*arrs, send_sems, recv_sems, after)
    return list(res[:ns]), list(res[ns:])


def _gather_forward(name, lands):
    n = len(lands)

    def body(*refs):
        ins, outs = refs[:n], refs[n:2 * n]
        send_sems, recv_sems = refs[2 * n:]
        x, y, c, sibling, chips = _place()
        sends, arrivals = [], []
        for j, (cx, cy) in enumerate(chips):
            for i in range(n):
                k = 3 * i + j
                ax = _half_axis(ins[i].shape[:1] + ins[i].shape[2:])
                src = _halves(ins[i].at[:, 2 * cx + cy], ax, c)[0]
                dst, theirs = _halves(outs[i].at[:, 2 * cx + cy], ax, c)
                sends.append(pltpu.make_async_remote_copy(src_ref=src, dst_ref=dst, send_sem=send_sems.at[k], recv_sem=recv_sems.at[k],
                                                          device_id=sibling, device_id_type=MESH))
                arrivals.append(pltpu.make_async_remote_copy(src_ref=theirs, dst_ref=theirs, send_sem=send_sems.at[k],
                                                             recv_sem=recv_sems.at[k], device_id=sibling, device_id_type=MESH))
        for cp in sends:
            cp.start()
        for cp in arrivals:
            cp.wait_recv()
        for cp in sends:
            cp.wait_send()

    return pl.pallas_call(
        body, out_shape=[jax.ShapeDtypeStruct(t.shape, t.dtype) for t in lands], in_specs=[ANY] * n, out_specs=[ANY] * n,
        input_output_aliases={i: i for i in range(n)}, scratch_shapes=_dma_sems(3 * n), name=name)(*lands)


def _swap_halves(name, gs):
    n = len(gs)

    def body(*refs):
        ins, got = refs[:n], refs[n:2 * n]
        send_sems, recv_sems = refs[2 * n:]
        x, y, c, sibling, _ = _place()
        sends = [pltpu.make_async_remote_copy(src_ref=_halves(ins[i], 1 + _half_axis(ins[i].shape[1:]), c)[1], dst_ref=got[i],
                                              send_sem=send_sems.at[i], recv_sem=recv_sems.at[i], device_id=sibling, device_id_type=MESH)
                 for i in range(n)]
        for cp in sends:
            cp.start()
        for cp in sends:
            cp.wait_recv()
        for cp in sends:
            cp.wait_send()

    def half_shape(t):
        ax = 1 + _half_axis(t.shape[1:])
        return t.shape[:ax] + (t.shape[ax] // 2,) + t.shape[ax + 1:]

    return pl.pallas_call(
        body, out_shape=[jax.ShapeDtypeStruct(half_shape(t), t.dtype) for t in gs],
        in_specs=[ANY] * n, out_specs=[ANY] * n, scratch_shapes=_dma_sems(n), name=name)(*gs)


def _share_halves(name, ts, axes):
    n = len(ts)

    def body(*refs):
        ins, outs = refs[:n], refs[n:2 * n]
        send_sems, recv_sems = refs[2 * n:]
        x, y, c, sibling, _ = _place()
        sends, arrivals = [], []
        for i in range(n):
            mine, theirs = _halves(outs[i], axes[i], c)
            sends.append(pltpu.make_async_remote_copy(src_ref=ins[i], dst_ref=mine, send_sem=send_sems.at[i], recv_sem=recv_sems.at[i],
                                                      device_id=sibling, device_id_type=MESH))
            arrivals.append(pltpu.make_async_remote_copy(src_ref=ins[i], dst_ref=theirs, send_sem=send_sems.at[i],
                                                         recv_sem=recv_sems.at[i], device_id=sibling, device_id_type=MESH))
        for cp in sends:
            cp.start()
        for cp in arrivals:
            cp.wait_recv()
        for cp in sends:
            cp.wait_send()

    return pl.pallas_call(
        body, out_shape=[jax.ShapeDtypeStruct(t.shape[:ax] + (2 * t.shape[ax],) + t.shape[ax + 1:], t.dtype) for t, ax in zip(ts, axes)],
        in_specs=[ANY] * n, out_specs=[ANY] * n, scratch_shapes=_dma_sems(n), name=name)(*ts)


PACK_W = 1024
PACK_TB = 512


def _sum_rows(name, parts, out_dtype):
    def f(*vals):
        acc = vals[0]
        for v in vals[1:]:
            acc = acc + v
        return (acc,)

    return _rowwise_fwd(name, f, [(p, PACK_W, 0) for p in parts], [], [(PACK_W, out_dtype)], tb=_tile(parts[0].shape[0], PACK_TB))[0]


def _sum_slots(name, ops, count, out_dtype):
    mat = ops[0][0].shape[1:]

    def body(*refs):
        acc = refs[0][...].astype(F32)
        for r in refs[1:-1]:
            acc = acc + r[...].astype(F32)
        refs[-1][...] = acc.astype(refs[-1].dtype)

    return pl.pallas_call(
        body, grid=(count,), in_specs=[pl.BlockSpec((None,) + mat, lambda i, s=s: (s * count + i, 0, 0)) for _, s in ops],
        out_specs=pl.BlockSpec((None,) + mat, lambda i: (i, 0, 0)), out_shape=jax.ShapeDtypeStruct((count,) + mat, out_dtype),
        compiler_params=_cp("parallel"), name=name)(*[a for a, _ in ops])


def _adamw_update(w_ref, g_ref, m_ref, v_ref, go_ref, d_ref, mo_ref, vo_ref):
    c1 = 1.0 / (1.0 - ADAM_B1 ** ADAM_STEP)
    c2 = 1.0 / (1.0 - ADAM_B2 ** ADAM_STEP)
    gg = g_ref[...]
    mn = ADAM_B1 * m_ref[...] + (1.0 - ADAM_B1) * gg
    vn = ADAM_B2 * v_ref[...] + (1.0 - ADAM_B2) * jnp.square(gg)
    go_ref[...] = gg
    d_ref[...] = -ADAM_LR * ((mn * c1) / (jnp.sqrt(vn * c2) + ADAM_EPS) + ADAM_WD * w_ref[...])
    mo_ref[...] = mn
    vo_ref[...] = vn


def _adamw_layers(name, w, g, m, v, lo, prev=None, dep=None):
    _, r, cw = w.shape

    def body(w_ref, g_ref, m_ref, v_ref, *rest):
        _adamw_update(w_ref, g_ref, m_ref, v_ref, *rest[-4:])

    full = pl.BlockSpec((None, r, LANES), lambda l, i: (lo + l, 0, i))
    extra = ([] if prev is None else list(prev)) + ([] if dep is None else [dep])
    return pl.pallas_call(
        body, grid=(g.shape[0], cw // LANES),
        in_specs=[full, pl.BlockSpec((None, r, LANES), lambda l, i: (l, 0, i)), full, full] + [pl.BlockSpec(memory_space=pl.ANY)] * len(extra),
        out_specs=[full] * 4, out_shape=[jax.ShapeDtypeStruct(w.shape, F32)] * 4,
        input_output_aliases={4 + i: i for i in range(0 if prev is None else 4)},
        compiler_params=_cp("parallel", "parallel"), name=name)(w, g, m, v, *extra)


def _adamw(name, w, g, m, v, row0=0, unit=None, prev=None, dep=None):
    r, cw = w.shape
    tb = unit or r
    while tb * cw > 300_000 and tb % 16 == 0:
        tb //= 2
    off = row0 // tb

    def body(w_ref, g_ref, m_ref, v_ref, *rest):
        _adamw_update(w_ref, g_ref, m_ref, v_ref, *rest[-4:])

    full = pl.BlockSpec((tb, cw), lambda i: (off + i, 0))
    extra = ([] if prev is None else list(prev)) + ([] if dep is None else [dep])
    n_prev = 0 if prev is None else 4
    return pl.pallas_call(
        body, grid=(g.shape[0] // tb,),
        in_specs=[full, pl.BlockSpec((tb, cw), lambda i: (i, 0)), full, full] + [pl.BlockSpec(memory_space=pl.ANY)] * len(extra),
        out_specs=[full] * 4, out_shape=[jax.ShapeDtypeStruct((r, cw), F32)] * 4,
        input_output_aliases={4 + i: i for i in range(n_prev)},
        compiler_params=_cp("parallel"), name=name)(w, g, m, v, *extra)


def _silu_rows(c):
    def body(c_ref, o_ref):
        rows = lax.broadcasted_iota(jnp.int32, o_ref.shape, 0)
        o_ref[...] = jnp.where(rows == 0, jnp.broadcast_to(_silu(c_ref[...]), o_ref.shape), 0.0)

    return pl.pallas_call(body, out_shape=jax.ShapeDtypeStruct((8, c.shape[1]), F32), name="cond_silu")(c)


_KINDS = ("w_in", "w_ssd_out", "w_attn_out", "w_pool_mix", "w_pool_out", "w_out", "w_ff1", "w_ff2")
_SMALL = ("b_ada", "norm1_w", "norm2_w", "conv_b", "dt_bias", "a_log", "d_skip", "ssd_norm_w", "q_norm_w", "k_norm_w",
          "pool_scale")
_ORDER = ("w_ada", "b_ada", "norm1_w", "norm2_w", "w_in", "conv_w", "conv_b", "dt_bias", "a_log", "d_skip", "ssd_norm_w",
          "w_ssd_out", "q_norm_w", "k_norm_w", "w_attn_out", "w_pool_mix", "pool_scale", "w_pool_out", "w_out", "w_ff1", "w_ff2")


def _pack_flat(arrs, rows, dtype):
    flat = jnp.concatenate([a.reshape(-1).astype(dtype) for a in arrs])
    return jnp.pad(flat, (0, rows * PACK_W - flat.shape[0])).reshape(rows, PACK_W)


def _unpack_flat(buf, shapes):
    flat = buf.reshape(-1)
    out, off = [], 0
    for shp in shapes:
        n = int(np.prod(shp))
        out.append(flat[off:off + n].reshape(shp))
        off += n
    return out


def _small_rows(n_elems):
    return -(-n_elems // (8 * PACK_W)) * 8


def kernel(x, c, w_ada, b_ada, norm1_w, norm2_w, w_in, conv_w, conv_b, dt_bias, a_log, d_skip, ssd_norm_w, w_ssd_out, q_norm_w, k_norm_w, w_attn_out, w_pool_mix, pool_scale, w_pool_out, w_out, w_ff1, w_ff2, loss_target, m_w_ada, m_b_ada, m_norm1_w, m_norm2_w, m_w_in, m_conv_w, m_conv_b, m_dt_bias, m_a_log, m_d_skip, m_ssd_norm_w, m_w_ssd_out, m_q_norm_w, m_k_norm_w, m_w_attn_out, m_w_pool_mix, m_pool_scale, m_w_pool_out, m_w_out, m_w_ff1, m_w_ff2, v_w_ada, v_b_ada, v_norm1_w, v_norm2_w, v_w_in, v_conv_w, v_conv_b, v_dt_bias, v_a_log, v_d_skip, v_ssd_norm_w, v_w_ssd_out, v_q_norm_w, v_k_norm_w, v_w_attn_out, v_w_pool_mix, v_pool_scale, v_w_pool_out, v_w_out, v_w_ff1, v_w_ff2):
    w = dict(w_ada=w_ada, b_ada=b_ada, norm1_w=norm1_w, norm2_w=norm2_w, w_in=w_in, conv_w=conv_w, conv_b=conv_b, dt_bias=dt_bias, a_log=a_log, d_skip=d_skip, ssd_norm_w=ssd_norm_w, w_ssd_out=w_ssd_out, q_norm_w=q_norm_w, k_norm_w=k_norm_w, w_attn_out=w_attn_out, w_pool_mix=w_pool_mix, pool_scale=pool_scale, w_pool_out=w_pool_out, w_out=w_out, w_ff1=w_ff1, w_ff2=w_ff2)
    m = dict(w_ada=m_w_ada, b_ada=m_b_ada, norm1_w=m_norm1_w, norm2_w=m_norm2_w, w_in=m_w_in, conv_w=m_conv_w, conv_b=m_conv_b, dt_bias=m_dt_bias, a_log=m_a_log, d_skip=m_d_skip, ssd_norm_w=m_ssd_norm_w, w_ssd_out=m_w_ssd_out, q_norm_w=m_q_norm_w, k_norm_w=m_k_norm_w, w_attn_out=m_w_attn_out, w_pool_mix=m_w_pool_mix, pool_scale=m_pool_scale, w_pool_out=m_w_pool_out, w_out=m_w_out, w_ff1=m_w_ff1, w_ff2=m_w_ff2)
    v = dict(w_ada=v_w_ada, b_ada=v_b_ada, norm1_w=v_norm1_w, norm2_w=v_norm2_w, w_in=v_w_in, conv_w=v_conv_w, conv_b=v_conv_b, dt_bias=v_dt_bias, a_log=v_a_log, d_skip=v_d_skip, ssd_norm_w=v_ssd_norm_w, w_ssd_out=v_w_ssd_out, q_norm_w=v_q_norm_w, k_norm_w=v_k_norm_w, w_attn_out=v_w_attn_out, w_pool_mix=v_w_pool_mix, pool_scale=v_pool_scale, w_pool_out=v_w_pool_out, w_out=v_w_out, w_ff1=v_w_ff1, w_ff2=v_w_ff2)
    chip = 2 * lax.axis_index("x") + lax.axis_index("y")
    dev = 2 * chip + lax.axis_index("c")
    ada_cols = w_ada.shape[2]

    wk = dict({k: w[k] for k in _KINDS}, w_in=jnp.transpose(w_in, (0, 2, 1)))
    mk, vk = {"w_in": jnp.transpose(m_w_in, (0, 2, 1))}, {"w_in": jnp.transpose(v_w_in, (0, 2, 1))}
    rest = _KINDS[1:]

    def start_gather(tag, lo, n, kinds, after):
        shards = [wk[k][lo:lo + n].astype(BF16) for k in kinds]
        lands = [lax.empty((n, N_CHIPS) + s.shape[1:], BF16) for s in shards]
        return _split_start("gather_start_" + tag, _gather_copies, shards, lands, after)

    gather_a1 = start_gather("a1", 0, 1, _KINDS[:1], c)
    c = c + gather_a1[4][0, 0]

    n_conv = conv_w.size // PACK_W
    rows1 = _small_rows((1 + n_conv) * PACK_W)
    blk = jnp.concatenate([_silu_rows(c)[:1], conv_w.reshape(n_conv, PACK_W), jnp.zeros((rows1 - 1 - n_conv, PACK_W), F32)])
    first = _allgather8("gather_cond", blk).reshape(N_DEV, rows1, PACK_W)
    cond_all = first[:, 0]
    conv_all = first[0::2, 1:1 + n_conv].reshape((N_CHIPS,) + conv_w.shape)
    conv_full = jnp.moveaxis(conv_all, 0, 2).reshape(DEPTH, SSD_CONV, CONV_DIM)
    b_cols = lax.dynamic_slice_in_dim(b_ada, chip * ada_cols, ada_cols, axis=1)
    mod_cols = jnp.stack([_matmul("ada_fwd", cond_all, w_ada[l], "nn", precise=True) + b_cols[l][None, :] for l in range(DEPTH)])
    mod_all = _allgather8("gather_mod", mod_cols.reshape(-1, PACK_W)).reshape(N_DEV, DEPTH, N_DEV, ada_cols)
    mine = lax.dynamic_index_in_dim(mod_all[0::2], dev, axis=2, keepdims=False)
    mods = jnp.moveaxis(mine, 0, 1).reshape(DEPTH, 6, D_MODEL)

    core = lax.axis_index("c")
    small_w = dict({k: w[k] for k in _SMALL[1:]}, conv_w=conv_full)

    def finish_gather(tag, handle, kinds, after):
        shards, lands = _split_wait("gather_wait_" + tag, _gather_copies, handle[0], handle[1], handle[2], handle[3], after)
        lands = _gather_forward("gather_forward_" + tag, lands)
        wg = {k: lax.dynamic_update_slice_in_dim(t, s[:, None], chip, axis=1) for k, t, s in zip(kinds, lands, shards)}
        n = shards[0].shape[0]
        for k in ("w_ssd_out", "w_pool_out", "w_out", "w_ff2"):
            if k in wg:
                wg[k] = wg[k].reshape(n, -1, D_MODEL)
        if "w_pool_mix" in wg:
            wg["w_pool_mix"] = jnp.moveaxis(wg["w_pool_mix"], 1, 2).reshape(n, 4, POOL_GW, POOL_GW)
        return wg

    wg_a1 = finish_gather("a1", gather_a1, _KINDS[:1], mods)
    gather_a2 = start_gather("a2", 0, 1, rest, wg_a1["w_in"])
    gather_b = start_gather("b", 1, DEPTH - 1, _KINDS, gather_a2[4])
    mods = mods + gather_b[4][0, 0]

    xc = x[0]
    Ws, saved = [None] * DEPTH, [None] * DEPTH
    Ws[0] = _layer_weights(wg_a1, 0, small_w, 0)
    xc, saved[0] = _layer_fwd(xc, mods[0], Ws[0], 0, late=lambda y: finish_gather("a2", gather_a2, rest, y))
    wg_b = finish_gather("b", gather_b, _KINDS, xc)
    for l in range(1, DEPTH):
        Ws[l] = _layer_weights(wg_b, l - 1, small_w, l)
        xc, saved[l] = _layer_fwd(xc, mods[l], Ws[l], l - 1)
    dx, loss = _loss_and_grad(xc, loss_target[0])
    dmods, grads = [None] * DEPTH, [None] * DEPTH

    def backward(l, lg, dx, mod, bufs):
        dx, dmods[l], g = _layer_bwd(dx, mod, Ws[l], saved[l], lg, bufs)
        grads[l] = _layer_grads_by_chip(g, bufs["w_in"], lg)
        return dx, {k: grads[l][k] for k in _KINDS}

    bufs_b = {k: lax.empty((N_CHIPS, DEPTH - 1) + wk[k].shape[1:], BF16) for k in _KINDS}
    for l in reversed(range(1, DEPTH)):
        dx, bufs_b = backward(l, l - 1, dx, mods[l], bufs_b)

    def flat(t):
        return t.reshape((-1,) + t.shape[-2:])

    def start_exchange(tag, bufs, after):
        gs = [bufs[k] for k in _KINDS]
        got = _swap_halves("grad_swap_" + tag, gs)
        axes = [1 + _half_axis(t.shape[1:]) for t in gs]
        own = [lax.dynamic_slice_in_dim(t, (t.shape[ax] // 2) * core, t.shape[ax] // 2, axis=ax) for t, ax in zip(gs, axes)]
        pairs = [_sum_slots(f"sum_pair_{tag}_{k}", [(flat(a), 0), (flat(b), 0)], flat(a).shape[0], BF16).reshape(a.shape)
                 for k, a, b in zip(_KINDS, own, got)]
        lands = [lax.empty((3,) + p.shape[1:], BF16) for p in pairs]
        return _split_start("exchange_start_" + tag, _exchange_copies, pairs, lands, after)

    def finish_exchange(tag, handle, after):
        pairs, partials = _split_wait("exchange_wait_" + tag, _exchange_copies, handle[0], handle[1], handle[2], handle[3], after)
        mine = [lax.dynamic_index_in_dim(p, chip, axis=0, keepdims=False) for p in pairs]
        totals = [_sum_slots(f"sum_chips_{tag}_{k}", [(flat(a), 0)] + [(flat(p), s) for s in range(3)], flat(a).shape[0], F32).reshape(a.shape)
                  for k, a, p in zip(_KINDS, mine, partials)]
        axes = [_half_axis((1,) + wk[k].shape[1:]) for k in _KINDS]
        return [lax.dynamic_update_slice_in_dim(t, mine_t, mine_t.shape[ax] * core, axis=ax)
                for t, mine_t, ax in zip(_share_halves("grad_share_" + tag, totals, axes), totals, axes)]

    def adamw_group(tag, lo, gs, prev, dep):
        out = {}
        out["w_in"] = _adamw_layers(f"adamw_{tag}_w_in", wk["w_in"], gs[0], mk["w_in"], vk["w_in"], lo,
                                    prev=None if prev is None else prev["w_in"], dep=dep)
        dep = out["w_in"][1]
        for k, gk in zip(rest, gs[1:]):
            shp = w[k].shape
            unit = int(np.prod(shp[1:-1]))
            two_d = lambda t: t.reshape(-1, shp[-1])
            out[k] = _adamw(f"adamw_{tag}_{k}", two_d(w[k]), two_d(gk), two_d(m[k]), two_d(v[k]), row0=lo * unit, unit=unit,
                            prev=None if prev is None else prev[k], dep=dep)
            dep = out[k][1]
        return out

    exchange_b = start_exchange("b", bufs_b, dx)
    bufs_a = {k: lax.empty((N_CHIPS, 1) + wk[k].shape[1:], BF16) for k in _KINDS}
    dx, bufs_a = backward(0, 0, dx, mods[0] + exchange_b[4][0, 0], bufs_a)
    g_b = finish_exchange("b", exchange_b, dx)
    grad_x, dmods = dx, jnp.stack(dmods)
    exchange_a = start_exchange("a", bufs_a, g_b[0])
    adam_b = adamw_group("b", 1, g_b, None, exchange_a[4])

    small = ([dmods] + [jnp.stack([grads[l][k] for l in range(DEPTH)]) for k in _SMALL[1:] + ("conv_w",)] + [loss[:, :1]])
    n_small = sum(int(np.prod(a.shape)) for a in small)
    rows_small = _small_rows(n_small)
    small_all = _allgather8("gather_small", _pack_flat(small, rows_small, F32))
    parts = [small_all[d * rows_small:(d + 1) * rows_small] for d in range(N_DEV)]
    small_sum = _unpack_flat(_sum_rows("sum_small", parts, F32), [a.shape for a in small])
    g_out = {"b_ada": small_sum[0].reshape(DEPTH, 6 * D_MODEL)}
    for k, t in zip(_SMALL[1:], small_sum[1:-2]):
        g_out[k] = t
    g_out["conv_w"] = lax.dynamic_slice_in_dim(small_sum[-2], chip * conv_w.shape[2], conv_w.shape[2], axis=2)
    loss_out = small_sum[-1][0, 0]
    dmod_all = jnp.stack([p[:DEPTH * 6].reshape(DEPTH, 6 * D_MODEL) for p in parts])
    dmod_cols = lax.dynamic_slice_in_dim(dmod_all, chip * ada_cols, ada_cols, axis=2)
    g_out["w_ada"] = jnp.stack([_matmul("ada_dw", cond_all, dmod_cols[:, l], "tn", precise=True) for l in range(DEPTH)])

    deltas, new_m, new_v = {}, {}, {}
    dep = adam_b[_KINDS[-1]][1]
    for k in ("w_ada", "conv_w"):
        shp = w[k].shape
        two_d = (int(np.prod(shp[:-1])), shp[-1])
        res = _adamw("adamw_" + k, *(t.reshape(two_d) for t in (w[k], g_out[k], m[k], v[k])), dep=dep)
        deltas[k], new_m[k], new_v[k] = (t.reshape(shp) for t in res[1:])
        dep = res[1]
    small_shapes = [w[k].shape for k in _SMALL]
    n_sm = sum(int(np.prod(s)) for s in small_shapes)
    res = _adamw("adamw_small", *[_pack_flat([t[k] for k in _SMALL], _small_rows(n_sm), F32) for t in (w, g_out, m, v)], dep=dep)[1:]
    for name_map, buf in zip((deltas, new_m, new_v), res):
        for k, t in zip(_SMALL, _unpack_flat(buf, small_shapes)):
            name_map[k] = t
    g_a = finish_exchange("a", exchange_a, res[0])
    for k, t in adamw_group("a", 0, g_a, adam_b, None).items():
        if k == "w_in":
            g_out[k], deltas[k], new_m[k], new_v[k] = (jnp.transpose(u, (0, 2, 1)) for u in t)
        else:
            g_out[k], deltas[k], new_m[k], new_v[k] = (u.reshape(w[k].shape) for u in t)

    return (loss_out, grad_x[None], *[g_out[k] for k in _ORDER], *[deltas[k] for k in _ORDER],
            *[new_m[k] for k in _ORDER], *[new_v[k] for k in _ORDER])
```
